```python
import math
import jax, jax.numpy as jnp
from jax import lax
import numpy as np

D_MODEL = 1024
BATCH = 8
SEQ = 8192
DEPTH = 1

CTX_LEN = 256
GRID_W = 64
N_HEADS = 8
QK_NOPE = 64
QK_ROPE = 32
QK_DIM = QK_NOPE + QK_ROPE
V_DIM = 64
Q_LORA = 384
KV_LORA = 256
ATTN_WIDTH = N_HEADS * V_DIM
ROPE_THETA = 10000.0
ROPE_AXIS_PAIRS = QK_ROPE // 4
Q_BLOCK = 128
SSM_WIDTH = 512
SSM_GROUP = 16
SSM_GROUPS = SSM_WIDTH // SSM_GROUP
SSM_STATE = 64
FFN_HIDDEN = 2816
CONV_W = 3
N_BRANCH = 2
EPS = 1e-6
IN_SPLITS = (Q_LORA, Q_LORA + KV_LORA, Q_LORA + KV_LORA + QK_ROPE,
             Q_LORA + KV_LORA + QK_ROPE + SSM_WIDTH)
IN_WIDTH = Q_LORA + KV_LORA + QK_ROPE + SSM_WIDTH + N_BRANCH * D_MODEL

kernel_name = "hybrid_mla_s5_convffn_prefix_ctx"


def rmsnorm(x, g):
    xf = x.astype(jnp.float32)
    y = xf * lax.rsqrt(jnp.mean(xf * xf, axis=-1, keepdims=True) + EPS)
    return (y * g.astype(jnp.float32)).astype(x.dtype)


def modulate(h, shift, scale):
    return h * (1.0 + scale) + shift


def axial_rope(rows):
    row = jnp.repeat(jnp.arange(rows), GRID_W)
    col = jnp.tile(jnp.arange(GRID_W), rows)
    freqs = ROPE_THETA ** (-jnp.arange(ROPE_AXIS_PAIRS, dtype=jnp.float32) / ROPE_AXIS_PAIRS)
    ang = jnp.concatenate([row[:, None] * freqs, col[:, None] * freqs], axis=-1)
    return jnp.cos(ang), jnp.sin(ang)


def apply_rope(t, cos, sin):
    nope, rope = t[..., :QK_NOPE], t[..., QK_NOPE:]
    r1, r2 = jnp.split(rope, 2, axis=-1)
    cs = cos[:, None, :].astype(t.dtype)
    sn = sin[:, None, :].astype(t.dtype)
    return jnp.concatenate([nope, r1 * cs - r2 * sn, r1 * sn + r2 * cs], axis=-1)


def mla_q(cq, p, cos, sin):
    b, n = cq.shape[:2]
    q = (rmsnorm(cq, p["q_a_g"]) @ p["w_uq"]).reshape(b, n, N_HEADS, QK_DIM)
    q = rmsnorm(q, p["q_norm_g"])
    return q if cos is None else apply_rope(q, cos, sin)


def mla_kv(ckv, krope, p, cos, sin):
    b, n = ckv.shape[:2]
    kv = (rmsnorm(ckv, p["kv_a_g"]) @ p["w_ukv"]).reshape(b, n, N_HEADS, QK_NOPE + V_DIM)
    k_nope, v = jnp.split(kv, [QK_NOPE], axis=-1)
    k_pe = jnp.broadcast_to(krope[:, :, None, :], (b, n, N_HEADS, QK_ROPE))
    k = rmsnorm(jnp.concatenate([k_nope, k_pe], axis=-1), p["k_norm_g"])
    if cos is not None:
        k = apply_rope(k, cos, sin)
    return k, v


def attend(q, k, v):
    s = jnp.einsum("bqhd,bkhd->bhqk", q, k, preferred_element_type=jnp.float32) * (QK_DIM ** -0.5)
    w = jax.nn.softmax(s, axis=-1).astype(v.dtype)
    return jnp.einsum("bhqk,bkhd->bqhd", w, v)


def latent_attention(q, k_lat, v_lat, k_ctx, v_ctx):
    k = jnp.concatenate([k_lat, k_ctx], axis=1)
    v = jnp.concatenate([v_lat, v_ctx], axis=1)
    b, n = q.shape[:2]
    qb = q.reshape(b, n // Q_BLOCK, Q_BLOCK, N_HEADS, QK_DIM).swapaxes(0, 1)
    o = lax.map(lambda qi: attend(qi, k, v), qb)
    return o.swapaxes(0, 1).reshape(b, n, ATTN_WIDTH)


def _ssm_combine(e_i, e_j):
    a_i, b_i = e_i
    a_j, b_j = e_j
    return a_j * a_i, a_j * b_i + b_j


def s5_states(u, p, init_f, init_b):
    b, n = u.shape[:2]
    ug = u.astype(jnp.float32).reshape(b, n, SSM_GROUPS, SSM_GROUP)
    bmat = lax.complex(p["b_re"].astype(jnp.float32), p["b_im"].astype(jnp.float32))
    out = []
    for sfx, init, reverse in (("f", init_f, False), ("b", init_b, True)):
        lam = lax.complex(p["lam_re_" + sfx].astype(jnp.float32), p["lam_im_" + sfx].astype(jnp.float32))
        dt = jnp.exp(p["log_dt_" + sfx].astype(jnp.float32))[:, None]
        lam_bar = jnp.exp(lam * dt)
        b_bar = ((lam_bar - 1.0) / lam)[..., None] * bmat
        bu = jnp.einsum("gnc,bsgc->bsgn", b_bar, ug)
        if init is not None:
            edge = n - 1 if reverse else 0
            bu = bu.at[:, edge].add(lam_bar * init)
        a = jnp.broadcast_to(lam_bar, bu.shape)
        _, xs = lax.associative_scan(_ssm_combine, (a, bu), reverse=reverse, axis=1)
        out.append(xs)
    return out[0], out[1]


def s5_readout(u, xs_f, xs_b, p):
    b, n = u.shape[:2]
    y = u.astype(jnp.float32) * p["d_skip"].astype(jnp.float32)
    for sfx, xs in (("f", xs_f), ("b", xs_b)):
        cm = lax.complex(p["c_re_" + sfx].astype(jnp.float32), p["c_im_" + sfx].astype(jnp.float32))
        y = y + jnp.einsum("gcn,bsgn->bsgc", cm, xs).real.reshape(b, n, SSM_WIDTH)
    return y.astype(u.dtype)


def ssm_glu(y, w_glu):
    val, gate = jnp.split(jax.nn.gelu(y) @ w_glu, 2, axis=-1)
    return val * jax.nn.sigmoid(gate)


def merge_branches(a, s, gate_logits, w_out):
    ga, gs = jnp.split(gate_logits, N_BRANCH, axis=-1)
    return (jax.nn.sigmoid(ga) * a + jax.nn.sigmoid(gs) * s) @ w_out


def dwconv3(u, w, bias):
    up = jnp.pad(u, ((0, 0), (1, 1), (0, 0)))
    return up[:, :-2] * w[0] + up[:, 1:-1] * w[1] + up[:, 2:] * w[2] + bias


def conv_ffn(h, p):
    u = dwconv3(h @ p["w_up"], p["conv_w"], p["conv_b"])
    val, gate = jnp.split(u, 2, axis=-1)
    return (jax.nn.silu(gate) * val) @ p["w_down"]


def hybrid_layer(x, ctx, c, c_ctx, p, cos, sin, update_ctx):
    mod = jax.nn.silu(c) @ p["w_mod"] + p["b_mod"]
    mod_ctx = jax.nn.silu(c_ctx) @ p["w_mod"] + p["b_mod"]
    sh1, sc1, g1, sh2, sc2, g2 = jnp.split(mod[:, None, :], 6, axis=-1)
    csh1, csc1, cg1, csh2, csc2, cg2 = jnp.split(mod_ctx, 6, axis=-1)

    h = modulate(rmsnorm(x, p["norm1_g"]), sh1, sc1)
    hc = modulate(rmsnorm(ctx, p["norm1_g"]), csh1, csc1)
    cq, ckv, kr, u, gl = jnp.split(h @ p["w_in"], IN_SPLITS, axis=-1)
    ccq, cckv, ckr, cu, cgl = jnp.split(hc @ p["w_in"], IN_SPLITS, axis=-1)

    k_c, v_c = mla_kv(cckv, ckr, p, None, None)
    xs_cf, xs_cb = s5_states(cu, p, None, None)

    q_l = mla_q(cq, p, cos, sin)
    k_l, v_l = mla_kv(ckv, kr, p, cos, sin)
    a_l = latent_attention(q_l, k_l, v_l, k_c, v_c) @ p["w_o_attn"]
    xs_f, xs_b = s5_states(u, p, xs_cf[:, -1], xs_cb[:, 0])
    s_l = ssm_glu(s5_readout(u, xs_f, xs_b, p), p["w_glu"])
    x = x + g1 * merge_branches(a_l, s_l, gl, p["w_out"])

    if update_ctx:
        a_c = attend(mla_q(ccq, p, None, None), k_c, v_c).reshape(ctx.shape[0], ctx.shape[1], ATTN_WIDTH)
        s_c = ssm_glu(s5_readout(cu, xs_cf, xs_cb, p), p["w_glu"])
        ctx = ctx + cg1 * merge_branches(a_c @ p["w_o_attn"], s_c, cgl, p["w_out"])

    x = x + g2 * conv_ffn(modulate(rmsnorm(x, p["norm2_g"]), sh2, sc2), p)
    if update_ctx:
        ctx = ctx + cg2 * conv_ffn(modulate(rmsnorm(ctx, p["norm2_g"]), csh2, csc2), p)
    return x, ctx


def _fwd_setup_inputs(seed: int = 0) -> dict:
    key = jax.random.key(seed)
    ks = iter(jax.random.split(key, 48))
    L, D, G, N, F = DEPTH, D_MODEL, SSM_GROUPS, SSM_STATE, FFN_HIDDEN

    def nrm(shape, scale):
        return jax.random.normal(next(ks), shape, jnp.float32) * scale

    def gain(n):
        return 1.0 + nrm((L, n), 0.02)

    n_idx = jnp.arange(N, dtype=jnp.float32)
    out = {}
    out["x"] = nrm((BATCH, SEQ, D), 1.0)
    out["c"] = nrm((BATCH, D), 1.0)
    out["ctx"] = nrm((BATCH, CTX_LEN, D), 1.0)
    out["c_ctx"] = nrm((D,), 1.0)
    out["w_mod"] = nrm((L, D, 6 * D), 0.5 * D ** -0.5)
    out["b_mod"] = nrm((L, 6 * D), 0.01)
    out["norm1_g"] = gain(D)
    out["norm2_g"] = gain(D)
    out["w_in"] = nrm((L, D, IN_WIDTH), D ** -0.5)
    out["q_a_g"] = gain(Q_LORA)
    out["w_uq"] = nrm((L, Q_LORA, N_HEADS * QK_DIM), Q_LORA ** -0.5)
    out["kv_a_g"] = gain(KV_LORA)
    out["w_ukv"] = nrm((L, KV_LORA, N_HEADS * (QK_NOPE + V_DIM)), KV_LORA ** -0.5)
    out["q_norm_g"] = gain(QK_DIM)
    out["k_norm_g"] = gain(QK_DIM)
    out["w_o_attn"] = nrm((L, ATTN_WIDTH, D), ATTN_WIDTH ** -0.5)
    for sfx in ("f", "b"):
        out["lam_re_" + sfx] = -0.5 + nrm((L, G, N), 0.01)
        out["lam_im_" + sfx] = math.pi * n_idx + nrm((L, G, N), 0.01)
        out["log_dt_" + sfx] = jax.random.uniform(next(ks), (L, G), jnp.float32,
                                                  math.log(1e-3), math.log(1e-1))
        out["c_re_" + sfx] = nrm((L, G, SSM_GROUP, N), (2.0 * N) ** -0.5)
        out["c_im_" + sfx] = nrm((L, G, SSM_GROUP, N), (2.0 * N) ** -0.5)
    out["b_re"] = nrm((L, G, N, SSM_GROUP), (2.0 * SSM_GROUP) ** -0.5)
    out["b_im"] = nrm((L, G, N, SSM_GROUP), (2.0 * SSM_GROUP) ** -0.5)
    out["d_skip"] = nrm((L, SSM_WIDTH), 1.0)
    out["w_glu"] = nrm((L, SSM_WIDTH, 2 * D), SSM_WIDTH ** -0.5)
    out["w_out"] = nrm((L, D, D), D ** -0.5)
    out["w_up"] = nrm((L, D, 2 * F), D ** -0.5)
    out["conv_w"] = nrm((L, CONV_W, 2 * F), CONV_W ** -0.5)
    out["conv_b"] = nrm((L, 2 * F), 0.01)
    out["w_down"] = nrm((L, F, D), F ** -0.5)
    return out


def _fwd_reference(x, c, ctx, c_ctx, w_mod, b_mod, norm1_g, norm2_g, w_in, q_a_g, w_uq, kv_a_g, w_ukv,
              q_norm_g, k_norm_g, w_o_attn,
              lam_re_f, lam_im_f, log_dt_f, c_re_f, c_im_f,
              lam_re_b, lam_im_b, log_dt_b, c_re_b, c_im_b,
              b_re, b_im, d_skip, w_glu, w_out, w_up, conv_w, conv_b, w_down):
    rows = x.shape[1] // GRID_W
    cos, sin = axial_rope(rows)
    for l in range(DEPTH):
        p = dict(w_mod=w_mod[l], b_mod=b_mod[l], norm1_g=norm1_g[l], norm2_g=norm2_g[l], w_in=w_in[l],
                 q_a_g=q_a_g[l], w_uq=w_uq[l], kv_a_g=kv_a_g[l], w_ukv=w_ukv[l],
                 q_norm_g=q_norm_g[l], k_norm_g=k_norm_g[l], w_o_attn=w_o_attn[l],
                 lam_re_f=lam_re_f[l], lam_im_f=lam_im_f[l], log_dt_f=log_dt_f[l],
                 c_re_f=c_re_f[l], c_im_f=c_im_f[l],
                 lam_re_b=lam_re_b[l], lam_im_b=lam_im_b[l], log_dt_b=log_dt_b[l],
                 c_re_b=c_re_b[l], c_im_b=c_im_b[l],
                 b_re=b_re[l], b_im=b_im[l], d_skip=d_skip[l], w_glu=w_glu[l], w_out=w_out[l],
                 w_up=w_up[l], conv_w=conv_w[l], conv_b=conv_b[l], w_down=w_down[l])
        x, ctx = hybrid_layer(x, ctx, c, c_ctx, p, cos, sin, update_ctx=(l < DEPTH - 1))
    return x


import jax as _jax
import jax.numpy as _jnp

TWIN_FORMAT = 'train_step'
FWD_PARAMS = ['x', 'c', 'ctx', 'c_ctx', 'w_mod', 'b_mod', 'norm1_g', 'norm2_g', 'w_in', 'q_a_g', 'w_uq', 'kv_a_g', 'w_ukv', 'q_norm_g', 'k_norm_g', 'w_o_attn', 'lam_re_f', 'lam_im_f', 'log_dt_f', 'c_re_f', 'c_im_f', 'lam_re_b', 'lam_im_b', 'log_dt_b', 'c_re_b', 'c_im_b', 'b_re', 'b_im', 'd_skip', 'w_glu', 'w_out', 'w_up', 'conv_w', 'conv_b', 'w_down']
TWIN_WEIGHTS = ['c_ctx', 'w_mod', 'b_mod', 'norm1_g', 'norm2_g', 'w_in', 'q_a_g', 'w_uq', 'kv_a_g', 'w_ukv', 'q_norm_g', 'k_norm_g', 'w_o_attn', 'lam_re_f', 'lam_im_f', 'log_dt_f', 'c_re_f', 'c_im_f', 'lam_re_b', 'lam_im_b', 'log_dt_b', 'c_re_b', 'c_im_b', 'b_re', 'b_im', 'd_skip', 'w_glu', 'w_out', 'w_up', 'conv_w', 'conv_b', 'w_down']
TWIN_DIFF_INPUT = 'x'
TWIN_INPUTS = ['x', 'c', 'ctx', 'c_ctx', 'w_mod', 'b_mod', 'norm1_g', 'norm2_g', 'w_in', 'q_a_g', 'w_uq', 'kv_a_g', 'w_ukv', 'q_norm_g', 'k_norm_g', 'w_o_attn', 'lam_re_f', 'lam_im_f', 'log_dt_f', 'c_re_f', 'c_im_f', 'lam_re_b', 'lam_im_b', 'log_dt_b', 'c_re_b', 'c_im_b', 'b_re', 'b_im', 'd_skip', 'w_glu', 'w_out', 'w_up', 'conv_w', 'conv_b', 'w_down', 'loss_target', 'm_c_ctx', 'm_w_mod', 'm_b_mod', 'm_norm1_g', 'm_norm2_g', 'm_w_in', 'm_q_a_g', 'm_w_uq', 'm_kv_a_g', 'm_w_ukv', 'm_q_norm_g', 'm_k_norm_g', 'm_w_o_attn', 'm_lam_re_f', 'm_lam_im_f', 'm_log_dt_f', 'm_c_re_f', 'm_c_im_f', 'm_lam_re_b', 'm_lam_im_b', 'm_log_dt_b', 'm_c_re_b', 'm_c_im_b', 'm_b_re', 'm_b_im', 'm_d_skip', 'm_w_glu', 'm_w_out', 'm_w_up', 'm_conv_w', 'm_conv_b', 'm_w_down', 'v_c_ctx', 'v_w_mod', 'v_b_mod', 'v_norm1_g', 'v_norm2_g', 'v_w_in', 'v_q_a_g', 'v_w_uq', 'v_kv_a_g', 'v_w_ukv', 'v_q_norm_g', 'v_k_norm_g', 'v_w_o_attn', 'v_lam_re_f', 'v_lam_im_f', 'v_log_dt_f', 'v_c_re_f', 'v_c_im_f', 'v_lam_re_b', 'v_lam_im_b', 'v_log_dt_b', 'v_c_re_b', 'v_c_im_b', 'v_b_re', 'v_b_im', 'v_d_skip', 'v_w_glu', 'v_w_out', 'v_w_up', 'v_conv_w', 'v_conv_b', 'v_w_down']
TWIN_OUTPUTS = ['loss', 'grad_x', 'grad_c_ctx', 'grad_w_mod', 'grad_b_mod', 'grad_norm1_g', 'grad_norm2_g', 'grad_w_in', 'grad_q_a_g', 'grad_w_uq', 'grad_kv_a_g', 'grad_w_ukv', 'grad_q_norm_g', 'grad_k_norm_g', 'grad_w_o_attn', 'grad_lam_re_f', 'grad_lam_im_f', 'grad_log_dt_f', 'grad_c_re_f', 'grad_c_im_f', 'grad_lam_re_b', 'grad_lam_im_b', 'grad_log_dt_b', 'grad_c_re_b', 'grad_c_im_b', 'grad_b_re', 'grad_b_im', 'grad_d_skip', 'grad_w_glu', 'grad_w_out', 'grad_w_up', 'grad_conv_w', 'grad_conv_b', 'grad_w_down', 'delta_c_ctx', 'delta_w_mod', 'delta_b_mod', 'delta_norm1_g', 'delta_norm2_g', 'delta_w_in', 'delta_q_a_g', 'delta_w_uq', 'delta_kv_a_g', 'delta_w_ukv', 'delta_q_norm_g', 'delta_k_norm_g', 'delta_w_o_attn', 'delta_lam_re_f', 'delta_lam_im_f', 'delta_log_dt_f', 'delta_c_re_f', 'delta_c_im_f', 'delta_lam_re_b', 'delta_lam_im_b', 'delta_log_dt_b', 'delta_c_re_b', 'delta_c_im_b', 'delta_b_re', 'delta_b_im', 'delta_d_skip', 'delta_w_glu', 'delta_w_out', 'delta_w_up', 'delta_conv_w', 'delta_conv_b', 'delta_w_down', 'new_m_c_ctx', 'new_m_w_mod', 'new_m_b_mod', 'new_m_norm1_g', 'new_m_norm2_g', 'new_m_w_in', 'new_m_q_a_g', 'new_m_w_uq', 'new_m_kv_a_g', 'new_m_w_ukv', 'new_m_q_norm_g', 'new_m_k_norm_g', 'new_m_w_o_attn', 'new_m_lam_re_f', 'new_m_lam_im_f', 'new_m_log_dt_f', 'new_m_c_re_f', 'new_m_c_im_f', 'new_m_lam_re_b', 'new_m_lam_im_b', 'new_m_log_dt_b', 'new_m_c_re_b', 'new_m_c_im_b', 'new_m_b_re', 'new_m_b_im', 'new_m_d_skip', 'new_m_w_glu', 'new_m_w_out', 'new_m_w_up', 'new_m_conv_w', 'new_m_conv_b', 'new_m_w_down', 'new_v_c_ctx', 'new_v_w_mod', 'new_v_b_mod', 'new_v_norm1_g', 'new_v_norm2_g', 'new_v_w_in', 'new_v_q_a_g', 'new_v_w_uq', 'new_v_kv_a_g', 'new_v_w_ukv', 'new_v_q_norm_g', 'new_v_k_norm_g', 'new_v_w_o_attn', 'new_v_lam_re_f', 'new_v_lam_im_f', 'new_v_log_dt_f', 'new_v_c_re_f', 'new_v_c_im_f', 'new_v_lam_re_b', 'new_v_lam_im_b', 'new_v_log_dt_b', 'new_v_c_re_b', 'new_v_c_im_b', 'new_v_b_re', 'new_v_b_im', 'new_v_d_skip', 'new_v_w_glu', 'new_v_w_out', 'new_v_w_up', 'new_v_conv_w', 'new_v_conv_b', 'new_v_w_down']
TWIN_LEAF_KINDS = {'loss': 'loss', 'grad_x': 'grad_x', 'grad_c_ctx': 'grad_w', 'grad_w_mod': 'grad_w', 'grad_b_mod': 'grad_w', 'grad_norm1_g': 'grad_w', 'grad_norm2_g': 'grad_w', 'grad_w_in': 'grad_w', 'grad_q_a_g': 'grad_w', 'grad_w_uq': 'grad_w', 'grad_kv_a_g': 'grad_w', 'grad_w_ukv': 'grad_w', 'grad_q_norm_g': 'grad_w', 'grad_k_norm_g': 'grad_w', 'grad_w_o_attn': 'grad_w', 'grad_lam_re_f': 'grad_w', 'grad_lam_im_f': 'grad_w', 'grad_log_dt_f': 'grad_w', 'grad_c_re_f': 'grad_w', 'grad_c_im_f': 'grad_w', 'grad_lam_re_b': 'grad_w', 'grad_lam_im_b': 'grad_w', 'grad_log_dt_b': 'grad_w', 'grad_c_re_b': 'grad_w', 'grad_c_im_b': 'grad_w', 'grad_b_re': 'grad_w', 'grad_b_im': 'grad_w', 'grad_d_skip': 'grad_w', 'grad_w_glu': 'grad_w', 'grad_w_out': 'grad_w', 'grad_w_up': 'grad_w', 'grad_conv_w': 'grad_w', 'grad_conv_b': 'grad_w', 'grad_w_down': 'grad_w', 'delta_c_ctx': 'delta_w', 'delta_w_mod': 'delta_w', 'delta_b_mod': 'delta_w', 'delta_norm1_g': 'delta_w', 'delta_norm2_g': 'delta_w', 'delta_w_in': 'delta_w', 'delta_q_a_g': 'delta_w', 'delta_w_uq': 'delta_w', 'delta_kv_a_g': 'delta_w', 'delta_w_ukv': 'delta_w', 'delta_q_norm_g': 'delta_w', 'delta_k_norm_g': 'delta_w', 'delta_w_o_attn': 'delta_w', 'delta_lam_re_f': 'delta_w', 'delta_lam_im_f': 'delta_w', 'delta_log_dt_f': 'delta_w', 'delta_c_re_f': 'delta_w', 'delta_c_im_f': 'delta_w', 'delta_lam_re_b': 'delta_w', 'delta_lam_im_b': 'delta_w', 'delta_log_dt_b': 'delta_w', 'delta_c_re_b': 'delta_w', 'delta_c_im_b': 'delta_w', 'delta_b_re': 'delta_w', 'delta_b_im': 'delta_w', 'delta_d_skip': 'delta_w', 'delta_w_glu': 'delta_w', 'delta_w_out': 'delta_w', 'delta_w_up': 'delta_w', 'delta_conv_w': 'delta_w', 'delta_conv_b': 'delta_w', 'delta_w_down': 'delta_w', 'new_m_c_ctx': 'new_m', 'new_m_w_mod': 'new_m', 'new_m_b_mod': 'new_m', 'new_m_norm1_g': 'new_m', 'new_m_norm2_g': 'new_m', 'new_m_w_in': 'new_m', 'new_m_q_a_g': 'new_m', 'new_m_w_uq': 'new_m', 'new_m_kv_a_g': 'new_m', 'new_m_w_ukv': 'new_m', 'new_m_q_norm_g': 'new_m', 'new_m_k_norm_g': 'new_m', 'new_m_w_o_attn': 'new_m', 'new_m_lam_re_f': 'new_m', 'new_m_lam_im_f': 'new_m', 'new_m_log_dt_f': 'new_m', 'new_m_c_re_f': 'new_m', 'new_m_c_im_f': 'new_m', 'new_m_lam_re_b': 'new_m', 'new_m_lam_im_b': 'new_m', 'new_m_log_dt_b': 'new_m', 'new_m_c_re_b': 'new_m', 'new_m_c_im_b': 'new_m', 'new_m_b_re': 'new_m', 'new_m_b_im': 'new_m', 'new_m_d_skip': 'new_m', 'new_m_w_glu': 'new_m', 'new_m_w_out': 'new_m', 'new_m_w_up': 'new_m', 'new_m_conv_w': 'new_m', 'new_m_conv_b': 'new_m', 'new_m_w_down': 'new_m', 'new_v_c_ctx': 'new_v', 'new_v_w_mod': 'new_v', 'new_v_b_mod': 'new_v', 'new_v_norm1_g': 'new_v', 'new_v_norm2_g': 'new_v', 'new_v_w_in': 'new_v', 'new_v_q_a_g': 'new_v', 'new_v_w_uq': 'new_v', 'new_v_kv_a_g': 'new_v', 'new_v_w_ukv': 'new_v', 'new_v_q_norm_g': 'new_v', 'new_v_k_norm_g': 'new_v', 'new_v_w_o_attn': 'new_v', 'new_v_lam_re_f': 'new_v', 'new_v_lam_im_f': 'new_v', 'new_v_log_dt_f': 'new_v', 'new_v_c_re_f': 'new_v', 'new_v_c_im_f': 'new_v', 'new_v_lam_re_b': 'new_v', 'new_v_lam_im_b': 'new_v', 'new_v_log_dt_b': 'new_v', 'new_v_c_re_b': 'new_v', 'new_v_c_im_b': 'new_v', 'new_v_b_re': 'new_v', 'new_v_b_im': 'new_v', 'new_v_d_skip': 'new_v', 'new_v_w_glu': 'new_v', 'new_v_w_out': 'new_v', 'new_v_w_up': 'new_v', 'new_v_conv_w': 'new_v', 'new_v_conv_b': 'new_v', 'new_v_w_down': 'new_v'}


def _forward(args):
    return _fwd_reference(*[args[k] for k in FWD_PARAMS])


def _output_shape():
    def fwd():
        inp = _fwd_setup_inputs(0)
        return _fwd_reference(*[inp[k] for k in FWD_PARAMS])
    out = _jax.eval_shape(fwd)
    return out.shape, out.dtype

N_MICROBATCH = 1
ADAM_LR = 0.001
ADAM_B1 = 0.9
ADAM_B2 = 0.999
ADAM_EPS = 1e-08
ADAM_WD = 0.01
ADAM_STEP = 10
PER_EXAMPLE_BATCH_AXIS = {'x': 0, 'c': 0, 'ctx': 0, 'loss_target': 0}
SHARED_INPUTS = []
_WEIGHT_DTYPES = {'c_ctx': _jnp.float32, 'w_mod': _jnp.float32, 'b_mod': _jnp.float32, 'norm1_g': _jnp.float32, 'norm2_g': _jnp.float32, 'w_in': _jnp.float32, 'q_a_g': _jnp.float32, 'w_uq': _jnp.float32, 'kv_a_g': _jnp.float32, 'w_ukv': _jnp.float32, 'q_norm_g': _jnp.float32, 'k_norm_g': _jnp.float32, 'w_o_attn': _jnp.float32, 'lam_re_f': _jnp.float32, 'lam_im_f': _jnp.float32, 'log_dt_f': _jnp.float32, 'c_re_f': _jnp.float32, 'c_im_f': _jnp.float32, 'lam_re_b': _jnp.float32, 'lam_im_b': _jnp.float32, 'log_dt_b': _jnp.float32, 'c_re_b': _jnp.float32, 'c_im_b': _jnp.float32, 'b_re': _jnp.float32, 'b_im': _jnp.float32, 'd_skip': _jnp.float32, 'w_glu': _jnp.float32, 'w_out': _jnp.float32, 'w_up': _jnp.float32, 'conv_w': _jnp.float32, 'conv_b': _jnp.float32, 'w_down': _jnp.float32}
MOMENT_SCALE = {'c_ctx': 2.842966e-02, 'w_mod': 1.253164e+00, 'b_mod': 3.426097e+00, 'norm1_g': 2.267736e-01, 'norm2_g': 6.672185e+00, 'w_in': 7.537437e-02, 'q_a_g': 1.076911e-02, 'w_uq': 7.652699e-03, 'kv_a_g': 4.682219e-01, 'w_ukv': 9.064923e-02, 'q_norm_g': 4.978090e-02, 'k_norm_g': 4.947244e-02, 'w_o_attn': 8.063148e-02, 'lam_re_f': 1.574342e-02, 'lam_im_f': 1.691077e-02, 'log_dt_f': 1.564134e+00, 'c_re_f': 1.304285e-02, 'c_im_f': 1.072775e-02, 'lam_re_b': 1.428856e-02, 'lam_im_b': 9.598527e-03, 'log_dt_b': 9.020406e-01, 'c_re_b': 1.289106e-02, 'c_im_b': 1.120330e-02, 'b_re': 6.848333e-03, 'b_im': 8.165482e-03, 'd_skip': 6.376174e-01, 'w_glu': 8.108971e-02, 'w_out': 1.089344e-01, 'w_up': 1.550854e-01, 'conv_w': 9.876785e-01, 'conv_b': 8.248664e-01, 'w_down': 1.179548e-01}


def _to_microbatches(a, axis):
    t = _jnp.moveaxis(a, axis, 0)
    t = t.reshape((N_MICROBATCH, t.shape[0] // N_MICROBATCH) + t.shape[1:])
    return _jnp.moveaxis(t, 1, axis + 1)


def setup_inputs(seed: int = 0) -> dict:
    inp = _fwd_setup_inputs(seed)
    key = _jax.random.fold_in(_jax.random.key(seed), 7919)
    shape, _ = _output_shape()
    out = dict(inp)
    out["loss_target"] = _jax.random.normal(_jax.random.fold_in(key, 0), shape, _jnp.float32)
    for i, name in enumerate(TWIN_WEIGHTS):
        w = inp[name].astype(_jnp.float32)
        if MOMENT_SCALE is None:
            s = _jnp.sqrt(_jnp.mean(_jnp.square(w)) + 1e-30)
        else:
            s = MOMENT_SCALE[name]
        km, kv = _jax.random.split(_jax.random.fold_in(key, i + 1))
        out[name] = w
        out["m_" + name] = s * _jax.random.normal(km, w.shape, _jnp.float32)
        out["v_" + name] = (s * s) * _jax.random.uniform(kv, w.shape, _jnp.float32, 0.5, 1.5)
    if N_MICROBATCH > 1:
        for name, axis in PER_EXAMPLE_BATCH_AXIS.items():
            out[name] = _to_microbatches(out[name], axis)
    return {'x': out['x'], 'c': out['c'], 'ctx': out['ctx'], 'c_ctx': out['c_ctx'], 'w_mod': out['w_mod'], 'b_mod': out['b_mod'], 'norm1_g': out['norm1_g'], 'norm2_g': out['norm2_g'], 'w_in': out['w_in'], 'q_a_g': out['q_a_g'], 'w_uq': out['w_uq'], 'kv_a_g': out['kv_a_g'], 'w_ukv': out['w_ukv'], 'q_norm_g': out['q_norm_g'], 'k_norm_g': out['k_norm_g'], 'w_o_attn': out['w_o_attn'], 'lam_re_f': out['lam_re_f'], 'lam_im_f': out['lam_im_f'], 'log_dt_f': out['log_dt_f'], 'c_re_f': out['c_re_f'], 'c_im_f': out['c_im_f'], 'lam_re_b': out['lam_re_b'], 'lam_im_b': out['lam_im_b'], 'log_dt_b': out['log_dt_b'], 'c_re_b': out['c_re_b'], 'c_im_b': out['c_im_b'], 'b_re': out['b_re'], 'b_im': out['b_im'], 'd_skip': out['d_skip'], 'w_glu': out['w_glu'], 'w_out': out['w_out'], 'w_up': out['w_up'], 'conv_w': out['conv_w'], 'conv_b': out['conv_b'], 'w_down': out['w_down'], 'loss_target': out['loss_target'], 'm_c_ctx': out['m_c_ctx'], 'm_w_mod': out['m_w_mod'], 'm_b_mod': out['m_b_mod'], 'm_norm1_g': out['m_norm1_g'], 'm_norm2_g': out['m_norm2_g'], 'm_w_in': out['m_w_in'], 'm_q_a_g': out['m_q_a_g'], 'm_w_uq': out['m_w_uq'], 'm_kv_a_g': out['m_kv_a_g'], 'm_w_ukv': out['m_w_ukv'], 'm_q_norm_g': out['m_q_norm_g'], 'm_k_norm_g': out['m_k_norm_g'], 'm_w_o_attn': out['m_w_o_attn'], 'm_lam_re_f': out['m_lam_re_f'], 'm_lam_im_f': out['m_lam_im_f'], 'm_log_dt_f': out['m_log_dt_f'], 'm_c_re_f': out['m_c_re_f'], 'm_c_im_f': out['m_c_im_f'], 'm_lam_re_b': out['m_lam_re_b'], 'm_lam_im_b': out['m_lam_im_b'], 'm_log_dt_b': out['m_log_dt_b'], 'm_c_re_b': out['m_c_re_b'], 'm_c_im_b': out['m_c_im_b'], 'm_b_re': out['m_b_re'], 'm_b_im': out['m_b_im'], 'm_d_skip': out['m_d_skip'], 'm_w_glu': out['m_w_glu'], 'm_w_out': out['m_w_out'], 'm_w_up': out['m_w_up'], 'm_conv_w': out['m_conv_w'], 'm_conv_b': out['m_conv_b'], 'm_w_down': out['m_w_down'], 'v_c_ctx': out['v_c_ctx'], 'v_w_mod': out['v_w_mod'], 'v_b_mod': out['v_b_mod'], 'v_norm1_g': out['v_norm1_g'], 'v_norm2_g': out['v_norm2_g'], 'v_w_in': out['v_w_in'], 'v_q_a_g': out['v_q_a_g'], 'v_w_uq': out['v_w_uq'], 'v_kv_a_g': out['v_kv_a_g'], 'v_w_ukv': out['v_w_ukv'], 'v_q_norm_g': out['v_q_norm_g'], 'v_k_norm_g': out['v_k_norm_g'], 'v_w_o_attn': out['v_w_o_attn'], 'v_lam_re_f': out['v_lam_re_f'], 'v_lam_im_f': out['v_lam_im_f'], 'v_log_dt_f': out['v_log_dt_f'], 'v_c_re_f': out['v_c_re_f'], 'v_c_im_f': out['v_c_im_f'], 'v_lam_re_b': out['v_lam_re_b'], 'v_lam_im_b': out['v_lam_im_b'], 'v_log_dt_b': out['v_log_dt_b'], 'v_c_re_b': out['v_c_re_b'], 'v_c_im_b': out['v_c_im_b'], 'v_b_re': out['v_b_re'], 'v_b_im': out['v_b_im'], 'v_d_skip': out['v_d_skip'], 'v_w_glu': out['v_w_glu'], 'v_w_out': out['v_w_out'], 'v_w_up': out['v_w_up'], 'v_conv_w': out['v_conv_w'], 'v_conv_b': out['v_conv_b'], 'v_w_down': out['v_w_down']}


def _loss(weights, diff, rest, loss_target):
    with _jax.named_scope("forward"):
        args = {**rest, TWIN_DIFF_INPUT: diff, **{k: w.astype(_WEIGHT_DTYPES[k]) for k, w in weights.items()}}
        y = _forward(args)
    with _jax.named_scope("loss_head"):
        err = _jnp.square(y.astype(_jnp.float32) - loss_target)
        return 0.5 * _jnp.sum(_jnp.mean(err, axis=-1)) if err.ndim else 0.5 * err


def _adamw(w, g, m, v):
    m = ADAM_B1 * m + (1.0 - ADAM_B1) * g
    v = ADAM_B2 * v + (1.0 - ADAM_B2) * _jnp.square(g)
    m_hat = m / (1.0 - ADAM_B1 ** ADAM_STEP)
    v_hat = v / (1.0 - ADAM_B2 ** ADAM_STEP)
    delta = -ADAM_LR * (m_hat / (_jnp.sqrt(v_hat) + ADAM_EPS) + ADAM_WD * w)
    return delta, m, v


def reference(x, c, ctx, c_ctx, w_mod, b_mod, norm1_g, norm2_g, w_in, q_a_g, w_uq, kv_a_g, w_ukv, q_norm_g, k_norm_g, w_o_attn, lam_re_f, lam_im_f, log_dt_f, c_re_f, c_im_f, lam_re_b, lam_im_b, log_dt_b, c_re_b, c_im_b, b_re, b_im, d_skip, w_glu, w_out, w_up, conv_w, conv_b, w_down, loss_target, m_c_ctx, m_w_mod, m_b_mod, m_norm1_g, m_norm2_g, m_w_in, m_q_a_g, m_w_uq, m_kv_a_g, m_w_ukv, m_q_norm_g, m_k_norm_g, m_w_o_attn, m_lam_re_f, m_lam_im_f, m_log_dt_f, m_c_re_f, m_c_im_f, m_lam_re_b, m_lam_im_b, m_log_dt_b, m_c_re_b, m_c_im_b, m_b_re, m_b_im, m_d_skip, m_w_glu, m_w_out, m_w_up, m_conv_w, m_conv_b, m_w_down, v_c_ctx, v_w_mod, v_b_mod, v_norm1_g, v_norm2_g, v_w_in, v_q_a_g, v_w_uq, v_kv_a_g, v_w_ukv, v_q_norm_g, v_k_norm_g, v_w_o_attn, v_lam_re_f, v_lam_im_f, v_log_dt_f, v_c_re_f, v_c_im_f, v_lam_re_b, v_lam_im_b, v_log_dt_b, v_c_re_b, v_c_im_b, v_b_re, v_b_im, v_d_skip, v_w_glu, v_w_out, v_w_up, v_conv_w, v_conv_b, v_w_down):
    given = dict(x=x, c=c, ctx=ctx, c_ctx=c_ctx, w_mod=w_mod, b_mod=b_mod, norm1_g=norm1_g, norm2_g=norm2_g, w_in=w_in, q_a_g=q_a_g, w_uq=w_uq, kv_a_g=kv_a_g, w_ukv=w_ukv, q_norm_g=q_norm_g, k_norm_g=k_norm_g, w_o_attn=w_o_attn, lam_re_f=lam_re_f, lam_im_f=lam_im_f, log_dt_f=log_dt_f, c_re_f=c_re_f, c_im_f=c_im_f, lam_re_b=lam_re_b, lam_im_b=lam_im_b, log_dt_b=log_dt_b, c_re_b=c_re_b, c_im_b=c_im_b, b_re=b_re, b_im=b_im, d_skip=d_skip, w_glu=w_glu, w_out=w_out, w_up=w_up, conv_w=conv_w, conv_b=conv_b, w_down=w_down, loss_target=loss_target, m_c_ctx=m_c_ctx, m_w_mod=m_w_mod, m_b_mod=m_b_mod, m_norm1_g=m_norm1_g, m_norm2_g=m_norm2_g, m_w_in=m_w_in, m_q_a_g=m_q_a_g, m_w_uq=m_w_uq, m_kv_a_g=m_kv_a_g, m_w_ukv=m_w_ukv, m_q_norm_g=m_q_norm_g, m_k_norm_g=m_k_norm_g, m_w_o_attn=m_w_o_attn, m_lam_re_f=m_lam_re_f, m_lam_im_f=m_lam_im_f, m_log_dt_f=m_log_dt_f, m_c_re_f=m_c_re_f, m_c_im_f=m_c_im_f, m_lam_re_b=m_lam_re_b, m_lam_im_b=m_lam_im_b, m_log_dt_b=m_log_dt_b, m_c_re_b=m_c_re_b, m_c_im_b=m_c_im_b, m_b_re=m_b_re, m_b_im=m_b_im, m_d_skip=m_d_skip, m_w_glu=m_w_glu, m_w_out=m_w_out, m_w_up=m_w_up, m_conv_w=m_conv_w, m_conv_b=m_conv_b, m_w_down=m_w_down, v_c_ctx=v_c_ctx, v_w_mod=v_w_mod, v_b_mod=v_b_mod, v_norm1_g=v_norm1_g, v_norm2_g=v_norm2_g, v_w_in=v_w_in, v_q_a_g=v_q_a_g, v_w_uq=v_w_uq, v_kv_a_g=v_kv_a_g, v_w_ukv=v_w_ukv, v_q_norm_g=v_q_norm_g, v_k_norm_g=v_k_norm_g, v_w_o_attn=v_w_o_attn, v_lam_re_f=v_lam_re_f, v_lam_im_f=v_lam_im_f, v_log_dt_f=v_log_dt_f, v_c_re_f=v_c_re_f, v_c_im_f=v_c_im_f, v_lam_re_b=v_lam_re_b, v_lam_im_b=v_lam_im_b, v_log_dt_b=v_log_dt_b, v_c_re_b=v_c_re_b, v_c_im_b=v_c_im_b, v_b_re=v_b_re, v_b_im=v_b_im, v_d_skip=v_d_skip, v_w_glu=v_w_glu, v_w_out=v_w_out, v_w_up=v_w_up, v_conv_w=v_conv_w, v_conv_b=v_conv_b, v_w_down=v_w_down)
    weights = {n: given[n] for n in TWIN_WEIGHTS}
    shared = {n: given[n] for n in SHARED_INPUTS}
    per_example = {n: given[n] for n in ['x', 'c', 'ctx']}
    grad_fn = _jax.value_and_grad(_loss, argnums=(0, 1))

    def one_microbatch(ex, loss_target):
        ex = dict(ex)
        diff = ex.pop(TWIN_DIFF_INPUT)
        return grad_fn(weights, diff, {**shared, **ex}, loss_target)

    if N_MICROBATCH == 1:
        loss, (grad_w, grad_x) = one_microbatch(per_example, given["loss_target"])
    else:
        def body(carry, xs):
            loss_sum, grad_sum = carry
            l_k, (gw_k, gx_k) = one_microbatch(xs[0], xs[1])
            with _jax.named_scope("update"):
                return (loss_sum + l_k, _jax.tree.map(_jnp.add, grad_sum, gw_k)), gx_k

        init = (_jnp.zeros((), _jnp.float32), _jax.tree.map(_jnp.zeros_like, weights))
        (loss, grad_w), grad_x = _jax.lax.scan(body, init, (per_example, given["loss_target"]))
    with _jax.named_scope("update"):
        delta_w, new_m, new_v = {}, {}, {}
        for n in TWIN_WEIGHTS:
            delta_w[n], new_m[n], new_v[n] = _adamw(weights[n], grad_w[n], given["m_" + n], given["v_" + n])
    return (loss, grad_x, *[grad_w[n] for n in TWIN_WEIGHTS], *[delta_w[n] for n in TWIN_WEIGHTS],
            *[new_m[n] for n in TWIN_WEIGHTS], *[new_v[n] for n in TWIN_WEIGHTS])
```

```python
import functools
import math

import numpy as np
import jax
import jax.numpy as jnp
from jax import lax
from jax.experimental import pallas as pl
from jax.experimental.pallas import tpu as pltpu

F32 = jnp.float32
MXU_DTYPE = jnp.bfloat16
MESH = pl.DeviceIdType.MESH

EPS = 1e-6
N_HEADS = 8
QK_NOPE = 64
QK_ROPE = 32
QK_DIM = QK_NOPE + QK_ROPE
V_DIM = 64
SLOT = 128
Q_LORA = 384
KV_LORA = 256
GRID_W = 64
ROPE_THETA = 10000.0
SSM_WIDTH = 512
SSM_GROUP = 16
SSM_GROUPS = 32
SSM_STATE = 64
N_STATE = SSM_GROUPS * SSM_STATE
CG_STATES = 512
N_CG = N_STATE // CG_STATES
SCAN_LANES = 256
PACK_W = 1024

ADAM_LR = 0.001
ADAM_B1 = 0.9
ADAM_B2 = 0.999
ADAM_EPS = 1e-08
ADAM_WD = 0.01
ADAM_STEP = 10

VMEM_LIMIT = 56 * 1024 * 1024


def _pick(n, cands):
    for c in cands:
        if c <= n and n % c == 0:
            return c
    return n


def _cparams(sem):
    return pltpu.CompilerParams(dimension_semantics=sem, vmem_limit_bytes=VMEM_LIMIT)


def _sds(shape, dtype):
    return jax.ShapeDtypeStruct(tuple(shape), dtype)


_K_CANDS = (2048, 1536, 1408, 1280, 1152, 1024, 896, 768, 704, 640, 512, 384, 256, 128)
_N_CANDS = (1024, 768, 512, 384, 256, 128)


def _mm(a, b, mode, name, out_dtype=F32, rows=None, a_off=0, b_off=0):
    if mode == "tn":
        t_rows = rows or a.shape[0]
        m, n = a.shape[1], b.shape[1]
        tk = _pick(t_rows, (512, 384, 256, 128, 64, 32, 16))
        tm = m if m <= 1024 else _pick(m, _N_CANDS)
        tn = n if n <= 1024 else _pick(n, _N_CANDS)
        nk = t_rows // tk
        ao, bo = a_off // tk, b_off // tk
        grid = (m // tm, n // tn, nk)
        in_specs = [pl.BlockSpec((tk, tm), lambda i, j, k: (k + ao, i)),
                    pl.BlockSpec((tk, tn), lambda i, j, k: (k + bo, j))]
        dn = (((0,), (0,)), ((), ()))
    else:
        m = rows or a.shape[0]
        kdim = a.shape[1]
        n = b.shape[1] if mode == "nn" else b.shape[0]
        tk = kdim if kdim <= 2048 else _pick(kdim, _K_CANDS)
        tm = _pick(m, (512, 384, 256, 128, 64, 32, 16))
        tn = n if n <= 1024 else _pick(n, _N_CANDS)
        nk = kdim // tk
        ao = a_off // tm
        grid = (m // tm, n // tn, nk)
        if mode == "nn":
            in_specs = [pl.BlockSpec((tm, tk), lambda i, j, k: (i + ao, k)),
                        pl.BlockSpec((tk, tn), lambda i, j, k: (k, j))]
            dn = (((1,), (0,)), ((), ()))
        else:
            in_specs = [pl.BlockSpec((tm, tk), lambda i, j, k: (i + ao, k)),
                        pl.BlockSpec((tn, tk), lambda i, j, k: (j, k))]
            dn = (((1,), (1,)), ((), ()))
    use_scratch = nk > 1 and out_dtype != F32

    def body(a_ref, b_ref, o_ref, *scr):
        r = lax.dot_general(a_ref[...].astype(MXU_DTYPE), b_ref[...].astype(MXU_DTYPE), dn,
                            preferred_element_type=F32)
        if nk == 1:
            o_ref[...] = r.astype(o_ref.dtype)
        else:
            k = pl.program_id(2)
            acc = scr[0] if use_scratch else o_ref

            @pl.when(k == 0)
            def _():
                acc[...] = r

            @pl.when(k > 0)
            def _():
                acc[...] += r

            if use_scratch:
                @pl.when(k == nk - 1)
                def _():
                    o_ref[...] = acc[...].astype(o_ref.dtype)

    return pl.pallas_call(
        body, name=name, grid=grid, in_specs=in_specs,
        out_specs=pl.BlockSpec((tm, tn), lambda i, j, k: (i, j)),
        out_shape=_sds((m, n), out_dtype),
        scratch_shapes=[pltpu.VMEM((tm, tn), F32)] if use_scratch else [],
        compiler_params=_cparams(("parallel", "parallel", "arbitrary")),
    )(a, b)


def _rowwise(body, *, name, nblk, tr, rows=(), halo=(), sels=(), fulls=(), outs=(), accs=(), seg=None):
    n_rows, n_sel, n_full, n_out, n_acc = len(rows), len(sels), len(fulls), len(outs), len(accs)
    halo = tuple(halo)
    maxw = max([r[2] for r in rows] + [o[0] for o in outs] + list(accs))
    sr = _pick(tr, tuple(s for s in (256, 128, 64, 32, 16) if s * maxw <= 131072) or (16,))
    nsub = tr // sr
    total8 = nblk * tr // 8

    def seg_of(i):
        return jnp.where(i >= seg, 1, 0) if seg is not None else 0

    in_specs, operands = [], []
    for arr, cb, w, roff in rows:
        ob = roff // tr
        in_specs.append(pl.BlockSpec((tr, w), lambda i, cb=cb, ob=ob: (i + ob, cb)))
        operands.append(arr)
    for h in halo:
        arr, cb, w, roff = rows[h]
        o8, t8 = roff // 8, tr // 8
        in_specs.append(pl.BlockSpec((8, w), lambda i, cb=cb, o8=o8, t8=t8: (jnp.maximum(i * t8 - 1, 0) + o8, cb)))
        in_specs.append(pl.BlockSpec((8, w), lambda i, cb=cb, o8=o8, t8=t8: (jnp.minimum((i + 1) * t8, total8 - 1) + o8, cb)))
        operands += [arr, arr]
    for arr in sels:
        in_specs.append(pl.BlockSpec((None,) + arr.shape[1:], lambda i: (seg_of(i), 0, 0)))
        operands.append(arr)
    for arr in fulls:
        in_specs.append(pl.BlockSpec(arr.shape, lambda i: (0, 0)))
        operands.append(arr)
    out_specs, out_shape = [], []
    for w, dt in outs:
        out_specs.append(pl.BlockSpec((tr, w), lambda i: (i, 0)))
        out_shape.append(_sds((nblk * tr, w), dt))
    for w in accs:
        if seg is None:
            out_specs.append(pl.BlockSpec((8, w), lambda i: (0, 0)))
            out_shape.append(_sds((8, w), F32))
        else:
            out_specs.append(pl.BlockSpec((None, 8, w), lambda i: (seg_of(i), 0, 0)))
            out_shape.append(_sds((2, 8, w), F32))
    n_halo = 2 * len(halo)

    def kern(*refs):
        row_refs = refs[:n_rows]
        halo_refs = refs[n_rows:n_rows + n_halo]
        sel_refs = refs[n_rows + n_halo:n_rows + n_halo + n_sel]
        full_refs = refs[n_rows + n_halo + n_sel:n_rows + n_halo + n_sel + n_full]
        o0 = n_rows + n_halo + n_sel + n_full
        out_refs = refs[o0:o0 + n_out]
        acc_refs = refs[o0 + n_out:o0 + n_out + n_acc]
        i = pl.program_id(0)
        if n_acc:
            first = (i == 0) if seg is None else ((i == 0) | (i == seg))

            @pl.when(first)
            def _():
                for a_ref in acc_refs:
                    a_ref[...] = jnp.zeros(a_ref.shape, F32)

        def sub(s, carry):
            r0 = pl.multiple_of(s * sr, sr)
            vals = []
            for idx, r in enumerate(row_refs):
                cur = r[pl.ds(r0, sr), :]
                if idx in halo:
                    hp = halo_refs[2 * halo.index(idx)]
                    hn = halo_refs[2 * halo.index(idx) + 1]
                    cur = cur.astype(F32)
                    rid = lax.broadcasted_iota(jnp.int32, cur.shape, 0)
                    lo = r[pl.ds(pl.multiple_of(jnp.maximum(r0 - 8, 0), 8), 8), :].astype(F32)
                    lo = jnp.where(s == 0, hp[...].astype(F32), lo)
                    lo = jnp.where((s == 0) & (i == 0), 0.0, lo)
                    hi = r[pl.ds(pl.multiple_of(jnp.minimum(r0 + sr, tr - 8), 8), 8), :].astype(F32)
                    hi = jnp.where(s == nsub - 1, hn[...].astype(F32), hi)
                    hi = jnp.where((s == nsub - 1) & (i == nblk - 1), 0.0, hi)
                    prev = jnp.where(rid == 0, jnp.broadcast_to(lo[7:8, :], cur.shape), pltpu.roll(cur, 1, 0))
                    nxt = jnp.where(rid == sr - 1, jnp.broadcast_to(hi[0:1, :], cur.shape), pltpu.roll(cur, sr - 1, 0))
                    vals.append((prev, cur, nxt))
                else:
                    vals.append(cur)
            res = body(*vals, *[r[...] for r in sel_refs], *[r[...] for r in full_refs])
            if not isinstance(res, (tuple, list)):
                res = (res,)
            for o_ref, v in zip(out_refs, res[:n_out]):
                o_ref[pl.ds(r0, sr), :] = v.astype(o_ref.dtype)
            for a_ref, v in zip(acc_refs, res[n_out:]):
                a_ref[...] += jnp.sum(v.astype(F32).reshape(sr // 8, 8, v.shape[-1]), axis=0)
            return carry

        lax.fori_loop(0, nsub, sub, 0)

    res = pl.pallas_call(
        kern, name=name, grid=(nblk,), in_specs=in_specs, out_specs=out_specs, out_shape=out_shape,
        compiler_params=_cparams(("arbitrary",)),
    )(*operands)
    return res


def _sigmoid(x):
    return 1.0 / (1.0 + jnp.exp(-x))


def _silu(x):
    return x * _sigmoid(x)


def _dsilu(x):
    s = _sigmoid(x)
    return s * (1.0 + x * (1.0 - s))


_GELU_K = math.sqrt(2.0 / math.pi)


def _gelu(x):
    return 0.5 * x * (1.0 + jnp.tanh(_GELU_K * (x + 0.044715 * x * x * x)))


def _dgelu(x):
    t = jnp.tanh(_GELU_K * (x + 0.044715 * x * x * x))
    return 0.5 * (1.0 + t) + 0.5 * x * (1.0 - t * t) * _GELU_K * (1.0 + 3.0 * 0.044715 * x * x)


def _rms_fwd(x, g, width):
    r = lax.rsqrt(jnp.sum(x * x, axis=-1, keepdims=True) * (1.0 / width) + EPS)
    return x * r * g


def _rms_bwd(x, g, dy, width):
    r = lax.rsqrt(jnp.sum(x * x, axis=-1, keepdims=True) * (1.0 / width) + EPS)
    xn = x * r
    dyg = dy * g
    dx = r * (dyg - xn * (jnp.sum(dyg * xn, axis=-1, keepdims=True) * (1.0 / width)))
    return dx, dy * xn


def _rope_fwd(y, c, sa, sb):
    return y * c + pltpu.roll(y, SLOT - 16, 1) * sa + pltpu.roll(y, 16, 1) * sb


def _rope_bwd(d, c, sa, sb):
    return d * c + pltpu.roll(d * sa, 16, 1) + pltpu.roll(d * sb, SLOT - 16, 1)


def _heads(v):
    return [v[:, h * SLOT:(h + 1) * SLOT] for h in range(N_HEADS)]


def _attn_fwd(q, k, v, nl, scale):
    n = k.shape[0]
    tq = _pick(nl, (512, 256, 128))
    tk = _pick(n, (1408, 1152, 768, 384, 256, 128))
    nk = n // tk
    rep = tk // SLOT

    def body(q_ref, k_ref, v_ref, o_ref, lse_ref, m_sc, l_sc, acc_sc):
        ki = pl.program_id(2)

        @pl.when(ki == 0)
        def _():
            m_sc[...] = jnp.full(m_sc.shape, -jnp.inf, F32)
            l_sc[...] = jnp.zeros(l_sc.shape, F32)
            acc_sc[...] = jnp.zeros(acc_sc.shape, F32)

        s = lax.dot_general(q_ref[...], k_ref[...], (((1,), (1,)), ((), ())), preferred_element_type=F32) * scale
        m_prev = m_sc[...]
        m_new = jnp.maximum(m_prev, jnp.max(s, axis=1, keepdims=True))
        alpha = jnp.exp(m_prev - m_new)
        p = jnp.exp(s - jnp.tile(m_new, (1, rep)))
        l_sc[...] = alpha * l_sc[...] + jnp.sum(p, axis=1, keepdims=True)
        acc_sc[...] = alpha * acc_sc[...] + jnp.dot(p.astype(MXU_DTYPE), v_ref[...], preferred_element_type=F32)
        m_sc[...] = m_new

        @pl.when(ki == nk - 1)
        def _():
            l = l_sc[...]
            o_ref[...] = (acc_sc[...] / l).astype(o_ref.dtype)
            lse_ref[...] = m_sc[...] + jnp.log(l)

    return pl.pallas_call(
        body, name="attn_fwd", grid=(N_HEADS, nl // tq, nk),
        in_specs=[pl.BlockSpec((tq, SLOT), lambda h, i, j: (i, h)),
                  pl.BlockSpec((tk, SLOT), lambda h, i, j: (j, h)),
                  pl.BlockSpec((tk, SLOT), lambda h, i, j: (j, h))],
        out_specs=[pl.BlockSpec((tq, SLOT), lambda h, i, j: (i, h)),
                   pl.BlockSpec((tq, SLOT), lambda h, i, j: (i, h))],
        out_shape=[_sds((nl, N_HEADS * SLOT), MXU_DTYPE), _sds((nl, N_HEADS * SLOT), F32)],
        scratch_shapes=[pltpu.VMEM((tq, SLOT), F32), pltpu.VMEM((tq, SLOT), F32), pltpu.VMEM((tq, SLOT), F32)],
        compiler_params=_cparams(("parallel", "parallel", "arbitrary")),
    )(q, k, v)


def _attn_bwd(q, k, kt, v, do, lse_t, delta_t, nl, scale):
    n = k.shape[0]
    tq = _pick(nl, (512, 256, 128))
    tk = _pick(n, (768, 384, 256, 128))
    nq, nk = nl // tq, n // tk

    def body(q_ref, k_ref, kt_ref, v_ref, do_ref, lse_ref, dl_ref, dq_ref, dk_ref, dv_ref, dk_acc, dv_acc):
        ki, qi = pl.program_id(1), pl.program_id(2)

        @pl.when((ki == 0) & (qi == 0))
        def _():
            dq_ref[...] = jnp.zeros(dq_ref.shape, F32)

        @pl.when(qi == 0)
        def _():
            dk_acc[...] = jnp.zeros(dk_acc.shape, F32)
            dv_acc[...] = jnp.zeros(dv_acc.shape, F32)

        qb, dob = q_ref[...], do_ref[...]
        s_t = lax.dot_general(k_ref[...], qb, (((1,), (1,)), ((), ())), preferred_element_type=F32) * scale
        p_t = jnp.exp(s_t - lse_ref[0:1, :])
        dp_t = lax.dot_general(v_ref[...], dob, (((1,), (1,)), ((), ())), preferred_element_type=F32)
        ds_t = (p_t * (dp_t - dl_ref[0:1, :]) * scale).astype(MXU_DTYPE)
        dv_acc[...] += jnp.dot(p_t.astype(MXU_DTYPE), dob, preferred_element_type=F32)
        dk_acc[...] += jnp.dot(ds_t, qb, preferred_element_type=F32)
        c0 = pl.multiple_of(qi * tq, tq)
        dq_ref[:, pl.ds(c0, tq)] += jnp.dot(kt_ref[...], ds_t, preferred_element_type=F32)

        @pl.when(qi == nq - 1)
        def _():
            dk_ref[...] = dk_acc[...]
            dv_ref[...] = dv_acc[...]

    return pl.pallas_call(
        body, name="attn_bwd", grid=(N_HEADS, nk, nq),
        in_specs=[pl.BlockSpec((tq, SLOT), lambda h, j, i: (i, h)),
                  pl.BlockSpec((tk, SLOT), lambda h, j, i: (j, h)),
                  pl.BlockSpec((SLOT, tk), lambda h, j, i: (h, j)),
                  pl.BlockSpec((tk, SLOT), lambda h, j, i: (j, h)),
                  pl.BlockSpec((tq, SLOT), lambda h, j, i: (i, h)),
                  pl.BlockSpec((None, 8, tq), lambda h, j, i: (h, 0, i)),
                  pl.BlockSpec((None, 8, tq), lambda h, j, i: (h, 0, i))],
        out_specs=[pl.BlockSpec((SLOT, nl), lambda h, j, i: (h, 0)),
                   pl.BlockSpec((tk, SLOT), lambda h, j, i: (j, h)),
                   pl.BlockSpec((tk, SLOT), lambda h, j, i: (j, h))],
        out_shape=[_sds((N_HEADS * SLOT, nl), F32), _sds((n, N_HEADS * SLOT), F32), _sds((n, N_HEADS * SLOT), F32)],
        scratch_shapes=[pltpu.VMEM((tk, SLOT), F32), pltpu.VMEM((tk, SLOT), F32)],
        compiler_params=_cparams(("arbitrary", "arbitrary", "arbitrary")),
    )(q, k, kt, v, do, lse_t, delta_t)


def _scan_consts(c_ref, lg):
    cs = slice(lg * SCAN_LANES, (lg + 1) * SCAN_LANES)
    return [c_ref[8 * kk:8 * kk + 8, cs] for kk in range(8)]


def _tile_scan(br, bi, consts, reverse):
    p1r, p1i, p2r, p2i, p4r, p4i = consts[:6]
    for pr, pi, kk in ((p1r, p1i, 1), (p2r, p2i, 2), (p4r, p4i, 4)):
        sh = (8 - kk) if reverse else kk
        sr_, si_ = pltpu.roll(br, sh, 0), pltpu.roll(bi, sh, 0)
        br, bi = br + pr * sr_ - pi * si_, bi + pr * si_ + pi * sr_
    return br, bi


def _s5_scan(useq, bbig, lamc, t_rows):
    n = useq.shape[1]
    nch = n // t_rows
    ntile = t_rows // 8
    w = SCAN_LANES

    def body(u_ref, b_ref, c_ref, xs_ref, carry):
        j = pl.program_id(2)

        @pl.when(j == 0)
        def _():
            carry[...] = jnp.zeros(carry.shape, F32)

        xs_ref[...] = jnp.dot(u_ref[...].astype(MXU_DTYPE), b_ref[...], preferred_element_type=F32)
        for lg in range(CG_STATES // w):
            re = slice(lg * w, (lg + 1) * w)
            im = slice(CG_STATES + lg * w, CG_STATES + (lg + 1) * w)
            consts = _scan_consts(c_ref, lg)
            qr, qi = consts[6], consts[7]

            def tile(t, st):
                cr, ci = st
                r0 = pl.multiple_of(t * 8, 8)
                br, bi = _tile_scan(xs_ref[pl.ds(r0, 8), re], xs_ref[pl.ds(r0, 8), im], consts, False)
                lr = jnp.broadcast_to(cr[7:8, :], br.shape)
                li = jnp.broadcast_to(ci[7:8, :], bi.shape)
                xr = br + qr * lr - qi * li
                xi = bi + qr * li + qi * lr
                xs_ref[pl.ds(r0, 8), re] = xr
                xs_ref[pl.ds(r0, 8), im] = xi
                return xr, xi

            cr, ci = lax.fori_loop(0, ntile, tile, (carry[:, re], carry[:, im]))
            carry[:, re] = cr
            carry[:, im] = ci

    cw = 2 * CG_STATES
    return pl.pallas_call(
        body, name="s5_scan", grid=(2, N_CG, nch),
        in_specs=[pl.BlockSpec((None, t_rows, SSM_WIDTH), lambda d, g, j: (d, j, 0)),
                  pl.BlockSpec((None, SSM_WIDTH, cw), lambda d, g, j: (d, 0, g)),
                  pl.BlockSpec((None, 64, CG_STATES), lambda d, g, j: (d, 0, g))],
        out_specs=pl.BlockSpec((None, t_rows, cw), lambda d, g, j: (d, j, g)),
        out_shape=_sds((2, n, 2 * N_STATE), F32),
        scratch_shapes=[pltpu.VMEM((8, cw), F32)],
        compiler_params=_cparams(("arbitrary", "arbitrary", "arbitrary")),
    )(useq, bbig, lamc)


def _s5_bwd(dyseq, useq, xs, cbig_t, bbig_t, lamc_adj, t_rows):
    n = useq.shape[1]
    nch = n // t_rows
    ntile = t_rows // 8
    w = SCAN_LANES
    cw = 2 * CG_STATES

    def body(dy_ref, u_ref, xs_ref, halo_ref, ct_ref, bt_ref, c_ref, du_ref, db_ref, dl_ref, gbuf, carry):
        j = pl.program_id(2)
        cj = nch - 1 - j

        @pl.when(j == 0)
        def _():
            carry[...] = jnp.zeros(carry.shape, F32)
            db_ref[...] = jnp.zeros(db_ref.shape, F32)
            dl_ref[...] = jnp.zeros(dl_ref.shape, F32)

        gbuf[...] = jnp.dot(dy_ref[...].astype(MXU_DTYPE), ct_ref[...], preferred_element_type=F32)
        for lg in range(CG_STATES // w):
            re = slice(lg * w, (lg + 1) * w)
            im = slice(CG_STATES + lg * w, CG_STATES + (lg + 1) * w)
            consts = _scan_consts(c_ref, lg)
            qr, qi = consts[6], consts[7]
            hr, hi = halo_ref[:, re], halo_ref[:, im]

            def tile(tt, st):
                gcr, gci, ar, ai = st
                t = ntile - 1 - tt
                r0 = pl.multiple_of(t * 8, 8)
                br, bi = _tile_scan(gbuf[pl.ds(r0, 8), re], gbuf[pl.ds(r0, 8), im], consts, True)
                lr = jnp.broadcast_to(gcr[0:1, :], br.shape)
                li = jnp.broadcast_to(gci[0:1, :], bi.shape)
                gr = br + qr * lr - qi * li
                gi = bi + qr * li + qi * lr
                gbuf[pl.ds(r0, 8), re] = gr
                gbuf[pl.ds(r0, 8), im] = gi
                xr, xi = xs_ref[pl.ds(r0, 8), re], xs_ref[pl.ds(r0, 8), im]
                rl = pl.multiple_of(jnp.maximum(r0 - 8, 0), 8)
                lor = jnp.where(t == 0, hr, xs_ref[pl.ds(rl, 8), re])
                loi = jnp.where(t == 0, hi, xs_ref[pl.ds(rl, 8), im])
                start = (t == 0) & (cj == 0)
                lor = jnp.where(start, 0.0, lor)
                loi = jnp.where(start, 0.0, loi)
                rid = lax.broadcasted_iota(jnp.int32, xr.shape, 0)
                xpr = jnp.where(rid == 0, jnp.broadcast_to(lor[7:8, :], xr.shape), pltpu.roll(xr, 1, 0))
                xpi = jnp.where(rid == 0, jnp.broadcast_to(loi[7:8, :], xi.shape), pltpu.roll(xi, 1, 0))
                ar = ar + gr * xpr + gi * xpi
                ai = ai - gr * xpi + gi * xpr
                return gr, gi, ar, ai

            z = jnp.zeros((8, w), F32)
            gcr, gci, ar, ai = lax.fori_loop(0, ntile, tile, (carry[:, re], carry[:, im], z, z))
            carry[:, re] = gcr
            carry[:, im] = gci
            dl_ref[:, re] += ar
            dl_ref[:, im] += ai
        g = gbuf[...].astype(MXU_DTYPE)
        du_ref[...] = jnp.dot(g, bt_ref[...], preferred_element_type=F32)
        db_ref[...] += lax.dot_general(u_ref[...].astype(MXU_DTYPE), g, (((0,), (0,)), ((), ())),
                                       preferred_element_type=F32)

    t8 = t_rows // 8
    return pl.pallas_call(
        body, name="s5_bwd", grid=(2, N_CG, nch),
        in_specs=[pl.BlockSpec((None, t_rows, SSM_WIDTH), lambda d, g, j: (d, nch - 1 - j, 0)),
                  pl.BlockSpec((None, t_rows, SSM_WIDTH), lambda d, g, j: (d, nch - 1 - j, 0)),
                  pl.BlockSpec((None, t_rows, cw), lambda d, g, j: (d, nch - 1 - j, g)),
                  pl.BlockSpec((None, 8, cw), lambda d, g, j: (d, jnp.maximum((nch - 1 - j) * t8 - 1, 0), g)),
                  pl.BlockSpec((None, SSM_WIDTH, cw), lambda d, g, j: (d, 0, g)),
                  pl.BlockSpec((None, cw, SSM_WIDTH), lambda d, g, j: (d, g, 0)),
                  pl.BlockSpec((None, 64, CG_STATES), lambda d, g, j: (d, 0, g))],
        out_specs=[pl.BlockSpec((None, None, t_rows, SSM_WIDTH), lambda d, g, j: (d, g, nch - 1 - j, 0)),
                   pl.BlockSpec((None, SSM_WIDTH, cw), lambda d, g, j: (d, 0, g)),
                   pl.BlockSpec((None, 8, cw), lambda d, g, j: (d, 0, g))],
        out_shape=[_sds((2, N_CG, n, SSM_WIDTH), F32), _sds((2, SSM_WIDTH, 2 * N_STATE), F32),
                   _sds((2, 8, 2 * N_STATE), F32)],
        scratch_shapes=[pltpu.VMEM((t_rows, cw), F32), pltpu.VMEM((8, cw), F32)],
        compiler_params=_cparams(("arbitrary", "arbitrary", "arbitrary")),
    )(dyseq, useq, xs, xs, cbig_t, bbig_t, lamc_adj)


def _state_cols():
    s = np.arange(N_STATE)
    re = (s // CG_STATES) * 2 * CG_STATES + (s % CG_STATES)
    return re, re + CG_STATES


def _block_diag(p_gnc):
    g = SSM_GROUPS
    z = jnp.zeros((g, SSM_GROUP, g, SSM_STATE), F32)
    idx = jnp.arange(g)
    z = z.at[idx, :, idx, :].set(jnp.transpose(p_gnc, (0, 2, 1)))
    return z.reshape(g * SSM_GROUP, g * SSM_STATE)


def _block_diag_extract(d):
    g = SSM_GROUPS
    idx = jnp.arange(g)
    blk = d.reshape(g, SSM_GROUP, g, SSM_STATE)[idx, :, idx, :]
    return jnp.transpose(blk, (0, 2, 1))


def _s5_disc(lam_re, lam_im, log_dt, b_re, b_im):
    lam = lax.complex(lam_re, lam_im)
    dt = jnp.exp(log_dt)[:, None]
    lam_bar = jnp.exp(lam * dt)
    b_bar = ((lam_bar - 1.0) / lam)[..., None] * lax.complex(b_re, b_im)
    return jnp.real(lam_bar), jnp.imag(lam_bar), jnp.real(b_bar), jnp.imag(b_bar)


def _lam_consts(lr, li, adjoint):
    lr, li = lr.reshape(-1), li.reshape(-1)
    if adjoint:
        li = -li
    pw = [(jnp.ones_like(lr), jnp.zeros_like(lr))]
    for _ in range(8):
        ar, ai = pw[-1]
        pw.append((ar * lr - ai * li, ar * li + ai * lr))
    rows = jnp.arange(8)[:, None]
    out = []
    for kk in (1, 2, 4):
        mask = (rows <= 7 - kk) if adjoint else (rows >= kk)
        out += [jnp.where(mask, pw[kk][0][None, :], 0.0), jnp.where(mask, pw[kk][1][None, :], 0.0)]
    qr = jnp.stack([pw[8 - r][0] if adjoint else pw[r + 1][0] for r in range(8)])
    qi = jnp.stack([pw[8 - r][1] if adjoint else pw[r + 1][1] for r in range(8)])
    return jnp.concatenate(out + [qr, qi], axis=0)


def _dev(t):
    return (t // 4, (t // 2) % 2, t % 2)


def _my_index():
    return 4 * lax.axis_index("x") + 2 * lax.axis_index("y") + lax.axis_index("c")


def _exchange8(g, name, same):
    r, c = g.shape[-2:]

    def body(g_ref, o_ref, ssem, rsem, lsem):
        me = _my_index()

        def src(t):
            return g_ref if same else g_ref.at[t]

        loc = pltpu.make_async_copy(src(me), o_ref.at[me], lsem)
        loc.start()
        sends = []
        for d in range(1, 8):
            t = (me + d) % 8
            cp = pltpu.make_async_remote_copy(src_ref=src(t), dst_ref=o_ref.at[me], send_sem=ssem.at[d - 1],
                                              recv_sem=rsem.at[d - 1], device_id=_dev(t), device_id_type=MESH)
            cp.start()
            sends.append(cp)
        for d in range(1, 8):
            s = (me + 8 - d) % 8
            pltpu.make_async_remote_copy(src_ref=src(s), dst_ref=o_ref.at[s], send_sem=ssem.at[d - 1],
                                         recv_sem=rsem.at[d - 1], device_id=_dev(s), device_id_type=MESH).wait_recv()
        for cp in sends:
            cp.wait_send()
        loc.wait()

    return pl.pallas_call(
        body, name=name, out_shape=_sds((8, r, c), g.dtype),
        in_specs=[pl.BlockSpec(memory_space=pl.ANY)], out_specs=pl.BlockSpec(memory_space=pl.ANY),
        scratch_shapes=[pltpu.SemaphoreType.DMA((7,)), pltpu.SemaphoreType.DMA((7,)), pltpu.SemaphoreType.DMA(())],
    )(g)


def _allgather_chips(w, name):
    r, c = w.shape

    def body(w_ref, o_ref, ssem, rsem, lsem):
        x, y, cc = lax.axis_index("x"), lax.axis_index("y"), lax.axis_index("c")
        k = 2 * x + y
        peers = [(1 - x, y), (x, 1 - y), (1 - x, 1 - y)]
        loc = pltpu.make_async_copy(w_ref, o_ref.at[k], lsem)
        loc.start()
        sends = []
        for j, (px, py) in enumerate(peers):
            cp = pltpu.make_async_remote_copy(src_ref=w_ref, dst_ref=o_ref.at[k], send_sem=ssem.at[j],
                                              recv_sem=rsem.at[j], device_id=(px, py, cc), device_id_type=MESH)
            cp.start()
            sends.append(cp)
        for j, (px, py) in enumerate(peers):
            pltpu.make_async_remote_copy(src_ref=w_ref, dst_ref=o_ref.at[2 * px + py], send_sem=ssem.at[j],
                                         recv_sem=rsem.at[j], device_id=(px, py, cc), device_id_type=MESH).wait_recv()
        for cp in sends:
            cp.wait_send()
        loc.wait()

    return pl.pallas_call(
        body, name=name, out_shape=_sds((4, r, c), w.dtype),
        in_specs=[pl.BlockSpec(memory_space=pl.ANY)], out_specs=pl.BlockSpec(memory_space=pl.ANY),
        scratch_shapes=[pltpu.SemaphoreType.DMA((3,)), pltpu.SemaphoreType.DMA((3,)), pltpu.SemaphoreType.DMA(())],
    )(w)


def _sibling_exchange(h, name):
    r, c = h.shape

    def body(h_ref, o_ref, ssem, rsem, lsem):
        x, y, cc = lax.axis_index("x"), lax.axis_index("y"), lax.axis_index("c")
        loc = pltpu.make_async_copy(h_ref, o_ref.at[cc], lsem)
        loc.start()
        cp = pltpu.make_async_remote_copy(src_ref=h_ref, dst_ref=o_ref.at[cc], send_sem=ssem, recv_sem=rsem,
                                          device_id=(x, y, 1 - cc), device_id_type=MESH)
        cp.start()
        pltpu.make_async_remote_copy(src_ref=h_ref, dst_ref=o_ref.at[1 - cc], send_sem=ssem, recv_sem=rsem,
                                     device_id=(x, y, 1 - cc), device_id_type=MESH).wait_recv()
        cp.wait_send()
        loc.wait()

    return pl.pallas_call(
        body, name=name, out_shape=_sds((2, r, c), h.dtype),
        in_specs=[pl.BlockSpec(memory_space=pl.ANY)], out_specs=pl.BlockSpec(memory_space=pl.ANY),
        scratch_shapes=[pltpu.SemaphoreType.DMA(()), pltpu.SemaphoreType.DMA(()), pltpu.SemaphoreType.DMA(())],
    )(h)


def _sum8(buf, name):
    _, r, c = buf.shape
    tr = _pick(r, (256, 128, 64, 32, 16, 8))
    flat = buf.reshape(8 * r, c)

    def body(*v):
        acc = v[0]
        for t in v[1:]:
            acc = acc + t
        return acc

    return _rowwise(body, name=name, nblk=r // tr, tr=tr, rows=[(flat, 0, c, s * r) for s in range(8)],
                    outs=[(c, F32)])[0]


def _pack(arrs, rows_mult=16):
    flat = jnp.concatenate([a.reshape(-1).astype(F32) for a in arrs])
    nel = flat.shape[0]
    r = -(-nel // PACK_W)
    r = -(-r // rows_mult) * rows_mult
    return jnp.pad(flat, (0, r * PACK_W - nel)).reshape(r, PACK_W)


def _unpack(buf, shapes):
    flat = buf.reshape(-1)
    out, o = [], 0
    for s in shapes:
        nel = int(np.prod(s))
        out.append(flat[o:o + nel].reshape(s))
        o += nel
    return out


def _adamw(g, w, m, v, name):
    r = g.shape[0]
    tr = _pick(r, (256, 128, 64, 32, 16, 8))
    c1 = 1.0 / (1.0 - ADAM_B1 ** ADAM_STEP)
    c2 = 1.0 / (1.0 - ADAM_B2 ** ADAM_STEP)

    def body(gv, wv, mv, vv):
        mn = ADAM_B1 * mv + (1.0 - ADAM_B1) * gv
        vn = ADAM_B2 * vv + (1.0 - ADAM_B2) * (gv * gv)
        delta = -ADAM_LR * ((mn * c1) / (jnp.sqrt(vn * c2) + ADAM_EPS) + ADAM_WD * wv)
        return delta, mn, vn

    return _rowwise(body, name=name, nblk=r // tr, tr=tr, rows=[(a, 0, PACK_W, 0) for a in (g, w, m, v)],
                    outs=[(PACK_W, F32)] * 3)


def kernel(x, c, ctx, c_ctx, w_mod, b_mod, norm1_g, norm2_g, w_in, q_a_g, w_uq, kv_a_g, w_ukv, q_norm_g, k_norm_g, w_o_attn, lam_re_f, lam_im_f, log_dt_f, c_re_f, c_im_f, lam_re_b, lam_im_b, log_dt_b, c_re_b, c_im_b, b_re, b_im, d_skip, w_glu, w_out, w_up, conv_w, conv_b, w_down, loss_target, m_c_ctx, m_w_mod, m_b_mod, m_norm1_g, m_norm2_g, m_w_in, m_q_a_g, m_w_uq, m_kv_a_g, m_w_ukv, m_q_norm_g, m_k_norm_g, m_w_o_attn, m_lam_re_f, m_lam_im_f, m_log_dt_f, m_c_re_f, m_c_im_f, m_lam_re_b, m_lam_im_b, m_log_dt_b, m_c_re_b, m_c_im_b, m_b_re, m_b_im, m_d_skip, m_w_glu, m_w_out, m_w_up, m_conv_w, m_conv_b, m_w_down, v_c_ctx, v_w_mod, v_b_mod, v_norm1_g, v_norm2_g, v_w_in, v_q_a_g, v_w_uq, v_kv_a_g, v_w_ukv, v_q_norm_g, v_k_norm_g, v_w_o_attn, v_lam_re_f, v_lam_im_f, v_log_dt_f, v_c_re_f, v_c_im_f, v_lam_re_b, v_lam_im_b, v_log_dt_b, v_c_re_b, v_c_im_b, v_b_re, v_b_im, v_d_skip, v_w_glu, v_w_out, v_w_up, v_conv_w, v_conv_b, v_w_down):
    weights = dict(c_ctx=c_ctx, w_mod=w_mod, b_mod=b_mod, norm1_g=norm1_g, norm2_g=norm2_g, w_in=w_in, q_a_g=q_a_g, w_uq=w_uq, kv_a_g=kv_a_g, w_ukv=w_ukv, q_norm_g=q_norm_g, k_norm_g=k_norm_g, w_o_attn=w_o_attn, lam_re_f=lam_re_f, lam_im_f=lam_im_f, log_dt_f=log_dt_f, c_re_f=c_re_f, c_im_f=c_im_f, lam_re_b=lam_re_b, lam_im_b=lam_im_b, log_dt_b=log_dt_b, c_re_b=c_re_b, c_im_b=c_im_b, b_re=b_re, b_im=b_im, d_skip=d_skip, w_glu=w_glu, w_out=w_out, w_up=w_up, conv_w=conv_w, conv_b=conv_b, w_down=w_down)
    mom_m = dict(c_ctx=m_c_ctx, w_mod=m_w_mod, b_mod=m_b_mod, norm1_g=m_norm1_g, norm2_g=m_norm2_g, w_in=m_w_in, q_a_g=m_q_a_g, w_uq=m_w_uq, kv_a_g=m_kv_a_g, w_ukv=m_w_ukv, q_norm_g=m_q_norm_g, k_norm_g=m_k_norm_g, w_o_attn=m_w_o_attn, lam_re_f=m_lam_re_f, lam_im_f=m_lam_im_f, log_dt_f=m_log_dt_f, c_re_f=m_c_re_f, c_im_f=m_c_im_f, lam_re_b=m_lam_re_b, lam_im_b=m_lam_im_b, log_dt_b=m_log_dt_b, c_re_b=m_c_re_b, c_im_b=m_c_im_b, b_re=m_b_re, b_im=m_b_im, d_skip=m_d_skip, w_glu=m_w_glu, w_out=m_w_out, w_up=m_w_up, conv_w=m_conv_w, conv_b=m_conv_b, w_down=m_w_down)
    mom_v = dict(c_ctx=v_c_ctx, w_mod=v_w_mod, b_mod=v_b_mod, norm1_g=v_norm1_g, norm2_g=v_norm2_g, w_in=v_w_in, q_a_g=v_q_a_g, w_uq=v_w_uq, kv_a_g=v_kv_a_g, w_ukv=v_w_ukv, q_norm_g=v_q_norm_g, k_norm_g=v_k_norm_g, w_o_attn=v_w_o_attn, lam_re_f=v_lam_re_f, lam_im_f=v_lam_im_f, log_dt_f=v_log_dt_f, c_re_f=v_c_re_f, c_im_f=v_c_im_f, lam_re_b=v_lam_re_b, lam_im_b=v_lam_im_b, log_dt_b=v_log_dt_b, c_re_b=v_c_re_b, c_im_b=v_c_im_b, b_re=v_b_re, b_im=v_b_im, d_skip=v_d_skip, w_glu=v_w_glu, w_out=v_w_out, w_up=v_w_up, conv_w=v_conv_w, conv_b=v_conv_b, w_down=v_w_down)
    names = list(weights)

    nl, d = x.shape[1], x.shape[2]
    nc = ctx.shape[1]
    n = nl + nc
    f2 = conv_b.shape[1]
    fh = f2 // 2
    d6 = b_mod.shape[1]
    mx, my, mc = lax.axis_index("x"), lax.axis_index("y"), lax.axis_index("c")
    chip = 2 * mx + my
    me = 4 * mx + 2 * my + mc
    tr = _pick(math.gcd(nl, nc), (256, 128, 64, 32, 16))
    nlb, nb = nl // tr, n // tr

    big_names = ["w_in", "w_uq", "w_ukv", "w_o_attn", "w_glu", "w_out", "w_up", "w_down"]
    row_sharded = ("w_out", "w_down")
    shard_shapes = [weights[k].shape[1:] for k in big_names]
    wpack = _pack([weights[k][0] for k in big_names]).astype(MXU_DTYPE)
    wall = _allgather_chips(wpack, "gather_weights")
    full = {}
    for k_, pieces in zip(big_names, zip(*[_unpack(wall[j], shard_shapes) for j in range(4)])):
        full[k_] = jnp.concatenate(pieces, axis=0 if k_ in row_sharded else 1)

    cwid = conv_w.shape[2]
    sw = -(-max(d, cwid) // 128) * 128
    small_in = jnp.zeros((8, sw), F32).at[0, :d].set(c[0]).at[1:4, :cwid].set(conv_w[0])
    small_all = _exchange8(small_in, "gather_c", True)
    cs = small_all[:, 0, :d]
    conv_w_full = jnp.concatenate([small_all[2 * j, 1:4, :cwid] for j in range(4)], axis=1)
    cs16 = jnp.zeros((16, d), F32).at[:8].set(cs).at[8].set(c_ctx)

    csh = w_mod.shape[2]
    b_mod_sh = lax.dynamic_slice(b_mod, (0, chip * csh), (1, csh))

    def mod_fwd_body(c_ref, w_ref, b_ref, o_ref):
        a = _silu(c_ref[...]).astype(MXU_DTYPE)
        o_ref[...] = jnp.dot(a, w_ref[...].astype(MXU_DTYPE), preferred_element_type=F32) + b_ref[...]

    mod_sh = pl.pallas_call(mod_fwd_body, name="mod_fwd", out_shape=_sds((16, csh), F32),
                            compiler_params=pltpu.CompilerParams(vmem_limit_bytes=VMEM_LIMIT))(cs16, w_mod[0], b_mod_sh)
    mod_all = _exchange8(mod_sh, "gather_mod", True)
    mod_full = jnp.concatenate([mod_all[2 * j] for j in range(4)], axis=1)
    modv = jnp.stack([lax.dynamic_slice(mod_full, (me, 0), (1, d6)), mod_full[8:9]])

    def mod_parts(m):
        return [m[:, j * d:(j + 1) * d] for j in range(6)]

    u_off, kv_off, kr_off = 2 * d, 2 * d + SSM_WIDTH, 2 * d + SSM_WIDTH + KV_LORA
    q_off = -(-(kr_off + SLOT) // Q_LORA) * Q_LORA
    zw = q_off + Q_LORA
    wi = full["w_in"]
    s0, s1, s2, s3 = Q_LORA, Q_LORA + KV_LORA, Q_LORA + KV_LORA + QK_ROPE, Q_LORA + KV_LORA + QK_ROPE + SSM_WIDTH
    zpad = lambda w_: jnp.zeros((d, w_), MXU_DTYPE)
    win_p = jnp.concatenate([wi[:, s3:], wi[:, s2:s3], wi[:, s0:s1], wi[:, s1:s2], zpad(SLOT - QK_ROPE),
                             zpad(q_off - kr_off - SLOT), wi[:, :s0]], axis=1)
    wuq_p = jnp.pad(full["w_uq"].reshape(Q_LORA, N_HEADS, QK_DIM), ((0, 0), (0, 0), (0, SLOT - QK_DIM))).reshape(Q_LORA, N_HEADS * SLOT)
    wukv3 = full["w_ukv"].reshape(KV_LORA, N_HEADS, QK_NOPE + V_DIM)
    padh = lambda t: jnp.pad(t, ((0, 0), (0, 0), (0, SLOT - t.shape[2]))).reshape(t.shape[0], N_HEADS * SLOT)
    wukv_p = jnp.concatenate([padh(wukv3[:, :, :QK_NOPE]), padh(wukv3[:, :, QK_NOPE:])], axis=1)
    wo_p = jnp.pad(full["w_o_attn"].reshape(N_HEADS, V_DIM, d), ((0, 0), (0, SLOT - V_DIM), (0, 0))).reshape(N_HEADS * SLOT, d)
    wglu, wout, wup, wdown = full["w_glu"], full["w_out"], full["w_up"], full["w_down"]
    hw = N_HEADS * SLOT
    gain_p = lambda g_: jnp.tile(jnp.pad(g_[0], (0, SLOT - QK_DIM)), N_HEADS)[None, :]
    qg_p, kg_p = gain_p(q_norm_g), gain_p(k_norm_g)

    tok = jnp.arange(nl)
    freqs = ROPE_THETA ** (-jnp.arange(QK_ROPE // 4, dtype=F32) / (QK_ROPE // 4))
    ang = jnp.concatenate([(tok // GRID_W)[:, None] * freqs, (tok % GRID_W)[:, None] * freqs], axis=-1)
    cos_t = jnp.concatenate([jnp.cos(ang), jnp.ones((nc, 16), F32)], axis=0)
    sin_t = jnp.concatenate([jnp.sin(ang), jnp.zeros((nc, 16), F32)], axis=0)
    zl = lambda w_: jnp.zeros((n, w_), F32)
    rope_c = jnp.concatenate([jnp.ones((n, QK_NOPE), F32), cos_t, cos_t, zl(SLOT - QK_DIM)], axis=1)
    rope_sa = jnp.concatenate([zl(QK_NOPE), -sin_t, zl(SLOT - QK_NOPE - 16)], axis=1)
    rope_sb = jnp.concatenate([zl(QK_NOPE + 16), sin_t, zl(SLOT - QK_DIM)], axis=1)

    re_cols, im_cols = _state_cols()
    dirs = (("f", lam_re_f, lam_im_f, log_dt_f, c_re_f, c_im_f), ("b", lam_re_b, lam_im_b, log_dt_b, c_re_b, c_im_b))
    bbig, cbig, lamc, lamc_adj, disc_vjps = [], [], [], [], []
    for _, l_re, l_im, l_dt, cr_, ci_ in dirs:
        (lbr, lbi, bbr, bbi), vjp = jax.vjp(_s5_disc, l_re[0], l_im[0], l_dt[0], b_re[0], b_im[0])
        disc_vjps.append(vjp)
        bb = jnp.zeros((SSM_WIDTH, 2 * N_STATE), F32).at[:, re_cols].set(_block_diag(bbr)).at[:, im_cols].set(_block_diag(bbi))
        cc_ = jnp.zeros((SSM_WIDTH, 2 * N_STATE), F32)
        cc_ = cc_.at[:, re_cols].set(_block_diag(jnp.transpose(cr_[0], (0, 2, 1))))
        cc_ = cc_.at[:, im_cols].set(-_block_diag(jnp.transpose(ci_[0], (0, 2, 1))))
        bbig.append(bb)
        cbig.append(cc_)
        lamc.append(_lam_consts(lbr, lbi, False))
        lamc_adj.append(_lam_consts(lbr, lbi, True))
    bbig = jnp.stack(bbig).astype(MXU_DTYPE)
    cbig_t = jnp.stack(cbig).astype(MXU_DTYPE)
    cbig_n = jnp.transpose(cbig_t, (0, 2, 1))
    bbig_t = jnp.transpose(bbig, (0, 2, 1))
    lamc, lamc_adj = jnp.stack(lamc), jnp.stack(lamc_adj)
    t_scan = _pick(n, (384, 256, 128, 64))

    xa = jnp.concatenate([x[0], ctx[0]], axis=0)
    n1g, n2g = norm1_g, norm2_g

    def norm1_body(xv, m, g):
        sh1, sc1 = m[:, :d], m[:, d:2 * d]
        return _rms_fwd(xv, g, d) * (1.0 + sc1) + sh1

    (h1,) = _rowwise(norm1_body, name="norm1_fwd", nblk=nb, tr=tr, rows=[(xa, 0, d, 0)], sels=[modv], fulls=[n1g],
                     outs=[(d, MXU_DTYPE)], seg=nlb)
    z = _mm(h1, win_p, "nn", "in_proj")
    gl_cb, u_cb, kv_cb, kr_cb, q_cb = 0, u_off // SSM_WIDTH, kv_off // KV_LORA, kr_off // SLOT, q_off // Q_LORA

    (cqn,) = _rowwise(lambda v, g: _rms_fwd(v, g, Q_LORA), name="qa_norm_fwd", nblk=nlb, tr=tr,
                      rows=[(z, q_cb, Q_LORA, 0)], fulls=[q_a_g], outs=[(Q_LORA, MXU_DTYPE)])
    qh = _mm(cqn, wuq_p, "nn", "q_up")

    def qhead_body(qv, cv, sav, sbv, g):
        return jnp.concatenate([_rope_fwd(_rms_fwd(t, g[:, :SLOT], QK_DIM), cv, sav, sbv) for t in _heads(qv)], axis=1)

    rope_rows = lambda: [(rope_c, 0, SLOT, 0), (rope_sa, 0, SLOT, 0), (rope_sb, 0, SLOT, 0)]
    (q_p,) = _rowwise(qhead_body, name="q_head_fwd", nblk=nlb, tr=tr, rows=[(qh, 0, hw, 0)] + rope_rows(),
                      fulls=[qg_p], outs=[(hw, MXU_DTYPE)])

    (ckvn,) = _rowwise(lambda v, g: _rms_fwd(v, g, KV_LORA), name="kva_norm_fwd", nblk=nb, tr=tr,
                       rows=[(z, kv_cb, KV_LORA, 0)], fulls=[kv_a_g], outs=[(KV_LORA, MXU_DTYPE)])
    kvpre = _mm(ckvn, wukv_p, "nn", "kv_up")

    def khead_body(kv_, vv_, krv, cv, sav, sbv, g):
        kpe = pltpu.roll(krv, QK_NOPE, 1)
        ks = [_rope_fwd(_rms_fwd(t + kpe, g[:, :SLOT], QK_DIM), cv, sav, sbv) for t in _heads(kv_)]
        return jnp.concatenate(ks, axis=1), vv_

    k_p, v_p = _rowwise(khead_body, name="k_head_fwd", nblk=nb, tr=tr,
                        rows=[(kvpre, 0, hw, 0), (kvpre, 1, hw, 0), (z, kr_cb, SLOT, 0)] + rope_rows(),
                        fulls=[kg_p], outs=[(hw, MXU_DTYPE), (hw, MXU_DTYPE)])

    scale = QK_DIM ** -0.5
    o_p, lse = _attn_fwd(q_p, k_p, v_p, nl, scale)
    a_l = _mm(o_p, wo_p, "nn", "attn_out")

    u_nat = z[:, u_off:u_off + SSM_WIDTH]
    u_lat, u_ctx = u_nat[:nl], u_nat[nl:]
    useq = jnp.stack([jnp.concatenate([u_ctx, u_lat], axis=0), jnp.flip(u_nat, axis=0)])
    xs = _s5_scan(useq, bbig, lamc, t_scan)
    xs2 = xs.reshape(2 * n, 2 * N_STATE)
    yf = _mm(xs2, cbig_n[0], "nn", "s5_read_f", rows=n, a_off=0)
    yb = _mm(xs2, cbig_n[1], "nn", "s5_read_b", rows=n, a_off=n)
    yf_lat, yb_lat = yf[nc:], jnp.flip(yb[nc:], axis=0)

    def ssm_out_body(uv, a, b, dsk):
        ys = uv * dsk + a + b
        return ys, _gelu(ys)

    ys, ge = _rowwise(ssm_out_body, name="s5_out_fwd", nblk=nlb, tr=tr,
                      rows=[(u_lat, 0, SSM_WIDTH, 0), (yf_lat, 0, SSM_WIDTH, 0), (yb_lat, 0, SSM_WIDTH, 0)],
                      fulls=[d_skip], outs=[(SSM_WIDTH, F32), (SSM_WIDTH, MXU_DTYPE)])
    glu_out = _mm(ge, wglu, "nn", "glu_proj")

    def merge_body(ga, gs, av, val, gate):
        return _sigmoid(ga) * av + _sigmoid(gs) * (val * _sigmoid(gate))

    merge_rows = lambda: [(z, 0, d, 0), (z, 1, d, 0), (a_l, 0, d, 0), (glu_out, 0, d, 0), (glu_out, 1, d, 0)]
    (merged,) = _rowwise(merge_body, name="merge_fwd", nblk=nlb, tr=tr, rows=merge_rows(), outs=[(d, MXU_DTYPE)])
    mo = _mm(merged, wout, "nn", "out_proj")
    mod_x = modv[0]

    def norm2_body(xv, mov, m, g):
        g1, sh2, sc2 = m[:, 2 * d:3 * d], m[:, 3 * d:4 * d], m[:, 4 * d:5 * d]
        x1v = xv + g1 * mov
        return x1v, _rms_fwd(x1v, g, d) * (1.0 + sc2) + sh2

    x1, h2 = _rowwise(norm2_body, name="norm2_fwd", nblk=nlb, tr=tr, rows=[(xa, 0, d, 0), (mo, 0, d, 0)],
                      fulls=[mod_x, n2g], outs=[(d, F32), (d, MXU_DTYPE)])
    up = _mm(h2, wup, "nn", "ffn_up")
    cw8 = jnp.zeros((8, f2), F32).at[:3].set(conv_w_full)

    def conv3(t3, w8, off):
        p_, c_, n_ = t3
        return p_ * w8[0:1, off:off + fh] + c_ * w8[1:2, off:off + fh] + n_ * w8[2:3, off:off + fh]

    def conv_fwd_body(val3, gate3, w8, bias):
        val2 = conv3(val3, w8, 0) + bias[:, :fh]
        gate2 = conv3(gate3, w8, fh) + bias[:, fh:]
        return _silu(gate2) * val2

    (act,) = _rowwise(conv_fwd_body, name="conv_fwd", nblk=nlb, tr=tr, rows=[(up, 0, fh, 0), (up, 1, fh, 0)],
                      halo=(0, 1), fulls=[cw8, conv_b], outs=[(fh, MXU_DTYPE)])
    dn = _mm(act, wdown, "nn", "ffn_down")
    tgt = loss_target[0]

    def loss_body(x1v, dnv, tv, m):
        g2 = m[:, 5 * d:6 * d]
        e = x1v + g2 * dnv - tv
        dx2v = e * (1.0 / d)
        return dx2v, dx2v * g2, e * e, dx2v * dnv

    dx2, ddn, loss_acc, dg2_acc = _rowwise(loss_body, name="loss", nblk=nlb, tr=tr,
                                           rows=[(x1, 0, d, 0), (dn, 0, d, 0), (tgt, 0, d, 0)], fulls=[mod_x],
                                           outs=[(d, F32), (d, MXU_DTYPE)], accs=[d, d])
    loss = lax.psum(0.5 / d * jnp.sum(loss_acc), ("x", "y", "c"))

    g_big = {}
    dact = _mm(ddn, wdown, "nt", "ffn_down_dx")
    g_big["w_down"] = _mm(act, ddn, "tn", "ffn_down_dw")

    def conv_bwd_body(val3, gate3, da, w8, bias):
        val2 = conv3(val3, w8, 0) + bias[:, :fh]
        gate2 = conv3(gate3, w8, fh) + bias[:, fh:]
        dval2 = da * _silu(gate2)
        dgate2 = da * val2 * _dsilu(gate2)
        du2 = jnp.concatenate([dval2, dgate2], axis=1)
        taps = [jnp.concatenate([dval2 * val3[j], dgate2 * gate3[j]], axis=1) for j in range(3)]
        return du2, du2, taps[0], taps[1], taps[2]

    du2, dcb_acc, dcw0, dcw1, dcw2 = _rowwise(conv_bwd_body, name="conv_bwd", nblk=nlb, tr=tr,
                                              rows=[(up, 0, fh, 0), (up, 1, fh, 0), (dact, 0, fh, 0)], halo=(0, 1),
                                              fulls=[cw8, conv_b], outs=[(f2, F32)], accs=[f2, f2, f2, f2])

    def conv_t_body(dval3, dgate3, w8):
        rev = lambda t3: (t3[2], t3[1], t3[0])
        return jnp.concatenate([conv3(rev(dval3), w8, 0), conv3(rev(dgate3), w8, fh)], axis=1)

    (dup,) = _rowwise(conv_t_body, name="conv_bwd_dx", nblk=nlb, tr=tr, rows=[(du2, 0, fh, 0), (du2, 1, fh, 0)],
                      halo=(0, 1), fulls=[cw8], outs=[(f2, MXU_DTYPE)])
    dh2 = _mm(dup, wup, "nt", "ffn_up_dx")
    g_big["w_up"] = _mm(h2, dup, "tn", "ffn_up_dw")

    def norm2_bwd_body(x1v, dh, dx2v, mov, m, g):
        g1, sc2 = m[:, 2 * d:3 * d], m[:, 4 * d:5 * d]
        y = _rms_fwd(x1v, g, d)
        dxn, dgc = _rms_bwd(x1v, g, dh * (1.0 + sc2), d)
        dx1v = dx2v + dxn
        return dx1v, dx1v * g1, dgc, dh, dh * y, dx1v * mov

    dx1, dmo, dn2g_acc, dsh2_acc, dsc2_acc, dg1_acc = _rowwise(
        norm2_bwd_body, name="norm2_bwd", nblk=nlb, tr=tr,
        rows=[(x1, 0, d, 0), (dh2, 0, d, 0), (dx2, 0, d, 0), (mo, 0, d, 0)], fulls=[mod_x, n2g],
        outs=[(d, F32), (d, MXU_DTYPE)], accs=[d, d, d, d])
    dmerged = _mm(dmo, wout, "nt", "out_proj_dx")
    g_big["w_out"] = _mm(merged, dmo, "tn", "out_proj_dw")

    def merge_bwd_body(ga, gs, av, val, gate, dm):
        sa_, ss_, sg_ = _sigmoid(ga), _sigmoid(gs), _sigmoid(gate)
        s_l = val * sg_
        ds_l = dm * ss_
        dga = dm * av * sa_ * (1.0 - sa_)
        dgs = dm * s_l * ss_ * (1.0 - ss_)
        dval = ds_l * sg_
        dgate = ds_l * val * sg_ * (1.0 - sg_)
        return dm * sa_, jnp.concatenate([dval, dgate], axis=1), jnp.concatenate([dga, dgs], axis=1)

    da_l, dglu, dgl = _rowwise(merge_bwd_body, name="merge_bwd", nblk=nlb, tr=tr,
                               rows=merge_rows() + [(dmerged, 0, d, 0)],
                               outs=[(d, MXU_DTYPE), (2 * d, MXU_DTYPE), (2 * d, MXU_DTYPE)])
    dge = _mm(dglu, wglu, "nt", "glu_proj_dx")
    g_big["w_glu"] = _mm(ge, dglu, "tn", "glu_proj_dw")

    def ssm_out_bwd_body(ysv, dgev, uv, dsk):
        dys_ = dgev * _dgelu(ysv)
        return dys_, dys_ * dsk, dys_ * uv

    dys, du_skip, ddskip_acc = _rowwise(ssm_out_bwd_body, name="s5_out_bwd", nblk=nlb, tr=tr,
                                        rows=[(ys, 0, SSM_WIDTH, 0), (dge, 0, SSM_WIDTH, 0), (u_lat, 0, SSM_WIDTH, 0)],
                                        fulls=[d_skip], outs=[(SSM_WIDTH, F32), (SSM_WIDTH, F32)], accs=[SSM_WIDTH])
    zc = jnp.zeros((nc, SSM_WIDTH), F32)
    dyseq = jnp.stack([jnp.concatenate([zc, dys], axis=0), jnp.flip(jnp.concatenate([dys, zc], axis=0), axis=0)])
    du_cg, dbbig, dlam = _s5_bwd(dyseq, useq, xs, cbig_t, bbig_t, lamc_adj, t_scan)
    dyseq2 = dyseq.reshape(2 * n, SSM_WIDTH)
    dcbig = [_mm(xs2, dyseq2, "tn", "s5_read_dw_" + dirs[j][0], rows=n, a_off=j * n, b_off=j * n) for j in range(2)]
    du_f = du_cg[0, 0] + du_cg[0, 1] + du_cg[0, 2] + du_cg[0, 3]
    du_b = du_cg[1, 0] + du_cg[1, 1] + du_cg[1, 2] + du_cg[1, 3]
    du_nat = jnp.concatenate([du_f[nc:], du_f[:nc]], axis=0) + jnp.flip(du_b, axis=0)
    du_nat = du_nat + jnp.concatenate([du_skip, zc], axis=0)

    do_f = _mm(da_l, wo_p, "nt", "attn_out_dx")
    g_wo_p = _mm(o_p, da_l, "tn", "attn_out_dw")

    def delta_body(dov, ov):
        prod = dov * ov.astype(F32)
        dl = [jnp.broadcast_to(jnp.sum(t, axis=-1, keepdims=True), t.shape) for t in _heads(prod)]
        return dov, jnp.concatenate(dl, axis=1)

    do_b, delta = _rowwise(delta_body, name="attn_delta", nblk=nlb, tr=tr, rows=[(do_f, 0, hw, 0), (o_p, 0, hw, 0)],
                           outs=[(hw, MXU_DTYPE), (hw, F32)])
    to_rows = lambda t: jnp.broadcast_to(jnp.transpose(t[:, ::SLOT])[:, None, :], (N_HEADS, 8, nl))
    dq_t, dk_p, dv_p = _attn_bwd(q_p, k_p, jnp.transpose(k_p), v_p, do_b, to_rows(lse), to_rows(delta), nl, scale)
    dq_p = jnp.transpose(dq_t)

    def qhead_bwd_body(qv, dqv, cv, sav, sbv, g):
        dxs, dgs = [], []
        for t, dt_ in zip(_heads(qv), _heads(dqv)):
            dx_, dg_ = _rms_bwd(t, g[:, :SLOT], _rope_bwd(dt_, cv, sav, sbv), QK_DIM)
            dxs.append(dx_)
            dgs.append(dg_)
        return jnp.concatenate(dxs, axis=1), jnp.concatenate(dgs, axis=1)

    dqh, dqg_acc = _rowwise(qhead_bwd_body, name="q_head_bwd", nblk=nlb, tr=tr,
                            rows=[(qh, 0, hw, 0), (dq_p, 0, hw, 0)] + rope_rows(), fulls=[qg_p],
                            outs=[(hw, MXU_DTYPE)], accs=[hw])
    dcqn = _mm(dqh, wuq_p, "nt", "q_up_dx")
    g_wuq_p = _mm(cqn, dqh, "tn", "q_up_dw")
    dcq, dqag_acc = _rowwise(lambda v, dy, g: _rms_bwd(v, g, dy, Q_LORA), name="qa_norm_bwd", nblk=nlb, tr=tr,
                             rows=[(z, q_cb, Q_LORA, 0), (dcqn, 0, Q_LORA, 0)], fulls=[q_a_g],
                             outs=[(Q_LORA, MXU_DTYPE)], accs=[Q_LORA])

    def khead_bwd_body(kv_, krv, dkv_, dvv_, cv, sav, sbv, g):
        kpe = pltpu.roll(krv, QK_NOPE, 1)
        lane = lax.broadcasted_iota(jnp.int32, krv.shape, 1)
        dxs, dgs, dkr_ = [], [], jnp.zeros(krv.shape, F32)
        for t, dt_ in zip(_heads(kv_), _heads(dkv_)):
            dx_, dg_ = _rms_bwd(t + kpe, g[:, :SLOT], _rope_bwd(dt_, cv, sav, sbv), QK_DIM)
            dxs.append(jnp.where(lane < QK_NOPE, dx_, 0.0))
            dgs.append(dg_)
            dkr_ = dkr_ + dx_
        dkr_ = jnp.where(lane < QK_ROPE, pltpu.roll(dkr_, SLOT - QK_NOPE, 1), 0.0)
        return jnp.concatenate(dxs + [dvv_], axis=1), dkr_, jnp.concatenate(dgs, axis=1)

    dkvpre, dkr, dkg_acc = _rowwise(khead_bwd_body, name="k_head_bwd", nblk=nb, tr=tr,
                                    rows=[(kvpre, 0, hw, 0), (z, kr_cb, SLOT, 0), (dk_p, 0, hw, 0), (dv_p, 0, hw, 0)] + rope_rows(),
                                    fulls=[kg_p], outs=[(2 * hw, MXU_DTYPE), (SLOT, MXU_DTYPE)], accs=[hw])
    dckvn = _mm(dkvpre, wukv_p, "nt", "kv_up_dx")
    g_wukv_p = _mm(ckvn, dkvpre, "tn", "kv_up_dw")
    dckv, dkvag_acc = _rowwise(lambda v, dy, g: _rms_bwd(v, g, dy, KV_LORA), name="kva_norm_bwd", nblk=nb, tr=tr,
                               rows=[(z, kv_cb, KV_LORA, 0), (dckvn, 0, KV_LORA, 0)], fulls=[kv_a_g],
                               outs=[(KV_LORA, MXU_DTYPE)], accs=[KV_LORA])

    padc = lambda t: jnp.concatenate([t, jnp.zeros((nc, t.shape[1]), t.dtype)], axis=0)
    dz = jnp.concatenate([padc(dgl), du_nat.astype(MXU_DTYPE), dckv, dkr,
                          jnp.zeros((n, q_off - kr_off - SLOT), MXU_DTYPE), padc(dcq)], axis=1)
    dh1 = _mm(dz, win_p, "nt", "in_proj_dx")
    g_win_p = _mm(h1, dz, "tn", "in_proj_dw")

    def norm1_bwd_body(xv, dh, m, g):
        sc1 = m[:, d:2 * d]
        y = _rms_fwd(xv, g, d)
        dxn, dgc = _rms_bwd(xv, g, dh * (1.0 + sc1), d)
        return dxn, dgc, dh, dh * y

    dxa, dn1g_acc, dsh1_acc, dsc1_acc = _rowwise(norm1_bwd_body, name="norm1_bwd", nblk=nb, tr=tr,
                                                 rows=[(xa, 0, d, 0), (dh1, 0, d, 0)], sels=[modv], fulls=[n1g],
                                                 outs=[(d, F32)], accs=[d, d, d], seg=nlb)
    grad_x = (dxa[:nl] + dx1)[None]

    red8 = lambda a: jnp.sum(a, axis=-2)
    dmod_own = jnp.concatenate([red8(dsh1_acc[0]), red8(dsc1_acc[0]), red8(dg1_acc), red8(dsh2_acc), red8(dsc2_acc), red8(dg2_acc)])
    dmod_ctx = jnp.concatenate([red8(dsh1_acc[1]), red8(dsc1_acc[1]), jnp.zeros((4 * d,), F32)])
    dm_in = jnp.zeros((8, d6), F32).at[0].set(dmod_own).at[1].set(dmod_ctx)
    dm_all = _exchange8(dm_in, "gather_dmod", True)
    dm_own_sh = lax.dynamic_slice(dm_all[:, 0, :], (0, chip * csh), (8, csh))
    dm_ctx_sh = lax.dynamic_slice(dm_all[:, 1, :], (0, chip * csh), (8, csh))

    def mod_bwd_body(c_ref, own_ref, ctx_ref, w_ref, gw_ref, gb_ref, gc_ref):
        cv = c_ref[...]
        a = _silu(cv).astype(MXU_DTYPE)
        own = own_ref[...]
        ctx_tot = ctx_ref[0:1, :]
        for j in range(1, 8):
            ctx_tot = ctx_tot + ctx_ref[j:j + 1, :]
        g16 = jnp.concatenate([own, jnp.broadcast_to(ctx_tot, own.shape)], axis=0)
        rid = lax.broadcasted_iota(jnp.int32, g16.shape, 0)
        g16 = jnp.where(rid <= 8, g16, 0.0)
        gw_ref[...] = lax.dot_general(a, g16.astype(MXU_DTYPE), (((0,), (0,)), ((), ())), preferred_element_type=F32)
        gb_ref[...] = jnp.broadcast_to(jnp.sum(own, axis=0, keepdims=True) + ctx_tot, gb_ref.shape)
        gc = lax.dot_general(jnp.broadcast_to(ctx_tot, own.shape).astype(MXU_DTYPE), w_ref[...].astype(MXU_DTYPE),
                             (((1,), (1,)), ((), ())), preferred_element_type=F32)
        gc_ref[...] = gc * _dsilu(cv[8:9, :])

    g_wmod, g_bmod_sh, g_cctx_part = pl.pallas_call(
        mod_bwd_body, name="mod_bwd", out_shape=[_sds((d, csh), F32), _sds((8, csh), F32), _sds((8, d), F32)],
        compiler_params=pltpu.CompilerParams(vmem_limit_bytes=VMEM_LIMIT))(cs16, dm_own_sh, dm_ctx_sh, w_mod[0])
    north = (mc == 0).astype(F32)
    g_bmod_part = lax.dynamic_update_slice(jnp.zeros((1, d6), F32), g_bmod_sh[0:1] * north, (0, chip * csh))
    g_cctx_part = g_cctx_part[0] * north

    small_g = {}
    for j, (sfx, _, _, _, _, _) in enumerate(dirs):
        dl = red8(dlam[j])
        dlr, dli = dl[re_cols].reshape(SSM_GROUPS, SSM_STATE), dl[im_cols].reshape(SSM_GROUPS, SSM_STATE)
        dbr, dbi = _block_diag_extract(dbbig[j][:, re_cols]), _block_diag_extract(dbbig[j][:, im_cols])
        g_lre, g_lim, g_ldt, g_bre, g_bim = disc_vjps[j]((dlr, dli, dbr, dbi))
        small_g["lam_re_" + sfx], small_g["lam_im_" + sfx], small_g["log_dt_" + sfx] = g_lre, g_lim, g_ldt
        small_g["b_re"] = small_g.get("b_re", 0.0) + g_bre
        small_g["b_im"] = small_g.get("b_im", 0.0) + g_bim
        dct = jnp.transpose(dcbig[j])
        small_g["c_re_" + sfx] = jnp.transpose(_block_diag_extract(dct[:, re_cols]), (0, 2, 1))
        small_g["c_im_" + sfx] = -jnp.transpose(_block_diag_extract(dct[:, im_cols]), (0, 2, 1))
    head_fold = lambda acc: jnp.sum(red8(acc).reshape(N_HEADS, SLOT), axis=0)[:QK_DIM]
    small_g.update(c_ctx=g_cctx_part, b_mod=g_bmod_part[0], norm1_g=red8(dn1g_acc[0]) + red8(dn1g_acc[1]),
                   norm2_g=red8(dn2g_acc), q_a_g=red8(dqag_acc), kv_a_g=red8(dkvag_acc), q_norm_g=head_fold(dqg_acc),
                   k_norm_g=head_fold(dkg_acc), d_skip=red8(ddskip_acc), conv_b=red8(dcb_acc))
    g_convw_full = jnp.stack([red8(dcw0), red8(dcw1), red8(dcw2)])
    small_names = ["c_ctx", "b_mod", "norm1_g", "norm2_g", "q_a_g", "kv_a_g", "q_norm_g", "k_norm_g",
                   "lam_re_f", "lam_im_f", "log_dt_f", "c_re_f", "c_im_f", "lam_re_b", "lam_im_b", "log_dt_b",
                   "c_re_b", "c_im_b", "b_re", "b_im", "d_skip", "conv_b"]
    small_shapes = [weights[k].shape for k in small_names]
    spack = _pack([small_g[k] for k in small_names] + [g_convw_full], rows_mult=8)
    sred = _sum8(_exchange8(spack, "gather_small_grads", True), "sum_small_grads")
    sg_list = _unpack(sred, small_shapes + [(3, f2)])
    g_small = dict(zip(small_names, sg_list[:-1]))
    g_small["conv_w"] = lax.dynamic_slice(sg_list[-1], (0, chip * cwid), (3, cwid))[None]

    gwi = g_win_p
    g_big["w_in"] = jnp.concatenate([gwi[:, q_off:q_off + Q_LORA], gwi[:, kv_off:kv_off + KV_LORA],
                                     gwi[:, kr_off:kr_off + QK_ROPE], gwi[:, u_off:u_off + SSM_WIDTH], gwi[:, :2 * d]], axis=1)
    g_big["w_uq"] = g_wuq_p.reshape(Q_LORA, N_HEADS, SLOT)[:, :, :QK_DIM].reshape(Q_LORA, N_HEADS * QK_DIM)
    gk3 = g_wukv_p[:, :hw].reshape(KV_LORA, N_HEADS, SLOT)[:, :, :QK_NOPE]
    gv3 = g_wukv_p[:, hw:].reshape(KV_LORA, N_HEADS, SLOT)[:, :, :V_DIM]
    g_big["w_ukv"] = jnp.concatenate([gk3, gv3], axis=2).reshape(KV_LORA, N_HEADS * (QK_NOPE + V_DIM))
    g_big["w_o_attn"] = g_wo_p.reshape(N_HEADS, SLOT, d)[:, :V_DIM].reshape(N_HEADS * V_DIM, d)

    def shard_half(k_, g_, j, hh):
        sr_, sc_ = weights[k_].shape[1:]
        blk = g_[j * sr_:(j + 1) * sr_] if k_ in row_sharded else g_[:, j * sc_:(j + 1) * sc_]
        return blk[hh * (sr_ // 2):(hh + 1) * (sr_ // 2)]

    half_shapes = [(s[0] // 2, s[1]) for s in shard_shapes]
    pieces = jnp.stack([_pack([shard_half(k_, g_big[k_], t // 2, t % 2) for k_ in big_names]) for t in range(8)])
    recv = _exchange8(pieces, "scatter_weight_grads", False)
    my_half = _sum8(recv, "sum_weight_grads")
    both = _sibling_exchange(my_half, "exchange_halves")
    halves = [_unpack(both[hh], half_shapes) for hh in range(2)]
    g_sh = {k_: jnp.concatenate([halves[0][i_], halves[1][i_]], axis=0)[None] for i_, k_ in enumerate(big_names)}
    g_sh["w_mod"] = g_wmod[None]

    adam_big = ["w_mod"] + big_names
    adam_small = small_names + ["conv_w"]
    grads = {**g_sh, **g_small}
    outs_d, outs_m, outs_v = {}, {}, {}
    for group, nm in ((adam_big, "adamw_big"), (adam_small, "adamw_small")):
        shapes = [weights[k_].shape for k_ in group]
        res = _adamw(_pack([grads[k_] for k_ in group]), _pack([weights[k_] for k_ in group]),
                     _pack([mom_m[k_] for k_ in group]), _pack([mom_v[k_] for k_ in group]), nm)
        for dst, buf in zip((outs_d, outs_m, outs_v), res):
            dst.update(zip(group, _unpack(buf, shapes)))
    grads = {k_: grads[k_].reshape(weights[k_].shape) for k_ in names}
    return (loss, grad_x, *[grads[k_] for k_ in names], *[outs_d[k_] for k_ in names],
            *[outs_m[k_] for k_ in names], *[outs_v[k_] for k_ in names])
```

```python
import functools
import math

import numpy as np
import jax
import jax.numpy as jnp
from jax import lax
from jax.experimental import pallas as pl
from jax.experimental.pallas import tpu as pltpu

F32 = jnp.float32
MXU_DTYPE = jnp.bfloat16
MESH = pl.DeviceIdType.MESH

EPS = 1e-6
N_HEADS = 8
QK_NOPE = 64
QK_ROPE = 32
QK_DIM = QK_NOPE + QK_ROPE
V_DIM = 64
SLOT = 128
Q_LORA = 384
KV_LORA = 256
GRID_W = 64
ROPE_THETA = 10000.0
SSM_WIDTH = 512
SSM_GROUP = 16
SSM_GROUPS = 32
SSM_STATE = 64
N_STATE = SSM_GROUPS * SSM_STATE
CG_STATES = 512
N_CG = N_STATE // CG_STATES
SCAN_LANES = 256
PACK_W = 1024

ADAM_LR = 0.001
ADAM_B1 = 0.9
ADAM_B2 = 0.999
ADAM_EPS = 1e-08
ADAM_WD = 0.01
ADAM_STEP = 10

VMEM_LIMIT = 56 * 1024 * 1024
LOG2E = 1.4426950408889634


def _pick(n, cands):
    for c in cands:
        if c <= n and n % c == 0:
            return c
    return n


def _cparams(sem):
    return pltpu.CompilerParams(dimension_semantics=sem, vmem_limit_bytes=VMEM_LIMIT)


def _sds(shape, dtype):
    return jax.ShapeDtypeStruct(tuple(shape), dtype)


_K_CANDS = (2048, 1536, 1408, 1280, 1152, 1024, 896, 768, 704, 640, 512, 384, 256, 128)
_N_CANDS = (1024, 768, 512, 384, 256, 128)


def _mm(a, b, mode, name, out_dtype=F32, rows=None, a_off=0, b_off=0):
    if mode == "tn":
        t_rows = rows or a.shape[0]
        m, n = a.shape[1], b.shape[1]
        tk = _pick(t_rows, (512, 384, 256, 128, 64, 32, 16))
        tm = m if m <= 1024 else _pick(m, _N_CANDS)
        tn = n if n <= 1024 else _pick(n, _N_CANDS)
        nk = t_rows // tk
        ao, bo = a_off // tk, b_off // tk
        grid = (m // tm, n // tn, nk)
        in_specs = [pl.BlockSpec((tk, tm), lambda i, j, k: (k + ao, i)),
                    pl.BlockSpec((tk, tn), lambda i, j, k: (k + bo, j))]
        dn = (((0,), (0,)), ((), ()))
    else:
        m = rows or a.shape[0]
        kdim = a.shape[1]
        n = b.shape[1] if mode == "nn" else b.shape[0]
        tk = kdim if kdim <= 2048 else _pick(kdim, _K_CANDS)
        tm = _pick(m, (512, 384, 256, 128, 64, 32, 16))
        tn = n if n <= 1024 else _pick(n, _N_CANDS)
        nk = kdim // tk
        ao = a_off // tm
        grid = (m // tm, n // tn, nk)
        if mode == "nn":
            in_specs = [pl.BlockSpec((tm, tk), lambda i, j, k: (i + ao, k)),
                        pl.BlockSpec((tk, tn), lambda i, j, k: (k, j))]
            dn = (((1,), (0,)), ((), ()))
        else:
            in_specs = [pl.BlockSpec((tm, tk), lambda i, j, k: (i + ao, k)),
                        pl.BlockSpec((tn, tk), lambda i, j, k: (j, k))]
            dn = (((1,), (1,)), ((), ()))
    use_scratch = nk > 1 and out_dtype != F32

    def body(a_ref, b_ref, o_ref, *scr):
        r = lax.dot_general(a_ref[...].astype(MXU_DTYPE), b_ref[...].astype(MXU_DTYPE), dn,
                            preferred_element_type=F32)
        if nk == 1:
            o_ref[...] = r.astype(o_ref.dtype)
        else:
            k = pl.program_id(2)
            acc = scr[0] if use_scratch else o_ref

            @pl.when(k == 0)
            def _():
                acc[...] = r

            @pl.when(k > 0)
            def _():
                acc[...] += r

            if use_scratch:
                @pl.when(k == nk - 1)
                def _():
                    o_ref[...] = acc[...].astype(o_ref.dtype)

    return pl.pallas_call(
        body, name=name, grid=grid, in_specs=in_specs,
        out_specs=pl.BlockSpec((tm, tn), lambda i, j, k: (i, j)),
        out_shape=_sds((m, n), out_dtype),
        scratch_shapes=[pltpu.VMEM((tm, tn), F32)] if use_scratch else [],
        compiler_params=_cparams(("parallel", "parallel", "arbitrary")),
    )(a, b)


def _rowwise(body, *, name, nblk, tr, rows=(), halo=(), sels=(), fulls=(), outs=(), accs=(), seg=None):
    n_rows, n_sel, n_full, n_out, n_acc = len(rows), len(sels), len(fulls), len(outs), len(accs)
    halo = tuple(halo)
    maxw = max([r[2] for r in rows] + [o[0] for o in outs] + list(accs))
    sr = _pick(tr, tuple(s for s in (256, 128, 64, 32, 16) if s * maxw <= 131072) or (16,))
    nsub = tr // sr
    total8 = nblk * tr // 8

    def seg_of(i):
        return jnp.where(i >= seg, 1, 0) if seg is not None else 0

    in_specs, operands = [], []
    for arr, cb, w, roff in rows:
        ob = roff // tr
        in_specs.append(pl.BlockSpec((tr, w), lambda i, cb=cb, ob=ob: (i + ob, cb)))
        operands.append(arr)
    for h in halo:
        arr, cb, w, roff = rows[h]
        o8, t8 = roff // 8, tr // 8
        in_specs.append(pl.BlockSpec((8, w), lambda i, cb=cb, o8=o8, t8=t8: (jnp.maximum(i * t8 - 1, 0) + o8, cb)))
        in_specs.append(pl.BlockSpec((8, w), lambda i, cb=cb, o8=o8, t8=t8: (jnp.minimum((i + 1) * t8, total8 - 1) + o8, cb)))
        operands += [arr, arr]
    for arr in sels:
        in_specs.append(pl.BlockSpec((None,) + arr.shape[1:], lambda i: (seg_of(i), 0, 0)))
        operands.append(arr)
    for arr in fulls:
        in_specs.append(pl.BlockSpec(arr.shape, lambda i: (0, 0)))
        operands.append(arr)
    out_specs, out_shape = [], []
    for w, dt in outs:
        out_specs.append(pl.BlockSpec((tr, w), lambda i: (i, 0)))
        out_shape.append(_sds((nblk * tr, w), dt))
    for w in accs:
        if seg is None:
            out_specs.append(pl.BlockSpec((8, w), lambda i: (0, 0)))
            out_shape.append(_sds((8, w), F32))
        else:
            out_specs.append(pl.BlockSpec((None, 8, w), lambda i: (seg_of(i), 0, 0)))
            out_shape.append(_sds((2, 8, w), F32))
    n_halo = 2 * len(halo)

    def kern(*refs):
        row_refs = refs[:n_rows]
        halo_refs = refs[n_rows:n_rows + n_halo]
        sel_refs = refs[n_rows + n_halo:n_rows + n_halo + n_sel]
        full_refs = refs[n_rows + n_halo + n_sel:n_rows + n_halo + n_sel + n_full]
        o0 = n_rows + n_halo + n_sel + n_full
        out_refs = refs[o0:o0 + n_out]
        acc_refs = refs[o0 + n_out:o0 + n_out + n_acc]
        i = pl.program_id(0)
        if n_acc:
            first = (i == 0) if seg is None else ((i == 0) | (i == seg))

            @pl.when(first)
            def _():
                for a_ref in acc_refs:
                    a_ref[...] = jnp.zeros(a_ref.shape, F32)

        def sub(s, carry):
            r0 = pl.multiple_of(s * sr, sr)
            vals = []
            for idx, r in enumerate(row_refs):
                cur = r[pl.ds(r0, sr), :]
                if idx in halo:
                    hp = halo_refs[2 * halo.index(idx)]
                    hn = halo_refs[2 * halo.index(idx) + 1]
                    cur = cur.astype(F32)
                    rid = lax.broadcasted_iota(jnp.int32, cur.shape, 0)
                    lo = r[pl.ds(pl.multiple_of(jnp.maximum(r0 - 8, 0), 8), 8), :].astype(F32)
                    lo = jnp.where(s == 0, hp[...].astype(F32), lo)
                    lo = jnp.where((s == 0) & (i == 0), 0.0, lo)
                    hi = r[pl.ds(pl.multiple_of(jnp.minimum(r0 + sr, tr - 8), 8), 8), :].astype(F32)
                    hi = jnp.where(s == nsub - 1, hn[...].astype(F32), hi)
                    hi = jnp.where((s == nsub - 1) & (i == nblk - 1), 0.0, hi)
                    prev = jnp.where(rid == 0, jnp.broadcast_to(lo[7:8, :], cur.shape), pltpu.roll(cur, 1, 0))
                    nxt = jnp.where(rid == sr - 1, jnp.broadcast_to(hi[0:1, :], cur.shape), pltpu.roll(cur, sr - 1, 0))
                    vals.append((prev, cur, nxt))
                else:
                    vals.append(cur)
            res = body(*vals, *[r[...] for r in sel_refs], *[r[...] for r in full_refs])
            if not isinstance(res, (tuple, list)):
                res = (res,)
            for o_ref, v in zip(out_refs, res[:n_out]):
                o_ref[pl.ds(r0, sr), :] = v.astype(o_ref.dtype)
            for a_ref, v in zip(acc_refs, res[n_out:]):
                a_ref[...] += jnp.sum(v.astype(F32).reshape(sr // 8, 8, v.shape[-1]), axis=0)
            return carry

        lax.fori_loop(0, nsub, sub, 0)

    res = pl.pallas_call(
        kern, name=name, grid=(nblk,), in_specs=in_specs, out_specs=out_specs, out_shape=out_shape,
        compiler_params=_cparams(("arbitrary",)),
    )(*operands)
    return res


def _sigmoid(x):
    return 1.0 / (1.0 + jnp.exp(-x))


def _silu(x):
    return x * _sigmoid(x)


def _dsilu(x):
    s = _sigmoid(x)
    return s * (1.0 + x * (1.0 - s))


_GELU_K = math.sqrt(2.0 / math.pi)


def _gelu(x):
    return 0.5 * x * (1.0 + jnp.tanh(_GELU_K * (x + 0.044715 * x * x * x)))


def _dgelu(x):
    t = jnp.tanh(_GELU_K * (x + 0.044715 * x * x * x))
    return 0.5 * (1.0 + t) + 0.5 * x * (1.0 - t * t) * _GELU_K * (1.0 + 3.0 * 0.044715 * x * x)


def _rms_fwd(x, g, width):
    r = lax.rsqrt(jnp.sum(x * x, axis=-1, keepdims=True) * (1.0 / width) + EPS)
    return x * r * g


def _rms_bwd(x, g, dy, width):
    r = lax.rsqrt(jnp.sum(x * x, axis=-1, keepdims=True) * (1.0 / width) + EPS)
    xn = x * r
    dyg = dy * g
    dx = r * (dyg - xn * (jnp.sum(dyg * xn, axis=-1, keepdims=True) * (1.0 / width)))
    return dx, dy * xn


def _rope_fwd(y, c, sa, sb):
    return y * c + pltpu.roll(y, SLOT - 16, 1) * sa + pltpu.roll(y, 16, 1) * sb


def _rope_bwd(d, c, sa, sb):
    return d * c + pltpu.roll(d * sa, 16, 1) + pltpu.roll(d * sb, SLOT - 16, 1)


def _heads(v):
    return [v[:, h * SLOT:(h + 1) * SLOT] for h in range(N_HEADS)]


def _attn_fwd(q, k, v, nl, scale):
    n = k.shape[0]
    tq = _pick(nl, (1024, 512, 256, 128))
    tk = _pick(n, (2816, 1408, 1152, 768, 384, 256, 128))
    sub = min(tq, 256)
    nk = n // tk
    rep = tk // SLOT
    c = scale * LOG2E

    def body(q_ref, k_ref, v_ref, o_ref, lse_ref, m_sc, l_sc, acc_sc):
        ki = pl.program_id(2)

        @pl.when(ki == 0)
        def _():
            m_sc[...] = jnp.full(m_sc.shape, -jnp.inf, F32)
            l_sc[...] = jnp.zeros(l_sc.shape, F32)
            acc_sc[...] = jnp.zeros(acc_sc.shape, F32)

        kb, vb = k_ref[...], v_ref[...]
        for sb in range(tq // sub):
            rows = slice(sb * sub, (sb + 1) * sub)
            s = lax.dot_general(q_ref[rows, :], kb, (((1,), (1,)), ((), ())), preferred_element_type=F32)
            m_prev = m_sc[rows, :]
            m_new = jnp.maximum(m_prev, jnp.max(s, axis=1, keepdims=True) * c)
            alpha = jnp.exp2(m_prev - m_new)
            p = jnp.exp2(s * c - jnp.tile(m_new, (1, rep)))
            l_sc[rows, :] = alpha * l_sc[rows, :] + jnp.sum(p, axis=1, keepdims=True)
            acc_sc[rows, :] = alpha * acc_sc[rows, :] + jnp.dot(p.astype(MXU_DTYPE), vb, preferred_element_type=F32)
            m_sc[rows, :] = m_new

        @pl.when(ki == nk - 1)
        def _():
            l = l_sc[...]
            o_ref[...] = (acc_sc[...] / l).astype(o_ref.dtype)
            lse_ref[...] = m_sc[...] + jnp.log2(l)

    return pl.pallas_call(
        body, name="attn_fwd", grid=(N_HEADS, nl // tq, nk),
        in_specs=[pl.BlockSpec((tq, SLOT), lambda h, i, j: (i, h)),
                  pl.BlockSpec((tk, SLOT), lambda h, i, j: (j, h)),
                  pl.BlockSpec((tk, SLOT), lambda h, i, j: (j, h))],
        out_specs=[pl.BlockSpec((tq, SLOT), lambda h, i, j: (i, h)),
                   pl.BlockSpec((tq, SLOT), lambda h, i, j: (i, h))],
        out_shape=[_sds((nl, N_HEADS * SLOT), MXU_DTYPE), _sds((nl, N_HEADS * SLOT), F32)],
        scratch_shapes=[pltpu.VMEM((tq, SLOT), F32), pltpu.VMEM((tq, SLOT), F32), pltpu.VMEM((tq, SLOT), F32)],
        compiler_params=_cparams(("parallel", "parallel", "arbitrary")),
    )(q, k, v)


def _attn_bwd(q, k, kt, v, do, lse_t, delta_t, nl, scale):
    n = k.shape[0]
    tq = _pick(nl, (1024, 512, 256, 128))
    tk = _pick(n, (2816, 1408, 1152, 768, 384, 256, 128))
    sub = _pick(tk, (256, 128))
    nq, nk = nl // tq, n // tk
    c = scale * LOG2E

    def body(q_ref, k_ref, kt_ref, v_ref, do_ref, lse_ref, dl_ref, dq_ref, dk_ref, dv_ref, dk_acc, dv_acc):
        ki, qi = pl.program_id(1), pl.program_id(2)

        @pl.when((ki == 0) & (qi == 0))
        def _():
            dq_ref[...] = jnp.zeros(dq_ref.shape, F32)

        @pl.when(qi == 0)
        def _():
            dk_acc[...] = jnp.zeros(dk_acc.shape, F32)
            dv_acc[...] = jnp.zeros(dv_acc.shape, F32)

        qb, dob = q_ref[...], do_ref[...]
        lse_r, dl_r = lse_ref[0:1, :], dl_ref[0:1, :]
        dq_part = None
        for sb in range(tk // sub):
            rows = slice(sb * sub, (sb + 1) * sub)
            s_t = lax.dot_general(k_ref[rows, :], qb, (((1,), (1,)), ((), ())), preferred_element_type=F32)
            p_t = jnp.exp2(s_t * c - lse_r)
            dp_t = lax.dot_general(v_ref[rows, :], dob, (((1,), (1,)), ((), ())), preferred_element_type=F32)
            ds_t = (p_t * (dp_t - dl_r) * scale).astype(MXU_DTYPE)
            dv_acc[rows, :] += jnp.dot(p_t.astype(MXU_DTYPE), dob, preferred_element_type=F32)
            dk_acc[rows, :] += jnp.dot(ds_t, qb, preferred_element_type=F32)
            part = jnp.dot(kt_ref[:, rows], ds_t, preferred_element_type=F32)
            dq_part = part if dq_part is None else dq_part + part
        c0 = pl.multiple_of(qi * tq, tq)
        dq_ref[:, pl.ds(c0, tq)] += dq_part

        @pl.when(qi == nq - 1)
        def _():
            dk_ref[...] = dk_acc[...]
            dv_ref[...] = dv_acc[...]

    return pl.pallas_call(
        body, name="attn_bwd", grid=(N_HEADS, nk, nq),
        in_specs=[pl.BlockSpec((tq, SLOT), lambda h, j, i: (i, h)),
                  pl.BlockSpec((tk, SLOT), lambda h, j, i: (j, h)),
                  pl.BlockSpec((SLOT, tk), lambda h, j, i: (h, j)),
                  pl.BlockSpec((tk, SLOT), lambda h, j, i: (j, h)),
                  pl.BlockSpec((tq, SLOT), lambda h, j, i: (i, h)),
                  pl.BlockSpec((None, 8, tq), lambda h, j, i: (h, 0, i)),
                  pl.BlockSpec((None, 8, tq), lambda h, j, i: (h, 0, i))],
        out_specs=[pl.BlockSpec((SLOT, nl), lambda h, j, i: (h, 0)),
                   pl.BlockSpec((tk, SLOT), lambda h, j, i: (j, h)),
                   pl.BlockSpec((tk, SLOT), lambda h, j, i: (j, h))],
        out_shape=[_sds((N_HEADS * SLOT, nl), F32), _sds((n, N_HEADS * SLOT), F32), _sds((n, N_HEADS * SLOT), F32)],
        scratch_shapes=[pltpu.VMEM((tk, SLOT), F32), pltpu.VMEM((tk, SLOT), F32)],
        compiler_params=_cparams(("arbitrary", "arbitrary", "arbitrary")),
    )(q, k, kt, v, do, lse_t, delta_t)


def _scan_consts(c_ref, lg):
    cs = slice(lg * SCAN_LANES, (lg + 1) * SCAN_LANES)
    return [c_ref[8 * kk:8 * kk + 8, cs] for kk in range(8)]


def _tile_scan(br, bi, consts, reverse):
    p1r, p1i, p2r, p2i, p4r, p4i = consts[:6]
    for pr, pi, kk in ((p1r, p1i, 1), (p2r, p2i, 2), (p4r, p4i, 4)):
        sh = (8 - kk) if reverse else kk
        sr_, si_ = pltpu.roll(br, sh, 0), pltpu.roll(bi, sh, 0)
        br, bi = br + pr * sr_ - pi * si_, bi + pr * si_ + pi * sr_
    return br, bi


def _seq_chunk(j, nch, nlc, reverse):
    return (nch - 1 - j) if reverse else (j + nlc) % nch


def _s5_scan(z, u_cb, bbig, lamc, t_rows, nl, reverse, name):
    n = z.shape[0]
    nch, nlc = n // t_rows, nl // t_rows
    ntile = t_rows // 8
    w = SCAN_LANES
    edge = 0 if reverse else 7

    def chunk(j):
        return _seq_chunk(j, nch, nlc, reverse)

    def body(u_ref, b_ref, c_ref, xs_ref, carry):
        j = pl.program_id(1)

        @pl.when(j == 0)
        def _():
            carry[...] = jnp.zeros(carry.shape, F32)

        xs_ref[...] = jnp.dot(u_ref[...].astype(MXU_DTYPE), b_ref[...], preferred_element_type=F32)
        for lg in range(CG_STATES // w):
            re = slice(lg * w, (lg + 1) * w)
            im = slice(CG_STATES + lg * w, CG_STATES + (lg + 1) * w)
            consts = _scan_consts(c_ref, lg)
            qr, qi = consts[6], consts[7]

            def tile(tt, st):
                cr, ci = st
                t = (ntile - 1 - tt) if reverse else tt
                r0 = pl.multiple_of(t * 8, 8)
                br, bi = _tile_scan(xs_ref[pl.ds(r0, 8), re], xs_ref[pl.ds(r0, 8), im], consts, reverse)
                lr = jnp.broadcast_to(cr[edge:edge + 1, :], br.shape)
                li = jnp.broadcast_to(ci[edge:edge + 1, :], bi.shape)
                xr = br + qr * lr - qi * li
                xi = bi + qr * li + qi * lr
                xs_ref[pl.ds(r0, 8), re] = xr
                xs_ref[pl.ds(r0, 8), im] = xi
                return xr, xi

            cr, ci = lax.fori_loop(0, ntile, tile, (carry[:, re], carry[:, im]))
            carry[:, re] = cr
            carry[:, im] = ci

    cw = 2 * CG_STATES
    return pl.pallas_call(
        body, name=name, grid=(N_CG, nch),
        in_specs=[pl.BlockSpec((t_rows, SSM_WIDTH), lambda g, j: (chunk(j), u_cb)),
                  pl.BlockSpec((SSM_WIDTH, cw), lambda g, j: (0, g)),
                  pl.BlockSpec((64, CG_STATES), lambda g, j: (0, g))],
        out_specs=pl.BlockSpec((t_rows, cw), lambda g, j: (chunk(j), g)),
        out_shape=_sds((n, 2 * N_STATE), F32),
        scratch_shapes=[pltpu.VMEM((8, cw), F32)],
        compiler_params=_cparams(("arbitrary", "arbitrary")),
    )(z, bbig, lamc)


def _s5_bwd(dys, z, u_cb, xs, cbig_t, bbig_t, lamc_adj, t_rows, nl, reverse, name):
    n = z.shape[0]
    nch, nlc = n // t_rows, nl // t_rows
    ntile = t_rows // 8
    t8 = t_rows // 8
    w = SCAN_LANES
    cw = 2 * CG_STATES
    adj_rev = not reverse
    edge = 0 if adj_rev else 7

    def chunk(j):
        return _seq_chunk(nch - 1 - j, nch, nlc, reverse)

    def halo_blk(j):
        if reverse:
            return jnp.minimum((chunk(j) + 1) * t8, n // 8 - 1)
        return (_seq_chunk(jnp.maximum(nch - 2 - j, 0), nch, nlc, False) + 1) * t8 - 1

    def body(dy_ref, u_ref, xs_ref, halo_ref, ct_ref, bt_ref, c_ref, du_ref, db_ref, dl_ref, gbuf, carry):
        j = pl.program_id(1)
        start = j == nch - 1

        @pl.when(j == 0)
        def _():
            carry[...] = jnp.zeros(carry.shape, F32)
            db_ref[...] = jnp.zeros(db_ref.shape, F32)
            dl_ref[...] = jnp.zeros(dl_ref.shape, F32)

        g0 = jnp.dot(dy_ref[...].astype(MXU_DTYPE), ct_ref[...], preferred_element_type=F32)
        gbuf[...] = jnp.where(chunk(j) < nlc, g0, 0.0)
        for lg in range(CG_STATES // w):
            re = slice(lg * w, (lg + 1) * w)
            im = slice(CG_STATES + lg * w, CG_STATES + (lg + 1) * w)
            consts = _scan_consts(c_ref, lg)
            qr, qi = consts[6], consts[7]
            hr, hi = halo_ref[:, re], halo_ref[:, im]

            def tile(tt, st):
                gcr, gci, ar, ai = st
                t = (ntile - 1 - tt) if adj_rev else tt
                r0 = pl.multiple_of(t * 8, 8)
                br, bi = _tile_scan(gbuf[pl.ds(r0, 8), re], gbuf[pl.ds(r0, 8), im], consts, adj_rev)
                lr = jnp.broadcast_to(gcr[edge:edge + 1, :], br.shape)
                li = jnp.broadcast_to(gci[edge:edge + 1, :], bi.shape)
                gr = br + qr * lr - qi * li
                gi = bi + qr * li + qi * lr
                gbuf[pl.ds(r0, 8), re] = gr
                gbuf[pl.ds(r0, 8), im] = gi
                xr, xi = xs_ref[pl.ds(r0, 8), re], xs_ref[pl.ds(r0, 8), im]
                rid = lax.broadcasted_iota(jnp.int32, xr.shape, 0)
                if reverse:
                    last = t == ntile - 1
                    rn = pl.multiple_of(jnp.minimum(r0 + 8, t_rows - 8), 8)
                    nbr = jnp.where(last, hr, xs_ref[pl.ds(rn, 8), re])
                    nbi = jnp.where(last, hi, xs_ref[pl.ds(rn, 8), im])
                    nbr = jnp.where(last & start, 0.0, nbr)
                    nbi = jnp.where(last & start, 0.0, nbi)
                    xpr = jnp.where(rid == 7, jnp.broadcast_to(nbr[0:1, :], xr.shape), pltpu.roll(xr, 7, 0))
                    xpi = jnp.where(rid == 7, jnp.broadcast_to(nbi[0:1, :], xi.shape), pltpu.roll(xi, 7, 0))
                else:
                    first = t == 0
                    rn = pl.multiple_of(jnp.maximum(r0 - 8, 0), 8)
                    nbr = jnp.where(first, hr, xs_ref[pl.ds(rn, 8), re])
                    nbi = jnp.where(first, hi, xs_ref[pl.ds(rn, 8), im])
                    nbr = jnp.where(first & start, 0.0, nbr)
                    nbi = jnp.where(first & start, 0.0, nbi)
                    xpr = jnp.where(rid == 0, jnp.broadcast_to(nbr[7:8, :], xr.shape), pltpu.roll(xr, 1, 0))
                    xpi = jnp.where(rid == 0, jnp.broadcast_to(nbi[7:8, :], xi.shape), pltpu.roll(xi, 1, 0))
                ar = ar + gr * xpr + gi * xpi
                ai = ai - gr * xpi + gi * xpr
                return gr, gi, ar, ai

            zz = jnp.zeros((8, w), F32)
            gcr, gci, ar, ai = lax.fori_loop(0, ntile, tile, (carry[:, re], carry[:, im], zz, zz))
            carry[:, re] = gcr
            carry[:, im] = gci
            dl_ref[:, re] += ar
            dl_ref[:, im] += ai
        g = gbuf[...].astype(MXU_DTYPE)
        du_ref[...] = jnp.dot(g, bt_ref[...], preferred_element_type=F32)
        db_ref[...] += lax.dot_general(u_ref[...].astype(MXU_DTYPE), g, (((0,), (0,)), ((), ())),
                                       preferred_element_type=F32)

    return pl.pallas_call(
        body, name=name, grid=(N_CG, nch),
        in_specs=[pl.BlockSpec((t_rows, SSM_WIDTH), lambda g, j: (jnp.minimum(chunk(j), nlc - 1), 0)),
                  pl.BlockSpec((t_rows, SSM_WIDTH), lambda g, j: (chunk(j), u_cb)),
                  pl.BlockSpec((t_rows, cw), lambda g, j: (chunk(j), g)),
                  pl.BlockSpec((8, cw), lambda g, j: (halo_blk(j), g)),
                  pl.BlockSpec((SSM_WIDTH, cw), lambda g, j: (0, g)),
                  pl.BlockSpec((cw, SSM_WIDTH), lambda g, j: (g, 0)),
                  pl.BlockSpec((64, CG_STATES), lambda g, j: (0, g))],
        out_specs=[pl.BlockSpec((None, t_rows, SSM_WIDTH), lambda g, j: (g, chunk(j), 0)),
                   pl.BlockSpec((SSM_WIDTH, cw), lambda g, j: (0, g)),
                   pl.BlockSpec((8, cw), lambda g, j: (0, g))],
        out_shape=[_sds((N_CG, n, SSM_WIDTH), F32), _sds((SSM_WIDTH, 2 * N_STATE), F32), _sds((8, 2 * N_STATE), F32)],
        scratch_shapes=[pltpu.VMEM((t_rows, cw), F32), pltpu.VMEM((8, cw), F32)],
        compiler_params=_cparams(("arbitrary", "arbitrary")),
    )(dys, z, xs, xs, cbig_t, bbig_t, lamc_adj)


def _to_cols(re_part, im_part):
    r = re_part.shape[0]
    parts = [t.reshape(r, N_CG, 1, CG_STATES) for t in (re_part, im_part)]
    return jnp.concatenate(parts, axis=2).reshape(r, 2 * N_STATE)


def _from_cols(d):
    r = d.shape[0]
    d4 = d.reshape(r, N_CG, 2, CG_STATES)
    return d4[:, :, 0].reshape(r, N_STATE), d4[:, :, 1].reshape(r, N_STATE)


def _group_mask():
    idx = jnp.arange(SSM_GROUPS)
    return (idx[:, None] == idx[None, :])[:, None, :, None]


def _block_diag(p_gnc):
    t = jnp.transpose(p_gnc, (0, 2, 1))[:, :, None, :]
    return jnp.where(_group_mask(), t, 0.0).reshape(SSM_GROUPS * SSM_GROUP, SSM_GROUPS * SSM_STATE)


def _block_diag_extract(d):
    d4 = d.reshape(SSM_GROUPS, SSM_GROUP, SSM_GROUPS, SSM_STATE)
    blk = jnp.sum(jnp.where(_group_mask(), d4, 0.0), axis=2)
    return jnp.transpose(blk, (0, 2, 1))


def _s5_disc(lam_re, lam_im, log_dt, b_re, b_im):
    lam = lax.complex(lam_re, lam_im)
    dt = jnp.exp(log_dt)[:, None]
    lam_bar = jnp.exp(lam * dt)
    b_bar = ((lam_bar - 1.0) / lam)[..., None] * lax.complex(b_re, b_im)
    return jnp.real(lam_bar), jnp.imag(lam_bar), jnp.real(b_bar), jnp.imag(b_bar)


def _lam_consts(lr, li, mirrored, conj):
    lam = lax.complex(lr.reshape(-1), -li.reshape(-1) if conj else li.reshape(-1))
    p2 = lam * lam
    p4 = p2 * p2
    pw = [lam, p2, p2 * lam, p4, p4 * lam, p4 * p2, p4 * p2 * lam, p4 * p4]
    rows = jnp.arange(8)[:, None]
    out = []
    for kk in (1, 2, 4):
        mask = (rows <= 7 - kk) if mirrored else (rows >= kk)
        pk = jnp.where(mask, pw[kk - 1][None, :], 0.0)
        out += [jnp.real(pk), jnp.imag(pk)]
    q = jnp.stack(pw[::-1] if mirrored else pw)
    return jnp.concatenate(out + [jnp.real(q), jnp.imag(q)], axis=0)


def _dev(t):
    return (t // 4, (t // 2) % 2, t % 2)


def _my_index():
    return 4 * lax.axis_index("x") + 2 * lax.axis_index("y") + lax.axis_index("c")


def _comm_call(body, name, arrs, lead, n_remote):
    nw = len(arrs)
    any_spec = pl.BlockSpec(memory_space=pl.ANY)
    return pl.pallas_call(
        body, name=name, out_shape=[_sds((lead,) + a.shape[-2:], a.dtype) for a in arrs],
        in_specs=[any_spec] * nw, out_specs=[any_spec] * nw,
        scratch_shapes=[pltpu.SemaphoreType.DMA((n_remote * nw,)), pltpu.SemaphoreType.DMA((n_remote * nw,)),
                        pltpu.SemaphoreType.DMA((nw,))],
    )(*arrs)


def _exchange8(gs, name, same):
    nw = len(gs)

    def body(*refs):
        g_refs, o_refs, (ssem, rsem, lsem) = refs[:nw], refs[nw:2 * nw], refs[2 * nw:]
        me = _my_index()
        locs, sends = [], []
        for i, (g_ref, o_ref) in enumerate(zip(g_refs, o_refs)):
            src = (lambda t, g_ref=g_ref: g_ref) if same else (lambda t, g_ref=g_ref: g_ref.at[t])
            loc = pltpu.make_async_copy(src(me), o_ref.at[me], lsem.at[i])
            loc.start()
            locs.append(loc)
            for d in range(1, 8):
                t = (me + d) % 8
                cp = pltpu.make_async_remote_copy(src_ref=src(t), dst_ref=o_ref.at[me], send_sem=ssem.at[7 * i + d - 1],
                                                  recv_sem=rsem.at[7 * i + d - 1], device_id=_dev(t), device_id_type=MESH)
                cp.start()
                sends.append(cp)
        for i, (g_ref, o_ref) in enumerate(zip(g_refs, o_refs)):
            src = (lambda t, g_ref=g_ref: g_ref) if same else (lambda t, g_ref=g_ref: g_ref.at[t])
            for d in range(1, 8):
                s = (me + 8 - d) % 8
                pltpu.make_async_remote_copy(src_ref=src(s), dst_ref=o_ref.at[s], send_sem=ssem.at[7 * i + d - 1],
                                             recv_sem=rsem.at[7 * i + d - 1], device_id=_dev(s),
                                             device_id_type=MESH).wait_recv()
        for cp in sends:
            cp.wait_send()
        for loc in locs:
            loc.wait()

    return _comm_call(body, name, gs, 8, 7)


def _allgather_chips(ws, name):
    nw = len(ws)

    def body(*refs):
        w_refs, o_refs, (ssem, rsem, lsem) = refs[:nw], refs[nw:2 * nw], refs[2 * nw:]
        x, y, cc = lax.axis_index("x"), lax.axis_index("y"), lax.axis_index("c")
        k = 2 * x + y
        peers = [(1 - x, y), (x, 1 - y), (1 - x, 1 - y)]
        locs, sends = [], []
        for i, (w_ref, o_ref) in enumerate(zip(w_refs, o_refs)):
            loc = pltpu.make_async_copy(w_ref, o_ref.at[k], lsem.at[i])
            loc.start()
            locs.append(loc)
            for j, (px, py) in enumerate(peers):
                cp = pltpu.make_async_remote_copy(src_ref=w_ref, dst_ref=o_ref.at[k], send_sem=ssem.at[3 * i + j],
                                                  recv_sem=rsem.at[3 * i + j], device_id=(px, py, cc), device_id_type=MESH)
                cp.start()
                sends.append(cp)
        for i, (w_ref, o_ref) in enumerate(zip(w_refs, o_refs)):
            for j, (px, py) in enumerate(peers):
                pltpu.make_async_remote_copy(src_ref=w_ref, dst_ref=o_ref.at[2 * px + py], send_sem=ssem.at[3 * i + j],
                                             recv_sem=rsem.at[3 * i + j], device_id=(px, py, cc),
                                             device_id_type=MESH).wait_recv()
        for cp in sends:
            cp.wait_send()
        for loc in locs:
            loc.wait()

    return _comm_call(body, name, ws, 4, 3)


def _sibling_exchange(hs, name):
    nw = len(hs)

    def body(*refs):
        h_refs, o_refs, (ssem, rsem, lsem) = refs[:nw], refs[nw:2 * nw], refs[2 * nw:]
        x, y, cc = lax.axis_index("x"), lax.axis_index("y"), lax.axis_index("c")
        locs, sends = [], []
        for i, (h_ref, o_ref) in enumerate(zip(h_refs, o_refs)):
            loc = pltpu.make_async_copy(h_ref, o_ref.at[cc], lsem.at[i])
            loc.start()
            locs.append(loc)
            cp = pltpu.make_async_remote_copy(src_ref=h_ref, dst_ref=o_ref.at[cc], send_sem=ssem.at[i], recv_sem=rsem.at[i],
                                              device_id=(x, y, 1 - cc), device_id_type=MESH)
            cp.start()
            sends.append(cp)
        for i, (h_ref, o_ref) in enumerate(zip(h_refs, o_refs)):
            pltpu.make_async_remote_copy(src_ref=h_ref, dst_ref=o_ref.at[1 - cc], send_sem=ssem.at[i], recv_sem=rsem.at[i],
                                         device_id=(x, y, 1 - cc), device_id_type=MESH).wait_recv()
        for cp in sends:
            cp.wait_send()
        for loc in locs:
            loc.wait()

    return _comm_call(body, name, hs, 2, 1)


def _sum8(buf, name):
    _, r, c = buf.shape
    tr = _pick(r, (256, 128, 64, 32, 16, 8))
    flat = buf.reshape(8 * r, c)

    def body(*v):
        acc = v[0]
        for t in v[1:]:
            acc = acc + t
        return acc

    return _rowwise(body, name=name, nblk=r // tr, tr=tr, rows=[(flat, 0, c, s * r) for s in range(8)],
                    outs=[(c, F32)])[0]


def _pack(arrs, rows_mult=16):
    flat = jnp.concatenate([a.reshape(-1).astype(F32) for a in arrs])
    nel = flat.shape[0]
    r = -(-nel // PACK_W)
    r = -(-r // rows_mult) * rows_mult
    return jnp.pad(flat, (0, r * PACK_W - nel)).reshape(r, PACK_W)


def _unpack(buf, shapes):
    flat = buf.reshape(-1)
    out, o = [], 0
    for s in shapes:
        nel = int(np.prod(s))
        out.append(flat[o:o + nel].reshape(s))
        o += nel
    return out


def _adamw(g, w, m, v, name):
    r, wd = g.shape
    tr = _pick(r, tuple(t for t in (256, 128, 64, 32, 16, 8) if t * wd <= 262144) or (8,))
    c1 = 1.0 / (1.0 - ADAM_B1 ** ADAM_STEP)
    c2 = 1.0 / (1.0 - ADAM_B2 ** ADAM_STEP)

    def body(gv, wv, mv, vv):
        mn = ADAM_B1 * mv + (1.0 - ADAM_B1) * gv
        vn = ADAM_B2 * vv + (1.0 - ADAM_B2) * (gv * gv)
        delta = -ADAM_LR * ((mn * c1) / (jnp.sqrt(vn * c2) + ADAM_EPS) + ADAM_WD * wv)
        return delta, mn, vn

    return _rowwise(body, name=name, nblk=r // tr, tr=tr, rows=[(a, 0, wd, 0) for a in (g, w, m, v)],
                    outs=[(wd, F32)] * 3)


def kernel(x, c, ctx, c_ctx, w_mod, b_mod, norm1_g, norm2_g, w_in, q_a_g, w_uq, kv_a_g, w_ukv, q_norm_g, k_norm_g, w_o_attn, lam_re_f, lam_im_f, log_dt_f, c_re_f, c_im_f, lam_re_b, lam_im_b, log_dt_b, c_re_b, c_im_b, b_re, b_im, d_skip, w_glu, w_out, w_up, conv_w, conv_b, w_down, loss_target, m_c_ctx, m_w_mod, m_b_mod, m_norm1_g, m_norm2_g, m_w_in, m_q_a_g, m_w_uq, m_kv_a_g, m_w_ukv, m_q_norm_g, m_k_norm_g, m_w_o_attn, m_lam_re_f, m_lam_im_f, m_log_dt_f, m_c_re_f, m_c_im_f, m_lam_re_b, m_lam_im_b, m_log_dt_b, m_c_re_b, m_c_im_b, m_b_re, m_b_im, m_d_skip, m_w_glu, m_w_out, m_w_up, m_conv_w, m_conv_b, m_w_down, v_c_ctx, v_w_mod, v_b_mod, v_norm1_g, v_norm2_g, v_w_in, v_q_a_g, v_w_uq, v_kv_a_g, v_w_ukv, v_q_norm_g, v_k_norm_g, v_w_o_attn, v_lam_re_f, v_lam_im_f, v_log_dt_f, v_c_re_f, v_c_im_f, v_lam_re_b, v_lam_im_b, v_log_dt_b, v_c_re_b, v_c_im_b, v_b_re, v_b_im, v_d_skip, v_w_glu, v_w_out, v_w_up, v_conv_w, v_conv_b, v_w_down):
    weights = dict(c_ctx=c_ctx, w_mod=w_mod, b_mod=b_mod, norm1_g=norm1_g, norm2_g=norm2_g, w_in=w_in, q_a_g=q_a_g, w_uq=w_uq, kv_a_g=kv_a_g, w_ukv=w_ukv, q_norm_g=q_norm_g, k_norm_g=k_norm_g, w_o_attn=w_o_attn, lam_re_f=lam_re_f, lam_im_f=lam_im_f, log_dt_f=log_dt_f, c_re_f=c_re_f, c_im_f=c_im_f, lam_re_b=lam_re_b, lam_im_b=lam_im_b, log_dt_b=log_dt_b, c_re_b=c_re_b, c_im_b=c_im_b, b_re=b_re, b_im=b_im, d_skip=d_skip, w_glu=w_glu, w_out=w_out, w_up=w_up, conv_w=conv_w, conv_b=conv_b, w_down=w_down)
    mom_m = dict(c_ctx=m_c_ctx, w_mod=m_w_mod, b_mod=m_b_mod, norm1_g=m_norm1_g, norm2_g=m_norm2_g, w_in=m_w_in, q_a_g=m_q_a_g, w_uq=m_w_uq, kv_a_g=m_kv_a_g, w_ukv=m_w_ukv, q_norm_g=m_q_norm_g, k_norm_g=m_k_norm_g, w_o_attn=m_w_o_attn, lam_re_f=m_lam_re_f, lam_im_f=m_lam_im_f, log_dt_f=m_log_dt_f, c_re_f=m_c_re_f, c_im_f=m_c_im_f, lam_re_b=m_lam_re_b, lam_im_b=m_lam_im_b, log_dt_b=m_log_dt_b, c_re_b=m_c_re_b, c_im_b=m_c_im_b, b_re=m_b_re, b_im=m_b_im, d_skip=m_d_skip, w_glu=m_w_glu, w_out=m_w_out, w_up=m_w_up, conv_w=m_conv_w, conv_b=m_conv_b, w_down=m_w_down)
    mom_v = dict(c_ctx=v_c_ctx, w_mod=v_w_mod, b_mod=v_b_mod, norm1_g=v_norm1_g, norm2_g=v_norm2_g, w_in=v_w_in, q_a_g=v_q_a_g, w_uq=v_w_uq, kv_a_g=v_kv_a_g, w_ukv=v_w_ukv, q_norm_g=v_q_norm_g, k_norm_g=v_k_norm_g, w_o_attn=v_w_o_attn, lam_re_f=v_lam_re_f, lam_im_f=v_lam_im_f, log_dt_f=v_log_dt_f, c_re_f=v_c_re_f, c_im_f=v_c_im_f, lam_re_b=v_lam_re_b, lam_im_b=v_lam_im_b, log_dt_b=v_log_dt_b, c_re_b=v_c_re_b, c_im_b=v_c_im_b, b_re=v_b_re, b_im=v_b_im, d_skip=v_d_skip, w_glu=v_w_glu, w_out=v_w_out, w_up=v_w_up, conv_w=v_conv_w, conv_b=v_conv_b, w_down=v_w_down)
    names = list(weights)

    nl, d = x.shape[1], x.shape[2]
    nc = ctx.shape[1]
    n = nl + nc
    f2 = conv_b.shape[1]
    fh = f2 // 2
    d6 = b_mod.shape[1]
    mx, my, mc = lax.axis_index("x"), lax.axis_index("y"), lax.axis_index("c")
    chip = 2 * mx + my
    me = 4 * mx + 2 * my + mc
    tr = _pick(math.gcd(nl, nc), (256, 128, 64, 32, 16))
    nlb, nb = nl // tr, n // tr

    big_names = ["w_in", "w_uq", "w_ukv", "w_o_attn", "w_glu", "w_out", "w_up", "w_down"]
    row_sharded = ("w_out", "w_down")
    gathered = _allgather_chips([weights[k][0].astype(MXU_DTYPE) for k in big_names], "gather_weights")
    full = {}
    for k_, gth in zip(big_names, gathered):
        _, r_, c_ = gth.shape
        full[k_] = gth.reshape(4 * r_, c_) if k_ in row_sharded else jnp.transpose(gth, (1, 0, 2)).reshape(r_, 4 * c_)

    cwid = conv_w.shape[2]
    sw = -(-max(d, cwid) // 128) * 128
    small_in = jnp.concatenate([jnp.pad(c, ((0, 0), (0, sw - d))), jnp.pad(conv_w[0], ((0, 4), (0, sw - cwid)))], axis=0)
    (small_all,) = _exchange8([small_in], "gather_c", True)
    cs = small_all[:, 0, :d]
    conv_w_full = jnp.concatenate([small_all[2 * j, 1:4, :cwid] for j in range(4)], axis=1)
    cs16 = jnp.concatenate([cs, c_ctx[None, :], jnp.zeros((7, d), F32)], axis=0)

    csh = w_mod.shape[2]
    b_mod_sh = lax.dynamic_slice(b_mod, (0, chip * csh), (1, csh))

    def mod_fwd_body(c_ref, w_ref, b_ref, o_ref):
        a = _silu(c_ref[...]).astype(MXU_DTYPE)
        o_ref[...] = jnp.dot(a, w_ref[...].astype(MXU_DTYPE), preferred_element_type=F32) + b_ref[...]

    mod_sh = pl.pallas_call(mod_fwd_body, name="mod_fwd", out_shape=_sds((16, csh), F32),
                            compiler_params=pltpu.CompilerParams(vmem_limit_bytes=VMEM_LIMIT))(cs16, w_mod[0], b_mod_sh)
    (mod_all,) = _exchange8([mod_sh], "gather_mod", True)
    mod_full = jnp.concatenate([mod_all[2 * j] for j in range(4)], axis=1)
    modv = jnp.stack([lax.dynamic_slice(mod_full, (me, 0), (1, d6)), mod_full[8:9]])

    def mod_parts(m):
        return [m[:, j * d:(j + 1) * d] for j in range(6)]

    u_off, kv_off, kr_off = 2 * d, 2 * d + SSM_WIDTH, 2 * d + SSM_WIDTH + KV_LORA
    q_off = -(-(kr_off + SLOT) // Q_LORA) * Q_LORA
    zw = q_off + Q_LORA
    wi = full["w_in"]
    s0, s1, s2, s3 = Q_LORA, Q_LORA + KV_LORA, Q_LORA + KV_LORA + QK_ROPE, Q_LORA + KV_LORA + QK_ROPE + SSM_WIDTH
    zpad = lambda w_: jnp.zeros((d, w_), MXU_DTYPE)
    win_p = jnp.concatenate([wi[:, s3:], wi[:, s2:s3], wi[:, s0:s1], wi[:, s1:s2], zpad(SLOT - QK_ROPE),
                             zpad(q_off - kr_off - SLOT), wi[:, :s0]], axis=1)
    wuq_p = jnp.pad(full["w_uq"].reshape(Q_LORA, N_HEADS, QK_DIM), ((0, 0), (0, 0), (0, SLOT - QK_DIM))).reshape(Q_LORA, N_HEADS * SLOT)
    wukv3 = full["w_ukv"].reshape(KV_LORA, N_HEADS, QK_NOPE + V_DIM)
    padh = lambda t: jnp.pad(t, ((0, 0), (0, 0), (0, SLOT - t.shape[2]))).reshape(t.shape[0], N_HEADS * SLOT)
    wukv_p = jnp.concatenate([padh(wukv3[:, :, :QK_NOPE]), padh(wukv3[:, :, QK_NOPE:])], axis=1)
    wo_p = jnp.pad(full["w_o_attn"].reshape(N_HEADS, V_DIM, d), ((0, 0), (0, SLOT - V_DIM), (0, 0))).reshape(N_HEADS * SLOT, d)
    wglu, wout, wup, wdown = full["w_glu"], full["w_out"], full["w_up"], full["w_down"]
    hw = N_HEADS * SLOT
    gain_p = lambda g_: jnp.tile(jnp.pad(g_[0], (0, SLOT - QK_DIM)), N_HEADS)[None, :]
    qg_p, kg_p = gain_p(q_norm_g), gain_p(k_norm_g)

    tok = jnp.arange(nl)
    freqs = ROPE_THETA ** (-jnp.arange(QK_ROPE // 4, dtype=F32) / (QK_ROPE // 4))
    ang = jnp.concatenate([(tok // GRID_W)[:, None] * freqs, (tok % GRID_W)[:, None] * freqs], axis=-1)
    cos_t = jnp.concatenate([jnp.cos(ang), jnp.ones((nc, 16), F32)], axis=0)
    sin_t = jnp.concatenate([jnp.sin(ang), jnp.zeros((nc, 16), F32)], axis=0)
    zl = lambda w_: jnp.zeros((n, w_), F32)
    rope_c = jnp.concatenate([jnp.ones((n, QK_NOPE), F32), cos_t, cos_t, zl(SLOT - QK_DIM)], axis=1)
    rope_sa = jnp.concatenate([zl(QK_NOPE), -sin_t, zl(SLOT - QK_NOPE - 16)], axis=1)
    rope_sb = jnp.concatenate([zl(QK_NOPE + 16), sin_t, zl(SLOT - QK_DIM)], axis=1)

    dirs = (("f", lam_re_f, lam_im_f, log_dt_f, c_re_f, c_im_f, False), ("b", lam_re_b, lam_im_b, log_dt_b, c_re_b, c_im_b, True))
    bbig, cbig_t, cbig_n, bbig_t, lamc, lamc_adj, disc_vjps = [], [], [], [], [], [], []
    for _, l_re, l_im, l_dt, cr_, ci_, rev_ in dirs:
        (lbr, lbi, bbr, bbi), vjp = jax.vjp(_s5_disc, l_re[0], l_im[0], l_dt[0], b_re[0], b_im[0])
        disc_vjps.append(vjp)
        bb = _to_cols(_block_diag(bbr), _block_diag(bbi)).astype(MXU_DTYPE)
        cc_ = _to_cols(_block_diag(jnp.transpose(cr_[0], (0, 2, 1))),
                       -_block_diag(jnp.transpose(ci_[0], (0, 2, 1)))).astype(MXU_DTYPE)
        bbig.append(bb)
        bbig_t.append(jnp.transpose(bb))
        cbig_t.append(cc_)
        cbig_n.append(jnp.transpose(cc_))
        lamc.append(_lam_consts(lbr, lbi, rev_, False))
        lamc_adj.append(_lam_consts(lbr, lbi, not rev_, True))
    t_scan = tr

    xa = jnp.concatenate([x[0], ctx[0]], axis=0)
    n1g, n2g = norm1_g, norm2_g

    def norm1_body(xv, m, g):
        sh1, sc1 = m[:, :d], m[:, d:2 * d]
        return _rms_fwd(xv, g, d) * (1.0 + sc1) + sh1

    (h1,) = _rowwise(norm1_body, name="norm1_fwd", nblk=nb, tr=tr, rows=[(xa, 0, d, 0)], sels=[modv], fulls=[n1g],
                     outs=[(d, MXU_DTYPE)], seg=nlb)
    z = _mm(h1, win_p, "nn", "in_proj")
    gl_cb, u_cb, kv_cb, kr_cb, q_cb = 0, u_off // SSM_WIDTH, kv_off // KV_LORA, kr_off // SLOT, q_off // Q_LORA

    (cqn,) = _rowwise(lambda v, g: _rms_fwd(v, g, Q_LORA), name="qa_norm_fwd", nblk=nlb, tr=tr,
                      rows=[(z, q_cb, Q_LORA, 0)], fulls=[q_a_g], outs=[(Q_LORA, MXU_DTYPE)])
    qh = _mm(cqn, wuq_p, "nn", "q_up")

    def qhead_body(qv, cv, sav, sbv, g):
        return jnp.concatenate([_rope_fwd(_rms_fwd(t, g[:, :SLOT], QK_DIM), cv, sav, sbv) for t in _heads(qv)], axis=1)

    rope_rows = lambda: [(rope_c, 0, SLOT, 0), (rope_sa, 0, SLOT, 0), (rope_sb, 0, SLOT, 0)]
    (q_p,) = _rowwise(qhead_body, name="q_head_fwd", nblk=nlb, tr=tr, rows=[(qh, 0, hw, 0)] + rope_rows(),
                      fulls=[qg_p], outs=[(hw, MXU_DTYPE)])

    (ckvn,) = _rowwise(lambda v, g: _rms_fwd(v, g, KV_LORA), name="kva_norm_fwd", nblk=nb, tr=tr,
                       rows=[(z, kv_cb, KV_LORA, 0)], fulls=[kv_a_g], outs=[(KV_LORA, MXU_DTYPE)])
    kvpre = _mm(ckvn, wukv_p, "nn", "kv_up")

    def khead_body(kv_, vv_, krv, cv, sav, sbv, g):
        kpe = pltpu.roll(krv, QK_NOPE, 1)
        ks = [_rope_fwd(_rms_fwd(t + kpe, g[:, :SLOT], QK_DIM), cv, sav, sbv) for t in _heads(kv_)]
        return jnp.concatenate(ks, axis=1), vv_

    k_p, v_p = _rowwise(khead_body, name="k_head_fwd", nblk=nb, tr=tr,
                        rows=[(kvpre, 0, hw, 0), (kvpre, 1, hw, 0), (z, kr_cb, SLOT, 0)] + rope_rows(),
                        fulls=[kg_p], outs=[(hw, MXU_DTYPE), (hw, MXU_DTYPE)])

    scale = QK_DIM ** -0.5
    o_p, lse = _attn_fwd(q_p, k_p, v_p, nl, scale)
    a_l = _mm(o_p, wo_p, "nn", "attn_out")

    xs = [_s5_scan(z, u_cb, bbig[j], lamc[j], t_scan, nl, dirs[j][6], "s5_scan_" + dirs[j][0]) for j in range(2)]
    ydir = [_mm(xs[j], cbig_n[j], "nn", "s5_read_" + dirs[j][0], rows=nl) for j in range(2)]

    def ssm_out_body(uv, a, b, dsk):
        ys = uv * dsk + a + b
        return ys, _gelu(ys)

    ys, ge = _rowwise(ssm_out_body, name="s5_out_fwd", nblk=nlb, tr=tr,
                      rows=[(z, u_cb, SSM_WIDTH, 0), (ydir[0], 0, SSM_WIDTH, 0), (ydir[1], 0, SSM_WIDTH, 0)],
                      fulls=[d_skip], outs=[(SSM_WIDTH, F32), (SSM_WIDTH, MXU_DTYPE)])
    glu_out = _mm(ge, wglu, "nn", "glu_proj")

    def merge_body(ga, gs, av, val, gate):
        return _sigmoid(ga) * av + _sigmoid(gs) * (val * _sigmoid(gate))

    merge_rows = lambda: [(z, 0, d, 0), (z, 1, d, 0), (a_l, 0, d, 0), (glu_out, 0, d, 0), (glu_out, 1, d, 0)]
    (merged,) = _rowwise(merge_body, name="merge_fwd", nblk=nlb, tr=tr, rows=merge_rows(), outs=[(d, MXU_DTYPE)])
    mo = _mm(merged, wout, "nn", "out_proj")
    mod_x = modv[0]

    def norm2_body(xv, mov, m, g):
        g1, sh2, sc2 = m[:, 2 * d:3 * d], m[:, 3 * d:4 * d], m[:, 4 * d:5 * d]
        x1v = xv + g1 * mov
        return x1v, _rms_fwd(x1v, g, d) * (1.0 + sc2) + sh2

    x1, h2 = _rowwise(norm2_body, name="norm2_fwd", nblk=nlb, tr=tr, rows=[(xa, 0, d, 0), (mo, 0, d, 0)],
                      fulls=[mod_x, n2g], outs=[(d, F32), (d, MXU_DTYPE)])
    up = _mm(h2, wup, "nn", "ffn_up")
    cw8 = jnp.zeros((8, f2), F32).at[:3].set(conv_w_full)

    def conv3(t3, w8, off):
        p_, c_, n_ = t3
        return p_ * w8[0:1, off:off + fh] + c_ * w8[1:2, off:off + fh] + n_ * w8[2:3, off:off + fh]

    def conv_fwd_body(val3, gate3, w8, bias):
        val2 = conv3(val3, w8, 0) + bias[:, :fh]
        gate2 = conv3(gate3, w8, fh) + bias[:, fh:]
        return _silu(gate2) * val2

    (act,) = _rowwise(conv_fwd_body, name="conv_fwd", nblk=nlb, tr=tr, rows=[(up, 0, fh, 0), (up, 1, fh, 0)],
                      halo=(0, 1), fulls=[cw8, conv_b], outs=[(fh, MXU_DTYPE)])
    dn = _mm(act, wdown, "nn", "ffn_down")
    tgt = loss_target[0]

    def loss_body(x1v, dnv, tv, m):
        g2 = m[:, 5 * d:6 * d]
        e = x1v + g2 * dnv - tv
        dx2v = e * (1.0 / d)
        return dx2v, dx2v * g2, e * e, dx2v * dnv

    dx2, ddn, loss_acc, dg2_acc = _rowwise(loss_body, name="loss", nblk=nlb, tr=tr,
                                           rows=[(x1, 0, d, 0), (dn, 0, d, 0), (tgt, 0, d, 0)], fulls=[mod_x],
                                           outs=[(d, F32), (d, MXU_DTYPE)], accs=[d, d])
    loss = lax.psum(0.5 / d * jnp.sum(loss_acc), ("x", "y", "c"))

    g_big = {}
    dact = _mm(ddn, wdown, "nt", "ffn_down_dx")
    g_big["w_down"] = _mm(act, ddn, "tn", "ffn_down_dw")

    def conv_bwd_body(val3, gate3, da, w8, bias):
        val2 = conv3(val3, w8, 0) + bias[:, :fh]
        gate2 = conv3(gate3, w8, fh) + bias[:, fh:]
        dval2 = da * _silu(gate2)
        dgate2 = da * val2 * _dsilu(gate2)
        du2 = jnp.concatenate([dval2, dgate2], axis=1)
        taps = [jnp.concatenate([dval2 * val3[j], dgate2 * gate3[j]], axis=1) for j in range(3)]
        return du2, du2, taps[0], taps[1], taps[2]

    du2, dcb_acc, dcw0, dcw1, dcw2 = _rowwise(conv_bwd_body, name="conv_bwd", nblk=nlb, tr=tr,
                                              rows=[(up, 0, fh, 0), (up, 1, fh, 0), (dact, 0, fh, 0)], halo=(0, 1),
                                              fulls=[cw8, conv_b], outs=[(f2, F32)], accs=[f2, f2, f2, f2])

    def conv_t_body(dval3, dgate3, w8):
        rev = lambda t3: (t3[2], t3[1], t3[0])
        return jnp.concatenate([conv3(rev(dval3), w8, 0), conv3(rev(dgate3), w8, fh)], axis=1)

    (dup,) = _rowwise(conv_t_body, name="conv_bwd_dx", nblk=nlb, tr=tr, rows=[(du2, 0, fh, 0), (du2, 1, fh, 0)],
                      halo=(0, 1), fulls=[cw8], outs=[(f2, MXU_DTYPE)])
    dh2 = _mm(dup, wup, "nt", "ffn_up_dx")
    g_big["w_up"] = _mm(h2, dup, "tn", "ffn_up_dw")

    def norm2_bwd_body(x1v, dh, dx2v, mov, m, g):
        g1, sc2 = m[:, 2 * d:3 * d], m[:, 4 * d:5 * d]
        y = _rms_fwd(x1v, g, d)
        dxn, dgc = _rms_bwd(x1v, g, dh * (1.0 + sc2), d)
        dx1v = dx2v + dxn
        return dx1v, dx1v * g1, dgc, dh, dh * y, dx1v * mov

    dx1, dmo, dn2g_acc, dsh2_acc, dsc2_acc, dg1_acc = _rowwise(
        norm2_bwd_body, name="norm2_bwd", nblk=nlb, tr=tr,
        rows=[(x1, 0, d, 0), (dh2, 0, d, 0), (dx2, 0, d, 0), (mo, 0, d, 0)], fulls=[mod_x, n2g],
        outs=[(d, F32), (d, MXU_DTYPE)], accs=[d, d, d, d])
    dmerged = _mm(dmo, wout, "nt", "out_proj_dx")
    g_big["w_out"] = _mm(merged, dmo, "tn", "out_proj_dw")

    def merge_bwd_body(ga, gs, av, val, gate, dm):
        sa_, ss_, sg_ = _sigmoid(ga), _sigmoid(gs), _sigmoid(gate)
        s_l = val * sg_
        ds_l = dm * ss_
        dga = dm * av * sa_ * (1.0 - sa_)
        dgs = dm * s_l * ss_ * (1.0 - ss_)
        dval = ds_l * sg_
        dgate = ds_l * val * sg_ * (1.0 - sg_)
        return dm * sa_, jnp.concatenate([dval, dgate], axis=1), jnp.concatenate([dga, dgs], axis=1)

    da_l, dglu, dgl = _rowwise(merge_bwd_body, name="merge_bwd", nblk=nlb, tr=tr,
                               rows=merge_rows() + [(dmerged, 0, d, 0)],
                               outs=[(d, MXU_DTYPE), (2 * d, MXU_DTYPE), (2 * d, MXU_DTYPE)])
    dge = _mm(dglu, wglu, "nt", "glu_proj_dx")
    g_big["w_glu"] = _mm(ge, dglu, "tn", "glu_proj_dw")

    def ssm_out_bwd_body(ysv, dgev, uv, dsk):
        dys_ = dgev * _dgelu(ysv)
        return dys_, dys_ * dsk, dys_ * uv

    dys, du_skip, ddskip_acc = _rowwise(ssm_out_bwd_body, name="s5_out_bwd", nblk=nlb, tr=tr,
                                        rows=[(ys, 0, SSM_WIDTH, 0), (dge, 0, SSM_WIDTH, 0), (z, u_cb, SSM_WIDTH, 0)],
                                        fulls=[d_skip], outs=[(SSM_WIDTH, F32), (SSM_WIDTH, F32)], accs=[SSM_WIDTH])
    s5b = [_s5_bwd(dys, z, u_cb, xs[j], cbig_t[j], bbig_t[j], lamc_adj[j], t_scan, nl, dirs[j][6], "s5_bwd_" + dirs[j][0])
           for j in range(2)]
    dcbig = [_mm(xs[j], dys, "tn", "s5_read_dw_" + dirs[j][0], rows=nl) for j in range(2)]
    du_nat = jnp.sum(s5b[0][0], axis=0) + jnp.sum(s5b[1][0], axis=0)
    du_nat = du_nat + jnp.concatenate([du_skip, jnp.zeros((nc, SSM_WIDTH), F32)], axis=0)

    do_f = _mm(da_l, wo_p, "nt", "attn_out_dx")
    g_wo_p = _mm(o_p, da_l, "tn", "attn_out_dw")

    def delta_body(dov, ov):
        prod = dov * ov.astype(F32)
        dl = [jnp.broadcast_to(jnp.sum(t, axis=-1, keepdims=True), t.shape) for t in _heads(prod)]
        return dov, jnp.concatenate(dl, axis=1)

    do_b, delta = _rowwise(delta_body, name="attn_delta", nblk=nlb, tr=tr, rows=[(do_f, 0, hw, 0), (o_p, 0, hw, 0)],
                           outs=[(hw, MXU_DTYPE), (hw, F32)])
    to_rows = lambda t: jnp.broadcast_to(jnp.transpose(t[:, ::SLOT])[:, None, :], (N_HEADS, 8, nl))
    dq_t, dk_p, dv_p = _attn_bwd(q_p, k_p, jnp.transpose(k_p), v_p, do_b, to_rows(lse), to_rows(delta), nl, scale)
    dq_p = jnp.transpose(dq_t)

    def qhead_bwd_body(qv, dqv, cv, sav, sbv, g):
        dxs, dgs = [], []
        for t, dt_ in zip(_heads(qv), _heads(dqv)):
            dx_, dg_ = _rms_bwd(t, g[:, :SLOT], _rope_bwd(dt_, cv, sav, sbv), QK_DIM)
            dxs.append(dx_)
            dgs.append(dg_)
        return jnp.concatenate(dxs, axis=1), jnp.concatenate(dgs, axis=1)

    dqh, dqg_acc = _rowwise(qhead_bwd_body, name="q_head_bwd", nblk=nlb, tr=tr,
                            rows=[(qh, 0, hw, 0), (dq_p, 0, hw, 0)] + rope_rows(), fulls=[qg_p],
                            outs=[(hw, MXU_DTYPE)], accs=[hw])
    dcqn = _mm(dqh, wuq_p, "nt", "q_up_dx")
    g_wuq_p = _mm(cqn, dqh, "tn", "q_up_dw")
    dcq, dqag_acc = _rowwise(lambda v, dy, g: _rms_bwd(v, g, dy, Q_LORA), name="qa_norm_bwd", nblk=nlb, tr=tr,
                             rows=[(z, q_cb, Q_LORA, 0), (dcqn, 0, Q_LORA, 0)], fulls=[q_a_g],
                             outs=[(Q_LORA, MXU_DTYPE)], accs=[Q_LORA])

    def khead_bwd_body(kv_, krv, dkv_, dvv_, cv, sav, sbv, g):
        kpe = pltpu.roll(krv, QK_NOPE, 1)
        lane = lax.broadcasted_iota(jnp.int32, krv.shape, 1)
        dxs, dgs, dkr_ = [], [], jnp.zeros(krv.shape, F32)
        for t, dt_ in zip(_heads(kv_), _heads(dkv_)):
            dx_, dg_ = _rms_bwd(t + kpe, g[:, :SLOT], _rope_bwd(dt_, cv, sav, sbv), QK_DIM)
            dxs.append(jnp.where(lane < QK_NOPE, dx_, 0.0))
            dgs.append(dg_)
            dkr_ = dkr_ + dx_
        dkr_ = jnp.where(lane < QK_ROPE, pltpu.roll(dkr_, SLOT - QK_NOPE, 1), 0.0)
        return jnp.concatenate(dxs + [dvv_], axis=1), dkr_, jnp.concatenate(dgs, axis=1)

    dkvpre, dkr, dkg_acc = _rowwise(khead_bwd_body, name="k_head_bwd", nblk=nb, tr=tr,
                                    rows=[(kvpre, 0, hw, 0), (z, kr_cb, SLOT, 0), (dk_p, 0, hw, 0), (dv_p, 0, hw, 0)] + rope_rows(),
                                    fulls=[kg_p], outs=[(2 * hw, MXU_DTYPE), (SLOT, MXU_DTYPE)], accs=[hw])
    dckvn = _mm(dkvpre, wukv_p, "nt", "kv_up_dx")
    g_wukv_p = _mm(ckvn, dkvpre, "tn", "kv_up_dw")
    dckv, dkvag_acc = _rowwise(lambda v, dy, g: _rms_bwd(v, g, dy, KV_LORA), name="kva_norm_bwd", nblk=nb, tr=tr,
                               rows=[(z, kv_cb, KV_LORA, 0), (dckvn, 0, KV_LORA, 0)], fulls=[kv_a_g],
                               outs=[(KV_LORA, MXU_DTYPE)], accs=[KV_LORA])

    padc = lambda t: jnp.concatenate([t, jnp.zeros((nc, t.shape[1]), t.dtype)], axis=0)
    dz = jnp.concatenate([padc(dgl), du_nat.astype(MXU_DTYPE), dckv, dkr,
                          jnp.zeros((n, q_off - kr_off - SLOT), MXU_DTYPE), padc(dcq)], axis=1)
    dh1 = _mm(dz, win_p, "nt", "in_proj_dx")
    g_win_p = _mm(h1, dz, "tn", "in_proj_dw")

    def norm1_bwd_body(xv, dh, m, g):
        sc1 = m[:, d:2 * d]
        y = _rms_fwd(xv, g, d)
        dxn, dgc = _rms_bwd(xv, g, dh * (1.0 + sc1), d)
        return dxn, dgc, dh, dh * y

    dxa, dn1g_acc, dsh1_acc, dsc1_acc = _rowwise(norm1_bwd_body, name="norm1_bwd", nblk=nb, tr=tr,
                                                 rows=[(xa, 0, d, 0), (dh1, 0, d, 0)], sels=[modv], fulls=[n1g],
                                                 outs=[(d, F32)], accs=[d, d, d], seg=nlb)
    grad_x = (dxa[:nl] + dx1)[None]

    red8 = lambda a: jnp.sum(a, axis=-2)
    dmod_own = jnp.concatenate([red8(dsh1_acc[0]), red8(dsc1_acc[0]), red8(dg1_acc), red8(dsh2_acc), red8(dsc2_acc), red8(dg2_acc)])
    dmod_ctx = jnp.concatenate([red8(dsh1_acc[1]), red8(dsc1_acc[1]), jnp.zeros((4 * d,), F32)])
    dm_in = jnp.concatenate([dmod_own[None, :], dmod_ctx[None, :], jnp.zeros((6, d6), F32)], axis=0)
    (dm_all,) = _exchange8([dm_in], "gather_dmod", True)
    dm_own_sh = lax.dynamic_slice(dm_all[:, 0, :], (0, chip * csh), (8, csh))
    dm_ctx_sh = lax.dynamic_slice(dm_all[:, 1, :], (0, chip * csh), (8, csh))

    def mod_bwd_body(c_ref, own_ref, ctx_ref, w_ref, gw_ref, gb_ref, gc_ref):
        cv = c_ref[...]
        a = _silu(cv).astype(MXU_DTYPE)
        own = own_ref[...]
        ctx_tot = ctx_ref[0:1, :]
        for j in range(1, 8):
            ctx_tot = ctx_tot + ctx_ref[j:j + 1, :]
        g16 = jnp.concatenate([own, jnp.broadcast_to(ctx_tot, own.shape)], axis=0)
        rid = lax.broadcasted_iota(jnp.int32, g16.shape, 0)
        g16 = jnp.where(rid <= 8, g16, 0.0)
        gw_ref[...] = lax.dot_general(a, g16.astype(MXU_DTYPE), (((0,), (0,)), ((), ())), preferred_element_type=F32)
        gb_ref[...] = jnp.broadcast_to(jnp.sum(own, axis=0, keepdims=True) + ctx_tot, gb_ref.shape)
        gc = lax.dot_general(jnp.broadcast_to(ctx_tot, own.shape).astype(MXU_DTYPE), w_ref[...].astype(MXU_DTYPE),
                             (((1,), (1,)), ((), ())), preferred_element_type=F32)
        gc_ref[...] = gc * _dsilu(cv[8:9, :])

    g_wmod, g_bmod_sh, g_cctx_part = pl.pallas_call(
        mod_bwd_body, name="mod_bwd", out_shape=[_sds((d, csh), F32), _sds((8, csh), F32), _sds((8, d), F32)],
        compiler_params=pltpu.CompilerParams(vmem_limit_bytes=VMEM_LIMIT))(cs16, dm_own_sh, dm_ctx_sh, w_mod[0])
    north = (mc == 0).astype(F32)
    g_bmod_part = lax.dynamic_update_slice(jnp.zeros((1, d6), F32), g_bmod_sh[0:1] * north, (0, chip * csh))
    g_cctx_part = g_cctx_part[0] * north

    small_g = {}
    for j, dr in enumerate(dirs):
        sfx = dr[0]
        _, dbbig_j, dlam_j = s5b[j]
        dl_re, dl_im = _from_cols(red8(dlam_j)[None, :])
        db_re, db_im = _from_cols(dbbig_j)
        cot = (dl_re.reshape(SSM_GROUPS, SSM_STATE), dl_im.reshape(SSM_GROUPS, SSM_STATE),
               _block_diag_extract(db_re), _block_diag_extract(db_im))
        g_lre, g_lim, g_ldt, g_bre, g_bim = disc_vjps[j](cot)
        small_g["lam_re_" + sfx], small_g["lam_im_" + sfx], small_g["log_dt_" + sfx] = g_lre, g_lim, g_ldt
        small_g["b_re"] = small_g.get("b_re", 0.0) + g_bre
        small_g["b_im"] = small_g.get("b_im", 0.0) + g_bim
        dc_re, dc_im = _from_cols(jnp.transpose(dcbig[j]))
        small_g["c_re_" + sfx] = jnp.transpose(_block_diag_extract(dc_re), (0, 2, 1))
        small_g["c_im_" + sfx] = -jnp.transpose(_block_diag_extract(dc_im), (0, 2, 1))
    head_fold = lambda acc: jnp.sum(red8(acc).reshape(N_HEADS, SLOT), axis=0)[:QK_DIM]
    small_g.update(c_ctx=g_cctx_part, b_mod=g_bmod_part[0], norm1_g=red8(dn1g_acc[0]) + red8(dn1g_acc[1]),
                   norm2_g=red8(dn2g_acc), q_a_g=red8(dqag_acc), kv_a_g=red8(dkvag_acc), q_norm_g=head_fold(dqg_acc),
                   k_norm_g=head_fold(dkg_acc), d_skip=red8(ddskip_acc), conv_b=red8(dcb_acc))
    g_convw_full = jnp.stack([red8(dcw0), red8(dcw1), red8(dcw2)])
    small_names = ["c_ctx", "b_mod", "norm1_g", "norm2_g", "q_a_g", "kv_a_g", "q_norm_g", "k_norm_g",
                   "lam_re_f", "lam_im_f", "log_dt_f", "c_re_f", "c_im_f", "lam_re_b", "lam_im_b", "log_dt_b",
                   "c_re_b", "c_im_b", "b_re", "b_im", "d_skip", "conv_b"]
    small_shapes = [weights[k].shape for k in small_names]
    spack = _pack([small_g[k] for k in small_names] + [g_convw_full], rows_mult=8)
    sred = _sum8(_exchange8([spack], "gather_small_grads", True)[0], "sum_small_grads")
    sg_list = _unpack(sred, small_shapes + [(3, f2)])
    g_small = dict(zip(small_names, sg_list[:-1]))
    g_small["conv_w"] = lax.dynamic_slice(sg_list[-1], (0, chip * cwid), (3, cwid))[None]

    gwi = g_win_p
    g_big["w_in"] = jnp.concatenate([gwi[:, q_off:q_off + Q_LORA], gwi[:, kv_off:kv_off + KV_LORA],
                                     gwi[:, kr_off:kr_off + QK_ROPE], gwi[:, u_off:u_off + SSM_WIDTH], gwi[:, :2 * d]], axis=1)
    g_big["w_uq"] = g_wuq_p.reshape(Q_LORA, N_HEADS, SLOT)[:, :, :QK_DIM].reshape(Q_LORA, N_HEADS * QK_DIM)
    gk3 = g_wukv_p[:, :hw].reshape(KV_LORA, N_HEADS, SLOT)[:, :, :QK_NOPE]
    gv3 = g_wukv_p[:, hw:].reshape(KV_LORA, N_HEADS, SLOT)[:, :, :V_DIM]
    g_big["w_ukv"] = jnp.concatenate([gk3, gv3], axis=2).reshape(KV_LORA, N_HEADS * (QK_NOPE + V_DIM))
    g_big["w_o_attn"] = g_wo_p.reshape(N_HEADS, SLOT, d)[:, :V_DIM].reshape(N_HEADS * V_DIM, d)

    def pieces(k_):
        r_, c_ = weights[k_].shape[1:]
        if k_ in row_sharded:
            return g_big[k_].reshape(8, r_ // 2, c_)
        return jnp.transpose(g_big[k_].reshape(2, r_ // 2, 4, c_), (2, 0, 1, 3)).reshape(8, r_ // 2, c_)

    recv = _exchange8([pieces(k_) for k_ in big_names], "scatter_weight_grads", False)
    my_half = [_sum8(rc, "sum_grad_" + k_) for k_, rc in zip(big_names, recv)]
    both = _sibling_exchange(my_half, "exchange_halves")
    g_sh = {k_: b_.reshape((1,) + weights[k_].shape[1:]) for k_, b_ in zip(big_names, both)}
    g_sh["w_mod"] = g_wmod[None]

    grads = {**g_sh, **g_small}
    outs_d, outs_m, outs_v = {}, {}, {}
    for k_ in ["w_mod"] + big_names:
        shp = weights[k_].shape
        res = _adamw(*[t.reshape(shp[1:]) for t in (grads[k_], weights[k_], mom_m[k_], mom_v[k_])], "adamw_" + k_)
        for dst, buf in zip((outs_d, outs_m, outs_v), res):
            dst[k_] = buf.reshape(shp)
    adam_small = small_names + ["conv_w"]
    shapes = [weights[k_].shape for k_ in adam_small]
    res = _adamw(*[_pack([src[k_] for k_ in adam_small], rows_mult=8) for src in (grads, weights, mom_m, mom_v)], "adamw_small")
    for dst, buf in zip((outs_d, outs_m, outs_v), res):
        dst.update(zip(adam_small, _unpack(buf, shapes)))
    grads = {k_: grads[k_].reshape(weights[k_].shape) for k_ in names}
    return (loss, grad_x, *[grads[k_] for k_ in names], *[outs_d[k_] for k_ in names],
            *[outs_m[k_] for k_ in names], *[outs_v[k_] for k_ in names])
```

```python
import functools
import math

import numpy as np
import jax
import jax.numpy as jnp
from jax import lax
from jax.experimental import pallas as pl
from jax.experimental.pallas import tpu as pltpu

F32 = jnp.float32
MXU_DTYPE = jnp.bfloat16
MESH = pl.DeviceIdType.MESH

EPS = 1e-6
N_HEADS = 8
QK_NOPE = 64
QK_ROPE = 32
QK_DIM = QK_NOPE + QK_ROPE
V_DIM = 64
SLOT = 128
Q_LORA = 384
KV_LORA = 256
GRID_W = 64
ROPE_THETA = 10000.0
SSM_WIDTH = 512
SSM_GROUP = 16
SSM_GROUPS = 32
SSM_STATE = 64
N_STATE = SSM_GROUPS * SSM_STATE
CG_STATES = 512
N_CG = N_STATE // CG_STATES
SCAN_LANES = 256
PACK_W = 1024

ADAM_LR = 0.001
ADAM_B1 = 0.9
ADAM_B2 = 0.999
ADAM_EPS = 1e-08
ADAM_WD = 0.01
ADAM_STEP = 10

VMEM_LIMIT = 56 * 1024 * 1024
LOG2E = 1.4426950408889634


def _pick(n, cands):
    for c in cands:
        if c <= n and n % c == 0:
            return c
    return n


def _cparams(sem):
    return pltpu.CompilerParams(dimension_semantics=sem, vmem_limit_bytes=VMEM_LIMIT)


def _sds(shape, dtype):
    return jax.ShapeDtypeStruct(tuple(shape), dtype)


_K_CANDS = (2816, 2048, 1536, 1408, 1280, 1152, 1024, 896, 768, 704, 640, 512, 384, 256, 128, 64, 32, 16)
_M_CANDS = (2048, 1408, 1024, 768, 512, 384, 256, 128, 64, 32, 16)
_N_CANDS = (1408, 1152, 1024, 768, 512, 384, 256, 128)
MM_VMEM_BUDGET = 40 * 1024 * 1024


def _mm_tiles(m, n, k_opts, a_bytes, b_bytes, o_bytes, m_cands):
    tn = n if n <= _N_CANDS[0] else _pick(n, _N_CANDS)
    for tk in k_opts:
        for tm in ((m,) if m <= m_cands[0] else ()) + tuple(t for t in m_cands if t < m and m % t == 0):
            if 2 * (tm * tk * a_bytes + tk * tn * b_bytes + tm * tn * o_bytes) + tm * tn * 4 <= MM_VMEM_BUDGET:
                return tm, tn, tk
    raise ValueError("no matmul tiling fits")


def _mm(a, b, mode, name, out_dtype=F32, rows=None, a_off=0, b_off=0):
    a_bytes, b_bytes, o_bytes = a.dtype.itemsize, b.dtype.itemsize, jnp.dtype(out_dtype).itemsize
    if mode == "tn":
        t_rows = rows or a.shape[0]
        m, n = a.shape[1], b.shape[1]
        k_opts = tuple(t for t in _K_CANDS if t <= t_rows and t_rows % t == 0) or (t_rows,)
        tm, tn, tk = _mm_tiles(m, n, k_opts, a_bytes, b_bytes, o_bytes, _M_CANDS[1:])
        nk = t_rows // tk
        ao, bo = a_off // tk, b_off // tk
        grid = (m // tm, n // tn, nk)
        in_specs = [pl.BlockSpec((tk, tm), lambda i, j, k: (k + ao, i)),
                    pl.BlockSpec((tk, tn), lambda i, j, k: (k + bo, j))]
        dn = (((0,), (0,)), ((), ()))
    else:
        m = rows or a.shape[0]
        kdim = a.shape[1]
        n = b.shape[1] if mode == "nn" else b.shape[0]
        k_opts = (kdim,) + tuple(t for t in _K_CANDS if t < kdim and kdim % t == 0)
        tm, tn, tk = _mm_tiles(m, n, k_opts, a_bytes, b_bytes, o_bytes, _M_CANDS)
        nk = kdim // tk
        ao = a_off // tm
        grid = (m // tm, n // tn, nk)
        if mode == "nn":
            in_specs = [pl.BlockSpec((tm, tk), lambda i, j, k: (i + ao, k)),
                        pl.BlockSpec((tk, tn), lambda i, j, k: (k, j))]
            dn = (((1,), (0,)), ((), ()))
        else:
            in_specs = [pl.BlockSpec((tm, tk), lambda i, j, k: (i + ao, k)),
                        pl.BlockSpec((tn, tk), lambda i, j, k: (j, k))]
            dn = (((1,), (1,)), ((), ()))
    use_scratch = nk > 1 and out_dtype != F32

    def body(a_ref, b_ref, o_ref, *scr):
        r = lax.dot_general(a_ref[...].astype(MXU_DTYPE), b_ref[...].astype(MXU_DTYPE), dn,
                            preferred_element_type=F32)
        if nk == 1:
            o_ref[...] = r.astype(o_ref.dtype)
        else:
            k = pl.program_id(2)
            acc = scr[0] if use_scratch else o_ref

            @pl.when(k == 0)
            def _():
                acc[...] = r

            @pl.when(k > 0)
            def _():
                acc[...] += r

            if use_scratch:
                @pl.when(k == nk - 1)
                def _():
                    o_ref[...] = acc[...].astype(o_ref.dtype)

    return pl.pallas_call(
        body, name=name, grid=grid, in_specs=in_specs,
        out_specs=pl.BlockSpec((tm, tn), lambda i, j, k: (i, j)),
        out_shape=_sds((m, n), out_dtype),
        scratch_shapes=[pltpu.VMEM((tm, tn), F32)] if use_scratch else [],
        compiler_params=_cparams(("parallel", "parallel", "arbitrary")),
    )(a, b)


def _rowwise(body, *, name, nblk, tr, rows=(), halo=(), sels=(), fulls=(), outs=(), accs=(), seg=None):
    n_rows, n_sel, n_full, n_out, n_acc = len(rows), len(sels), len(fulls), len(outs), len(accs)
    halo = tuple(halo)
    maxw = max([r[2] for r in rows] + [o[0] for o in outs] + list(accs))
    sr = _pick(tr, tuple(s for s in (256, 128, 64, 32, 16) if s * maxw <= 131072) or (16,))
    nsub = tr // sr
    total8 = nblk * tr // 8

    def seg_of(i):
        return jnp.where(i >= seg, 1, 0) if seg is not None else 0

    in_specs, operands = [], []
    for arr, cb, w, roff in rows:
        ob = roff // tr
        in_specs.append(pl.BlockSpec((tr, w), lambda i, cb=cb, ob=ob: (i + ob, cb)))
        operands.append(arr)
    for h in halo:
        arr, cb, w, roff = rows[h]
        o8, t8 = roff // 8, tr // 8
        in_specs.append(pl.BlockSpec((8, w), lambda i, cb=cb, o8=o8, t8=t8: (jnp.maximum(i * t8 - 1, 0) + o8, cb)))
        in_specs.append(pl.BlockSpec((8, w), lambda i, cb=cb, o8=o8, t8=t8: (jnp.minimum((i + 1) * t8, total8 - 1) + o8, cb)))
        operands += [arr, arr]
    for arr in sels:
        in_specs.append(pl.BlockSpec((None,) + arr.shape[1:], lambda i: (seg_of(i), 0, 0)))
        operands.append(arr)
    for arr in fulls:
        in_specs.append(pl.BlockSpec(arr.shape, lambda i: (0, 0)))
        operands.append(arr)
    out_specs, out_shape = [], []
    for w, dt in outs:
        out_specs.append(pl.BlockSpec((tr, w), lambda i: (i, 0)))
        out_shape.append(_sds((nblk * tr, w), dt))
    for w in accs:
        if seg is None:
            out_specs.append(pl.BlockSpec((8, w), lambda i: (0, 0)))
            out_shape.append(_sds((8, w), F32))
        else:
            out_specs.append(pl.BlockSpec((None, 8, w), lambda i: (seg_of(i), 0, 0)))
            out_shape.append(_sds((2, 8, w), F32))
    n_halo = 2 * len(halo)

    def kern(*refs):
        row_refs = refs[:n_rows]
        halo_refs = refs[n_rows:n_rows + n_halo]
        sel_refs = refs[n_rows + n_halo:n_rows + n_halo + n_sel]
        full_refs = refs[n_rows + n_halo + n_sel:n_rows + n_halo + n_sel + n_full]
        o0 = n_rows + n_halo + n_sel + n_full
        out_refs = refs[o0:o0 + n_out]
        acc_refs = refs[o0 + n_out:o0 + n_out + n_acc]
        i = pl.program_id(0)
        if n_acc:
            first = (i == 0) if seg is None else ((i == 0) | (i == seg))

            @pl.when(first)
            def _():
                for a_ref in acc_refs:
                    a_ref[...] = jnp.zeros(a_ref.shape, F32)

        def sub(s, carry):
            r0 = pl.multiple_of(s * sr, sr)
            vals = []
            for idx, r in enumerate(row_refs):
                cur = r[pl.ds(r0, sr), :]
                if idx in halo:
                    hp = halo_refs[2 * halo.index(idx)]
                    hn = halo_refs[2 * halo.index(idx) + 1]
                    cur = cur.astype(F32)
                    rid = lax.broadcasted_iota(jnp.int32, cur.shape, 0)
                    lo = r[pl.ds(pl.multiple_of(jnp.maximum(r0 - 8, 0), 8), 8), :].astype(F32)
                    lo = jnp.where(s == 0, hp[...].astype(F32), lo)
                    lo = jnp.where((s == 0) & (i == 0), 0.0, lo)
                    hi = r[pl.ds(pl.multiple_of(jnp.minimum(r0 + sr, tr - 8), 8), 8), :].astype(F32)
                    hi = jnp.where(s == nsub - 1, hn[...].astype(F32), hi)
                    hi = jnp.where((s == nsub - 1) & (i == nblk - 1), 0.0, hi)
                    prev = jnp.where(rid == 0, jnp.broadcast_to(lo[7:8, :], cur.shape), pltpu.roll(cur, 1, 0))
                    nxt = jnp.where(rid == sr - 1, jnp.broadcast_to(hi[0:1, :], cur.shape), pltpu.roll(cur, sr - 1, 0))
                    vals.append((prev, cur, nxt))
                else:
                    vals.append(cur)
            res = body(*vals, *[r[...] for r in sel_refs], *[r[...] for r in full_refs])
            if not isinstance(res, (tuple, list)):
                res = (res,)
            for o_ref, v in zip(out_refs, res[:n_out]):
                o_ref[pl.ds(r0, sr), :] = v.astype(o_ref.dtype)
            for a_ref, v in zip(acc_refs, res[n_out:]):
                a_ref[...] += jnp.sum(v.astype(F32).reshape(sr // 8, 8, v.shape[-1]), axis=0)
            return carry

        lax.fori_loop(0, nsub, sub, 0)

    res = pl.pallas_call(
        kern, name=name, grid=(nblk,), in_specs=in_specs, out_specs=out_specs, out_shape=out_shape,
        compiler_params=_cparams(("arbitrary",)),
    )(*operands)
    return res


def _sigmoid(x):
    return 1.0 / (1.0 + jnp.exp(-x))


def _silu(x):
    return x * _sigmoid(x)


def _dsilu(x):
    s = _sigmoid(x)
    return s * (1.0 + x * (1.0 - s))


_GELU_K = math.sqrt(2.0 / math.pi)


def _gelu(x):
    return 0.5 * x * (1.0 + jnp.tanh(_GELU_K * (x + 0.044715 * x * x * x)))


def _dgelu(x):
    t = jnp.tanh(_GELU_K * (x + 0.044715 * x * x * x))
    return 0.5 * (1.0 + t) + 0.5 * x * (1.0 - t * t) * _GELU_K * (1.0 + 3.0 * 0.044715 * x * x)


def _rms_fwd(x, g, width):
    r = lax.rsqrt(jnp.sum(x * x, axis=-1, keepdims=True) * (1.0 / width) + EPS)
    return x * r * g


def _rms_bwd(x, g, dy, width):
    r = lax.rsqrt(jnp.sum(x * x, axis=-1, keepdims=True) * (1.0 / width) + EPS)
    xn = x * r
    dyg = dy * g
    dx = r * (dyg - xn * (jnp.sum(dyg * xn, axis=-1, keepdims=True) * (1.0 / width)))
    return dx, dy * xn


def _rope_fwd(y, c, sa, sb):
    return y * c + pltpu.roll(y, SLOT - 16, 1) * sa + pltpu.roll(y, 16, 1) * sb


def _rope_bwd(d, c, sa, sb):
    return d * c + pltpu.roll(d * sa, 16, 1) + pltpu.roll(d * sb, SLOT - 16, 1)


def _heads(v):
    return [v[:, h * SLOT:(h + 1) * SLOT] for h in range(N_HEADS)]


def _attn_fwd(q, k, v, nl, scale):
    n = k.shape[0]
    tq = _pick(nl, (1024, 512, 256, 128))
    tk = _pick(n, (2816, 1408, 1152, 768, 384, 256, 128))
    sub = min(tq, 256)
    nk = n // tk
    rep = tk // SLOT
    c = scale * LOG2E

    def body(q_ref, k_ref, v_ref, o_ref, lse_ref, m_sc, l_sc, acc_sc):
        ki = pl.program_id(2)

        @pl.when(ki == 0)
        def _():
            m_sc[...] = jnp.full(m_sc.shape, -jnp.inf, F32)
            l_sc[...] = jnp.zeros(l_sc.shape, F32)
            acc_sc[...] = jnp.zeros(acc_sc.shape, F32)

        kb, vb = k_ref[...], v_ref[...]
        for sb in range(tq // sub):
            rows = slice(sb * sub, (sb + 1) * sub)
            s = lax.dot_general(q_ref[rows, :], kb, (((1,), (1,)), ((), ())), preferred_element_type=F32)
            m_prev = m_sc[rows, :]
            m_new = jnp.maximum(m_prev, jnp.max(s, axis=1, keepdims=True) * c)
            alpha = jnp.exp2(m_prev - m_new)
            p = jnp.exp2(s * c - jnp.tile(m_new, (1, rep)))
            l_sc[rows, :] = alpha * l_sc[rows, :] + jnp.sum(p, axis=1, keepdims=True)
            acc_sc[rows, :] = alpha * acc_sc[rows, :] + jnp.dot(p.astype(MXU_DTYPE), vb, preferred_element_type=F32)
            m_sc[rows, :] = m_new

        @pl.when(ki == nk - 1)
        def _():
            l = l_sc[...]
            o_ref[...] = (acc_sc[...] / l).astype(o_ref.dtype)
            lse_ref[...] = m_sc[...] + jnp.log2(l)

    return pl.pallas_call(
        body, name="attn_fwd", grid=(N_HEADS, nl // tq, nk),
        in_specs=[pl.BlockSpec((tq, SLOT), lambda h, i, j: (i, h)),
                  pl.BlockSpec((tk, SLOT), lambda h, i, j: (j, h)),
                  pl.BlockSpec((tk, SLOT), lambda h, i, j: (j, h))],
        out_specs=[pl.BlockSpec((tq, SLOT), lambda h, i, j: (i, h)),
                   pl.BlockSpec((tq, SLOT), lambda h, i, j: (i, h))],
        out_shape=[_sds((nl, N_HEADS * SLOT), MXU_DTYPE), _sds((nl, N_HEADS * SLOT), F32)],
        scratch_shapes=[pltpu.VMEM((tq, SLOT), F32), pltpu.VMEM((tq, SLOT), F32), pltpu.VMEM((tq, SLOT), F32)],
        compiler_params=_cparams(("parallel", "parallel", "arbitrary")),
    )(q, k, v)


def _attn_bwd(q, k, kt, v, do, lse_t, delta_t, nl, scale):
    n = k.shape[0]
    tq = _pick(nl, (1024, 512, 256, 128))
    tk = _pick(n, (2816, 1408, 1152, 768, 384, 256, 128))
    sub = _pick(tk, (256, 128))
    nq, nk = nl // tq, n // tk
    c = scale * LOG2E

    def body(q_ref, k_ref, kt_ref, v_ref, do_ref, lse_ref, dl_ref, dq_ref, dk_ref, dv_ref, dk_acc, dv_acc):
        ki, qi = pl.program_id(1), pl.program_id(2)

        @pl.when((ki == 0) & (qi == 0))
        def _():
            dq_ref[...] = jnp.zeros(dq_ref.shape, F32)

        @pl.when(qi == 0)
        def _():
            dk_acc[...] = jnp.zeros(dk_acc.shape, F32)
            dv_acc[...] = jnp.zeros(dv_acc.shape, F32)

        qb, dob = q_ref[...], do_ref[...]
        lse_r, dl_r = lse_ref[0:1, :], dl_ref[0:1, :]
        dq_part = None
        for sb in range(tk // sub):
            rows = slice(sb * sub, (sb + 1) * sub)
            s_t = lax.dot_general(k_ref[rows, :], qb, (((1,), (1,)), ((), ())), preferred_element_type=F32)
            p_t = jnp.exp2(s_t * c - lse_r)
            dp_t = lax.dot_general(v_ref[rows, :], dob, (((1,), (1,)), ((), ())), preferred_element_type=F32)
            ds_t = (p_t * (dp_t - dl_r) * scale).astype(MXU_DTYPE)
            dv_acc[rows, :] += jnp.dot(p_t.astype(MXU_DTYPE), dob, preferred_element_type=F32)
            dk_acc[rows, :] += jnp.dot(ds_t, qb, preferred_element_type=F32)
            part = jnp.dot(kt_ref[:, rows], ds_t, preferred_element_type=F32)
            dq_part = part if dq_part is None else dq_part + part
        c0 = pl.multiple_of(qi * tq, tq)
        dq_ref[:, pl.ds(c0, tq)] += dq_part

        @pl.when(qi == nq - 1)
        def _():
            dk_ref[...] = dk_acc[...]
            dv_ref[...] = dv_acc[...]

    return pl.pallas_call(
        body, name="attn_bwd", grid=(N_HEADS, nk, nq),
        in_specs=[pl.BlockSpec((tq, SLOT), lambda h, j, i: (i, h)),
                  pl.BlockSpec((tk, SLOT), lambda h, j, i: (j, h)),
                  pl.BlockSpec((SLOT, tk), lambda h, j, i: (h, j)),
                  pl.BlockSpec((tk, SLOT), lambda h, j, i: (j, h)),
                  pl.BlockSpec((tq, SLOT), lambda h, j, i: (i, h)),
                  pl.BlockSpec((None, 8, tq), lambda h, j, i: (h, 0, i)),
                  pl.BlockSpec((None, 8, tq), lambda h, j, i: (h, 0, i))],
        out_specs=[pl.BlockSpec((SLOT, nl), lambda h, j, i: (h, 0)),
                   pl.BlockSpec((tk, SLOT), lambda h, j, i: (j, h)),
                   pl.BlockSpec((tk, SLOT), lambda h, j, i: (j, h))],
        out_shape=[_sds((N_HEADS * SLOT, nl), F32), _sds((n, N_HEADS * SLOT), F32), _sds((n, N_HEADS * SLOT), F32)],
        scratch_shapes=[pltpu.VMEM((tk, SLOT), F32), pltpu.VMEM((tk, SLOT), F32)],
        compiler_params=_cparams(("arbitrary", "arbitrary", "arbitrary")),
    )(q, k, kt, v, do, lse_t, delta_t)


def _scan_consts(c_ref, lg):
    cs = slice(lg * SCAN_LANES, (lg + 1) * SCAN_LANES)
    return [c_ref[8 * kk:8 * kk + 8, cs] for kk in range(8)]


def _tile_scan(br, bi, consts, reverse):
    p1r, p1i, p2r, p2i, p4r, p4i = consts[:6]
    for pr, pi, kk in ((p1r, p1i, 1), (p2r, p2i, 2), (p4r, p4i, 4)):
        sh = (8 - kk) if reverse else kk
        sr_, si_ = pltpu.roll(br, sh, 0), pltpu.roll(bi, sh, 0)
        br, bi = br + pr * sr_ - pi * si_, bi + pr * si_ + pi * sr_
    return br, bi


def _seq_chunk(j, nch, nlc, reverse):
    return (nch - 1 - j) if reverse else (j + nlc) % nch


def _s5_scan(z, u_cb, bbig, lamc, t_rows, nl, reverse, name):
    n = z.shape[0]
    nch, nlc = n // t_rows, nl // t_rows
    ntile = t_rows // 8
    w = SCAN_LANES
    edge = 0 if reverse else 7

    def chunk(j):
        return _seq_chunk(j, nch, nlc, reverse)

    def body(u_ref, b_ref, c_ref, xs_ref, carry):
        j = pl.program_id(1)

        @pl.when(j == 0)
        def _():
            carry[...] = jnp.zeros(carry.shape, F32)

        xs_ref[...] = jnp.dot(u_ref[...].astype(MXU_DTYPE), b_ref[...], preferred_element_type=F32)
        for lg in range(CG_STATES // w):
            re = slice(lg * w, (lg + 1) * w)
            im = slice(CG_STATES + lg * w, CG_STATES + (lg + 1) * w)
            consts = _scan_consts(c_ref, lg)
            qr, qi = consts[6], consts[7]

            def tile(tt, st):
                cr, ci = st
                t = (ntile - 1 - tt) if reverse else tt
                r0 = pl.multiple_of(t * 8, 8)
                br, bi = _tile_scan(xs_ref[pl.ds(r0, 8), re], xs_ref[pl.ds(r0, 8), im], consts, reverse)
                lr = jnp.broadcast_to(cr[edge:edge + 1, :], br.shape)
                li = jnp.broadcast_to(ci[edge:edge + 1, :], bi.shape)
                xr = br + qr * lr - qi * li
                xi = bi + qr * li + qi * lr
                xs_ref[pl.ds(r0, 8), re] = xr
                xs_ref[pl.ds(r0, 8), im] = xi
                return xr, xi

            cr, ci = lax.fori_loop(0, ntile, tile, (carry[:, re], carry[:, im]))
            carry[:, re] = cr
            carry[:, im] = ci

    cw = 2 * CG_STATES
    return pl.pallas_call(
        body, name=name, grid=(N_CG, nch),
        in_specs=[pl.BlockSpec((t_rows, SSM_WIDTH), lambda g, j: (chunk(j), u_cb)),
                  pl.BlockSpec((SSM_WIDTH, cw), lambda g, j: (0, g)),
                  pl.BlockSpec((64, CG_STATES), lambda g, j: (0, g))],
        out_specs=pl.BlockSpec((t_rows, cw), lambda g, j: (chunk(j), g)),
        out_shape=_sds((n, 2 * N_STATE), F32),
        scratch_shapes=[pltpu.VMEM((8, cw), F32)],
        compiler_params=_cparams(("arbitrary", "arbitrary")),
    )(z, bbig, lamc)


def _s5_bwd(dys, z, u_cb, xs, cbig_t, bbig_t, lamc_adj, t_rows, nl, reverse, name):
    n = z.shape[0]
    nch, nlc = n // t_rows, nl // t_rows
    ntile = t_rows // 8
    t8 = t_rows // 8
    w = SCAN_LANES
    cw = 2 * CG_STATES
    adj_rev = not reverse
    edge = 0 if adj_rev else 7

    def chunk(j):
        return _seq_chunk(nch - 1 - j, nch, nlc, reverse)

    def halo_blk(j):
        if reverse:
            return jnp.minimum((chunk(j) + 1) * t8, n // 8 - 1)
        return (_seq_chunk(jnp.maximum(nch - 2 - j, 0), nch, nlc, False) + 1) * t8 - 1

    def body(dy_ref, u_ref, xs_ref, halo_ref, ct_ref, bt_ref, c_ref, du_ref, db_ref, dl_ref, gbuf, carry):
        j = pl.program_id(1)
        start = j == nch - 1

        @pl.when(j == 0)
        def _():
            carry[...] = jnp.zeros(carry.shape, F32)
            db_ref[...] = jnp.zeros(db_ref.shape, F32)
            dl_ref[...] = jnp.zeros(dl_ref.shape, F32)

        g0 = jnp.dot(dy_ref[...].astype(MXU_DTYPE), ct_ref[...], preferred_element_type=F32)
        gbuf[...] = jnp.where(chunk(j) < nlc, g0, 0.0)
        for lg in range(CG_STATES // w):
            re = slice(lg * w, (lg + 1) * w)
            im = slice(CG_STATES + lg * w, CG_STATES + (lg + 1) * w)
            consts = _scan_consts(c_ref, lg)
            qr, qi = consts[6], consts[7]
            hr, hi = halo_ref[:, re], halo_ref[:, im]

            def tile(tt, st):
                gcr, gci, ar, ai = st
                t = (ntile - 1 - tt) if adj_rev else tt
                r0 = pl.multiple_of(t * 8, 8)
                br, bi = _tile_scan(gbuf[pl.ds(r0, 8), re], gbuf[pl.ds(r0, 8), im], consts, adj_rev)
                lr = jnp.broadcast_to(gcr[edge:edge + 1, :], br.shape)
                li = jnp.broadcast_to(gci[edge:edge + 1, :], bi.shape)
                gr = br + qr * lr - qi * li
                gi = bi + qr * li + qi * lr
                gbuf[pl.ds(r0, 8), re] = gr
                gbuf[pl.ds(r0, 8), im] = gi
                xr, xi = xs_ref[pl.ds(r0, 8), re], xs_ref[pl.ds(r0, 8), im]
                rid = lax.broadcasted_iota(jnp.int32, xr.shape, 0)
                if reverse:
                    last = t == ntile - 1
                    rn = pl.multiple_of(jnp.minimum(r0 + 8, t_rows - 8), 8)
                    nbr = jnp.where(last, hr, xs_ref[pl.ds(rn, 8), re])
                    nbi = jnp.where(last, hi, xs_ref[pl.ds(rn, 8), im])
                    nbr = jnp.where(last & start, 0.0, nbr)
                    nbi = jnp.where(last & start, 0.0, nbi)
                    xpr = jnp.where(rid == 7, jnp.broadcast_to(nbr[0:1, :], xr.shape), pltpu.roll(xr, 7, 0))
                    xpi = jnp.where(rid == 7, jnp.broadcast_to(nbi[0:1, :], xi.shape), pltpu.roll(xi, 7, 0))
                else:
                    first = t == 0
                    rn = pl.multiple_of(jnp.maximum(r0 - 8, 0), 8)
                    nbr = jnp.where(first, hr, xs_ref[pl.ds(rn, 8), re])
                    nbi = jnp.where(first, hi, xs_ref[pl.ds(rn, 8), im])
                    nbr = jnp.where(first & start, 0.0, nbr)
                    nbi = jnp.where(first & start, 0.0, nbi)
                    xpr = jnp.where(rid == 0, jnp.broadcast_to(nbr[7:8, :], xr.shape), pltpu.roll(xr, 1, 0))
                    xpi = jnp.where(rid == 0, jnp.broadcast_to(nbi[7:8, :], xi.shape), pltpu.roll(xi, 1, 0))
                ar = ar + gr * xpr + gi * xpi
                ai = ai - gr * xpi + gi * xpr
                return gr, gi, ar, ai

            zz = jnp.zeros((8, w), F32)
            gcr, gci, ar, ai = lax.fori_loop(0, ntile, tile, (carry[:, re], carry[:, im], zz, zz))
            carry[:, re] = gcr
            carry[:, im] = gci
            dl_ref[:, re] += ar
            dl_ref[:, im] += ai
        g = gbuf[...].astype(MXU_DTYPE)
        du_ref[...] = jnp.dot(g, bt_ref[...], preferred_element_type=F32)
        db_ref[...] += lax.dot_general(u_ref[...].astype(MXU_DTYPE), g, (((0,), (0,)), ((), ())),
                                       preferred_element_type=F32)

    return pl.pallas_call(
        body, name=name, grid=(N_CG, nch),
        in_specs=[pl.BlockSpec((t_rows, SSM_WIDTH), lambda g, j: (jnp.minimum(chunk(j), nlc - 1), 0)),
                  pl.BlockSpec((t_rows, SSM_WIDTH), lambda g, j: (chunk(j), u_cb)),
                  pl.BlockSpec((t_rows, cw), lambda g, j: (chunk(j), g)),
                  pl.BlockSpec((8, cw), lambda g, j: (halo_blk(j), g)),
                  pl.BlockSpec((SSM_WIDTH, cw), lambda g, j: (0, g)),
                  pl.BlockSpec((cw, SSM_WIDTH), lambda g, j: (g, 0)),
                  pl.BlockSpec((64, CG_STATES), lambda g, j: (0, g))],
        out_specs=[pl.BlockSpec((None, t_rows, SSM_WIDTH), lambda g, j: (g, chunk(j), 0)),
                   pl.BlockSpec((SSM_WIDTH, cw), lambda g, j: (0, g)),
                   pl.BlockSpec((8, cw), lambda g, j: (0, g))],
        out_shape=[_sds((N_CG, n, SSM_WIDTH), F32), _sds((SSM_WIDTH, 2 * N_STATE), F32), _sds((8, 2 * N_STATE), F32)],
        scratch_shapes=[pltpu.VMEM((t_rows, cw), F32), pltpu.VMEM((8, cw), F32)],
        compiler_params=_cparams(("arbitrary", "arbitrary")),
    )(dys, z, xs, xs, cbig_t, bbig_t, lamc_adj)


def _to_cols(re_part, im_part):
    r = re_part.shape[0]
    parts = [t.reshape(r, N_CG, 1, CG_STATES) for t in (re_part, im_part)]
    return jnp.concatenate(parts, axis=2).reshape(r, 2 * N_STATE)


def _from_cols(d):
    r = d.shape[0]
    d4 = d.reshape(r, N_CG, 2, CG_STATES)
    return d4[:, :, 0].reshape(r, N_STATE), d4[:, :, 1].reshape(r, N_STATE)


def _group_mask():
    idx = jnp.arange(SSM_GROUPS)
    return (idx[:, None] == idx[None, :])[:, None, :, None]


def _block_diag(p_gnc):
    t = jnp.transpose(p_gnc, (0, 2, 1))[:, :, None, :]
    return jnp.where(_group_mask(), t, 0.0).reshape(SSM_GROUPS * SSM_GROUP, SSM_GROUPS * SSM_STATE)


def _block_diag_extract(d):
    d4 = d.reshape(SSM_GROUPS, SSM_GROUP, SSM_GROUPS, SSM_STATE)
    blk = jnp.sum(jnp.where(_group_mask(), d4, 0.0), axis=2)
    return jnp.transpose(blk, (0, 2, 1))


def _s5_disc(lam_re, lam_im, log_dt, b_re, b_im):
    lam = lax.complex(lam_re, lam_im)
    dt = jnp.exp(log_dt)[:, None]
    lam_bar = jnp.exp(lam * dt)
    b_bar = ((lam_bar - 1.0) / lam)[..., None] * lax.complex(b_re, b_im)
    return jnp.real(lam_bar), jnp.imag(lam_bar), jnp.real(b_bar), jnp.imag(b_bar)


def _lam_consts(lr, li, mirrored, conj):
    lam = lax.complex(lr.reshape(-1), -li.reshape(-1) if conj else li.reshape(-1))
    p2 = lam * lam
    p4 = p2 * p2
    pw = [lam, p2, p2 * lam, p4, p4 * lam, p4 * p2, p4 * p2 * lam, p4 * p4]
    rows = jnp.arange(8)[:, None]
    out = []
    for kk in (1, 2, 4):
        mask = (rows <= 7 - kk) if mirrored else (rows >= kk)
        pk = jnp.where(mask, pw[kk - 1][None, :], 0.0)
        out += [jnp.real(pk), jnp.imag(pk)]
    q = jnp.stack(pw[::-1] if mirrored else pw)
    return jnp.concatenate(out + [jnp.real(q), jnp.imag(q)], axis=0)


def _dev(t):
    return (t // 4, (t // 2) % 2, t % 2)


def _my_index():
    return 4 * lax.axis_index("x") + 2 * lax.axis_index("y") + lax.axis_index("c")


def _comm_call(body, name, arrs, lead, n_remote):
    nw = len(arrs)
    any_spec = pl.BlockSpec(memory_space=pl.ANY)
    return pl.pallas_call(
        body, name=name, out_shape=[_sds((lead,) + a.shape[-2:], a.dtype) for a in arrs],
        in_specs=[any_spec] * nw, out_specs=[any_spec] * nw,
        scratch_shapes=[pltpu.SemaphoreType.DMA((n_remote * nw,)), pltpu.SemaphoreType.DMA((n_remote * nw,)),
                        pltpu.SemaphoreType.DMA((nw,))],
    )(*arrs)


def _exchange8(gs, name, same):
    nw = len(gs)

    def body(*refs):
        g_refs, o_refs, (ssem, rsem, lsem) = refs[:nw], refs[nw:2 * nw], refs[2 * nw:]
        me = _my_index()
        locs, sends = [], []
        for i, (g_ref, o_ref) in enumerate(zip(g_refs, o_refs)):
            src = (lambda t, g_ref=g_ref: g_ref) if same else (lambda t, g_ref=g_ref: g_ref.at[t])
            loc = pltpu.make_async_copy(src(me), o_ref.at[me], lsem.at[i])
            loc.start()
            locs.append(loc)
            for d in range(1, 8):
                t = (me + d) % 8
                cp = pltpu.make_async_remote_copy(src_ref=src(t), dst_ref=o_ref.at[me], send_sem=ssem.at[7 * i + d - 1],
                                                  recv_sem=rsem.at[7 * i + d - 1], device_id=_dev(t), device_id_type=MESH)
                cp.start()
                sends.append(cp)
        for i, (g_ref, o_ref) in enumerate(zip(g_refs, o_refs)):
            src = (lambda t, g_ref=g_ref: g_ref) if same else (lambda t, g_ref=g_ref: g_ref.at[t])
            for d in range(1, 8):
                s = (me + 8 - d) % 8
                pltpu.make_async_remote_copy(src_ref=src(s), dst_ref=o_ref.at[s], send_sem=ssem.at[7 * i + d - 1],
                                             recv_sem=rsem.at[7 * i + d - 1], device_id=_dev(s),
                                             device_id_type=MESH).wait_recv()
        for cp in sends:
            cp.wait_send()
        for loc in locs:
            loc.wait()

    return _comm_call(body, name, gs, 8, 7)


def _exchange_chips(ws, name, gather):
    nw = len(ws)

    def body(*refs):
        w_refs, o_refs, (ssem, rsem, lsem) = refs[:nw], refs[nw:2 * nw], refs[2 * nw:]
        x, y, cc = lax.axis_index("x"), lax.axis_index("y"), lax.axis_index("c")
        k = 2 * x + y
        peers = [(1 - x, y), (x, 1 - y), (1 - x, 1 - y)]
        locs, sends = [], []
        for i, (w_ref, o_ref) in enumerate(zip(w_refs, o_refs)):
            if gather:
                loc = pltpu.make_async_copy(w_ref.at[cc], o_ref.at[k], lsem.at[i])
                loc.start()
                locs.append(loc)
            for j, (px, py) in enumerate(peers):
                src, dst = (w_ref.at[cc], o_ref.at[k]) if gather else (w_ref.at[2 * px + py], o_ref.at[j])
                cp = pltpu.make_async_remote_copy(src_ref=src, dst_ref=dst, send_sem=ssem.at[3 * i + j],
                                                  recv_sem=rsem.at[3 * i + j], device_id=(px, py, cc), device_id_type=MESH)
                cp.start()
                sends.append(cp)
        for i, (w_ref, o_ref) in enumerate(zip(w_refs, o_refs)):
            for j, (px, py) in enumerate(peers):
                src, dst = (w_ref.at[cc], o_ref.at[2 * px + py]) if gather else (w_ref.at[k], o_ref.at[j])
                pltpu.make_async_remote_copy(src_ref=src, dst_ref=dst, send_sem=ssem.at[3 * i + j],
                                             recv_sem=rsem.at[3 * i + j], device_id=(px, py, cc),
                                             device_id_type=MESH).wait_recv()
        for cp in sends:
            cp.wait_send()
        for loc in locs:
            loc.wait()

    return _comm_call(body, name, ws, 4 if gather else 3, 3)


def _sibling_send(hs, name):
    nw = len(hs)

    def body(*refs):
        h_refs, o_refs, (ssem, rsem, lsem) = refs[:nw], refs[nw:2 * nw], refs[2 * nw:]
        x, y, cc = lax.axis_index("x"), lax.axis_index("y"), lax.axis_index("c")
        sends = []
        for i, (h_ref, o_ref) in enumerate(zip(h_refs, o_refs)):
            cp = pltpu.make_async_remote_copy(src_ref=h_ref.at[1 - cc], dst_ref=o_ref, send_sem=ssem.at[i],
                                              recv_sem=rsem.at[i], device_id=(x, y, 1 - cc), device_id_type=MESH)
            cp.start()
            sends.append(cp)
        for i, (h_ref, o_ref) in enumerate(zip(h_refs, o_refs)):
            pltpu.make_async_remote_copy(src_ref=h_ref.at[cc], dst_ref=o_ref, send_sem=ssem.at[i], recv_sem=rsem.at[i],
                                         device_id=(x, y, 1 - cc), device_id_type=MESH).wait_recv()
        for cp in sends:
            cp.wait_send()

    nw_spec = pl.BlockSpec(memory_space=pl.ANY)
    return pl.pallas_call(
        body, name=name, out_shape=[_sds(h.shape[1:], h.dtype) for h in hs],
        in_specs=[nw_spec] * nw, out_specs=[nw_spec] * nw,
        scratch_shapes=[pltpu.SemaphoreType.DMA((nw,)), pltpu.SemaphoreType.DMA((nw,)), pltpu.SemaphoreType.DMA((nw,))],
    )(*hs)


def _sibling_exchange(hs, name):
    nw = len(hs)

    def body(*refs):
        h_refs, o_refs, (ssem, rsem, lsem) = refs[:nw], refs[nw:2 * nw], refs[2 * nw:]
        x, y, cc = lax.axis_index("x"), lax.axis_index("y"), lax.axis_index("c")
        locs, sends = [], []
        for i, (h_ref, o_ref) in enumerate(zip(h_refs, o_refs)):
            loc = pltpu.make_async_copy(h_ref, o_ref.at[cc], lsem.at[i])
            loc.start()
            locs.append(loc)
            cp = pltpu.make_async_remote_copy(src_ref=h_ref, dst_ref=o_ref.at[cc], send_sem=ssem.at[i], recv_sem=rsem.at[i],
                                              device_id=(x, y, 1 - cc), device_id_type=MESH)
            cp.start()
            sends.append(cp)
        for i, (h_ref, o_ref) in enumerate(zip(h_refs, o_refs)):
            pltpu.make_async_remote_copy(src_ref=h_ref, dst_ref=o_ref.at[1 - cc], send_sem=ssem.at[i], recv_sem=rsem.at[i],
                                         device_id=(x, y, 1 - cc), device_id_type=MESH).wait_recv()
        for cp in sends:
            cp.wait_send()
        for loc in locs:
            loc.wait()

    return _comm_call(body, name, hs, 2, 1)


def _sum8(buf, name):
    _, r, c = buf.shape
    tr = _pick(r, (256, 128, 64, 32, 16, 8))
    flat = buf.reshape(8 * r, c)

    def body(*v):
        acc = v[0]
        for t in v[1:]:
            acc = acc + t
        return acc

    return _rowwise(body, name=name, nblk=r // tr, tr=tr, rows=[(flat, 0, c, s * r) for s in range(8)],
                    outs=[(c, F32)])[0]


def _pack(arrs, rows_mult=16):
    flat = jnp.concatenate([a.reshape(-1).astype(F32) for a in arrs])
    nel = flat.shape[0]
    r = -(-nel // PACK_W)
    r = -(-r // rows_mult) * rows_mult
    return jnp.pad(flat, (0, r * PACK_W - nel)).reshape(r, PACK_W)


def _unpack(buf, shapes):
    flat = buf.reshape(-1)
    out, o = [], 0
    for s in shapes:
        nel = int(np.prod(s))
        out.append(flat[o:o + nel].reshape(s))
        o += nel
    return out


def _adamw(g, w, m, v, name):
    r, wd = g.shape
    tr = _pick(r, tuple(t for t in (256, 128, 64, 32, 16, 8) if t * wd <= 262144) or (8,))
    c1 = 1.0 / (1.0 - ADAM_B1 ** ADAM_STEP)
    c2 = 1.0 / (1.0 - ADAM_B2 ** ADAM_STEP)

    def body(gv, wv, mv, vv):
        mn = ADAM_B1 * mv + (1.0 - ADAM_B1) * gv
        vn = ADAM_B2 * vv + (1.0 - ADAM_B2) * (gv * gv)
        delta = -ADAM_LR * ((mn * c1) / (jnp.sqrt(vn * c2) + ADAM_EPS) + ADAM_WD * wv)
        return delta, mn, vn

    return _rowwise(body, name=name, nblk=r // tr, tr=tr, rows=[(a, 0, wd, 0) for a in (g, w, m, v)],
                    outs=[(wd, F32)] * 3)


def kernel(x, c, ctx, c_ctx, w_mod, b_mod, norm1_g, norm2_g, w_in, q_a_g, w_uq, kv_a_g, w_ukv, q_norm_g, k_norm_g, w_o_attn, lam_re_f, lam_im_f, log_dt_f, c_re_f, c_im_f, lam_re_b, lam_im_b, log_dt_b, c_re_b, c_im_b, b_re, b_im, d_skip, w_glu, w_out, w_up, conv_w, conv_b, w_down, loss_target, m_c_ctx, m_w_mod, m_b_mod, m_norm1_g, m_norm2_g, m_w_in, m_q_a_g, m_w_uq, m_kv_a_g, m_w_ukv, m_q_norm_g, m_k_norm_g, m_w_o_attn, m_lam_re_f, m_lam_im_f, m_log_dt_f, m_c_re_f, m_c_im_f, m_lam_re_b, m_lam_im_b, m_log_dt_b, m_c_re_b, m_c_im_b, m_b_re, m_b_im, m_d_skip, m_w_glu, m_w_out, m_w_up, m_conv_w, m_conv_b, m_w_down, v_c_ctx, v_w_mod, v_b_mod, v_norm1_g, v_norm2_g, v_w_in, v_q_a_g, v_w_uq, v_kv_a_g, v_w_ukv, v_q_norm_g, v_k_norm_g, v_w_o_attn, v_lam_re_f, v_lam_im_f, v_log_dt_f, v_c_re_f, v_c_im_f, v_lam_re_b, v_lam_im_b, v_log_dt_b, v_c_re_b, v_c_im_b, v_b_re, v_b_im, v_d_skip, v_w_glu, v_w_out, v_w_up, v_conv_w, v_conv_b, v_w_down):
    weights = dict(c_ctx=c_ctx, w_mod=w_mod, b_mod=b_mod, norm1_g=norm1_g, norm2_g=norm2_g, w_in=w_in, q_a_g=q_a_g, w_uq=w_uq, kv_a_g=kv_a_g, w_ukv=w_ukv, q_norm_g=q_norm_g, k_norm_g=k_norm_g, w_o_attn=w_o_attn, lam_re_f=lam_re_f, lam_im_f=lam_im_f, log_dt_f=log_dt_f, c_re_f=c_re_f, c_im_f=c_im_f, lam_re_b=lam_re_b, lam_im_b=lam_im_b, log_dt_b=log_dt_b, c_re_b=c_re_b, c_im_b=c_im_b, b_re=b_re, b_im=b_im, d_skip=d_skip, w_glu=w_glu, w_out=w_out, w_up=w_up, conv_w=conv_w, conv_b=conv_b, w_down=w_down)
    mom_m = dict(c_ctx=m_c_ctx, w_mod=m_w_mod, b_mod=m_b_mod, norm1_g=m_norm1_g, norm2_g=m_norm2_g, w_in=m_w_in, q_a_g=m_q_a_g, w_uq=m_w_uq, kv_a_g=m_kv_a_g, w_ukv=m_w_ukv, q_norm_g=m_q_norm_g, k_norm_g=m_k_norm_g, w_o_attn=m_w_o_attn, lam_re_f=m_lam_re_f, lam_im_f=m_lam_im_f, log_dt_f=m_log_dt_f, c_re_f=m_c_re_f, c_im_f=m_c_im_f, lam_re_b=m_lam_re_b, lam_im_b=m_lam_im_b, log_dt_b=m_log_dt_b, c_re_b=m_c_re_b, c_im_b=m_c_im_b, b_re=m_b_re, b_im=m_b_im, d_skip=m_d_skip, w_glu=m_w_glu, w_out=m_w_out, w_up=m_w_up, conv_w=m_conv_w, conv_b=m_conv_b, w_down=m_w_down)
    mom_v = dict(c_ctx=v_c_ctx, w_mod=v_w_mod, b_mod=v_b_mod, norm1_g=v_norm1_g, norm2_g=v_norm2_g, w_in=v_w_in, q_a_g=v_q_a_g, w_uq=v_w_uq, kv_a_g=v_kv_a_g, w_ukv=v_w_ukv, q_norm_g=v_q_norm_g, k_norm_g=v_k_norm_g, w_o_attn=v_w_o_attn, lam_re_f=v_lam_re_f, lam_im_f=v_lam_im_f, log_dt_f=v_log_dt_f, c_re_f=v_c_re_f, c_im_f=v_c_im_f, lam_re_b=v_lam_re_b, lam_im_b=v_lam_im_b, log_dt_b=v_log_dt_b, c_re_b=v_c_re_b, c_im_b=v_c_im_b, b_re=v_b_re, b_im=v_b_im, d_skip=v_d_skip, w_glu=v_w_glu, w_out=v_w_out, w_up=v_w_up, conv_w=v_conv_w, conv_b=v_conv_b, w_down=v_w_down)
    names = list(weights)

    nl, d = x.shape[1], x.shape[2]
    nc = ctx.shape[1]
    n = nl + nc
    f2 = conv_b.shape[1]
    fh = f2 // 2
    d6 = b_mod.shape[1]
    mx, my, mc = lax.axis_index("x"), lax.axis_index("y"), lax.axis_index("c")
    chip = 2 * mx + my
    me = 4 * mx + 2 * my + mc
    tr = _pick(math.gcd(nl, nc), (256, 128, 64, 32, 16))
    nlb, nb = nl // tr, n // tr

    big_names = ["w_in", "w_uq", "w_ukv", "w_o_attn", "w_glu", "w_out", "w_up", "w_down"]
    row_sharded = ("w_out", "w_down")
    halves_in = [weights[k][0].astype(MXU_DTYPE).reshape(2, weights[k].shape[1] // 2, weights[k].shape[2]) for k in big_names]
    my_halves = _exchange_chips(halves_in, "gather_weights", True)
    gathered = _sibling_exchange([t.reshape(-1, t.shape[2]) for t in my_halves], "gather_weight_halves")
    full = {}
    for k_, gth in zip(big_names, gathered):
        r_, c_ = weights[k_].shape[1:]
        g4 = gth.reshape(2, 4, r_ // 2, c_)
        full[k_] = (jnp.transpose(g4, (1, 0, 2, 3)).reshape(4 * r_, c_) if k_ in row_sharded
                    else jnp.transpose(g4, (0, 2, 1, 3)).reshape(r_, 4 * c_))

    cwid = conv_w.shape[2]
    sw = -(-max(d, cwid) // 128) * 128
    small_in = jnp.concatenate([jnp.pad(c, ((0, 0), (0, sw - d))), jnp.pad(conv_w[0], ((0, 4), (0, sw - cwid)))], axis=0)
    (small_all,) = _exchange8([small_in], "gather_c", True)
    cs = small_all[:, 0, :d]
    conv_w_full = jnp.concatenate([small_all[2 * j, 1:4, :cwid] for j in range(4)], axis=1)
    cs16 = jnp.concatenate([cs, c_ctx[None, :], jnp.zeros((7, d), F32)], axis=0)

    csh = w_mod.shape[2]
    b_mod_sh = lax.dynamic_slice(b_mod, (0, chip * csh), (1, csh))

    def mod_fwd_body(c_ref, w_ref, b_ref, o_ref):
        a = _silu(c_ref[...]).astype(MXU_DTYPE)
        o_ref[...] = jnp.dot(a, w_ref[...].astype(MXU_DTYPE), preferred_element_type=F32) + b_ref[...]

    mod_sh = pl.pallas_call(mod_fwd_body, name="mod_fwd", out_shape=_sds((16, csh), F32),
                            compiler_params=pltpu.CompilerParams(vmem_limit_bytes=VMEM_LIMIT))(cs16, w_mod[0], b_mod_sh)
    (mod_all,) = _exchange8([mod_sh], "gather_mod", True)
    mod_full = jnp.concatenate([mod_all[2 * j] for j in range(4)], axis=1)
    modv = jnp.stack([lax.dynamic_slice(mod_full, (me, 0), (1, d6)), mod_full[8:9]])

    def mod_parts(m):
        return [m[:, j * d:(j + 1) * d] for j in range(6)]

    u_off, kv_off, kr_off = 2 * d, 2 * d + SSM_WIDTH, 2 * d + SSM_WIDTH + KV_LORA
    q_off = -(-(kr_off + SLOT) // Q_LORA) * Q_LORA
    zw = q_off + Q_LORA
    wi = full["w_in"]
    s0, s1, s2, s3 = Q_LORA, Q_LORA + KV_LORA, Q_LORA + KV_LORA + QK_ROPE, Q_LORA + KV_LORA + QK_ROPE + SSM_WIDTH
    zpad = lambda w_: jnp.zeros((d, w_), MXU_DTYPE)
    win_p = jnp.concatenate([wi[:, s3:], wi[:, s2:s3], wi[:, s0:s1], wi[:, s1:s2], zpad(SLOT - QK_ROPE),
                             zpad(q_off - kr_off - SLOT), wi[:, :s0]], axis=1)
    wuq_p = jnp.pad(full["w_uq"].reshape(Q_LORA, N_HEADS, QK_DIM), ((0, 0), (0, 0), (0, SLOT - QK_DIM))).reshape(Q_LORA, N_HEADS * SLOT)
    wukv3 = full["w_ukv"].reshape(KV_LORA, N_HEADS, QK_NOPE + V_DIM)
    padh = lambda t: jnp.pad(t, ((0, 0), (0, 0), (0, SLOT - t.shape[2]))).reshape(t.shape[0], N_HEADS * SLOT)
    wukv_p = jnp.concatenate([padh(wukv3[:, :, :QK_NOPE]), padh(wukv3[:, :, QK_NOPE:])], axis=1)
    wo_p = jnp.pad(full["w_o_attn"].reshape(N_HEADS, V_DIM, d), ((0, 0), (0, SLOT - V_DIM), (0, 0))).reshape(N_HEADS * SLOT, d)
    wglu, wout, wup, wdown = full["w_glu"], full["w_out"], full["w_up"], full["w_down"]
    hw = N_HEADS * SLOT
    gain_p = lambda g_: jnp.tile(jnp.pad(g_[0], (0, SLOT - QK_DIM)), N_HEADS)[None, :]
    qg_p, kg_p = gain_p(q_norm_g), gain_p(k_norm_g)

    tok = jnp.arange(nl)
    freqs = ROPE_THETA ** (-jnp.arange(QK_ROPE // 4, dtype=F32) / (QK_ROPE // 4))
    ang = jnp.concatenate([(tok // GRID_W)[:, None] * freqs, (tok % GRID_W)[:, None] * freqs], axis=-1)
    cos_t = jnp.concatenate([jnp.cos(ang), jnp.ones((nc, 16), F32)], axis=0)
    sin_t = jnp.concatenate([jnp.sin(ang), jnp.zeros((nc, 16), F32)], axis=0)
    zl = lambda w_: jnp.zeros((n, w_), F32)
    rope_c = jnp.concatenate([jnp.ones((n, QK_NOPE), F32), cos_t, cos_t, zl(SLOT - QK_DIM)], axis=1)
    rope_sa = jnp.concatenate([zl(QK_NOPE), -sin_t, zl(SLOT - QK_NOPE - 16)], axis=1)
    rope_sb = jnp.concatenate([zl(QK_NOPE + 16), sin_t, zl(SLOT - QK_DIM)], axis=1)

    dirs = (("f", lam_re_f, lam_im_f, log_dt_f, c_re_f, c_im_f, False), ("b", lam_re_b, lam_im_b, log_dt_b, c_re_b, c_im_b, True))
    bbig, cbig_t, cbig_n, bbig_t, lamc, lamc_adj, disc_vjps = [], [], [], [], [], [], []
    for _, l_re, l_im, l_dt, cr_, ci_, rev_ in dirs:
        (lbr, lbi, bbr, bbi), vjp = jax.vjp(_s5_disc, l_re[0], l_im[0], l_dt[0], b_re[0], b_im[0])
        disc_vjps.append(vjp)
        bb = _to_cols(_block_diag(bbr), _block_diag(bbi)).astype(MXU_DTYPE)
        cc_ = _to_cols(_block_diag(jnp.transpose(cr_[0], (0, 2, 1))),
                       -_block_diag(jnp.transpose(ci_[0], (0, 2, 1)))).astype(MXU_DTYPE)
        bbig.append(bb)
        bbig_t.append(jnp.transpose(bb))
        cbig_t.append(cc_)
        cbig_n.append(jnp.transpose(cc_))
        lamc.append(_lam_consts(lbr, lbi, rev_, False))
        lamc_adj.append(_lam_consts(lbr, lbi, not rev_, True))
    t_scan = tr

    xa = jnp.concatenate([x[0], ctx[0]], axis=0)
    n1g, n2g = norm1_g, norm2_g

    def norm1_body(xv, m, g):
        sh1, sc1 = m[:, :d], m[:, d:2 * d]
        return _rms_fwd(xv, g, d) * (1.0 + sc1) + sh1

    (h1,) = _rowwise(norm1_body, name="norm1_fwd", nblk=nb, tr=tr, rows=[(xa, 0, d, 0)], sels=[modv], fulls=[n1g],
                     outs=[(d, MXU_DTYPE)], seg=nlb)
    z = _mm(h1, win_p, "nn", "in_proj")
    gl_cb, u_cb, kv_cb, kr_cb, q_cb = 0, u_off // SSM_WIDTH, kv_off // KV_LORA, kr_off // SLOT, q_off // Q_LORA

    (cqn,) = _rowwise(lambda v, g: _rms_fwd(v, g, Q_LORA), name="qa_norm_fwd", nblk=nlb, tr=tr,
                      rows=[(z, q_cb, Q_LORA, 0)], fulls=[q_a_g], outs=[(Q_LORA, MXU_DTYPE)])
    qh = _mm(cqn, wuq_p, "nn", "q_up")

    def qhead_body(qv, cv, sav, sbv, g):
        return jnp.concatenate([_rope_fwd(_rms_fwd(t, g[:, :SLOT], QK_DIM), cv, sav, sbv) for t in _heads(qv)], axis=1)

    rope_rows = lambda: [(rope_c, 0, SLOT, 0), (rope_sa, 0, SLOT, 0), (rope_sb, 0, SLOT, 0)]
    (q_p,) = _rowwise(qhead_body, name="q_head_fwd", nblk=nlb, tr=tr, rows=[(qh, 0, hw, 0)] + rope_rows(),
                      fulls=[qg_p], outs=[(hw, MXU_DTYPE)])

    (ckvn,) = _rowwise(lambda v, g: _rms_fwd(v, g, KV_LORA), name="kva_norm_fwd", nblk=nb, tr=tr,
                       rows=[(z, kv_cb, KV_LORA, 0)], fulls=[kv_a_g], outs=[(KV_LORA, MXU_DTYPE)])
    kvpre = _mm(ckvn, wukv_p, "nn", "kv_up")

    def khead_body(kv_, vv_, krv, cv, sav, sbv, g):
        kpe = pltpu.roll(krv, QK_NOPE, 1)
        ks = [_rope_fwd(_rms_fwd(t + kpe, g[:, :SLOT], QK_DIM), cv, sav, sbv) for t in _heads(kv_)]
        return jnp.concatenate(ks, axis=1), vv_

    k_p, v_p = _rowwise(khead_body, name="k_head_fwd", nblk=nb, tr=tr,
                        rows=[(kvpre, 0, hw, 0), (kvpre, 1, hw, 0), (z, kr_cb, SLOT, 0)] + rope_rows(),
                        fulls=[kg_p], outs=[(hw, MXU_DTYPE), (hw, MXU_DTYPE)])

    scale = QK_DIM ** -0.5
    o_p, lse = _attn_fwd(q_p, k_p, v_p, nl, scale)
    a_l = _mm(o_p, wo_p, "nn", "attn_out")

    xs = [_s5_scan(z, u_cb, bbig[j], lamc[j], t_scan, nl, dirs[j][6], "s5_scan_" + dirs[j][0]) for j in range(2)]
    ydir = [_mm(xs[j], cbig_n[j], "nn", "s5_read_" + dirs[j][0], rows=nl) for j in range(2)]

    def ssm_out_body(uv, a, b, dsk):
        ys = uv * dsk + a + b
        return ys, _gelu(ys)

    ys, ge = _rowwise(ssm_out_body, name="s5_out_fwd", nblk=nlb, tr=tr,
                      rows=[(z, u_cb, SSM_WIDTH, 0), (ydir[0], 0, SSM_WIDTH, 0), (ydir[1], 0, SSM_WIDTH, 0)],
                      fulls=[d_skip], outs=[(SSM_WIDTH, F32), (SSM_WIDTH, MXU_DTYPE)])
    glu_out = _mm(ge, wglu, "nn", "glu_proj")

    def merge_body(ga, gs, av, val, gate):
        return _sigmoid(ga) * av + _sigmoid(gs) * (val * _sigmoid(gate))

    merge_rows = lambda: [(z, 0, d, 0), (z, 1, d, 0), (a_l, 0, d, 0), (glu_out, 0, d, 0), (glu_out, 1, d, 0)]
    (merged,) = _rowwise(merge_body, name="merge_fwd", nblk=nlb, tr=tr, rows=merge_rows(), outs=[(d, MXU_DTYPE)])
    mo = _mm(merged, wout, "nn", "out_proj")
    mod_x = modv[0]

    def norm2_body(xv, mov, m, g):
        g1, sh2, sc2 = m[:, 2 * d:3 * d], m[:, 3 * d:4 * d], m[:, 4 * d:5 * d]
        x1v = xv + g1 * mov
        return x1v, _rms_fwd(x1v, g, d) * (1.0 + sc2) + sh2

    x1, h2 = _rowwise(norm2_body, name="norm2_fwd", nblk=nlb, tr=tr, rows=[(xa, 0, d, 0), (mo, 0, d, 0)],
                      fulls=[mod_x, n2g], outs=[(d, F32), (d, MXU_DTYPE)])
    up = _mm(h2, wup, "nn", "ffn_up")
    cw8 = jnp.zeros((8, f2), F32).at[:3].set(conv_w_full)

    def conv3(t3, w8, off):
        p_, c_, n_ = t3
        return p_ * w8[0:1, off:off + fh] + c_ * w8[1:2, off:off + fh] + n_ * w8[2:3, off:off + fh]

    def conv_fwd_body(val3, gate3, w8, bias):
        val2 = conv3(val3, w8, 0) + bias[:, :fh]
        gate2 = conv3(gate3, w8, fh) + bias[:, fh:]
        return _silu(gate2) * val2

    (act,) = _rowwise(conv_fwd_body, name="conv_fwd", nblk=nlb, tr=tr, rows=[(up, 0, fh, 0), (up, 1, fh, 0)],
                      halo=(0, 1), fulls=[cw8, conv_b], outs=[(fh, MXU_DTYPE)])
    dn = _mm(act, wdown, "nn", "ffn_down")
    tgt = loss_target[0]

    def loss_body(x1v, dnv, tv, m):
        g2 = m[:, 5 * d:6 * d]
        e = x1v + g2 * dnv - tv
        dx2v = e * (1.0 / d)
        return dx2v, dx2v * g2, e * e, dx2v * dnv

    dx2, ddn, loss_acc, dg2_acc = _rowwise(loss_body, name="loss", nblk=nlb, tr=tr,
                                           rows=[(x1, 0, d, 0), (dn, 0, d, 0), (tgt, 0, d, 0)], fulls=[mod_x],
                                           outs=[(d, F32), (d, MXU_DTYPE)], accs=[d, d])
    loss = lax.psum(0.5 / d * jnp.sum(loss_acc), ("x", "y", "c"))

    g_big = {}
    dact = _mm(ddn, wdown, "nt", "ffn_down_dx")
    g_big["w_down"] = _mm(act, ddn, "tn", "ffn_down_dw")

    def conv_bwd_body(val3, gate3, da, w8, bias):
        val2 = conv3(val3, w8, 0) + bias[:, :fh]
        gate2 = conv3(gate3, w8, fh) + bias[:, fh:]
        dval2 = da * _silu(gate2)
        dgate2 = da * val2 * _dsilu(gate2)
        du2 = jnp.concatenate([dval2, dgate2], axis=1)
        taps = [jnp.concatenate([dval2 * val3[j], dgate2 * gate3[j]], axis=1) for j in range(3)]
        return du2, du2, taps[0], taps[1], taps[2]

    du2, dcb_acc, dcw0, dcw1, dcw2 = _rowwise(conv_bwd_body, name="conv_bwd", nblk=nlb, tr=tr,
                                              rows=[(up, 0, fh, 0), (up, 1, fh, 0), (dact, 0, fh, 0)], halo=(0, 1),
                                              fulls=[cw8, conv_b], outs=[(f2, F32)], accs=[f2, f2, f2, f2])

    def conv_t_body(dval3, dgate3, w8):
        rev = lambda t3: (t3[2], t3[1], t3[0])
        return jnp.concatenate([conv3(rev(dval3), w8, 0), conv3(rev(dgate3), w8, fh)], axis=1)

    (dup,) = _rowwise(conv_t_body, name="conv_bwd_dx", nblk=nlb, tr=tr, rows=[(du2, 0, fh, 0), (du2, 1, fh, 0)],
                      halo=(0, 1), fulls=[cw8], outs=[(f2, MXU_DTYPE)])
    dh2 = _mm(dup, wup, "nt", "ffn_up_dx")
    g_big["w_up"] = _mm(h2, dup, "tn", "ffn_up_dw")

    def norm2_bwd_body(x1v, dh, dx2v, mov, m, g):
        g1, sc2 = m[:, 2 * d:3 * d], m[:, 4 * d:5 * d]
        y = _rms_fwd(x1v, g, d)
        dxn, dgc = _rms_bwd(x1v, g, dh * (1.0 + sc2), d)
        dx1v = dx2v + dxn
        return dx1v, dx1v * g1, dgc, dh, dh * y, dx1v * mov

    dx1, dmo, dn2g_acc, dsh2_acc, dsc2_acc, dg1_acc = _rowwise(
        norm2_bwd_body, name="norm2_bwd", nblk=nlb, tr=tr,
        rows=[(x1, 0, d, 0), (dh2, 0, d, 0), (dx2, 0, d, 0), (mo, 0, d, 0)], fulls=[mod_x, n2g],
        outs=[(d, F32), (d, MXU_DTYPE)], accs=[d, d, d, d])
    dmerged = _mm(dmo, wout, "nt", "out_proj_dx")
    g_big["w_out"] = _mm(merged, dmo, "tn", "out_proj_dw")

    def merge_bwd_body(ga, gs, av, val, gate, dm):
        sa_, ss_, sg_ = _sigmoid(ga), _sigmoid(gs), _sigmoid(gate)
        s_l = val * sg_
        ds_l = dm * ss_
        dga = dm * av * sa_ * (1.0 - sa_)
        dgs = dm * s_l * ss_ * (1.0 - ss_)
        dval = ds_l * sg_
        dgate = ds_l * val * sg_ * (1.0 - sg_)
        return dm * sa_, jnp.concatenate([dval, dgate], axis=1), jnp.concatenate([dga, dgs], axis=1)

    da_l, dglu, dgl = _rowwise(merge_bwd_body, name="merge_bwd", nblk=nlb, tr=tr,
                               rows=merge_rows() + [(dmerged, 0, d, 0)],
                               outs=[(d, MXU_DTYPE), (2 * d, MXU_DTYPE), (2 * d, MXU_DTYPE)])
    dge = _mm(dglu, wglu, "nt", "glu_proj_dx")
    g_big["w_glu"] = _mm(ge, dglu, "tn", "glu_proj_dw")

    def ssm_out_bwd_body(ysv, dgev, uv, dsk):
        dys_ = dgev * _dgelu(ysv)
        return dys_, dys_ * dsk, dys_ * uv

    dys, du_skip, ddskip_acc = _rowwise(ssm_out_bwd_body, name="s5_out_bwd", nblk=nlb, tr=tr,
                                        rows=[(ys, 0, SSM_WIDTH, 0), (dge, 0, SSM_WIDTH, 0), (z, u_cb, SSM_WIDTH, 0)],
                                        fulls=[d_skip], outs=[(SSM_WIDTH, F32), (SSM_WIDTH, F32)], accs=[SSM_WIDTH])
    s5b = [_s5_bwd(dys, z, u_cb, xs[j], cbig_t[j], bbig_t[j], lamc_adj[j], t_scan, nl, dirs[j][6], "s5_bwd_" + dirs[j][0])
           for j in range(2)]
    dcbig = [_mm(xs[j], dys, "tn", "s5_read_dw_" + dirs[j][0], rows=nl) for j in range(2)]
    du_nat = jnp.sum(s5b[0][0], axis=0) + jnp.sum(s5b[1][0], axis=0)
    du_nat = du_nat + jnp.concatenate([du_skip, jnp.zeros((nc, SSM_WIDTH), F32)], axis=0)

    do_f = _mm(da_l, wo_p, "nt", "attn_out_dx")
    g_wo_p = _mm(o_p, da_l, "tn", "attn_out_dw")

    def delta_body(dov, ov):
        prod = dov * ov.astype(F32)
        dl = [jnp.broadcast_to(jnp.sum(t, axis=-1, keepdims=True), t.shape) for t in _heads(prod)]
        return dov, jnp.concatenate(dl, axis=1)

    do_b, delta = _rowwise(delta_body, name="attn_delta", nblk=nlb, tr=tr, rows=[(do_f, 0, hw, 0), (o_p, 0, hw, 0)],
                           outs=[(hw, MXU_DTYPE), (hw, F32)])
    to_rows = lambda t: jnp.broadcast_to(jnp.transpose(t[:, ::SLOT])[:, None, :], (N_HEADS, 8, nl))
    dq_t, dk_p, dv_p = _attn_bwd(q_p, k_p, jnp.transpose(k_p), v_p, do_b, to_rows(lse), to_rows(delta), nl, scale)
    dq_p = jnp.transpose(dq_t)

    def qhead_bwd_body(qv, dqv, cv, sav, sbv, g):
        dxs, dgs = [], []
        for t, dt_ in zip(_heads(qv), _heads(dqv)):
            dx_, dg_ = _rms_bwd(t, g[:, :SLOT], _rope_bwd(dt_, cv, sav, sbv), QK_DIM)
            dxs.append(dx_)
            dgs.append(dg_)
        return jnp.concatenate(dxs, axis=1), jnp.concatenate(dgs, axis=1)

    dqh, dqg_acc = _rowwise(qhead_bwd_body, name="q_head_bwd", nblk=nlb, tr=tr,
                            rows=[(qh, 0, hw, 0), (dq_p, 0, hw, 0)] + rope_rows(), fulls=[qg_p],
                            outs=[(hw, MXU_DTYPE)], accs=[hw])
    dcqn = _mm(dqh, wuq_p, "nt", "q_up_dx")
    g_wuq_p = _mm(cqn, dqh, "tn", "q_up_dw")
    dcq, dqag_acc = _rowwise(lambda v, dy, g: _rms_bwd(v, g, dy, Q_LORA), name="qa_norm_bwd", nblk=nlb, tr=tr,
                             rows=[(z, q_cb, Q_LORA, 0), (dcqn, 0, Q_LORA, 0)], fulls=[q_a_g],
                             outs=[(Q_LORA, MXU_DTYPE)], accs=[Q_LORA])

    def khead_bwd_body(kv_, krv, dkv_, dvv_, cv, sav, sbv, g):
        kpe = pltpu.roll(krv, QK_NOPE, 1)
        lane = lax.broadcasted_iota(jnp.int32, krv.shape, 1)
        dxs, dgs, dkr_ = [], [], jnp.zeros(krv.shape, F32)
        for t, dt_ in zip(_heads(kv_), _heads(dkv_)):
            dx_, dg_ = _rms_bwd(t + kpe, g[:, :SLOT], _rope_bwd(dt_, cv, sav, sbv), QK_DIM)
            dxs.append(jnp.where(lane < QK_NOPE, dx_, 0.0))
            dgs.append(dg_)
            dkr_ = dkr_ + dx_
        dkr_ = jnp.where(lane < QK_ROPE, pltpu.roll(dkr_, SLOT - QK_NOPE, 1), 0.0)
        return jnp.concatenate(dxs + [dvv_], axis=1), dkr_, jnp.concatenate(dgs, axis=1)

    dkvpre, dkr, dkg_acc = _rowwise(khead_bwd_body, name="k_head_bwd", nblk=nb, tr=tr,
                                    rows=[(kvpre, 0, hw, 0), (z, kr_cb, SLOT, 0), (dk_p, 0, hw, 0), (dv_p, 0, hw, 0)] + rope_rows(),
                                    fulls=[kg_p], outs=[(2 * hw, MXU_DTYPE), (SLOT, MXU_DTYPE)], accs=[hw])
    dckvn = _mm(dkvpre, wukv_p, "nt", "kv_up_dx")
    g_wukv_p = _mm(ckvn, dkvpre, "tn", "kv_up_dw")
    dckv, dkvag_acc = _rowwise(lambda v, dy, g: _rms_bwd(v, g, dy, KV_LORA), name="kva_norm_bwd", nblk=nb, tr=tr,
                               rows=[(z, kv_cb, KV_LORA, 0), (dckvn, 0, KV_LORA, 0)], fulls=[kv_a_g],
                               outs=[(KV_LORA, MXU_DTYPE)], accs=[KV_LORA])

    padc = lambda t: jnp.concatenate([t, jnp.zeros((nc, t.shape[1]), t.dtype)], axis=0)
    dz = jnp.concatenate([padc(dgl), du_nat.astype(MXU_DTYPE), dckv, dkr,
                          jnp.zeros((n, q_off - kr_off - SLOT), MXU_DTYPE), padc(dcq)], axis=1)
    dh1 = _mm(dz, win_p, "nt", "in_proj_dx")
    g_win_p = _mm(h1, dz, "tn", "in_proj_dw")

    def norm1_bwd_body(xv, dh, m, g):
        sc1 = m[:, d:2 * d]
        y = _rms_fwd(xv, g, d)
        dxn, dgc = _rms_bwd(xv, g, dh * (1.0 + sc1), d)
        return dxn, dgc, dh, dh * y

    dxa, dn1g_acc, dsh1_acc, dsc1_acc = _rowwise(norm1_bwd_body, name="norm1_bwd", nblk=nb, tr=tr,
                                                 rows=[(xa, 0, d, 0), (dh1, 0, d, 0)], sels=[modv], fulls=[n1g],
                                                 outs=[(d, F32)], accs=[d, d, d], seg=nlb)
    grad_x = (dxa[:nl] + dx1)[None]

    red8 = lambda a: jnp.sum(a, axis=-2)
    dmod_own = jnp.concatenate([red8(dsh1_acc[0]), red8(dsc1_acc[0]), red8(dg1_acc), red8(dsh2_acc), red8(dsc2_acc), red8(dg2_acc)])
    dmod_ctx = jnp.concatenate([red8(dsh1_acc[1]), red8(dsc1_acc[1]), jnp.zeros((4 * d,), F32)])
    dm_in = jnp.concatenate([dmod_own[None, :], dmod_ctx[None, :], jnp.zeros((6, d6), F32)], axis=0)
    (dm_all,) = _exchange8([dm_in], "gather_dmod", True)
    dm_own_sh = lax.dynamic_slice(dm_all[:, 0, :], (0, chip * csh), (8, csh))
    dm_ctx_sh = lax.dynamic_slice(dm_all[:, 1, :], (0, chip * csh), (8, csh))

    def mod_bwd_body(c_ref, own_ref, ctx_ref, w_ref, gw_ref, gb_ref, gc_ref):
        cv = c_ref[...]
        a = _silu(cv).astype(MXU_DTYPE)
        own = own_ref[...]
        ctx_tot = ctx_ref[0:1, :]
        for j in range(1, 8):
            ctx_tot = ctx_tot + ctx_ref[j:j + 1, :]
        g16 = jnp.concatenate([own, jnp.broadcast_to(ctx_tot, own.shape)], axis=0)
        rid = lax.broadcasted_iota(jnp.int32, g16.shape, 0)
        g16 = jnp.where(rid <= 8, g16, 0.0)
        gw_ref[...] = lax.dot_general(a, g16.astype(MXU_DTYPE), (((0,), (0,)), ((), ())), preferred_element_type=F32)
        gb_ref[...] = jnp.broadcast_to(jnp.sum(own, axis=0, keepdims=True) + ctx_tot, gb_ref.shape)
        gc = lax.dot_general(jnp.broadcast_to(ctx_tot, own.shape).astype(MXU_DTYPE), w_ref[...].astype(MXU_DTYPE),
                             (((1,), (1,)), ((), ())), preferred_element_type=F32)
        gc_ref[...] = gc * _dsilu(cv[8:9, :])

    g_wmod, g_bmod_sh, g_cctx_part = pl.pallas_call(
        mod_bwd_body, name="mod_bwd", out_shape=[_sds((d, csh), F32), _sds((8, csh), F32), _sds((8, d), F32)],
        compiler_params=pltpu.CompilerParams(vmem_limit_bytes=VMEM_LIMIT))(cs16, dm_own_sh, dm_ctx_sh, w_mod[0])
    north = (mc == 0).astype(F32)
    g_bmod_part = lax.dynamic_update_slice(jnp.zeros((1, d6), F32), g_bmod_sh[0:1] * north, (0, chip * csh))
    g_cctx_part = g_cctx_part[0] * north

    small_g = {}
    for j, dr in enumerate(dirs):
        sfx = dr[0]
        _, dbbig_j, dlam_j = s5b[j]
        dl_re, dl_im = _from_cols(red8(dlam_j)[None, :])
        db_re, db_im = _from_cols(dbbig_j)
        cot = (dl_re.reshape(SSM_GROUPS, SSM_STATE), dl_im.reshape(SSM_GROUPS, SSM_STATE),
               _block_diag_extract(db_re), _block_diag_extract(db_im))
        g_lre, g_lim, g_ldt, g_bre, g_bim = disc_vjps[j](cot)
        small_g["lam_re_" + sfx], small_g["lam_im_" + sfx], small_g["log_dt_" + sfx] = g_lre, g_lim, g_ldt
        small_g["b_re"] = small_g.get("b_re", 0.0) + g_bre
        small_g["b_im"] = small_g.get("b_im", 0.0) + g_bim
        dc_re, dc_im = _from_cols(jnp.transpose(dcbig[j]))
        small_g["c_re_" + sfx] = jnp.transpose(_block_diag_extract(dc_re), (0, 2, 1))
        small_g["c_im_" + sfx] = -jnp.transpose(_block_diag_extract(dc_im), (0, 2, 1))
    head_fold = lambda acc: jnp.sum(red8(acc).reshape(N_HEADS, SLOT), axis=0)[:QK_DIM]
    small_g.update(c_ctx=g_cctx_part, b_mod=g_bmod_part[0], norm1_g=red8(dn1g_acc[0]) + red8(dn1g_acc[1]),
                   norm2_g=red8(dn2g_acc), q_a_g=red8(dqag_acc), kv_a_g=red8(dkvag_acc), q_norm_g=head_fold(dqg_acc),
                   k_norm_g=head_fold(dkg_acc), d_skip=red8(ddskip_acc), conv_b=red8(dcb_acc))
    g_convw_full = jnp.stack([red8(dcw0), red8(dcw1), red8(dcw2)])
    small_names = ["c_ctx", "b_mod", "norm1_g", "norm2_g", "q_a_g", "kv_a_g", "q_norm_g", "k_norm_g",
                   "lam_re_f", "lam_im_f", "log_dt_f", "c_re_f", "c_im_f", "lam_re_b", "lam_im_b", "log_dt_b",
                   "c_re_b", "c_im_b", "b_re", "b_im", "d_skip", "conv_b"]
    small_shapes = [weights[k].shape for k in small_names]
    spack = _pack([small_g[k] for k in small_names] + [g_convw_full], rows_mult=8)
    sred = _sum8(_exchange8([spack], "gather_small_grads", True)[0], "sum_small_grads")
    sg_list = _unpack(sred, small_shapes + [(3, f2)])
    g_small = dict(zip(small_names, sg_list[:-1]))
    g_small["conv_w"] = lax.dynamic_slice(sg_list[-1], (0, chip * cwid), (3, cwid))[None]

    gwi = g_win_p
    g_big["w_in"] = jnp.concatenate([gwi[:, q_off:q_off + Q_LORA], gwi[:, kv_off:kv_off + KV_LORA],
                                     gwi[:, kr_off:kr_off + QK_ROPE], gwi[:, u_off:u_off + SSM_WIDTH], gwi[:, :2 * d]], axis=1)
    g_big["w_uq"] = g_wuq_p.reshape(Q_LORA, N_HEADS, SLOT)[:, :, :QK_DIM].reshape(Q_LORA, N_HEADS * QK_DIM)
    gk3 = g_wukv_p[:, :hw].reshape(KV_LORA, N_HEADS, SLOT)[:, :, :QK_NOPE]
    gv3 = g_wukv_p[:, hw:].reshape(KV_LORA, N_HEADS, SLOT)[:, :, :V_DIM]
    g_big["w_ukv"] = jnp.concatenate([gk3, gv3], axis=2).reshape(KV_LORA, N_HEADS * (QK_NOPE + V_DIM))
    g_big["w_o_attn"] = g_wo_p.reshape(N_HEADS, SLOT, d)[:, :V_DIM].reshape(N_HEADS * V_DIM, d)

    def pieces(k_):
        r_, c_ = weights[k_].shape[1:]
        if k_ in row_sharded:
            p4 = jnp.transpose(g_big[k_].reshape(4, 2, r_ // 2, c_), (1, 0, 2, 3))
        else:
            p4 = jnp.transpose(g_big[k_].reshape(2, r_ // 2, 4, c_), (0, 2, 1, 3))
        return p4.reshape(2, 2 * r_, c_)

    pcs = [pieces(k_) for k_ in big_names]
    from_sibling = _sibling_send(pcs, "grads_to_sibling")
    my_half = []
    for k_, p_, got in zip(big_names, pcs, from_sibling):
        rows4, c_ = got.shape
        rh = rows4 // 4
        own = lax.dynamic_index_in_dim(p_, mc, 0, keepdims=False)
        tr_ = _pick(rows4, (256, 128, 64, 32, 16))
        s32, sb = _rowwise(lambda a, b: (a + b, a + b), name="sum_chip_" + k_, nblk=rows4 // tr_, tr=tr_,
                           rows=[(own, 0, c_, 0), (got, 0, c_, 0)], outs=[(c_, F32), (c_, MXU_DTYPE)])
        my_half.append((s32, sb, rh, c_))
    recv3 = _exchange_chips([sb.reshape(4, rh, c_) for _, sb, rh, c_ in my_half], "scatter_weight_grads", False)
    reduced = []
    for k_, (s32, _, rh, c_), r3 in zip(big_names, my_half, recv3):
        mine = lax.dynamic_slice(s32, (chip * rh, 0), (rh, c_))
        tr_ = _pick(rh, (256, 128, 64, 32, 16))
        (red,) = _rowwise(lambda a, b0, b1, b2: a + b0 + b1 + b2, name="sum_grad_" + k_, nblk=rh // tr_, tr=tr_,
                          rows=[(mine, 0, c_, 0)] + [(r3.reshape(3 * rh, c_), 0, c_, j * rh) for j in range(3)],
                          outs=[(c_, F32)])
        reduced.append(red)
    both = _sibling_exchange(reduced, "exchange_halves")
    g_sh = {k_: b_.reshape((1,) + weights[k_].shape[1:]) for k_, b_ in zip(big_names, both)}
    g_sh["w_mod"] = g_wmod[None]

    grads = {**g_sh, **g_small}
    outs_d, outs_m, outs_v = {}, {}, {}
    for k_ in ["w_mod"] + big_names:
        shp = weights[k_].shape
        res = _adamw(*[t.reshape(shp[1:]) for t in (grads[k_], weights[k_], mom_m[k_], mom_v[k_])], "adamw_" + k_)
        for dst, buf in zip((outs_d, outs_m, outs_v), res):
            dst[k_] = buf.reshape(shp)
    adam_small = small_names + ["conv_w"]
    shapes = [weights[k_].shape for k_ in adam_small]
    res = _adamw(*[_pack([src[k_] for k_ in adam_small], rows_mult=8) for src in (grads, weights, mom_m, mom_v)], "adamw_small")
    for dst, buf in zip((outs_d, outs_m, outs_v), res):
        dst.update(zip(adam_small, _unpack(buf, shapes)))
    grads = {k_: grads[k_].reshape(weights[k_].shape) for k_ in names}
    return (loss, grad_x, *[grads[k_] for k_ in names], *[outs_d[k_] for k_ in names],
            *[outs_m[k_] for k_ in names], *[outs_v[k_] for k_ in names])
```

```python
import functools
import math

import numpy as np
import jax
import jax.numpy as jnp
from jax import lax
from jax.experimental import pallas as pl
from jax.experimental.pallas import tpu as pltpu

F32 = jnp.float32
MXU_DTYPE = jnp.bfloat16
MESH = pl.DeviceIdType.MESH

EPS = 1e-6
N_HEADS = 8
QK_NOPE = 64
QK_ROPE = 32
QK_DIM = QK_NOPE + QK_ROPE
V_DIM = 64
SLOT = 128
Q_LORA = 384
KV_LORA = 256
GRID_W = 64
ROPE_THETA = 10000.0
SSM_WIDTH = 512
SSM_GROUP = 16
SSM_GROUPS = 32
SSM_STATE = 64
N_STATE = SSM_GROUPS * SSM_STATE
CG_STATES = 512
N_CG = N_STATE // CG_STATES
SCAN_LANES = 256
PACK_W = 1024

ADAM_LR = 0.001
ADAM_B1 = 0.9
ADAM_B2 = 0.999
ADAM_EPS = 1e-08
ADAM_WD = 0.01
ADAM_STEP = 10

VMEM_LIMIT = 56 * 1024 * 1024
LOG2E = 1.4426950408889634


def _pick(n, cands):
    for c in cands:
        if c <= n and n % c == 0:
            return c
    return n


def _cparams(sem):
    return pltpu.CompilerParams(dimension_semantics=sem, vmem_limit_bytes=VMEM_LIMIT)


def _sds(shape, dtype):
    return jax.ShapeDtypeStruct(tuple(shape), dtype)


_K_CANDS = (2816, 2048, 1536, 1408, 1280, 1152, 1024, 896, 768, 704, 640, 512, 384, 256, 128, 64, 32, 16)
_M_CANDS = (2048, 1408, 1024, 768, 512, 384, 256, 128, 64, 32, 16)
_N_CANDS = (1408, 1152, 1024, 768, 512, 384, 256, 128)
MM_VMEM_BUDGET = 40 * 1024 * 1024


def _mm_tiles(m, n, k_opts, a_bytes, b_bytes, o_bytes, m_cands):
    tn = n if n <= _N_CANDS[0] else _pick(n, _N_CANDS)
    for tk in k_opts:
        for tm in ((m,) if m <= m_cands[0] else ()) + tuple(t for t in m_cands if t < m and m % t == 0):
            if 2 * (tm * tk * a_bytes + tk * tn * b_bytes + tm * tn * o_bytes) + tm * tn * 4 <= MM_VMEM_BUDGET:
                return tm, tn, tk
    raise ValueError("no matmul tiling fits")


def _mm(a, b, mode, name, out_dtype=F32, rows=None, a_off=0, b_off=0):
    a_bytes, b_bytes, o_bytes = a.dtype.itemsize, b.dtype.itemsize, jnp.dtype(out_dtype).itemsize
    if mode == "tn":
        t_rows = rows or a.shape[0]
        m, n = a.shape[1], b.shape[1]
        k_opts = tuple(t for t in _K_CANDS if t <= t_rows and t_rows % t == 0) or (t_rows,)
        tm, tn, tk = _mm_tiles(m, n, k_opts, a_bytes, b_bytes, o_bytes, _M_CANDS[1:])
        nk = t_rows // tk
        ao, bo = a_off // tk, b_off // tk
        grid = (m // tm, n // tn, nk)
        in_specs = [pl.BlockSpec((tk, tm), lambda i, j, k: (k + ao, i)),
                    pl.BlockSpec((tk, tn), lambda i, j, k: (k + bo, j))]
        dn = (((0,), (0,)), ((), ()))
    else:
        m = rows or a.shape[0]
        kdim = a.shape[1]
        n = b.shape[1] if mode == "nn" else b.shape[0]
        k_opts = (kdim,) + tuple(t for t in _K_CANDS if t < kdim and kdim % t == 0)
        tm, tn, tk = _mm_tiles(m, n, k_opts, a_bytes, b_bytes, o_bytes, _M_CANDS)
        nk = kdim // tk
        ao = a_off // tm
        grid = (m // tm, n // tn, nk)
        if mode == "nn":
            in_specs = [pl.BlockSpec((tm, tk), lambda i, j, k: (i + ao, k)),
                        pl.BlockSpec((tk, tn), lambda i, j, k: (k, j))]
            dn = (((1,), (0,)), ((), ()))
        else:
            in_specs = [pl.BlockSpec((tm, tk), lambda i, j, k: (i + ao, k)),
                        pl.BlockSpec((tn, tk), lambda i, j, k: (j, k))]
            dn = (((1,), (1,)), ((), ()))
    use_scratch = nk > 1 and out_dtype != F32

    def body(a_ref, b_ref, o_ref, *scr):
        r = lax.dot_general(a_ref[...].astype(MXU_DTYPE), b_ref[...].astype(MXU_DTYPE), dn,
                            preferred_element_type=F32)
        if nk == 1:
            o_ref[...] = r.astype(o_ref.dtype)
        else:
            k = pl.program_id(2)
            acc = scr[0] if use_scratch else o_ref

            @pl.when(k == 0)
            def _():
                acc[...] = r

            @pl.when(k > 0)
            def _():
                acc[...] += r

            if use_scratch:
                @pl.when(k == nk - 1)
                def _():
                    o_ref[...] = acc[...].astype(o_ref.dtype)

    return pl.pallas_call(
        body, name=name, grid=grid, in_specs=in_specs,
        out_specs=pl.BlockSpec((tm, tn), lambda i, j, k: (i, j)),
        out_shape=_sds((m, n), out_dtype),
        scratch_shapes=[pltpu.VMEM((tm, tn), F32)] if use_scratch else [],
        compiler_params=_cparams(("parallel", "parallel", "arbitrary")),
    )(a, b)


def _rowwise(body, *, name, nblk, tr, rows=(), halo=(), sels=(), fulls=(), outs=(), accs=(), seg=None):
    n_rows, n_sel, n_full, n_out, n_acc = len(rows), len(sels), len(fulls), len(outs), len(accs)
    halo = tuple(halo)
    maxw = max([r[2] for r in rows] + [o[0] for o in outs] + list(accs))
    sr = _pick(tr, tuple(s for s in (256, 128, 64, 32, 16) if s * maxw <= 131072) or (16,))
    nsub = tr // sr
    total8 = nblk * tr // 8

    def seg_of(i):
        return jnp.where(i >= seg, 1, 0) if seg is not None else 0

    in_specs, operands = [], []
    for arr, cb, w, roff in rows:
        ob = roff // tr
        in_specs.append(pl.BlockSpec((tr, w), lambda i, cb=cb, ob=ob: (i + ob, cb)))
        operands.append(arr)
    for h in halo:
        arr, cb, w, roff = rows[h]
        o8, t8 = roff // 8, tr // 8
        in_specs.append(pl.BlockSpec((8, w), lambda i, cb=cb, o8=o8, t8=t8: (jnp.maximum(i * t8 - 1, 0) + o8, cb)))
        in_specs.append(pl.BlockSpec((8, w), lambda i, cb=cb, o8=o8, t8=t8: (jnp.minimum((i + 1) * t8, total8 - 1) + o8, cb)))
        operands += [arr, arr]
    for arr in sels:
        in_specs.append(pl.BlockSpec((None,) + arr.shape[1:], lambda i: (seg_of(i), 0, 0)))
        operands.append(arr)
    for arr in fulls:
        in_specs.append(pl.BlockSpec(arr.shape, lambda i: (0, 0)))
        operands.append(arr)
    out_specs, out_shape = [], []
    for w, dt in outs:
        out_specs.append(pl.BlockSpec((tr, w), lambda i: (i, 0)))
        out_shape.append(_sds((nblk * tr, w), dt))
    for w in accs:
        if seg is None:
            out_specs.append(pl.BlockSpec((8, w), lambda i: (0, 0)))
            out_shape.append(_sds((8, w), F32))
        else:
            out_specs.append(pl.BlockSpec((None, 8, w), lambda i: (seg_of(i), 0, 0)))
            out_shape.append(_sds((2, 8, w), F32))
    n_halo = 2 * len(halo)

    def kern(*refs):
        row_refs = refs[:n_rows]
        halo_refs = refs[n_rows:n_rows + n_halo]
        sel_refs = refs[n_rows + n_halo:n_rows + n_halo + n_sel]
        full_refs = refs[n_rows + n_halo + n_sel:n_rows + n_halo + n_sel + n_full]
        o0 = n_rows + n_halo + n_sel + n_full
        out_refs = refs[o0:o0 + n_out]
        acc_refs = refs[o0 + n_out:o0 + n_out + n_acc]
        i = pl.program_id(0)
        if n_acc:
            first = (i == 0) if seg is None else ((i == 0) | (i == seg))

            @pl.when(first)
            def _():
                for a_ref in acc_refs:
                    a_ref[...] = jnp.zeros(a_ref.shape, F32)

        def sub(s, carry):
            r0 = pl.multiple_of(s * sr, sr)
            vals = []
            for idx, r in enumerate(row_refs):
                cur = r[pl.ds(r0, sr), :]
                if idx in halo:
                    hp = halo_refs[2 * halo.index(idx)]
                    hn = halo_refs[2 * halo.index(idx) + 1]
                    cur = cur.astype(F32)
                    rid = lax.broadcasted_iota(jnp.int32, cur.shape, 0)
                    lo = r[pl.ds(pl.multiple_of(jnp.maximum(r0 - 8, 0), 8), 8), :].astype(F32)
                    lo = jnp.where(s == 0, hp[...].astype(F32), lo)
                    lo = jnp.where((s == 0) & (i == 0), 0.0, lo)
                    hi = r[pl.ds(pl.multiple_of(jnp.minimum(r0 + sr, tr - 8), 8), 8), :].astype(F32)
                    hi = jnp.where(s == nsub - 1, hn[...].astype(F32), hi)
                    hi = jnp.where((s == nsub - 1) & (i == nblk - 1), 0.0, hi)
                    prev = jnp.where(rid == 0, jnp.broadcast_to(lo[7:8, :], cur.shape), pltpu.roll(cur, 1, 0))
                    nxt = jnp.where(rid == sr - 1, jnp.broadcast_to(hi[0:1, :], cur.shape), pltpu.roll(cur, sr - 1, 0))
                    vals.append((prev, cur, nxt))
                else:
                    vals.append(cur)
            res = body(*vals, *[r[...] for r in sel_refs], *[r[...] for r in full_refs])
            if not isinstance(res, (tuple, list)):
                res = (res,)
            for o_ref, v in zip(out_refs, res[:n_out]):
                o_ref[pl.ds(r0, sr), :] = v.astype(o_ref.dtype)
            for a_ref, v in zip(acc_refs, res[n_out:]):
                a_ref[...] += jnp.sum(v.astype(F32).reshape(sr // 8, 8, v.shape[-1]), axis=0)
            return carry

        lax.fori_loop(0, nsub, sub, 0)

    res = pl.pallas_call(
        kern, name=name, grid=(nblk,), in_specs=in_specs, out_specs=out_specs, out_shape=out_shape,
        compiler_params=_cparams(("arbitrary",)),
    )(*operands)
    return res


def _sigmoid(x):
    return 1.0 / (1.0 + jnp.exp(-x))


def _silu(x):
    return x * _sigmoid(x)


def _dsilu(x):
    s = _sigmoid(x)
    return s * (1.0 + x * (1.0 - s))


_GELU_K = math.sqrt(2.0 / math.pi)


def _gelu(x):
    return 0.5 * x * (1.0 + jnp.tanh(_GELU_K * (x + 0.044715 * x * x * x)))


def _dgelu(x):
    t = jnp.tanh(_GELU_K * (x + 0.044715 * x * x * x))
    return 0.5 * (1.0 + t) + 0.5 * x * (1.0 - t * t) * _GELU_K * (1.0 + 3.0 * 0.044715 * x * x)


def _rms_fwd(x, g, width):
    r = lax.rsqrt(jnp.sum(x * x, axis=-1, keepdims=True) * (1.0 / width) + EPS)
    return x * r * g


def _rms_bwd(x, g, dy, width):
    r = lax.rsqrt(jnp.sum(x * x, axis=-1, keepdims=True) * (1.0 / width) + EPS)
    xn = x * r
    dyg = dy * g
    dx = r * (dyg - xn * (jnp.sum(dyg * xn, axis=-1, keepdims=True) * (1.0 / width)))
    return dx, dy * xn


def _rope_fwd(y, c, sa, sb):
    return y * c + pltpu.roll(y, SLOT - 16, 1) * sa + pltpu.roll(y, 16, 1) * sb


def _rope_bwd(d, c, sa, sb):
    return d * c + pltpu.roll(d * sa, 16, 1) + pltpu.roll(d * sb, SLOT - 16, 1)


def _heads(v):
    return [v[:, h * SLOT:(h + 1) * SLOT] for h in range(N_HEADS)]


def _attn_fwd(q, k, v, nl, scale):
    n = k.shape[0]
    tq = _pick(nl, (1024, 512, 256, 128))
    tk = _pick(n, (2816, 1408, 1152, 768, 384, 256, 128))
    sub = min(tq, 256)
    nk = n // tk
    rep = tk // SLOT
    c = scale * LOG2E

    def body(q_ref, k_ref, v_ref, o_ref, lse_ref, m_sc, l_sc, acc_sc):
        ki = pl.program_id(2)

        @pl.when(ki == 0)
        def _():
            m_sc[...] = jnp.full(m_sc.shape, -jnp.inf, F32)
            l_sc[...] = jnp.zeros(l_sc.shape, F32)
            acc_sc[...] = jnp.zeros(acc_sc.shape, F32)

        kb, vb = k_ref[...], v_ref[...]
        for sb in range(tq // sub):
            rows = slice(sb * sub, (sb + 1) * sub)
            s = lax.dot_general(q_ref[rows, :], kb, (((1,), (1,)), ((), ())), preferred_element_type=F32)
            m_prev = m_sc[rows, :]
            m_new = jnp.maximum(m_prev, jnp.max(s, axis=1, keepdims=True) * c)
            alpha = jnp.exp2(m_prev - m_new)
            p = jnp.exp2(s * c - jnp.tile(m_new, (1, rep)))
            l_sc[rows, :] = alpha * l_sc[rows, :] + jnp.sum(p, axis=1, keepdims=True)
            acc_sc[rows, :] = alpha * acc_sc[rows, :] + jnp.dot(p.astype(MXU_DTYPE), vb, preferred_element_type=F32)
            m_sc[rows, :] = m_new

        @pl.when(ki == nk - 1)
        def _():
            l = l_sc[...]
            o_ref[...] = (acc_sc[...] / l).astype(o_ref.dtype)
            lse_ref[...] = m_sc[...] + jnp.log2(l)

    return pl.pallas_call(
        body, name="attn_fwd", grid=(N_HEADS, nl // tq, nk),
        in_specs=[pl.BlockSpec((tq, SLOT), lambda h, i, j: (i, h)),
                  pl.BlockSpec((tk, SLOT), lambda h, i, j: (j, h)),
                  pl.BlockSpec((tk, SLOT), lambda h, i, j: (j, h))],
        out_specs=[pl.BlockSpec((tq, SLOT), lambda h, i, j: (i, h)),
                   pl.BlockSpec((tq, SLOT), lambda h, i, j: (i, h))],
        out_shape=[_sds((nl, N_HEADS * SLOT), MXU_DTYPE), _sds((nl, N_HEADS * SLOT), F32)],
        scratch_shapes=[pltpu.VMEM((tq, SLOT), F32), pltpu.VMEM((tq, SLOT), F32), pltpu.VMEM((tq, SLOT), F32)],
        compiler_params=_cparams(("parallel", "parallel", "arbitrary")),
    )(q, k, v)


def _attn_bwd(q, k, kt, v, do, lse_t, delta_t, nl, scale):
    n = k.shape[0]
    tq = _pick(nl, (1024, 512, 256, 128))
    tk = _pick(n, (2816, 1408, 1152, 768, 384, 256, 128))
    sub = _pick(tk, (256, 128))
    nq, nk = nl // tq, n // tk
    c = scale * LOG2E

    def body(q_ref, k_ref, kt_ref, v_ref, do_ref, lse_ref, dl_ref, dq_ref, dk_ref, dv_ref, dk_acc, dv_acc):
        ki, qi = pl.program_id(1), pl.program_id(2)

        @pl.when((ki == 0) & (qi == 0))
        def _():
            dq_ref[...] = jnp.zeros(dq_ref.shape, F32)

        @pl.when(qi == 0)
        def _():
            dk_acc[...] = jnp.zeros(dk_acc.shape, F32)
            dv_acc[...] = jnp.zeros(dv_acc.shape, F32)

        qb, dob = q_ref[...], do_ref[...]
        lse_r, dl_r = lse_ref[0:1, :], dl_ref[0:1, :]
        dq_part = None
        for sb in range(tk // sub):
            rows = slice(sb * sub, (sb + 1) * sub)
            s_t = lax.dot_general(k_ref[rows, :], qb, (((1,), (1,)), ((), ())), preferred_element_type=F32)
            p_t = jnp.exp2(s_t * c - lse_r)
            dp_t = lax.dot_general(v_ref[rows, :], dob, (((1,), (1,)), ((), ())), preferred_element_type=F32)
            ds_t = (p_t * (dp_t - dl_r) * scale).astype(MXU_DTYPE)
            dv_acc[rows, :] += jnp.dot(p_t.astype(MXU_DTYPE), dob, preferred_element_type=F32)
            dk_acc[rows, :] += jnp.dot(ds_t, qb, preferred_element_type=F32)
            part = jnp.dot(kt_ref[:, rows], ds_t, preferred_element_type=F32)
            dq_part = part if dq_part is None else dq_part + part
        c0 = pl.multiple_of(qi * tq, tq)
        dq_ref[:, pl.ds(c0, tq)] += dq_part

        @pl.when(qi == nq - 1)
        def _():
            dk_ref[...] = dk_acc[...]
            dv_ref[...] = dv_acc[...]

    return pl.pallas_call(
        body, name="attn_bwd", grid=(N_HEADS, nk, nq),
        in_specs=[pl.BlockSpec((tq, SLOT), lambda h, j, i: (i, h)),
                  pl.BlockSpec((tk, SLOT), lambda h, j, i: (j, h)),
                  pl.BlockSpec((SLOT, tk), lambda h, j, i: (h, j)),
                  pl.BlockSpec((tk, SLOT), lambda h, j, i: (j, h)),
                  pl.BlockSpec((tq, SLOT), lambda h, j, i: (i, h)),
                  pl.BlockSpec((None, 8, tq), lambda h, j, i: (h, 0, i)),
                  pl.BlockSpec((None, 8, tq), lambda h, j, i: (h, 0, i))],
        out_specs=[pl.BlockSpec((SLOT, nl), lambda h, j, i: (h, 0)),
                   pl.BlockSpec((tk, SLOT), lambda h, j, i: (j, h)),
                   pl.BlockSpec((tk, SLOT), lambda h, j, i: (j, h))],
        out_shape=[_sds((N_HEADS * SLOT, nl), F32), _sds((n, N_HEADS * SLOT), F32), _sds((n, N_HEADS * SLOT), F32)],
        scratch_shapes=[pltpu.VMEM((tk, SLOT), F32), pltpu.VMEM((tk, SLOT), F32)],
        compiler_params=_cparams(("arbitrary", "arbitrary", "arbitrary")),
    )(q, k, kt, v, do, lse_t, delta_t)


def _scan_consts(c_ref, lg):
    cs = slice(lg * SCAN_LANES, (lg + 1) * SCAN_LANES)
    return [c_ref[8 * kk:8 * kk + 8, cs] for kk in range(8)]


def _tile_scan(br, bi, consts, reverse):
    p1r, p1i, p2r, p2i, p4r, p4i = consts[:6]
    for pr, pi, kk in ((p1r, p1i, 1), (p2r, p2i, 2), (p4r, p4i, 4)):
        sh = (8 - kk) if reverse else kk
        sr_, si_ = pltpu.roll(br, sh, 0), pltpu.roll(bi, sh, 0)
        br, bi = br + pr * sr_ - pi * si_, bi + pr * si_ + pi * sr_
    return br, bi


def _seq_chunk(j, nch, nlc, reverse):
    return (nch - 1 - j) if reverse else (j + nlc) % nch


def _s5_scan(z, u_cb, bbig, lamc, t_rows, nl, reverse, name):
    n = z.shape[0]
    nch, nlc = n // t_rows, nl // t_rows
    ntile = t_rows // 8
    w = SCAN_LANES
    edge = 0 if reverse else 7

    def chunk(j):
        return _seq_chunk(j, nch, nlc, reverse)

    def body(u_ref, b_ref, c_ref, xs_ref, carry):
        j = pl.program_id(1)

        @pl.when(j == 0)
        def _():
            carry[...] = jnp.zeros(carry.shape, F32)

        xs_ref[...] = jnp.dot(u_ref[...].astype(MXU_DTYPE), b_ref[...], preferred_element_type=F32)
        for lg in range(CG_STATES // w):
            re = slice(lg * w, (lg + 1) * w)
            im = slice(CG_STATES + lg * w, CG_STATES + (lg + 1) * w)
            consts = _scan_consts(c_ref, lg)
            qr, qi = consts[6], consts[7]

            def tile(tt, st):
                cr, ci = st
                t = (ntile - 1 - tt) if reverse else tt
                r0 = pl.multiple_of(t * 8, 8)
                br, bi = _tile_scan(xs_ref[pl.ds(r0, 8), re], xs_ref[pl.ds(r0, 8), im], consts, reverse)
                lr = jnp.broadcast_to(cr[edge:edge + 1, :], br.shape)
                li = jnp.broadcast_to(ci[edge:edge + 1, :], bi.shape)
                xr = br + qr * lr - qi * li
                xi = bi + qr * li + qi * lr
                xs_ref[pl.ds(r0, 8), re] = xr
                xs_ref[pl.ds(r0, 8), im] = xi
                return xr, xi

            cr, ci = lax.fori_loop(0, ntile, tile, (carry[:, re], carry[:, im]))
            carry[:, re] = cr
            carry[:, im] = ci

    cw = 2 * CG_STATES
    return pl.pallas_call(
        body, name=name, grid=(N_CG, nch),
        in_specs=[pl.BlockSpec((t_rows, SSM_WIDTH), lambda g, j: (chunk(j), u_cb)),
                  pl.BlockSpec((SSM_WIDTH, cw), lambda g, j: (0, g)),
                  pl.BlockSpec((64, CG_STATES), lambda g, j: (0, g))],
        out_specs=pl.BlockSpec((t_rows, cw), lambda g, j: (chunk(j), g)),
        out_shape=_sds((n, 2 * N_STATE), F32),
        scratch_shapes=[pltpu.VMEM((8, cw), F32)],
        compiler_params=_cparams(("arbitrary", "arbitrary")),
    )(z, bbig, lamc)


def _s5_bwd(dys, z, u_cb, xs, cbig_t, bbig_t, lamc_adj, t_rows, nl, reverse, name):
    n = z.shape[0]
    nch, nlc = n // t_rows, nl // t_rows
    ntile = t_rows // 8
    t8 = t_rows // 8
    w = SCAN_LANES
    cw = 2 * CG_STATES
    adj_rev = not reverse
    edge = 0 if adj_rev else 7

    def chunk(j):
        return _seq_chunk(nch - 1 - j, nch, nlc, reverse)

    def halo_blk(j):
        if reverse:
            return jnp.minimum((chunk(j) + 1) * t8, n // 8 - 1)
        return (_seq_chunk(jnp.maximum(nch - 2 - j, 0), nch, nlc, False) + 1) * t8 - 1

    def body(dy_ref, u_ref, xs_ref, halo_ref, ct_ref, bt_ref, c_ref, du_ref, db_ref, dl_ref, gbuf, carry):
        j = pl.program_id(1)
        start = j == nch - 1

        @pl.when(j == 0)
        def _():
            carry[...] = jnp.zeros(carry.shape, F32)
            db_ref[...] = jnp.zeros(db_ref.shape, F32)
            dl_ref[...] = jnp.zeros(dl_ref.shape, F32)

        g0 = jnp.dot(dy_ref[...].astype(MXU_DTYPE), ct_ref[...], preferred_element_type=F32)
        gbuf[...] = jnp.where(chunk(j) < nlc, g0, 0.0)
        for lg in range(CG_STATES // w):
            re = slice(lg * w, (lg + 1) * w)
            im = slice(CG_STATES + lg * w, CG_STATES + (lg + 1) * w)
            consts = _scan_consts(c_ref, lg)
            qr, qi = consts[6], consts[7]
            hr, hi = halo_ref[:, re], halo_ref[:, im]

            def tile(tt, st):
                gcr, gci, ar, ai = st
                t = (ntile - 1 - tt) if adj_rev else tt
                r0 = pl.multiple_of(t * 8, 8)
                br, bi = _tile_scan(gbuf[pl.ds(r0, 8), re], gbuf[pl.ds(r0, 8), im], consts, adj_rev)
                lr = jnp.broadcast_to(gcr[edge:edge + 1, :], br.shape)
                li = jnp.broadcast_to(gci[edge:edge + 1, :], bi.shape)
                gr = br + qr * lr - qi * li
                gi = bi + qr * li + qi * lr
                gbuf[pl.ds(r0, 8), re] = gr
                gbuf[pl.ds(r0, 8), im] = gi
                xr, xi = xs_ref[pl.ds(r0, 8), re], xs_ref[pl.ds(r0, 8), im]
                rid = lax.broadcasted_iota(jnp.int32, xr.shape, 0)
                if reverse:
                    last = t == ntile - 1
                    rn = pl.multiple_of(jnp.minimum(r0 + 8, t_rows - 8), 8)
                    nbr = jnp.where(last, hr, xs_ref[pl.ds(rn, 8), re])
                    nbi = jnp.where(last, hi, xs_ref[pl.ds(rn, 8), im])
                    nbr = jnp.where(last & start, 0.0, nbr)
                    nbi = jnp.where(last & start, 0.0, nbi)
                    xpr = jnp.where(rid == 7, jnp.broadcast_to(nbr[0:1, :], xr.shape), pltpu.roll(xr, 7, 0))
                    xpi = jnp.where(rid == 7, jnp.broadcast_to(nbi[0:1, :], xi.shape), pltpu.roll(xi, 7, 0))
                else:
                    first = t == 0
                    rn = pl.multiple_of(jnp.maximum(r0 - 8, 0), 8)
                    nbr = jnp.where(first, hr, xs_ref[pl.ds(rn, 8), re])
                    nbi = jnp.where(first, hi, xs_ref[pl.ds(rn, 8), im])
                    nbr = jnp.where(first & start, 0.0, nbr)
                    nbi = jnp.where(first & start, 0.0, nbi)
                    xpr = jnp.where(rid == 0, jnp.broadcast_to(nbr[7:8, :], xr.shape), pltpu.roll(xr, 1, 0))
                    xpi = jnp.where(rid == 0, jnp.broadcast_to(nbi[7:8, :], xi.shape), pltpu.roll(xi, 1, 0))
                ar = ar + gr * xpr + gi * xpi
                ai = ai - gr * xpi + gi * xpr
                return gr, gi, ar, ai

            zz = jnp.zeros((8, w), F32)
            gcr, gci, ar, ai = lax.fori_loop(0, ntile, tile, (carry[:, re], carry[:, im], zz, zz))
            carry[:, re] = gcr
            carry[:, im] = gci
            dl_ref[:, re] += ar
            dl_ref[:, im] += ai
        g = gbuf[...].astype(MXU_DTYPE)
        du_ref[...] = jnp.dot(g, bt_ref[...], preferred_element_type=F32)
        db_ref[...] += lax.dot_general(u_ref[...].astype(MXU_DTYPE), g, (((0,), (0,)), ((), ())),
                                       preferred_element_type=F32)

    return pl.pallas_call(
        body, name=name, grid=(N_CG, nch),
        in_specs=[pl.BlockSpec((t_rows, SSM_WIDTH), lambda g, j: (jnp.minimum(chunk(j), nlc - 1), 0)),
                  pl.BlockSpec((t_rows, SSM_WIDTH), lambda g, j: (chunk(j), u_cb)),
                  pl.BlockSpec((t_rows, cw), lambda g, j: (chunk(j), g)),
                  pl.BlockSpec((8, cw), lambda g, j: (halo_blk(j), g)),
                  pl.BlockSpec((SSM_WIDTH, cw), lambda g, j: (0, g)),
                  pl.BlockSpec((cw, SSM_WIDTH), lambda g, j: (g, 0)),
                  pl.BlockSpec((64, CG_STATES), lambda g, j: (0, g))],
        out_specs=[pl.BlockSpec((None, t_rows, SSM_WIDTH), lambda g, j: (g, chunk(j), 0)),
                   pl.BlockSpec((SSM_WIDTH, cw), lambda g, j: (0, g)),
                   pl.BlockSpec((8, cw), lambda g, j: (0, g))],
        out_shape=[_sds((N_CG, n, SSM_WIDTH), F32), _sds((SSM_WIDTH, 2 * N_STATE), F32), _sds((8, 2 * N_STATE), F32)],
        scratch_shapes=[pltpu.VMEM((t_rows, cw), F32), pltpu.VMEM((8, cw), F32)],
        compiler_params=_cparams(("arbitrary", "arbitrary")),
    )(dys, z, xs, xs, cbig_t, bbig_t, lamc_adj)


def _to_cols(re_part, im_part):
    r = re_part.shape[0]
    parts = [t.reshape(r, N_CG, 1, CG_STATES) for t in (re_part, im_part)]
    return jnp.concatenate(parts, axis=2).reshape(r, 2 * N_STATE)


def _from_cols(d):
    r = d.shape[0]
    d4 = d.reshape(r, N_CG, 2, CG_STATES)
    return d4[:, :, 0].reshape(r, N_STATE), d4[:, :, 1].reshape(r, N_STATE)


def _group_mask():
    idx = jnp.arange(SSM_GROUPS)
    return (idx[:, None] == idx[None, :])[:, None, :, None]


def _block_diag(p_gnc):
    t = jnp.transpose(p_gnc, (0, 2, 1))[:, :, None, :]
    return jnp.where(_group_mask(), t, 0.0).reshape(SSM_GROUPS * SSM_GROUP, SSM_GROUPS * SSM_STATE)


def _block_diag_extract(d):
    d4 = d.reshape(SSM_GROUPS, SSM_GROUP, SSM_GROUPS, SSM_STATE)
    blk = jnp.sum(jnp.where(_group_mask(), d4, 0.0), axis=2)
    return jnp.transpose(blk, (0, 2, 1))


def _s5_disc(lam_re, lam_im, log_dt, b_re, b_im):
    lam = lax.complex(lam_re, lam_im)
    dt = jnp.exp(log_dt)[:, None]
    lam_bar = jnp.exp(lam * dt)
    b_bar = ((lam_bar - 1.0) / lam)[..., None] * lax.complex(b_re, b_im)
    return jnp.real(lam_bar), jnp.imag(lam_bar), jnp.real(b_bar), jnp.imag(b_bar)


def _lam_consts(lr, li, mirrored, conj):
    lam = lax.complex(lr.reshape(-1), -li.reshape(-1) if conj else li.reshape(-1))
    p2 = lam * lam
    p4 = p2 * p2
    pw = [lam, p2, p2 * lam, p4, p4 * lam, p4 * p2, p4 * p2 * lam, p4 * p4]
    rows = jnp.arange(8)[:, None]
    out = []
    for kk in (1, 2, 4):
        mask = (rows <= 7 - kk) if mirrored else (rows >= kk)
        pk = jnp.where(mask, pw[kk - 1][None, :], 0.0)
        out += [jnp.real(pk), jnp.imag(pk)]
    q = jnp.stack(pw[::-1] if mirrored else pw)
    return jnp.concatenate(out + [jnp.real(q), jnp.imag(q)], axis=0)


def _dev(t):
    return (t // 4, (t // 2) % 2, t % 2)


def _my_index():
    return 4 * lax.axis_index("x") + 2 * lax.axis_index("y") + lax.axis_index("c")


def _comm_call(body, name, arrs, lead, n_remote):
    nw = len(arrs)
    any_spec = pl.BlockSpec(memory_space=pl.ANY)
    return pl.pallas_call(
        body, name=name, out_shape=[_sds((lead,) + a.shape[-2:], a.dtype) for a in arrs],
        in_specs=[any_spec] * nw, out_specs=[any_spec] * nw,
        scratch_shapes=[pltpu.SemaphoreType.DMA((n_remote * nw,)), pltpu.SemaphoreType.DMA((n_remote * nw,)),
                        pltpu.SemaphoreType.DMA((2 * nw,))] + [pltpu.VMEM(a.shape[-2:], a.dtype) for a in arrs],
        compiler_params=pltpu.CompilerParams(vmem_limit_bytes=VMEM_LIMIT),
    )(*arrs)


class _LocalCopy:
    def __init__(self, src, dst, buf, sem_in, sem_out):
        self.fetch = pltpu.make_async_copy(src, buf, sem_in)
        self.store = pltpu.make_async_copy(buf, dst, sem_out)
        self.fetch.start()

    def forward(self):
        self.fetch.wait()
        self.store.start()

    def finish(self):
        self.store.wait()


def _exchange8(gs, name, same):
    nw = len(gs)

    def body(*refs):
        g_refs, o_refs, (ssem, rsem, lsem), bufs = refs[:nw], refs[nw:2 * nw], refs[2 * nw:2 * nw + 3], refs[2 * nw + 3:]
        me = _my_index()
        locs, sends = [], []
        for i, (g_ref, o_ref) in enumerate(zip(g_refs, o_refs)):
            src = (lambda t, g_ref=g_ref: g_ref) if same else (lambda t, g_ref=g_ref: g_ref.at[t])
            locs.append(_LocalCopy(src(me), o_ref.at[me], bufs[i], lsem.at[2 * i], lsem.at[2 * i + 1]))
            for d in range(1, 8):
                t = (me + d) % 8
                cp = pltpu.make_async_remote_copy(src_ref=src(t), dst_ref=o_ref.at[me], send_sem=ssem.at[7 * i + d - 1],
                                                  recv_sem=rsem.at[7 * i + d - 1], device_id=_dev(t), device_id_type=MESH)
                cp.start()
                sends.append(cp)
        for loc in locs:
            loc.forward()
        for i, (g_ref, o_ref) in enumerate(zip(g_refs, o_refs)):
            src = (lambda t, g_ref=g_ref: g_ref) if same else (lambda t, g_ref=g_ref: g_ref.at[t])
            for d in range(1, 8):
                s = (me + 8 - d) % 8
                pltpu.make_async_remote_copy(src_ref=src(s), dst_ref=o_ref.at[s], send_sem=ssem.at[7 * i + d - 1],
                                             recv_sem=rsem.at[7 * i + d - 1], device_id=_dev(s),
                                             device_id_type=MESH).wait_recv()
        for cp in sends:
            cp.wait_send()
        for loc in locs:
            loc.finish()

    return _comm_call(body, name, gs, 8, 7)


def _exchange_chips(ws, name, gather):
    nw = len(ws)

    def body(*refs):
        w_refs, o_refs, (ssem, rsem, lsem), bufs = refs[:nw], refs[nw:2 * nw], refs[2 * nw:2 * nw + 3], refs[2 * nw + 3:]
        x, y, cc = lax.axis_index("x"), lax.axis_index("y"), lax.axis_index("c")
        k = 2 * x + y
        peers = [(1 - x, y), (x, 1 - y), (1 - x, 1 - y)]
        locs, sends = [], []
        for i, (w_ref, o_ref) in enumerate(zip(w_refs, o_refs)):
            if gather:
                locs.append(_LocalCopy(w_ref.at[cc], o_ref.at[k], bufs[i], lsem.at[2 * i], lsem.at[2 * i + 1]))
            for j, (px, py) in enumerate(peers):
                src, dst = (w_ref.at[cc], o_ref.at[k]) if gather else (w_ref.at[2 * px + py], o_ref.at[j])
                cp = pltpu.make_async_remote_copy(src_ref=src, dst_ref=dst, send_sem=ssem.at[3 * i + j],
                                                  recv_sem=rsem.at[3 * i + j], device_id=(px, py, cc), device_id_type=MESH)
                cp.start()
                sends.append(cp)
        for loc in locs:
            loc.forward()
        for i, (w_ref, o_ref) in enumerate(zip(w_refs, o_refs)):
            for j, (px, py) in enumerate(peers):
                src, dst = (w_ref.at[cc], o_ref.at[2 * px + py]) if gather else (w_ref.at[k], o_ref.at[j])
                pltpu.make_async_remote_copy(src_ref=src, dst_ref=dst, send_sem=ssem.at[3 * i + j],
                                             recv_sem=rsem.at[3 * i + j], device_id=(px, py, cc),
                                             device_id_type=MESH).wait_recv()
        for cp in sends:
            cp.wait_send()
        for loc in locs:
            loc.finish()

    return _comm_call(body, name, ws, 4 if gather else 3, 3)


def _sibling_send(hs, name):
    nw = len(hs)

    def body(*refs):
        h_refs, o_refs, (ssem, rsem, lsem) = refs[:nw], refs[nw:2 * nw], refs[2 * nw:]
        x, y, cc = lax.axis_index("x"), lax.axis_index("y"), lax.axis_index("c")
        sends = []
        for i, (h_ref, o_ref) in enumerate(zip(h_refs, o_refs)):
            cp = pltpu.make_async_remote_copy(src_ref=h_ref.at[1 - cc], dst_ref=o_ref, send_sem=ssem.at[i],
                                              recv_sem=rsem.at[i], device_id=(x, y, 1 - cc), device_id_type=MESH)
            cp.start()
            sends.append(cp)
        for i, (h_ref, o_ref) in enumerate(zip(h_refs, o_refs)):
            pltpu.make_async_remote_copy(src_ref=h_ref.at[cc], dst_ref=o_ref, send_sem=ssem.at[i], recv_sem=rsem.at[i],
                                         device_id=(x, y, 1 - cc), device_id_type=MESH).wait_recv()
        for cp in sends:
            cp.wait_send()

    nw_spec = pl.BlockSpec(memory_space=pl.ANY)
    return pl.pallas_call(
        body, name=name, out_shape=[_sds(h.shape[1:], h.dtype) for h in hs],
        in_specs=[nw_spec] * nw, out_specs=[nw_spec] * nw,
        scratch_shapes=[pltpu.SemaphoreType.DMA((nw,)), pltpu.SemaphoreType.DMA((nw,)), pltpu.SemaphoreType.DMA((nw,))],
    )(*hs)


def _sibling_exchange(hs, name):
    nw = len(hs)

    def body(*refs):
        h_refs, o_refs, (ssem, rsem, lsem), bufs = refs[:nw], refs[nw:2 * nw], refs[2 * nw:2 * nw + 3], refs[2 * nw + 3:]
        x, y, cc = lax.axis_index("x"), lax.axis_index("y"), lax.axis_index("c")
        locs, sends = [], []
        for i, (h_ref, o_ref) in enumerate(zip(h_refs, o_refs)):
            locs.append(_LocalCopy(h_ref, o_ref.at[cc], bufs[i], lsem.at[2 * i], lsem.at[2 * i + 1]))
            cp = pltpu.make_async_remote_copy(src_ref=h_ref, dst_ref=o_ref.at[cc], send_sem=ssem.at[i], recv_sem=rsem.at[i],
                                              device_id=(x, y, 1 - cc), device_id_type=MESH)
            cp.start()
            sends.append(cp)
        for loc in locs:
            loc.forward()
        for i, (h_ref, o_ref) in enumerate(zip(h_refs, o_refs)):
            pltpu.make_async_remote_copy(src_ref=h_ref, dst_ref=o_ref.at[1 - cc], send_sem=ssem.at[i], recv_sem=rsem.at[i],
                                         device_id=(x, y, 1 - cc), device_id_type=MESH).wait_recv()
        for cp in sends:
            cp.wait_send()
        for loc in locs:
            loc.finish()

    return _comm_call(body, name, hs, 2, 1)


def _sum8(buf, name):
    _, r, c = buf.shape
    tr = _pick(r, (256, 128, 64, 32, 16, 8))
    flat = buf.reshape(8 * r, c)

    def body(*v):
        acc = v[0]
        for t in v[1:]:
            acc = acc + t
        return acc

    return _rowwise(body, name=name, nblk=r // tr, tr=tr, rows=[(flat, 0, c, s * r) for s in range(8)],
                    outs=[(c, F32)])[0]


def _pack(arrs, rows_mult=16):
    flat = jnp.concatenate([a.reshape(-1).astype(F32) for a in arrs])
    nel = flat.shape[0]
    r = -(-nel // PACK_W)
    r = -(-r // rows_mult) * rows_mult
    return jnp.pad(flat, (0, r * PACK_W - nel)).reshape(r, PACK_W)


def _unpack(buf, shapes):
    flat = buf.reshape(-1)
    out, o = [], 0
    for s in shapes:
        nel = int(np.prod(s))
        out.append(flat[o:o + nel].reshape(s))
        o += nel
    return out


def _adamw(g, w, m, v, name):
    r, wd = g.shape
    tr = _pick(r, tuple(t for t in (256, 128, 64, 32, 16, 8) if t * wd <= 262144) or (8,))
    c1 = 1.0 / (1.0 - ADAM_B1 ** ADAM_STEP)
    c2 = 1.0 / (1.0 - ADAM_B2 ** ADAM_STEP)

    def body(gv, wv, mv, vv):
        mn = ADAM_B1 * mv + (1.0 - ADAM_B1) * gv
        vn = ADAM_B2 * vv + (1.0 - ADAM_B2) * (gv * gv)
        delta = -ADAM_LR * ((mn * c1) / (jnp.sqrt(vn * c2) + ADAM_EPS) + ADAM_WD * wv)
        return delta, mn, vn

    return _rowwise(body, name=name, nblk=r // tr, tr=tr, rows=[(a, 0, wd, 0) for a in (g, w, m, v)],
                    outs=[(wd, F32)] * 3)


def kernel(x, c, ctx, c_ctx, w_mod, b_mod, norm1_g, norm2_g, w_in, q_a_g, w_uq, kv_a_g, w_ukv, q_norm_g, k_norm_g, w_o_attn, lam_re_f, lam_im_f, log_dt_f, c_re_f, c_im_f, lam_re_b, lam_im_b, log_dt_b, c_re_b, c_im_b, b_re, b_im, d_skip, w_glu, w_out, w_up, conv_w, conv_b, w_down, loss_target, m_c_ctx, m_w_mod, m_b_mod, m_norm1_g, m_norm2_g, m_w_in, m_q_a_g, m_w_uq, m_kv_a_g, m_w_ukv, m_q_norm_g, m_k_norm_g, m_w_o_attn, m_lam_re_f, m_lam_im_f, m_log_dt_f, m_c_re_f, m_c_im_f, m_lam_re_b, m_lam_im_b, m_log_dt_b, m_c_re_b, m_c_im_b, m_b_re, m_b_im, m_d_skip, m_w_glu, m_w_out, m_w_up, m_conv_w, m_conv_b, m_w_down, v_c_ctx, v_w_mod, v_b_mod, v_norm1_g, v_norm2_g, v_w_in, v_q_a_g, v_w_uq, v_kv_a_g, v_w_ukv, v_q_norm_g, v_k_norm_g, v_w_o_attn, v_lam_re_f, v_lam_im_f, v_log_dt_f, v_c_re_f, v_c_im_f, v_lam_re_b, v_lam_im_b, v_log_dt_b, v_c_re_b, v_c_im_b, v_b_re, v_b_im, v_d_skip, v_w_glu, v_w_out, v_w_up, v_conv_w, v_conv_b, v_w_down):
    weights = dict(c_ctx=c_ctx, w_mod=w_mod, b_mod=b_mod, norm1_g=norm1_g, norm2_g=norm2_g, w_in=w_in, q_a_g=q_a_g, w_uq=w_uq, kv_a_g=kv_a_g, w_ukv=w_ukv, q_norm_g=q_norm_g, k_norm_g=k_norm_g, w_o_attn=w_o_attn, lam_re_f=lam_re_f, lam_im_f=lam_im_f, log_dt_f=log_dt_f, c_re_f=c_re_f, c_im_f=c_im_f, lam_re_b=lam_re_b, lam_im_b=lam_im_b, log_dt_b=log_dt_b, c_re_b=c_re_b, c_im_b=c_im_b, b_re=b_re, b_im=b_im, d_skip=d_skip, w_glu=w_glu, w_out=w_out, w_up=w_up, conv_w=conv_w, conv_b=conv_b, w_down=w_down)
    mom_m = dict(c_ctx=m_c_ctx, w_mod=m_w_mod, b_mod=m_b_mod, norm1_g=m_norm1_g, norm2_g=m_norm2_g, w_in=m_w_in, q_a_g=m_q_a_g, w_uq=m_w_uq, kv_a_g=m_kv_a_g, w_ukv=m_w_ukv, q_norm_g=m_q_norm_g, k_norm_g=m_k_norm_g, w_o_attn=m_w_o_attn, lam_re_f=m_lam_re_f, lam_im_f=m_lam_im_f, log_dt_f=m_log_dt_f, c_re_f=m_c_re_f, c_im_f=m_c_im_f, lam_re_b=m_lam_re_b, lam_im_b=m_lam_im_b, log_dt_b=m_log_dt_b, c_re_b=m_c_re_b, c_im_b=m_c_im_b, b_re=m_b_re, b_im=m_b_im, d_skip=m_d_skip, w_glu=m_w_glu, w_out=m_w_out, w_up=m_w_up, conv_w=m_conv_w, conv_b=m_conv_b, w_down=m_w_down)
    mom_v = dict(c_ctx=v_c_ctx, w_mod=v_w_mod, b_mod=v_b_mod, norm1_g=v_norm1_g, norm2_g=v_norm2_g, w_in=v_w_in, q_a_g=v_q_a_g, w_uq=v_w_uq, kv_a_g=v_kv_a_g, w_ukv=v_w_ukv, q_norm_g=v_q_norm_g, k_norm_g=v_k_norm_g, w_o_attn=v_w_o_attn, lam_re_f=v_lam_re_f, lam_im_f=v_lam_im_f, log_dt_f=v_log_dt_f, c_re_f=v_c_re_f, c_im_f=v_c_im_f, lam_re_b=v_lam_re_b, lam_im_b=v_lam_im_b, log_dt_b=v_log_dt_b, c_re_b=v_c_re_b, c_im_b=v_c_im_b, b_re=v_b_re, b_im=v_b_im, d_skip=v_d_skip, w_glu=v_w_glu, w_out=v_w_out, w_up=v_w_up, conv_w=v_conv_w, conv_b=v_conv_b, w_down=v_w_down)
    names = list(weights)

    nl, d = x.shape[1], x.shape[2]
    nc = ctx.shape[1]
    n = nl + nc
    f2 = conv_b.shape[1]
    fh = f2 // 2
    d6 = b_mod.shape[1]
    mx, my, mc = lax.axis_index("x"), lax.axis_index("y"), lax.axis_index("c")
    chip = 2 * mx + my
    me = 4 * mx + 2 * my + mc
    tr = _pick(math.gcd(nl, nc), (256, 128, 64, 32, 16))
    nlb, nb = nl // tr, n // tr

    big_names = ["w_in", "w_uq", "w_ukv", "w_o_attn", "w_glu", "w_out", "w_up", "w_down"]
    row_sharded = ("w_out", "w_down")
    halves_in = [weights[k][0].astype(MXU_DTYPE).reshape(2, weights[k].shape[1] // 2, weights[k].shape[2]) for k in big_names]
    my_halves = _exchange_chips(halves_in, "gather_weights", True)
    gathered = _sibling_exchange([t.reshape(-1, t.shape[2]) for t in my_halves], "gather_weight_halves")
    full = {}
    for k_, gth in zip(big_names, gathered):
        r_, c_ = weights[k_].shape[1:]
        g4 = gth.reshape(2, 4, r_ // 2, c_)
        full[k_] = (jnp.transpose(g4, (1, 0, 2, 3)).reshape(4 * r_, c_) if k_ in row_sharded
                    else jnp.transpose(g4, (0, 2, 1, 3)).reshape(r_, 4 * c_))

    cwid = conv_w.shape[2]
    sw = -(-max(d, cwid) // 128) * 128
    small_in = jnp.concatenate([jnp.pad(c, ((0, 0), (0, sw - d))), jnp.pad(conv_w[0], ((0, 4), (0, sw - cwid)))], axis=0)
    (small_all,) = _exchange8([small_in], "gather_c", True)
    cs = small_all[:, 0, :d]
    conv_w_full = jnp.concatenate([small_all[2 * j, 1:4, :cwid] for j in range(4)], axis=1)
    cs16 = jnp.concatenate([cs, c_ctx[None, :], jnp.zeros((7, d), F32)], axis=0)

    csh = w_mod.shape[2]
    b_mod_sh = lax.dynamic_slice(b_mod, (0, chip * csh), (1, csh))

    def mod_fwd_body(c_ref, w_ref, b_ref, o_ref):
        a = _silu(c_ref[...]).astype(MXU_DTYPE)
        o_ref[...] = jnp.dot(a, w_ref[...].astype(MXU_DTYPE), preferred_element_type=F32) + b_ref[...]

    mod_sh = pl.pallas_call(mod_fwd_body, name="mod_fwd", out_shape=_sds((16, csh), F32),
                            compiler_params=pltpu.CompilerParams(vmem_limit_bytes=VMEM_LIMIT))(cs16, w_mod[0], b_mod_sh)
    (mod_all,) = _exchange8([mod_sh], "gather_mod", True)
    mod_full = jnp.concatenate([mod_all[2 * j] for j in range(4)], axis=1)
    modv = jnp.stack([lax.dynamic_slice(mod_full, (me, 0), (1, d6)), mod_full[8:9]])

    def mod_parts(m):
        return [m[:, j * d:(j + 1) * d] for j in range(6)]

    u_off, kv_off, kr_off = 2 * d, 2 * d + SSM_WIDTH, 2 * d + SSM_WIDTH + KV_LORA
    q_off = -(-(kr_off + SLOT) // Q_LORA) * Q_LORA
    zw = q_off + Q_LORA
    wi = full["w_in"]
    s0, s1, s2, s3 = Q_LORA, Q_LORA + KV_LORA, Q_LORA + KV_LORA + QK_ROPE, Q_LORA + KV_LORA + QK_ROPE + SSM_WIDTH
    zpad = lambda w_: jnp.zeros((d, w_), MXU_DTYPE)
    win_p = jnp.concatenate([wi[:, s3:], wi[:, s2:s3], wi[:, s0:s1], wi[:, s1:s2], zpad(SLOT - QK_ROPE),
                             zpad(q_off - kr_off - SLOT), wi[:, :s0]], axis=1)
    wuq_p = jnp.pad(full["w_uq"].reshape(Q_LORA, N_HEADS, QK_DIM), ((0, 0), (0, 0), (0, SLOT - QK_DIM))).reshape(Q_LORA, N_HEADS * SLOT)
    wukv3 = full["w_ukv"].reshape(KV_LORA, N_HEADS, QK_NOPE + V_DIM)
    padh = lambda t: jnp.pad(t, ((0, 0), (0, 0), (0, SLOT - t.shape[2]))).reshape(t.shape[0], N_HEADS * SLOT)
    wukv_p = jnp.concatenate([padh(wukv3[:, :, :QK_NOPE]), padh(wukv3[:, :, QK_NOPE:])], axis=1)
    wo_p = jnp.pad(full["w_o_attn"].reshape(N_HEADS, V_DIM, d), ((0, 0), (0, SLOT - V_DIM), (0, 0))).reshape(N_HEADS * SLOT, d)
    wglu, wout, wup, wdown = full["w_glu"], full["w_out"], full["w_up"], full["w_down"]
    hw = N_HEADS * SLOT
    gain_p = lambda g_: jnp.tile(jnp.pad(g_[0], (0, SLOT - QK_DIM)), N_HEADS)[None, :]
    qg_p, kg_p = gain_p(q_norm_g), gain_p(k_norm_g)

    tok = jnp.arange(nl)
    freqs = ROPE_THETA ** (-jnp.arange(QK_ROPE // 4, dtype=F32) / (QK_ROPE // 4))
    ang = jnp.concatenate([(tok // GRID_W)[:, None] * freqs, (tok % GRID_W)[:, None] * freqs], axis=-1)
    cos_t = jnp.concatenate([jnp.cos(ang), jnp.ones((nc, 16), F32)], axis=0)
    sin_t = jnp.concatenate([jnp.sin(ang), jnp.zeros((nc, 16), F32)], axis=0)
    zl = lambda w_: jnp.zeros((n, w_), F32)
    rope_c = jnp.concatenate([jnp.ones((n, QK_NOPE), F32), cos_t, cos_t, zl(SLOT - QK_DIM)], axis=1)
    rope_sa = jnp.concatenate([zl(QK_NOPE), -sin_t, zl(SLOT - QK_NOPE - 16)], axis=1)
    rope_sb = jnp.concatenate([zl(QK_NOPE + 16), sin_t, zl(SLOT - QK_DIM)], axis=1)

    dirs = (("f", lam_re_f, lam_im_f, log_dt_f, c_re_f, c_im_f, False), ("b", lam_re_b, lam_im_b, log_dt_b, c_re_b, c_im_b, True))
    bbig, cbig_t, cbig_n, bbig_t, lamc, lamc_adj, disc_vjps = [], [], [], [], [], [], []
    for _, l_re, l_im, l_dt, cr_, ci_, rev_ in dirs:
        (lbr, lbi, bbr, bbi), vjp = jax.vjp(_s5_disc, l_re[0], l_im[0], l_dt[0], b_re[0], b_im[0])
        disc_vjps.append(vjp)
        bb = _to_cols(_block_diag(bbr), _block_diag(bbi)).astype(MXU_DTYPE)
        cc_ = _to_cols(_block_diag(jnp.transpose(cr_[0], (0, 2, 1))),
                       -_block_diag(jnp.transpose(ci_[0], (0, 2, 1)))).astype(MXU_DTYPE)
        bbig.append(bb)
        bbig_t.append(jnp.transpose(bb))
        cbig_t.append(cc_)
        cbig_n.append(jnp.transpose(cc_))
        lamc.append(_lam_consts(lbr, lbi, rev_, False))
        lamc_adj.append(_lam_consts(lbr, lbi, not rev_, True))
    t_scan = tr

    xa = jnp.concatenate([x[0], ctx[0]], axis=0)
    n1g, n2g = norm1_g, norm2_g

    def norm1_body(xv, m, g):
        sh1, sc1 = m[:, :d], m[:, d:2 * d]
        return _rms_fwd(xv, g, d) * (1.0 + sc1) + sh1

    (h1,) = _rowwise(norm1_body, name="norm1_fwd", nblk=nb, tr=tr, rows=[(xa, 0, d, 0)], sels=[modv], fulls=[n1g],
                     outs=[(d, MXU_DTYPE)], seg=nlb)
    z = _mm(h1, win_p, "nn", "in_proj")
    gl_cb, u_cb, kv_cb, kr_cb, q_cb = 0, u_off // SSM_WIDTH, kv_off // KV_LORA, kr_off // SLOT, q_off // Q_LORA

    (cqn,) = _rowwise(lambda v, g: _rms_fwd(v, g, Q_LORA), name="qa_norm_fwd", nblk=nlb, tr=tr,
                      rows=[(z, q_cb, Q_LORA, 0)], fulls=[q_a_g], outs=[(Q_LORA, MXU_DTYPE)])
    qh = _mm(cqn, wuq_p, "nn", "q_up")

    def qhead_body(qv, cv, sav, sbv, g):
        return jnp.concatenate([_rope_fwd(_rms_fwd(t, g[:, :SLOT], QK_DIM), cv, sav, sbv) for t in _heads(qv)], axis=1)

    rope_rows = lambda: [(rope_c, 0, SLOT, 0), (rope_sa, 0, SLOT, 0), (rope_sb, 0, SLOT, 0)]
    (q_p,) = _rowwise(qhead_body, name="q_head_fwd", nblk=nlb, tr=tr, rows=[(qh, 0, hw, 0)] + rope_rows(),
                      fulls=[qg_p], outs=[(hw, MXU_DTYPE)])

    (ckvn,) = _rowwise(lambda v, g: _rms_fwd(v, g, KV_LORA), name="kva_norm_fwd", nblk=nb, tr=tr,
                       rows=[(z, kv_cb, KV_LORA, 0)], fulls=[kv_a_g], outs=[(KV_LORA, MXU_DTYPE)])
    kvpre = _mm(ckvn, wukv_p, "nn", "kv_up")

    def khead_body(kv_, vv_, krv, cv, sav, sbv, g):
        kpe = pltpu.roll(krv, QK_NOPE, 1)
        ks = [_rope_fwd(_rms_fwd(t + kpe, g[:, :SLOT], QK_DIM), cv, sav, sbv) for t in _heads(kv_)]
        return jnp.concatenate(ks, axis=1), vv_

    k_p, v_p = _rowwise(khead_body, name="k_head_fwd", nblk=nb, tr=tr,
                        rows=[(kvpre, 0, hw, 0), (kvpre, 1, hw, 0), (z, kr_cb, SLOT, 0)] + rope_rows(),
                        fulls=[kg_p], outs=[(hw, MXU_DTYPE), (hw, MXU_DTYPE)])

    scale = QK_DIM ** -0.5
    o_p, lse = _attn_fwd(q_p, k_p, v_p, nl, scale)
    a_l = _mm(o_p, wo_p, "nn", "attn_out")

    xs = [_s5_scan(z, u_cb, bbig[j], lamc[j], t_scan, nl, dirs[j][6], "s5_scan_" + dirs[j][0]) for j in range(2)]
    ydir = [_mm(xs[j], cbig_n[j], "nn", "s5_read_" + dirs[j][0], rows=nl) for j in range(2)]

    def ssm_out_body(uv, a, b, dsk):
        ys = uv * dsk + a + b
        return ys, _gelu(ys)

    ys, ge = _rowwise(ssm_out_body, name="s5_out_fwd", nblk=nlb, tr=tr,
                      rows=[(z, u_cb, SSM_WIDTH, 0), (ydir[0], 0, SSM_WIDTH, 0), (ydir[1], 0, SSM_WIDTH, 0)],
                      fulls=[d_skip], outs=[(SSM_WIDTH, F32), (SSM_WIDTH, MXU_DTYPE)])
    glu_out = _mm(ge, wglu, "nn", "glu_proj")

    def merge_body(ga, gs, av, val, gate):
        return _sigmoid(ga) * av + _sigmoid(gs) * (val * _sigmoid(gate))

    merge_rows = lambda: [(z, 0, d, 0), (z, 1, d, 0), (a_l, 0, d, 0), (glu_out, 0, d, 0), (glu_out, 1, d, 0)]
    (merged,) = _rowwise(merge_body, name="merge_fwd", nblk=nlb, tr=tr, rows=merge_rows(), outs=[(d, MXU_DTYPE)])
    mo = _mm(merged, wout, "nn", "out_proj")
    mod_x = modv[0]

    def norm2_body(xv, mov, m, g):
        g1, sh2, sc2 = m[:, 2 * d:3 * d], m[:, 3 * d:4 * d], m[:, 4 * d:5 * d]
        x1v = xv + g1 * mov
        return x1v, _rms_fwd(x1v, g, d) * (1.0 + sc2) + sh2

    x1, h2 = _rowwise(norm2_body, name="norm2_fwd", nblk=nlb, tr=tr, rows=[(xa, 0, d, 0), (mo, 0, d, 0)],
                      fulls=[mod_x, n2g], outs=[(d, F32), (d, MXU_DTYPE)])
    up = _mm(h2, wup, "nn", "ffn_up")
    cw8 = jnp.zeros((8, f2), F32).at[:3].set(conv_w_full)

    def conv3(t3, w8, off):
        p_, c_, n_ = t3
        return p_ * w8[0:1, off:off + fh] + c_ * w8[1:2, off:off + fh] + n_ * w8[2:3, off:off + fh]

    def conv_fwd_body(val3, gate3, w8, bias):
        val2 = conv3(val3, w8, 0) + bias[:, :fh]
        gate2 = conv3(gate3, w8, fh) + bias[:, fh:]
        return _silu(gate2) * val2

    (act,) = _rowwise(conv_fwd_body, name="conv_fwd", nblk=nlb, tr=tr, rows=[(up, 0, fh, 0), (up, 1, fh, 0)],
                      halo=(0, 1), fulls=[cw8, conv_b], outs=[(fh, MXU_DTYPE)])
    dn = _mm(act, wdown, "nn", "ffn_down")
    tgt = loss_target[0]

    def loss_body(x1v, dnv, tv, m):
        g2 = m[:, 5 * d:6 * d]
        e = x1v + g2 * dnv - tv
        dx2v = e * (1.0 / d)
        return dx2v, dx2v * g2, e * e, dx2v * dnv

    dx2, ddn, loss_acc, dg2_acc = _rowwise(loss_body, name="loss", nblk=nlb, tr=tr,
                                           rows=[(x1, 0, d, 0), (dn, 0, d, 0), (tgt, 0, d, 0)], fulls=[mod_x],
                                           outs=[(d, F32), (d, MXU_DTYPE)], accs=[d, d])
    loss = lax.psum(0.5 / d * jnp.sum(loss_acc), ("x", "y", "c"))

    g_big = {}
    dact = _mm(ddn, wdown, "nt", "ffn_down_dx")
    g_big["w_down"] = _mm(act, ddn, "tn", "ffn_down_dw")

    def conv_bwd_body(val3, gate3, da, w8, bias):
        val2 = conv3(val3, w8, 0) + bias[:, :fh]
        gate2 = conv3(gate3, w8, fh) + bias[:, fh:]
        dval2 = da * _silu(gate2)
        dgate2 = da * val2 * _dsilu(gate2)
        du2 = jnp.concatenate([dval2, dgate2], axis=1)
        taps = [jnp.concatenate([dval2 * val3[j], dgate2 * gate3[j]], axis=1) for j in range(3)]
        return du2, du2, taps[0], taps[1], taps[2]

    du2, dcb_acc, dcw0, dcw1, dcw2 = _rowwise(conv_bwd_body, name="conv_bwd", nblk=nlb, tr=tr,
                                              rows=[(up, 0, fh, 0), (up, 1, fh, 0), (dact, 0, fh, 0)], halo=(0, 1),
                                              fulls=[cw8, conv_b], outs=[(f2, F32)], accs=[f2, f2, f2, f2])

    def conv_t_body(dval3, dgate3, w8):
        rev = lambda t3: (t3[2], t3[1], t3[0])
        return jnp.concatenate([conv3(rev(dval3), w8, 0), conv3(rev(dgate3), w8, fh)], axis=1)

    (dup,) = _rowwise(conv_t_body, name="conv_bwd_dx", nblk=nlb, tr=tr, rows=[(du2, 0, fh, 0), (du2, 1, fh, 0)],
                      halo=(0, 1), fulls=[cw8], outs=[(f2, MXU_DTYPE)])
    dh2 = _mm(dup, wup, "nt", "ffn_up_dx")
    g_big["w_up"] = _mm(h2, dup, "tn", "ffn_up_dw")

    def norm2_bwd_body(x1v, dh, dx2v, mov, m, g):
        g1, sc2 = m[:, 2 * d:3 * d], m[:, 4 * d:5 * d]
        y = _rms_fwd(x1v, g, d)
        dxn, dgc = _rms_bwd(x1v, g, dh * (1.0 + sc2), d)
        dx1v = dx2v + dxn
        return dx1v, dx1v * g1, dgc, dh, dh * y, dx1v * mov

    dx1, dmo, dn2g_acc, dsh2_acc, dsc2_acc, dg1_acc = _rowwise(
        norm2_bwd_body, name="norm2_bwd", nblk=nlb, tr=tr,
        rows=[(x1, 0, d, 0), (dh2, 0, d, 0), (dx2, 0, d, 0), (mo, 0, d, 0)], fulls=[mod_x, n2g],
        outs=[(d, F32), (d, MXU_DTYPE)], accs=[d, d, d, d])
    dmerged = _mm(dmo, wout, "nt", "out_proj_dx")
    g_big["w_out"] = _mm(merged, dmo, "tn", "out_proj_dw")

    def merge_bwd_body(ga, gs, av, val, gate, dm):
        sa_, ss_, sg_ = _sigmoid(ga), _sigmoid(gs), _sigmoid(gate)
        s_l = val * sg_
        ds_l = dm * ss_
        dga = dm * av * sa_ * (1.0 - sa_)
        dgs = dm * s_l * ss_ * (1.0 - ss_)
        dval = ds_l * sg_
        dgate = ds_l * val * sg_ * (1.0 - sg_)
        return dm * sa_, jnp.concatenate([dval, dgate], axis=1), jnp.concatenate([dga, dgs], axis=1)

    da_l, dglu, dgl = _rowwise(merge_bwd_body, name="merge_bwd", nblk=nlb, tr=tr,
                               rows=merge_rows() + [(dmerged, 0, d, 0)],
                               outs=[(d, MXU_DTYPE), (2 * d, MXU_DTYPE), (2 * d, MXU_DTYPE)])
    dge = _mm(dglu, wglu, "nt", "glu_proj_dx")
    g_big["w_glu"] = _mm(ge, dglu, "tn", "glu_proj_dw")

    def ssm_out_bwd_body(ysv, dgev, uv, dsk):
        dys_ = dgev * _dgelu(ysv)
        return dys_, dys_ * dsk, dys_ * uv

    dys, du_skip, ddskip_acc = _rowwise(ssm_out_bwd_body, name="s5_out_bwd", nblk=nlb, tr=tr,
                                        rows=[(ys, 0, SSM_WIDTH, 0), (dge, 0, SSM_WIDTH, 0), (z, u_cb, SSM_WIDTH, 0)],
                                        fulls=[d_skip], outs=[(SSM_WIDTH, F32), (SSM_WIDTH, F32)], accs=[SSM_WIDTH])
    s5b = [_s5_bwd(dys, z, u_cb, xs[j], cbig_t[j], bbig_t[j], lamc_adj[j], t_scan, nl, dirs[j][6], "s5_bwd_" + dirs[j][0])
           for j in range(2)]
    dcbig = [_mm(xs[j], dys, "tn", "s5_read_dw_" + dirs[j][0], rows=nl) for j in range(2)]
    du_nat = jnp.sum(s5b[0][0], axis=0) + jnp.sum(s5b[1][0], axis=0)
    du_nat = du_nat + jnp.concatenate([du_skip, jnp.zeros((nc, SSM_WIDTH), F32)], axis=0)

    do_f = _mm(da_l, wo_p, "nt", "attn_out_dx")
    g_wo_p = _mm(o_p, da_l, "tn", "attn_out_dw")

    def delta_body(dov, ov):
        prod = dov * ov.astype(F32)
        dl = [jnp.broadcast_to(jnp.sum(t, axis=-1, keepdims=True), t.shape) for t in _heads(prod)]
        return dov, jnp.concatenate(dl, axis=1)

    do_b, delta = _rowwise(delta_body, name="attn_delta", nblk=nlb, tr=tr, rows=[(do_f, 0, hw, 0), (o_p, 0, hw, 0)],
                           outs=[(hw, MXU_DTYPE), (hw, F32)])
    to_rows = lambda t: jnp.broadcast_to(jnp.transpose(t[:, ::SLOT])[:, None, :], (N_HEADS, 8, nl))
    dq_t, dk_p, dv_p = _attn_bwd(q_p, k_p, jnp.transpose(k_p), v_p, do_b, to_rows(lse), to_rows(delta), nl, scale)
    dq_p = jnp.transpose(dq_t)

    def qhead_bwd_body(qv, dqv, cv, sav, sbv, g):
        dxs, dgs = [], []
        for t, dt_ in zip(_heads(qv), _heads(dqv)):
            dx_, dg_ = _rms_bwd(t, g[:, :SLOT], _rope_bwd(dt_, cv, sav, sbv), QK_DIM)
            dxs.append(dx_)
            dgs.append(dg_)
        return jnp.concatenate(dxs, axis=1), jnp.concatenate(dgs, axis=1)

    dqh, dqg_acc = _rowwise(qhead_bwd_body, name="q_head_bwd", nblk=nlb, tr=tr,
                            rows=[(qh, 0, hw, 0), (dq_p, 0, hw, 0)] + rope_rows(), fulls=[qg_p],
                            outs=[(hw, MXU_DTYPE)], accs=[hw])
    dcqn = _mm(dqh, wuq_p, "nt", "q_up_dx")
    g_wuq_p = _mm(cqn, dqh, "tn", "q_up_dw")
    dcq, dqag_acc = _rowwise(lambda v, dy, g: _rms_bwd(v, g, dy, Q_LORA), name="qa_norm_bwd", nblk=nlb, tr=tr,
                             rows=[(z, q_cb, Q_LORA, 0), (dcqn, 0, Q_LORA, 0)], fulls=[q_a_g],
                             outs=[(Q_LORA, MXU_DTYPE)], accs=[Q_LORA])

    def khead_bwd_body(kv_, krv, dkv_, dvv_, cv, sav, sbv, g):
        kpe = pltpu.roll(krv, QK_NOPE, 1)
        lane = lax.broadcasted_iota(jnp.int32, krv.shape, 1)
        dxs, dgs, dkr_ = [], [], jnp.zeros(krv.shape, F32)
        for t, dt_ in zip(_heads(kv_), _heads(dkv_)):
            dx_, dg_ = _rms_bwd(t + kpe, g[:, :SLOT], _rope_bwd(dt_, cv, sav, sbv), QK_DIM)
            dxs.append(jnp.where(lane < QK_NOPE, dx_, 0.0))
            dgs.append(dg_)
            dkr_ = dkr_ + dx_
        dkr_ = jnp.where(lane < QK_ROPE, pltpu.roll(dkr_, SLOT - QK_NOPE, 1), 0.0)
        return jnp.concatenate(dxs + [dvv_], axis=1), dkr_, jnp.concatenate(dgs, axis=1)

    dkvpre, dkr, dkg_acc = _rowwise(khead_bwd_body, name="k_head_bwd", nblk=nb, tr=tr,
                                    rows=[(kvpre, 0, hw, 0), (z, kr_cb, SLOT, 0), (dk_p, 0, hw, 0), (dv_p, 0, hw, 0)] + rope_rows(),
                                    fulls=[kg_p], outs=[(2 * hw, MXU_DTYPE), (SLOT, MXU_DTYPE)], accs=[hw])
    dckvn = _mm(dkvpre, wukv_p, "nt", "kv_up_dx")
    g_wukv_p = _mm(ckvn, dkvpre, "tn", "kv_up_dw")
    dckv, dkvag_acc = _rowwise(lambda v, dy, g: _rms_bwd(v, g, dy, KV_LORA), name="kva_norm_bwd", nblk=nb, tr=tr,
                               rows=[(z, kv_cb, KV_LORA, 0), (dckvn, 0, KV_LORA, 0)], fulls=[kv_a_g],
                               outs=[(KV_LORA, MXU_DTYPE)], accs=[KV_LORA])

    padc = lambda t: jnp.concatenate([t, jnp.zeros((nc, t.shape[1]), t.dtype)], axis=0)
    dz = jnp.concatenate([padc(dgl), du_nat.astype(MXU_DTYPE), dckv, dkr,
                          jnp.zeros((n, q_off - kr_off - SLOT), MXU_DTYPE), padc(dcq)], axis=1)
    dh1 = _mm(dz, win_p, "nt", "in_proj_dx")
    g_win_p = _mm(h1, dz, "tn", "in_proj_dw")

    def norm1_bwd_body(xv, dh, m, g):
        sc1 = m[:, d:2 * d]
        y = _rms_fwd(xv, g, d)
        dxn, dgc = _rms_bwd(xv, g, dh * (1.0 + sc1), d)
        return dxn, dgc, dh, dh * y

    dxa, dn1g_acc, dsh1_acc, dsc1_acc = _rowwise(norm1_bwd_body, name="norm1_bwd", nblk=nb, tr=tr,
                                                 rows=[(xa, 0, d, 0), (dh1, 0, d, 0)], sels=[modv], fulls=[n1g],
                                                 outs=[(d, F32)], accs=[d, d, d], seg=nlb)
    grad_x = (dxa[:nl] + dx1)[None]

    red8 = lambda a: jnp.sum(a, axis=-2)
    dmod_own = jnp.concatenate([red8(dsh1_acc[0]), red8(dsc1_acc[0]), red8(dg1_acc), red8(dsh2_acc), red8(dsc2_acc), red8(dg2_acc)])
    dmod_ctx = jnp.concatenate([red8(dsh1_acc[1]), red8(dsc1_acc[1]), jnp.zeros((4 * d,), F32)])
    dm_in = jnp.concatenate([dmod_own[None, :], dmod_ctx[None, :], jnp.zeros((6, d6), F32)], axis=0)
    (dm_all,) = _exchange8([dm_in], "gather_dmod", True)
    dm_own_sh = lax.dynamic_slice(dm_all[:, 0, :], (0, chip * csh), (8, csh))
    dm_ctx_sh = lax.dynamic_slice(dm_all[:, 1, :], (0, chip * csh), (8, csh))

    def mod_bwd_body(c_ref, own_ref, ctx_ref, w_ref, gw_ref, gb_ref, gc_ref):
        cv = c_ref[...]
        a = _silu(cv).astype(MXU_DTYPE)
        own = own_ref[...]
        ctx_tot = ctx_ref[0:1, :]
        for j in range(1, 8):
            ctx_tot = ctx_tot + ctx_ref[j:j + 1, :]
        g16 = jnp.concatenate([own, jnp.broadcast_to(ctx_tot, own.shape)], axis=0)
        rid = lax.broadcasted_iota(jnp.int32, g16.shape, 0)
        g16 = jnp.where(rid <= 8, g16, 0.0)
        gw_ref[...] = lax.dot_general(a, g16.astype(MXU_DTYPE), (((0,), (0,)), ((), ())), preferred_element_type=F32)
        gb_ref[...] = jnp.broadcast_to(jnp.sum(own, axis=0, keepdims=True) + ctx_tot, gb_ref.shape)
        gc = lax.dot_general(jnp.broadcast_to(ctx_tot, own.shape).astype(MXU_DTYPE), w_ref[...].astype(MXU_DTYPE),
                             (((1,), (1,)), ((), ())), preferred_element_type=F32)
        gc_ref[...] = gc * _dsilu(cv[8:9, :])

    g_wmod, g_bmod_sh, g_cctx_part = pl.pallas_call(
        mod_bwd_body, name="mod_bwd", out_shape=[_sds((d, csh), F32), _sds((8, csh), F32), _sds((8, d), F32)],
        compiler_params=pltpu.CompilerParams(vmem_limit_bytes=VMEM_LIMIT))(cs16, dm_own_sh, dm_ctx_sh, w_mod[0])
    north = (mc == 0).astype(F32)
    g_bmod_part = lax.dynamic_update_slice(jnp.zeros((1, d6), F32), g_bmod_sh[0:1] * north, (0, chip * csh))
    g_cctx_part = g_cctx_part[0] * north

    small_g = {}
    for j, dr in enumerate(dirs):
        sfx = dr[0]
        _, dbbig_j, dlam_j = s5b[j]
        dl_re, dl_im = _from_cols(red8(dlam_j)[None, :])
        db_re, db_im = _from_cols(dbbig_j)
        cot = (dl_re.reshape(SSM_GROUPS, SSM_STATE), dl_im.reshape(SSM_GROUPS, SSM_STATE),
               _block_diag_extract(db_re), _block_diag_extract(db_im))
        g_lre, g_lim, g_ldt, g_bre, g_bim = disc_vjps[j](cot)
        small_g["lam_re_" + sfx], small_g["lam_im_" + sfx], small_g["log_dt_" + sfx] = g_lre, g_lim, g_ldt
        small_g["b_re"] = small_g.get("b_re", 0.0) + g_bre
        small_g["b_im"] = small_g.get("b_im", 0.0) + g_bim
        dc_re, dc_im = _from_cols(jnp.transpose(dcbig[j]))
        small_g["c_re_" + sfx] = jnp.transpose(_block_diag_extract(dc_re), (0, 2, 1))
        small_g["c_im_" + sfx] = -jnp.transpose(_block_diag_extract(dc_im), (0, 2, 1))
    head_fold = lambda acc: jnp.sum(red8(acc).reshape(N_HEADS, SLOT), axis=0)[:QK_DIM]
    small_g.update(c_ctx=g_cctx_part, b_mod=g_bmod_part[0], norm1_g=red8(dn1g_acc[0]) + red8(dn1g_acc[1]),
                   norm2_g=red8(dn2g_acc), q_a_g=red8(dqag_acc), kv_a_g=red8(dkvag_acc), q_norm_g=head_fold(dqg_acc),
                   k_norm_g=head_fold(dkg_acc), d_skip=red8(ddskip_acc), conv_b=red8(dcb_acc))
    g_convw_full = jnp.stack([red8(dcw0), red8(dcw1), red8(dcw2)])
    small_names = ["c_ctx", "b_mod", "norm1_g", "norm2_g", "q_a_g", "kv_a_g", "q_norm_g", "k_norm_g",
                   "lam_re_f", "lam_im_f", "log_dt_f", "c_re_f", "c_im_f", "lam_re_b", "lam_im_b", "log_dt_b",
                   "c_re_b", "c_im_b", "b_re", "b_im", "d_skip", "conv_b"]
    small_shapes = [weights[k].shape for k in small_names]
    spack = _pack([small_g[k] for k in small_names] + [g_convw_full], rows_mult=8)
    sred = _sum8(_exchange8([spack], "gather_small_grads", True)[0], "sum_small_grads")
    sg_list = _unpack(sred, small_shapes + [(3, f2)])
    g_small = dict(zip(small_names, sg_list[:-1]))
    g_small["conv_w"] = lax.dynamic_slice(sg_list[-1], (0, chip * cwid), (3, cwid))[None]

    gwi = g_win_p
    g_big["w_in"] = jnp.concatenate([gwi[:, q_off:q_off + Q_LORA], gwi[:, kv_off:kv_off + KV_LORA],
                                     gwi[:, kr_off:kr_off + QK_ROPE], gwi[:, u_off:u_off + SSM_WIDTH], gwi[:, :2 * d]], axis=1)
    g_big["w_uq"] = g_wuq_p.reshape(Q_LORA, N_HEADS, SLOT)[:, :, :QK_DIM].reshape(Q_LORA, N_HEADS * QK_DIM)
    gk3 = g_wukv_p[:, :hw].reshape(KV_LORA, N_HEADS, SLOT)[:, :, :QK_NOPE]
    gv3 = g_wukv_p[:, hw:].reshape(KV_LORA, N_HEADS, SLOT)[:, :, :V_DIM]
    g_big["w_ukv"] = jnp.concatenate([gk3, gv3], axis=2).reshape(KV_LORA, N_HEADS * (QK_NOPE + V_DIM))
    g_big["w_o_attn"] = g_wo_p.reshape(N_HEADS, SLOT, d)[:, :V_DIM].reshape(N_HEADS * V_DIM, d)

    def pieces(k_):
        r_, c_ = weights[k_].shape[1:]
        if k_ in row_sharded:
            p4 = jnp.transpose(g_big[k_].reshape(4, 2, r_ // 2, c_), (1, 0, 2, 3))
        else:
            p4 = jnp.transpose(g_big[k_].reshape(2, r_ // 2, 4, c_), (0, 2, 1, 3))
        return p4.reshape(2, 2 * r_, c_)

    pcs = [pieces(k_) for k_ in big_names]
    from_sibling = _sibling_send(pcs, "grads_to_sibling")
    my_half = []
    for k_, p_, got in zip(big_names, pcs, from_sibling):
        rows4, c_ = got.shape
        rh = rows4 // 4
        own = lax.dynamic_index_in_dim(p_, mc, 0, keepdims=False)
        tr_ = _pick(rows4, (256, 128, 64, 32, 16))
        s32, sb = _rowwise(lambda a, b: (a + b, a + b), name="sum_chip_" + k_, nblk=rows4 // tr_, tr=tr_,
                           rows=[(own, 0, c_, 0), (got, 0, c_, 0)], outs=[(c_, F32), (c_, MXU_DTYPE)])
        my_half.append((s32, sb, rh, c_))
    recv3 = _exchange_chips([sb.reshape(4, rh, c_) for _, sb, rh, c_ in my_half], "scatter_weight_grads", False)
    reduced = []
    for k_, (s32, _, rh, c_), r3 in zip(big_names, my_half, recv3):
        mine = lax.dynamic_slice(s32, (chip * rh, 0), (rh, c_))
        tr_ = _pick(rh, (256, 128, 64, 32, 16))
        (red,) = _rowwise(lambda a, b0, b1, b2: a + b0 + b1 + b2, name="sum_grad_" + k_, nblk=rh // tr_, tr=tr_,
                          rows=[(mine, 0, c_, 0)] + [(r3.reshape(3 * rh, c_), 0, c_, j * rh) for j in range(3)],
                          outs=[(c_, F32)])
        reduced.append(red)
    both = _sibling_exchange(reduced, "exchange_halves")
    g_sh = {k_: b_.reshape((1,) + weights[k_].shape[1:]) for k_, b_ in zip(big_names, both)}
    g_sh["w_mod"] = g_wmod[None]

    grads = {**g_sh, **g_small}
    outs_d, outs_m, outs_v = {}, {}, {}
    for k_ in ["w_mod"] + big_names:
        shp = weights[k_].shape
        res = _adamw(*[t.reshape(shp[1:]) for t in (grads[k_], weights[k_], mom_m[k_], mom_v[k_])], "adamw_" + k_)
        for dst, buf in zip((outs_d, outs_m, outs_v), res):
            dst[k_] = buf.reshape(shp)
    adam_small = small_names + ["conv_w"]
    shapes = [weights[k_].shape for k_ in adam_small]
    res = _adamw(*[_pack([src[k_] for k_ in adam_small], rows_mult=8) for src in (grads, weights, mom_m, mom_v)], "adamw_small")
    for dst, buf in zip((outs_d, outs_m, outs_v), res):
        dst.update(zip(adam_small, _unpack(buf, shapes)))
    grads = {k_: grads[k_].reshape(weights[k_].shape) for k_ in names}
    return (loss, grad_x, *[grads[k_] for k_ in names], *[outs_d[k_] for k_ in names],
            *[outs_m[k_] for k_ in names], *[outs_v[k_] for k_ in names])
```

```python
import functools
import math

import numpy as np
import jax
import jax.numpy as jnp
from jax import lax
from jax.experimental import pallas as pl
from jax.experimental.pallas import tpu as pltpu

F32 = jnp.float32
MXU_DTYPE = jnp.bfloat16
MESH = pl.DeviceIdType.MESH

EPS = 1e-6
N_HEADS = 8
QK_NOPE = 64
QK_ROPE = 32
QK_DIM = QK_NOPE + QK_ROPE
V_DIM = 64
SLOT = 128
Q_LORA = 384
KV_LORA = 256
GRID_W = 64
ROPE_THETA = 10000.0
SSM_WIDTH = 512
SSM_GROUP = 16
SSM_GROUPS = 32
SSM_STATE = 64
N_STATE = SSM_GROUPS * SSM_STATE
CG_STATES = 512
N_CG = N_STATE // CG_STATES
CG_CHANNELS = SSM_WIDTH // N_CG
SCAN_LANES = 512
PACK_W = 1024

ADAM_LR = 0.001
ADAM_B1 = 0.9
ADAM_B2 = 0.999
ADAM_EPS = 1e-08
ADAM_WD = 0.01
ADAM_STEP = 10

VMEM_LIMIT = 56 * 1024 * 1024
LOG2E = 1.4426950408889634


def _pick(n, cands):
    for c in cands:
        if c <= n and n % c == 0:
            return c
    return n


def _cparams(sem):
    return pltpu.CompilerParams(dimension_semantics=sem, vmem_limit_bytes=VMEM_LIMIT)


def _sds(shape, dtype):
    return jax.ShapeDtypeStruct(tuple(shape), dtype)


_K_CANDS = (2816, 2048, 1536, 1408, 1280, 1152, 1024, 896, 768, 704, 640, 512, 384, 256, 128, 64, 32, 16)
_M_CANDS = (2048, 1408, 1024, 768, 512, 384, 256, 128, 64, 32, 16)
_N_CANDS = (1408, 1152, 1024, 768, 512, 384, 256, 128)
MM_VMEM_BUDGET = 40 * 1024 * 1024


def _mm_tiles(m, n, k_opts, a_bytes, b_bytes, o_bytes, m_cands):
    tn = n if n <= _N_CANDS[0] else _pick(n, _N_CANDS)
    for tk in k_opts:
        for tm in ((m,) if m <= m_cands[0] else ()) + tuple(t for t in m_cands if t < m and m % t == 0):
            if 2 * (tm * tk * a_bytes + tk * tn * b_bytes + tm * tn * o_bytes) + tm * tn * 4 <= MM_VMEM_BUDGET:
                return tm, tn, tk
    raise ValueError("no matmul tiling fits")


def _mm(a, b, mode, name, out_dtype=F32, rows=None, a_off=0, b_off=0):
    a_bytes, b_bytes, o_bytes = a.dtype.itemsize, b.dtype.itemsize, jnp.dtype(out_dtype).itemsize
    if mode == "tn":
        t_rows = rows or a.shape[0]
        m, n = a.shape[1], b.shape[1]
        k_opts = tuple(t for t in _K_CANDS if t <= t_rows and t_rows % t == 0) or (t_rows,)
        tm, tn, tk = _mm_tiles(m, n, k_opts, a_bytes, b_bytes, o_bytes, _M_CANDS[1:])
        nk = t_rows // tk
        ao, bo = a_off // tk, b_off // tk
        grid = (m // tm, n // tn, nk)
        in_specs = [pl.BlockSpec((tk, tm), lambda i, j, k: (k + ao, i)),
                    pl.BlockSpec((tk, tn), lambda i, j, k: (k + bo, j))]
        dn = (((0,), (0,)), ((), ()))
    else:
        m = rows or a.shape[0]
        kdim = a.shape[1]
        n = b.shape[1] if mode == "nn" else b.shape[0]
        k_opts = (kdim,) + tuple(t for t in _K_CANDS if t < kdim and kdim % t == 0)
        tm, tn, tk = _mm_tiles(m, n, k_opts, a_bytes, b_bytes, o_bytes, _M_CANDS)
        nk = kdim // tk
        ao = a_off // tm
        grid = (m // tm, n // tn, nk)
        if mode == "nn":
            in_specs = [pl.BlockSpec((tm, tk), lambda i, j, k: (i + ao, k)),
                        pl.BlockSpec((tk, tn), lambda i, j, k: (k, j))]
            dn = (((1,), (0,)), ((), ()))
        else:
            in_specs = [pl.BlockSpec((tm, tk), lambda i, j, k: (i + ao, k)),
                        pl.BlockSpec((tn, tk), lambda i, j, k: (j, k))]
            dn = (((1,), (1,)), ((), ()))
    use_scratch = nk > 1 and out_dtype != F32

    def body(a_ref, b_ref, o_ref, *scr):
        r = lax.dot_general(a_ref[...].astype(MXU_DTYPE), b_ref[...].astype(MXU_DTYPE), dn,
                            preferred_element_type=F32)
        if nk == 1:
            o_ref[...] = r.astype(o_ref.dtype)
        else:
            k = pl.program_id(2)
            acc = scr[0] if use_scratch else o_ref

            @pl.when(k == 0)
            def _():
                acc[...] = r

            @pl.when(k > 0)
            def _():
                acc[...] += r

            if use_scratch:
                @pl.when(k == nk - 1)
                def _():
                    o_ref[...] = acc[...].astype(o_ref.dtype)

    return pl.pallas_call(
        body, name=name, grid=grid, in_specs=in_specs,
        out_specs=pl.BlockSpec((tm, tn), lambda i, j, k: (i, j)),
        out_shape=_sds((m, n), out_dtype),
        scratch_shapes=[pltpu.VMEM((tm, tn), F32)] if use_scratch else [],
        compiler_params=_cparams(("parallel", "parallel", "arbitrary")),
    )(a, b)


def _rowwise(body, *, name, nblk, tr, rows=(), halo=(), sels=(), fulls=(), outs=(), accs=(), seg=None):
    n_rows, n_sel, n_full, n_out, n_acc = len(rows), len(sels), len(fulls), len(outs), len(accs)
    halo = tuple(halo)
    maxw = max([r[2] for r in rows] + [o[0] for o in outs] + list(accs))
    sr = _pick(tr, tuple(s for s in (256, 128, 64, 32, 16) if s * maxw <= 131072) or (16,))
    nsub = tr // sr
    total8 = nblk * tr // 8

    def seg_of(i):
        return jnp.where(i >= seg, 1, 0) if seg is not None else 0

    in_specs, operands = [], []
    for arr, cb, w, roff in rows:
        ob = roff // tr
        in_specs.append(pl.BlockSpec((tr, w), lambda i, cb=cb, ob=ob: (i + ob, cb)))
        operands.append(arr)
    for h in halo:
        arr, cb, w, roff = rows[h]
        o8, t8 = roff // 8, tr // 8
        in_specs.append(pl.BlockSpec((8, w), lambda i, cb=cb, o8=o8, t8=t8: (jnp.maximum(i * t8 - 1, 0) + o8, cb)))
        in_specs.append(pl.BlockSpec((8, w), lambda i, cb=cb, o8=o8, t8=t8: (jnp.minimum((i + 1) * t8, total8 - 1) + o8, cb)))
        operands += [arr, arr]
    for arr in sels:
        in_specs.append(pl.BlockSpec((None,) + arr.shape[1:], lambda i: (seg_of(i), 0, 0)))
        operands.append(arr)
    for arr in fulls:
        in_specs.append(pl.BlockSpec(arr.shape, lambda i: (0, 0)))
        operands.append(arr)
    out_specs, out_shape = [], []
    for w, dt in outs:
        out_specs.append(pl.BlockSpec((tr, w), lambda i: (i, 0)))
        out_shape.append(_sds((nblk * tr, w), dt))
    for w in accs:
        if seg is None:
            out_specs.append(pl.BlockSpec((8, w), lambda i: (0, 0)))
            out_shape.append(_sds((8, w), F32))
        else:
            out_specs.append(pl.BlockSpec((None, 8, w), lambda i: (seg_of(i), 0, 0)))
            out_shape.append(_sds((2, 8, w), F32))
    n_halo = 2 * len(halo)

    def kern(*refs):
        row_refs = refs[:n_rows]
        halo_refs = refs[n_rows:n_rows + n_halo]
        sel_refs = refs[n_rows + n_halo:n_rows + n_halo + n_sel]
        full_refs = refs[n_rows + n_halo + n_sel:n_rows + n_halo + n_sel + n_full]
        o0 = n_rows + n_halo + n_sel + n_full
        out_refs = refs[o0:o0 + n_out]
        acc_refs = refs[o0 + n_out:o0 + n_out + n_acc]
        i = pl.program_id(0)
        if n_acc:
            first = (i == 0) if seg is None else ((i == 0) | (i == seg))

            @pl.when(first)
            def _():
                for a_ref in acc_refs:
                    a_ref[...] = jnp.zeros(a_ref.shape, F32)

        def sub(s, carry):
            r0 = pl.multiple_of(s * sr, sr)
            vals = []
            for idx, r in enumerate(row_refs):
                cur = r[pl.ds(r0, sr), :]
                if idx in halo:
                    hp = halo_refs[2 * halo.index(idx)]
                    hn = halo_refs[2 * halo.index(idx) + 1]
                    cur = cur.astype(F32)
                    rid = lax.broadcasted_iota(jnp.int32, cur.shape, 0)
                    lo = r[pl.ds(pl.multiple_of(jnp.maximum(r0 - 8, 0), 8), 8), :].astype(F32)
                    lo = jnp.where(s == 0, hp[...].astype(F32), lo)
                    lo = jnp.where((s == 0) & (i == 0), 0.0, lo)
                    hi = r[pl.ds(pl.multiple_of(jnp.minimum(r0 + sr, tr - 8), 8), 8), :].astype(F32)
                    hi = jnp.where(s == nsub - 1, hn[...].astype(F32), hi)
                    hi = jnp.where((s == nsub - 1) & (i == nblk - 1), 0.0, hi)
                    prev = jnp.where(rid == 0, jnp.broadcast_to(lo[7:8, :], cur.shape), pltpu.roll(cur, 1, 0))
                    nxt = jnp.where(rid == sr - 1, jnp.broadcast_to(hi[0:1, :], cur.shape), pltpu.roll(cur, sr - 1, 0))
                    vals.append((prev, cur, nxt))
                else:
                    vals.append(cur)
            res = body(*vals, *[r[...] for r in sel_refs], *[r[...] for r in full_refs])
            if not isinstance(res, (tuple, list)):
                res = (res,)
            for o_ref, v in zip(out_refs, res[:n_out]):
                o_ref[pl.ds(r0, sr), :] = v.astype(o_ref.dtype)
            for a_ref, v in zip(acc_refs, res[n_out:]):
                a_ref[...] += jnp.sum(v.astype(F32).reshape(sr // 8, 8, v.shape[-1]), axis=0)
            return carry

        lax.fori_loop(0, nsub, sub, 0)

    res = pl.pallas_call(
        kern, name=name, grid=(nblk,), in_specs=in_specs, out_specs=out_specs, out_shape=out_shape,
        compiler_params=_cparams(("arbitrary",)),
    )(*operands)
    return res


def _sigmoid(x):
    return 1.0 / (1.0 + jnp.exp(-x))


def _silu(x):
    return x * _sigmoid(x)


def _dsilu(x):
    s = _sigmoid(x)
    return s * (1.0 + x * (1.0 - s))


_GELU_K = math.sqrt(2.0 / math.pi)


def _gelu(x):
    return 0.5 * x * (1.0 + jnp.tanh(_GELU_K * (x + 0.044715 * x * x * x)))


def _dgelu(x):
    t = jnp.tanh(_GELU_K * (x + 0.044715 * x * x * x))
    return 0.5 * (1.0 + t) + 0.5 * x * (1.0 - t * t) * _GELU_K * (1.0 + 3.0 * 0.044715 * x * x)


def _rms_fwd(x, g, width):
    r = lax.rsqrt(jnp.sum(x * x, axis=-1, keepdims=True) * (1.0 / width) + EPS)
    return x * r * g


def _rms_bwd(x, g, dy, width):
    r = lax.rsqrt(jnp.sum(x * x, axis=-1, keepdims=True) * (1.0 / width) + EPS)
    xn = x * r
    dyg = dy * g
    dx = r * (dyg - xn * (jnp.sum(dyg * xn, axis=-1, keepdims=True) * (1.0 / width)))
    return dx, dy * xn


def _rope_fwd(y, c, sa, sb):
    return y * c + pltpu.roll(y, SLOT - 16, 1) * sa + pltpu.roll(y, 16, 1) * sb


def _rope_bwd(d, c, sa, sb):
    return d * c + pltpu.roll(d * sa, 16, 1) + pltpu.roll(d * sb, SLOT - 16, 1)


def _heads(v):
    return [v[:, h * SLOT:(h + 1) * SLOT] for h in range(N_HEADS)]


def _attn_fwd(q, k, v, nl, scale):
    n = k.shape[0]
    tq = _pick(nl, (1024, 512, 256, 128))
    tk = _pick(n, (2816, 1408, 1152, 768, 384, 256, 128))
    sub = min(tq, 256)
    nk = n // tk
    rep = tk // SLOT
    c = scale * LOG2E

    def body(q_ref, k_ref, v_ref, o_ref, lse_ref, m_sc, l_sc, acc_sc):
        ki = pl.program_id(2)

        @pl.when(ki == 0)
        def _():
            m_sc[...] = jnp.full(m_sc.shape, -jnp.inf, F32)
            l_sc[...] = jnp.zeros(l_sc.shape, F32)
            acc_sc[...] = jnp.zeros(acc_sc.shape, F32)

        kb, vb = k_ref[...], v_ref[...]
        for sb in range(tq // sub):
            rows = slice(sb * sub, (sb + 1) * sub)
            s = lax.dot_general(q_ref[rows, :], kb, (((1,), (1,)), ((), ())), preferred_element_type=F32)
            m_prev = m_sc[rows, :]
            m_new = jnp.maximum(m_prev, jnp.max(s, axis=1, keepdims=True) * c)
            alpha = jnp.exp2(m_prev - m_new)
            p = jnp.exp2(s * c - jnp.tile(m_new, (1, rep)))
            l_sc[rows, :] = alpha * l_sc[rows, :] + jnp.sum(p, axis=1, keepdims=True)
            acc_sc[rows, :] = alpha * acc_sc[rows, :] + jnp.dot(p.astype(MXU_DTYPE), vb, preferred_element_type=F32)
            m_sc[rows, :] = m_new

        @pl.when(ki == nk - 1)
        def _():
            l = l_sc[...]
            o_ref[...] = (acc_sc[...] / l).astype(o_ref.dtype)
            lse_ref[...] = jnp.transpose(m_sc[...] + jnp.log2(l))[0:8, :]

    return pl.pallas_call(
        body, name="attn_fwd", grid=(N_HEADS, nl // tq, nk),
        in_specs=[pl.BlockSpec((tq, SLOT), lambda h, i, j: (i, h)),
                  pl.BlockSpec((tk, SLOT), lambda h, i, j: (j, h)),
                  pl.BlockSpec((tk, SLOT), lambda h, i, j: (j, h))],
        out_specs=[pl.BlockSpec((tq, SLOT), lambda h, i, j: (i, h)),
                   pl.BlockSpec((None, 8, tq), lambda h, i, j: (h, 0, i))],
        out_shape=[_sds((nl, N_HEADS * SLOT), MXU_DTYPE), _sds((N_HEADS, 8, nl), F32)],
        scratch_shapes=[pltpu.VMEM((tq, SLOT), F32), pltpu.VMEM((tq, SLOT), F32), pltpu.VMEM((tq, SLOT), F32)],
        compiler_params=_cparams(("parallel", "parallel", "arbitrary")),
    )(q, k, v)


def _attn_bwd(q, k, kt, v, do, o, lse_t, nl, scale):
    n = k.shape[0]
    tq = _pick(nl, (1024, 512, 256, 128))
    tk = _pick(n, (2816, 1408, 1152, 768, 384, 256, 128))
    sub = _pick(tk, (256, 128))
    nq, nk = nl // tq, n // tk
    c = scale * LOG2E

    def body(q_ref, k_ref, kt_ref, v_ref, do_ref, o_ref, lse_ref, dq_ref, dk_ref, dv_ref, dk_acc, dv_acc):
        ki, qi = pl.program_id(1), pl.program_id(2)

        @pl.when((ki == 0) & (qi == 0))
        def _():
            dq_ref[...] = jnp.zeros(dq_ref.shape, F32)

        @pl.when(qi == 0)
        def _():
            dk_acc[...] = jnp.zeros(dk_acc.shape, F32)
            dv_acc[...] = jnp.zeros(dv_acc.shape, F32)

        qb, dof = q_ref[...], do_ref[...]
        dob = dof.astype(MXU_DTYPE)
        lse_r = lse_ref[0:1, :]
        dl_r = jnp.sum(jnp.transpose(dof * o_ref[...].astype(F32)), axis=0, keepdims=True)
        dq_part = None
        for sb in range(tk // sub):
            rows = slice(sb * sub, (sb + 1) * sub)
            s_t = lax.dot_general(k_ref[rows, :], qb, (((1,), (1,)), ((), ())), preferred_element_type=F32)
            p_t = jnp.exp2(s_t * c - lse_r)
            dp_t = lax.dot_general(v_ref[rows, :], dob, (((1,), (1,)), ((), ())), preferred_element_type=F32)
            ds_t = (p_t * (dp_t - dl_r) * scale).astype(MXU_DTYPE)
            dv_acc[rows, :] += jnp.dot(p_t.astype(MXU_DTYPE), dob, preferred_element_type=F32)
            dk_acc[rows, :] += jnp.dot(ds_t, qb, preferred_element_type=F32)
            part = jnp.dot(kt_ref[:, rows], ds_t, preferred_element_type=F32)
            dq_part = part if dq_part is None else dq_part + part
        c0 = pl.multiple_of(qi * tq, tq)
        dq_ref[:, pl.ds(c0, tq)] += dq_part

        @pl.when(qi == nq - 1)
        def _():
            dk_ref[...] = dk_acc[...]
            dv_ref[...] = dv_acc[...]

    return pl.pallas_call(
        body, name="attn_bwd", grid=(N_HEADS, nk, nq),
        in_specs=[pl.BlockSpec((tq, SLOT), lambda h, j, i: (i, h)),
                  pl.BlockSpec((tk, SLOT), lambda h, j, i: (j, h)),
                  pl.BlockSpec((SLOT, tk), lambda h, j, i: (h, j)),
                  pl.BlockSpec((tk, SLOT), lambda h, j, i: (j, h)),
                  pl.BlockSpec((tq, SLOT), lambda h, j, i: (i, h)),
                  pl.BlockSpec((tq, SLOT), lambda h, j, i: (i, h)),
                  pl.BlockSpec((None, 8, tq), lambda h, j, i: (h, 0, i))],
        out_specs=[pl.BlockSpec((SLOT, nl), lambda h, j, i: (h, 0)),
                   pl.BlockSpec((tk, SLOT), lambda h, j, i: (j, h)),
                   pl.BlockSpec((tk, SLOT), lambda h, j, i: (j, h))],
        out_shape=[_sds((N_HEADS * SLOT, nl), F32), _sds((n, N_HEADS * SLOT), F32), _sds((n, N_HEADS * SLOT), F32)],
        scratch_shapes=[pltpu.VMEM((tk, SLOT), F32), pltpu.VMEM((tk, SLOT), F32)],
        compiler_params=_cparams(("arbitrary", "arbitrary", "arbitrary")),
    )(q, k, kt, v, do, o, lse_t)


def _scan_consts(c_ref, lg):
    cs = slice(lg * SCAN_LANES, (lg + 1) * SCAN_LANES)
    return [c_ref[8 * kk:8 * kk + 8, cs] for kk in range(8)]


def _tile_scan(br, bi, consts, reverse):
    p1r, p1i, p2r, p2i, p4r, p4i = consts[:6]
    for pr, pi, kk in ((p1r, p1i, 1), (p2r, p2i, 2), (p4r, p4i, 4)):
        sh = (8 - kk) if reverse else kk
        sr_, si_ = pltpu.roll(br, sh, 0), pltpu.roll(bi, sh, 0)
        br, bi = br + pr * sr_ - pi * si_, bi + pr * si_ + pi * sr_
    return br, bi


def _seq_chunk(j, nch, nlc, reverse):
    return (nch - 1 - j) if reverse else (j + nlc) % nch


def _s5_scan(z, u_cb, bbd, cbd_n, lamc, t_rows, nl, reverse, name):
    n = z.shape[0]
    nch, nlc = n // t_rows, nl // t_rows
    ntile = t_rows // 8
    w = SCAN_LANES
    edge = 0 if reverse else 7
    ucb = u_cb * (SSM_WIDTH // CG_CHANNELS)

    def chunk(j):
        return _seq_chunk(j, nch, nlc, reverse)

    def body(u_ref, b_ref, cn_ref, c_ref, xs_ref, y_ref, carry):
        j = pl.program_id(1)

        @pl.when(j == 0)
        def _():
            carry[...] = jnp.zeros(carry.shape, F32)

        xs_ref[...] = jnp.dot(u_ref[...].astype(MXU_DTYPE), b_ref[...], preferred_element_type=F32)
        for lg in range(CG_STATES // w):
            re = slice(lg * w, (lg + 1) * w)
            im = slice(CG_STATES + lg * w, CG_STATES + (lg + 1) * w)
            consts = _scan_consts(c_ref, lg)
            qr, qi = consts[6], consts[7]

            def tile(tt, st):
                cr, ci = st
                t = (ntile - 1 - tt) if reverse else tt
                r0 = pl.multiple_of(t * 8, 8)
                br, bi = _tile_scan(xs_ref[pl.ds(r0, 8), re], xs_ref[pl.ds(r0, 8), im], consts, reverse)
                lr = jnp.broadcast_to(cr[edge:edge + 1, :], br.shape)
                li = jnp.broadcast_to(ci[edge:edge + 1, :], bi.shape)
                xr = br + qr * lr - qi * li
                xi = bi + qr * li + qi * lr
                xs_ref[pl.ds(r0, 8), re] = xr
                xs_ref[pl.ds(r0, 8), im] = xi
                return xr, xi

            cr, ci = lax.fori_loop(0, ntile, tile, (carry[:, re], carry[:, im]))
            carry[:, re] = cr
            carry[:, im] = ci
        y_ref[...] = jnp.dot(xs_ref[...].astype(MXU_DTYPE), cn_ref[...], preferred_element_type=F32)

    cw = 2 * CG_STATES
    return pl.pallas_call(
        body, name=name, grid=(N_CG, nch),
        in_specs=[pl.BlockSpec((t_rows, CG_CHANNELS), lambda g, j: (chunk(j), ucb + g)),
                  pl.BlockSpec((CG_CHANNELS, cw), lambda g, j: (g, 0)),
                  pl.BlockSpec((cw, CG_CHANNELS), lambda g, j: (g, 0)),
                  pl.BlockSpec((64, CG_STATES), lambda g, j: (0, g))],
        out_specs=[pl.BlockSpec((t_rows, cw), lambda g, j: (chunk(j), g)),
                   pl.BlockSpec((t_rows, CG_CHANNELS), lambda g, j: (chunk(j), g))],
        out_shape=[_sds((n, 2 * N_STATE), F32), _sds((n, SSM_WIDTH), F32)],
        scratch_shapes=[pltpu.VMEM((8, cw), F32)],
        compiler_params=_cparams(("arbitrary", "arbitrary")),
    )(z, bbd, cbd_n, lamc)


def _s5_bwd(dys, z, u_cb, xs, cbd_t, bbd_t, lamc_adj, t_rows, nl, reverse, name):
    n = z.shape[0]
    nch, nlc = n // t_rows, nl // t_rows
    ntile = t_rows // 8
    t8 = t_rows // 8
    w = SCAN_LANES
    cw = 2 * CG_STATES
    adj_rev = not reverse
    edge = 0 if adj_rev else 7

    def chunk(j):
        return _seq_chunk(nch - 1 - j, nch, nlc, reverse)

    def halo_blk(j):
        if reverse:
            return jnp.minimum((chunk(j) + 1) * t8, n // 8 - 1)
        return (_seq_chunk(jnp.maximum(nch - 2 - j, 0), nch, nlc, False) + 1) * t8 - 1

    def body(dy_ref, u_ref, xs_ref, halo_ref, ct_ref, bt_ref, c_ref, du_ref, db_ref, dc_ref, dl_ref, gbuf, carry):
        j = pl.program_id(1)
        start = j == nch - 1

        @pl.when(j == 0)
        def _():
            carry[...] = jnp.zeros(carry.shape, F32)
            db_ref[...] = jnp.zeros(db_ref.shape, F32)
            dc_ref[...] = jnp.zeros(dc_ref.shape, F32)
            dl_ref[...] = jnp.zeros(dl_ref.shape, F32)

        dy = jnp.where(chunk(j) < nlc, dy_ref[...], 0.0).astype(MXU_DTYPE)
        gbuf[...] = jnp.dot(dy, ct_ref[...], preferred_element_type=F32)
        dc_ref[...] += lax.dot_general(dy, xs_ref[...].astype(MXU_DTYPE), (((0,), (0,)), ((), ())),
                                       preferred_element_type=F32)
        for lg in range(CG_STATES // w):
            re = slice(lg * w, (lg + 1) * w)
            im = slice(CG_STATES + lg * w, CG_STATES + (lg + 1) * w)
            consts = _scan_consts(c_ref, lg)
            qr, qi = consts[6], consts[7]
            hr, hi = halo_ref[:, re], halo_ref[:, im]

            def tile(tt, st):
                gcr, gci, ar, ai = st
                t = (ntile - 1 - tt) if adj_rev else tt
                r0 = pl.multiple_of(t * 8, 8)
                br, bi = _tile_scan(gbuf[pl.ds(r0, 8), re], gbuf[pl.ds(r0, 8), im], consts, adj_rev)
                lr = jnp.broadcast_to(gcr[edge:edge + 1, :], br.shape)
                li = jnp.broadcast_to(gci[edge:edge + 1, :], bi.shape)
                gr = br + qr * lr - qi * li
                gi = bi + qr * li + qi * lr
                gbuf[pl.ds(r0, 8), re] = gr
                gbuf[pl.ds(r0, 8), im] = gi
                xr, xi = xs_ref[pl.ds(r0, 8), re], xs_ref[pl.ds(r0, 8), im]
                rid = lax.broadcasted_iota(jnp.int32, xr.shape, 0)
                if reverse:
                    last = t == ntile - 1
                    rn = pl.multiple_of(jnp.minimum(r0 + 8, t_rows - 8), 8)
                    nbr = jnp.where(last, hr, xs_ref[pl.ds(rn, 8), re])
                    nbi = jnp.where(last, hi, xs_ref[pl.ds(rn, 8), im])
                    nbr = jnp.where(last & start, 0.0, nbr)
                    nbi = jnp.where(last & start, 0.0, nbi)
                    xpr = jnp.where(rid == 7, jnp.broadcast_to(nbr[0:1, :], xr.shape), pltpu.roll(xr, 7, 0))
                    xpi = jnp.where(rid == 7, jnp.broadcast_to(nbi[0:1, :], xi.shape), pltpu.roll(xi, 7, 0))
                else:
                    first = t == 0
                    rn = pl.multiple_of(jnp.maximum(r0 - 8, 0), 8)
                    nbr = jnp.where(first, hr, xs_ref[pl.ds(rn, 8), re])
                    nbi = jnp.where(first, hi, xs_ref[pl.ds(rn, 8), im])
                    nbr = jnp.where(first & start, 0.0, nbr)
                    nbi = jnp.where(first & start, 0.0, nbi)
                    xpr = jnp.where(rid == 0, jnp.broadcast_to(nbr[7:8, :], xr.shape), pltpu.roll(xr, 1, 0))
                    xpi = jnp.where(rid == 0, jnp.broadcast_to(nbi[7:8, :], xi.shape), pltpu.roll(xi, 1, 0))
                ar = ar + gr * xpr + gi * xpi
                ai = ai - gr * xpi + gi * xpr
                return gr, gi, ar, ai

            zz = jnp.zeros((8, w), F32)
            gcr, gci, ar, ai = lax.fori_loop(0, ntile, tile, (carry[:, re], carry[:, im], zz, zz))
            carry[:, re] = gcr
            carry[:, im] = gci
            dl_ref[:, re] += ar
            dl_ref[:, im] += ai
        g = gbuf[...].astype(MXU_DTYPE)
        du_ref[...] = jnp.dot(g, bt_ref[...], preferred_element_type=F32)
        db_ref[...] += lax.dot_general(u_ref[...].astype(MXU_DTYPE), g, (((0,), (0,)), ((), ())),
                                       preferred_element_type=F32)

    ucb = u_cb * (SSM_WIDTH // CG_CHANNELS)
    return pl.pallas_call(
        body, name=name, grid=(N_CG, nch),
        in_specs=[pl.BlockSpec((t_rows, CG_CHANNELS), lambda g, j: (jnp.minimum(chunk(j), nlc - 1), g)),
                  pl.BlockSpec((t_rows, CG_CHANNELS), lambda g, j: (chunk(j), ucb + g)),
                  pl.BlockSpec((t_rows, cw), lambda g, j: (chunk(j), g)),
                  pl.BlockSpec((8, cw), lambda g, j: (halo_blk(j), g)),
                  pl.BlockSpec((CG_CHANNELS, cw), lambda g, j: (g, 0)),
                  pl.BlockSpec((cw, CG_CHANNELS), lambda g, j: (g, 0)),
                  pl.BlockSpec((64, CG_STATES), lambda g, j: (0, g))],
        out_specs=[pl.BlockSpec((t_rows, CG_CHANNELS), lambda g, j: (chunk(j), g)),
                   pl.BlockSpec((CG_CHANNELS, cw), lambda g, j: (g, 0)),
                   pl.BlockSpec((CG_CHANNELS, cw), lambda g, j: (g, 0)),
                   pl.BlockSpec((8, cw), lambda g, j: (0, g))],
        out_shape=[_sds((n, SSM_WIDTH), F32), _sds((SSM_WIDTH, cw), F32), _sds((SSM_WIDTH, cw), F32),
                   _sds((8, 2 * N_STATE), F32)],
        scratch_shapes=[pltpu.VMEM((t_rows, cw), F32), pltpu.VMEM((8, cw), F32)],
        compiler_params=_cparams(("arbitrary", "arbitrary")),
    )(dys, z, xs, xs, cbd_t, bbd_t, lamc_adj)


_CG_GROUPS = SSM_GROUPS // N_CG


def _group_mask():
    idx = jnp.arange(_CG_GROUPS)
    return (idx[:, None] == idx[None, :])[None, :, None, None, :, None]


def _diag_blocks(p_re, p_im):
    t = jnp.stack([p_re, p_im], axis=2).reshape(N_CG, _CG_GROUPS, SSM_GROUP, 2, 1, SSM_STATE)
    return jnp.where(_group_mask(), t, 0.0).reshape(SSM_WIDTH, 2 * CG_STATES)


def _diag_extract(d):
    d6 = d.reshape(N_CG, _CG_GROUPS, SSM_GROUP, 2, _CG_GROUPS, SSM_STATE)
    blk = jnp.sum(jnp.where(_group_mask(), d6, 0.0), axis=4)
    blk = blk.reshape(SSM_GROUPS, SSM_GROUP, 2, SSM_STATE)
    return blk[:, :, 0], blk[:, :, 1]


def _block_transpose(d):
    return jnp.transpose(d.reshape(N_CG, CG_CHANNELS, 2 * CG_STATES), (0, 2, 1)).reshape(2 * N_STATE, CG_CHANNELS)


def _s5_disc(lam_re, lam_im, log_dt, b_re, b_im):
    lam = lax.complex(lam_re, lam_im)
    dt = jnp.exp(log_dt)[:, None]
    lam_bar = jnp.exp(lam * dt)
    b_bar = ((lam_bar - 1.0) / lam)[..., None] * lax.complex(b_re, b_im)
    return jnp.real(lam_bar), jnp.imag(lam_bar), jnp.real(b_bar), jnp.imag(b_bar)


def _lam_consts(lr, li, mirrored, conj):
    lam = lax.complex(lr.reshape(-1), -li.reshape(-1) if conj else li.reshape(-1))
    p2 = lam * lam
    p4 = p2 * p2
    pw = [lam, p2, p2 * lam, p4, p4 * lam, p4 * p2, p4 * p2 * lam, p4 * p4]
    rows = jnp.arange(8)[:, None]
    out = []
    for kk in (1, 2, 4):
        mask = (rows <= 7 - kk) if mirrored else (rows >= kk)
        pk = jnp.where(mask, pw[kk - 1][None, :], 0.0)
        out += [jnp.real(pk), jnp.imag(pk)]
    q = jnp.stack(pw[::-1] if mirrored else pw)
    return jnp.concatenate(out + [jnp.real(q), jnp.imag(q)], axis=0)


def _dev(t):
    return (t // 4, (t // 2) % 2, t % 2)


def _my_index():
    return 4 * lax.axis_index("x") + 2 * lax.axis_index("y") + lax.axis_index("c")


def _comm_call(body, name, arrs, lead, n_remote):
    nw = len(arrs)
    any_spec = pl.BlockSpec(memory_space=pl.ANY)
    return pl.pallas_call(
        body, name=name, out_shape=[_sds((lead,) + a.shape[-2:], a.dtype) for a in arrs],
        in_specs=[any_spec] * nw, out_specs=[any_spec] * nw,
        scratch_shapes=[pltpu.SemaphoreType.DMA((n_remote * nw,)), pltpu.SemaphoreType.DMA((n_remote * nw,)),
                        pltpu.SemaphoreType.DMA((2 * nw,))] + [pltpu.VMEM(a.shape[-2:], a.dtype) for a in arrs],
        compiler_params=pltpu.CompilerParams(vmem_limit_bytes=VMEM_LIMIT),
    )(*arrs)


class _LocalCopy:
    def __init__(self, src, dst, buf, sem_in, sem_out):
        self.fetch = pltpu.make_async_copy(src, buf, sem_in)
        self.store = pltpu.make_async_copy(buf, dst, sem_out)
        self.fetch.start()

    def forward(self):
        self.fetch.wait()
        self.store.start()

    def finish(self):
        self.store.wait()


def _exchange8(gs, name, same):
    nw = len(gs)

    def body(*refs):
        g_refs, o_refs, (ssem, rsem, lsem), bufs = refs[:nw], refs[nw:2 * nw], refs[2 * nw:2 * nw + 3], refs[2 * nw + 3:]
        me = _my_index()
        locs, sends = [], []
        for i, (g_ref, o_ref) in enumerate(zip(g_refs, o_refs)):
            src = (lambda t, g_ref=g_ref: g_ref) if same else (lambda t, g_ref=g_ref: g_ref.at[t])
            locs.append(_LocalCopy(src(me), o_ref.at[me], bufs[i], lsem.at[2 * i], lsem.at[2 * i + 1]))
            for d in range(1, 8):
                t = (me + d) % 8
                cp = pltpu.make_async_remote_copy(src_ref=src(t), dst_ref=o_ref.at[me], send_sem=ssem.at[7 * i + d - 1],
                                                  recv_sem=rsem.at[7 * i + d - 1], device_id=_dev(t), device_id_type=MESH)
                cp.start()
                sends.append(cp)
        for loc in locs:
            loc.forward()
        for i, (g_ref, o_ref) in enumerate(zip(g_refs, o_refs)):
            src = (lambda t, g_ref=g_ref: g_ref) if same else (lambda t, g_ref=g_ref: g_ref.at[t])
            for d in range(1, 8):
                s = (me + 8 - d) % 8
                pltpu.make_async_remote_copy(src_ref=src(s), dst_ref=o_ref.at[s], send_sem=ssem.at[7 * i + d - 1],
                                             recv_sem=rsem.at[7 * i + d - 1], device_id=_dev(s),
                                             device_id_type=MESH).wait_recv()
        for cp in sends:
            cp.wait_send()
        for loc in locs:
            loc.finish()

    return _comm_call(body, name, gs, 8, 7)


def _exchange_chips(ws, name, gather):
    nw = len(ws)

    def body(*refs):
        w_refs, o_refs, (ssem, rsem, lsem), bufs = refs[:nw], refs[nw:2 * nw], refs[2 * nw:2 * nw + 3], refs[2 * nw + 3:]
        x, y, cc = lax.axis_index("x"), lax.axis_index("y"), lax.axis_index("c")
        k = 2 * x + y
        peers = [(1 - x, y), (x, 1 - y), (1 - x, 1 - y)]
        locs, sends = [], []
        for i, (w_ref, o_ref) in enumerate(zip(w_refs, o_refs)):
            if gather:
                locs.append(_LocalCopy(w_ref.at[cc], o_ref.at[k], bufs[i], lsem.at[2 * i], lsem.at[2 * i + 1]))
            for j, (px, py) in enumerate(peers):
                src, dst = (w_ref.at[cc], o_ref.at[k]) if gather else (w_ref.at[2 * px + py], o_ref.at[j])
                cp = pltpu.make_async_remote_copy(src_ref=src, dst_ref=dst, send_sem=ssem.at[3 * i + j],
                                                  recv_sem=rsem.at[3 * i + j], device_id=(px, py, cc), device_id_type=MESH)
                cp.start()
                sends.append(cp)
        for loc in locs:
            loc.forward()
        for i, (w_ref, o_ref) in enumerate(zip(w_refs, o_refs)):
            for j, (px, py) in enumerate(peers):
                src, dst = (w_ref.at[cc], o_ref.at[2 * px + py]) if gather else (w_ref.at[k], o_ref.at[j])
                pltpu.make_async_remote_copy(src_ref=src, dst_ref=dst, send_sem=ssem.at[3 * i + j],
                                             recv_sem=rsem.at[3 * i + j], device_id=(px, py, cc),
                                             device_id_type=MESH).wait_recv()
        for cp in sends:
            cp.wait_send()
        for loc in locs:
            loc.finish()

    return _comm_call(body, name, ws, 4 if gather else 3, 3)


def _sibling_send(hs, name):
    nw = len(hs)

    def body(*refs):
        h_refs, o_refs, (ssem, rsem, lsem) = refs[:nw], refs[nw:2 * nw], refs[2 * nw:]
        x, y, cc = lax.axis_index("x"), lax.axis_index("y"), lax.axis_index("c")
        sends = []
        for i, (h_ref, o_ref) in enumerate(zip(h_refs, o_refs)):
            cp = pltpu.make_async_remote_copy(src_ref=h_ref.at[1 - cc], dst_ref=o_ref, send_sem=ssem.at[i],
                                              recv_sem=rsem.at[i], device_id=(x, y, 1 - cc), device_id_type=MESH)
            cp.start()
            sends.append(cp)
        for i, (h_ref, o_ref) in enumerate(zip(h_refs, o_refs)):
            pltpu.make_async_remote_copy(src_ref=h_ref.at[cc], dst_ref=o_ref, send_sem=ssem.at[i], recv_sem=rsem.at[i],
                                         device_id=(x, y, 1 - cc), device_id_type=MESH).wait_recv()
        for cp in sends:
            cp.wait_send()

    nw_spec = pl.BlockSpec(memory_space=pl.ANY)
    return pl.pallas_call(
        body, name=name, out_shape=[_sds(h.shape[1:], h.dtype) for h in hs],
        in_specs=[nw_spec] * nw, out_specs=[nw_spec] * nw,
        scratch_shapes=[pltpu.SemaphoreType.DMA((nw,)), pltpu.SemaphoreType.DMA((nw,)), pltpu.SemaphoreType.DMA((nw,))],
    )(*hs)


def _sibling_exchange(hs, name):
    nw = len(hs)

    def body(*refs):
        h_refs, o_refs, (ssem, rsem, lsem), bufs = refs[:nw], refs[nw:2 * nw], refs[2 * nw:2 * nw + 3], refs[2 * nw + 3:]
        x, y, cc = lax.axis_index("x"), lax.axis_index("y"), lax.axis_index("c")
        locs, sends = [], []
        for i, (h_ref, o_ref) in enumerate(zip(h_refs, o_refs)):
            locs.append(_LocalCopy(h_ref, o_ref.at[cc], bufs[i], lsem.at[2 * i], lsem.at[2 * i + 1]))
            cp = pltpu.make_async_remote_copy(src_ref=h_ref, dst_ref=o_ref.at[cc], send_sem=ssem.at[i], recv_sem=rsem.at[i],
                                              device_id=(x, y, 1 - cc), device_id_type=MESH)
            cp.start()
            sends.append(cp)
        for loc in locs:
            loc.forward()
        for i, (h_ref, o_ref) in enumerate(zip(h_refs, o_refs)):
            pltpu.make_async_remote_copy(src_ref=h_ref, dst_ref=o_ref.at[1 - cc], send_sem=ssem.at[i], recv_sem=rsem.at[i],
                                         device_id=(x, y, 1 - cc), device_id_type=MESH).wait_recv()
        for cp in sends:
            cp.wait_send()
        for loc in locs:
            loc.finish()

    return _comm_call(body, name, hs, 2, 1)


def _sum8(buf, name):
    _, r, c = buf.shape
    tr = _pick(r, (256, 128, 64, 32, 16, 8))
    flat = buf.reshape(8 * r, c)

    def body(*v):
        acc = v[0]
        for t in v[1:]:
            acc = acc + t
        return acc

    return _rowwise(body, name=name, nblk=r // tr, tr=tr, rows=[(flat, 0, c, s * r) for s in range(8)],
                    outs=[(c, F32)])[0]


def _pack(arrs, rows_mult=16):
    flat = jnp.concatenate([a.reshape(-1).astype(F32) for a in arrs])
    nel = flat.shape[0]
    r = -(-nel // PACK_W)
    r = -(-r // rows_mult) * rows_mult
    return jnp.pad(flat, (0, r * PACK_W - nel)).reshape(r, PACK_W)


def _unpack(buf, shapes):
    flat = buf.reshape(-1)
    out, o = [], 0
    for s in shapes:
        nel = int(np.prod(s))
        out.append(flat[o:o + nel].reshape(s))
        o += nel
    return out


def _adamw(g, w, m, v, name):
    r, wd = g.shape
    tr = _pick(r, tuple(t for t in (256, 128, 64, 32, 16, 8) if t * wd <= 262144) or (8,))
    c1 = 1.0 / (1.0 - ADAM_B1 ** ADAM_STEP)
    c2 = 1.0 / (1.0 - ADAM_B2 ** ADAM_STEP)

    def body(gv, wv, mv, vv):
        mn = ADAM_B1 * mv + (1.0 - ADAM_B1) * gv
        vn = ADAM_B2 * vv + (1.0 - ADAM_B2) * (gv * gv)
        delta = -ADAM_LR * ((mn * c1) / (jnp.sqrt(vn * c2) + ADAM_EPS) + ADAM_WD * wv)
        return delta, mn, vn

    return _rowwise(body, name=name, nblk=r // tr, tr=tr, rows=[(a, 0, wd, 0) for a in (g, w, m, v)],
                    outs=[(wd, F32)] * 3)


def kernel(x, c, ctx, c_ctx, w_mod, b_mod, norm1_g, norm2_g, w_in, q_a_g, w_uq, kv_a_g, w_ukv, q_norm_g, k_norm_g, w_o_attn, lam_re_f, lam_im_f, log_dt_f, c_re_f, c_im_f, lam_re_b, lam_im_b, log_dt_b, c_re_b, c_im_b, b_re, b_im, d_skip, w_glu, w_out, w_up, conv_w, conv_b, w_down, loss_target, m_c_ctx, m_w_mod, m_b_mod, m_norm1_g, m_norm2_g, m_w_in, m_q_a_g, m_w_uq, m_kv_a_g, m_w_ukv, m_q_norm_g, m_k_norm_g, m_w_o_attn, m_lam_re_f, m_lam_im_f, m_log_dt_f, m_c_re_f, m_c_im_f, m_lam_re_b, m_lam_im_b, m_log_dt_b, m_c_re_b, m_c_im_b, m_b_re, m_b_im, m_d_skip, m_w_glu, m_w_out, m_w_up, m_conv_w, m_conv_b, m_w_down, v_c_ctx, v_w_mod, v_b_mod, v_norm1_g, v_norm2_g, v_w_in, v_q_a_g, v_w_uq, v_kv_a_g, v_w_ukv, v_q_norm_g, v_k_norm_g, v_w_o_attn, v_lam_re_f, v_lam_im_f, v_log_dt_f, v_c_re_f, v_c_im_f, v_lam_re_b, v_lam_im_b, v_log_dt_b, v_c_re_b, v_c_im_b, v_b_re, v_b_im, v_d_skip, v_w_glu, v_w_out, v_w_up, v_conv_w, v_conv_b, v_w_down):
    weights = dict(c_ctx=c_ctx, w_mod=w_mod, b_mod=b_mod, norm1_g=norm1_g, norm2_g=norm2_g, w_in=w_in, q_a_g=q_a_g, w_uq=w_uq, kv_a_g=kv_a_g, w_ukv=w_ukv, q_norm_g=q_norm_g, k_norm_g=k_norm_g, w_o_attn=w_o_attn, lam_re_f=lam_re_f, lam_im_f=lam_im_f, log_dt_f=log_dt_f, c_re_f=c_re_f, c_im_f=c_im_f, lam_re_b=lam_re_b, lam_im_b=lam_im_b, log_dt_b=log_dt_b, c_re_b=c_re_b, c_im_b=c_im_b, b_re=b_re, b_im=b_im, d_skip=d_skip, w_glu=w_glu, w_out=w_out, w_up=w_up, conv_w=conv_w, conv_b=conv_b, w_down=w_down)
    mom_m = dict(c_ctx=m_c_ctx, w_mod=m_w_mod, b_mod=m_b_mod, norm1_g=m_norm1_g, norm2_g=m_norm2_g, w_in=m_w_in, q_a_g=m_q_a_g, w_uq=m_w_uq, kv_a_g=m_kv_a_g, w_ukv=m_w_ukv, q_norm_g=m_q_norm_g, k_norm_g=m_k_norm_g, w_o_attn=m_w_o_attn, lam_re_f=m_lam_re_f, lam_im_f=m_lam_im_f, log_dt_f=m_log_dt_f, c_re_f=m_c_re_f, c_im_f=m_c_im_f, lam_re_b=m_lam_re_b, lam_im_b=m_lam_im_b, log_dt_b=m_log_dt_b, c_re_b=m_c_re_b, c_im_b=m_c_im_b, b_re=m_b_re, b_im=m_b_im, d_skip=m_d_skip, w_glu=m_w_glu, w_out=m_w_out, w_up=m_w_up, conv_w=m_conv_w, conv_b=m_conv_b, w_down=m_w_down)
    mom_v = dict(c_ctx=v_c_ctx, w_mod=v_w_mod, b_mod=v_b_mod, norm1_g=v_norm1_g, norm2_g=v_norm2_g, w_in=v_w_in, q_a_g=v_q_a_g, w_uq=v_w_uq, kv_a_g=v_kv_a_g, w_ukv=v_w_ukv, q_norm_g=v_q_norm_g, k_norm_g=v_k_norm_g, w_o_attn=v_w_o_attn, lam_re_f=v_lam_re_f, lam_im_f=v_lam_im_f, log_dt_f=v_log_dt_f, c_re_f=v_c_re_f, c_im_f=v_c_im_f, lam_re_b=v_lam_re_b, lam_im_b=v_lam_im_b, log_dt_b=v_log_dt_b, c_re_b=v_c_re_b, c_im_b=v_c_im_b, b_re=v_b_re, b_im=v_b_im, d_skip=v_d_skip, w_glu=v_w_glu, w_out=v_w_out, w_up=v_w_up, conv_w=v_conv_w, conv_b=v_conv_b, w_down=v_w_down)
    names = list(weights)

    nl, d = x.shape[1], x.shape[2]
    nc = ctx.shape[1]
    n = nl + nc
    f2 = conv_b.shape[1]
    fh = f2 // 2
    d6 = b_mod.shape[1]
    mx, my, mc = lax.axis_index("x"), lax.axis_index("y"), lax.axis_index("c")
    chip = 2 * mx + my
    me = 4 * mx + 2 * my + mc
    tr = _pick(math.gcd(nl, nc), (256, 128, 64, 32, 16))
    nlb, nb = nl // tr, n // tr

    big_names = ["w_in", "w_uq", "w_ukv", "w_o_attn", "w_glu", "w_out", "w_up", "w_down"]
    row_sharded = ("w_out", "w_down")
    halves_in = [weights[k][0].astype(MXU_DTYPE).reshape(2, weights[k].shape[1] // 2, weights[k].shape[2]) for k in big_names]
    my_halves = _exchange_chips(halves_in, "gather_weights", True)
    gathered = _sibling_exchange([t.reshape(-1, t.shape[2]) for t in my_halves], "gather_weight_halves")
    full = {}
    for k_, gth in zip(big_names, gathered):
        r_, c_ = weights[k_].shape[1:]
        g4 = gth.reshape(2, 4, r_ // 2, c_)
        full[k_] = (jnp.transpose(g4, (1, 0, 2, 3)).reshape(4 * r_, c_) if k_ in row_sharded
                    else jnp.transpose(g4, (0, 2, 1, 3)).reshape(r_, 4 * c_))

    cwid = conv_w.shape[2]
    sw = -(-max(d, cwid) // 128) * 128
    small_in = jnp.concatenate([jnp.pad(c, ((0, 0), (0, sw - d))), jnp.pad(conv_w[0], ((0, 4), (0, sw - cwid)))], axis=0)
    (small_all,) = _exchange8([small_in], "gather_c", True)
    cs = small_all[:, 0, :d]
    conv_w_full = jnp.concatenate([small_all[2 * j, 1:4, :cwid] for j in range(4)], axis=1)
    cs16 = jnp.concatenate([cs, c_ctx[None, :], jnp.zeros((7, d), F32)], axis=0)

    csh = w_mod.shape[2]
    b_mod_sh = lax.dynamic_slice(b_mod, (0, chip * csh), (1, csh))

    def mod_fwd_body(c_ref, w_ref, b_ref, o_ref):
        a = _silu(c_ref[...]).astype(MXU_DTYPE)
        o_ref[...] = jnp.dot(a, w_ref[...].astype(MXU_DTYPE), preferred_element_type=F32) + b_ref[...]

    mod_sh = pl.pallas_call(mod_fwd_body, name="mod_fwd", out_shape=_sds((16, csh), F32),
                            compiler_params=pltpu.CompilerParams(vmem_limit_bytes=VMEM_LIMIT))(cs16, w_mod[0], b_mod_sh)
    (mod_all,) = _exchange8([mod_sh], "gather_mod", True)
    mod_full = jnp.concatenate([mod_all[2 * j] for j in range(4)], axis=1)
    modv = jnp.stack([lax.dynamic_slice(mod_full, (me, 0), (1, d6)), mod_full[8:9]])

    def mod_parts(m):
        return [m[:, j * d:(j + 1) * d] for j in range(6)]

    u_off, kv_off, kr_off = 2 * d, 2 * d + SSM_WIDTH, 2 * d + SSM_WIDTH + KV_LORA
    q_off = -(-(kr_off + SLOT) // Q_LORA) * Q_LORA
    zw = q_off + Q_LORA
    wi = full["w_in"]
    s0, s1, s2, s3 = Q_LORA, Q_LORA + KV_LORA, Q_LORA + KV_LORA + QK_ROPE, Q_LORA + KV_LORA + QK_ROPE + SSM_WIDTH
    zpad = lambda w_: jnp.zeros((d, w_), MXU_DTYPE)
    win_p = jnp.concatenate([wi[:, s3:], wi[:, s2:s3], wi[:, s0:s1], wi[:, s1:s2], zpad(SLOT - QK_ROPE),
                             zpad(q_off - kr_off - SLOT), wi[:, :s0]], axis=1)
    wuq_p = jnp.pad(full["w_uq"].reshape(Q_LORA, N_HEADS, QK_DIM), ((0, 0), (0, 0), (0, SLOT - QK_DIM))).reshape(Q_LORA, N_HEADS * SLOT)
    wukv3 = full["w_ukv"].reshape(KV_LORA, N_HEADS, QK_NOPE + V_DIM)
    padh = lambda t: jnp.pad(t, ((0, 0), (0, 0), (0, SLOT - t.shape[2]))).reshape(t.shape[0], N_HEADS * SLOT)
    wukv_p = jnp.concatenate([padh(wukv3[:, :, :QK_NOPE]), padh(wukv3[:, :, QK_NOPE:])], axis=1)
    wo_p = jnp.pad(full["w_o_attn"].reshape(N_HEADS, V_DIM, d), ((0, 0), (0, SLOT - V_DIM), (0, 0))).reshape(N_HEADS * SLOT, d)
    wglu, wout, wup, wdown = full["w_glu"], full["w_out"], full["w_up"], full["w_down"]
    hw = N_HEADS * SLOT
    gain_p = lambda g_: jnp.tile(jnp.pad(g_[0], (0, SLOT - QK_DIM)), N_HEADS)[None, :]
    qg_p, kg_p = gain_p(q_norm_g), gain_p(k_norm_g)

    tok = jnp.arange(nl)
    freqs = ROPE_THETA ** (-jnp.arange(QK_ROPE // 4, dtype=F32) / (QK_ROPE // 4))
    ang = jnp.concatenate([(tok // GRID_W)[:, None] * freqs, (tok % GRID_W)[:, None] * freqs], axis=-1)
    cos_t = jnp.concatenate([jnp.cos(ang), jnp.ones((nc, 16), F32)], axis=0)
    sin_t = jnp.concatenate([jnp.sin(ang), jnp.zeros((nc, 16), F32)], axis=0)
    zl = lambda w_: jnp.zeros((n, w_), F32)
    rope_c = jnp.concatenate([jnp.ones((n, QK_NOPE), F32), cos_t, cos_t, zl(SLOT - QK_DIM)], axis=1)
    rope_sa = jnp.concatenate([zl(QK_NOPE), -sin_t, zl(SLOT - QK_NOPE - 16)], axis=1)
    rope_sb = jnp.concatenate([zl(QK_NOPE + 16), sin_t, zl(SLOT - QK_DIM)], axis=1)

    dirs = (("f", lam_re_f, lam_im_f, log_dt_f, c_re_f, c_im_f, False), ("b", lam_re_b, lam_im_b, log_dt_b, c_re_b, c_im_b, True))
    bbd, cbd_t, cbd_n, bbd_t, lamc, lamc_adj, disc_vjps = [], [], [], [], [], [], []
    for _, l_re, l_im, l_dt, cr_, ci_, rev_ in dirs:
        (lbr, lbi, bbr, bbi), vjp = jax.vjp(_s5_disc, l_re[0], l_im[0], l_dt[0], b_re[0], b_im[0])
        disc_vjps.append(vjp)
        bb = _diag_blocks(jnp.transpose(bbr, (0, 2, 1)), jnp.transpose(bbi, (0, 2, 1))).astype(MXU_DTYPE)
        cc_ = _diag_blocks(cr_[0], -ci_[0]).astype(MXU_DTYPE)
        bbd.append(bb)
        bbd_t.append(_block_transpose(bb))
        cbd_t.append(cc_)
        cbd_n.append(_block_transpose(cc_))
        lamc.append(_lam_consts(lbr, lbi, rev_, False))
        lamc_adj.append(_lam_consts(lbr, lbi, not rev_, True))
    t_scan = tr

    xa = jnp.concatenate([x[0], ctx[0]], axis=0)
    n1g, n2g = norm1_g, norm2_g

    def norm1_body(xv, m, g):
        sh1, sc1 = m[:, :d], m[:, d:2 * d]
        return _rms_fwd(xv, g, d) * (1.0 + sc1) + sh1

    (h1,) = _rowwise(norm1_body, name="norm1_fwd", nblk=nb, tr=tr, rows=[(xa, 0, d, 0)], sels=[modv], fulls=[n1g],
                     outs=[(d, MXU_DTYPE)], seg=nlb)
    z = _mm(h1, win_p, "nn", "in_proj")
    gl_cb, u_cb, kv_cb, kr_cb, q_cb = 0, u_off // SSM_WIDTH, kv_off // KV_LORA, kr_off // SLOT, q_off // Q_LORA

    (cqn,) = _rowwise(lambda v, g: _rms_fwd(v, g, Q_LORA), name="qa_norm_fwd", nblk=nlb, tr=tr,
                      rows=[(z, q_cb, Q_LORA, 0)], fulls=[q_a_g], outs=[(Q_LORA, MXU_DTYPE)])
    qh = _mm(cqn, wuq_p, "nn", "q_up")

    def qhead_body(qv, cv, sav, sbv, g):
        return jnp.concatenate([_rope_fwd(_rms_fwd(t, g[:, :SLOT], QK_DIM), cv, sav, sbv) for t in _heads(qv)], axis=1)

    rope_rows = lambda: [(rope_c, 0, SLOT, 0), (rope_sa, 0, SLOT, 0), (rope_sb, 0, SLOT, 0)]
    (q_p,) = _rowwise(qhead_body, name="q_head_fwd", nblk=nlb, tr=tr, rows=[(qh, 0, hw, 0)] + rope_rows(),
                      fulls=[qg_p], outs=[(hw, MXU_DTYPE)])

    (ckvn,) = _rowwise(lambda v, g: _rms_fwd(v, g, KV_LORA), name="kva_norm_fwd", nblk=nb, tr=tr,
                       rows=[(z, kv_cb, KV_LORA, 0)], fulls=[kv_a_g], outs=[(KV_LORA, MXU_DTYPE)])
    kvpre = _mm(ckvn, wukv_p, "nn", "kv_up")

    def khead_body(kv_, vv_, krv, cv, sav, sbv, g):
        kpe = pltpu.roll(krv, QK_NOPE, 1)
        ks = [_rope_fwd(_rms_fwd(t + kpe, g[:, :SLOT], QK_DIM), cv, sav, sbv) for t in _heads(kv_)]
        return jnp.concatenate(ks, axis=1), vv_

    k_p, v_p = _rowwise(khead_body, name="k_head_fwd", nblk=nb, tr=tr,
                        rows=[(kvpre, 0, hw, 0), (kvpre, 1, hw, 0), (z, kr_cb, SLOT, 0)] + rope_rows(),
                        fulls=[kg_p], outs=[(hw, MXU_DTYPE), (hw, MXU_DTYPE)])

    scale = QK_DIM ** -0.5
    o_p, lse = _attn_fwd(q_p, k_p, v_p, nl, scale)
    a_l = _mm(o_p, wo_p, "nn", "attn_out")

    scans = [_s5_scan(z, u_cb, bbd[j], cbd_n[j], lamc[j], t_scan, nl, dirs[j][6], "s5_scan_" + dirs[j][0]) for j in range(2)]
    xs, ydir = [s_[0] for s_ in scans], [s_[1] for s_ in scans]

    def ssm_out_body(uv, a, b, dsk):
        ys = uv * dsk + a + b
        return ys, _gelu(ys)

    ys, ge = _rowwise(ssm_out_body, name="s5_out_fwd", nblk=nlb, tr=tr,
                      rows=[(z, u_cb, SSM_WIDTH, 0), (ydir[0], 0, SSM_WIDTH, 0), (ydir[1], 0, SSM_WIDTH, 0)],
                      fulls=[d_skip], outs=[(SSM_WIDTH, F32), (SSM_WIDTH, MXU_DTYPE)])
    glu_out = _mm(ge, wglu, "nn", "glu_proj")

    def merge_body(ga, gs, av, val, gate):
        return _sigmoid(ga) * av + _sigmoid(gs) * (val * _sigmoid(gate))

    merge_rows = lambda: [(z, 0, d, 0), (z, 1, d, 0), (a_l, 0, d, 0), (glu_out, 0, d, 0), (glu_out, 1, d, 0)]
    (merged,) = _rowwise(merge_body, name="merge_fwd", nblk=nlb, tr=tr, rows=merge_rows(), outs=[(d, MXU_DTYPE)])
    mo = _mm(merged, wout, "nn", "out_proj")
    mod_x = modv[0]

    def norm2_body(xv, mov, m, g):
        g1, sh2, sc2 = m[:, 2 * d:3 * d], m[:, 3 * d:4 * d], m[:, 4 * d:5 * d]
        x1v = xv + g1 * mov
        return x1v, _rms_fwd(x1v, g, d) * (1.0 + sc2) + sh2

    x1, h2 = _rowwise(norm2_body, name="norm2_fwd", nblk=nlb, tr=tr, rows=[(xa, 0, d, 0), (mo, 0, d, 0)],
                      fulls=[mod_x, n2g], outs=[(d, F32), (d, MXU_DTYPE)])
    up = _mm(h2, wup, "nn", "ffn_up")
    cw8 = jnp.zeros((8, f2), F32).at[:3].set(conv_w_full)

    def conv3(t3, w8, off):
        p_, c_, n_ = t3
        return p_ * w8[0:1, off:off + fh] + c_ * w8[1:2, off:off + fh] + n_ * w8[2:3, off:off + fh]

    def conv_fwd_body(val3, gate3, w8, bias):
        val2 = conv3(val3, w8, 0) + bias[:, :fh]
        gate2 = conv3(gate3, w8, fh) + bias[:, fh:]
        return _silu(gate2) * val2

    (act,) = _rowwise(conv_fwd_body, name="conv_fwd", nblk=nlb, tr=tr, rows=[(up, 0, fh, 0), (up, 1, fh, 0)],
                      halo=(0, 1), fulls=[cw8, conv_b], outs=[(fh, MXU_DTYPE)])
    dn = _mm(act, wdown, "nn", "ffn_down")
    tgt = loss_target[0]

    def loss_body(x1v, dnv, tv, m):
        g2 = m[:, 5 * d:6 * d]
        e = x1v + g2 * dnv - tv
        dx2v = e * (1.0 / d)
        return dx2v, dx2v * g2, e * e, dx2v * dnv

    dx2, ddn, loss_acc, dg2_acc = _rowwise(loss_body, name="loss", nblk=nlb, tr=tr,
                                           rows=[(x1, 0, d, 0), (dn, 0, d, 0), (tgt, 0, d, 0)], fulls=[mod_x],
                                           outs=[(d, F32), (d, MXU_DTYPE)], accs=[d, d])
    loss = lax.psum(0.5 / d * jnp.sum(loss_acc), ("x", "y", "c"))

    g_big = {}
    dact = _mm(ddn, wdown, "nt", "ffn_down_dx")
    g_big["w_down"] = _mm(act, ddn, "tn", "ffn_down_dw")

    def conv_bwd_body(val3, gate3, da, w8, bias):
        val2 = conv3(val3, w8, 0) + bias[:, :fh]
        gate2 = conv3(gate3, w8, fh) + bias[:, fh:]
        dval2 = da * _silu(gate2)
        dgate2 = da * val2 * _dsilu(gate2)
        du2 = jnp.concatenate([dval2, dgate2], axis=1)
        taps = [jnp.concatenate([dval2 * val3[j], dgate2 * gate3[j]], axis=1) for j in range(3)]
        return du2, du2, taps[0], taps[1], taps[2]

    du2, dcb_acc, dcw0, dcw1, dcw2 = _rowwise(conv_bwd_body, name="conv_bwd", nblk=nlb, tr=tr,
                                              rows=[(up, 0, fh, 0), (up, 1, fh, 0), (dact, 0, fh, 0)], halo=(0, 1),
                                              fulls=[cw8, conv_b], outs=[(f2, F32)], accs=[f2, f2, f2, f2])

    def conv_t_body(dval3, dgate3, w8):
        rev = lambda t3: (t3[2], t3[1], t3[0])
        return jnp.concatenate([conv3(rev(dval3), w8, 0), conv3(rev(dgate3), w8, fh)], axis=1)

    (dup,) = _rowwise(conv_t_body, name="conv_bwd_dx", nblk=nlb, tr=tr, rows=[(du2, 0, fh, 0), (du2, 1, fh, 0)],
                      halo=(0, 1), fulls=[cw8], outs=[(f2, MXU_DTYPE)])
    dh2 = _mm(dup, wup, "nt", "ffn_up_dx")
    g_big["w_up"] = _mm(h2, dup, "tn", "ffn_up_dw")

    def norm2_bwd_body(x1v, dh, dx2v, mov, m, g):
        g1, sc2 = m[:, 2 * d:3 * d], m[:, 4 * d:5 * d]
        y = _rms_fwd(x1v, g, d)
        dxn, dgc = _rms_bwd(x1v, g, dh * (1.0 + sc2), d)
        dx1v = dx2v + dxn
        return dx1v, dx1v * g1, dgc, dh, dh * y, dx1v * mov

    dx1, dmo, dn2g_acc, dsh2_acc, dsc2_acc, dg1_acc = _rowwise(
        norm2_bwd_body, name="norm2_bwd", nblk=nlb, tr=tr,
        rows=[(x1, 0, d, 0), (dh2, 0, d, 0), (dx2, 0, d, 0), (mo, 0, d, 0)], fulls=[mod_x, n2g],
        outs=[(d, F32), (d, MXU_DTYPE)], accs=[d, d, d, d])
    dmerged = _mm(dmo, wout, "nt", "out_proj_dx")
    g_big["w_out"] = _mm(merged, dmo, "tn", "out_proj_dw")

    def merge_bwd_body(ga, gs, av, val, gate, dm):
        sa_, ss_, sg_ = _sigmoid(ga), _sigmoid(gs), _sigmoid(gate)
        s_l = val * sg_
        ds_l = dm * ss_
        dga = dm * av * sa_ * (1.0 - sa_)
        dgs = dm * s_l * ss_ * (1.0 - ss_)
        dval = ds_l * sg_
        dgate = ds_l * val * sg_ * (1.0 - sg_)
        return dm * sa_, jnp.concatenate([dval, dgate], axis=1), jnp.concatenate([dga, dgs], axis=1)

    da_l, dglu, dgl = _rowwise(merge_bwd_body, name="merge_bwd", nblk=nlb, tr=tr,
                               rows=merge_rows() + [(dmerged, 0, d, 0)],
                               outs=[(d, MXU_DTYPE), (2 * d, MXU_DTYPE), (2 * d, MXU_DTYPE)])
    dge = _mm(dglu, wglu, "nt", "glu_proj_dx")
    g_big["w_glu"] = _mm(ge, dglu, "tn", "glu_proj_dw")

    def ssm_out_bwd_body(ysv, dgev, uv, dsk):
        dys_ = dgev * _dgelu(ysv)
        return dys_, dys_ * dsk, dys_ * uv

    dys, du_skip, ddskip_acc = _rowwise(ssm_out_bwd_body, name="s5_out_bwd", nblk=nlb, tr=tr,
                                        rows=[(ys, 0, SSM_WIDTH, 0), (dge, 0, SSM_WIDTH, 0), (z, u_cb, SSM_WIDTH, 0)],
                                        fulls=[d_skip], outs=[(SSM_WIDTH, F32), (SSM_WIDTH, F32)], accs=[SSM_WIDTH])
    s5b = [_s5_bwd(dys, z, u_cb, xs[j], cbd_t[j], bbd_t[j], lamc_adj[j], t_scan, nl, dirs[j][6], "s5_bwd_" + dirs[j][0])
           for j in range(2)]
    du_nat = s5b[0][0] + s5b[1][0] + jnp.concatenate([du_skip, jnp.zeros((nc, SSM_WIDTH), F32)], axis=0)

    do_f = _mm(da_l, wo_p, "nt", "attn_out_dx")
    g_wo_p = _mm(o_p, da_l, "tn", "attn_out_dw")

    dq_t, dk_p, dv_p = _attn_bwd(q_p, k_p, jnp.transpose(k_p), v_p, do_f, o_p, lse, nl, scale)
    dq_p = jnp.transpose(dq_t)

    def qhead_bwd_body(qv, dqv, cv, sav, sbv, g):
        dxs, dgs = [], []
        for t, dt_ in zip(_heads(qv), _heads(dqv)):
            dx_, dg_ = _rms_bwd(t, g[:, :SLOT], _rope_bwd(dt_, cv, sav, sbv), QK_DIM)
            dxs.append(dx_)
            dgs.append(dg_)
        return jnp.concatenate(dxs, axis=1), jnp.concatenate(dgs, axis=1)

    dqh, dqg_acc = _rowwise(qhead_bwd_body, name="q_head_bwd", nblk=nlb, tr=tr,
                            rows=[(qh, 0, hw, 0), (dq_p, 0, hw, 0)] + rope_rows(), fulls=[qg_p],
                            outs=[(hw, MXU_DTYPE)], accs=[hw])
    dcqn = _mm(dqh, wuq_p, "nt", "q_up_dx")
    g_wuq_p = _mm(cqn, dqh, "tn", "q_up_dw")
    dcq, dqag_acc = _rowwise(lambda v, dy, g: _rms_bwd(v, g, dy, Q_LORA), name="qa_norm_bwd", nblk=nlb, tr=tr,
                             rows=[(z, q_cb, Q_LORA, 0), (dcqn, 0, Q_LORA, 0)], fulls=[q_a_g],
                             outs=[(Q_LORA, MXU_DTYPE)], accs=[Q_LORA])

    def khead_bwd_body(kv_, krv, dkv_, dvv_, cv, sav, sbv, g):
        kpe = pltpu.roll(krv, QK_NOPE, 1)
        lane = lax.broadcasted_iota(jnp.int32, krv.shape, 1)
        dxs, dgs, dkr_ = [], [], jnp.zeros(krv.shape, F32)
        for t, dt_ in zip(_heads(kv_), _heads(dkv_)):
            dx_, dg_ = _rms_bwd(t + kpe, g[:, :SLOT], _rope_bwd(dt_, cv, sav, sbv), QK_DIM)
            dxs.append(jnp.where(lane < QK_NOPE, dx_, 0.0))
            dgs.append(dg_)
            dkr_ = dkr_ + dx_
        dkr_ = jnp.where(lane < QK_ROPE, pltpu.roll(dkr_, SLOT - QK_NOPE, 1), 0.0)
        return jnp.concatenate(dxs + [dvv_], axis=1), dkr_, jnp.concatenate(dgs, axis=1)

    dkvpre, dkr, dkg_acc = _rowwise(khead_bwd_body, name="k_head_bwd", nblk=nb, tr=tr,
                                    rows=[(kvpre, 0, hw, 0), (z, kr_cb, SLOT, 0), (dk_p, 0, hw, 0), (dv_p, 0, hw, 0)] + rope_rows(),
                                    fulls=[kg_p], outs=[(2 * hw, MXU_DTYPE), (SLOT, MXU_DTYPE)], accs=[hw])
    dckvn = _mm(dkvpre, wukv_p, "nt", "kv_up_dx")
    g_wukv_p = _mm(ckvn, dkvpre, "tn", "kv_up_dw")
    dckv, dkvag_acc = _rowwise(lambda v, dy, g: _rms_bwd(v, g, dy, KV_LORA), name="kva_norm_bwd", nblk=nb, tr=tr,
                               rows=[(z, kv_cb, KV_LORA, 0), (dckvn, 0, KV_LORA, 0)], fulls=[kv_a_g],
                               outs=[(KV_LORA, MXU_DTYPE)], accs=[KV_LORA])

    padc = lambda t: jnp.concatenate([t, jnp.zeros((nc, t.shape[1]), t.dtype)], axis=0)
    dz = jnp.concatenate([padc(dgl), du_nat.astype(MXU_DTYPE), dckv, dkr,
                          jnp.zeros((n, q_off - kr_off - SLOT), MXU_DTYPE), padc(dcq)], axis=1)
    dh1 = _mm(dz, win_p, "nt", "in_proj_dx")
    g_win_p = _mm(h1, dz, "tn", "in_proj_dw")

    def norm1_bwd_body(xv, dh, m, g):
        sc1 = m[:, d:2 * d]
        y = _rms_fwd(xv, g, d)
        dxn, dgc = _rms_bwd(xv, g, dh * (1.0 + sc1), d)
        return dxn, dgc, dh, dh * y

    dxa, dn1g_acc, dsh1_acc, dsc1_acc = _rowwise(norm1_bwd_body, name="norm1_bwd", nblk=nb, tr=tr,
                                                 rows=[(xa, 0, d, 0), (dh1, 0, d, 0)], sels=[modv], fulls=[n1g],
                                                 outs=[(d, F32)], accs=[d, d, d], seg=nlb)
    grad_x = (dxa[:nl] + dx1)[None]

    red8 = lambda a: jnp.sum(a, axis=-2)
    dmod_own = jnp.concatenate([red8(dsh1_acc[0]), red8(dsc1_acc[0]), red8(dg1_acc), red8(dsh2_acc), red8(dsc2_acc), red8(dg2_acc)])
    dmod_ctx = jnp.concatenate([red8(dsh1_acc[1]), red8(dsc1_acc[1]), jnp.zeros((4 * d,), F32)])
    dm_in = jnp.concatenate([dmod_own[None, :], dmod_ctx[None, :], jnp.zeros((6, d6), F32)], axis=0)
    (dm_all,) = _exchange8([dm_in], "gather_dmod", True)
    dm_own_sh = lax.dynamic_slice(dm_all[:, 0, :], (0, chip * csh), (8, csh))
    dm_ctx_sh = lax.dynamic_slice(dm_all[:, 1, :], (0, chip * csh), (8, csh))

    def mod_bwd_body(c_ref, own_ref, ctx_ref, w_ref, gw_ref, gb_ref, gc_ref):
        cv = c_ref[...]
        a = _silu(cv).astype(MXU_DTYPE)
        own = own_ref[...]
        ctx_tot = ctx_ref[0:1, :]
        for j in range(1, 8):
            ctx_tot = ctx_tot + ctx_ref[j:j + 1, :]
        g16 = jnp.concatenate([own, jnp.broadcast_to(ctx_tot, own.shape)], axis=0)
        rid = lax.broadcasted_iota(jnp.int32, g16.shape, 0)
        g16 = jnp.where(rid <= 8, g16, 0.0)
        gw_ref[...] = lax.dot_general(a, g16.astype(MXU_DTYPE), (((0,), (0,)), ((), ())), preferred_element_type=F32)
        gb_ref[...] = jnp.broadcast_to(jnp.sum(own, axis=0, keepdims=True) + ctx_tot, gb_ref.shape)
        gc = lax.dot_general(jnp.broadcast_to(ctx_tot, own.shape).astype(MXU_DTYPE), w_ref[...].astype(MXU_DTYPE),
                             (((1,), (1,)), ((), ())), preferred_element_type=F32)
        gc_ref[...] = gc * _dsilu(cv[8:9, :])

    g_wmod, g_bmod_sh, g_cctx_part = pl.pallas_call(
        mod_bwd_body, name="mod_bwd", out_shape=[_sds((d, csh), F32), _sds((8, csh), F32), _sds((8, d), F32)],
        compiler_params=pltpu.CompilerParams(vmem_limit_bytes=VMEM_LIMIT))(cs16, dm_own_sh, dm_ctx_sh, w_mod[0])
    north = (mc == 0).astype(F32)
    g_bmod_part = lax.dynamic_update_slice(jnp.zeros((1, d6), F32), g_bmod_sh[0:1] * north, (0, chip * csh))
    g_cctx_part = g_cctx_part[0] * north

    small_g = {}
    for j, dr in enumerate(dirs):
        sfx = dr[0]
        _, dbbd_j, dcbd_j, dlam_j = s5b[j]
        dl = red8(dlam_j).reshape(N_CG, 2, CG_STATES)
        db_re, db_im = _diag_extract(dbbd_j)
        cot = (dl[:, 0].reshape(SSM_GROUPS, SSM_STATE), dl[:, 1].reshape(SSM_GROUPS, SSM_STATE),
               jnp.transpose(db_re, (0, 2, 1)), jnp.transpose(db_im, (0, 2, 1)))
        g_lre, g_lim, g_ldt, g_bre, g_bim = disc_vjps[j](cot)
        small_g["lam_re_" + sfx], small_g["lam_im_" + sfx], small_g["log_dt_" + sfx] = g_lre, g_lim, g_ldt
        small_g["b_re"] = small_g.get("b_re", 0.0) + g_bre
        small_g["b_im"] = small_g.get("b_im", 0.0) + g_bim
        dc_re, dc_im = _diag_extract(dcbd_j)
        small_g["c_re_" + sfx], small_g["c_im_" + sfx] = dc_re, -dc_im
    head_fold = lambda acc: jnp.sum(red8(acc).reshape(N_HEADS, SLOT), axis=0)[:QK_DIM]
    small_g.update(c_ctx=g_cctx_part, b_mod=g_bmod_part[0], norm1_g=red8(dn1g_acc[0]) + red8(dn1g_acc[1]),
                   norm2_g=red8(dn2g_acc), q_a_g=red8(dqag_acc), kv_a_g=red8(dkvag_acc), q_norm_g=head_fold(dqg_acc),
                   k_norm_g=head_fold(dkg_acc), d_skip=red8(ddskip_acc), conv_b=red8(dcb_acc))
    g_convw_full = jnp.stack([red8(dcw0), red8(dcw1), red8(dcw2)])
    small_names = ["c_ctx", "b_mod", "norm1_g", "norm2_g", "q_a_g", "kv_a_g", "q_norm_g", "k_norm_g",
                   "lam_re_f", "lam_im_f", "log_dt_f", "c_re_f", "c_im_f", "lam_re_b", "lam_im_b", "log_dt_b",
                   "c_re_b", "c_im_b", "b_re", "b_im", "d_skip", "conv_b"]
    small_shapes = [weights[k].shape for k in small_names]
    spack = _pack([small_g[k] for k in small_names] + [g_convw_full], rows_mult=8)
    sred = _sum8(_exchange8([spack], "gather_small_grads", True)[0], "sum_small_grads")
    sg_list = _unpack(sred, small_shapes + [(3, f2)])
    g_small = dict(zip(small_names, sg_list[:-1]))
    g_small["conv_w"] = lax.dynamic_slice(sg_list[-1], (0, chip * cwid), (3, cwid))[None]

    gwi = g_win_p
    g_big["w_in"] = jnp.concatenate([gwi[:, q_off:q_off + Q_LORA], gwi[:, kv_off:kv_off + KV_LORA],
                                     gwi[:, kr_off:kr_off + QK_ROPE], gwi[:, u_off:u_off + SSM_WIDTH], gwi[:, :2 * d]], axis=1)
    g_big["w_uq"] = g_wuq_p.reshape(Q_LORA, N_HEADS, SLOT)[:, :, :QK_DIM].reshape(Q_LORA, N_HEADS * QK_DIM)
    gk3 = g_wukv_p[:, :hw].reshape(KV_LORA, N_HEADS, SLOT)[:, :, :QK_NOPE]
    gv3 = g_wukv_p[:, hw:].reshape(KV_LORA, N_HEADS, SLOT)[:, :, :V_DIM]
    g_big["w_ukv"] = jnp.concatenate([gk3, gv3], axis=2).reshape(KV_LORA, N_HEADS * (QK_NOPE + V_DIM))
    g_big["w_o_attn"] = g_wo_p.reshape(N_HEADS, SLOT, d)[:, :V_DIM].reshape(N_HEADS * V_DIM, d)

    def pieces(k_):
        r_, c_ = weights[k_].shape[1:]
        if k_ in row_sharded:
            p4 = jnp.transpose(g_big[k_].reshape(4, 2, r_ // 2, c_), (1, 0, 2, 3))
        else:
            p4 = jnp.transpose(g_big[k_].reshape(2, r_ // 2, 4, c_), (0, 2, 1, 3))
        return p4.reshape(2, 2 * r_, c_)

    pcs = [pieces(k_) for k_ in big_names]
    from_sibling = _sibling_send(pcs, "grads_to_sibling")
    my_half = []
    for k_, p_, got in zip(big_names, pcs, from_sibling):
        rows4, c_ = got.shape
        rh = rows4 // 4
        own = lax.dynamic_index_in_dim(p_, mc, 0, keepdims=False)
        tr_ = _pick(rows4, (256, 128, 64, 32, 16))
        s32, sb = _rowwise(lambda a, b: (a + b, a + b), name="sum_chip_" + k_, nblk=rows4 // tr_, tr=tr_,
                           rows=[(own, 0, c_, 0), (got, 0, c_, 0)], outs=[(c_, F32), (c_, MXU_DTYPE)])
        my_half.append((s32, sb, rh, c_))
    recv3 = _exchange_chips([sb.reshape(4, rh, c_) for _, sb, rh, c_ in my_half], "scatter_weight_grads", False)
    reduced = []
    for k_, (s32, _, rh, c_), r3 in zip(big_names, my_half, recv3):
        mine = lax.dynamic_slice(s32, (chip * rh, 0), (rh, c_))
        tr_ = _pick(rh, (256, 128, 64, 32, 16))
        (red,) = _rowwise(lambda a, b0, b1, b2: a + b0 + b1 + b2, name="sum_grad_" + k_, nblk=rh // tr_, tr=tr_,
                          rows=[(mine, 0, c_, 0)] + [(r3.reshape(3 * rh, c_), 0, c_, j * rh) for j in range(3)],
                          outs=[(c_, F32)])
        reduced.append(red)
    both = _sibling_exchange(reduced, "exchange_halves")
    g_sh = {k_: b_.reshape((1,) + weights[k_].shape[1:]) for k_, b_ in zip(big_names, both)}
    g_sh["w_mod"] = g_wmod[None]

    grads = {**g_sh, **g_small}
    outs_d, outs_m, outs_v = {}, {}, {}
    for k_ in ["w_mod"] + big_names:
        shp = weights[k_].shape
        res = _adamw(*[t.reshape(shp[1:]) for t in (grads[k_], weights[k_], mom_m[k_], mom_v[k_])], "adamw_" + k_)
        for dst, buf in zip((outs_d, outs_m, outs_v), res):
            dst[k_] = buf.reshape(shp)
    adam_small = small_names + ["conv_w"]
    shapes = [weights[k_].shape for k_ in adam_small]
    res = _adamw(*[_pack([src[k_] for k_ in adam_small], rows_mult=8) for src in (grads, weights, mom_m, mom_v)], "adamw_small")
    for dst, buf in zip((outs_d, outs_m, outs_v), res):
        dst.update(zip(adam_small, _unpack(buf, shapes)))
    grads = {k_: grads[k_].reshape(weights[k_].shape) for k_ in names}
    return (loss, grad_x, *[grads[k_] for k_ in names], *[outs_d[k_] for k_ in names],
            *[outs_m[k_] for k_ in names], *[outs_v[k_] for k_ in names])
```

```python
import functools
import math

import numpy as np
import jax
import jax.numpy as jnp
from jax import lax
from jax.experimental import pallas as pl
from jax.experimental.pallas import tpu as pltpu

F32 = jnp.float32
MXU_DTYPE = jnp.bfloat16
MESH = pl.DeviceIdType.MESH

EPS = 1e-6
N_HEADS = 8
QK_NOPE = 64
QK_ROPE = 32
QK_DIM = QK_NOPE + QK_ROPE
V_DIM = 64
SLOT = 128
Q_LORA = 384
KV_LORA = 256
GRID_W = 64
ROPE_THETA = 10000.0
SSM_WIDTH = 512
SSM_GROUP = 16
SSM_GROUPS = 32
SSM_STATE = 64
N_STATE = SSM_GROUPS * SSM_STATE
CG_STATES = 512
N_CG = N_STATE // CG_STATES
CG_CHANNELS = SSM_WIDTH // N_CG
SCAN_LANES = 512
PACK_W = 1024

ADAM_LR = 0.001
ADAM_B1 = 0.9
ADAM_B2 = 0.999
ADAM_EPS = 1e-08
ADAM_WD = 0.01
ADAM_STEP = 10

VMEM_LIMIT = 56 * 1024 * 1024
LOG2E = 1.4426950408889634


def _pick(n, cands):
    for c in cands:
        if c <= n and n % c == 0:
            return c
    return n


def _cparams(sem):
    return pltpu.CompilerParams(dimension_semantics=sem, vmem_limit_bytes=VMEM_LIMIT)


def _sds(shape, dtype):
    return jax.ShapeDtypeStruct(tuple(shape), dtype)


_K_CANDS = (2816, 2048, 1536, 1408, 1280, 1152, 1024, 896, 768, 704, 640, 512, 384, 256, 128, 64, 32, 16)
_M_CANDS = (2048, 1408, 1024, 768, 512, 384, 256, 128, 64, 32, 16)
_N_CANDS = (1408, 1152, 1024, 768, 512, 384, 256, 128)
MM_VMEM_BUDGET = 40 * 1024 * 1024


def _mm_tiles(m, n, k_opts, a_bytes, b_bytes, o_bytes, m_cands):
    tn = n if n <= _N_CANDS[0] else _pick(n, _N_CANDS)
    for tk in k_opts:
        for tm in ((m,) if m <= m_cands[0] else ()) + tuple(t for t in m_cands if t < m and m % t == 0):
            if 2 * (tm * tk * a_bytes + tk * tn * b_bytes + tm * tn * o_bytes) + tm * tn * 4 <= MM_VMEM_BUDGET:
                return tm, tn, tk
    raise ValueError("no matmul tiling fits")


def _mm(a, b, mode, name, out_dtype=F32, rows=None, a_off=0, b_off=0):
    a_bytes, b_bytes, o_bytes = a.dtype.itemsize, b.dtype.itemsize, jnp.dtype(out_dtype).itemsize
    if mode == "tn":
        t_rows = rows or a.shape[0]
        m, n = a.shape[1], b.shape[1]
        k_opts = tuple(t for t in _K_CANDS if t <= t_rows and t_rows % t == 0) or (t_rows,)
        tm, tn, tk = _mm_tiles(m, n, k_opts, a_bytes, b_bytes, o_bytes, _M_CANDS[1:])
        nk = t_rows // tk
        ao, bo = a_off // tk, b_off // tk
        grid = (m // tm, n // tn, nk)
        in_specs = [pl.BlockSpec((tk, tm), lambda i, j, k: (k + ao, i)),
                    pl.BlockSpec((tk, tn), lambda i, j, k: (k + bo, j))]
        dn = (((0,), (0,)), ((), ()))
    else:
        m = rows or a.shape[0]
        kdim = a.shape[1]
        n = b.shape[1] if mode == "nn" else b.shape[0]
        k_opts = (kdim,) + tuple(t for t in _K_CANDS if t < kdim and kdim % t == 0)
        tm, tn, tk = _mm_tiles(m, n, k_opts, a_bytes, b_bytes, o_bytes, _M_CANDS)
        nk = kdim // tk
        ao = a_off // tm
        grid = (m // tm, n // tn, nk)
        if mode == "nn":
            in_specs = [pl.BlockSpec((tm, tk), lambda i, j, k: (i + ao, k)),
                        pl.BlockSpec((tk, tn), lambda i, j, k: (k, j))]
            dn = (((1,), (0,)), ((), ()))
        else:
            in_specs = [pl.BlockSpec((tm, tk), lambda i, j, k: (i + ao, k)),
                        pl.BlockSpec((tn, tk), lambda i, j, k: (j, k))]
            dn = (((1,), (1,)), ((), ()))
    use_scratch = nk > 1 and out_dtype != F32

    def body(a_ref, b_ref, o_ref, *scr):
        r = lax.dot_general(a_ref[...].astype(MXU_DTYPE), b_ref[...].astype(MXU_DTYPE), dn,
                            preferred_element_type=F32)
        if nk == 1:
            o_ref[...] = r.astype(o_ref.dtype)
        else:
            k = pl.program_id(2)
            acc = scr[0] if use_scratch else o_ref

            @pl.when(k == 0)
            def _():
                acc[...] = r

            @pl.when(k > 0)
            def _():
                acc[...] += r

            if use_scratch:
                @pl.when(k == nk - 1)
                def _():
                    o_ref[...] = acc[...].astype(o_ref.dtype)

    return pl.pallas_call(
        body, name=name, grid=grid, in_specs=in_specs,
        out_specs=pl.BlockSpec((tm, tn), lambda i, j, k: (i, j)),
        out_shape=_sds((m, n), out_dtype),
        scratch_shapes=[pltpu.VMEM((tm, tn), F32)] if use_scratch else [],
        compiler_params=_cparams(("parallel", "parallel", "arbitrary")),
    )(a, b)


def _rowwise(body, *, name, nblk, tr, rows=(), halo=(), sels=(), fulls=(), outs=(), accs=(), seg=None):
    n_rows, n_sel, n_full, n_out, n_acc = len(rows), len(sels), len(fulls), len(outs), len(accs)
    halo = tuple(halo)
    maxw = max([r[2] for r in rows] + [o[0] for o in outs] + list(accs))
    sr = _pick(tr, tuple(s for s in (256, 128, 64, 32, 16) if s * maxw <= 131072) or (16,))
    nsub = tr // sr
    total8 = nblk * tr // 8

    def seg_of(i):
        return jnp.where(i >= seg, 1, 0) if seg is not None else 0

    in_specs, operands = [], []
    for arr, cb, w, roff in rows:
        ob = roff // tr
        in_specs.append(pl.BlockSpec((tr, w), lambda i, cb=cb, ob=ob: (i + ob, cb)))
        operands.append(arr)
    for h in halo:
        arr, cb, w, roff = rows[h]
        o8, t8 = roff // 8, tr // 8
        in_specs.append(pl.BlockSpec((8, w), lambda i, cb=cb, o8=o8, t8=t8: (jnp.maximum(i * t8 - 1, 0) + o8, cb)))
        in_specs.append(pl.BlockSpec((8, w), lambda i, cb=cb, o8=o8, t8=t8: (jnp.minimum((i + 1) * t8, total8 - 1) + o8, cb)))
        operands += [arr, arr]
    for arr in sels:
        in_specs.append(pl.BlockSpec((None,) + arr.shape[1:], lambda i: (seg_of(i), 0, 0)))
        operands.append(arr)
    for arr in fulls:
        in_specs.append(pl.BlockSpec(arr.shape, lambda i: (0, 0)))
        operands.append(arr)
    out_specs, out_shape = [], []
    for w, dt in outs:
        out_specs.append(pl.BlockSpec((tr, w), lambda i: (i, 0)))
        out_shape.append(_sds((nblk * tr, w), dt))
    for w in accs:
        if seg is None:
            out_specs.append(pl.BlockSpec((8, w), lambda i: (0, 0)))
            out_shape.append(_sds((8, w), F32))
        else:
            out_specs.append(pl.BlockSpec((None, 8, w), lambda i: (seg_of(i), 0, 0)))
            out_shape.append(_sds((2, 8, w), F32))
    n_halo = 2 * len(halo)

    def kern(*refs):
        row_refs = refs[:n_rows]
        halo_refs = refs[n_rows:n_rows + n_halo]
        sel_refs = refs[n_rows + n_halo:n_rows + n_halo + n_sel]
        full_refs = refs[n_rows + n_halo + n_sel:n_rows + n_halo + n_sel + n_full]
        o0 = n_rows + n_halo + n_sel + n_full
        out_refs = refs[o0:o0 + n_out]
        acc_refs = refs[o0 + n_out:o0 + n_out + n_acc]
        i = pl.program_id(0)
        if n_acc:
            first = (i == 0) if seg is None else ((i == 0) | (i == seg))

            @pl.when(first)
            def _():
                for a_ref in acc_refs:
                    a_ref[...] = jnp.zeros(a_ref.shape, F32)

        def sub(s, carry):
            r0 = pl.multiple_of(s * sr, sr)
            vals = []
            for idx, r in enumerate(row_refs):
                cur = r[pl.ds(r0, sr), :]
                if idx in halo:
                    hp = halo_refs[2 * halo.index(idx)]
                    hn = halo_refs[2 * halo.index(idx) + 1]
                    cur = cur.astype(F32)
                    rid = lax.broadcasted_iota(jnp.int32, cur.shape, 0)
                    lo = r[pl.ds(pl.multiple_of(jnp.maximum(r0 - 8, 0), 8), 8), :].astype(F32)
                    lo = jnp.where(s == 0, hp[...].astype(F32), lo)
                    lo = jnp.where((s == 0) & (i == 0), 0.0, lo)
                    hi = r[pl.ds(pl.multiple_of(jnp.minimum(r0 + sr, tr - 8), 8), 8), :].astype(F32)
                    hi = jnp.where(s == nsub - 1, hn[...].astype(F32), hi)
                    hi = jnp.where((s == nsub - 1) & (i == nblk - 1), 0.0, hi)
                    prev = jnp.where(rid == 0, jnp.broadcast_to(lo[7:8, :], cur.shape), pltpu.roll(cur, 1, 0))
                    nxt = jnp.where(rid == sr - 1, jnp.broadcast_to(hi[0:1, :], cur.shape), pltpu.roll(cur, sr - 1, 0))
                    vals.append((prev, cur, nxt))
                else:
                    vals.append(cur)
            res = body(*vals, *[r[...] for r in sel_refs], *[r[...] for r in full_refs])
            if not isinstance(res, (tuple, list)):
                res = (res,)
            for o_ref, v in zip(out_refs, res[:n_out]):
                o_ref[pl.ds(r0, sr), :] = v.astype(o_ref.dtype)
            for a_ref, v in zip(acc_refs, res[n_out:]):
                a_ref[...] += jnp.sum(v.astype(F32).reshape(sr // 8, 8, v.shape[-1]), axis=0)
            return carry

        lax.fori_loop(0, nsub, sub, 0)

    res = pl.pallas_call(
        kern, name=name, grid=(nblk,), in_specs=in_specs, out_specs=out_specs, out_shape=out_shape,
        compiler_params=_cparams(("arbitrary",)),
    )(*operands)
    return res


def _sigmoid(x):
    return 1.0 / (1.0 + jnp.exp(-x))


def _silu(x):
    return x * _sigmoid(x)


def _dsilu(x):
    s = _sigmoid(x)
    return s * (1.0 + x * (1.0 - s))


_GELU_K = math.sqrt(2.0 / math.pi)


def _gelu(x):
    return 0.5 * x * (1.0 + jnp.tanh(_GELU_K * (x + 0.044715 * x * x * x)))


def _dgelu(x):
    t = jnp.tanh(_GELU_K * (x + 0.044715 * x * x * x))
    return 0.5 * (1.0 + t) + 0.5 * x * (1.0 - t * t) * _GELU_K * (1.0 + 3.0 * 0.044715 * x * x)


def _rms_fwd(x, g, width):
    r = lax.rsqrt(jnp.sum(x * x, axis=-1, keepdims=True) * (1.0 / width) + EPS)
    return x * r * g


def _rms_bwd(x, g, dy, width):
    r = lax.rsqrt(jnp.sum(x * x, axis=-1, keepdims=True) * (1.0 / width) + EPS)
    xn = x * r
    dyg = dy * g
    dx = r * (dyg - xn * (jnp.sum(dyg * xn, axis=-1, keepdims=True) * (1.0 / width)))
    return dx, dy * xn


def _rope_fwd(y, c, sa, sb):
    return y * c + pltpu.roll(y, SLOT - 16, 1) * sa + pltpu.roll(y, 16, 1) * sb


def _rope_bwd(d, c, sa, sb):
    return d * c + pltpu.roll(d * sa, 16, 1) + pltpu.roll(d * sb, SLOT - 16, 1)


def _heads(v):
    return [v[:, h * SLOT:(h + 1) * SLOT] for h in range(N_HEADS)]


def _attn_fwd(q, k, v, nl, scale):
    n = k.shape[0]
    tq = _pick(nl, (4096, 2048, 1024, 512, 256, 128))
    tk = _pick(n, (2816, 1408, 1152, 768, 384, 256, 128))
    sub = min(tq, 512)
    nk = n // tk
    rep = tk // SLOT
    c = scale * LOG2E

    def body(q_ref, k_ref, v_ref, o_ref, lse_ref, m_sc, l_sc, acc_sc):
        ki = pl.program_id(2)

        @pl.when(ki == 0)
        def _():
            m_sc[...] = jnp.full(m_sc.shape, -jnp.inf, F32)
            l_sc[...] = jnp.zeros(l_sc.shape, F32)
            acc_sc[...] = jnp.zeros(acc_sc.shape, F32)

        kb, vb = k_ref[...], v_ref[...]
        for sb in range(tq // sub):
            rows = slice(sb * sub, (sb + 1) * sub)
            s = lax.dot_general(q_ref[rows, :], kb, (((1,), (1,)), ((), ())), preferred_element_type=F32)
            m_prev = m_sc[rows, :]
            m_new = jnp.maximum(m_prev, jnp.max(s, axis=1, keepdims=True) * c)
            alpha = jnp.exp2(m_prev - m_new)
            p = jnp.exp2(s * c - jnp.tile(m_new, (1, rep)))
            l_sc[rows, :] = alpha * l_sc[rows, :] + jnp.sum(p, axis=1, keepdims=True)
            acc_sc[rows, :] = alpha * acc_sc[rows, :] + jnp.dot(p.astype(MXU_DTYPE), vb, preferred_element_type=F32)
            m_sc[rows, :] = m_new

        @pl.when(ki == nk - 1)
        def _():
            l = l_sc[...]
            o_ref[...] = (acc_sc[...] / l).astype(o_ref.dtype)
            lse_ref[...] = jnp.transpose(m_sc[...] + jnp.log2(l))[0:8, :]

    return pl.pallas_call(
        body, name="attn_fwd", grid=(N_HEADS, nl // tq, nk),
        in_specs=[pl.BlockSpec((tq, SLOT), lambda h, i, j: (i, h)),
                  pl.BlockSpec((tk, SLOT), lambda h, i, j: (j, h)),
                  pl.BlockSpec((tk, SLOT), lambda h, i, j: (j, h))],
        out_specs=[pl.BlockSpec((tq, SLOT), lambda h, i, j: (i, h)),
                   pl.BlockSpec((None, 8, tq), lambda h, i, j: (h, 0, i))],
        out_shape=[_sds((nl, N_HEADS * SLOT), MXU_DTYPE), _sds((N_HEADS, 8, nl), F32)],
        scratch_shapes=[pltpu.VMEM((tq, SLOT), F32), pltpu.VMEM((tq, SLOT), F32), pltpu.VMEM((tq, SLOT), F32)],
        compiler_params=_cparams(("parallel", "parallel", "arbitrary")),
    )(q, k, v)


def _attn_bwd(q, k, kt, v, do, o, lse_t, nl, scale):
    n = k.shape[0]
    tq = _pick(nl, (2048, 1024, 512, 256, 128))
    tk = _pick(n, (2816, 1408, 1152, 768, 384, 256, 128))
    sub = _pick(tk, (256, 128))
    nq, nk = nl // tq, n // tk
    c = scale * LOG2E

    def body(q_ref, k_ref, kt_ref, v_ref, do_ref, o_ref, lse_ref, dq_ref, dk_ref, dv_ref, dk_acc, dv_acc):
        ki, qi = pl.program_id(1), pl.program_id(2)

        @pl.when((ki == 0) & (qi == 0))
        def _():
            dq_ref[...] = jnp.zeros(dq_ref.shape, F32)

        @pl.when(qi == 0)
        def _():
            dk_acc[...] = jnp.zeros(dk_acc.shape, F32)
            dv_acc[...] = jnp.zeros(dv_acc.shape, F32)

        qb, dof = q_ref[...], do_ref[...]
        dob = dof.astype(MXU_DTYPE)
        lse_r = lse_ref[0:1, :]
        dl_r = jnp.sum(jnp.transpose(dof * o_ref[...].astype(F32)), axis=0, keepdims=True)
        dq_part = None
        for sb in range(tk // sub):
            rows = slice(sb * sub, (sb + 1) * sub)
            s_t = lax.dot_general(k_ref[rows, :], qb, (((1,), (1,)), ((), ())), preferred_element_type=F32)
            p_t = jnp.exp2(s_t * c - lse_r)
            dp_t = lax.dot_general(v_ref[rows, :], dob, (((1,), (1,)), ((), ())), preferred_element_type=F32)
            ds_t = (p_t * (dp_t - dl_r) * scale).astype(MXU_DTYPE)
            dv_acc[rows, :] += jnp.dot(p_t.astype(MXU_DTYPE), dob, preferred_element_type=F32)
            dk_acc[rows, :] += jnp.dot(ds_t, qb, preferred_element_type=F32)
            part = jnp.dot(kt_ref[:, rows], ds_t, preferred_element_type=F32)
            dq_part = part if dq_part is None else dq_part + part
        c0 = pl.multiple_of(qi * tq, tq)
        dq_ref[:, pl.ds(c0, tq)] += dq_part

        @pl.when(qi == nq - 1)
        def _():
            dk_ref[...] = dk_acc[...]
            dv_ref[...] = dv_acc[...]

    return pl.pallas_call(
        body, name="attn_bwd", grid=(N_HEADS, nk, nq),
        in_specs=[pl.BlockSpec((tq, SLOT), lambda h, j, i: (i, h)),
                  pl.BlockSpec((tk, SLOT), lambda h, j, i: (j, h)),
                  pl.BlockSpec((SLOT, tk), lambda h, j, i: (h, j)),
                  pl.BlockSpec((tk, SLOT), lambda h, j, i: (j, h)),
                  pl.BlockSpec((tq, SLOT), lambda h, j, i: (i, h)),
                  pl.BlockSpec((tq, SLOT), lambda h, j, i: (i, h)),
                  pl.BlockSpec((None, 8, tq), lambda h, j, i: (h, 0, i))],
        out_specs=[pl.BlockSpec((SLOT, nl), lambda h, j, i: (h, 0)),
                   pl.BlockSpec((tk, SLOT), lambda h, j, i: (j, h)),
                   pl.BlockSpec((tk, SLOT), lambda h, j, i: (j, h))],
        out_shape=[_sds((N_HEADS * SLOT, nl), F32), _sds((n, N_HEADS * SLOT), F32), _sds((n, N_HEADS * SLOT), F32)],
        scratch_shapes=[pltpu.VMEM((tk, SLOT), F32), pltpu.VMEM((tk, SLOT), F32)],
        compiler_params=_cparams(("arbitrary", "arbitrary", "arbitrary")),
    )(q, k, kt, v, do, o, lse_t)


def _scan_consts(c_ref, lg):
    cs = slice(lg * SCAN_LANES, (lg + 1) * SCAN_LANES)
    return [c_ref[8 * kk:8 * kk + 8, cs] for kk in range(8)]


def _tile_scan(br, bi, consts, reverse):
    p1r, p1i, p2r, p2i, p4r, p4i = consts[:6]
    for pr, pi, kk in ((p1r, p1i, 1), (p2r, p2i, 2), (p4r, p4i, 4)):
        sh = (8 - kk) if reverse else kk
        sr_, si_ = pltpu.roll(br, sh, 0), pltpu.roll(bi, sh, 0)
        br, bi = br + pr * sr_ - pi * si_, bi + pr * si_ + pi * sr_
    return br, bi


def _seq_chunk(j, nch, nlc, reverse):
    return (nch - 1 - j) if reverse else (j + nlc) % nch


def _s5_scan(z, u_cb, bbd, cbd_n, lamc, t_rows, nl, reverse, name):
    n = z.shape[0]
    nch, nlc = n // t_rows, nl // t_rows
    ntile = t_rows // 8
    w = SCAN_LANES
    edge = 0 if reverse else 7
    ucb = u_cb * (SSM_WIDTH // CG_CHANNELS)

    def chunk(j):
        return _seq_chunk(j, nch, nlc, reverse)

    def body(u_ref, b_ref, cn_ref, c_ref, xs_ref, y_ref, carry):
        j = pl.program_id(1)

        @pl.when(j == 0)
        def _():
            carry[...] = jnp.zeros(carry.shape, F32)

        xs_ref[...] = jnp.dot(u_ref[...].astype(MXU_DTYPE), b_ref[...], preferred_element_type=F32)
        for lg in range(CG_STATES // w):
            re = slice(lg * w, (lg + 1) * w)
            im = slice(CG_STATES + lg * w, CG_STATES + (lg + 1) * w)
            consts = _scan_consts(c_ref, lg)
            qr, qi = consts[6], consts[7]

            def tile(tt, st):
                cr, ci = st
                t = (ntile - 1 - tt) if reverse else tt
                r0 = pl.multiple_of(t * 8, 8)
                br, bi = _tile_scan(xs_ref[pl.ds(r0, 8), re], xs_ref[pl.ds(r0, 8), im], consts, reverse)
                lr = jnp.broadcast_to(cr[edge:edge + 1, :], br.shape)
                li = jnp.broadcast_to(ci[edge:edge + 1, :], bi.shape)
                xr = br + qr * lr - qi * li
                xi = bi + qr * li + qi * lr
                xs_ref[pl.ds(r0, 8), re] = xr
                xs_ref[pl.ds(r0, 8), im] = xi
                return xr, xi

            cr, ci = lax.fori_loop(0, ntile, tile, (carry[:, re], carry[:, im]))
            carry[:, re] = cr
            carry[:, im] = ci
        y_ref[...] = jnp.dot(xs_ref[...].astype(MXU_DTYPE), cn_ref[...], preferred_element_type=F32)

    cw = 2 * CG_STATES
    return pl.pallas_call(
        body, name=name, grid=(N_CG, nch),
        in_specs=[pl.BlockSpec((t_rows, CG_CHANNELS), lambda g, j: (chunk(j), ucb + g)),
                  pl.BlockSpec((CG_CHANNELS, cw), lambda g, j: (g, 0)),
                  pl.BlockSpec((cw, CG_CHANNELS), lambda g, j: (g, 0)),
                  pl.BlockSpec((64, CG_STATES), lambda g, j: (0, g))],
        out_specs=[pl.BlockSpec((t_rows, cw), lambda g, j: (chunk(j), g)),
                   pl.BlockSpec((t_rows, CG_CHANNELS), lambda g, j: (chunk(j), g))],
        out_shape=[_sds((n, 2 * N_STATE), F32), _sds((n, SSM_WIDTH), F32)],
        scratch_shapes=[pltpu.VMEM((8, cw), F32)],
        compiler_params=_cparams(("arbitrary", "arbitrary")),
    )(z, bbd, cbd_n, lamc)


def _s5_bwd(dys, z, u_cb, xs, cbd_t, bbd_t, lamc_adj, t_rows, nl, reverse, name):
    n = z.shape[0]
    nch, nlc = n // t_rows, nl // t_rows
    ntile = t_rows // 8
    t8 = t_rows // 8
    w = SCAN_LANES
    cw = 2 * CG_STATES
    adj_rev = not reverse
    edge = 0 if adj_rev else 7

    def chunk(j):
        return _seq_chunk(nch - 1 - j, nch, nlc, reverse)

    def halo_blk(j):
        if reverse:
            return jnp.minimum((chunk(j) + 1) * t8, n // 8 - 1)
        return (_seq_chunk(jnp.maximum(nch - 2 - j, 0), nch, nlc, False) + 1) * t8 - 1

    def body(dy_ref, u_ref, xs_ref, halo_ref, ct_ref, bt_ref, c_ref, du_ref, db_ref, dc_ref, dl_ref, gbuf, carry):
        j = pl.program_id(1)
        start = j == nch - 1

        @pl.when(j == 0)
        def _():
            carry[...] = jnp.zeros(carry.shape, F32)
            db_ref[...] = jnp.zeros(db_ref.shape, F32)
            dc_ref[...] = jnp.zeros(dc_ref.shape, F32)
            dl_ref[...] = jnp.zeros(dl_ref.shape, F32)

        dy = jnp.where(chunk(j) < nlc, dy_ref[...], 0.0).astype(MXU_DTYPE)
        gbuf[...] = jnp.dot(dy, ct_ref[...], preferred_element_type=F32)
        dc_ref[...] += lax.dot_general(dy, xs_ref[...].astype(MXU_DTYPE), (((0,), (0,)), ((), ())),
                                       preferred_element_type=F32)
        for lg in range(CG_STATES // w):
            re = slice(lg * w, (lg + 1) * w)
            im = slice(CG_STATES + lg * w, CG_STATES + (lg + 1) * w)
            consts = _scan_consts(c_ref, lg)
            qr, qi = consts[6], consts[7]
            hr, hi = halo_ref[:, re], halo_ref[:, im]

            def tile(tt, st):
                gcr, gci, ar, ai = st
                t = (ntile - 1 - tt) if adj_rev else tt
                r0 = pl.multiple_of(t * 8, 8)
                br, bi = _tile_scan(gbuf[pl.ds(r0, 8), re], gbuf[pl.ds(r0, 8), im], consts, adj_rev)
                lr = jnp.broadcast_to(gcr[edge:edge + 1, :], br.shape)
                li = jnp.broadcast_to(gci[edge:edge + 1, :], bi.shape)
                gr = br + qr * lr - qi * li
                gi = bi + qr * li + qi * lr
                gbuf[pl.ds(r0, 8), re] = gr
                gbuf[pl.ds(r0, 8), im] = gi
                xr, xi = xs_ref[pl.ds(r0, 8), re], xs_ref[pl.ds(r0, 8), im]
                rid = lax.broadcasted_iota(jnp.int32, xr.shape, 0)
                if reverse:
                    last = t == ntile - 1
                    rn = pl.multiple_of(jnp.minimum(r0 + 8, t_rows - 8), 8)
                    nbr = jnp.where(last, hr, xs_ref[pl.ds(rn, 8), re])
                    nbi = jnp.where(last, hi, xs_ref[pl.ds(rn, 8), im])
                    nbr = jnp.where(last & start, 0.0, nbr)
                    nbi = jnp.where(last & start, 0.0, nbi)
                    xpr = jnp.where(rid == 7, jnp.broadcast_to(nbr[0:1, :], xr.shape), pltpu.roll(xr, 7, 0))
                    xpi = jnp.where(rid == 7, jnp.broadcast_to(nbi[0:1, :], xi.shape), pltpu.roll(xi, 7, 0))
                else:
                    first = t == 0
                    rn = pl.multiple_of(jnp.maximum(r0 - 8, 0), 8)
                    nbr = jnp.where(first, hr, xs_ref[pl.ds(rn, 8), re])
                    nbi = jnp.where(first, hi, xs_ref[pl.ds(rn, 8), im])
                    nbr = jnp.where(first & start, 0.0, nbr)
                    nbi = jnp.where(first & start, 0.0, nbi)
                    xpr = jnp.where(rid == 0, jnp.broadcast_to(nbr[7:8, :], xr.shape), pltpu.roll(xr, 1, 0))
                    xpi = jnp.where(rid == 0, jnp.broadcast_to(nbi[7:8, :], xi.shape), pltpu.roll(xi, 1, 0))
                ar = ar + gr * xpr + gi * xpi
                ai = ai - gr * xpi + gi * xpr
                return gr, gi, ar, ai

            zz = jnp.zeros((8, w), F32)
            gcr, gci, ar, ai = lax.fori_loop(0, ntile, tile, (carry[:, re], carry[:, im], zz, zz))
            carry[:, re] = gcr
            carry[:, im] = gci
            dl_ref[:, re] += ar
            dl_ref[:, im] += ai
        g = gbuf[...].astype(MXU_DTYPE)
        du_ref[...] = jnp.dot(g, bt_ref[...], preferred_element_type=F32)
        db_ref[...] += lax.dot_general(u_ref[...].astype(MXU_DTYPE), g, (((0,), (0,)), ((), ())),
                                       preferred_element_type=F32)

    ucb = u_cb * (SSM_WIDTH // CG_CHANNELS)
    return pl.pallas_call(
        body, name=name, grid=(N_CG, nch),
        in_specs=[pl.BlockSpec((t_rows, CG_CHANNELS), lambda g, j: (jnp.minimum(chunk(j), nlc - 1), g)),
                  pl.BlockSpec((t_rows, CG_CHANNELS), lambda g, j: (chunk(j), ucb + g)),
                  pl.BlockSpec((t_rows, cw), lambda g, j: (chunk(j), g)),
                  pl.BlockSpec((8, cw), lambda g, j: (halo_blk(j), g)),
                  pl.BlockSpec((CG_CHANNELS, cw), lambda g, j: (g, 0)),
                  pl.BlockSpec((cw, CG_CHANNELS), lambda g, j: (g, 0)),
                  pl.BlockSpec((64, CG_STATES), lambda g, j: (0, g))],
        out_specs=[pl.BlockSpec((t_rows, CG_CHANNELS), lambda g, j: (chunk(j), g)),
                   pl.BlockSpec((CG_CHANNELS, cw), lambda g, j: (g, 0)),
                   pl.BlockSpec((CG_CHANNELS, cw), lambda g, j: (g, 0)),
                   pl.BlockSpec((8, cw), lambda g, j: (0, g))],
        out_shape=[_sds((n, SSM_WIDTH), F32), _sds((SSM_WIDTH, cw), F32), _sds((SSM_WIDTH, cw), F32),
                   _sds((8, 2 * N_STATE), F32)],
        scratch_shapes=[pltpu.VMEM((t_rows, cw), F32), pltpu.VMEM((8, cw), F32)],
        compiler_params=_cparams(("arbitrary", "arbitrary")),
    )(dys, z, xs, xs, cbd_t, bbd_t, lamc_adj)


_CG_GROUPS = SSM_GROUPS // N_CG


def _group_mask():
    idx = jnp.arange(_CG_GROUPS)
    return (idx[:, None] == idx[None, :])[None, :, None, None, :, None]


def _diag_blocks(p_re, p_im):
    t = jnp.stack([p_re, p_im], axis=2).reshape(N_CG, _CG_GROUPS, SSM_GROUP, 2, 1, SSM_STATE)
    return jnp.where(_group_mask(), t, 0.0).reshape(SSM_WIDTH, 2 * CG_STATES)


def _diag_extract(d):
    d6 = d.reshape(N_CG, _CG_GROUPS, SSM_GROUP, 2, _CG_GROUPS, SSM_STATE)
    blk = jnp.sum(jnp.where(_group_mask(), d6, 0.0), axis=4)
    blk = blk.reshape(SSM_GROUPS, SSM_GROUP, 2, SSM_STATE)
    return blk[:, :, 0], blk[:, :, 1]


def _block_transpose(d):
    return jnp.transpose(d.reshape(N_CG, CG_CHANNELS, 2 * CG_STATES), (0, 2, 1)).reshape(2 * N_STATE, CG_CHANNELS)


def _s5_disc(lam_re, lam_im, log_dt, b_re, b_im):
    lam = lax.complex(lam_re, lam_im)
    dt = jnp.exp(log_dt)[:, None]
    lam_bar = jnp.exp(lam * dt)
    b_bar = ((lam_bar - 1.0) / lam)[..., None] * lax.complex(b_re, b_im)
    return jnp.real(lam_bar), jnp.imag(lam_bar), jnp.real(b_bar), jnp.imag(b_bar)


def _lam_consts(lr, li, mirrored, conj):
    lam = lax.complex(lr.reshape(-1), -li.reshape(-1) if conj else li.reshape(-1))
    p2 = lam * lam
    p4 = p2 * p2
    pw = [lam, p2, p2 * lam, p4, p4 * lam, p4 * p2, p4 * p2 * lam, p4 * p4]
    rows = jnp.arange(8)[:, None]
    out = []
    for kk in (1, 2, 4):
        mask = (rows <= 7 - kk) if mirrored else (rows >= kk)
        pk = jnp.where(mask, pw[kk - 1][None, :], 0.0)
        out += [jnp.real(pk), jnp.imag(pk)]
    q = jnp.stack(pw[::-1] if mirrored else pw)
    return jnp.concatenate(out + [jnp.real(q), jnp.imag(q)], axis=0)


def _dev(t):
    return (t // 4, (t // 2) % 2, t % 2)


def _my_index():
    return 4 * lax.axis_index("x") + 2 * lax.axis_index("y") + lax.axis_index("c")


def _comm_call(body, name, arrs, lead, n_remote):
    nw = len(arrs)
    any_spec = pl.BlockSpec(memory_space=pl.ANY)
    return pl.pallas_call(
        body, name=name, out_shape=[_sds((lead,) + a.shape[-2:], a.dtype) for a in arrs],
        in_specs=[any_spec] * nw, out_specs=[any_spec] * nw,
        scratch_shapes=[pltpu.SemaphoreType.DMA((n_remote * nw,)), pltpu.SemaphoreType.DMA((n_remote * nw,)),
                        pltpu.SemaphoreType.DMA((2 * nw,))] + [pltpu.VMEM(a.shape[-2:], a.dtype) for a in arrs],
        compiler_params=pltpu.CompilerParams(vmem_limit_bytes=VMEM_LIMIT),
    )(*arrs)


class _LocalCopy:
    def __init__(self, src, dst, buf, sem_in, sem_out):
        self.fetch = pltpu.make_async_copy(src, buf, sem_in)
        self.store = pltpu.make_async_copy(buf, dst, sem_out)
        self.fetch.start()

    def forward(self):
        self.fetch.wait()
        self.store.start()

    def finish(self):
        self.store.wait()


def _exchange8(gs, name, same):
    nw = len(gs)

    def body(*refs):
        g_refs, o_refs, (ssem, rsem, lsem), bufs = refs[:nw], refs[nw:2 * nw], refs[2 * nw:2 * nw + 3], refs[2 * nw + 3:]
        me = _my_index()
        locs, sends = [], []
        for i, (g_ref, o_ref) in enumerate(zip(g_refs, o_refs)):
            src = (lambda t, g_ref=g_ref: g_ref) if same else (lambda t, g_ref=g_ref: g_ref.at[t])
            locs.append(_LocalCopy(src(me), o_ref.at[me], bufs[i], lsem.at[2 * i], lsem.at[2 * i + 1]))
            for d in range(1, 8):
                t = (me + d) % 8
                cp = pltpu.make_async_remote_copy(src_ref=src(t), dst_ref=o_ref.at[me], send_sem=ssem.at[7 * i + d - 1],
                                                  recv_sem=rsem.at[7 * i + d - 1], device_id=_dev(t), device_id_type=MESH)
                cp.start()
                sends.append(cp)
        for loc in locs:
            loc.forward()
        for i, (g_ref, o_ref) in enumerate(zip(g_refs, o_refs)):
            src = (lambda t, g_ref=g_ref: g_ref) if same else (lambda t, g_ref=g_ref: g_ref.at[t])
            for d in range(1, 8):
                s = (me + 8 - d) % 8
                pltpu.make_async_remote_copy(src_ref=src(s), dst_ref=o_ref.at[s], send_sem=ssem.at[7 * i + d - 1],
                                             recv_sem=rsem.at[7 * i + d - 1], device_id=_dev(s),
                                             device_id_type=MESH).wait_recv()
        for cp in sends:
            cp.wait_send()
        for loc in locs:
            loc.finish()

    return _comm_call(body, name, gs, 8, 7)


def _exchange_chips(ws, name, gather):
    nw = len(ws)

    def body(*refs):
        w_refs, o_refs, (ssem, rsem, lsem), bufs = refs[:nw], refs[nw:2 * nw], refs[2 * nw:2 * nw + 3], refs[2 * nw + 3:]
        x, y, cc = lax.axis_index("x"), lax.axis_index("y"), lax.axis_index("c")
        k = 2 * x + y
        peers = [(1 - x, y), (x, 1 - y), (1 - x, 1 - y)]
        locs, sends = [], []
        for i, (w_ref, o_ref) in enumerate(zip(w_refs, o_refs)):
            if gather:
                locs.append(_LocalCopy(w_ref.at[cc], o_ref.at[k], bufs[i], lsem.at[2 * i], lsem.at[2 * i + 1]))
            for j, (px, py) in enumerate(peers):
                src, dst = (w_ref.at[cc], o_ref.at[k]) if gather else (w_ref.at[2 * px + py], o_ref.at[j])
                cp = pltpu.make_async_remote_copy(src_ref=src, dst_ref=dst, send_sem=ssem.at[3 * i + j],
                                                  recv_sem=rsem.at[3 * i + j], device_id=(px, py, cc), device_id_type=MESH)
                cp.start()
                sends.append(cp)
        for loc in locs:
            loc.forward()
        for i, (w_ref, o_ref) in enumerate(zip(w_refs, o_refs)):
            for j, (px, py) in enumerate(peers):
                src, dst = (w_ref.at[cc], o_ref.at[2 * px + py]) if gather else (w_ref.at[k], o_ref.at[j])
                pltpu.make_async_remote_copy(src_ref=src, dst_ref=dst, send_sem=ssem.at[3 * i + j],
                                             recv_sem=rsem.at[3 * i + j], device_id=(px, py, cc),
                                             device_id_type=MESH).wait_recv()
        for cp in sends:
            cp.wait_send()
        for loc in locs:
            loc.finish()

    return _comm_call(body, name, ws, 4 if gather else 3, 3)


def _sibling_send(hs, name):
    nw = len(hs)

    def body(*refs):
        h_refs, o_refs, (ssem, rsem, lsem) = refs[:nw], refs[nw:2 * nw], refs[2 * nw:]
        x, y, cc = lax.axis_index("x"), lax.axis_index("y"), lax.axis_index("c")
        sends = []
        for i, (h_ref, o_ref) in enumerate(zip(h_refs, o_refs)):
            cp = pltpu.make_async_remote_copy(src_ref=h_ref.at[1 - cc], dst_ref=o_ref, send_sem=ssem.at[i],
                                              recv_sem=rsem.at[i], device_id=(x, y, 1 - cc), device_id_type=MESH)
            cp.start()
            sends.append(cp)
        for i, (h_ref, o_ref) in enumerate(zip(h_refs, o_refs)):
            pltpu.make_async_remote_copy(src_ref=h_ref.at[cc], dst_ref=o_ref, send_sem=ssem.at[i], recv_sem=rsem.at[i],
                                         device_id=(x, y, 1 - cc), device_id_type=MESH).wait_recv()
        for cp in sends:
            cp.wait_send()

    nw_spec = pl.BlockSpec(memory_space=pl.ANY)
    return pl.pallas_call(
        body, name=name, out_shape=[_sds(h.shape[1:], h.dtype) for h in hs],
        in_specs=[nw_spec] * nw, out_specs=[nw_spec] * nw,
        scratch_shapes=[pltpu.SemaphoreType.DMA((nw,)), pltpu.SemaphoreType.DMA((nw,)), pltpu.SemaphoreType.DMA((nw,))],
    )(*hs)


def _sibling_exchange(hs, name):
    nw = len(hs)

    def body(*refs):
        h_refs, o_refs, (ssem, rsem, lsem), bufs = refs[:nw], refs[nw:2 * nw], refs[2 * nw:2 * nw + 3], refs[2 * nw + 3:]
        x, y, cc = lax.axis_index("x"), lax.axis_index("y"), lax.axis_index("c")
        locs, sends = [], []
        for i, (h_ref, o_ref) in enumerate(zip(h_refs, o_refs)):
            locs.append(_LocalCopy(h_ref, o_ref.at[cc], bufs[i], lsem.at[2 * i], lsem.at[2 * i + 1]))
            cp = pltpu.make_async_remote_copy(src_ref=h_ref, dst_ref=o_ref.at[cc], send_sem=ssem.at[i], recv_sem=rsem.at[i],
                                              device_id=(x, y, 1 - cc), device_id_type=MESH)
            cp.start()
            sends.append(cp)
        for loc in locs:
            loc.forward()
        for i, (h_ref, o_ref) in enumerate(zip(h_refs, o_refs)):
            pltpu.make_async_remote_copy(src_ref=h_ref, dst_ref=o_ref.at[1 - cc], send_sem=ssem.at[i], recv_sem=rsem.at[i],
                                         device_id=(x, y, 1 - cc), device_id_type=MESH).wait_recv()
        for cp in sends:
            cp.wait_send()
        for loc in locs:
            loc.finish()

    return _comm_call(body, name, hs, 2, 1)


def _sum8(buf, name):
    _, r, c = buf.shape
    tr = _pick(r, (256, 128, 64, 32, 16, 8))
    flat = buf.reshape(8 * r, c)

    def body(*v):
        acc = v[0]
        for t in v[1:]:
            acc = acc + t
        return acc

    return _rowwise(body, name=name, nblk=r // tr, tr=tr, rows=[(flat, 0, c, s * r) for s in range(8)],
                    outs=[(c, F32)])[0]


def _pack(arrs, rows_mult=16):
    flat = jnp.concatenate([a.reshape(-1).astype(F32) for a in arrs])
    nel = flat.shape[0]
    r = -(-nel // PACK_W)
    r = -(-r // rows_mult) * rows_mult
    return jnp.pad(flat, (0, r * PACK_W - nel)).reshape(r, PACK_W)


def _unpack(buf, shapes):
    flat = buf.reshape(-1)
    out, o = [], 0
    for s in shapes:
        nel = int(np.prod(s))
        out.append(flat[o:o + nel].reshape(s))
        o += nel
    return out


def _adamw(g, w, m, v, name):
    r, wd = g.shape
    tr = _pick(r, tuple(t for t in (256, 128, 64, 32, 16, 8) if t * wd <= 262144) or (8,))
    c1 = 1.0 / (1.0 - ADAM_B1 ** ADAM_STEP)
    c2 = 1.0 / (1.0 - ADAM_B2 ** ADAM_STEP)

    def body(gv, wv, mv, vv):
        mn = ADAM_B1 * mv + (1.0 - ADAM_B1) * gv
        vn = ADAM_B2 * vv + (1.0 - ADAM_B2) * (gv * gv)
        delta = -ADAM_LR * ((mn * c1) / (jnp.sqrt(vn * c2) + ADAM_EPS) + ADAM_WD * wv)
        return delta, mn, vn

    return _rowwise(body, name=name, nblk=r // tr, tr=tr, rows=[(a, 0, wd, 0) for a in (g, w, m, v)],
                    outs=[(wd, F32)] * 3)


def kernel(x, c, ctx, c_ctx, w_mod, b_mod, norm1_g, norm2_g, w_in, q_a_g, w_uq, kv_a_g, w_ukv, q_norm_g, k_norm_g, w_o_attn, lam_re_f, lam_im_f, log_dt_f, c_re_f, c_im_f, lam_re_b, lam_im_b, log_dt_b, c_re_b, c_im_b, b_re, b_im, d_skip, w_glu, w_out, w_up, conv_w, conv_b, w_down, loss_target, m_c_ctx, m_w_mod, m_b_mod, m_norm1_g, m_norm2_g, m_w_in, m_q_a_g, m_w_uq, m_kv_a_g, m_w_ukv, m_q_norm_g, m_k_norm_g, m_w_o_attn, m_lam_re_f, m_lam_im_f, m_log_dt_f, m_c_re_f, m_c_im_f, m_lam_re_b, m_lam_im_b, m_log_dt_b, m_c_re_b, m_c_im_b, m_b_re, m_b_im, m_d_skip, m_w_glu, m_w_out, m_w_up, m_conv_w, m_conv_b, m_w_down, v_c_ctx, v_w_mod, v_b_mod, v_norm1_g, v_norm2_g, v_w_in, v_q_a_g, v_w_uq, v_kv_a_g, v_w_ukv, v_q_norm_g, v_k_norm_g, v_w_o_attn, v_lam_re_f, v_lam_im_f, v_log_dt_f, v_c_re_f, v_c_im_f, v_lam_re_b, v_lam_im_b, v_log_dt_b, v_c_re_b, v_c_im_b, v_b_re, v_b_im, v_d_skip, v_w_glu, v_w_out, v_w_up, v_conv_w, v_conv_b, v_w_down):
    weights = dict(c_ctx=c_ctx, w_mod=w_mod, b_mod=b_mod, norm1_g=norm1_g, norm2_g=norm2_g, w_in=w_in, q_a_g=q_a_g, w_uq=w_uq, kv_a_g=kv_a_g, w_ukv=w_ukv, q_norm_g=q_norm_g, k_norm_g=k_norm_g, w_o_attn=w_o_attn, lam_re_f=lam_re_f, lam_im_f=lam_im_f, log_dt_f=log_dt_f, c_re_f=c_re_f, c_im_f=c_im_f, lam_re_b=lam_re_b, lam_im_b=lam_im_b, log_dt_b=log_dt_b, c_re_b=c_re_b, c_im_b=c_im_b, b_re=b_re, b_im=b_im, d_skip=d_skip, w_glu=w_glu, w_out=w_out, w_up=w_up, conv_w=conv_w, conv_b=conv_b, w_down=w_down)
    mom_m = dict(c_ctx=m_c_ctx, w_mod=m_w_mod, b_mod=m_b_mod, norm1_g=m_norm1_g, norm2_g=m_norm2_g, w_in=m_w_in, q_a_g=m_q_a_g, w_uq=m_w_uq, kv_a_g=m_kv_a_g, w_ukv=m_w_ukv, q_norm_g=m_q_norm_g, k_norm_g=m_k_norm_g, w_o_attn=m_w_o_attn, lam_re_f=m_lam_re_f, lam_im_f=m_lam_im_f, log_dt_f=m_log_dt_f, c_re_f=m_c_re_f, c_im_f=m_c_im_f, lam_re_b=m_lam_re_b, lam_im_b=m_lam_im_b, log_dt_b=m_log_dt_b, c_re_b=m_c_re_b, c_im_b=m_c_im_b, b_re=m_b_re, b_im=m_b_im, d_skip=m_d_skip, w_glu=m_w_glu, w_out=m_w_out, w_up=m_w_up, conv_w=m_conv_w, conv_b=m_conv_b, w_down=m_w_down)
    mom_v = dict(c_ctx=v_c_ctx, w_mod=v_w_mod, b_mod=v_b_mod, norm1_g=v_norm1_g, norm2_g=v_norm2_g, w_in=v_w_in, q_a_g=v_q_a_g, w_uq=v_w_uq, kv_a_g=v_kv_a_g, w_ukv=v_w_ukv, q_norm_g=v_q_norm_g, k_norm_g=v_k_norm_g, w_o_attn=v_w_o_attn, lam_re_f=v_lam_re_f, lam_im_f=v_lam_im_f, log_dt_f=v_log_dt_f, c_re_f=v_c_re_f, c_im_f=v_c_im_f, lam_re_b=v_lam_re_b, lam_im_b=v_lam_im_b, log_dt_b=v_log_dt_b, c_re_b=v_c_re_b, c_im_b=v_c_im_b, b_re=v_b_re, b_im=v_b_im, d_skip=v_d_skip, w_glu=v_w_glu, w_out=v_w_out, w_up=v_w_up, conv_w=v_conv_w, conv_b=v_conv_b, w_down=v_w_down)
    names = list(weights)

    nl, d = x.shape[1], x.shape[2]
    nc = ctx.shape[1]
    n = nl + nc
    f2 = conv_b.shape[1]
    fh = f2 // 2
    d6 = b_mod.shape[1]
    mx, my, mc = lax.axis_index("x"), lax.axis_index("y"), lax.axis_index("c")
    chip = 2 * mx + my
    me = 4 * mx + 2 * my + mc
    tr = _pick(math.gcd(nl, nc), (256, 128, 64, 32, 16))
    nlb, nb = nl // tr, n // tr

    big_names = ["w_in", "w_uq", "w_ukv", "w_o_attn", "w_glu", "w_out", "w_up", "w_down"]
    row_sharded = ("w_out", "w_down")
    halves_in = [weights[k][0].astype(MXU_DTYPE).reshape(2, weights[k].shape[1] // 2, weights[k].shape[2]) for k in big_names]
    my_halves = _exchange_chips(halves_in, "gather_weights", True)
    gathered = _sibling_exchange([t.reshape(-1, t.shape[2]) for t in my_halves], "gather_weight_halves")
    full = {}
    for k_, gth in zip(big_names, gathered):
        r_, c_ = weights[k_].shape[1:]
        g4 = gth.reshape(2, 4, r_ // 2, c_)
        full[k_] = (jnp.transpose(g4, (1, 0, 2, 3)).reshape(4 * r_, c_) if k_ in row_sharded
                    else jnp.transpose(g4, (0, 2, 1, 3)).reshape(r_, 4 * c_))

    cwid = conv_w.shape[2]
    sw = -(-max(d, cwid) // 128) * 128
    small_in = jnp.concatenate([jnp.pad(c, ((0, 0), (0, sw - d))), jnp.pad(conv_w[0], ((0, 4), (0, sw - cwid)))], axis=0)
    (small_all,) = _exchange8([small_in], "gather_c", True)
    cs = small_all[:, 0, :d]
    conv_w_full = jnp.concatenate([small_all[2 * j, 1:4, :cwid] for j in range(4)], axis=1)
    cs16 = jnp.concatenate([cs, c_ctx[None, :], jnp.zeros((7, d), F32)], axis=0)

    csh = w_mod.shape[2]
    b_mod_sh = lax.dynamic_slice(b_mod, (0, chip * csh), (1, csh))

    def mod_fwd_body(c_ref, w_ref, b_ref, o_ref):
        a = _silu(c_ref[...]).astype(MXU_DTYPE)
        o_ref[...] = jnp.dot(a, w_ref[...].astype(MXU_DTYPE), preferred_element_type=F32) + b_ref[...]

    mod_sh = pl.pallas_call(mod_fwd_body, name="mod_fwd", out_shape=_sds((16, csh), F32),
                            compiler_params=pltpu.CompilerParams(vmem_limit_bytes=VMEM_LIMIT))(cs16, w_mod[0], b_mod_sh)
    (mod_all,) = _exchange8([mod_sh], "gather_mod", True)
    mod_full = jnp.concatenate([mod_all[2 * j] for j in range(4)], axis=1)
    modv = jnp.stack([lax.dynamic_slice(mod_full, (me, 0), (1, d6)), mod_full[8:9]])

    def mod_parts(m):
        return [m[:, j * d:(j + 1) * d] for j in range(6)]

    u_off, kv_off, kr_off = 2 * d, 2 * d + SSM_WIDTH, 2 * d + SSM_WIDTH + KV_LORA
    q_off = -(-(kr_off + SLOT) // Q_LORA) * Q_LORA
    zw = q_off + Q_LORA
    wi = full["w_in"]
    s0, s1, s2, s3 = Q_LORA, Q_LORA + KV_LORA, Q_LORA + KV_LORA + QK_ROPE, Q_LORA + KV_LORA + QK_ROPE + SSM_WIDTH
    zpad = lambda w_: jnp.zeros((d, w_), MXU_DTYPE)
    win_p = jnp.concatenate([wi[:, s3:], wi[:, s2:s3], wi[:, s0:s1], wi[:, s1:s2], zpad(SLOT - QK_ROPE),
                             zpad(q_off - kr_off - SLOT), wi[:, :s0]], axis=1)
    wuq_p = jnp.pad(full["w_uq"].reshape(Q_LORA, N_HEADS, QK_DIM), ((0, 0), (0, 0), (0, SLOT - QK_DIM))).reshape(Q_LORA, N_HEADS * SLOT)
    wukv3 = full["w_ukv"].reshape(KV_LORA, N_HEADS, QK_NOPE + V_DIM)
    padh = lambda t: jnp.pad(t, ((0, 0), (0, 0), (0, SLOT - t.shape[2]))).reshape(t.shape[0], N_HEADS * SLOT)
    wukv_p = jnp.concatenate([padh(wukv3[:, :, :QK_NOPE]), padh(wukv3[:, :, QK_NOPE:])], axis=1)
    wo_p = jnp.pad(full["w_o_attn"].reshape(N_HEADS, V_DIM, d), ((0, 0), (0, SLOT - V_DIM), (0, 0))).reshape(N_HEADS * SLOT, d)
    wglu, wout, wup, wdown = full["w_glu"], full["w_out"], full["w_up"], full["w_down"]
    hw = N_HEADS * SLOT
    gain_p = lambda g_: jnp.tile(jnp.pad(g_[0], (0, SLOT - QK_DIM)), N_HEADS)[None, :]
    qg_p, kg_p = gain_p(q_norm_g), gain_p(k_norm_g)

    tok = jnp.arange(nl)
    freqs = ROPE_THETA ** (-jnp.arange(QK_ROPE // 4, dtype=F32) / (QK_ROPE // 4))
    ang = jnp.concatenate([(tok // GRID_W)[:, None] * freqs, (tok % GRID_W)[:, None] * freqs], axis=-1)
    cos_t = jnp.concatenate([jnp.cos(ang), jnp.ones((nc, 16), F32)], axis=0)
    sin_t = jnp.concatenate([jnp.sin(ang), jnp.zeros((nc, 16), F32)], axis=0)
    zl = lambda w_: jnp.zeros((n, w_), F32)
    rope_c = jnp.concatenate([jnp.ones((n, QK_NOPE), F32), cos_t, cos_t, zl(SLOT - QK_DIM)], axis=1)
    rope_sa = jnp.concatenate([zl(QK_NOPE), -sin_t, zl(SLOT - QK_NOPE - 16)], axis=1)
    rope_sb = jnp.concatenate([zl(QK_NOPE + 16), sin_t, zl(SLOT - QK_DIM)], axis=1)

    dirs = (("f", lam_re_f, lam_im_f, log_dt_f, c_re_f, c_im_f, False), ("b", lam_re_b, lam_im_b, log_dt_b, c_re_b, c_im_b, True))
    bbd, cbd_t, cbd_n, bbd_t, lamc, lamc_adj, disc_vjps = [], [], [], [], [], [], []
    for _, l_re, l_im, l_dt, cr_, ci_, rev_ in dirs:
        (lbr, lbi, bbr, bbi), vjp = jax.vjp(_s5_disc, l_re[0], l_im[0], l_dt[0], b_re[0], b_im[0])
        disc_vjps.append(vjp)
        bb = _diag_blocks(jnp.transpose(bbr, (0, 2, 1)), jnp.transpose(bbi, (0, 2, 1))).astype(MXU_DTYPE)
        cc_ = _diag_blocks(cr_[0], -ci_[0]).astype(MXU_DTYPE)
        bbd.append(bb)
        bbd_t.append(_block_transpose(bb))
        cbd_t.append(cc_)
        cbd_n.append(_block_transpose(cc_))
        lamc.append(_lam_consts(lbr, lbi, rev_, False))
        lamc_adj.append(_lam_consts(lbr, lbi, not rev_, True))
    t_scan = tr

    xa = jnp.concatenate([x[0], ctx[0]], axis=0)
    n1g, n2g = norm1_g, norm2_g

    def norm1_body(xv, m, g):
        sh1, sc1 = m[:, :d], m[:, d:2 * d]
        return _rms_fwd(xv, g, d) * (1.0 + sc1) + sh1

    (h1,) = _rowwise(norm1_body, name="norm1_fwd", nblk=nb, tr=tr, rows=[(xa, 0, d, 0)], sels=[modv], fulls=[n1g],
                     outs=[(d, MXU_DTYPE)], seg=nlb)
    z = _mm(h1, win_p, "nn", "in_proj")
    gl_cb, u_cb, kv_cb, kr_cb, q_cb = 0, u_off // SSM_WIDTH, kv_off // KV_LORA, kr_off // SLOT, q_off // Q_LORA

    (cqn,) = _rowwise(lambda v, g: _rms_fwd(v, g, Q_LORA), name="qa_norm_fwd", nblk=nlb, tr=tr,
                      rows=[(z, q_cb, Q_LORA, 0)], fulls=[q_a_g], outs=[(Q_LORA, MXU_DTYPE)])
    qh = _mm(cqn, wuq_p, "nn", "q_up")

    def qhead_body(qv, cv, sav, sbv, g):
        return jnp.concatenate([_rope_fwd(_rms_fwd(t, g[:, :SLOT], QK_DIM), cv, sav, sbv) for t in _heads(qv)], axis=1)

    rope_rows = lambda: [(rope_c, 0, SLOT, 0), (rope_sa, 0, SLOT, 0), (rope_sb, 0, SLOT, 0)]
    (q_p,) = _rowwise(qhead_body, name="q_head_fwd", nblk=nlb, tr=tr, rows=[(qh, 0, hw, 0)] + rope_rows(),
                      fulls=[qg_p], outs=[(hw, MXU_DTYPE)])

    (ckvn,) = _rowwise(lambda v, g: _rms_fwd(v, g, KV_LORA), name="kva_norm_fwd", nblk=nb, tr=tr,
                       rows=[(z, kv_cb, KV_LORA, 0)], fulls=[kv_a_g], outs=[(KV_LORA, MXU_DTYPE)])
    kvpre = _mm(ckvn, wukv_p, "nn", "kv_up")

    def khead_body(kv_, vv_, krv, cv, sav, sbv, g):
        kpe = pltpu.roll(krv, QK_NOPE, 1)
        ks = [_rope_fwd(_rms_fwd(t + kpe, g[:, :SLOT], QK_DIM), cv, sav, sbv) for t in _heads(kv_)]
        return jnp.concatenate(ks, axis=1), vv_

    k_p, v_p = _rowwise(khead_body, name="k_head_fwd", nblk=nb, tr=tr,
                        rows=[(kvpre, 0, hw, 0), (kvpre, 1, hw, 0), (z, kr_cb, SLOT, 0)] + rope_rows(),
                        fulls=[kg_p], outs=[(hw, MXU_DTYPE), (hw, MXU_DTYPE)])

    scale = QK_DIM ** -0.5
    o_p, lse = _attn_fwd(q_p, k_p, v_p, nl, scale)
    a_l = _mm(o_p, wo_p, "nn", "attn_out")

    scans = [_s5_scan(z, u_cb, bbd[j], cbd_n[j], lamc[j], t_scan, nl, dirs[j][6], "s5_scan_" + dirs[j][0]) for j in range(2)]
    xs, ydir = [s_[0] for s_ in scans], [s_[1] for s_ in scans]

    def ssm_out_body(uv, a, b, dsk):
        ys = uv * dsk + a + b
        return ys, _gelu(ys)

    ys, ge = _rowwise(ssm_out_body, name="s5_out_fwd", nblk=nlb, tr=tr,
                      rows=[(z, u_cb, SSM_WIDTH, 0), (ydir[0], 0, SSM_WIDTH, 0), (ydir[1], 0, SSM_WIDTH, 0)],
                      fulls=[d_skip], outs=[(SSM_WIDTH, F32), (SSM_WIDTH, MXU_DTYPE)])
    glu_out = _mm(ge, wglu, "nn", "glu_proj")

    def merge_body(ga, gs, av, val, gate):
        return _sigmoid(ga) * av + _sigmoid(gs) * (val * _sigmoid(gate))

    merge_rows = lambda: [(z, 0, d, 0), (z, 1, d, 0), (a_l, 0, d, 0), (glu_out, 0, d, 0), (glu_out, 1, d, 0)]
    (merged,) = _rowwise(merge_body, name="merge_fwd", nblk=nlb, tr=tr, rows=merge_rows(), outs=[(d, MXU_DTYPE)])
    mo = _mm(merged, wout, "nn", "out_proj")
    mod_x = modv[0]

    def norm2_body(xv, mov, m, g):
        g1, sh2, sc2 = m[:, 2 * d:3 * d], m[:, 3 * d:4 * d], m[:, 4 * d:5 * d]
        x1v = xv + g1 * mov
        return x1v, _rms_fwd(x1v, g, d) * (1.0 + sc2) + sh2

    x1, h2 = _rowwise(norm2_body, name="norm2_fwd", nblk=nlb, tr=tr, rows=[(xa, 0, d, 0), (mo, 0, d, 0)],
                      fulls=[mod_x, n2g], outs=[(d, F32), (d, MXU_DTYPE)])
    up = _mm(h2, wup, "nn", "ffn_up")
    cw8 = jnp.zeros((8, f2), F32).at[:3].set(conv_w_full)

    def conv3(t3, w8, off):
        p_, c_, n_ = t3
        return p_ * w8[0:1, off:off + fh] + c_ * w8[1:2, off:off + fh] + n_ * w8[2:3, off:off + fh]

    def conv_fwd_body(val3, gate3, w8, bias):
        val2 = conv3(val3, w8, 0) + bias[:, :fh]
        gate2 = conv3(gate3, w8, fh) + bias[:, fh:]
        return _silu(gate2) * val2

    (act,) = _rowwise(conv_fwd_body, name="conv_fwd", nblk=nlb, tr=tr, rows=[(up, 0, fh, 0), (up, 1, fh, 0)],
                      halo=(0, 1), fulls=[cw8, conv_b], outs=[(fh, MXU_DTYPE)])
    dn = _mm(act, wdown, "nn", "ffn_down")
    tgt = loss_target[0]

    def loss_body(x1v, dnv, tv, m):
        g2 = m[:, 5 * d:6 * d]
        e = x1v + g2 * dnv - tv
        dx2v = e * (1.0 / d)
        return dx2v, dx2v * g2, e * e, dx2v * dnv

    dx2, ddn, loss_acc, dg2_acc = _rowwise(loss_body, name="loss", nblk=nlb, tr=tr,
                                           rows=[(x1, 0, d, 0), (dn, 0, d, 0), (tgt, 0, d, 0)], fulls=[mod_x],
                                           outs=[(d, F32), (d, MXU_DTYPE)], accs=[d, d])
    loss = lax.psum(0.5 / d * jnp.sum(loss_acc), ("x", "y", "c"))

    g_big = {}
    dact = _mm(ddn, wdown, "nt", "ffn_down_dx")
    g_big["w_down"] = _mm(act, ddn, "tn", "ffn_down_dw")

    def conv_bwd_body(val3, gate3, da, w8, bias):
        val2 = conv3(val3, w8, 0) + bias[:, :fh]
        gate2 = conv3(gate3, w8, fh) + bias[:, fh:]
        sg = _sigmoid(gate2)
        dval2 = da * (gate2 * sg)
        dgate2 = da * val2 * (sg * (1.0 + gate2 * (1.0 - sg)))
        du2 = jnp.concatenate([dval2, dgate2], axis=1)
        taps = [jnp.concatenate([dval2 * val3[j], dgate2 * gate3[j]], axis=1) for j in range(3)]
        return du2, du2, taps[0], taps[1], taps[2]

    du2, dcb_acc, dcw0, dcw1, dcw2 = _rowwise(conv_bwd_body, name="conv_bwd", nblk=nlb, tr=tr,
                                              rows=[(up, 0, fh, 0), (up, 1, fh, 0), (dact, 0, fh, 0)], halo=(0, 1),
                                              fulls=[cw8, conv_b], outs=[(f2, F32)], accs=[f2, f2, f2, f2])

    def conv_t_body(dval3, dgate3, w8):
        rev = lambda t3: (t3[2], t3[1], t3[0])
        return jnp.concatenate([conv3(rev(dval3), w8, 0), conv3(rev(dgate3), w8, fh)], axis=1)

    (dup,) = _rowwise(conv_t_body, name="conv_bwd_dx", nblk=nlb, tr=tr, rows=[(du2, 0, fh, 0), (du2, 1, fh, 0)],
                      halo=(0, 1), fulls=[cw8], outs=[(f2, MXU_DTYPE)])
    dh2 = _mm(dup, wup, "nt", "ffn_up_dx")
    g_big["w_up"] = _mm(h2, dup, "tn", "ffn_up_dw")

    def norm2_bwd_body(x1v, dh, dx2v, mov, m, g):
        g1, sc2 = m[:, 2 * d:3 * d], m[:, 4 * d:5 * d]
        y = _rms_fwd(x1v, g, d)
        dxn, dgc = _rms_bwd(x1v, g, dh * (1.0 + sc2), d)
        dx1v = dx2v + dxn
        return dx1v, dx1v * g1, dgc, dh, dh * y, dx1v * mov

    dx1, dmo, dn2g_acc, dsh2_acc, dsc2_acc, dg1_acc = _rowwise(
        norm2_bwd_body, name="norm2_bwd", nblk=nlb, tr=tr,
        rows=[(x1, 0, d, 0), (dh2, 0, d, 0), (dx2, 0, d, 0), (mo, 0, d, 0)], fulls=[mod_x, n2g],
        outs=[(d, F32), (d, MXU_DTYPE)], accs=[d, d, d, d])
    dmerged = _mm(dmo, wout, "nt", "out_proj_dx")
    g_big["w_out"] = _mm(merged, dmo, "tn", "out_proj_dw")

    def merge_bwd_body(ga, gs, av, val, gate, dm):
        sa_, ss_, sg_ = _sigmoid(ga), _sigmoid(gs), _sigmoid(gate)
        s_l = val * sg_
        ds_l = dm * ss_
        dga = dm * av * sa_ * (1.0 - sa_)
        dgs = dm * s_l * ss_ * (1.0 - ss_)
        dval = ds_l * sg_
        dgate = ds_l * val * sg_ * (1.0 - sg_)
        return dm * sa_, jnp.concatenate([dval, dgate], axis=1), jnp.concatenate([dga, dgs], axis=1)

    da_l, dglu, dgl = _rowwise(merge_bwd_body, name="merge_bwd", nblk=nlb, tr=tr,
                               rows=merge_rows() + [(dmerged, 0, d, 0)],
                               outs=[(d, MXU_DTYPE), (2 * d, MXU_DTYPE), (2 * d, MXU_DTYPE)])
    dge = _mm(dglu, wglu, "nt", "glu_proj_dx")
    g_big["w_glu"] = _mm(ge, dglu, "tn", "glu_proj_dw")

    def ssm_out_bwd_body(ysv, dgev, uv, dsk):
        dys_ = dgev * _dgelu(ysv)
        return dys_, dys_ * dsk, dys_ * uv

    dys, du_skip, ddskip_acc = _rowwise(ssm_out_bwd_body, name="s5_out_bwd", nblk=nlb, tr=tr,
                                        rows=[(ys, 0, SSM_WIDTH, 0), (dge, 0, SSM_WIDTH, 0), (z, u_cb, SSM_WIDTH, 0)],
                                        fulls=[d_skip], outs=[(SSM_WIDTH, F32), (SSM_WIDTH, F32)], accs=[SSM_WIDTH])
    s5b = [_s5_bwd(dys, z, u_cb, xs[j], cbd_t[j], bbd_t[j], lamc_adj[j], t_scan, nl, dirs[j][6], "s5_bwd_" + dirs[j][0])
           for j in range(2)]
    du_nat = s5b[0][0] + s5b[1][0] + jnp.concatenate([du_skip, jnp.zeros((nc, SSM_WIDTH), F32)], axis=0)

    do_f = _mm(da_l, wo_p, "nt", "attn_out_dx")
    g_wo_p = _mm(o_p, da_l, "tn", "attn_out_dw")

    dq_t, dk_p, dv_p = _attn_bwd(q_p, k_p, jnp.transpose(k_p), v_p, do_f, o_p, lse, nl, scale)
    dq_p = jnp.transpose(dq_t)

    def qhead_bwd_body(qv, dqv, cv, sav, sbv, g):
        dxs, dgs = [], []
        for t, dt_ in zip(_heads(qv), _heads(dqv)):
            dx_, dg_ = _rms_bwd(t, g[:, :SLOT], _rope_bwd(dt_, cv, sav, sbv), QK_DIM)
            dxs.append(dx_)
            dgs.append(dg_)
        return jnp.concatenate(dxs, axis=1), jnp.concatenate(dgs, axis=1)

    dqh, dqg_acc = _rowwise(qhead_bwd_body, name="q_head_bwd", nblk=nlb, tr=tr,
                            rows=[(qh, 0, hw, 0), (dq_p, 0, hw, 0)] + rope_rows(), fulls=[qg_p],
                            outs=[(hw, MXU_DTYPE)], accs=[hw])
    dcqn = _mm(dqh, wuq_p, "nt", "q_up_dx")
    g_wuq_p = _mm(cqn, dqh, "tn", "q_up_dw")
    dcq, dqag_acc = _rowwise(lambda v, dy, g: _rms_bwd(v, g, dy, Q_LORA), name="qa_norm_bwd", nblk=nlb, tr=tr,
                             rows=[(z, q_cb, Q_LORA, 0), (dcqn, 0, Q_LORA, 0)], fulls=[q_a_g],
                             outs=[(Q_LORA, MXU_DTYPE)], accs=[Q_LORA])

    def khead_bwd_body(kv_, krv, dkv_, dvv_, cv, sav, sbv, g):
        kpe = pltpu.roll(krv, QK_NOPE, 1)
        lane = lax.broadcasted_iota(jnp.int32, krv.shape, 1)
        dxs, dgs, dkr_ = [], [], jnp.zeros(krv.shape, F32)
        for t, dt_ in zip(_heads(kv_), _heads(dkv_)):
            dx_, dg_ = _rms_bwd(t + kpe, g[:, :SLOT], _rope_bwd(dt_, cv, sav, sbv), QK_DIM)
            dxs.append(jnp.where(lane < QK_NOPE, dx_, 0.0))
            dgs.append(dg_)
            dkr_ = dkr_ + dx_
        dkr_ = jnp.where(lane < QK_ROPE, pltpu.roll(dkr_, SLOT - QK_NOPE, 1), 0.0)
        return jnp.concatenate(dxs + [dvv_], axis=1), dkr_, jnp.concatenate(dgs, axis=1)

    dkvpre, dkr, dkg_acc = _rowwise(khead_bwd_body, name="k_head_bwd", nblk=nb, tr=tr,
                                    rows=[(kvpre, 0, hw, 0), (z, kr_cb, SLOT, 0), (dk_p, 0, hw, 0), (dv_p, 0, hw, 0)] + rope_rows(),
                                    fulls=[kg_p], outs=[(2 * hw, MXU_DTYPE), (SLOT, MXU_DTYPE)], accs=[hw])
    dckvn = _mm(dkvpre, wukv_p, "nt", "kv_up_dx")
    g_wukv_p = _mm(ckvn, dkvpre, "tn", "kv_up_dw")
    dckv, dkvag_acc = _rowwise(lambda v, dy, g: _rms_bwd(v, g, dy, KV_LORA), name="kva_norm_bwd", nblk=nb, tr=tr,
                               rows=[(z, kv_cb, KV_LORA, 0), (dckvn, 0, KV_LORA, 0)], fulls=[kv_a_g],
                               outs=[(KV_LORA, MXU_DTYPE)], accs=[KV_LORA])

    padc = lambda t: jnp.concatenate([t, jnp.zeros((nc, t.shape[1]), t.dtype)], axis=0)
    dz = jnp.concatenate([padc(dgl), du_nat.astype(MXU_DTYPE), dckv, dkr,
                          jnp.zeros((n, q_off - kr_off - SLOT), MXU_DTYPE), padc(dcq)], axis=1)
    dh1 = _mm(dz, win_p, "nt", "in_proj_dx")
    g_win_p = _mm(h1, dz, "tn", "in_proj_dw")

    def norm1_bwd_body(xv, dh, m, g):
        sc1 = m[:, d:2 * d]
        y = _rms_fwd(xv, g, d)
        dxn, dgc = _rms_bwd(xv, g, dh * (1.0 + sc1), d)
        return dxn, dgc, dh, dh * y

    dxa, dn1g_acc, dsh1_acc, dsc1_acc = _rowwise(norm1_bwd_body, name="norm1_bwd", nblk=nb, tr=tr,
                                                 rows=[(xa, 0, d, 0), (dh1, 0, d, 0)], sels=[modv], fulls=[n1g],
                                                 outs=[(d, F32)], accs=[d, d, d], seg=nlb)
    grad_x = (dxa[:nl] + dx1)[None]

    red8 = lambda a: jnp.sum(a, axis=-2)
    dmod_own = jnp.concatenate([red8(dsh1_acc[0]), red8(dsc1_acc[0]), red8(dg1_acc), red8(dsh2_acc), red8(dsc2_acc), red8(dg2_acc)])
    dmod_ctx = jnp.concatenate([red8(dsh1_acc[1]), red8(dsc1_acc[1]), jnp.zeros((4 * d,), F32)])
    dm_in = jnp.concatenate([dmod_own[None, :], dmod_ctx[None, :], jnp.zeros((6, d6), F32)], axis=0)
    (dm_all,) = _exchange8([dm_in], "gather_dmod", True)
    dm_own_sh = lax.dynamic_slice(dm_all[:, 0, :], (0, chip * csh), (8, csh))
    dm_ctx_sh = lax.dynamic_slice(dm_all[:, 1, :], (0, chip * csh), (8, csh))

    def mod_bwd_body(c_ref, own_ref, ctx_ref, w_ref, gw_ref, gb_ref, gc_ref):
        cv = c_ref[...]
        a = _silu(cv).astype(MXU_DTYPE)
        own = own_ref[...]
        ctx_tot = ctx_ref[0:1, :]
        for j in range(1, 8):
            ctx_tot = ctx_tot + ctx_ref[j:j + 1, :]
        g16 = jnp.concatenate([own, jnp.broadcast_to(ctx_tot, own.shape)], axis=0)
        rid = lax.broadcasted_iota(jnp.int32, g16.shape, 0)
        g16 = jnp.where(rid <= 8, g16, 0.0)
        gw_ref[...] = lax.dot_general(a, g16.astype(MXU_DTYPE), (((0,), (0,)), ((), ())), preferred_element_type=F32)
        gb_ref[...] = jnp.broadcast_to(jnp.sum(own, axis=0, keepdims=True) + ctx_tot, gb_ref.shape)
        gc = lax.dot_general(jnp.broadcast_to(ctx_tot, own.shape).astype(MXU_DTYPE), w_ref[...].astype(MXU_DTYPE),
                             (((1,), (1,)), ((), ())), preferred_element_type=F32)
        gc_ref[...] = gc * _dsilu(cv[8:9, :])

    g_wmod, g_bmod_sh, g_cctx_part = pl.pallas_call(
        mod_bwd_body, name="mod_bwd", out_shape=[_sds((d, csh), F32), _sds((8, csh), F32), _sds((8, d), F32)],
        compiler_params=pltpu.CompilerParams(vmem_limit_bytes=VMEM_LIMIT))(cs16, dm_own_sh, dm_ctx_sh, w_mod[0])
    north = (mc == 0).astype(F32)
    g_bmod_part = lax.dynamic_update_slice(jnp.zeros((1, d6), F32), g_bmod_sh[0:1] * north, (0, chip * csh))
    g_cctx_part = g_cctx_part[0] * north

    small_g = {}
    for j, dr in enumerate(dirs):
        sfx = dr[0]
        _, dbbd_j, dcbd_j, dlam_j = s5b[j]
        dl = red8(dlam_j).reshape(N_CG, 2, CG_STATES)
        db_re, db_im = _diag_extract(dbbd_j)
        cot = (dl[:, 0].reshape(SSM_GROUPS, SSM_STATE), dl[:, 1].reshape(SSM_GROUPS, SSM_STATE),
               jnp.transpose(db_re, (0, 2, 1)), jnp.transpose(db_im, (0, 2, 1)))
        g_lre, g_lim, g_ldt, g_bre, g_bim = disc_vjps[j](cot)
        small_g["lam_re_" + sfx], small_g["lam_im_" + sfx], small_g["log_dt_" + sfx] = g_lre, g_lim, g_ldt
        small_g["b_re"] = small_g.get("b_re", 0.0) + g_bre
        small_g["b_im"] = small_g.get("b_im", 0.0) + g_bim
        dc_re, dc_im = _diag_extract(dcbd_j)
        small_g["c_re_" + sfx], small_g["c_im_" + sfx] = dc_re, -dc_im
    head_fold = lambda acc: jnp.sum(red8(acc).reshape(N_HEADS, SLOT), axis=0)[:QK_DIM]
    small_g.update(c_ctx=g_cctx_part, b_mod=g_bmod_part[0], norm1_g=red8(dn1g_acc[0]) + red8(dn1g_acc[1]),
                   norm2_g=red8(dn2g_acc), q_a_g=red8(dqag_acc), kv_a_g=red8(dkvag_acc), q_norm_g=head_fold(dqg_acc),
                   k_norm_g=head_fold(dkg_acc), d_skip=red8(ddskip_acc), conv_b=red8(dcb_acc))
    g_convw_full = jnp.stack([red8(dcw0), red8(dcw1), red8(dcw2)])
    small_names = ["c_ctx", "b_mod", "norm1_g", "norm2_g", "q_a_g", "kv_a_g", "q_norm_g", "k_norm_g",
                   "lam_re_f", "lam_im_f", "log_dt_f", "c_re_f", "c_im_f", "lam_re_b", "lam_im_b", "log_dt_b",
                   "c_re_b", "c_im_b", "b_re", "b_im", "d_skip", "conv_b"]
    small_shapes = [weights[k].shape for k in small_names]
    spack = _pack([small_g[k] for k in small_names] + [g_convw_full], rows_mult=8)
    sred = _sum8(_exchange8([spack], "gather_small_grads", True)[0], "sum_small_grads")
    sg_list = _unpack(sred, small_shapes + [(3, f2)])
    g_small = dict(zip(small_names, sg_list[:-1]))
    g_small["conv_w"] = lax.dynamic_slice(sg_list[-1], (0, chip * cwid), (3, cwid))[None]

    gwi = g_win_p
    g_big["w_in"] = jnp.concatenate([gwi[:, q_off:q_off + Q_LORA], gwi[:, kv_off:kv_off + KV_LORA],
                                     gwi[:, kr_off:kr_off + QK_ROPE], gwi[:, u_off:u_off + SSM_WIDTH], gwi[:, :2 * d]], axis=1)
    g_big["w_uq"] = g_wuq_p.reshape(Q_LORA, N_HEADS, SLOT)[:, :, :QK_DIM].reshape(Q_LORA, N_HEADS * QK_DIM)
    gk3 = g_wukv_p[:, :hw].reshape(KV_LORA, N_HEADS, SLOT)[:, :, :QK_NOPE]
    gv3 = g_wukv_p[:, hw:].reshape(KV_LORA, N_HEADS, SLOT)[:, :, :V_DIM]
    g_big["w_ukv"] = jnp.concatenate([gk3, gv3], axis=2).reshape(KV_LORA, N_HEADS * (QK_NOPE + V_DIM))
    g_big["w_o_attn"] = g_wo_p.reshape(N_HEADS, SLOT, d)[:, :V_DIM].reshape(N_HEADS * V_DIM, d)

    def pieces(k_):
        r_, c_ = weights[k_].shape[1:]
        if k_ in row_sharded:
            p4 = jnp.transpose(g_big[k_].reshape(4, 2, r_ // 2, c_), (1, 0, 2, 3))
        else:
            p4 = jnp.transpose(g_big[k_].reshape(2, r_ // 2, 4, c_), (0, 2, 1, 3))
        return p4.reshape(2, 2 * r_, c_)

    pcs = [pieces(k_) for k_ in big_names]
    from_sibling = _sibling_send(pcs, "grads_to_sibling")
    my_half = []
    for k_, p_, got in zip(big_names, pcs, from_sibling):
        rows4, c_ = got.shape
        rh = rows4 // 4
        own = lax.dynamic_index_in_dim(p_, mc, 0, keepdims=False)
        tr_ = _pick(rows4, (256, 128, 64, 32, 16))
        s32, sb = _rowwise(lambda a, b: (a + b, a + b), name="sum_chip_" + k_, nblk=rows4 // tr_, tr=tr_,
                           rows=[(own, 0, c_, 0), (got, 0, c_, 0)], outs=[(c_, F32), (c_, MXU_DTYPE)])
        my_half.append((s32, sb, rh, c_))
    recv3 = _exchange_chips([sb.reshape(4, rh, c_) for _, sb, rh, c_ in my_half], "scatter_weight_grads", False)
    reduced = []
    for k_, (s32, _, rh, c_), r3 in zip(big_names, my_half, recv3):
        mine = lax.dynamic_slice(s32, (chip * rh, 0), (rh, c_))
        tr_ = _pick(rh, (256, 128, 64, 32, 16))
        (red,) = _rowwise(lambda a, b0, b1, b2: a + b0 + b1 + b2, name="sum_grad_" + k_, nblk=rh // tr_, tr=tr_,
                          rows=[(mine, 0, c_, 0)] + [(r3.reshape(3 * rh, c_), 0, c_, j * rh) for j in range(3)],
                          outs=[(c_, F32)])
        reduced.append(red)
    both = _sibling_exchange(reduced, "exchange_halves")
    g_sh = {k_: b_.reshape((1,) + weights[k_].shape[1:]) for k_, b_ in zip(big_names, both)}
    g_sh["w_mod"] = g_wmod[None]

    grads = {**g_sh, **g_small}
    outs_d, outs_m, outs_v = {}, {}, {}
    for k_ in ["w_mod"] + big_names:
        shp = weights[k_].shape
        res = _adamw(*[t.reshape(shp[1:]) for t in (grads[k_], weights[k_], mom_m[k_], mom_v[k_])], "adamw_" + k_)
        for dst, buf in zip((outs_d, outs_m, outs_v), res):
            dst[k_] = buf.reshape(shp)
    adam_small = small_names + ["conv_w"]
    shapes = [weights[k_].shape for k_ in adam_small]
    res = _adamw(*[_pack([src[k_] for k_ in adam_small], rows_mult=8) for src in (grads, weights, mom_m, mom_v)], "adamw_small")
    for dst, buf in zip((outs_d, outs_m, outs_v), res):
        dst.update(zip(adam_small, _unpack(buf, shapes)))
    grads = {k_: grads[k_].reshape(weights[k_].shape) for k_ in names}
    return (loss, grad_x, *[grads[k_] for k_ in names], *[outs_d[k_] for k_ in names],
            *[outs_m[k_] for k_ in names], *[outs_v[k_] for k_ in names])
```

```python
import functools
import math

import numpy as np
import jax
import jax.numpy as jnp
from jax import lax
from jax.experimental import pallas as pl
from jax.experimental.pallas import tpu as pltpu

F32 = jnp.float32
MXU_DTYPE = jnp.bfloat16
MESH = pl.DeviceIdType.MESH

EPS = 1e-6
N_HEADS = 8
QK_NOPE = 64
QK_ROPE = 32
QK_DIM = QK_NOPE + QK_ROPE
V_DIM = 64
SLOT = 128
Q_LORA = 384
KV_LORA = 256
GRID_W = 64
ROPE_THETA = 10000.0
SSM_WIDTH = 512
SSM_GROUP = 16
SSM_GROUPS = 32
SSM_STATE = 64
N_STATE = SSM_GROUPS * SSM_STATE
CG_STATES = 512
N_CG = N_STATE // CG_STATES
CG_CHANNELS = SSM_WIDTH // N_CG
SCAN_LANES = 512
PACK_W = 1024

ADAM_LR = 0.001
ADAM_B1 = 0.9
ADAM_B2 = 0.999
ADAM_EPS = 1e-08
ADAM_WD = 0.01
ADAM_STEP = 10

VMEM_LIMIT = 56 * 1024 * 1024
LOG2E = 1.4426950408889634


def _pick(n, cands):
    for c in cands:
        if c <= n and n % c == 0:
            return c
    return n


def _cparams(sem):
    return pltpu.CompilerParams(dimension_semantics=sem, vmem_limit_bytes=VMEM_LIMIT)


def _sds(shape, dtype):
    return jax.ShapeDtypeStruct(tuple(shape), dtype)


_K_CANDS = (2816, 2048, 1536, 1408, 1280, 1152, 1024, 896, 768, 704, 640, 512, 384, 256, 128, 64, 32, 16)
_M_CANDS = (2048, 1408, 1024, 768, 512, 384, 256, 128, 64, 32, 16)
_N_CANDS = (1408, 1152, 1024, 768, 512, 384, 256, 128)
MM_VMEM_BUDGET = 40 * 1024 * 1024


def _mm_tiles(m, n, k_opts, a_bytes, b_bytes, o_bytes, m_cands):
    tn = n if n <= _N_CANDS[0] else _pick(n, _N_CANDS)
    for tk in k_opts:
        for tm in ((m,) if m <= m_cands[0] else ()) + tuple(t for t in m_cands if t < m and m % t == 0):
            if 2 * (tm * tk * a_bytes + tk * tn * b_bytes + tm * tn * o_bytes) + tm * tn * 4 <= MM_VMEM_BUDGET:
                return tm, tn, tk
    raise ValueError("no matmul tiling fits")


def _mm(a, b, mode, name, out_dtype=F32, rows=None, a_off=0, b_off=0):
    a_bytes, b_bytes, o_bytes = a.dtype.itemsize, b.dtype.itemsize, jnp.dtype(out_dtype).itemsize
    if mode == "tn":
        t_rows = rows or a.shape[0]
        m, n = a.shape[1], b.shape[1]
        k_opts = tuple(t for t in _K_CANDS if t <= t_rows and t_rows % t == 0) or (t_rows,)
        tm, tn, tk = _mm_tiles(m, n, k_opts, a_bytes, b_bytes, o_bytes, _M_CANDS[1:])
        nk = t_rows // tk
        ao, bo = a_off // tk, b_off // tk
        grid = (m // tm, n // tn, nk)
        in_specs = [pl.BlockSpec((tk, tm), lambda i, j, k: (k + ao, i)),
                    pl.BlockSpec((tk, tn), lambda i, j, k: (k + bo, j))]
        dn = (((0,), (0,)), ((), ()))
    else:
        m = rows or a.shape[0]
        kdim = a.shape[1]
        n = b.shape[1] if mode == "nn" else b.shape[0]
        k_opts = (kdim,) + tuple(t for t in _K_CANDS if t < kdim and kdim % t == 0)
        tm, tn, tk = _mm_tiles(m, n, k_opts, a_bytes, b_bytes, o_bytes, _M_CANDS)
        nk = kdim // tk
        ao = a_off // tm
        grid = (m // tm, n // tn, nk)
        if mode == "nn":
            in_specs = [pl.BlockSpec((tm, tk), lambda i, j, k: (i + ao, k)),
                        pl.BlockSpec((tk, tn), lambda i, j, k: (k, j))]
            dn = (((1,), (0,)), ((), ()))
        else:
            in_specs = [pl.BlockSpec((tm, tk), lambda i, j, k: (i + ao, k)),
                        pl.BlockSpec((tn, tk), lambda i, j, k: (j, k))]
            dn = (((1,), (1,)), ((), ()))
    use_scratch = nk > 1 and out_dtype != F32

    def body(a_ref, b_ref, o_ref, *scr):
        r = lax.dot_general(a_ref[...].astype(MXU_DTYPE), b_ref[...].astype(MXU_DTYPE), dn,
                            preferred_element_type=F32)
        if nk == 1:
            o_ref[...] = r.astype(o_ref.dtype)
        else:
            k = pl.program_id(2)
            acc = scr[0] if use_scratch else o_ref

            @pl.when(k == 0)
            def _():
                acc[...] = r

            @pl.when(k > 0)
            def _():
                acc[...] += r

            if use_scratch:
                @pl.when(k == nk - 1)
                def _():
                    o_ref[...] = acc[...].astype(o_ref.dtype)

    return pl.pallas_call(
        body, name=name, grid=grid, in_specs=in_specs,
        out_specs=pl.BlockSpec((tm, tn), lambda i, j, k: (i, j)),
        out_shape=_sds((m, n), out_dtype),
        scratch_shapes=[pltpu.VMEM((tm, tn), F32)] if use_scratch else [],
        compiler_params=_cparams(("parallel", "parallel", "arbitrary")),
    )(a, b)


def _rowwise(body, *, name, nblk, tr, rows=(), halo=(), sels=(), fulls=(), outs=(), accs=(), seg=None):
    n_rows, n_sel, n_full, n_out, n_acc = len(rows), len(sels), len(fulls), len(outs), len(accs)
    halo = tuple(halo)
    maxw = max([r[2] for r in rows] + [o[0] for o in outs] + list(accs))
    sr = _pick(tr, tuple(s for s in (256, 128, 64, 32, 16) if s * maxw <= 131072) or (16,))
    nsub = tr // sr
    total8 = nblk * tr // 8

    def seg_of(i):
        return jnp.where(i >= seg, 1, 0) if seg is not None else 0

    in_specs, operands = [], []
    for arr, cb, w, roff in rows:
        ob = roff // tr
        last = arr.shape[0] // tr - 1
        in_specs.append(pl.BlockSpec((tr, w), lambda i, cb=cb, ob=ob, last=last: (jnp.minimum(i + ob, last), cb)))
        operands.append(arr)
    for h in halo:
        arr, cb, w, roff = rows[h]
        o8, t8 = roff // 8, tr // 8
        in_specs.append(pl.BlockSpec((8, w), lambda i, cb=cb, o8=o8, t8=t8: (jnp.maximum(i * t8 - 1, 0) + o8, cb)))
        in_specs.append(pl.BlockSpec((8, w), lambda i, cb=cb, o8=o8, t8=t8: (jnp.minimum((i + 1) * t8, total8 - 1) + o8, cb)))
        operands += [arr, arr]
    for arr in sels:
        in_specs.append(pl.BlockSpec((None,) + arr.shape[1:], lambda i: (seg_of(i), 0, 0)))
        operands.append(arr)
    for arr in fulls:
        in_specs.append(pl.BlockSpec(arr.shape, lambda i: (0, 0)))
        operands.append(arr)
    out_specs, out_shape = [], []
    for w, dt in outs:
        out_specs.append(pl.BlockSpec((tr, w), lambda i: (i, 0)))
        out_shape.append(_sds((nblk * tr, w), dt))
    for w in accs:
        if seg is None:
            out_specs.append(pl.BlockSpec((8, w), lambda i: (0, 0)))
            out_shape.append(_sds((8, w), F32))
        else:
            out_specs.append(pl.BlockSpec((None, 8, w), lambda i: (seg_of(i), 0, 0)))
            out_shape.append(_sds((2, 8, w), F32))
    n_halo = 2 * len(halo)

    def kern(*refs):
        row_refs = refs[:n_rows]
        halo_refs = refs[n_rows:n_rows + n_halo]
        sel_refs = refs[n_rows + n_halo:n_rows + n_halo + n_sel]
        full_refs = refs[n_rows + n_halo + n_sel:n_rows + n_halo + n_sel + n_full]
        o0 = n_rows + n_halo + n_sel + n_full
        out_refs = refs[o0:o0 + n_out]
        acc_refs = refs[o0 + n_out:o0 + n_out + n_acc]
        i = pl.program_id(0)
        if n_acc:
            first = (i == 0) if seg is None else ((i == 0) | (i == seg))

            @pl.when(first)
            def _():
                for a_ref in acc_refs:
                    a_ref[...] = jnp.zeros(a_ref.shape, F32)

        def sub(s, carry):
            r0 = pl.multiple_of(s * sr, sr)
            vals = []
            for idx, r in enumerate(row_refs):
                cur = r[pl.ds(r0, sr), :]
                if idx in halo:
                    hp = halo_refs[2 * halo.index(idx)]
                    hn = halo_refs[2 * halo.index(idx) + 1]
                    cur = cur.astype(F32)
                    rid = lax.broadcasted_iota(jnp.int32, cur.shape, 0)
                    lo = r[pl.ds(pl.multiple_of(jnp.maximum(r0 - 8, 0), 8), 8), :].astype(F32)
                    lo = jnp.where(s == 0, hp[...].astype(F32), lo)
                    lo = jnp.where((s == 0) & (i == 0), 0.0, lo)
                    hi = r[pl.ds(pl.multiple_of(jnp.minimum(r0 + sr, tr - 8), 8), 8), :].astype(F32)
                    hi = jnp.where(s == nsub - 1, hn[...].astype(F32), hi)
                    hi = jnp.where((s == nsub - 1) & (i == nblk - 1), 0.0, hi)
                    prev = jnp.where(rid == 0, jnp.broadcast_to(lo[7:8, :], cur.shape), pltpu.roll(cur, 1, 0))
                    nxt = jnp.where(rid == sr - 1, jnp.broadcast_to(hi[0:1, :], cur.shape), pltpu.roll(cur, sr - 1, 0))
                    vals.append((prev, cur, nxt))
                else:
                    vals.append(cur)
            res = body(*vals, *[r[...] for r in sel_refs], *[r[...] for r in full_refs])
            if not isinstance(res, (tuple, list)):
                res = (res,)
            for o_ref, v in zip(out_refs, res[:n_out]):
                o_ref[pl.ds(r0, sr), :] = v.astype(o_ref.dtype)
            for a_ref, v in zip(acc_refs, res[n_out:]):
                a_ref[...] += jnp.sum(v.astype(F32).reshape(sr // 8, 8, v.shape[-1]), axis=0)
            return carry

        lax.fori_loop(0, nsub, sub, 0)

    res = pl.pallas_call(
        kern, name=name, grid=(nblk,), in_specs=in_specs, out_specs=out_specs, out_shape=out_shape,
        compiler_params=_cparams(("arbitrary",)),
    )(*operands)
    return res


def _sigmoid(x):
    return 1.0 / (1.0 + jnp.exp(-x))


def _silu(x):
    return x * _sigmoid(x)


def _dsilu(x):
    s = _sigmoid(x)
    return s * (1.0 + x * (1.0 - s))


_GELU_K = math.sqrt(2.0 / math.pi)


def _gelu(x):
    return 0.5 * x * (1.0 + jnp.tanh(_GELU_K * (x + 0.044715 * x * x * x)))


def _dgelu(x):
    t = jnp.tanh(_GELU_K * (x + 0.044715 * x * x * x))
    return 0.5 * (1.0 + t) + 0.5 * x * (1.0 - t * t) * _GELU_K * (1.0 + 3.0 * 0.044715 * x * x)


def _rms_fwd(x, g, width):
    r = lax.rsqrt(jnp.sum(x * x, axis=-1, keepdims=True) * (1.0 / width) + EPS)
    return x * r * g


def _rms_bwd(x, g, dy, width):
    r = lax.rsqrt(jnp.sum(x * x, axis=-1, keepdims=True) * (1.0 / width) + EPS)
    xn = x * r
    dyg = dy * g
    dx = r * (dyg - xn * (jnp.sum(dyg * xn, axis=-1, keepdims=True) * (1.0 / width)))
    return dx, dy * xn


def _rope_fwd(y, c, sa, sb):
    return y * c + pltpu.roll(y, SLOT - 16, 1) * sa + pltpu.roll(y, 16, 1) * sb


def _rope_bwd(d, c, sa, sb):
    return d * c + pltpu.roll(d * sa, 16, 1) + pltpu.roll(d * sb, SLOT - 16, 1)


def _heads(v):
    return [v[:, h * SLOT:(h + 1) * SLOT] for h in range(N_HEADS)]


def _attn_fwd(q, k, v, nl, scale):
    n = k.shape[0]
    tq = _pick(nl, (4096, 2048, 1024, 512, 256, 128))
    tk = _pick(n, (2816, 1408, 1152, 768, 384, 256, 128))
    sub = min(tq, 512)
    nk = n // tk
    rep = tk // SLOT
    c = scale * LOG2E

    def body(q_ref, k_ref, v_ref, o_ref, lse_ref, m_sc, l_sc, acc_sc):
        ki = pl.program_id(2)

        @pl.when(ki == 0)
        def _():
            m_sc[...] = jnp.full(m_sc.shape, -jnp.inf, F32)
            l_sc[...] = jnp.zeros(l_sc.shape, F32)
            acc_sc[...] = jnp.zeros(acc_sc.shape, F32)

        kb, vb = k_ref[...], v_ref[...]
        for sb in range(tq // sub):
            rows = slice(sb * sub, (sb + 1) * sub)
            s = lax.dot_general(q_ref[rows, :], kb, (((1,), (1,)), ((), ())), preferred_element_type=F32)
            m_prev = m_sc[rows, :]
            m_new = jnp.maximum(m_prev, jnp.max(s, axis=1, keepdims=True) * c)
            alpha = jnp.exp2(m_prev - m_new)
            p = jnp.exp2(s * c - jnp.tile(m_new, (1, rep)))
            l_sc[rows, :] = alpha * l_sc[rows, :] + jnp.sum(p, axis=1, keepdims=True)
            acc_sc[rows, :] = alpha * acc_sc[rows, :] + jnp.dot(p.astype(MXU_DTYPE), vb, preferred_element_type=F32)
            m_sc[rows, :] = m_new

        @pl.when(ki == nk - 1)
        def _():
            l = l_sc[...]
            o_ref[...] = (acc_sc[...] / l).astype(o_ref.dtype)
            lse_ref[...] = jnp.transpose(m_sc[...] + jnp.log2(l))[0:8, :]

    return pl.pallas_call(
        body, name="attn_fwd", grid=(N_HEADS, nl // tq, nk),
        in_specs=[pl.BlockSpec((tq, SLOT), lambda h, i, j: (i, h)),
                  pl.BlockSpec((tk, SLOT), lambda h, i, j: (j, h)),
                  pl.BlockSpec((tk, SLOT), lambda h, i, j: (j, h))],
        out_specs=[pl.BlockSpec((tq, SLOT), lambda h, i, j: (i, h)),
                   pl.BlockSpec((None, 8, tq), lambda h, i, j: (h, 0, i))],
        out_shape=[_sds((nl, N_HEADS * SLOT), MXU_DTYPE), _sds((N_HEADS, 8, nl), F32)],
        scratch_shapes=[pltpu.VMEM((tq, SLOT), F32), pltpu.VMEM((tq, SLOT), F32), pltpu.VMEM((tq, SLOT), F32)],
        compiler_params=_cparams(("parallel", "parallel", "arbitrary")),
    )(q, k, v)


def _attn_bwd(q, k, v, do, o, lse_t, nl, scale):
    n = k.shape[0]
    tq = _pick(nl, (2048, 1024, 512, 256, 128))
    tk = _pick(n, (2816, 1408, 1152, 768, 384, 256, 128))
    sub = _pick(tk, (256, 128))
    nq, nk = nl // tq, n // tk
    c = scale * LOG2E

    def body(q_ref, k_ref, v_ref, do_ref, o_ref, lse_ref, dq_ref, dk_ref, dv_ref, dq_acc, dk_acc, dv_acc):
        ki, qi = pl.program_id(1), pl.program_id(2)

        @pl.when((ki == 0) & (qi == 0))
        def _():
            dq_acc[...] = jnp.zeros(dq_acc.shape, F32)

        @pl.when(qi == 0)
        def _():
            dk_acc[...] = jnp.zeros(dk_acc.shape, F32)
            dv_acc[...] = jnp.zeros(dv_acc.shape, F32)

        qb, dof = q_ref[...], do_ref[...]
        dob = dof.astype(MXU_DTYPE)
        lse_r = lse_ref[0:1, :]
        dl_r = jnp.sum(jnp.transpose(dof * o_ref[...].astype(F32)), axis=0, keepdims=True)
        dq_part = None
        for sb in range(tk // sub):
            rows = slice(sb * sub, (sb + 1) * sub)
            kb = k_ref[rows, :]
            s_t = lax.dot_general(kb, qb, (((1,), (1,)), ((), ())), preferred_element_type=F32)
            p_t = jnp.exp2(s_t * c - lse_r)
            dp_t = lax.dot_general(v_ref[rows, :], dob, (((1,), (1,)), ((), ())), preferred_element_type=F32)
            ds_t = (p_t * (dp_t - dl_r) * scale).astype(MXU_DTYPE)
            dv_acc[rows, :] += jnp.dot(p_t.astype(MXU_DTYPE), dob, preferred_element_type=F32)
            dk_acc[rows, :] += jnp.dot(ds_t, qb, preferred_element_type=F32)
            part = lax.dot_general(kb, ds_t, (((0,), (0,)), ((), ())), preferred_element_type=F32)
            dq_part = part if dq_part is None else dq_part + part
        c0 = pl.multiple_of(qi * tq, tq)
        dq_acc[:, pl.ds(c0, tq)] += dq_part

        @pl.when(ki == nk - 1)
        def _():
            dq_ref[...] = jnp.transpose(dq_acc[:, pl.ds(c0, tq)])

        @pl.when(qi == nq - 1)
        def _():
            dk_ref[...] = dk_acc[...]
            dv_ref[...] = dv_acc[...]

    return pl.pallas_call(
        body, name="attn_bwd", grid=(N_HEADS, nk, nq),
        in_specs=[pl.BlockSpec((tq, SLOT), lambda h, j, i: (i, h)),
                  pl.BlockSpec((tk, SLOT), lambda h, j, i: (j, h)),
                  pl.BlockSpec((tk, SLOT), lambda h, j, i: (j, h)),
                  pl.BlockSpec((tq, SLOT), lambda h, j, i: (i, h)),
                  pl.BlockSpec((tq, SLOT), lambda h, j, i: (i, h)),
                  pl.BlockSpec((None, 8, tq), lambda h, j, i: (h, 0, i))],
        out_specs=[pl.BlockSpec((tq, SLOT), lambda h, j, i: (jnp.where(j == nk - 1, i, 0), h)),
                   pl.BlockSpec((tk, SLOT), lambda h, j, i: (j, h)),
                   pl.BlockSpec((tk, SLOT), lambda h, j, i: (j, h))],
        out_shape=[_sds((nl, N_HEADS * SLOT), F32), _sds((n, N_HEADS * SLOT), F32), _sds((n, N_HEADS * SLOT), F32)],
        scratch_shapes=[pltpu.VMEM((SLOT, nl), F32), pltpu.VMEM((tk, SLOT), F32), pltpu.VMEM((tk, SLOT), F32)],
        compiler_params=_cparams(("arbitrary", "arbitrary", "arbitrary")),
    )(q, k, v, do, o, lse_t)


def _scan_consts(c_ref, lg):
    cs = slice(lg * SCAN_LANES, (lg + 1) * SCAN_LANES)
    return [c_ref[8 * kk:8 * kk + 8, cs] for kk in range(8)]


def _tile_scan(br, bi, consts, reverse):
    p1r, p1i, p2r, p2i, p4r, p4i = consts[:6]
    for pr, pi, kk in ((p1r, p1i, 1), (p2r, p2i, 2), (p4r, p4i, 4)):
        sh = (8 - kk) if reverse else kk
        sr_, si_ = pltpu.roll(br, sh, 0), pltpu.roll(bi, sh, 0)
        br, bi = br + pr * sr_ - pi * si_, bi + pr * si_ + pi * sr_
    return br, bi


def _seq_chunk(j, nch, nlc, reverse):
    return (nch - 1 - j) if reverse else (j + nlc) % nch


def _s5_scan(z, u_cb, bbd, cbd_n, lamc, t_rows, nl, reverse, name):
    n = z.shape[0]
    nch, nlc = n // t_rows, nl // t_rows
    ntile = t_rows // 8
    w = SCAN_LANES
    edge = 0 if reverse else 7
    ucb = u_cb * (SSM_WIDTH // CG_CHANNELS)

    def chunk(j):
        return _seq_chunk(j, nch, nlc, reverse)

    def body(u_ref, b_ref, cn_ref, c_ref, xs_ref, y_ref, carry):
        j = pl.program_id(1)

        @pl.when(j == 0)
        def _():
            carry[...] = jnp.zeros(carry.shape, F32)

        xs_ref[...] = jnp.dot(u_ref[...].astype(MXU_DTYPE), b_ref[...], preferred_element_type=F32)
        for lg in range(CG_STATES // w):
            re = slice(lg * w, (lg + 1) * w)
            im = slice(CG_STATES + lg * w, CG_STATES + (lg + 1) * w)
            consts = _scan_consts(c_ref, lg)
            qr, qi = consts[6], consts[7]

            def tile(tt, st):
                cr, ci = st
                t = (ntile - 1 - tt) if reverse else tt
                r0 = pl.multiple_of(t * 8, 8)
                br, bi = _tile_scan(xs_ref[pl.ds(r0, 8), re], xs_ref[pl.ds(r0, 8), im], consts, reverse)
                lr = jnp.broadcast_to(cr[edge:edge + 1, :], br.shape)
                li = jnp.broadcast_to(ci[edge:edge + 1, :], bi.shape)
                xr = br + qr * lr - qi * li
                xi = bi + qr * li + qi * lr
                xs_ref[pl.ds(r0, 8), re] = xr
                xs_ref[pl.ds(r0, 8), im] = xi
                return xr, xi

            cr, ci = lax.fori_loop(0, ntile, tile, (carry[:, re], carry[:, im]))
            carry[:, re] = cr
            carry[:, im] = ci
        y_ref[...] = jnp.dot(xs_ref[...].astype(MXU_DTYPE), cn_ref[...], preferred_element_type=F32)

    cw = 2 * CG_STATES
    return pl.pallas_call(
        body, name=name, grid=(N_CG, nch),
        in_specs=[pl.BlockSpec((t_rows, CG_CHANNELS), lambda g, j: (chunk(j), ucb + g)),
                  pl.BlockSpec((CG_CHANNELS, cw), lambda g, j: (g, 0)),
                  pl.BlockSpec((cw, CG_CHANNELS), lambda g, j: (g, 0)),
                  pl.BlockSpec((64, CG_STATES), lambda g, j: (0, g))],
        out_specs=[pl.BlockSpec((t_rows, cw), lambda g, j: (chunk(j), g)),
                   pl.BlockSpec((t_rows, CG_CHANNELS), lambda g, j: (chunk(j), g))],
        out_shape=[_sds((n, 2 * N_STATE), F32), _sds((n, SSM_WIDTH), F32)],
        scratch_shapes=[pltpu.VMEM((8, cw), F32)],
        compiler_params=_cparams(("arbitrary", "arbitrary")),
    )(z, bbd, cbd_n, lamc)


def _s5_bwd(dys, z, u_cb, xs, cbd_t, bbd_t, lamc_adj, t_rows, nl, reverse, name):
    n = z.shape[0]
    nch, nlc = n // t_rows, nl // t_rows
    ntile = t_rows // 8
    t8 = t_rows // 8
    w = SCAN_LANES
    cw = 2 * CG_STATES
    adj_rev = not reverse
    edge = 0 if adj_rev else 7

    def chunk(j):
        return _seq_chunk(nch - 1 - j, nch, nlc, reverse)

    def halo_blk(j):
        if reverse:
            return jnp.minimum((chunk(j) + 1) * t8, n // 8 - 1)
        return (_seq_chunk(jnp.maximum(nch - 2 - j, 0), nch, nlc, False) + 1) * t8 - 1

    def body(dy_ref, u_ref, xs_ref, halo_ref, ct_ref, bt_ref, c_ref, du_ref, db_ref, dc_ref, dl_ref, gbuf, carry):
        j = pl.program_id(1)
        start = j == nch - 1

        @pl.when(j == 0)
        def _():
            carry[...] = jnp.zeros(carry.shape, F32)
            db_ref[...] = jnp.zeros(db_ref.shape, F32)
            dc_ref[...] = jnp.zeros(dc_ref.shape, F32)
            dl_ref[...] = jnp.zeros(dl_ref.shape, F32)

        dy = jnp.where(chunk(j) < nlc, dy_ref[...], 0.0).astype(MXU_DTYPE)
        gbuf[...] = jnp.dot(dy, ct_ref[...], preferred_element_type=F32)
        dc_ref[...] += lax.dot_general(dy, xs_ref[...].astype(MXU_DTYPE), (((0,), (0,)), ((), ())),
                                       preferred_element_type=F32)
        for lg in range(CG_STATES // w):
            re = slice(lg * w, (lg + 1) * w)
            im = slice(CG_STATES + lg * w, CG_STATES + (lg + 1) * w)
            consts = _scan_consts(c_ref, lg)
            qr, qi = consts[6], consts[7]
            hr, hi = halo_ref[:, re], halo_ref[:, im]

            def tile(tt, st):
                gcr, gci, ar, ai = st
                t = (ntile - 1 - tt) if adj_rev else tt
                r0 = pl.multiple_of(t * 8, 8)
                br, bi = _tile_scan(gbuf[pl.ds(r0, 8), re], gbuf[pl.ds(r0, 8), im], consts, adj_rev)
                lr = jnp.broadcast_to(gcr[edge:edge + 1, :], br.shape)
                li = jnp.broadcast_to(gci[edge:edge + 1, :], bi.shape)
                gr = br + qr * lr - qi * li
                gi = bi + qr * li + qi * lr
                gbuf[pl.ds(r0, 8), re] = gr
                gbuf[pl.ds(r0, 8), im] = gi
                xr, xi = xs_ref[pl.ds(r0, 8), re], xs_ref[pl.ds(r0, 8), im]
                rid = lax.broadcasted_iota(jnp.int32, xr.shape, 0)
                if reverse:
                    last = t == ntile - 1
                    rn = pl.multiple_of(jnp.minimum(r0 + 8, t_rows - 8), 8)
                    nbr = jnp.where(last, hr, xs_ref[pl.ds(rn, 8), re])
                    nbi = jnp.where(last, hi, xs_ref[pl.ds(rn, 8), im])
                    nbr = jnp.where(last & start, 0.0, nbr)
                    nbi = jnp.where(last & start, 0.0, nbi)
                    xpr = jnp.where(rid == 7, jnp.broadcast_to(nbr[0:1, :], xr.shape), pltpu.roll(xr, 7, 0))
                    xpi = jnp.where(rid == 7, jnp.broadcast_to(nbi[0:1, :], xi.shape), pltpu.roll(xi, 7, 0))
                else:
                    first = t == 0
                    rn = pl.multiple_of(jnp.maximum(r0 - 8, 0), 8)
                    nbr = jnp.where(first, hr, xs_ref[pl.ds(rn, 8), re])
                    nbi = jnp.where(first, hi, xs_ref[pl.ds(rn, 8), im])
                    nbr = jnp.where(first & start, 0.0, nbr)
                    nbi = jnp.where(first & start, 0.0, nbi)
                    xpr = jnp.where(rid == 0, jnp.broadcast_to(nbr[7:8, :], xr.shape), pltpu.roll(xr, 1, 0))
                    xpi = jnp.where(rid == 0, jnp.broadcast_to(nbi[7:8, :], xi.shape), pltpu.roll(xi, 1, 0))
                ar = ar + gr * xpr + gi * xpi
                ai = ai - gr * xpi + gi * xpr
                return gr, gi, ar, ai

            zz = jnp.zeros((8, w), F32)
            gcr, gci, ar, ai = lax.fori_loop(0, ntile, tile, (carry[:, re], carry[:, im], zz, zz))
            carry[:, re] = gcr
            carry[:, im] = gci
            dl_ref[:, re] += ar
            dl_ref[:, im] += ai
        g = gbuf[...].astype(MXU_DTYPE)
        du_ref[...] = jnp.dot(g, bt_ref[...], preferred_element_type=F32)
        db_ref[...] += lax.dot_general(u_ref[...].astype(MXU_DTYPE), g, (((0,), (0,)), ((), ())),
                                       preferred_element_type=F32)

    ucb = u_cb * (SSM_WIDTH // CG_CHANNELS)
    return pl.pallas_call(
        body, name=name, grid=(N_CG, nch),
        in_specs=[pl.BlockSpec((t_rows, CG_CHANNELS), lambda g, j: (jnp.minimum(chunk(j), nlc - 1), g)),
                  pl.BlockSpec((t_rows, CG_CHANNELS), lambda g, j: (chunk(j), ucb + g)),
                  pl.BlockSpec((t_rows, cw), lambda g, j: (chunk(j), g)),
                  pl.BlockSpec((8, cw), lambda g, j: (halo_blk(j), g)),
                  pl.BlockSpec((CG_CHANNELS, cw), lambda g, j: (g, 0)),
                  pl.BlockSpec((cw, CG_CHANNELS), lambda g, j: (g, 0)),
                  pl.BlockSpec((64, CG_STATES), lambda g, j: (0, g))],
        out_specs=[pl.BlockSpec((t_rows, CG_CHANNELS), lambda g, j: (chunk(j), g)),
                   pl.BlockSpec((CG_CHANNELS, cw), lambda g, j: (g, 0)),
                   pl.BlockSpec((CG_CHANNELS, cw), lambda g, j: (g, 0)),
                   pl.BlockSpec((8, cw), lambda g, j: (0, g))],
        out_shape=[_sds((n, SSM_WIDTH), F32), _sds((SSM_WIDTH, cw), F32), _sds((SSM_WIDTH, cw), F32),
                   _sds((8, 2 * N_STATE), F32)],
        scratch_shapes=[pltpu.VMEM((t_rows, cw), F32), pltpu.VMEM((8, cw), F32)],
        compiler_params=_cparams(("arbitrary", "arbitrary")),
    )(dys, z, xs, xs, cbd_t, bbd_t, lamc_adj)


_CG_GROUPS = SSM_GROUPS // N_CG


def _group_mask():
    idx = jnp.arange(_CG_GROUPS)
    return (idx[:, None] == idx[None, :])[None, :, None, None, :, None]


def _diag_blocks(p_re, p_im):
    t = jnp.stack([p_re, p_im], axis=2).reshape(N_CG, _CG_GROUPS, SSM_GROUP, 2, 1, SSM_STATE)
    return jnp.where(_group_mask(), t, 0.0).reshape(SSM_WIDTH, 2 * CG_STATES)


def _diag_extract(d):
    d6 = d.reshape(N_CG, _CG_GROUPS, SSM_GROUP, 2, _CG_GROUPS, SSM_STATE)
    blk = jnp.sum(jnp.where(_group_mask(), d6, 0.0), axis=4)
    blk = blk.reshape(SSM_GROUPS, SSM_GROUP, 2, SSM_STATE)
    return blk[:, :, 0], blk[:, :, 1]


def _block_transpose(d):
    return jnp.transpose(d.reshape(N_CG, CG_CHANNELS, 2 * CG_STATES), (0, 2, 1)).reshape(2 * N_STATE, CG_CHANNELS)


def _s5_disc(lam_re, lam_im, log_dt, b_re, b_im):
    lam = lax.complex(lam_re, lam_im)
    dt = jnp.exp(log_dt)[:, None]
    lam_bar = jnp.exp(lam * dt)
    b_bar = ((lam_bar - 1.0) / lam)[..., None] * lax.complex(b_re, b_im)
    return jnp.real(lam_bar), jnp.imag(lam_bar), jnp.real(b_bar), jnp.imag(b_bar)


def _lam_consts(lr, li, mirrored, conj):
    lam = lax.complex(lr.reshape(-1), -li.reshape(-1) if conj else li.reshape(-1))
    p2 = lam * lam
    p4 = p2 * p2
    pw = [lam, p2, p2 * lam, p4, p4 * lam, p4 * p2, p4 * p2 * lam, p4 * p4]
    rows = jnp.arange(8)[:, None]
    out = []
    for kk in (1, 2, 4):
        mask = (rows <= 7 - kk) if mirrored else (rows >= kk)
        pk = jnp.where(mask, pw[kk - 1][None, :], 0.0)
        out += [jnp.real(pk), jnp.imag(pk)]
    q = jnp.stack(pw[::-1] if mirrored else pw)
    return jnp.concatenate(out + [jnp.real(q), jnp.imag(q)], axis=0)


def _dev(t):
    return (t // 4, (t // 2) % 2, t % 2)


def _my_index():
    return 4 * lax.axis_index("x") + 2 * lax.axis_index("y") + lax.axis_index("c")


def _comm_call(body, name, arrs, lead, n_remote):
    nw = len(arrs)
    any_spec = pl.BlockSpec(memory_space=pl.ANY)
    return pl.pallas_call(
        body, name=name, out_shape=[_sds((lead,) + a.shape[-2:], a.dtype) for a in arrs],
        in_specs=[any_spec] * nw, out_specs=[any_spec] * nw,
        scratch_shapes=[pltpu.SemaphoreType.DMA((n_remote * nw,)), pltpu.SemaphoreType.DMA((n_remote * nw,)),
                        pltpu.SemaphoreType.DMA((2 * nw,))] + [pltpu.VMEM(a.shape[-2:], a.dtype) for a in arrs],
        compiler_params=pltpu.CompilerParams(vmem_limit_bytes=VMEM_LIMIT),
    )(*arrs)


class _LocalCopy:
    def __init__(self, src, dst, buf, sem_in, sem_out):
        self.fetch = pltpu.make_async_copy(src, buf, sem_in)
        self.store = pltpu.make_async_copy(buf, dst, sem_out)
        self.fetch.start()

    def forward(self):
        self.fetch.wait()
        self.store.start()

    def finish(self):
        self.store.wait()


def _exchange8(gs, name, same):
    nw = len(gs)

    def body(*refs):
        g_refs, o_refs, (ssem, rsem, lsem), bufs = refs[:nw], refs[nw:2 * nw], refs[2 * nw:2 * nw + 3], refs[2 * nw + 3:]
        me = _my_index()
        locs, sends = [], []
        for i, (g_ref, o_ref) in enumerate(zip(g_refs, o_refs)):
            src = (lambda t, g_ref=g_ref: g_ref) if same else (lambda t, g_ref=g_ref: g_ref.at[t])
            locs.append(_LocalCopy(src(me), o_ref.at[me], bufs[i], lsem.at[2 * i], lsem.at[2 * i + 1]))
            for d in range(1, 8):
                t = (me + d) % 8
                cp = pltpu.make_async_remote_copy(src_ref=src(t), dst_ref=o_ref.at[me], send_sem=ssem.at[7 * i + d - 1],
                                                  recv_sem=rsem.at[7 * i + d - 1], device_id=_dev(t), device_id_type=MESH)
                cp.start()
                sends.append(cp)
        for loc in locs:
            loc.forward()
        for i, (g_ref, o_ref) in enumerate(zip(g_refs, o_refs)):
            src = (lambda t, g_ref=g_ref: g_ref) if same else (lambda t, g_ref=g_ref: g_ref.at[t])
            for d in range(1, 8):
                s = (me + 8 - d) % 8
                pltpu.make_async_remote_copy(src_ref=src(s), dst_ref=o_ref.at[s], send_sem=ssem.at[7 * i + d - 1],
                                             recv_sem=rsem.at[7 * i + d - 1], device_id=_dev(s),
                                             device_id_type=MESH).wait_recv()
        for cp in sends:
            cp.wait_send()
        for loc in locs:
            loc.finish()

    return _comm_call(body, name, gs, 8, 7)


def _exchange_chips(ws, name, gather):
    nw = len(ws)

    def body(*refs):
        w_refs, o_refs, (ssem, rsem, lsem), bufs = refs[:nw], refs[nw:2 * nw], refs[2 * nw:2 * nw + 3], refs[2 * nw + 3:]
        x, y, cc = lax.axis_index("x"), lax.axis_index("y"), lax.axis_index("c")
        k = 2 * x + y
        peers = [(1 - x, y), (x, 1 - y), (1 - x, 1 - y)]
        locs, sends = [], []
        for i, (w_ref, o_ref) in enumerate(zip(w_refs, o_refs)):
            if gather:
                locs.append(_LocalCopy(w_ref.at[cc], o_ref.at[k], bufs[i], lsem.at[2 * i], lsem.at[2 * i + 1]))
            for j, (px, py) in enumerate(peers):
                src, dst = (w_ref.at[cc], o_ref.at[k]) if gather else (w_ref.at[2 * px + py], o_ref.at[j])
                cp = pltpu.make_async_remote_copy(src_ref=src, dst_ref=dst, send_sem=ssem.at[3 * i + j],
                                                  recv_sem=rsem.at[3 * i + j], device_id=(px, py, cc), device_id_type=MESH)
                cp.start()
                sends.append(cp)
        for loc in locs:
            loc.forward()
        for i, (w_ref, o_ref) in enumerate(zip(w_refs, o_refs)):
            for j, (px, py) in enumerate(peers):
                src, dst = (w_ref.at[cc], o_ref.at[2 * px + py]) if gather else (w_ref.at[k], o_ref.at[j])
                pltpu.make_async_remote_copy(src_ref=src, dst_ref=dst, send_sem=ssem.at[3 * i + j],
                                             recv_sem=rsem.at[3 * i + j], device_id=(px, py, cc),
                                             device_id_type=MESH).wait_recv()
        for cp in sends:
            cp.wait_send()
        for loc in locs:
            loc.finish()

    return _comm_call(body, name, ws, 4 if gather else 3, 3)


def _sibling_send(hs, name):
    nw = len(hs)

    def body(*refs):
        h_refs, o_refs, (ssem, rsem, lsem) = refs[:nw], refs[nw:2 * nw], refs[2 * nw:]
        x, y, cc = lax.axis_index("x"), lax.axis_index("y"), lax.axis_index("c")
        sends = []
        for i, (h_ref, o_ref) in enumerate(zip(h_refs, o_refs)):
            cp = pltpu.make_async_remote_copy(src_ref=h_ref.at[1 - cc], dst_ref=o_ref, send_sem=ssem.at[i],
                                              recv_sem=rsem.at[i], device_id=(x, y, 1 - cc), device_id_type=MESH)
            cp.start()
            sends.append(cp)
        for i, (h_ref, o_ref) in enumerate(zip(h_refs, o_refs)):
            pltpu.make_async_remote_copy(src_ref=h_ref.at[cc], dst_ref=o_ref, send_sem=ssem.at[i], recv_sem=rsem.at[i],
                                         device_id=(x, y, 1 - cc), device_id_type=MESH).wait_recv()
        for cp in sends:
            cp.wait_send()

    nw_spec = pl.BlockSpec(memory_space=pl.ANY)
    return pl.pallas_call(
        body, name=name, out_shape=[_sds(h.shape[1:], h.dtype) for h in hs],
        in_specs=[nw_spec] * nw, out_specs=[nw_spec] * nw,
        scratch_shapes=[pltpu.SemaphoreType.DMA((nw,)), pltpu.SemaphoreType.DMA((nw,)), pltpu.SemaphoreType.DMA((nw,))],
    )(*hs)


def _sibling_exchange(hs, name):
    nw = len(hs)

    def body(*refs):
        h_refs, o_refs, (ssem, rsem, lsem), bufs = refs[:nw], refs[nw:2 * nw], refs[2 * nw:2 * nw + 3], refs[2 * nw + 3:]
        x, y, cc = lax.axis_index("x"), lax.axis_index("y"), lax.axis_index("c")
        locs, sends = [], []
        for i, (h_ref, o_ref) in enumerate(zip(h_refs, o_refs)):
            locs.append(_LocalCopy(h_ref, o_ref.at[cc], bufs[i], lsem.at[2 * i], lsem.at[2 * i + 1]))
            cp = pltpu.make_async_remote_copy(src_ref=h_ref, dst_ref=o_ref.at[cc], send_sem=ssem.at[i], recv_sem=rsem.at[i],
                                              device_id=(x, y, 1 - cc), device_id_type=MESH)
            cp.start()
            sends.append(cp)
        for loc in locs:
            loc.forward()
        for i, (h_ref, o_ref) in enumerate(zip(h_refs, o_refs)):
            pltpu.make_async_remote_copy(src_ref=h_ref, dst_ref=o_ref.at[1 - cc], send_sem=ssem.at[i], recv_sem=rsem.at[i],
                                         device_id=(x, y, 1 - cc), device_id_type=MESH).wait_recv()
        for cp in sends:
            cp.wait_send()
        for loc in locs:
            loc.finish()

    return _comm_call(body, name, hs, 2, 1)


def _sum8(buf, name):
    _, r, c = buf.shape
    tr = _pick(r, (256, 128, 64, 32, 16, 8))
    flat = buf.reshape(8 * r, c)

    def body(*v):
        acc = v[0]
        for t in v[1:]:
            acc = acc + t
        return acc

    return _rowwise(body, name=name, nblk=r // tr, tr=tr, rows=[(flat, 0, c, s * r) for s in range(8)],
                    outs=[(c, F32)])[0]


def _pack(arrs, rows_mult=16):
    flat = jnp.concatenate([a.reshape(-1).astype(F32) for a in arrs])
    nel = flat.shape[0]
    r = -(-nel // PACK_W)
    r = -(-r // rows_mult) * rows_mult
    return jnp.pad(flat, (0, r * PACK_W - nel)).reshape(r, PACK_W)


def _unpack(buf, shapes):
    flat = buf.reshape(-1)
    out, o = [], 0
    for s in shapes:
        nel = int(np.prod(s))
        out.append(flat[o:o + nel].reshape(s))
        o += nel
    return out


def _adamw(g, w, m, v, name):
    r, wd = g.shape
    tr = _pick(r, tuple(t for t in (256, 128, 64, 32, 16, 8) if t * wd <= 262144) or (8,))
    c1 = 1.0 / (1.0 - ADAM_B1 ** ADAM_STEP)
    c2 = 1.0 / (1.0 - ADAM_B2 ** ADAM_STEP)

    def body(gv, wv, mv, vv):
        mn = ADAM_B1 * mv + (1.0 - ADAM_B1) * gv
        vn = ADAM_B2 * vv + (1.0 - ADAM_B2) * (gv * gv)
        delta = -ADAM_LR * ((mn * c1) / (jnp.sqrt(vn * c2) + ADAM_EPS) + ADAM_WD * wv)
        return delta, mn, vn

    return _rowwise(body, name=name, nblk=r // tr, tr=tr, rows=[(a, 0, wd, 0) for a in (g, w, m, v)],
                    outs=[(wd, F32)] * 3)


def kernel(x, c, ctx, c_ctx, w_mod, b_mod, norm1_g, norm2_g, w_in, q_a_g, w_uq, kv_a_g, w_ukv, q_norm_g, k_norm_g, w_o_attn, lam_re_f, lam_im_f, log_dt_f, c_re_f, c_im_f, lam_re_b, lam_im_b, log_dt_b, c_re_b, c_im_b, b_re, b_im, d_skip, w_glu, w_out, w_up, conv_w, conv_b, w_down, loss_target, m_c_ctx, m_w_mod, m_b_mod, m_norm1_g, m_norm2_g, m_w_in, m_q_a_g, m_w_uq, m_kv_a_g, m_w_ukv, m_q_norm_g, m_k_norm_g, m_w_o_attn, m_lam_re_f, m_lam_im_f, m_log_dt_f, m_c_re_f, m_c_im_f, m_lam_re_b, m_lam_im_b, m_log_dt_b, m_c_re_b, m_c_im_b, m_b_re, m_b_im, m_d_skip, m_w_glu, m_w_out, m_w_up, m_conv_w, m_conv_b, m_w_down, v_c_ctx, v_w_mod, v_b_mod, v_norm1_g, v_norm2_g, v_w_in, v_q_a_g, v_w_uq, v_kv_a_g, v_w_ukv, v_q_norm_g, v_k_norm_g, v_w_o_attn, v_lam_re_f, v_lam_im_f, v_log_dt_f, v_c_re_f, v_c_im_f, v_lam_re_b, v_lam_im_b, v_log_dt_b, v_c_re_b, v_c_im_b, v_b_re, v_b_im, v_d_skip, v_w_glu, v_w_out, v_w_up, v_conv_w, v_conv_b, v_w_down):
    weights = dict(c_ctx=c_ctx, w_mod=w_mod, b_mod=b_mod, norm1_g=norm1_g, norm2_g=norm2_g, w_in=w_in, q_a_g=q_a_g, w_uq=w_uq, kv_a_g=kv_a_g, w_ukv=w_ukv, q_norm_g=q_norm_g, k_norm_g=k_norm_g, w_o_attn=w_o_attn, lam_re_f=lam_re_f, lam_im_f=lam_im_f, log_dt_f=log_dt_f, c_re_f=c_re_f, c_im_f=c_im_f, lam_re_b=lam_re_b, lam_im_b=lam_im_b, log_dt_b=log_dt_b, c_re_b=c_re_b, c_im_b=c_im_b, b_re=b_re, b_im=b_im, d_skip=d_skip, w_glu=w_glu, w_out=w_out, w_up=w_up, conv_w=conv_w, conv_b=conv_b, w_down=w_down)
    mom_m = dict(c_ctx=m_c_ctx, w_mod=m_w_mod, b_mod=m_b_mod, norm1_g=m_norm1_g, norm2_g=m_norm2_g, w_in=m_w_in, q_a_g=m_q_a_g, w_uq=m_w_uq, kv_a_g=m_kv_a_g, w_ukv=m_w_ukv, q_norm_g=m_q_norm_g, k_norm_g=m_k_norm_g, w_o_attn=m_w_o_attn, lam_re_f=m_lam_re_f, lam_im_f=m_lam_im_f, log_dt_f=m_log_dt_f, c_re_f=m_c_re_f, c_im_f=m_c_im_f, lam_re_b=m_lam_re_b, lam_im_b=m_lam_im_b, log_dt_b=m_log_dt_b, c_re_b=m_c_re_b, c_im_b=m_c_im_b, b_re=m_b_re, b_im=m_b_im, d_skip=m_d_skip, w_glu=m_w_glu, w_out=m_w_out, w_up=m_w_up, conv_w=m_conv_w, conv_b=m_conv_b, w_down=m_w_down)
    mom_v = dict(c_ctx=v_c_ctx, w_mod=v_w_mod, b_mod=v_b_mod, norm1_g=v_norm1_g, norm2_g=v_norm2_g, w_in=v_w_in, q_a_g=v_q_a_g, w_uq=v_w_uq, kv_a_g=v_kv_a_g, w_ukv=v_w_ukv, q_norm_g=v_q_norm_g, k_norm_g=v_k_norm_g, w_o_attn=v_w_o_attn, lam_re_f=v_lam_re_f, lam_im_f=v_lam_im_f, log_dt_f=v_log_dt_f, c_re_f=v_c_re_f, c_im_f=v_c_im_f, lam_re_b=v_lam_re_b, lam_im_b=v_lam_im_b, log_dt_b=v_log_dt_b, c_re_b=v_c_re_b, c_im_b=v_c_im_b, b_re=v_b_re, b_im=v_b_im, d_skip=v_d_skip, w_glu=v_w_glu, w_out=v_w_out, w_up=v_w_up, conv_w=v_conv_w, conv_b=v_conv_b, w_down=v_w_down)
    names = list(weights)

    nl, d = x.shape[1], x.shape[2]
    nc = ctx.shape[1]
    n = nl + nc
    f2 = conv_b.shape[1]
    fh = f2 // 2
    d6 = b_mod.shape[1]
    mx, my, mc = lax.axis_index("x"), lax.axis_index("y"), lax.axis_index("c")
    chip = 2 * mx + my
    me = 4 * mx + 2 * my + mc
    tr = _pick(math.gcd(nl, nc), (256, 128, 64, 32, 16))
    nlb, nb = nl // tr, n // tr

    big_names = ["w_in", "w_uq", "w_ukv", "w_o_attn", "w_glu", "w_out", "w_up", "w_down"]
    row_sharded = ("w_out", "w_down")
    halves_in = [weights[k][0].astype(MXU_DTYPE).reshape(2, weights[k].shape[1] // 2, weights[k].shape[2]) for k in big_names]
    my_halves = _exchange_chips(halves_in, "gather_weights", True)
    gathered = _sibling_exchange([t.reshape(-1, t.shape[2]) for t in my_halves], "gather_weight_halves")
    full = {}
    for k_, gth in zip(big_names, gathered):
        r_, c_ = weights[k_].shape[1:]
        g4 = gth.reshape(2, 4, r_ // 2, c_)
        full[k_] = (jnp.transpose(g4, (1, 0, 2, 3)).reshape(4 * r_, c_) if k_ in row_sharded
                    else jnp.transpose(g4, (0, 2, 1, 3)).reshape(r_, 4 * c_))

    cwid = conv_w.shape[2]
    sw = -(-max(d, cwid) // 128) * 128
    small_in = jnp.concatenate([jnp.pad(c, ((0, 0), (0, sw - d))), jnp.pad(conv_w[0], ((0, 4), (0, sw - cwid)))], axis=0)
    (small_all,) = _exchange8([small_in], "gather_c", True)
    cs = small_all[:, 0, :d]
    conv_w_full = jnp.concatenate([small_all[2 * j, 1:4, :cwid] for j in range(4)], axis=1)
    cs16 = jnp.concatenate([cs, c_ctx[None, :], jnp.zeros((7, d), F32)], axis=0)

    csh = w_mod.shape[2]
    b_mod_sh = lax.dynamic_slice(b_mod, (0, chip * csh), (1, csh))

    def mod_fwd_body(c_ref, w_ref, b_ref, o_ref):
        a = _silu(c_ref[...]).astype(MXU_DTYPE)
        o_ref[...] = jnp.dot(a, w_ref[...].astype(MXU_DTYPE), preferred_element_type=F32) + b_ref[...]

    mod_sh = pl.pallas_call(mod_fwd_body, name="mod_fwd", out_shape=_sds((16, csh), F32),
                            compiler_params=pltpu.CompilerParams(vmem_limit_bytes=VMEM_LIMIT))(cs16, w_mod[0], b_mod_sh)
    (mod_all,) = _exchange8([mod_sh], "gather_mod", True)
    mod_full = jnp.concatenate([mod_all[2 * j] for j in range(4)], axis=1)
    modv = jnp.stack([lax.dynamic_slice(mod_full, (me, 0), (1, d6)), mod_full[8:9]])

    def mod_parts(m):
        return [m[:, j * d:(j + 1) * d] for j in range(6)]

    u_off, kv_off, kr_off = 2 * d, 2 * d + SSM_WIDTH, 2 * d + SSM_WIDTH + KV_LORA
    q_off = -(-(kr_off + SLOT) // Q_LORA) * Q_LORA
    zw = q_off + Q_LORA
    wi = full["w_in"]
    s0, s1, s2, s3 = Q_LORA, Q_LORA + KV_LORA, Q_LORA + KV_LORA + QK_ROPE, Q_LORA + KV_LORA + QK_ROPE + SSM_WIDTH
    zpad = lambda w_: jnp.zeros((d, w_), MXU_DTYPE)
    win_p = jnp.concatenate([wi[:, s3:], wi[:, s2:s3], wi[:, s0:s1], wi[:, s1:s2], zpad(SLOT - QK_ROPE),
                             zpad(q_off - kr_off - SLOT), wi[:, :s0]], axis=1)
    wuq_p = jnp.pad(full["w_uq"].reshape(Q_LORA, N_HEADS, QK_DIM), ((0, 0), (0, 0), (0, SLOT - QK_DIM))).reshape(Q_LORA, N_HEADS * SLOT)
    wukv3 = full["w_ukv"].reshape(KV_LORA, N_HEADS, QK_NOPE + V_DIM)
    padh = lambda t: jnp.pad(t, ((0, 0), (0, 0), (0, SLOT - t.shape[2]))).reshape(t.shape[0], N_HEADS * SLOT)
    wukv_p = jnp.concatenate([padh(wukv3[:, :, :QK_NOPE]), padh(wukv3[:, :, QK_NOPE:])], axis=1)
    wo_p = jnp.pad(full["w_o_attn"].reshape(N_HEADS, V_DIM, d), ((0, 0), (0, SLOT - V_DIM), (0, 0))).reshape(N_HEADS * SLOT, d)
    wglu, wout, wup, wdown = full["w_glu"], full["w_out"], full["w_up"], full["w_down"]
    hw = N_HEADS * SLOT
    gain_p = lambda g_: jnp.tile(jnp.pad(g_[0], (0, SLOT - QK_DIM)), N_HEADS)[None, :]
    qg_p, kg_p = gain_p(q_norm_g), gain_p(k_norm_g)

    tok = jnp.arange(nl)
    freqs = ROPE_THETA ** (-jnp.arange(QK_ROPE // 4, dtype=F32) / (QK_ROPE // 4))
    ang = jnp.concatenate([(tok // GRID_W)[:, None] * freqs, (tok % GRID_W)[:, None] * freqs], axis=-1)
    cos_t = jnp.concatenate([jnp.cos(ang), jnp.ones((nc, 16), F32)], axis=0)
    sin_t = jnp.concatenate([jnp.sin(ang), jnp.zeros((nc, 16), F32)], axis=0)
    zl = lambda w_: jnp.zeros((n, w_), F32)
    rope_c = jnp.concatenate([jnp.ones((n, QK_NOPE), F32), cos_t, cos_t, zl(SLOT - QK_DIM)], axis=1)
    rope_sa = jnp.concatenate([zl(QK_NOPE), -sin_t, zl(SLOT - QK_NOPE - 16)], axis=1)
    rope_sb = jnp.concatenate([zl(QK_NOPE + 16), sin_t, zl(SLOT - QK_DIM)], axis=1)

    dirs = (("f", lam_re_f, lam_im_f, log_dt_f, c_re_f, c_im_f, False), ("b", lam_re_b, lam_im_b, log_dt_b, c_re_b, c_im_b, True))
    bbd, cbd_t, cbd_n, bbd_t, lamc, lamc_adj, disc_vjps = [], [], [], [], [], [], []
    for _, l_re, l_im, l_dt, cr_, ci_, rev_ in dirs:
        (lbr, lbi, bbr, bbi), vjp = jax.vjp(_s5_disc, l_re[0], l_im[0], l_dt[0], b_re[0], b_im[0])
        disc_vjps.append(vjp)
        bb = _diag_blocks(jnp.transpose(bbr, (0, 2, 1)), jnp.transpose(bbi, (0, 2, 1))).astype(MXU_DTYPE)
        cc_ = _diag_blocks(cr_[0], -ci_[0]).astype(MXU_DTYPE)
        bbd.append(bb)
        bbd_t.append(_block_transpose(bb))
        cbd_t.append(cc_)
        cbd_n.append(_block_transpose(cc_))
        lamc.append(_lam_consts(lbr, lbi, rev_, False))
        lamc_adj.append(_lam_consts(lbr, lbi, not rev_, True))
    t_scan = tr

    xa = jnp.concatenate([x[0], ctx[0]], axis=0)
    n1g, n2g = norm1_g, norm2_g

    def norm1_body(xv, m, g):
        sh1, sc1 = m[:, :d], m[:, d:2 * d]
        return _rms_fwd(xv, g, d) * (1.0 + sc1) + sh1

    (h1,) = _rowwise(norm1_body, name="norm1_fwd", nblk=nb, tr=tr, rows=[(xa, 0, d, 0)], sels=[modv], fulls=[n1g],
                     outs=[(d, MXU_DTYPE)], seg=nlb)
    z = _mm(h1, win_p, "nn", "in_proj")
    gl_cb, u_cb, kv_cb, kr_cb, q_cb = 0, u_off // SSM_WIDTH, kv_off // KV_LORA, kr_off // SLOT, q_off // Q_LORA

    (cqn,) = _rowwise(lambda v, g: _rms_fwd(v, g, Q_LORA), name="qa_norm_fwd", nblk=nlb, tr=tr,
                      rows=[(z, q_cb, Q_LORA, 0)], fulls=[q_a_g], outs=[(Q_LORA, MXU_DTYPE)])
    qh = _mm(cqn, wuq_p, "nn", "q_up")

    def qhead_body(qv, cv, sav, sbv, g):
        return jnp.concatenate([_rope_fwd(_rms_fwd(t, g[:, :SLOT], QK_DIM), cv, sav, sbv) for t in _heads(qv)], axis=1)

    rope_rows = lambda: [(rope_c, 0, SLOT, 0), (rope_sa, 0, SLOT, 0), (rope_sb, 0, SLOT, 0)]
    (q_p,) = _rowwise(qhead_body, name="q_head_fwd", nblk=nlb, tr=tr, rows=[(qh, 0, hw, 0)] + rope_rows(),
                      fulls=[qg_p], outs=[(hw, MXU_DTYPE)])

    (ckvn,) = _rowwise(lambda v, g: _rms_fwd(v, g, KV_LORA), name="kva_norm_fwd", nblk=nb, tr=tr,
                       rows=[(z, kv_cb, KV_LORA, 0)], fulls=[kv_a_g], outs=[(KV_LORA, MXU_DTYPE)])
    kvpre = _mm(ckvn, wukv_p, "nn", "kv_up")

    def khead_body(kv_, vv_, krv, cv, sav, sbv, g):
        kpe = pltpu.roll(krv, QK_NOPE, 1)
        ks = [_rope_fwd(_rms_fwd(t + kpe, g[:, :SLOT], QK_DIM), cv, sav, sbv) for t in _heads(kv_)]
        return jnp.concatenate(ks, axis=1), vv_

    k_p, v_p = _rowwise(khead_body, name="k_head_fwd", nblk=nb, tr=tr,
                        rows=[(kvpre, 0, hw, 0), (kvpre, 1, hw, 0), (z, kr_cb, SLOT, 0)] + rope_rows(),
                        fulls=[kg_p], outs=[(hw, MXU_DTYPE), (hw, MXU_DTYPE)])

    scale = QK_DIM ** -0.5
    o_p, lse = _attn_fwd(q_p, k_p, v_p, nl, scale)
    a_l = _mm(o_p, wo_p, "nn", "attn_out")

    scans = [_s5_scan(z, u_cb, bbd[j], cbd_n[j], lamc[j], t_scan, nl, dirs[j][6], "s5_scan_" + dirs[j][0]) for j in range(2)]
    xs, ydir = [s_[0] for s_ in scans], [s_[1] for s_ in scans]

    def ssm_out_body(uv, a, b, dsk):
        ys = uv * dsk + a + b
        return ys, _gelu(ys)

    ys, ge = _rowwise(ssm_out_body, name="s5_out_fwd", nblk=nlb, tr=tr,
                      rows=[(z, u_cb, SSM_WIDTH, 0), (ydir[0], 0, SSM_WIDTH, 0), (ydir[1], 0, SSM_WIDTH, 0)],
                      fulls=[d_skip], outs=[(SSM_WIDTH, F32), (SSM_WIDTH, MXU_DTYPE)])
    glu_out = _mm(ge, wglu, "nn", "glu_proj")

    def merge_body(ga, gs, av, val, gate):
        return _sigmoid(ga) * av + _sigmoid(gs) * (val * _sigmoid(gate))

    merge_rows = lambda: [(z, 0, d, 0), (z, 1, d, 0), (a_l, 0, d, 0), (glu_out, 0, d, 0), (glu_out, 1, d, 0)]
    (merged,) = _rowwise(merge_body, name="merge_fwd", nblk=nlb, tr=tr, rows=merge_rows(), outs=[(d, MXU_DTYPE)])
    mo = _mm(merged, wout, "nn", "out_proj")
    mod_x = modv[0]

    def norm2_body(xv, mov, m, g):
        g1, sh2, sc2 = m[:, 2 * d:3 * d], m[:, 3 * d:4 * d], m[:, 4 * d:5 * d]
        x1v = xv + g1 * mov
        return x1v, _rms_fwd(x1v, g, d) * (1.0 + sc2) + sh2

    x1, h2 = _rowwise(norm2_body, name="norm2_fwd", nblk=nlb, tr=tr, rows=[(xa, 0, d, 0), (mo, 0, d, 0)],
                      fulls=[mod_x, n2g], outs=[(d, F32), (d, MXU_DTYPE)])
    up = _mm(h2, wup, "nn", "ffn_up")
    cw8 = jnp.zeros((8, f2), F32).at[:3].set(conv_w_full)

    def conv3(t3, w8, off):
        p_, c_, n_ = t3
        return p_ * w8[0:1, off:off + fh] + c_ * w8[1:2, off:off + fh] + n_ * w8[2:3, off:off + fh]

    def conv_fwd_body(val3, gate3, w8, bias):
        val2 = conv3(val3, w8, 0) + bias[:, :fh]
        gate2 = conv3(gate3, w8, fh) + bias[:, fh:]
        return _silu(gate2) * val2

    (act,) = _rowwise(conv_fwd_body, name="conv_fwd", nblk=nlb, tr=tr, rows=[(up, 0, fh, 0), (up, 1, fh, 0)],
                      halo=(0, 1), fulls=[cw8, conv_b], outs=[(fh, MXU_DTYPE)])
    dn = _mm(act, wdown, "nn", "ffn_down")
    tgt = loss_target[0]

    def loss_body(x1v, dnv, tv, m):
        g2 = m[:, 5 * d:6 * d]
        e = x1v + g2 * dnv - tv
        dx2v = e * (1.0 / d)
        return dx2v, dx2v * g2, e * e, dx2v * dnv

    dx2, ddn, loss_acc, dg2_acc = _rowwise(loss_body, name="loss", nblk=nlb, tr=tr,
                                           rows=[(x1, 0, d, 0), (dn, 0, d, 0), (tgt, 0, d, 0)], fulls=[mod_x],
                                           outs=[(d, F32), (d, MXU_DTYPE)], accs=[d, d])
    loss = lax.psum(0.5 / d * jnp.sum(loss_acc), ("x", "y", "c"))

    g_big = {}
    dact = _mm(ddn, wdown, "nt", "ffn_down_dx")
    g_big["w_down"] = _mm(act, ddn, "tn", "ffn_down_dw")

    def conv_bwd_body(val3, gate3, da, w8, bias):
        val2 = conv3(val3, w8, 0) + bias[:, :fh]
        gate2 = conv3(gate3, w8, fh) + bias[:, fh:]
        sg = _sigmoid(gate2)
        dval2 = da * (gate2 * sg)
        dgate2 = da * val2 * (sg * (1.0 + gate2 * (1.0 - sg)))
        du2 = jnp.concatenate([dval2, dgate2], axis=1)
        taps = [jnp.concatenate([dval2 * val3[j], dgate2 * gate3[j]], axis=1) for j in range(3)]
        return du2, du2, taps[0], taps[1], taps[2]

    du2, dcb_acc, dcw0, dcw1, dcw2 = _rowwise(conv_bwd_body, name="conv_bwd", nblk=nlb, tr=tr,
                                              rows=[(up, 0, fh, 0), (up, 1, fh, 0), (dact, 0, fh, 0)], halo=(0, 1),
                                              fulls=[cw8, conv_b], outs=[(f2, F32)], accs=[f2, f2, f2, f2])

    def conv_t_body(dval3, dgate3, w8):
        rev = lambda t3: (t3[2], t3[1], t3[0])
        return jnp.concatenate([conv3(rev(dval3), w8, 0), conv3(rev(dgate3), w8, fh)], axis=1)

    (dup,) = _rowwise(conv_t_body, name="conv_bwd_dx", nblk=nlb, tr=tr, rows=[(du2, 0, fh, 0), (du2, 1, fh, 0)],
                      halo=(0, 1), fulls=[cw8], outs=[(f2, MXU_DTYPE)])
    dh2 = _mm(dup, wup, "nt", "ffn_up_dx")
    g_big["w_up"] = _mm(h2, dup, "tn", "ffn_up_dw")

    def norm2_bwd_body(x1v, dh, dx2v, mov, m, g):
        g1, sc2 = m[:, 2 * d:3 * d], m[:, 4 * d:5 * d]
        y = _rms_fwd(x1v, g, d)
        dxn, dgc = _rms_bwd(x1v, g, dh * (1.0 + sc2), d)
        dx1v = dx2v + dxn
        return dx1v, dx1v * g1, dgc, dh, dh * y, dx1v * mov

    dx1, dmo, dn2g_acc, dsh2_acc, dsc2_acc, dg1_acc = _rowwise(
        norm2_bwd_body, name="norm2_bwd", nblk=nlb, tr=tr,
        rows=[(x1, 0, d, 0), (dh2, 0, d, 0), (dx2, 0, d, 0), (mo, 0, d, 0)], fulls=[mod_x, n2g],
        outs=[(d, F32), (d, MXU_DTYPE)], accs=[d, d, d, d])
    dmerged = _mm(dmo, wout, "nt", "out_proj_dx")
    g_big["w_out"] = _mm(merged, dmo, "tn", "out_proj_dw")

    def merge_bwd_body(ga, gs, av, val, gate, dm):
        sa_, ss_, sg_ = _sigmoid(ga), _sigmoid(gs), _sigmoid(gate)
        s_l = val * sg_
        ds_l = dm * ss_
        dga = dm * av * sa_ * (1.0 - sa_)
        dgs = dm * s_l * ss_ * (1.0 - ss_)
        dval = ds_l * sg_
        dgate = ds_l * val * sg_ * (1.0 - sg_)
        return dm * sa_, jnp.concatenate([dval, dgate], axis=1), jnp.concatenate([dga, dgs], axis=1)

    da_l, dglu, dgl = _rowwise(merge_bwd_body, name="merge_bwd", nblk=nlb, tr=tr,
                               rows=merge_rows() + [(dmerged, 0, d, 0)],
                               outs=[(d, MXU_DTYPE), (2 * d, MXU_DTYPE), (2 * d, MXU_DTYPE)])
    dge = _mm(dglu, wglu, "nt", "glu_proj_dx")
    g_big["w_glu"] = _mm(ge, dglu, "tn", "glu_proj_dw")

    def ssm_out_bwd_body(ysv, dgev, uv, dsk):
        dys_ = dgev * _dgelu(ysv)
        return dys_, dys_ * dsk, dys_ * uv

    dys, du_skip, ddskip_acc = _rowwise(ssm_out_bwd_body, name="s5_out_bwd", nblk=nlb, tr=tr,
                                        rows=[(ys, 0, SSM_WIDTH, 0), (dge, 0, SSM_WIDTH, 0), (z, u_cb, SSM_WIDTH, 0)],
                                        fulls=[d_skip], outs=[(SSM_WIDTH, F32), (SSM_WIDTH, F32)], accs=[SSM_WIDTH])
    s5b = [_s5_bwd(dys, z, u_cb, xs[j], cbd_t[j], bbd_t[j], lamc_adj[j], t_scan, nl, dirs[j][6], "s5_bwd_" + dirs[j][0])
           for j in range(2)]
    du_nat = s5b[0][0] + s5b[1][0] + jnp.concatenate([du_skip, jnp.zeros((nc, SSM_WIDTH), F32)], axis=0)

    do_f = _mm(da_l, wo_p, "nt", "attn_out_dx")
    g_wo_p = _mm(o_p, da_l, "tn", "attn_out_dw")

    dq_p, dk_p, dv_p = _attn_bwd(q_p, k_p, v_p, do_f, o_p, lse, nl, scale)

    def qhead_bwd_body(qv, dqv, cv, sav, sbv, g):
        dxs, dgs = [], []
        for t, dt_ in zip(_heads(qv), _heads(dqv)):
            dx_, dg_ = _rms_bwd(t, g[:, :SLOT], _rope_bwd(dt_, cv, sav, sbv), QK_DIM)
            dxs.append(dx_)
            dgs.append(dg_)
        return jnp.concatenate(dxs, axis=1), jnp.concatenate(dgs, axis=1)

    dqh, dqg_acc = _rowwise(qhead_bwd_body, name="q_head_bwd", nblk=nlb, tr=tr,
                            rows=[(qh, 0, hw, 0), (dq_p, 0, hw, 0)] + rope_rows(), fulls=[qg_p],
                            outs=[(hw, MXU_DTYPE)], accs=[hw])
    dcqn = _mm(dqh, wuq_p, "nt", "q_up_dx")
    g_wuq_p = _mm(cqn, dqh, "tn", "q_up_dw")
    dcq, dqag_acc = _rowwise(lambda v, dy, g: _rms_bwd(v, g, dy, Q_LORA), name="qa_norm_bwd", nblk=nlb, tr=tr,
                             rows=[(z, q_cb, Q_LORA, 0), (dcqn, 0, Q_LORA, 0)], fulls=[q_a_g],
                             outs=[(Q_LORA, MXU_DTYPE)], accs=[Q_LORA])

    def khead_bwd_body(kv_, krv, dkv_, dvv_, cv, sav, sbv, g):
        kpe = pltpu.roll(krv, QK_NOPE, 1)
        lane = lax.broadcasted_iota(jnp.int32, krv.shape, 1)
        dxs, dgs, dkr_ = [], [], jnp.zeros(krv.shape, F32)
        for t, dt_ in zip(_heads(kv_), _heads(dkv_)):
            dx_, dg_ = _rms_bwd(t + kpe, g[:, :SLOT], _rope_bwd(dt_, cv, sav, sbv), QK_DIM)
            dxs.append(jnp.where(lane < QK_NOPE, dx_, 0.0))
            dgs.append(dg_)
            dkr_ = dkr_ + dx_
        dkr_ = jnp.where(lane < QK_ROPE, pltpu.roll(dkr_, SLOT - QK_NOPE, 1), 0.0)
        return jnp.concatenate(dxs + [dvv_], axis=1), dkr_, jnp.concatenate(dgs, axis=1)

    dkvpre, dkr, dkg_acc = _rowwise(khead_bwd_body, name="k_head_bwd", nblk=nb, tr=tr,
                                    rows=[(kvpre, 0, hw, 0), (z, kr_cb, SLOT, 0), (dk_p, 0, hw, 0), (dv_p, 0, hw, 0)] + rope_rows(),
                                    fulls=[kg_p], outs=[(2 * hw, MXU_DTYPE), (SLOT, MXU_DTYPE)], accs=[hw])
    dckvn = _mm(dkvpre, wukv_p, "nt", "kv_up_dx")
    g_wukv_p = _mm(ckvn, dkvpre, "tn", "kv_up_dw")
    dckv, dkvag_acc = _rowwise(lambda v, dy, g: _rms_bwd(v, g, dy, KV_LORA), name="kva_norm_bwd", nblk=nb, tr=tr,
                               rows=[(z, kv_cb, KV_LORA, 0), (dckvn, 0, KV_LORA, 0)], fulls=[kv_a_g],
                               outs=[(KV_LORA, MXU_DTYPE)], accs=[KV_LORA])

    padc = lambda t: jnp.concatenate([t, jnp.zeros((nc, t.shape[1]), t.dtype)], axis=0)
    dz = jnp.concatenate([padc(dgl), du_nat.astype(MXU_DTYPE), dckv, dkr,
                          jnp.zeros((n, q_off - kr_off - SLOT), MXU_DTYPE), padc(dcq)], axis=1)
    dh1 = _mm(dz, win_p, "nt", "in_proj_dx")
    g_win_p = _mm(h1, dz, "tn", "in_proj_dw")

    def norm1_bwd_body(xv, dh, dx1v, m, g):
        sc1 = m[:, d:2 * d]
        y = _rms_fwd(xv, g, d)
        dxn, dgc = _rms_bwd(xv, g, dh * (1.0 + sc1), d)
        return dxn + dx1v, dgc, dh, dh * y

    dxa, dn1g_acc, dsh1_acc, dsc1_acc = _rowwise(norm1_bwd_body, name="norm1_bwd", nblk=nb, tr=tr,
                                                 rows=[(xa, 0, d, 0), (dh1, 0, d, 0), (dx1, 0, d, 0)], sels=[modv],
                                                 fulls=[n1g], outs=[(d, F32)], accs=[d, d, d], seg=nlb)
    grad_x = dxa[:nl][None]

    red8 = lambda a: jnp.sum(a, axis=-2)
    dmod_own = jnp.concatenate([red8(dsh1_acc[0]), red8(dsc1_acc[0]), red8(dg1_acc), red8(dsh2_acc), red8(dsc2_acc), red8(dg2_acc)])
    dmod_ctx = jnp.concatenate([red8(dsh1_acc[1]), red8(dsc1_acc[1]), jnp.zeros((4 * d,), F32)])
    dm_in = jnp.concatenate([dmod_own[None, :], dmod_ctx[None, :], jnp.zeros((6, d6), F32)], axis=0)
    (dm_all,) = _exchange8([dm_in], "gather_dmod", True)
    dm_own_sh = lax.dynamic_slice(dm_all[:, 0, :], (0, chip * csh), (8, csh))
    dm_ctx_sh = lax.dynamic_slice(dm_all[:, 1, :], (0, chip * csh), (8, csh))

    def mod_bwd_body(c_ref, own_ref, ctx_ref, w_ref, gw_ref, gb_ref, gc_ref):
        cv = c_ref[...]
        a = _silu(cv).astype(MXU_DTYPE)
        own = own_ref[...]
        ctx_tot = ctx_ref[0:1, :]
        for j in range(1, 8):
            ctx_tot = ctx_tot + ctx_ref[j:j + 1, :]
        g16 = jnp.concatenate([own, jnp.broadcast_to(ctx_tot, own.shape)], axis=0)
        rid = lax.broadcasted_iota(jnp.int32, g16.shape, 0)
        g16 = jnp.where(rid <= 8, g16, 0.0)
        gw_ref[...] = lax.dot_general(a, g16.astype(MXU_DTYPE), (((0,), (0,)), ((), ())), preferred_element_type=F32)
        gb_ref[...] = jnp.broadcast_to(jnp.sum(own, axis=0, keepdims=True) + ctx_tot, gb_ref.shape)
        gc = lax.dot_general(jnp.broadcast_to(ctx_tot, own.shape).astype(MXU_DTYPE), w_ref[...].astype(MXU_DTYPE),
                             (((1,), (1,)), ((), ())), preferred_element_type=F32)
        gc_ref[...] = gc * _dsilu(cv[8:9, :])

    g_wmod, g_bmod_sh, g_cctx_part = pl.pallas_call(
        mod_bwd_body, name="mod_bwd", out_shape=[_sds((d, csh), F32), _sds((8, csh), F32), _sds((8, d), F32)],
        compiler_params=pltpu.CompilerParams(vmem_limit_bytes=VMEM_LIMIT))(cs16, dm_own_sh, dm_ctx_sh, w_mod[0])
    north = (mc == 0).astype(F32)
    g_bmod_part = lax.dynamic_update_slice(jnp.zeros((1, d6), F32), g_bmod_sh[0:1] * north, (0, chip * csh))
    g_cctx_part = g_cctx_part[0] * north

    small_g = {}
    for j, dr in enumerate(dirs):
        sfx = dr[0]
        _, dbbd_j, dcbd_j, dlam_j = s5b[j]
        dl = red8(dlam_j).reshape(N_CG, 2, CG_STATES)
        db_re, db_im = _diag_extract(dbbd_j)
        cot = (dl[:, 0].reshape(SSM_GROUPS, SSM_STATE), dl[:, 1].reshape(SSM_GROUPS, SSM_STATE),
               jnp.transpose(db_re, (0, 2, 1)), jnp.transpose(db_im, (0, 2, 1)))
        g_lre, g_lim, g_ldt, g_bre, g_bim = disc_vjps[j](cot)
        small_g["lam_re_" + sfx], small_g["lam_im_" + sfx], small_g["log_dt_" + sfx] = g_lre, g_lim, g_ldt
        small_g["b_re"] = small_g.get("b_re", 0.0) + g_bre
        small_g["b_im"] = small_g.get("b_im", 0.0) + g_bim
        dc_re, dc_im = _diag_extract(dcbd_j)
        small_g["c_re_" + sfx], small_g["c_im_" + sfx] = dc_re, -dc_im
    head_fold = lambda acc: jnp.sum(red8(acc).reshape(N_HEADS, SLOT), axis=0)[:QK_DIM]
    small_g.update(c_ctx=g_cctx_part, b_mod=g_bmod_part[0], norm1_g=red8(dn1g_acc[0]) + red8(dn1g_acc[1]),
                   norm2_g=red8(dn2g_acc), q_a_g=red8(dqag_acc), kv_a_g=red8(dkvag_acc), q_norm_g=head_fold(dqg_acc),
                   k_norm_g=head_fold(dkg_acc), d_skip=red8(ddskip_acc), conv_b=red8(dcb_acc))
    g_convw_full = jnp.stack([red8(dcw0), red8(dcw1), red8(dcw2)])
    small_names = ["c_ctx", "b_mod", "norm1_g", "norm2_g", "q_a_g", "kv_a_g", "q_norm_g", "k_norm_g",
                   "lam_re_f", "lam_im_f", "log_dt_f", "c_re_f", "c_im_f", "lam_re_b", "lam_im_b", "log_dt_b",
                   "c_re_b", "c_im_b", "b_re", "b_im", "d_skip", "conv_b"]
    small_shapes = [weights[k].shape for k in small_names]
    spack = _pack([small_g[k] for k in small_names] + [g_convw_full], rows_mult=8)
    sred = _sum8(_exchange8([spack], "gather_small_grads", True)[0], "sum_small_grads")
    sg_list = _unpack(sred, small_shapes + [(3, f2)])
    g_small = dict(zip(small_names, sg_list[:-1]))
    g_small["conv_w"] = lax.dynamic_slice(sg_list[-1], (0, chip * cwid), (3, cwid))[None]

    gwi = g_win_p
    g_big["w_in"] = jnp.concatenate([gwi[:, q_off:q_off + Q_LORA], gwi[:, kv_off:kv_off + KV_LORA],
                                     gwi[:, kr_off:kr_off + QK_ROPE], gwi[:, u_off:u_off + SSM_WIDTH], gwi[:, :2 * d]], axis=1)
    g_big["w_uq"] = g_wuq_p.reshape(Q_LORA, N_HEADS, SLOT)[:, :, :QK_DIM].reshape(Q_LORA, N_HEADS * QK_DIM)
    gk3 = g_wukv_p[:, :hw].reshape(KV_LORA, N_HEADS, SLOT)[:, :, :QK_NOPE]
    gv3 = g_wukv_p[:, hw:].reshape(KV_LORA, N_HEADS, SLOT)[:, :, :V_DIM]
    g_big["w_ukv"] = jnp.concatenate([gk3, gv3], axis=2).reshape(KV_LORA, N_HEADS * (QK_NOPE + V_DIM))
    g_big["w_o_attn"] = g_wo_p.reshape(N_HEADS, SLOT, d)[:, :V_DIM].reshape(N_HEADS * V_DIM, d)

    def pieces(k_):
        r_, c_ = weights[k_].shape[1:]
        if k_ in row_sharded:
            p4 = jnp.transpose(g_big[k_].reshape(4, 2, r_ // 2, c_), (1, 0, 2, 3))
        else:
            p4 = jnp.transpose(g_big[k_].reshape(2, r_ // 2, 4, c_), (0, 2, 1, 3))
        return p4.reshape(2, 2 * r_, c_)

    pcs = [pieces(k_) for k_ in big_names]
    from_sibling = _sibling_send(pcs, "grads_to_sibling")
    my_half = []
    for k_, p_, got in zip(big_names, pcs, from_sibling):
        rows4, c_ = got.shape
        rh = rows4 // 4
        own = lax.dynamic_index_in_dim(p_, mc, 0, keepdims=False)
        tr_ = _pick(rows4, (256, 128, 64, 32, 16))
        s32, sb = _rowwise(lambda a, b: (a + b, a + b), name="sum_chip_" + k_, nblk=rows4 // tr_, tr=tr_,
                           rows=[(own, 0, c_, 0), (got, 0, c_, 0)], outs=[(c_, F32), (c_, MXU_DTYPE)])
        my_half.append((s32, sb, rh, c_))
    recv3 = _exchange_chips([sb.reshape(4, rh, c_) for _, sb, rh, c_ in my_half], "scatter_weight_grads", False)
    reduced = []
    for k_, (s32, _, rh, c_), r3 in zip(big_names, my_half, recv3):
        mine = lax.dynamic_slice(s32, (chip * rh, 0), (rh, c_))
        tr_ = _pick(rh, (256, 128, 64, 32, 16))
        (red,) = _rowwise(lambda a, b0, b1, b2: a + b0 + b1 + b2, name="sum_grad_" + k_, nblk=rh // tr_, tr=tr_,
                          rows=[(mine, 0, c_, 0)] + [(r3.reshape(3 * rh, c_), 0, c_, j * rh) for j in range(3)],
                          outs=[(c_, F32)])
        reduced.append(red)
    both = _sibling_exchange(reduced, "exchange_halves")
    g_sh = {k_: b_.reshape((1,) + weights[k_].shape[1:]) for k_, b_ in zip(big_names, both)}
    g_sh["w_mod"] = g_wmod[None]

    grads = {**g_sh, **g_small}
    outs_d, outs_m, outs_v = {}, {}, {}
    for k_ in ["w_mod"] + big_names:
        shp = weights[k_].shape
        res = _adamw(*[t.reshape(shp[1:]) for t in (grads[k_], weights[k_], mom_m[k_], mom_v[k_])], "adamw_" + k_)
        for dst, buf in zip((outs_d, outs_m, outs_v), res):
            dst[k_] = buf.reshape(shp)
    adam_small = small_names + ["conv_w"]
    shapes = [weights[k_].shape for k_ in adam_small]
    res = _adamw(*[_pack([src[k_] for k_ in adam_small], rows_mult=8) for src in (grads, weights, mom_m, mom_v)], "adamw_small")
    for dst, buf in zip((outs_d, outs_m, outs_v), res):
        dst.update(zip(adam_small, _unpack(buf, shapes)))
    grads = {k_: grads[k_].reshape(weights[k_].shape) for k_ in names}
    return (loss, grad_x, *[grads[k_] for k_ in names], *[outs_d[k_] for k_ in names],
            *[outs_m[k_] for k_ in names], *[outs_v[k_] for k_ in names])
```

```python
import functools
import math

import numpy as np
import jax
import jax.numpy as jnp
from jax import lax
from jax.experimental import pallas as pl
from jax.experimental.pallas import tpu as pltpu

F32 = jnp.float32
MXU_DTYPE = jnp.bfloat16
MESH = pl.DeviceIdType.MESH

EPS = 1e-6
N_HEADS = 8
QK_NOPE = 64
QK_ROPE = 32
QK_DIM = QK_NOPE + QK_ROPE
V_DIM = 64
SLOT = 128
Q_LORA = 384
KV_LORA = 256
GRID_W = 64
ROPE_THETA = 10000.0
SSM_WIDTH = 512
SSM_GROUP = 16
SSM_GROUPS = 32
SSM_STATE = 64
N_STATE = SSM_GROUPS * SSM_STATE
CG_STATES = 512
N_CG = N_STATE // CG_STATES
CG_CHANNELS = SSM_WIDTH // N_CG
SCAN_LANES = 512
PACK_W = 1024

ADAM_LR = 0.001
ADAM_B1 = 0.9
ADAM_B2 = 0.999
ADAM_EPS = 1e-08
ADAM_WD = 0.01
ADAM_STEP = 10

VMEM_LIMIT = 56 * 1024 * 1024
LOG2E = 1.4426950408889634


def _pick(n, cands):
    for c in cands:
        if c <= n and n % c == 0:
            return c
    return n


def _cparams(sem):
    return pltpu.CompilerParams(dimension_semantics=sem, vmem_limit_bytes=VMEM_LIMIT)


def _sds(shape, dtype):
    return jax.ShapeDtypeStruct(tuple(shape), dtype)


_K_CANDS = (2816, 2048, 1536, 1408, 1280, 1152, 1024, 896, 768, 704, 640, 512, 384, 256, 128, 64, 32, 16)
_M_CANDS = (2048, 1408, 1024, 768, 512, 384, 256, 128, 64, 32, 16)
_N_CANDS = (1408, 1152, 1024, 768, 512, 384, 256, 128)
MM_VMEM_BUDGET = 40 * 1024 * 1024


def _mm_tiles(m, n, k_opts, a_bytes, b_bytes, o_bytes, m_cands):
    tn = n if n <= _N_CANDS[0] else _pick(n, _N_CANDS)
    for tk in k_opts:
        for tm in ((m,) if m <= m_cands[0] else ()) + tuple(t for t in m_cands if t < m and m % t == 0):
            if 2 * (tm * tk * a_bytes + tk * tn * b_bytes + tm * tn * o_bytes) + tm * tn * 4 <= MM_VMEM_BUDGET:
                return tm, tn, tk
    raise ValueError("no matmul tiling fits")


def _mm(a, b, mode, name, out_dtype=F32, rows=None, a_off=0, b_off=0):
    a_bytes, b_bytes, o_bytes = a.dtype.itemsize, b.dtype.itemsize, jnp.dtype(out_dtype).itemsize
    if mode == "tn":
        t_rows = rows or a.shape[0]
        m, n = a.shape[1], b.shape[1]
        k_opts = tuple(t for t in _K_CANDS if t <= t_rows and t_rows % t == 0) or (t_rows,)
        tm, tn, tk = _mm_tiles(m, n, k_opts, a_bytes, b_bytes, o_bytes, _M_CANDS[1:])
        nk = t_rows // tk
        ao, bo = a_off // tk, b_off // tk
        grid = (m // tm, n // tn, nk)
        in_specs = [pl.BlockSpec((tk, tm), lambda i, j, k: (k + ao, i)),
                    pl.BlockSpec((tk, tn), lambda i, j, k: (k + bo, j))]
        dn = (((0,), (0,)), ((), ()))
    else:
        m = rows or a.shape[0]
        kdim = a.shape[1]
        n = b.shape[1] if mode == "nn" else b.shape[0]
        k_opts = (kdim,) + tuple(t for t in _K_CANDS if t < kdim and kdim % t == 0)
        tm, tn, tk = _mm_tiles(m, n, k_opts, a_bytes, b_bytes, o_bytes, _M_CANDS)
        nk = kdim // tk
        ao = a_off // tm
        grid = (m // tm, n // tn, nk)
        if mode == "nn":
            in_specs = [pl.BlockSpec((tm, tk), lambda i, j, k: (i + ao, k)),
                        pl.BlockSpec((tk, tn), lambda i, j, k: (k, j))]
            dn = (((1,), (0,)), ((), ()))
        else:
            in_specs = [pl.BlockSpec((tm, tk), lambda i, j, k: (i + ao, k)),
                        pl.BlockSpec((tn, tk), lambda i, j, k: (j, k))]
            dn = (((1,), (1,)), ((), ()))
    use_scratch = nk > 1 and out_dtype != F32

    def body(a_ref, b_ref, o_ref, *scr):
        r = lax.dot_general(a_ref[...].astype(MXU_DTYPE), b_ref[...].astype(MXU_DTYPE), dn,
                            preferred_element_type=F32)
        if nk == 1:
            o_ref[...] = r.astype(o_ref.dtype)
        else:
            k = pl.program_id(2)
            acc = scr[0] if use_scratch else o_ref

            @pl.when(k == 0)
            def _():
                acc[...] = r

            @pl.when(k > 0)
            def _():
                acc[...] += r

            if use_scratch:
                @pl.when(k == nk - 1)
                def _():
                    o_ref[...] = acc[...].astype(o_ref.dtype)

    return pl.pallas_call(
        body, name=name, grid=grid, in_specs=in_specs,
        out_specs=pl.BlockSpec((tm, tn), lambda i, j, k: (i, j)),
        out_shape=_sds((m, n), out_dtype),
        scratch_shapes=[pltpu.VMEM((tm, tn), F32)] if use_scratch else [],
        compiler_params=_cparams(("parallel", "parallel", "arbitrary")),
    )(a, b)


def _rowwise(body, *, name, nblk, tr, rows=(), halo=(), sels=(), fulls=(), outs=(), accs=(), seg=None):
    n_rows, n_sel, n_full, n_out, n_acc = len(rows), len(sels), len(fulls), len(outs), len(accs)
    halo = tuple(halo)
    maxw = max([r[2] for r in rows] + [o[0] for o in outs] + list(accs))
    sr = _pick(tr, tuple(s for s in (256, 128, 64, 32, 16) if s * maxw <= 131072) or (16,))
    nsub = tr // sr
    total8 = nblk * tr // 8

    def seg_of(i):
        return jnp.where(i >= seg, 1, 0) if seg is not None else 0

    in_specs, operands = [], []
    for arr, cb, w, roff in rows:
        ob = roff // tr
        last = arr.shape[0] // tr - 1
        in_specs.append(pl.BlockSpec((tr, w), lambda i, cb=cb, ob=ob, last=last: (jnp.minimum(i + ob, last), cb)))
        operands.append(arr)
    for h in halo:
        arr, cb, w, roff = rows[h]
        o8, t8 = roff // 8, tr // 8
        in_specs.append(pl.BlockSpec((8, w), lambda i, cb=cb, o8=o8, t8=t8: (jnp.maximum(i * t8 - 1, 0) + o8, cb)))
        in_specs.append(pl.BlockSpec((8, w), lambda i, cb=cb, o8=o8, t8=t8: (jnp.minimum((i + 1) * t8, total8 - 1) + o8, cb)))
        operands += [arr, arr]
    for arr in sels:
        in_specs.append(pl.BlockSpec((None,) + arr.shape[1:], lambda i: (seg_of(i), 0, 0)))
        operands.append(arr)
    for arr in fulls:
        in_specs.append(pl.BlockSpec(arr.shape, lambda i: (0, 0)))
        operands.append(arr)
    out_specs, out_shape = [], []
    for w, dt in outs:
        out_specs.append(pl.BlockSpec((tr, w), lambda i: (i, 0)))
        out_shape.append(_sds((nblk * tr, w), dt))
    for w in accs:
        if seg is None:
            out_specs.append(pl.BlockSpec((8, w), lambda i: (0, 0)))
            out_shape.append(_sds((8, w), F32))
        else:
            out_specs.append(pl.BlockSpec((None, 8, w), lambda i: (seg_of(i), 0, 0)))
            out_shape.append(_sds((2, 8, w), F32))
    n_halo = 2 * len(halo)

    def kern(*refs):
        row_refs = refs[:n_rows]
        halo_refs = refs[n_rows:n_rows + n_halo]
        sel_refs = refs[n_rows + n_halo:n_rows + n_halo + n_sel]
        full_refs = refs[n_rows + n_halo + n_sel:n_rows + n_halo + n_sel + n_full]
        o0 = n_rows + n_halo + n_sel + n_full
        out_refs = refs[o0:o0 + n_out]
        acc_refs = refs[o0 + n_out:o0 + n_out + n_acc]
        i = pl.program_id(0)
        if n_acc:
            first = (i == 0) if seg is None else ((i == 0) | (i == seg))

            @pl.when(first)
            def _():
                for a_ref in acc_refs:
                    a_ref[...] = jnp.zeros(a_ref.shape, F32)

        def sub(s, carry):
            r0 = pl.multiple_of(s * sr, sr)
            vals = []
            for idx, r in enumerate(row_refs):
                cur = r[pl.ds(r0, sr), :]
                if idx in halo:
                    hp = halo_refs[2 * halo.index(idx)]
                    hn = halo_refs[2 * halo.index(idx) + 1]
                    cur = cur.astype(F32)
                    rid = lax.broadcasted_iota(jnp.int32, cur.shape, 0)
                    lo = r[pl.ds(pl.multiple_of(jnp.maximum(r0 - 8, 0), 8), 8), :].astype(F32)
                    lo = jnp.where(s == 0, hp[...].astype(F32), lo)
                    lo = jnp.where((s == 0) & (i == 0), 0.0, lo)
                    hi = r[pl.ds(pl.multiple_of(jnp.minimum(r0 + sr, tr - 8), 8), 8), :].astype(F32)
                    hi = jnp.where(s == nsub - 1, hn[...].astype(F32), hi)
                    hi = jnp.where((s == nsub - 1) & (i == nblk - 1), 0.0, hi)
                    prev = jnp.where(rid == 0, jnp.broadcast_to(lo[7:8, :], cur.shape), pltpu.roll(cur, 1, 0))
                    nxt = jnp.where(rid == sr - 1, jnp.broadcast_to(hi[0:1, :], cur.shape), pltpu.roll(cur, sr - 1, 0))
                    vals.append((prev, cur, nxt))
                else:
                    vals.append(cur)
            res = body(*vals, *[r[...] for r in sel_refs], *[r[...] for r in full_refs])
            if not isinstance(res, (tuple, list)):
                res = (res,)
            for o_ref, v in zip(out_refs, res[:n_out]):
                o_ref[pl.ds(r0, sr), :] = v.astype(o_ref.dtype)
            for a_ref, v in zip(acc_refs, res[n_out:]):
                a_ref[...] += jnp.sum(v.astype(F32).reshape(sr // 8, 8, v.shape[-1]), axis=0)
            return carry

        lax.fori_loop(0, nsub, sub, 0)

    res = pl.pallas_call(
        kern, name=name, grid=(nblk,), in_specs=in_specs, out_specs=out_specs, out_shape=out_shape,
        compiler_params=_cparams(("arbitrary",)),
    )(*operands)
    return res


def _sigmoid(x):
    return 1.0 / (1.0 + jnp.exp(-x))


def _silu(x):
    return x * _sigmoid(x)


def _dsilu(x):
    s = _sigmoid(x)
    return s * (1.0 + x * (1.0 - s))


_GELU_K = math.sqrt(2.0 / math.pi)


def _gelu(x):
    return 0.5 * x * (1.0 + jnp.tanh(_GELU_K * (x + 0.044715 * x * x * x)))


def _dgelu(x):
    t = jnp.tanh(_GELU_K * (x + 0.044715 * x * x * x))
    return 0.5 * (1.0 + t) + 0.5 * x * (1.0 - t * t) * _GELU_K * (1.0 + 3.0 * 0.044715 * x * x)


def _rms_fwd(x, g, width):
    r = lax.rsqrt(jnp.sum(x * x, axis=-1, keepdims=True) * (1.0 / width) + EPS)
    return x * r * g


def _rms_bwd(x, g, dy, width):
    r = lax.rsqrt(jnp.sum(x * x, axis=-1, keepdims=True) * (1.0 / width) + EPS)
    xn = x * r
    dyg = dy * g
    dx = r * (dyg - xn * (jnp.sum(dyg * xn, axis=-1, keepdims=True) * (1.0 / width)))
    return dx, dy * xn


def _rope_fwd(y, c, sa, sb):
    return y * c + pltpu.roll(y, SLOT - 16, 1) * sa + pltpu.roll(y, 16, 1) * sb


def _rope_bwd(d, c, sa, sb):
    return d * c + pltpu.roll(d * sa, 16, 1) + pltpu.roll(d * sb, SLOT - 16, 1)


def _heads(v):
    return [v[:, h * SLOT:(h + 1) * SLOT] for h in range(N_HEADS)]


def _attn_fwd(q, k, v, nl, scale, side=None):
    n = k.shape[0]
    tq = _pick(nl, (4096, 2048, 1024, 512, 256, 128))
    tk = _pick(n, (2816, 1408, 1152, 768, 384, 256, 128))
    sub = min(tq, 512)
    nk = n // tk
    rep = tk // SLOT
    c = scale * LOG2E

    def body(q_ref, k_ref, v_ref, o_ref, lse_ref, m_sc, l_sc, acc_sc):
        ki = pl.program_id(2)

        @pl.when(ki == 0)
        def _():
            m_sc[...] = jnp.full(m_sc.shape, -jnp.inf, F32)
            l_sc[...] = jnp.zeros(l_sc.shape, F32)
            acc_sc[...] = jnp.zeros(acc_sc.shape, F32)

        kb, vb = k_ref[...], v_ref[...]
        for sb in range(tq // sub):
            rows = slice(sb * sub, (sb + 1) * sub)
            s = lax.dot_general(q_ref[rows, :], kb, (((1,), (1,)), ((), ())), preferred_element_type=F32)
            m_prev = m_sc[rows, :]
            m_new = jnp.maximum(m_prev, jnp.max(s, axis=1, keepdims=True) * c)
            alpha = jnp.exp2(m_prev - m_new)
            p = jnp.exp2(s * c - jnp.tile(m_new, (1, rep)))
            l_sc[rows, :] = alpha * l_sc[rows, :] + jnp.sum(p, axis=1, keepdims=True)
            acc_sc[rows, :] = alpha * acc_sc[rows, :] + jnp.dot(p.astype(MXU_DTYPE), vb, preferred_element_type=F32)
            m_sc[rows, :] = m_new

        @pl.when(ki == nk - 1)
        def _():
            l = l_sc[...]
            o_ref[...] = (acc_sc[...] / l).astype(o_ref.dtype)
            lse_ref[...] = jnp.transpose(m_sc[...] + jnp.log2(l))[0:8, :]

    grid = (N_HEADS, nl // tq, nk)
    body, s_in, s_out, s_shape, s_scr = _ride_along(body, 3, 2, 3, grid, side)
    res = pl.pallas_call(
        body, name="attn_fwd", grid=grid,
        in_specs=[pl.BlockSpec((tq, SLOT), lambda h, i, j: (i, h)),
                  pl.BlockSpec((tk, SLOT), lambda h, i, j: (j, h)),
                  pl.BlockSpec((tk, SLOT), lambda h, i, j: (j, h))] + s_in,
        out_specs=[pl.BlockSpec((tq, SLOT), lambda h, i, j: (i, h)),
                   pl.BlockSpec((None, 8, tq), lambda h, i, j: (h, 0, i))] + s_out,
        out_shape=[_sds((nl, N_HEADS * SLOT), MXU_DTYPE), _sds((N_HEADS, 8, nl), F32)] + s_shape,
        scratch_shapes=[pltpu.VMEM((tq, SLOT), F32), pltpu.VMEM((tq, SLOT), F32), pltpu.VMEM((tq, SLOT), F32)] + s_scr,
        compiler_params=_cparams(("arbitrary", "arbitrary", "arbitrary")),
    )(q, k, v, *(side[0] if side else ()))
    return res[0], res[1], res[2:]


def _attn_bwd(q, k, v, do, o, lse_t, nl, scale, side=None):
    n = k.shape[0]
    tq = _pick(nl, (2048, 1024, 512, 256, 128))
    tk = _pick(n, (2816, 1408, 1152, 768, 384, 256, 128))
    sub = _pick(tk, (256, 128))
    nq, nk = nl // tq, n // tk
    c = scale * LOG2E

    def body(q_ref, k_ref, v_ref, do_ref, o_ref, lse_ref, dq_ref, dk_ref, dv_ref, dq_acc, dk_acc, dv_acc):
        ki, qi = pl.program_id(1), pl.program_id(2)

        @pl.when((ki == 0) & (qi == 0))
        def _():
            dq_acc[...] = jnp.zeros(dq_acc.shape, F32)

        @pl.when(qi == 0)
        def _():
            dk_acc[...] = jnp.zeros(dk_acc.shape, F32)
            dv_acc[...] = jnp.zeros(dv_acc.shape, F32)

        qb, dof = q_ref[...], do_ref[...]
        dob = dof.astype(MXU_DTYPE)
        lse_r = lse_ref[0:1, :]
        dl_r = jnp.sum(jnp.transpose(dof * o_ref[...].astype(F32)), axis=0, keepdims=True)
        dq_part = None
        for sb in range(tk // sub):
            rows = slice(sb * sub, (sb + 1) * sub)
            kb = k_ref[rows, :]
            s_t = lax.dot_general(kb, qb, (((1,), (1,)), ((), ())), preferred_element_type=F32)
            p_t = jnp.exp2(s_t * c - lse_r)
            dp_t = lax.dot_general(v_ref[rows, :], dob, (((1,), (1,)), ((), ())), preferred_element_type=F32)
            ds_t = (p_t * (dp_t - dl_r) * scale).astype(MXU_DTYPE)
            dv_acc[rows, :] += jnp.dot(p_t.astype(MXU_DTYPE), dob, preferred_element_type=F32)
            dk_acc[rows, :] += jnp.dot(ds_t, qb, preferred_element_type=F32)
            part = lax.dot_general(kb, ds_t, (((0,), (0,)), ((), ())), preferred_element_type=F32)
            dq_part = part if dq_part is None else dq_part + part
        c0 = pl.multiple_of(qi * tq, tq)
        dq_acc[:, pl.ds(c0, tq)] += dq_part

        @pl.when(ki == nk - 1)
        def _():
            dq_ref[...] = jnp.transpose(dq_acc[:, pl.ds(c0, tq)])

        @pl.when(qi == nq - 1)
        def _():
            dk_ref[...] = dk_acc[...]
            dv_ref[...] = dv_acc[...]

    grid = (N_HEADS, nk, nq)
    body, s_in, s_out, s_shape, s_scr = _ride_along(body, 6, 3, 3, grid, side)
    res = pl.pallas_call(
        body, name="attn_bwd", grid=grid,
        in_specs=[pl.BlockSpec((tq, SLOT), lambda h, j, i: (i, h)),
                  pl.BlockSpec((tk, SLOT), lambda h, j, i: (j, h)),
                  pl.BlockSpec((tk, SLOT), lambda h, j, i: (j, h)),
                  pl.BlockSpec((tq, SLOT), lambda h, j, i: (i, h)),
                  pl.BlockSpec((tq, SLOT), lambda h, j, i: (i, h)),
                  pl.BlockSpec((None, 8, tq), lambda h, j, i: (h, 0, i))] + s_in,
        out_specs=[pl.BlockSpec((tq, SLOT), lambda h, j, i: (jnp.where(j == nk - 1, i, 0), h)),
                   pl.BlockSpec((tk, SLOT), lambda h, j, i: (j, h)),
                   pl.BlockSpec((tk, SLOT), lambda h, j, i: (j, h))] + s_out,
        out_shape=[_sds((nl, N_HEADS * SLOT), F32), _sds((n, N_HEADS * SLOT), F32), _sds((n, N_HEADS * SLOT), F32)] + s_shape,
        scratch_shapes=[pltpu.VMEM((SLOT, nl), F32), pltpu.VMEM((tk, SLOT), F32), pltpu.VMEM((tk, SLOT), F32)] + s_scr,
        compiler_params=_cparams(("arbitrary", "arbitrary", "arbitrary")),
    )(q, k, v, do, o, lse_t, *(side[0] if side else ()))
    return res[0], res[1], res[2], res[3:]


def _scan_consts(c_ref, lg):
    cs = slice(lg * SCAN_LANES, (lg + 1) * SCAN_LANES)
    return [c_ref[8 * kk:8 * kk + 8, cs] for kk in range(8)]


def _tile_scan(br, bi, consts, reverse):
    p1r, p1i, p2r, p2i, p4r, p4i = consts[:6]
    for pr, pi, kk in ((p1r, p1i, 1), (p2r, p2i, 2), (p4r, p4i, 4)):
        sh = (8 - kk) if reverse else kk
        sr_, si_ = pltpu.roll(br, sh, 0), pltpu.roll(bi, sh, 0)
        br, bi = br + pr * sr_ - pi * si_, bi + pr * si_ + pi * sr_
    return br, bi


def _seq_chunk(j, nch, nlc, reverse):
    return (nch - 1 - j) if reverse else (j + nlc) % nch


def _s5_scan(z, u_cb, bbd, cbd_n, lamc, t_rows, nl, reverse, name):
    n = z.shape[0]
    nch, nlc = n // t_rows, nl // t_rows
    ntile = t_rows // 8
    w = SCAN_LANES
    edge = 0 if reverse else 7
    ucb = u_cb * (SSM_WIDTH // CG_CHANNELS)

    def chunk(j):
        return _seq_chunk(j, nch, nlc, reverse)

    def body(u_ref, b_ref, cn_ref, c_ref, xs_ref, y_ref, carry):
        j = pl.program_id(1)

        @pl.when(j == 0)
        def _():
            carry[...] = jnp.zeros(carry.shape, F32)

        xs_ref[...] = jnp.dot(u_ref[...].astype(MXU_DTYPE), b_ref[...], preferred_element_type=F32)
        for lg in range(CG_STATES // w):
            re = slice(lg * w, (lg + 1) * w)
            im = slice(CG_STATES + lg * w, CG_STATES + (lg + 1) * w)
            consts = _scan_consts(c_ref, lg)
            qr, qi = consts[6], consts[7]

            def tile(tt, st):
                cr, ci = st
                t = (ntile - 1 - tt) if reverse else tt
                r0 = pl.multiple_of(t * 8, 8)
                br, bi = _tile_scan(xs_ref[pl.ds(r0, 8), re], xs_ref[pl.ds(r0, 8), im], consts, reverse)
                lr = jnp.broadcast_to(cr[edge:edge + 1, :], br.shape)
                li = jnp.broadcast_to(ci[edge:edge + 1, :], bi.shape)
                xr = br + qr * lr - qi * li
                xi = bi + qr * li + qi * lr
                xs_ref[pl.ds(r0, 8), re] = xr
                xs_ref[pl.ds(r0, 8), im] = xi
                return xr, xi

            cr, ci = lax.fori_loop(0, ntile, tile, (carry[:, re], carry[:, im]))
            carry[:, re] = cr
            carry[:, im] = ci
        y_ref[...] = jnp.dot(xs_ref[...].astype(MXU_DTYPE), cn_ref[...], preferred_element_type=F32)

    cw = 2 * CG_STATES
    return pl.pallas_call(
        body, name=name, grid=(N_CG, nch),
        in_specs=[pl.BlockSpec((t_rows, CG_CHANNELS), lambda g, j: (chunk(j), ucb + g)),
                  pl.BlockSpec((CG_CHANNELS, cw), lambda g, j: (g, 0)),
                  pl.BlockSpec((cw, CG_CHANNELS), lambda g, j: (g, 0)),
                  pl.BlockSpec((64, CG_STATES), lambda g, j: (0, g))],
        out_specs=[pl.BlockSpec((t_rows, cw), lambda g, j: (chunk(j), g)),
                   pl.BlockSpec((t_rows, CG_CHANNELS), lambda g, j: (chunk(j), g))],
        out_shape=[_sds((n, 2 * N_STATE), F32), _sds((n, SSM_WIDTH), F32)],
        scratch_shapes=[pltpu.VMEM((8, cw), F32)],
        compiler_params=_cparams(("arbitrary", "arbitrary")),
    )(z, bbd, cbd_n, lamc)


def _s5_bwd(dys, z, u_cb, xs, cbd_t, bbd_t, lamc_adj, t_rows, nl, reverse, name):
    n = z.shape[0]
    nch, nlc = n // t_rows, nl // t_rows
    ntile = t_rows // 8
    t8 = t_rows // 8
    w = SCAN_LANES
    cw = 2 * CG_STATES
    adj_rev = not reverse
    edge = 0 if adj_rev else 7

    def chunk(j):
        return _seq_chunk(nch - 1 - j, nch, nlc, reverse)

    def halo_blk(j):
        if reverse:
            return jnp.minimum((chunk(j) + 1) * t8, n // 8 - 1)
        return (_seq_chunk(jnp.maximum(nch - 2 - j, 0), nch, nlc, False) + 1) * t8 - 1

    def body(dy_ref, u_ref, xs_ref, halo_ref, ct_ref, bt_ref, c_ref, du_ref, db_ref, dc_ref, dl_ref, gbuf, carry):
        j = pl.program_id(1)
        start = j == nch - 1

        @pl.when(j == 0)
        def _():
            carry[...] = jnp.zeros(carry.shape, F32)
            db_ref[...] = jnp.zeros(db_ref.shape, F32)
            dc_ref[...] = jnp.zeros(dc_ref.shape, F32)
            dl_ref[...] = jnp.zeros(dl_ref.shape, F32)

        dy = jnp.where(chunk(j) < nlc, dy_ref[...], 0.0).astype(MXU_DTYPE)
        gbuf[...] = jnp.dot(dy, ct_ref[...], preferred_element_type=F32)
        dc_ref[...] += lax.dot_general(dy, xs_ref[...].astype(MXU_DTYPE), (((0,), (0,)), ((), ())),
                                       preferred_element_type=F32)
        for lg in range(CG_STATES // w):
            re = slice(lg * w, (lg + 1) * w)
            im = slice(CG_STATES + lg * w, CG_STATES + (lg + 1) * w)
            consts = _scan_consts(c_ref, lg)
            qr, qi = consts[6], consts[7]
            hr, hi = halo_ref[:, re], halo_ref[:, im]

            def tile(tt, st):
                gcr, gci, ar, ai = st
                t = (ntile - 1 - tt) if adj_rev else tt
                r0 = pl.multiple_of(t * 8, 8)
                br, bi = _tile_scan(gbuf[pl.ds(r0, 8), re], gbuf[pl.ds(r0, 8), im], consts, adj_rev)
                lr = jnp.broadcast_to(gcr[edge:edge + 1, :], br.shape)
                li = jnp.broadcast_to(gci[edge:edge + 1, :], bi.shape)
                gr = br + qr * lr - qi * li
                gi = bi + qr * li + qi * lr
                gbuf[pl.ds(r0, 8), re] = gr
                gbuf[pl.ds(r0, 8), im] = gi
                xr, xi = xs_ref[pl.ds(r0, 8), re], xs_ref[pl.ds(r0, 8), im]
                rid = lax.broadcasted_iota(jnp.int32, xr.shape, 0)
                if reverse:
                    last = t == ntile - 1
                    rn = pl.multiple_of(jnp.minimum(r0 + 8, t_rows - 8), 8)
                    nbr = jnp.where(last, hr, xs_ref[pl.ds(rn, 8), re])
                    nbi = jnp.where(last, hi, xs_ref[pl.ds(rn, 8), im])
                    nbr = jnp.where(last & start, 0.0, nbr)
                    nbi = jnp.where(last & start, 0.0, nbi)
                    xpr = jnp.where(rid == 7, jnp.broadcast_to(nbr[0:1, :], xr.shape), pltpu.roll(xr, 7, 0))
                    xpi = jnp.where(rid == 7, jnp.broadcast_to(nbi[0:1, :], xi.shape), pltpu.roll(xi, 7, 0))
                else:
                    first = t == 0
                    rn = pl.multiple_of(jnp.maximum(r0 - 8, 0), 8)
                    nbr = jnp.where(first, hr, xs_ref[pl.ds(rn, 8), re])
                    nbi = jnp.where(first, hi, xs_ref[pl.ds(rn, 8), im])
                    nbr = jnp.where(first & start, 0.0, nbr)
                    nbi = jnp.where(first & start, 0.0, nbi)
                    xpr = jnp.where(rid == 0, jnp.broadcast_to(nbr[7:8, :], xr.shape), pltpu.roll(xr, 1, 0))
                    xpi = jnp.where(rid == 0, jnp.broadcast_to(nbi[7:8, :], xi.shape), pltpu.roll(xi, 1, 0))
                ar = ar + gr * xpr + gi * xpi
                ai = ai - gr * xpi + gi * xpr
                return gr, gi, ar, ai

            zz = jnp.zeros((8, w), F32)
            gcr, gci, ar, ai = lax.fori_loop(0, ntile, tile, (carry[:, re], carry[:, im], zz, zz))
            carry[:, re] = gcr
            carry[:, im] = gci
            dl_ref[:, re] += ar
            dl_ref[:, im] += ai
        g = gbuf[...].astype(MXU_DTYPE)
        du_ref[...] = jnp.dot(g, bt_ref[...], preferred_element_type=F32)
        db_ref[...] += lax.dot_general(u_ref[...].astype(MXU_DTYPE), g, (((0,), (0,)), ((), ())),
                                       preferred_element_type=F32)

    ucb = u_cb * (SSM_WIDTH // CG_CHANNELS)
    return pl.pallas_call(
        body, name=name, grid=(N_CG, nch),
        in_specs=[pl.BlockSpec((t_rows, CG_CHANNELS), lambda g, j: (jnp.minimum(chunk(j), nlc - 1), g)),
                  pl.BlockSpec((t_rows, CG_CHANNELS), lambda g, j: (chunk(j), ucb + g)),
                  pl.BlockSpec((t_rows, cw), lambda g, j: (chunk(j), g)),
                  pl.BlockSpec((8, cw), lambda g, j: (halo_blk(j), g)),
                  pl.BlockSpec((CG_CHANNELS, cw), lambda g, j: (g, 0)),
                  pl.BlockSpec((cw, CG_CHANNELS), lambda g, j: (g, 0)),
                  pl.BlockSpec((64, CG_STATES), lambda g, j: (0, g))],
        out_specs=[pl.BlockSpec((t_rows, CG_CHANNELS), lambda g, j: (chunk(j), g)),
                   pl.BlockSpec((CG_CHANNELS, cw), lambda g, j: (g, 0)),
                   pl.BlockSpec((CG_CHANNELS, cw), lambda g, j: (g, 0)),
                   pl.BlockSpec((8, cw), lambda g, j: (0, g))],
        out_shape=[_sds((n, SSM_WIDTH), F32), _sds((SSM_WIDTH, cw), F32), _sds((SSM_WIDTH, cw), F32),
                   _sds((8, 2 * N_STATE), F32)],
        scratch_shapes=[pltpu.VMEM((t_rows, cw), F32), pltpu.VMEM((8, cw), F32)],
        compiler_params=_cparams(("arbitrary", "arbitrary")),
    )(dys, z, xs, xs, cbd_t, bbd_t, lamc_adj)


_CG_GROUPS = SSM_GROUPS // N_CG


def _group_mask():
    idx = jnp.arange(_CG_GROUPS)
    return (idx[:, None] == idx[None, :])[None, :, None, None, :, None]


def _diag_blocks(p_re, p_im):
    t = jnp.stack([p_re, p_im], axis=2).reshape(N_CG, _CG_GROUPS, SSM_GROUP, 2, 1, SSM_STATE)
    return jnp.where(_group_mask(), t, 0.0).reshape(SSM_WIDTH, 2 * CG_STATES)


def _diag_extract(d):
    d6 = d.reshape(N_CG, _CG_GROUPS, SSM_GROUP, 2, _CG_GROUPS, SSM_STATE)
    blk = jnp.sum(jnp.where(_group_mask(), d6, 0.0), axis=4)
    blk = blk.reshape(SSM_GROUPS, SSM_GROUP, 2, SSM_STATE)
    return blk[:, :, 0], blk[:, :, 1]


def _block_transpose(d):
    return jnp.transpose(d.reshape(N_CG, CG_CHANNELS, 2 * CG_STATES), (0, 2, 1)).reshape(2 * N_STATE, CG_CHANNELS)


def _s5_disc(lam_re, lam_im, log_dt, b_re, b_im):
    lam = lax.complex(lam_re, lam_im)
    dt = jnp.exp(log_dt)[:, None]
    lam_bar = jnp.exp(lam * dt)
    b_bar = ((lam_bar - 1.0) / lam)[..., None] * lax.complex(b_re, b_im)
    return jnp.real(lam_bar), jnp.imag(lam_bar), jnp.real(b_bar), jnp.imag(b_bar)


def _lam_consts(lr, li, mirrored, conj):
    lam = lax.complex(lr.reshape(-1), -li.reshape(-1) if conj else li.reshape(-1))
    p2 = lam * lam
    p4 = p2 * p2
    pw = [lam, p2, p2 * lam, p4, p4 * lam, p4 * p2, p4 * p2 * lam, p4 * p4]
    rows = jnp.arange(8)[:, None]
    out = []
    for kk in (1, 2, 4):
        mask = (rows <= 7 - kk) if mirrored else (rows >= kk)
        pk = jnp.where(mask, pw[kk - 1][None, :], 0.0)
        out += [jnp.real(pk), jnp.imag(pk)]
    q = jnp.stack(pw[::-1] if mirrored else pw)
    return jnp.concatenate(out + [jnp.real(q), jnp.imag(q)], axis=0)


def _dev(t):
    return (t // 4, (t // 2) % 2, t % 2)


def _my_index():
    return 4 * lax.axis_index("x") + 2 * lax.axis_index("y") + lax.axis_index("c")


def _comm_call(body, name, arrs, lead, n_remote):
    nw = len(arrs)
    any_spec = pl.BlockSpec(memory_space=pl.ANY)
    return pl.pallas_call(
        body, name=name, out_shape=[_sds((lead,) + a.shape[-2:], a.dtype) for a in arrs],
        in_specs=[any_spec] * nw, out_specs=[any_spec] * nw,
        scratch_shapes=[pltpu.SemaphoreType.DMA((n_remote * nw,)), pltpu.SemaphoreType.DMA((n_remote * nw,)),
                        pltpu.SemaphoreType.DMA((2 * nw,))] + [pltpu.VMEM(a.shape[-2:], a.dtype) for a in arrs],
        compiler_params=pltpu.CompilerParams(vmem_limit_bytes=VMEM_LIMIT),
    )(*arrs)


class _LocalCopy:
    def __init__(self, src, dst, buf, sem_in, sem_out):
        self.fetch = pltpu.make_async_copy(src, buf, sem_in)
        self.store = pltpu.make_async_copy(buf, dst, sem_out)
        self.fetch.start()

    def forward(self):
        self.fetch.wait()
        self.store.start()

    def finish(self):
        self.store.wait()


def _exchange8(gs, name, same):
    nw = len(gs)

    def body(*refs):
        g_refs, o_refs, (ssem, rsem, lsem), bufs = refs[:nw], refs[nw:2 * nw], refs[2 * nw:2 * nw + 3], refs[2 * nw + 3:]
        me = _my_index()
        locs, sends = [], []
        for i, (g_ref, o_ref) in enumerate(zip(g_refs, o_refs)):
            src = (lambda t, g_ref=g_ref: g_ref) if same else (lambda t, g_ref=g_ref: g_ref.at[t])
            locs.append(_LocalCopy(src(me), o_ref.at[me], bufs[i], lsem.at[2 * i], lsem.at[2 * i + 1]))
            for d in range(1, 8):
                t = (me + d) % 8
                cp = pltpu.make_async_remote_copy(src_ref=src(t), dst_ref=o_ref.at[me], send_sem=ssem.at[7 * i + d - 1],
                                                  recv_sem=rsem.at[7 * i + d - 1], device_id=_dev(t), device_id_type=MESH)
                cp.start()
                sends.append(cp)
        for loc in locs:
            loc.forward()
        for i, (g_ref, o_ref) in enumerate(zip(g_refs, o_refs)):
            src = (lambda t, g_ref=g_ref: g_ref) if same else (lambda t, g_ref=g_ref: g_ref.at[t])
            for d in range(1, 8):
                s = (me + 8 - d) % 8
                pltpu.make_async_remote_copy(src_ref=src(s), dst_ref=o_ref.at[s], send_sem=ssem.at[7 * i + d - 1],
                                             recv_sem=rsem.at[7 * i + d - 1], device_id=_dev(s),
                                             device_id_type=MESH).wait_recv()
        for cp in sends:
            cp.wait_send()
        for loc in locs:
            loc.finish()

    return _comm_call(body, name, gs, 8, 7)


def _chip_copies(w_refs, o_refs, ssem, rsem, lsem, bufs, gather):
    x, y, cc = lax.axis_index("x"), lax.axis_index("y"), lax.axis_index("c")
    k = 2 * x + y
    peers = [(1 - x, y), (x, 1 - y), (1 - x, 1 - y)]
    fetch, store, sends, recvs = [], [], [], []
    for i, (w_ref, o_ref) in enumerate(zip(w_refs, o_refs)):
        if gather:
            fetch.append(pltpu.make_async_copy(w_ref.at[cc], bufs[i], lsem.at[2 * i]))
            store.append(pltpu.make_async_copy(bufs[i], o_ref.at[k], lsem.at[2 * i + 1]))
        for j, (px, py) in enumerate(peers):
            sems = dict(send_sem=ssem.at[3 * i + j], recv_sem=rsem.at[3 * i + j], device_id=(px, py, cc), device_id_type=MESH)
            src, dst = (w_ref.at[cc], o_ref.at[k]) if gather else (w_ref.at[2 * px + py], o_ref.at[j])
            sends.append(pltpu.make_async_remote_copy(src_ref=src, dst_ref=dst, **sems))
            src, dst = (w_ref.at[cc], o_ref.at[2 * px + py]) if gather else (w_ref.at[k], o_ref.at[j])
            recvs.append(pltpu.make_async_remote_copy(src_ref=src, dst_ref=dst, **sems))
    return fetch, store, sends, recvs


def _chips_start(*args):
    fetch, _, sends, _ = _chip_copies(*args)
    for cp in fetch + sends:
        cp.start()


def _chips_finish(*args):
    fetch, store, sends, recvs = _chip_copies(*args)
    for cp in fetch:
        cp.wait()
    for cp in store:
        cp.start()
    for cp in recvs:
        cp.wait_recv()
    for cp in sends:
        cp.wait_send()
    for cp in store:
        cp.wait()


def _chips_scratch(ws, gather):
    nw = len(ws)
    return ([pltpu.SemaphoreType.DMA((3 * nw,)), pltpu.SemaphoreType.DMA((3 * nw,)), pltpu.SemaphoreType.DMA((2 * nw,))]
            + ([pltpu.VMEM(a.shape[-2:], a.dtype) for a in ws] if gather else []))


def _ride_along(core, n_in, n_out, n_scr, grid, side):
    if side is None:
        return core, [], [], [], []
    arrs, gather = side
    ns = len(arrs)

    def body(*refs):
        a, b, c_ = n_in + ns, n_in + ns + n_out, n_in + 2 * ns + n_out
        s_scr = refs[c_ + n_scr:]
        sargs = (refs[n_in:a], refs[b:c_], *s_scr[:3], s_scr[3:], gather)
        ids = [pl.program_id(ax) for ax in range(len(grid))]
        first, last = ids[0] == 0, ids[0] == grid[0] - 1
        for i_, g_ in zip(ids[1:], grid[1:]):
            first, last = first & (i_ == 0), last & (i_ == g_ - 1)

        @pl.when(first)
        def _():
            _chips_start(*sargs)

        core(*refs[:n_in], *refs[a:b], *refs[c_:c_ + n_scr])

        @pl.when(last)
        def _():
            _chips_finish(*sargs)

    any_spec = pl.BlockSpec(memory_space=pl.ANY)
    shapes = [_sds((4 if gather else 3,) + a_.shape[-2:], a_.dtype) for a_ in arrs]
    return body, [any_spec] * ns, [any_spec] * ns, shapes, _chips_scratch(arrs, gather)


def _exchange_chips(ws, name, gather):
    nw = len(ws)

    def body(*refs):
        args = (refs[:nw], refs[nw:2 * nw], *refs[2 * nw:2 * nw + 3], refs[2 * nw + 3:], gather)
        _chips_start(*args)
        _chips_finish(*args)

    return _comm_call(body, name, ws, 4 if gather else 3, 3)


def _sibling_send(hs, name):
    nw = len(hs)

    def body(*refs):
        h_refs, o_refs, (ssem, rsem, lsem) = refs[:nw], refs[nw:2 * nw], refs[2 * nw:]
        x, y, cc = lax.axis_index("x"), lax.axis_index("y"), lax.axis_index("c")
        sends = []
        for i, (h_ref, o_ref) in enumerate(zip(h_refs, o_refs)):
            cp = pltpu.make_async_remote_copy(src_ref=h_ref.at[1 - cc], dst_ref=o_ref, send_sem=ssem.at[i],
                                              recv_sem=rsem.at[i], device_id=(x, y, 1 - cc), device_id_type=MESH)
            cp.start()
            sends.append(cp)
        for i, (h_ref, o_ref) in enumerate(zip(h_refs, o_refs)):
            pltpu.make_async_remote_copy(src_ref=h_ref.at[cc], dst_ref=o_ref, send_sem=ssem.at[i], recv_sem=rsem.at[i],
                                         device_id=(x, y, 1 - cc), device_id_type=MESH).wait_recv()
        for cp in sends:
            cp.wait_send()

    nw_spec = pl.BlockSpec(memory_space=pl.ANY)
    return pl.pallas_call(
        body, name=name, out_shape=[_sds(h.shape[1:], h.dtype) for h in hs],
        in_specs=[nw_spec] * nw, out_specs=[nw_spec] * nw,
        scratch_shapes=[pltpu.SemaphoreType.DMA((nw,)), pltpu.SemaphoreType.DMA((nw,)), pltpu.SemaphoreType.DMA((nw,))],
    )(*hs)


def _sibling_exchange(hs, name):
    nw = len(hs)

    def body(*refs):
        h_refs, o_refs, (ssem, rsem, lsem), bufs = refs[:nw], refs[nw:2 * nw], refs[2 * nw:2 * nw + 3], refs[2 * nw + 3:]
        x, y, cc = lax.axis_index("x"), lax.axis_index("y"), lax.axis_index("c")
        locs, sends = [], []
        for i, (h_ref, o_ref) in enumerate(zip(h_refs, o_refs)):
            locs.append(_LocalCopy(h_ref, o_ref.at[cc], bufs[i], lsem.at[2 * i], lsem.at[2 * i + 1]))
            cp = pltpu.make_async_remote_copy(src_ref=h_ref, dst_ref=o_ref.at[cc], send_sem=ssem.at[i], recv_sem=rsem.at[i],
                                              device_id=(x, y, 1 - cc), device_id_type=MESH)
            cp.start()
            sends.append(cp)
        for loc in locs:
            loc.forward()
        for i, (h_ref, o_ref) in enumerate(zip(h_refs, o_refs)):
            pltpu.make_async_remote_copy(src_ref=h_ref, dst_ref=o_ref.at[1 - cc], send_sem=ssem.at[i], recv_sem=rsem.at[i],
                                         device_id=(x, y, 1 - cc), device_id_type=MESH).wait_recv()
        for cp in sends:
            cp.wait_send()
        for loc in locs:
            loc.finish()

    return _comm_call(body, name, hs, 2, 1)


def _sum8(buf, name):
    _, r, c = buf.shape
    tr = _pick(r, (256, 128, 64, 32, 16, 8))
    flat = buf.reshape(8 * r, c)

    def body(*v):
        acc = v[0]
        for t in v[1:]:
            acc = acc + t
        return acc

    return _rowwise(body, name=name, nblk=r // tr, tr=tr, rows=[(flat, 0, c, s * r) for s in range(8)],
                    outs=[(c, F32)])[0]


def _pack(arrs, rows_mult=16):
    flat = jnp.concatenate([a.reshape(-1).astype(F32) for a in arrs])
    nel = flat.shape[0]
    r = -(-nel // PACK_W)
    r = -(-r // rows_mult) * rows_mult
    return jnp.pad(flat, (0, r * PACK_W - nel)).reshape(r, PACK_W)


def _unpack(buf, shapes):
    flat = buf.reshape(-1)
    out, o = [], 0
    for s in shapes:
        nel = int(np.prod(s))
        out.append(flat[o:o + nel].reshape(s))
        o += nel
    return out


def _adamw(g, w, m, v, name):
    r, wd = g.shape
    tr = _pick(r, tuple(t for t in (256, 128, 64, 32, 16, 8) if t * wd <= 262144) or (8,))
    c1 = 1.0 / (1.0 - ADAM_B1 ** ADAM_STEP)
    c2 = 1.0 / (1.0 - ADAM_B2 ** ADAM_STEP)

    def body(gv, wv, mv, vv):
        mn = ADAM_B1 * mv + (1.0 - ADAM_B1) * gv
        vn = ADAM_B2 * vv + (1.0 - ADAM_B2) * (gv * gv)
        delta = -ADAM_LR * ((mn * c1) / (jnp.sqrt(vn * c2) + ADAM_EPS) + ADAM_WD * wv)
        return delta, mn, vn

    return _rowwise(body, name=name, nblk=r // tr, tr=tr, rows=[(a, 0, wd, 0) for a in (g, w, m, v)],
                    outs=[(wd, F32)] * 3)


def kernel(x, c, ctx, c_ctx, w_mod, b_mod, norm1_g, norm2_g, w_in, q_a_g, w_uq, kv_a_g, w_ukv, q_norm_g, k_norm_g, w_o_attn, lam_re_f, lam_im_f, log_dt_f, c_re_f, c_im_f, lam_re_b, lam_im_b, log_dt_b, c_re_b, c_im_b, b_re, b_im, d_skip, w_glu, w_out, w_up, conv_w, conv_b, w_down, loss_target, m_c_ctx, m_w_mod, m_b_mod, m_norm1_g, m_norm2_g, m_w_in, m_q_a_g, m_w_uq, m_kv_a_g, m_w_ukv, m_q_norm_g, m_k_norm_g, m_w_o_attn, m_lam_re_f, m_lam_im_f, m_log_dt_f, m_c_re_f, m_c_im_f, m_lam_re_b, m_lam_im_b, m_log_dt_b, m_c_re_b, m_c_im_b, m_b_re, m_b_im, m_d_skip, m_w_glu, m_w_out, m_w_up, m_conv_w, m_conv_b, m_w_down, v_c_ctx, v_w_mod, v_b_mod, v_norm1_g, v_norm2_g, v_w_in, v_q_a_g, v_w_uq, v_kv_a_g, v_w_ukv, v_q_norm_g, v_k_norm_g, v_w_o_attn, v_lam_re_f, v_lam_im_f, v_log_dt_f, v_c_re_f, v_c_im_f, v_lam_re_b, v_lam_im_b, v_log_dt_b, v_c_re_b, v_c_im_b, v_b_re, v_b_im, v_d_skip, v_w_glu, v_w_out, v_w_up, v_conv_w, v_conv_b, v_w_down):
    weights = dict(c_ctx=c_ctx, w_mod=w_mod, b_mod=b_mod, norm1_g=norm1_g, norm2_g=norm2_g, w_in=w_in, q_a_g=q_a_g, w_uq=w_uq, kv_a_g=kv_a_g, w_ukv=w_ukv, q_norm_g=q_norm_g, k_norm_g=k_norm_g, w_o_attn=w_o_attn, lam_re_f=lam_re_f, lam_im_f=lam_im_f, log_dt_f=log_dt_f, c_re_f=c_re_f, c_im_f=c_im_f, lam_re_b=lam_re_b, lam_im_b=lam_im_b, log_dt_b=log_dt_b, c_re_b=c_re_b, c_im_b=c_im_b, b_re=b_re, b_im=b_im, d_skip=d_skip, w_glu=w_glu, w_out=w_out, w_up=w_up, conv_w=conv_w, conv_b=conv_b, w_down=w_down)
    mom_m = dict(c_ctx=m_c_ctx, w_mod=m_w_mod, b_mod=m_b_mod, norm1_g=m_norm1_g, norm2_g=m_norm2_g, w_in=m_w_in, q_a_g=m_q_a_g, w_uq=m_w_uq, kv_a_g=m_kv_a_g, w_ukv=m_w_ukv, q_norm_g=m_q_norm_g, k_norm_g=m_k_norm_g, w_o_attn=m_w_o_attn, lam_re_f=m_lam_re_f, lam_im_f=m_lam_im_f, log_dt_f=m_log_dt_f, c_re_f=m_c_re_f, c_im_f=m_c_im_f, lam_re_b=m_lam_re_b, lam_im_b=m_lam_im_b, log_dt_b=m_log_dt_b, c_re_b=m_c_re_b, c_im_b=m_c_im_b, b_re=m_b_re, b_im=m_b_im, d_skip=m_d_skip, w_glu=m_w_glu, w_out=m_w_out, w_up=m_w_up, conv_w=m_conv_w, conv_b=m_conv_b, w_down=m_w_down)
    mom_v = dict(c_ctx=v_c_ctx, w_mod=v_w_mod, b_mod=v_b_mod, norm1_g=v_norm1_g, norm2_g=v_norm2_g, w_in=v_w_in, q_a_g=v_q_a_g, w_uq=v_w_uq, kv_a_g=v_kv_a_g, w_ukv=v_w_ukv, q_norm_g=v_q_norm_g, k_norm_g=v_k_norm_g, w_o_attn=v_w_o_attn, lam_re_f=v_lam_re_f, lam_im_f=v_lam_im_f, log_dt_f=v_log_dt_f, c_re_f=v_c_re_f, c_im_f=v_c_im_f, lam_re_b=v_lam_re_b, lam_im_b=v_lam_im_b, log_dt_b=v_log_dt_b, c_re_b=v_c_re_b, c_im_b=v_c_im_b, b_re=v_b_re, b_im=v_b_im, d_skip=v_d_skip, w_glu=v_w_glu, w_out=v_w_out, w_up=v_w_up, conv_w=v_conv_w, conv_b=v_conv_b, w_down=v_w_down)
    names = list(weights)

    nl, d = x.shape[1], x.shape[2]
    nc = ctx.shape[1]
    n = nl + nc
    f2 = conv_b.shape[1]
    fh = f2 // 2
    d6 = b_mod.shape[1]
    mx, my, mc = lax.axis_index("x"), lax.axis_index("y"), lax.axis_index("c")
    chip = 2 * mx + my
    me = 4 * mx + 2 * my + mc
    tr = _pick(math.gcd(nl, nc), (256, 128, 64, 32, 16))
    nlb, nb = nl // tr, n // tr

    big_names = ["w_in", "w_uq", "w_ukv", "w_o_attn", "w_glu", "w_out", "w_up", "w_down"]
    row_sharded = ("w_out", "w_down")
    ffn_names = ["w_up", "w_down"]
    early_names = [k for k in big_names if k not in ffn_names]
    full = {}

    def halves_in(names_):
        return [weights[k][0].astype(MXU_DTYPE).reshape(2, weights[k].shape[1] // 2, weights[k].shape[2]) for k in names_]

    def assemble(names_, my_halves, name):
        gathered = _sibling_exchange([t.reshape(-1, t.shape[2]) for t in my_halves], name)
        for k_, gth in zip(names_, gathered):
            r_, c_ = weights[k_].shape[1:]
            g4 = gth.reshape(2, 4, r_ // 2, c_)
            full[k_] = (jnp.transpose(g4, (1, 0, 2, 3)).reshape(4 * r_, c_) if k_ in row_sharded
                        else jnp.transpose(g4, (0, 2, 1, 3)).reshape(r_, 4 * c_))

    assemble(early_names, _exchange_chips(halves_in(early_names), "gather_weights", True), "gather_weight_halves")

    cwid = conv_w.shape[2]
    sw = -(-max(d, cwid) // 128) * 128
    small_in = jnp.concatenate([jnp.pad(c, ((0, 0), (0, sw - d))), jnp.pad(conv_w[0], ((0, 4), (0, sw - cwid)))], axis=0)
    (small_all,) = _exchange8([small_in], "gather_c", True)
    cs = small_all[:, 0, :d]
    conv_w_full = jnp.concatenate([small_all[2 * j, 1:4, :cwid] for j in range(4)], axis=1)
    cs16 = jnp.concatenate([cs, c_ctx[None, :], jnp.zeros((7, d), F32)], axis=0)

    csh = w_mod.shape[2]
    b_mod_sh = lax.dynamic_slice(b_mod, (0, chip * csh), (1, csh))

    def mod_fwd_body(c_ref, w_ref, b_ref, o_ref):
        a = _silu(c_ref[...]).astype(MXU_DTYPE)
        o_ref[...] = jnp.dot(a, w_ref[...].astype(MXU_DTYPE), preferred_element_type=F32) + b_ref[...]

    mod_sh = pl.pallas_call(mod_fwd_body, name="mod_fwd", out_shape=_sds((16, csh), F32),
                            compiler_params=pltpu.CompilerParams(vmem_limit_bytes=VMEM_LIMIT))(cs16, w_mod[0], b_mod_sh)
    (mod_all,) = _exchange8([mod_sh], "gather_mod", True)
    mod_full = jnp.concatenate([mod_all[2 * j] for j in range(4)], axis=1)
    modv = jnp.stack([lax.dynamic_slice(mod_full, (me, 0), (1, d6)), mod_full[8:9]])

    def mod_parts(m):
        return [m[:, j * d:(j + 1) * d] for j in range(6)]

    u_off, kv_off, kr_off = 2 * d, 2 * d + SSM_WIDTH, 2 * d + SSM_WIDTH + KV_LORA
    q_off = -(-(kr_off + SLOT) // Q_LORA) * Q_LORA
    zw = q_off + Q_LORA
    wi = full["w_in"]
    s0, s1, s2, s3 = Q_LORA, Q_LORA + KV_LORA, Q_LORA + KV_LORA + QK_ROPE, Q_LORA + KV_LORA + QK_ROPE + SSM_WIDTH
    zpad = lambda w_: jnp.zeros((d, w_), MXU_DTYPE)
    win_p = jnp.concatenate([wi[:, s3:], wi[:, s2:s3], wi[:, s0:s1], wi[:, s1:s2], zpad(SLOT - QK_ROPE),
                             zpad(q_off - kr_off - SLOT), wi[:, :s0]], axis=1)
    wuq_p = jnp.pad(full["w_uq"].reshape(Q_LORA, N_HEADS, QK_DIM), ((0, 0), (0, 0), (0, SLOT - QK_DIM))).reshape(Q_LORA, N_HEADS * SLOT)
    wukv3 = full["w_ukv"].reshape(KV_LORA, N_HEADS, QK_NOPE + V_DIM)
    padh = lambda t: jnp.pad(t, ((0, 0), (0, 0), (0, SLOT - t.shape[2]))).reshape(t.shape[0], N_HEADS * SLOT)
    wukv_p = jnp.concatenate([padh(wukv3[:, :, :QK_NOPE]), padh(wukv3[:, :, QK_NOPE:])], axis=1)
    wo_p = jnp.pad(full["w_o_attn"].reshape(N_HEADS, V_DIM, d), ((0, 0), (0, SLOT - V_DIM), (0, 0))).reshape(N_HEADS * SLOT, d)
    wglu, wout = full["w_glu"], full["w_out"]
    hw = N_HEADS * SLOT
    gain_p = lambda g_: jnp.tile(jnp.pad(g_[0], (0, SLOT - QK_DIM)), N_HEADS)[None, :]
    qg_p, kg_p = gain_p(q_norm_g), gain_p(k_norm_g)

    tok = jnp.arange(nl)
    freqs = ROPE_THETA ** (-jnp.arange(QK_ROPE // 4, dtype=F32) / (QK_ROPE // 4))
    ang = jnp.concatenate([(tok // GRID_W)[:, None] * freqs, (tok % GRID_W)[:, None] * freqs], axis=-1)
    cos_t = jnp.concatenate([jnp.cos(ang), jnp.ones((nc, 16), F32)], axis=0)
    sin_t = jnp.concatenate([jnp.sin(ang), jnp.zeros((nc, 16), F32)], axis=0)
    zl = lambda w_: jnp.zeros((n, w_), F32)
    rope_c = jnp.concatenate([jnp.ones((n, QK_NOPE), F32), cos_t, cos_t, zl(SLOT - QK_DIM)], axis=1)
    rope_sa = jnp.concatenate([zl(QK_NOPE), -sin_t, zl(SLOT - QK_NOPE - 16)], axis=1)
    rope_sb = jnp.concatenate([zl(QK_NOPE + 16), sin_t, zl(SLOT - QK_DIM)], axis=1)

    dirs = (("f", lam_re_f, lam_im_f, log_dt_f, c_re_f, c_im_f, False), ("b", lam_re_b, lam_im_b, log_dt_b, c_re_b, c_im_b, True))
    bbd, cbd_t, cbd_n, bbd_t, lamc, lamc_adj, disc_vjps = [], [], [], [], [], [], []
    for _, l_re, l_im, l_dt, cr_, ci_, rev_ in dirs:
        (lbr, lbi, bbr, bbi), vjp = jax.vjp(_s5_disc, l_re[0], l_im[0], l_dt[0], b_re[0], b_im[0])
        disc_vjps.append(vjp)
        bb = _diag_blocks(jnp.transpose(bbr, (0, 2, 1)), jnp.transpose(bbi, (0, 2, 1))).astype(MXU_DTYPE)
        cc_ = _diag_blocks(cr_[0], -ci_[0]).astype(MXU_DTYPE)
        bbd.append(bb)
        bbd_t.append(_block_transpose(bb))
        cbd_t.append(cc_)
        cbd_n.append(_block_transpose(cc_))
        lamc.append(_lam_consts(lbr, lbi, rev_, False))
        lamc_adj.append(_lam_consts(lbr, lbi, not rev_, True))
    t_scan = tr

    xa = jnp.concatenate([x[0], ctx[0]], axis=0)
    n1g, n2g = norm1_g, norm2_g

    def norm1_body(xv, m, g):
        sh1, sc1 = m[:, :d], m[:, d:2 * d]
        return _rms_fwd(xv, g, d) * (1.0 + sc1) + sh1

    (h1,) = _rowwise(norm1_body, name="norm1_fwd", nblk=nb, tr=tr, rows=[(xa, 0, d, 0)], sels=[modv], fulls=[n1g],
                     outs=[(d, MXU_DTYPE)], seg=nlb)
    z = _mm(h1, win_p, "nn", "in_proj")
    gl_cb, u_cb, kv_cb, kr_cb, q_cb = 0, u_off // SSM_WIDTH, kv_off // KV_LORA, kr_off // SLOT, q_off // Q_LORA

    (cqn,) = _rowwise(lambda v, g: _rms_fwd(v, g, Q_LORA), name="qa_norm_fwd", nblk=nlb, tr=tr,
                      rows=[(z, q_cb, Q_LORA, 0)], fulls=[q_a_g], outs=[(Q_LORA, MXU_DTYPE)])
    qh = _mm(cqn, wuq_p, "nn", "q_up")

    def qhead_body(qv, cv, sav, sbv, g):
        return jnp.concatenate([_rope_fwd(_rms_fwd(t, g[:, :SLOT], QK_DIM), cv, sav, sbv) for t in _heads(qv)], axis=1)

    rope_rows = lambda: [(rope_c, 0, SLOT, 0), (rope_sa, 0, SLOT, 0), (rope_sb, 0, SLOT, 0)]
    (q_p,) = _rowwise(qhead_body, name="q_head_fwd", nblk=nlb, tr=tr, rows=[(qh, 0, hw, 0)] + rope_rows(),
                      fulls=[qg_p], outs=[(hw, MXU_DTYPE)])

    (ckvn,) = _rowwise(lambda v, g: _rms_fwd(v, g, KV_LORA), name="kva_norm_fwd", nblk=nb, tr=tr,
                       rows=[(z, kv_cb, KV_LORA, 0)], fulls=[kv_a_g], outs=[(KV_LORA, MXU_DTYPE)])
    kvpre = _mm(ckvn, wukv_p, "nn", "kv_up")

    def khead_body(kv_, vv_, krv, cv, sav, sbv, g):
        kpe = pltpu.roll(krv, QK_NOPE, 1)
        ks = [_rope_fwd(_rms_fwd(t + kpe, g[:, :SLOT], QK_DIM), cv, sav, sbv) for t in _heads(kv_)]
        return jnp.concatenate(ks, axis=1), vv_

    k_p, v_p = _rowwise(khead_body, name="k_head_fwd", nblk=nb, tr=tr,
                        rows=[(kvpre, 0, hw, 0), (kvpre, 1, hw, 0), (z, kr_cb, SLOT, 0)] + rope_rows(),
                        fulls=[kg_p], outs=[(hw, MXU_DTYPE), (hw, MXU_DTYPE)])

    scale = QK_DIM ** -0.5
    o_p, lse, ffn_halves = _attn_fwd(q_p, k_p, v_p, nl, scale, side=(halves_in(ffn_names), True))
    assemble(ffn_names, ffn_halves, "gather_ffn_weight_halves")
    wup, wdown = full["w_up"], full["w_down"]
    a_l = _mm(o_p, wo_p, "nn", "attn_out")

    scans = [_s5_scan(z, u_cb, bbd[j], cbd_n[j], lamc[j], t_scan, nl, dirs[j][6], "s5_scan_" + dirs[j][0]) for j in range(2)]
    xs, ydir = [s_[0] for s_ in scans], [s_[1] for s_ in scans]

    def ssm_out_body(uv, a, b, dsk):
        ys = uv * dsk + a + b
        return ys, _gelu(ys)

    ys, ge = _rowwise(ssm_out_body, name="s5_out_fwd", nblk=nlb, tr=tr,
                      rows=[(z, u_cb, SSM_WIDTH, 0), (ydir[0], 0, SSM_WIDTH, 0), (ydir[1], 0, SSM_WIDTH, 0)],
                      fulls=[d_skip], outs=[(SSM_WIDTH, F32), (SSM_WIDTH, MXU_DTYPE)])
    glu_out = _mm(ge, wglu, "nn", "glu_proj")

    def merge_body(ga, gs, av, val, gate):
        return _sigmoid(ga) * av + _sigmoid(gs) * (val * _sigmoid(gate))

    merge_rows = lambda: [(z, 0, d, 0), (z, 1, d, 0), (a_l, 0, d, 0), (glu_out, 0, d, 0), (glu_out, 1, d, 0)]
    (merged,) = _rowwise(merge_body, name="merge_fwd", nblk=nlb, tr=tr, rows=merge_rows(), outs=[(d, MXU_DTYPE)])
    mo = _mm(merged, wout, "nn", "out_proj")
    mod_x = modv[0]

    def norm2_body(xv, mov, m, g):
        g1, sh2, sc2 = m[:, 2 * d:3 * d], m[:, 3 * d:4 * d], m[:, 4 * d:5 * d]
        x1v = xv + g1 * mov
        return x1v, _rms_fwd(x1v, g, d) * (1.0 + sc2) + sh2

    x1, h2 = _rowwise(norm2_body, name="norm2_fwd", nblk=nlb, tr=tr, rows=[(xa, 0, d, 0), (mo, 0, d, 0)],
                      fulls=[mod_x, n2g], outs=[(d, F32), (d, MXU_DTYPE)])
    up = _mm(h2, wup, "nn", "ffn_up")
    cw8 = jnp.zeros((8, f2), F32).at[:3].set(conv_w_full)

    def conv3(t3, w8, off):
        p_, c_, n_ = t3
        return p_ * w8[0:1, off:off + fh] + c_ * w8[1:2, off:off + fh] + n_ * w8[2:3, off:off + fh]

    def conv_fwd_body(val3, gate3, w8, bias):
        val2 = conv3(val3, w8, 0) + bias[:, :fh]
        gate2 = conv3(gate3, w8, fh) + bias[:, fh:]
        return _silu(gate2) * val2

    (act,) = _rowwise(conv_fwd_body, name="conv_fwd", nblk=nlb, tr=tr, rows=[(up, 0, fh, 0), (up, 1, fh, 0)],
                      halo=(0, 1), fulls=[cw8, conv_b], outs=[(fh, MXU_DTYPE)])
    dn = _mm(act, wdown, "nn", "ffn_down")
    tgt = loss_target[0]

    def loss_body(x1v, dnv, tv, m):
        g2 = m[:, 5 * d:6 * d]
        e = x1v + g2 * dnv - tv
        dx2v = e * (1.0 / d)
        return dx2v, dx2v * g2, e * e, dx2v * dnv

    dx2, ddn, loss_acc, dg2_acc = _rowwise(loss_body, name="loss", nblk=nlb, tr=tr,
                                           rows=[(x1, 0, d, 0), (dn, 0, d, 0), (tgt, 0, d, 0)], fulls=[mod_x],
                                           outs=[(d, F32), (d, MXU_DTYPE)], accs=[d, d])
    loss = lax.psum(0.5 / d * jnp.sum(loss_acc), ("x", "y", "c"))

    g_big = {}

    def chip_partials(names_, name):
        pcs = []
        for k_ in names_:
            r_, c_ = weights[k_].shape[1:]
            if k_ in row_sharded:
                p4 = jnp.transpose(g_big[k_].reshape(4, 2, r_ // 2, c_), (1, 0, 2, 3))
            else:
                p4 = jnp.transpose(g_big[k_].reshape(2, r_ // 2, 4, c_), (0, 2, 1, 3))
            pcs.append(p4.reshape(2, 2 * r_, c_))
        out = []
        for k_, p_, got in zip(names_, pcs, _sibling_send(pcs, name)):
            rows4, c_ = got.shape
            own = lax.dynamic_index_in_dim(p_, mc, 0, keepdims=False)
            tr_ = _pick(rows4, (256, 128, 64, 32, 16))
            s32, sb = _rowwise(lambda a, b: (a + b, a + b), name="sum_chip_" + k_, nblk=rows4 // tr_, tr=tr_,
                               rows=[(own, 0, c_, 0), (got, 0, c_, 0)], outs=[(c_, F32), (c_, MXU_DTYPE)])
            out.append((s32, sb, rows4 // 4, c_))
        return out

    def scatter_in(partials):
        return [sb.reshape(4, rh, c_) for _, sb, rh, c_ in partials]

    def chip_sums(names_, partials, recv3):
        out = []
        for k_, (s32, _, rh, c_), r3 in zip(names_, partials, recv3):
            mine = lax.dynamic_slice(s32, (chip * rh, 0), (rh, c_))
            tr_ = _pick(rh, (256, 128, 64, 32, 16))
            (red,) = _rowwise(lambda a, b0, b1, b2: a + b0 + b1 + b2, name="sum_grad_" + k_, nblk=rh // tr_, tr=tr_,
                              rows=[(mine, 0, c_, 0)] + [(r3.reshape(3 * rh, c_), 0, c_, j * rh) for j in range(3)],
                              outs=[(c_, F32)])
            out.append(red)
        return out

    dact = _mm(ddn, wdown, "nt", "ffn_down_dx")
    g_big["w_down"] = _mm(act, ddn, "tn", "ffn_down_dw")

    def conv_bwd_body(val3, gate3, da, w8, bias):
        val2 = conv3(val3, w8, 0) + bias[:, :fh]
        gate2 = conv3(gate3, w8, fh) + bias[:, fh:]
        sg = _sigmoid(gate2)
        dval2 = da * (gate2 * sg)
        dgate2 = da * val2 * (sg * (1.0 + gate2 * (1.0 - sg)))
        du2 = jnp.concatenate([dval2, dgate2], axis=1)
        taps = [jnp.concatenate([dval2 * val3[j], dgate2 * gate3[j]], axis=1) for j in range(3)]
        return du2, du2, taps[0], taps[1], taps[2]

    du2, dcb_acc, dcw0, dcw1, dcw2 = _rowwise(conv_bwd_body, name="conv_bwd", nblk=nlb, tr=tr,
                                              rows=[(up, 0, fh, 0), (up, 1, fh, 0), (dact, 0, fh, 0)], halo=(0, 1),
                                              fulls=[cw8, conv_b], outs=[(f2, F32)], accs=[f2, f2, f2, f2])

    def conv_t_body(dval3, dgate3, w8):
        rev = lambda t3: (t3[2], t3[1], t3[0])
        return jnp.concatenate([conv3(rev(dval3), w8, 0), conv3(rev(dgate3), w8, fh)], axis=1)

    (dup,) = _rowwise(conv_t_body, name="conv_bwd_dx", nblk=nlb, tr=tr, rows=[(du2, 0, fh, 0), (du2, 1, fh, 0)],
                      halo=(0, 1), fulls=[cw8], outs=[(f2, MXU_DTYPE)])
    dh2 = _mm(dup, wup, "nt", "ffn_up_dx")
    g_big["w_up"] = _mm(h2, dup, "tn", "ffn_up_dw")
    ffn_partials = chip_partials(ffn_names, "ffn_grads_to_sibling")

    def norm2_bwd_body(x1v, dh, dx2v, mov, m, g):
        g1, sc2 = m[:, 2 * d:3 * d], m[:, 4 * d:5 * d]
        y = _rms_fwd(x1v, g, d)
        dxn, dgc = _rms_bwd(x1v, g, dh * (1.0 + sc2), d)
        dx1v = dx2v + dxn
        return dx1v, dx1v * g1, dgc, dh, dh * y, dx1v * mov

    dx1, dmo, dn2g_acc, dsh2_acc, dsc2_acc, dg1_acc = _rowwise(
        norm2_bwd_body, name="norm2_bwd", nblk=nlb, tr=tr,
        rows=[(x1, 0, d, 0), (dh2, 0, d, 0), (dx2, 0, d, 0), (mo, 0, d, 0)], fulls=[mod_x, n2g],
        outs=[(d, F32), (d, MXU_DTYPE)], accs=[d, d, d, d])
    dmerged = _mm(dmo, wout, "nt", "out_proj_dx")
    g_big["w_out"] = _mm(merged, dmo, "tn", "out_proj_dw")

    def merge_bwd_body(ga, gs, av, val, gate, dm):
        sa_, ss_, sg_ = _sigmoid(ga), _sigmoid(gs), _sigmoid(gate)
        s_l = val * sg_
        ds_l = dm * ss_
        dga = dm * av * sa_ * (1.0 - sa_)
        dgs = dm * s_l * ss_ * (1.0 - ss_)
        dval = ds_l * sg_
        dgate = ds_l * val * sg_ * (1.0 - sg_)
        return dm * sa_, jnp.concatenate([dval, dgate], axis=1), jnp.concatenate([dga, dgs], axis=1)

    da_l, dglu, dgl = _rowwise(merge_bwd_body, name="merge_bwd", nblk=nlb, tr=tr,
                               rows=merge_rows() + [(dmerged, 0, d, 0)],
                               outs=[(d, MXU_DTYPE), (2 * d, MXU_DTYPE), (2 * d, MXU_DTYPE)])
    dge = _mm(dglu, wglu, "nt", "glu_proj_dx")
    g_big["w_glu"] = _mm(ge, dglu, "tn", "glu_proj_dw")

    def ssm_out_bwd_body(ysv, dgev, uv, dsk):
        dys_ = dgev * _dgelu(ysv)
        return dys_, dys_ * dsk, dys_ * uv

    dys, du_skip, ddskip_acc = _rowwise(ssm_out_bwd_body, name="s5_out_bwd", nblk=nlb, tr=tr,
                                        rows=[(ys, 0, SSM_WIDTH, 0), (dge, 0, SSM_WIDTH, 0), (z, u_cb, SSM_WIDTH, 0)],
                                        fulls=[d_skip], outs=[(SSM_WIDTH, F32), (SSM_WIDTH, F32)], accs=[SSM_WIDTH])
    s5b = [_s5_bwd(dys, z, u_cb, xs[j], cbd_t[j], bbd_t[j], lamc_adj[j], t_scan, nl, dirs[j][6], "s5_bwd_" + dirs[j][0])
           for j in range(2)]
    du_nat = s5b[0][0] + s5b[1][0] + jnp.concatenate([du_skip, jnp.zeros((nc, SSM_WIDTH), F32)], axis=0)

    do_f = _mm(da_l, wo_p, "nt", "attn_out_dx")
    g_wo_p = _mm(o_p, da_l, "tn", "attn_out_dw")

    dq_p, dk_p, dv_p, ffn_recv = _attn_bwd(q_p, k_p, v_p, do_f, o_p, lse, nl, scale, side=(scatter_in(ffn_partials), False))

    def qhead_bwd_body(qv, dqv, cv, sav, sbv, g):
        dxs, dgs = [], []
        for t, dt_ in zip(_heads(qv), _heads(dqv)):
            dx_, dg_ = _rms_bwd(t, g[:, :SLOT], _rope_bwd(dt_, cv, sav, sbv), QK_DIM)
            dxs.append(dx_)
            dgs.append(dg_)
        return jnp.concatenate(dxs, axis=1), jnp.concatenate(dgs, axis=1)

    dqh, dqg_acc = _rowwise(qhead_bwd_body, name="q_head_bwd", nblk=nlb, tr=tr,
                            rows=[(qh, 0, hw, 0), (dq_p, 0, hw, 0)] + rope_rows(), fulls=[qg_p],
                            outs=[(hw, MXU_DTYPE)], accs=[hw])
    dcqn = _mm(dqh, wuq_p, "nt", "q_up_dx")
    g_wuq_p = _mm(cqn, dqh, "tn", "q_up_dw")
    dcq, dqag_acc = _rowwise(lambda v, dy, g: _rms_bwd(v, g, dy, Q_LORA), name="qa_norm_bwd", nblk=nlb, tr=tr,
                             rows=[(z, q_cb, Q_LORA, 0), (dcqn, 0, Q_LORA, 0)], fulls=[q_a_g],
                             outs=[(Q_LORA, MXU_DTYPE)], accs=[Q_LORA])

    def khead_bwd_body(kv_, krv, dkv_, dvv_, cv, sav, sbv, g):
        kpe = pltpu.roll(krv, QK_NOPE, 1)
        lane = lax.broadcasted_iota(jnp.int32, krv.shape, 1)
        dxs, dgs, dkr_ = [], [], jnp.zeros(krv.shape, F32)
        for t, dt_ in zip(_heads(kv_), _heads(dkv_)):
            dx_, dg_ = _rms_bwd(t + kpe, g[:, :SLOT], _rope_bwd(dt_, cv, sav, sbv), QK_DIM)
            dxs.append(jnp.where(lane < QK_NOPE, dx_, 0.0))
            dgs.append(dg_)
            dkr_ = dkr_ + dx_
        dkr_ = jnp.where(lane < QK_ROPE, pltpu.roll(dkr_, SLOT - QK_NOPE, 1), 0.0)
        return jnp.concatenate(dxs + [dvv_], axis=1), dkr_, jnp.concatenate(dgs, axis=1)

    dkvpre, dkr, dkg_acc = _rowwise(khead_bwd_body, name="k_head_bwd", nblk=nb, tr=tr,
                                    rows=[(kvpre, 0, hw, 0), (z, kr_cb, SLOT, 0), (dk_p, 0, hw, 0), (dv_p, 0, hw, 0)] + rope_rows(),
                                    fulls=[kg_p], outs=[(2 * hw, MXU_DTYPE), (SLOT, MXU_DTYPE)], accs=[hw])
    dckvn = _mm(dkvpre, wukv_p, "nt", "kv_up_dx")
    g_wukv_p = _mm(ckvn, dkvpre, "tn", "kv_up_dw")
    dckv, dkvag_acc = _rowwise(lambda v, dy, g: _rms_bwd(v, g, dy, KV_LORA), name="kva_norm_bwd", nblk=nb, tr=tr,
                               rows=[(z, kv_cb, KV_LORA, 0), (dckvn, 0, KV_LORA, 0)], fulls=[kv_a_g],
                               outs=[(KV_LORA, MXU_DTYPE)], accs=[KV_LORA])

    padc = lambda t: jnp.concatenate([t, jnp.zeros((nc, t.shape[1]), t.dtype)], axis=0)
    dz = jnp.concatenate([padc(dgl), du_nat.astype(MXU_DTYPE), dckv, dkr,
                          jnp.zeros((n, q_off - kr_off - SLOT), MXU_DTYPE), padc(dcq)], axis=1)
    dh1 = _mm(dz, win_p, "nt", "in_proj_dx")
    g_win_p = _mm(h1, dz, "tn", "in_proj_dw")

    def norm1_bwd_body(xv, dh, dx1v, m, g):
        sc1 = m[:, d:2 * d]
        y = _rms_fwd(xv, g, d)
        dxn, dgc = _rms_bwd(xv, g, dh * (1.0 + sc1), d)
        return dxn + dx1v, dgc, dh, dh * y

    dxa, dn1g_acc, dsh1_acc, dsc1_acc = _rowwise(norm1_bwd_body, name="norm1_bwd", nblk=nb, tr=tr,
                                                 rows=[(xa, 0, d, 0), (dh1, 0, d, 0), (dx1, 0, d, 0)], sels=[modv],
                                                 fulls=[n1g], outs=[(d, F32)], accs=[d, d, d], seg=nlb)
    grad_x = dxa[:nl][None]

    red8 = lambda a: jnp.sum(a, axis=-2)
    dmod_own = jnp.concatenate([red8(dsh1_acc[0]), red8(dsc1_acc[0]), red8(dg1_acc), red8(dsh2_acc), red8(dsc2_acc), red8(dg2_acc)])
    dmod_ctx = jnp.concatenate([red8(dsh1_acc[1]), red8(dsc1_acc[1]), jnp.zeros((4 * d,), F32)])
    dm_in = jnp.concatenate([dmod_own[None, :], dmod_ctx[None, :], jnp.zeros((6, d6), F32)], axis=0)
    (dm_all,) = _exchange8([dm_in], "gather_dmod", True)
    dm_own_sh = lax.dynamic_slice(dm_all[:, 0, :], (0, chip * csh), (8, csh))
    dm_ctx_sh = lax.dynamic_slice(dm_all[:, 1, :], (0, chip * csh), (8, csh))

    def mod_bwd_body(c_ref, own_ref, ctx_ref, w_ref, gw_ref, gb_ref, gc_ref):
        cv = c_ref[...]
        a = _silu(cv).astype(MXU_DTYPE)
        own = own_ref[...]
        ctx_tot = ctx_ref[0:1, :]
        for j in range(1, 8):
            ctx_tot = ctx_tot + ctx_ref[j:j + 1, :]
        g16 = jnp.concatenate([own, jnp.broadcast_to(ctx_tot, own.shape)], axis=0)
        rid = lax.broadcasted_iota(jnp.int32, g16.shape, 0)
        g16 = jnp.where(rid <= 8, g16, 0.0)
        gw_ref[...] = lax.dot_general(a, g16.astype(MXU_DTYPE), (((0,), (0,)), ((), ())), preferred_element_type=F32)
        gb_ref[...] = jnp.broadcast_to(jnp.sum(own, axis=0, keepdims=True) + ctx_tot, gb_ref.shape)
        gc = lax.dot_general(jnp.broadcast_to(ctx_tot, own.shape).astype(MXU_DTYPE), w_ref[...].astype(MXU_DTYPE),
                             (((1,), (1,)), ((), ())), preferred_element_type=F32)
        gc_ref[...] = gc * _dsilu(cv[8:9, :])

    g_wmod, g_bmod_sh, g_cctx_part = pl.pallas_call(
        mod_bwd_body, name="mod_bwd", out_shape=[_sds((d, csh), F32), _sds((8, csh), F32), _sds((8, d), F32)],
        compiler_params=pltpu.CompilerParams(vmem_limit_bytes=VMEM_LIMIT))(cs16, dm_own_sh, dm_ctx_sh, w_mod[0])
    north = (mc == 0).astype(F32)
    g_bmod_part = lax.dynamic_update_slice(jnp.zeros((1, d6), F32), g_bmod_sh[0:1] * north, (0, chip * csh))
    g_cctx_part = g_cctx_part[0] * north

    small_g = {}
    for j, dr in enumerate(dirs):
        sfx = dr[0]
        _, dbbd_j, dcbd_j, dlam_j = s5b[j]
        dl = red8(dlam_j).reshape(N_CG, 2, CG_STATES)
        db_re, db_im = _diag_extract(dbbd_j)
        cot = (dl[:, 0].reshape(SSM_GROUPS, SSM_STATE), dl[:, 1].reshape(SSM_GROUPS, SSM_STATE),
               jnp.transpose(db_re, (0, 2, 1)), jnp.transpose(db_im, (0, 2, 1)))
        g_lre, g_lim, g_ldt, g_bre, g_bim = disc_vjps[j](cot)
        small_g["lam_re_" + sfx], small_g["lam_im_" + sfx], small_g["log_dt_" + sfx] = g_lre, g_lim, g_ldt
        small_g["b_re"] = small_g.get("b_re", 0.0) + g_bre
        small_g["b_im"] = small_g.get("b_im", 0.0) + g_bim
        dc_re, dc_im = _diag_extract(dcbd_j)
        small_g["c_re_" + sfx], small_g["c_im_" + sfx] = dc_re, -dc_im
    head_fold = lambda acc: jnp.sum(red8(acc).reshape(N_HEADS, SLOT), axis=0)[:QK_DIM]
    small_g.update(c_ctx=g_cctx_part, b_mod=g_bmod_part[0], norm1_g=red8(dn1g_acc[0]) + red8(dn1g_acc[1]),
                   norm2_g=red8(dn2g_acc), q_a_g=red8(dqag_acc), kv_a_g=red8(dkvag_acc), q_norm_g=head_fold(dqg_acc),
                   k_norm_g=head_fold(dkg_acc), d_skip=red8(ddskip_acc), conv_b=red8(dcb_acc))
    g_convw_full = jnp.stack([red8(dcw0), red8(dcw1), red8(dcw2)])
    small_names = ["c_ctx", "b_mod", "norm1_g", "norm2_g", "q_a_g", "kv_a_g", "q_norm_g", "k_norm_g",
                   "lam_re_f", "lam_im_f", "log_dt_f", "c_re_f", "c_im_f", "lam_re_b", "lam_im_b", "log_dt_b",
                   "c_re_b", "c_im_b", "b_re", "b_im", "d_skip", "conv_b"]
    small_shapes = [weights[k].shape for k in small_names]
    spack = _pack([small_g[k] for k in small_names] + [g_convw_full], rows_mult=8)
    sred = _sum8(_exchange8([spack], "gather_small_grads", True)[0], "sum_small_grads")
    sg_list = _unpack(sred, small_shapes + [(3, f2)])
    g_small = dict(zip(small_names, sg_list[:-1]))
    g_small["conv_w"] = lax.dynamic_slice(sg_list[-1], (0, chip * cwid), (3, cwid))[None]

    gwi = g_win_p
    g_big["w_in"] = jnp.concatenate([gwi[:, q_off:q_off + Q_LORA], gwi[:, kv_off:kv_off + KV_LORA],
                                     gwi[:, kr_off:kr_off + QK_ROPE], gwi[:, u_off:u_off + SSM_WIDTH], gwi[:, :2 * d]], axis=1)
    g_big["w_uq"] = g_wuq_p.reshape(Q_LORA, N_HEADS, SLOT)[:, :, :QK_DIM].reshape(Q_LORA, N_HEADS * QK_DIM)
    gk3 = g_wukv_p[:, :hw].reshape(KV_LORA, N_HEADS, SLOT)[:, :, :QK_NOPE]
    gv3 = g_wukv_p[:, hw:].reshape(KV_LORA, N_HEADS, SLOT)[:, :, :V_DIM]
    g_big["w_ukv"] = jnp.concatenate([gk3, gv3], axis=2).reshape(KV_LORA, N_HEADS * (QK_NOPE + V_DIM))
    g_big["w_o_attn"] = g_wo_p.reshape(N_HEADS, SLOT, d)[:, :V_DIM].reshape(N_HEADS * V_DIM, d)

    early_partials = chip_partials(early_names, "grads_to_sibling")
    early_recv = _exchange_chips(scatter_in(early_partials), "scatter_weight_grads", False)
    reduced = dict(zip(early_names + ffn_names, chip_sums(early_names, early_partials, early_recv)
                       + chip_sums(ffn_names, ffn_partials, ffn_recv)))
    both = _sibling_exchange([reduced[k_] for k_ in big_names], "exchange_halves")
    g_sh = {k_: b_.reshape((1,) + weights[k_].shape[1:]) for k_, b_ in zip(big_names, both)}
    g_sh["w_mod"] = g_wmod[None]

    grads = {**g_sh, **g_small}
    outs_d, outs_m, outs_v = {}, {}, {}
    for k_ in ["w_mod"] + big_names:
        shp = weights[k_].shape
        res = _adamw(*[t.reshape(shp[1:]) for t in (grads[k_], weights[k_], mom_m[k_], mom_v[k_])], "adamw_" + k_)
        for dst, buf in zip((outs_d, outs_m, outs_v), res):
            dst[k_] = buf.reshape(shp)
    adam_small = small_names + ["conv_w"]
    shapes = [weights[k_].shape for k_ in adam_small]
    res = _adamw(*[_pack([src[k_] for k_ in adam_small], rows_mult=8) for src in (grads, weights, mom_m, mom_v)], "adamw_small")
    for dst, buf in zip((outs_d, outs_m, outs_v), res):
        dst.update(zip(adam_small, _unpack(buf, shapes)))
    grads = {k_: grads[k_].reshape(weights[k_].shape) for k_ in names}
    return (loss, grad_x, *[grads[k_] for k_ in names], *[outs_d[k_] for k_ in names],
            *[outs_m[k_] for k_ in names], *[outs_v[k_] for k_ in names])
```

```python
import functools
import math

import numpy as np
import jax
import jax.numpy as jnp
from jax import lax
from jax.experimental import pallas as pl
from jax.experimental.pallas import tpu as pltpu

F32 = jnp.float32
MXU_DTYPE = jnp.bfloat16
MESH = pl.DeviceIdType.MESH

EPS = 1e-6
N_HEADS = 8
QK_NOPE = 64
QK_ROPE = 32
QK_DIM = QK_NOPE + QK_ROPE
V_DIM = 64
SLOT = 128
Q_LORA = 384
KV_LORA = 256
GRID_W = 64
ROPE_THETA = 10000.0
SSM_WIDTH = 512
SSM_GROUP = 16
SSM_GROUPS = 32
SSM_STATE = 64
N_STATE = SSM_GROUPS * SSM_STATE
CG_STATES = 512
N_CG = N_STATE // CG_STATES
CG_CHANNELS = SSM_WIDTH // N_CG
SCAN_LANES = 512
PACK_W = 1024

ADAM_LR = 0.001
ADAM_B1 = 0.9
ADAM_B2 = 0.999
ADAM_EPS = 1e-08
ADAM_WD = 0.01
ADAM_STEP = 10

VMEM_LIMIT = 56 * 1024 * 1024
LOG2E = 1.4426950408889634


def _pick(n, cands):
    for c in cands:
        if c <= n and n % c == 0:
            return c
    return n


def _cparams(sem):
    return pltpu.CompilerParams(dimension_semantics=sem, vmem_limit_bytes=VMEM_LIMIT)


def _sds(shape, dtype):
    return jax.ShapeDtypeStruct(tuple(shape), dtype)


_K_CANDS = (2816, 2048, 1536, 1408, 1280, 1152, 1024, 896, 768, 704, 640, 512, 384, 256, 128, 64, 32, 16)
_M_CANDS = (2048, 1408, 1024, 768, 512, 384, 256, 128, 64, 32, 16)
_N_CANDS = (1408, 1152, 1024, 768, 512, 384, 256, 128)
MM_VMEM_BUDGET = 40 * 1024 * 1024


def _mm_tiles(m, n, k_opts, a_bytes, b_bytes, o_bytes, m_cands):
    tn = n if n <= _N_CANDS[0] else _pick(n, _N_CANDS)
    for tk in k_opts:
        for tm in ((m,) if m <= m_cands[0] else ()) + tuple(t for t in m_cands if t < m and m % t == 0):
            if 2 * (tm * tk * a_bytes + tk * tn * b_bytes + tm * tn * o_bytes) + tm * tn * 4 <= MM_VMEM_BUDGET:
                return tm, tn, tk
    raise ValueError("no matmul tiling fits")


def _mm(a, b, mode, name, out_dtype=F32, rows=None, a_off=0, b_off=0):
    a_bytes, b_bytes, o_bytes = a.dtype.itemsize, b.dtype.itemsize, jnp.dtype(out_dtype).itemsize
    if mode == "tn":
        t_rows = rows or a.shape[0]
        m, n = a.shape[1], b.shape[1]
        k_opts = tuple(t for t in _K_CANDS if t <= t_rows and t_rows % t == 0) or (t_rows,)
        tm, tn, tk = _mm_tiles(m, n, k_opts, a_bytes, b_bytes, o_bytes, _M_CANDS[1:])
        nk = t_rows // tk
        ao, bo = a_off // tk, b_off // tk
        grid = (m // tm, n // tn, nk)
        in_specs = [pl.BlockSpec((tk, tm), lambda i, j, k: (k + ao, i)),
                    pl.BlockSpec((tk, tn), lambda i, j, k: (k + bo, j))]
        dn = (((0,), (0,)), ((), ()))
    else:
        m = rows or a.shape[0]
        kdim = a.shape[1]
        n = b.shape[1] if mode == "nn" else b.shape[0]
        k_opts = (kdim,) + tuple(t for t in _K_CANDS if t < kdim and kdim % t == 0)
        tm, tn, tk = _mm_tiles(m, n, k_opts, a_bytes, b_bytes, o_bytes, _M_CANDS)
        nk = kdim // tk
        ao = a_off // tm
        grid = (m // tm, n // tn, nk)
        if mode == "nn":
            in_specs = [pl.BlockSpec((tm, tk), lambda i, j, k: (i + ao, k)),
                        pl.BlockSpec((tk, tn), lambda i, j, k: (k, j))]
            dn = (((1,), (0,)), ((), ()))
        else:
            in_specs = [pl.BlockSpec((tm, tk), lambda i, j, k: (i + ao, k)),
                        pl.BlockSpec((tn, tk), lambda i, j, k: (j, k))]
            dn = (((1,), (1,)), ((), ()))
    use_scratch = nk > 1 and out_dtype != F32

    def body(a_ref, b_ref, o_ref, *scr):
        r = lax.dot_general(a_ref[...].astype(MXU_DTYPE), b_ref[...].astype(MXU_DTYPE), dn,
                            preferred_element_type=F32)
        if nk == 1:
            o_ref[...] = r.astype(o_ref.dtype)
        else:
            k = pl.program_id(2)
            acc = scr[0] if use_scratch else o_ref

            @pl.when(k == 0)
            def _():
                acc[...] = r

            @pl.when(k > 0)
            def _():
                acc[...] += r

            if use_scratch:
                @pl.when(k == nk - 1)
                def _():
                    o_ref[...] = acc[...].astype(o_ref.dtype)

    return pl.pallas_call(
        body, name=name, grid=grid, in_specs=in_specs,
        out_specs=pl.BlockSpec((tm, tn), lambda i, j, k: (i, j)),
        out_shape=_sds((m, n), out_dtype),
        scratch_shapes=[pltpu.VMEM((tm, tn), F32)] if use_scratch else [],
        compiler_params=_cparams(("parallel", "parallel", "arbitrary")),
    )(a, b)


def _rowwise(body, *, name, nblk, tr, rows=(), halo=(), sels=(), fulls=(), outs=(), accs=(), seg=None):
    n_rows, n_sel, n_full, n_out, n_acc = len(rows), len(sels), len(fulls), len(outs), len(accs)
    halo = tuple(halo)
    maxw = max([r[2] for r in rows] + [o[0] for o in outs] + list(accs))
    sr = _pick(tr, tuple(s for s in (256, 128, 64, 32, 16) if s * maxw <= 131072) or (16,))
    nsub = tr // sr
    total8 = nblk * tr // 8

    def seg_of(i):
        return jnp.where(i >= seg, 1, 0) if seg is not None else 0

    in_specs, operands = [], []
    for arr, cb, w, roff in rows:
        ob = roff // tr
        last = arr.shape[0] // tr - 1
        in_specs.append(pl.BlockSpec((tr, w), lambda i, cb=cb, ob=ob, last=last: (jnp.minimum(i + ob, last), cb)))
        operands.append(arr)
    for h in halo:
        arr, cb, w, roff = rows[h]
        o8, t8 = roff // 8, tr // 8
        in_specs.append(pl.BlockSpec((8, w), lambda i, cb=cb, o8=o8, t8=t8: (jnp.maximum(i * t8 - 1, 0) + o8, cb)))
        in_specs.append(pl.BlockSpec((8, w), lambda i, cb=cb, o8=o8, t8=t8: (jnp.minimum((i + 1) * t8, total8 - 1) + o8, cb)))
        operands += [arr, arr]
    for arr in sels:
        in_specs.append(pl.BlockSpec((None,) + arr.shape[1:], lambda i: (seg_of(i), 0, 0)))
        operands.append(arr)
    for arr in fulls:
        in_specs.append(pl.BlockSpec(arr.shape, lambda i: (0, 0)))
        operands.append(arr)
    out_specs, out_shape = [], []
    for w, dt in outs:
        out_specs.append(pl.BlockSpec((tr, w), lambda i: (i, 0)))
        out_shape.append(_sds((nblk * tr, w), dt))
    for w in accs:
        if seg is None:
            out_specs.append(pl.BlockSpec((8, w), lambda i: (0, 0)))
            out_shape.append(_sds((8, w), F32))
        else:
            out_specs.append(pl.BlockSpec((None, 8, w), lambda i: (seg_of(i), 0, 0)))
            out_shape.append(_sds((2, 8, w), F32))
    n_halo = 2 * len(halo)

    def kern(*refs):
        row_refs = refs[:n_rows]
        halo_refs = refs[n_rows:n_rows + n_halo]
        sel_refs = refs[n_rows + n_halo:n_rows + n_halo + n_sel]
        full_refs = refs[n_rows + n_halo + n_sel:n_rows + n_halo + n_sel + n_full]
        o0 = n_rows + n_halo + n_sel + n_full
        out_refs = refs[o0:o0 + n_out]
        acc_refs = refs[o0 + n_out:o0 + n_out + n_acc]
        i = pl.program_id(0)
        if n_acc:
            first = (i == 0) if seg is None else ((i == 0) | (i == seg))

            @pl.when(first)
            def _():
                for a_ref in acc_refs:
                    a_ref[...] = jnp.zeros(a_ref.shape, F32)

        def sub(s, carry):
            r0 = pl.multiple_of(s * sr, sr)
            vals = []
            for idx, r in enumerate(row_refs):
                cur = r[pl.ds(r0, sr), :]
                if idx in halo:
                    hp = halo_refs[2 * halo.index(idx)]
                    hn = halo_refs[2 * halo.index(idx) + 1]
                    cur = cur.astype(F32)
                    rid = lax.broadcasted_iota(jnp.int32, cur.shape, 0)
                    lo = r[pl.ds(pl.multiple_of(jnp.maximum(r0 - 8, 0), 8), 8), :].astype(F32)
                    lo = jnp.where(s == 0, hp[...].astype(F32), lo)
                    lo = jnp.where((s == 0) & (i == 0), 0.0, lo)
                    hi = r[pl.ds(pl.multiple_of(jnp.minimum(r0 + sr, tr - 8), 8), 8), :].astype(F32)
                    hi = jnp.where(s == nsub - 1, hn[...].astype(F32), hi)
                    hi = jnp.where((s == nsub - 1) & (i == nblk - 1), 0.0, hi)
                    prev = jnp.where(rid == 0, jnp.broadcast_to(lo[7:8, :], cur.shape), pltpu.roll(cur, 1, 0))
                    nxt = jnp.where(rid == sr - 1, jnp.broadcast_to(hi[0:1, :], cur.shape), pltpu.roll(cur, sr - 1, 0))
                    vals.append((prev, cur, nxt))
                else:
                    vals.append(cur)
            res = body(*vals, *[r[...] for r in sel_refs], *[r[...] for r in full_refs])
            if not isinstance(res, (tuple, list)):
                res = (res,)
            for o_ref, v in zip(out_refs, res[:n_out]):
                o_ref[pl.ds(r0, sr), :] = v.astype(o_ref.dtype)
            for a_ref, v in zip(acc_refs, res[n_out:]):
                a_ref[...] += jnp.sum(v.astype(F32).reshape(sr // 8, 8, v.shape[-1]), axis=0)
            return carry

        lax.fori_loop(0, nsub, sub, 0)

    res = pl.pallas_call(
        kern, name=name, grid=(nblk,), in_specs=in_specs, out_specs=out_specs, out_shape=out_shape,
        compiler_params=_cparams(("arbitrary",)),
    )(*operands)
    return res


def _sigmoid(x):
    return 1.0 / (1.0 + jnp.exp(-x))


def _silu(x):
    return x * _sigmoid(x)


def _dsilu(x):
    s = _sigmoid(x)
    return s * (1.0 + x * (1.0 - s))


_GELU_K = math.sqrt(2.0 / math.pi)


def _gelu(x):
    return 0.5 * x * (1.0 + jnp.tanh(_GELU_K * (x + 0.044715 * x * x * x)))


def _dgelu(x):
    t = jnp.tanh(_GELU_K * (x + 0.044715 * x * x * x))
    return 0.5 * (1.0 + t) + 0.5 * x * (1.0 - t * t) * _GELU_K * (1.0 + 3.0 * 0.044715 * x * x)


def _rms_fwd(x, g, width):
    r = lax.rsqrt(jnp.sum(x * x, axis=-1, keepdims=True) * (1.0 / width) + EPS)
    return x * r * g


def _rms_bwd(x, g, dy, width):
    r = lax.rsqrt(jnp.sum(x * x, axis=-1, keepdims=True) * (1.0 / width) + EPS)
    xn = x * r
    dyg = dy * g
    dx = r * (dyg - xn * (jnp.sum(dyg * xn, axis=-1, keepdims=True) * (1.0 / width)))
    return dx, dy * xn


def _rope_fwd(y, c, sa, sb):
    return y * c + pltpu.roll(y, SLOT - 16, 1) * sa + pltpu.roll(y, 16, 1) * sb


def _rope_bwd(d, c, sa, sb):
    return d * c + pltpu.roll(d * sa, 16, 1) + pltpu.roll(d * sb, SLOT - 16, 1)


def _heads(v):
    return [v[:, h * SLOT:(h + 1) * SLOT] for h in range(N_HEADS)]


def _attn_fwd(q, k, v, nl, scale, side=None):
    n = k.shape[0]
    tq = _pick(nl, (4096, 2048, 1024, 512, 256, 128))
    tk = _pick(n, (2816, 1408, 1152, 768, 384, 256, 128))
    sub = min(tq, 512)
    nk = n // tk
    rep = tk // SLOT
    c = scale * LOG2E

    def body(q_ref, k_ref, v_ref, o_ref, lse_ref, m_sc, l_sc, acc_sc):
        ki = pl.program_id(2)

        @pl.when(ki == 0)
        def _():
            m_sc[...] = jnp.full(m_sc.shape, -jnp.inf, F32)
            l_sc[...] = jnp.zeros(l_sc.shape, F32)
            acc_sc[...] = jnp.zeros(acc_sc.shape, F32)

        kb, vb = k_ref[...], v_ref[...]
        for sb in range(tq // sub):
            rows = slice(sb * sub, (sb + 1) * sub)
            s = lax.dot_general(q_ref[rows, :], kb, (((1,), (1,)), ((), ())), preferred_element_type=F32)
            m_prev = m_sc[rows, :]
            m_new = jnp.maximum(m_prev, jnp.max(s, axis=1, keepdims=True) * c)
            alpha = jnp.exp2(m_prev - m_new)
            p = jnp.exp2(s * c - jnp.tile(m_new, (1, rep)))
            l_sc[rows, :] = alpha * l_sc[rows, :] + jnp.sum(p, axis=1, keepdims=True)
            acc_sc[rows, :] = alpha * acc_sc[rows, :] + jnp.dot(p.astype(MXU_DTYPE), vb, preferred_element_type=F32)
            m_sc[rows, :] = m_new

        @pl.when(ki == nk - 1)
        def _():
            l = l_sc[...]
            o_ref[...] = (acc_sc[...] / l).astype(o_ref.dtype)
            lse_ref[...] = jnp.transpose(m_sc[...] + jnp.log2(l))[0:8, :]

    grid = (N_HEADS, nl // tq, nk)
    body, s_in, s_out, s_shape, s_scr = _ride_along(body, 3, 2, 3, grid, side)
    res = pl.pallas_call(
        body, name="attn_fwd", grid=grid,
        in_specs=[pl.BlockSpec((tq, SLOT), lambda h, i, j: (i, h)),
                  pl.BlockSpec((tk, SLOT), lambda h, i, j: (j, h)),
                  pl.BlockSpec((tk, SLOT), lambda h, i, j: (j, h))] + s_in,
        out_specs=[pl.BlockSpec((tq, SLOT), lambda h, i, j: (i, h)),
                   pl.BlockSpec((None, 8, tq), lambda h, i, j: (h, 0, i))] + s_out,
        out_shape=[_sds((nl, N_HEADS * SLOT), MXU_DTYPE), _sds((N_HEADS, 8, nl), F32)] + s_shape,
        scratch_shapes=[pltpu.VMEM((tq, SLOT), F32), pltpu.VMEM((tq, SLOT), F32), pltpu.VMEM((tq, SLOT), F32)] + s_scr,
        compiler_params=_cparams(("arbitrary", "arbitrary", "arbitrary")),
    )(q, k, v, *(side[0] if side else ()))
    return res[0], res[1], res[2:]


def _attn_bwd(q, k, v, do, o, lse_t, nl, scale, side=None):
    n = k.shape[0]
    tq = _pick(nl, (2048, 1024, 512, 256, 128))
    tk = _pick(n, (2816, 1408, 1152, 768, 384, 256, 128))
    sub = _pick(tk, (256, 128))
    nq, nk = nl // tq, n // tk
    c = scale * LOG2E

    def body(q_ref, k_ref, v_ref, do_ref, o_ref, lse_ref, dq_ref, dk_ref, dv_ref, dq_acc, dk_acc, dv_acc):
        ki, qi = pl.program_id(1), pl.program_id(2)

        @pl.when((ki == 0) & (qi == 0))
        def _():
            dq_acc[...] = jnp.zeros(dq_acc.shape, F32)

        @pl.when(qi == 0)
        def _():
            dk_acc[...] = jnp.zeros(dk_acc.shape, F32)
            dv_acc[...] = jnp.zeros(dv_acc.shape, F32)

        qb, dof = q_ref[...], do_ref[...]
        dob = dof.astype(MXU_DTYPE)
        lse_r = lse_ref[0:1, :]
        dl_r = jnp.sum(jnp.transpose(dof * o_ref[...].astype(F32)), axis=0, keepdims=True)
        dq_part = None
        for sb in range(tk // sub):
            rows = slice(sb * sub, (sb + 1) * sub)
            kb = k_ref[rows, :]
            s_t = lax.dot_general(kb, qb, (((1,), (1,)), ((), ())), preferred_element_type=F32)
            p_t = jnp.exp2(s_t * c - lse_r)
            dp_t = lax.dot_general(v_ref[rows, :], dob, (((1,), (1,)), ((), ())), preferred_element_type=F32)
            ds_t = (p_t * (dp_t - dl_r) * scale).astype(MXU_DTYPE)
            dv_acc[rows, :] += jnp.dot(p_t.astype(MXU_DTYPE), dob, preferred_element_type=F32)
            dk_acc[rows, :] += jnp.dot(ds_t, qb, preferred_element_type=F32)
            part = lax.dot_general(kb, ds_t, (((0,), (0,)), ((), ())), preferred_element_type=F32)
            dq_part = part if dq_part is None else dq_part + part
        c0 = pl.multiple_of(qi * tq, tq)
        dq_acc[:, pl.ds(c0, tq)] += dq_part

        @pl.when(ki == nk - 1)
        def _():
            dq_ref[...] = jnp.transpose(dq_acc[:, pl.ds(c0, tq)])

        @pl.when(qi == nq - 1)
        def _():
            dk_ref[...] = dk_acc[...]
            dv_ref[...] = dv_acc[...]

    grid = (N_HEADS, nk, nq)
    body, s_in, s_out, s_shape, s_scr = _ride_along(body, 6, 3, 3, grid, side)
    res = pl.pallas_call(
        body, name="attn_bwd", grid=grid,
        in_specs=[pl.BlockSpec((tq, SLOT), lambda h, j, i: (i, h)),
                  pl.BlockSpec((tk, SLOT), lambda h, j, i: (j, h)),
                  pl.BlockSpec((tk, SLOT), lambda h, j, i: (j, h)),
                  pl.BlockSpec((tq, SLOT), lambda h, j, i: (i, h)),
                  pl.BlockSpec((tq, SLOT), lambda h, j, i: (i, h)),
                  pl.BlockSpec((None, 8, tq), lambda h, j, i: (h, 0, i))] + s_in,
        out_specs=[pl.BlockSpec((tq, SLOT), lambda h, j, i: (jnp.where(j == nk - 1, i, 0), h)),
                   pl.BlockSpec((tk, SLOT), lambda h, j, i: (j, h)),
                   pl.BlockSpec((tk, SLOT), lambda h, j, i: (j, h))] + s_out,
        out_shape=[_sds((nl, N_HEADS * SLOT), F32), _sds((n, N_HEADS * SLOT), F32), _sds((n, N_HEADS * SLOT), F32)] + s_shape,
        scratch_shapes=[pltpu.VMEM((SLOT, nl), F32), pltpu.VMEM((tk, SLOT), F32), pltpu.VMEM((tk, SLOT), F32)] + s_scr,
        compiler_params=_cparams(("arbitrary", "arbitrary", "arbitrary")),
    )(q, k, v, do, o, lse_t, *(side[0] if side else ()))
    return res[0], res[1], res[2], res[3:]


def _scan_consts(c_ref, lg):
    cs = slice(lg * SCAN_LANES, (lg + 1) * SCAN_LANES)
    return [c_ref[8 * kk:8 * kk + 8, cs] for kk in range(8)]


def _tile_scan(br, bi, consts, reverse):
    p1r, p1i, p2r, p2i, p4r, p4i = consts[:6]
    for pr, pi, kk in ((p1r, p1i, 1), (p2r, p2i, 2), (p4r, p4i, 4)):
        sh = (8 - kk) if reverse else kk
        sr_, si_ = pltpu.roll(br, sh, 0), pltpu.roll(bi, sh, 0)
        br, bi = br + pr * sr_ - pi * si_, bi + pr * si_ + pi * sr_
    return br, bi


def _seq_chunk(j, nch, nlc, reverse):
    return (nch - 1 - j) if reverse else (j + nlc) % nch


def _s5_scan(z, u_cb, bbd, cbd_n, lamc, t_rows, nl, reverse, name):
    n = z.shape[0]
    nch, nlc = n // t_rows, nl // t_rows
    ntile = t_rows // 8
    w = SCAN_LANES
    edge = 0 if reverse else 7
    ucb = u_cb * (SSM_WIDTH // CG_CHANNELS)

    def chunk(j):
        return _seq_chunk(j, nch, nlc, reverse)

    def body(u_ref, b_ref, cn_ref, c_ref, xs_ref, y_ref, carry):
        j = pl.program_id(1)

        @pl.when(j == 0)
        def _():
            carry[...] = jnp.zeros(carry.shape, F32)

        xs_ref[...] = jnp.dot(u_ref[...].astype(MXU_DTYPE), b_ref[...], preferred_element_type=F32)
        for lg in range(CG_STATES // w):
            re = slice(lg * w, (lg + 1) * w)
            im = slice(CG_STATES + lg * w, CG_STATES + (lg + 1) * w)
            consts = _scan_consts(c_ref, lg)
            qr, qi = consts[6], consts[7]

            def tile(tt, st):
                cr, ci = st
                t = (ntile - 1 - tt) if reverse else tt
                r0 = pl.multiple_of(t * 8, 8)
                br, bi = _tile_scan(xs_ref[pl.ds(r0, 8), re], xs_ref[pl.ds(r0, 8), im], consts, reverse)
                lr = jnp.broadcast_to(cr[edge:edge + 1, :], br.shape)
                li = jnp.broadcast_to(ci[edge:edge + 1, :], bi.shape)
                xr = br + qr * lr - qi * li
                xi = bi + qr * li + qi * lr
                xs_ref[pl.ds(r0, 8), re] = xr
                xs_ref[pl.ds(r0, 8), im] = xi
                return xr, xi

            cr, ci = lax.fori_loop(0, ntile, tile, (carry[:, re], carry[:, im]))
            carry[:, re] = cr
            carry[:, im] = ci
        y_ref[...] = jnp.dot(xs_ref[...].astype(MXU_DTYPE), cn_ref[...], preferred_element_type=F32)

    cw = 2 * CG_STATES
    return pl.pallas_call(
        body, name=name, grid=(N_CG, nch),
        in_specs=[pl.BlockSpec((t_rows, CG_CHANNELS), lambda g, j: (chunk(j), ucb + g)),
                  pl.BlockSpec((CG_CHANNELS, cw), lambda g, j: (g, 0)),
                  pl.BlockSpec((cw, CG_CHANNELS), lambda g, j: (g, 0)),
                  pl.BlockSpec((64, CG_STATES), lambda g, j: (0, g))],
        out_specs=[pl.BlockSpec((t_rows, cw), lambda g, j: (chunk(j), g)),
                   pl.BlockSpec((t_rows, CG_CHANNELS), lambda g, j: (chunk(j), g))],
        out_shape=[_sds((n, 2 * N_STATE), F32), _sds((n, SSM_WIDTH), F32)],
        scratch_shapes=[pltpu.VMEM((8, cw), F32)],
        compiler_params=_cparams(("arbitrary", "arbitrary")),
    )(z, bbd, cbd_n, lamc)


def _s5_bwd(dys, z, u_cb, xs, cbd_t, bbd_t, lamc_adj, t_rows, nl, reverse, name):
    n = z.shape[0]
    nch, nlc = n // t_rows, nl // t_rows
    ntile = t_rows // 8
    t8 = t_rows // 8
    w = SCAN_LANES
    cw = 2 * CG_STATES
    adj_rev = not reverse
    edge = 0 if adj_rev else 7

    def chunk(j):
        return _seq_chunk(nch - 1 - j, nch, nlc, reverse)

    def halo_blk(j):
        if reverse:
            return jnp.minimum((chunk(j) + 1) * t8, n // 8 - 1)
        return (_seq_chunk(jnp.maximum(nch - 2 - j, 0), nch, nlc, False) + 1) * t8 - 1

    def body(dy_ref, u_ref, xs_ref, halo_ref, ct_ref, bt_ref, c_ref, du_ref, db_ref, dc_ref, dl_ref, gbuf, carry):
        j = pl.program_id(1)
        start = j == nch - 1

        @pl.when(j == 0)
        def _():
            carry[...] = jnp.zeros(carry.shape, F32)
            db_ref[...] = jnp.zeros(db_ref.shape, F32)
            dc_ref[...] = jnp.zeros(dc_ref.shape, F32)
            dl_ref[...] = jnp.zeros(dl_ref.shape, F32)

        dy = jnp.where(chunk(j) < nlc, dy_ref[...], 0.0).astype(MXU_DTYPE)
        gbuf[...] = jnp.dot(dy, ct_ref[...], preferred_element_type=F32)
        dc_ref[...] += lax.dot_general(dy, xs_ref[...].astype(MXU_DTYPE), (((0,), (0,)), ((), ())),
                                       preferred_element_type=F32)
        for lg in range(CG_STATES // w):
            re = slice(lg * w, (lg + 1) * w)
            im = slice(CG_STATES + lg * w, CG_STATES + (lg + 1) * w)
            consts = _scan_consts(c_ref, lg)
            qr, qi = consts[6], consts[7]
            hr, hi = halo_ref[:, re], halo_ref[:, im]

            def tile(tt, st):
                gcr, gci, ar, ai = st
                t = (ntile - 1 - tt) if adj_rev else tt
                r0 = pl.multiple_of(t * 8, 8)
                br, bi = _tile_scan(gbuf[pl.ds(r0, 8), re], gbuf[pl.ds(r0, 8), im], consts, adj_rev)
                lr = jnp.broadcast_to(gcr[edge:edge + 1, :], br.shape)
                li = jnp.broadcast_to(gci[edge:edge + 1, :], bi.shape)
                gr = br + qr * lr - qi * li
                gi = bi + qr * li + qi * lr
                gbuf[pl.ds(r0, 8), re] = gr
                gbuf[pl.ds(r0, 8), im] = gi
                xr, xi = xs_ref[pl.ds(r0, 8), re], xs_ref[pl.ds(r0, 8), im]
                rid = lax.broadcasted_iota(jnp.int32, xr.shape, 0)
                if reverse:
                    last = t == ntile - 1
                    rn = pl.multiple_of(jnp.minimum(r0 + 8, t_rows - 8), 8)
                    nbr = jnp.where(last, hr, xs_ref[pl.ds(rn, 8), re])
                    nbi = jnp.where(last, hi, xs_ref[pl.ds(rn, 8), im])
                    nbr = jnp.where(last & start, 0.0, nbr)
                    nbi = jnp.where(last & start, 0.0, nbi)
                    xpr = jnp.where(rid == 7, jnp.broadcast_to(nbr[0:1, :], xr.shape), pltpu.roll(xr, 7, 0))
                    xpi = jnp.where(rid == 7, jnp.broadcast_to(nbi[0:1, :], xi.shape), pltpu.roll(xi, 7, 0))
                else:
                    first = t == 0
                    rn = pl.multiple_of(jnp.maximum(r0 - 8, 0), 8)
                    nbr = jnp.where(first, hr, xs_ref[pl.ds(rn, 8), re])
                    nbi = jnp.where(first, hi, xs_ref[pl.ds(rn, 8), im])
                    nbr = jnp.where(first & start, 0.0, nbr)
                    nbi = jnp.where(first & start, 0.0, nbi)
                    xpr = jnp.where(rid == 0, jnp.broadcast_to(nbr[7:8, :], xr.shape), pltpu.roll(xr, 1, 0))
                    xpi = jnp.where(rid == 0, jnp.broadcast_to(nbi[7:8, :], xi.shape), pltpu.roll(xi, 1, 0))
                ar = ar + gr * xpr + gi * xpi
                ai = ai - gr * xpi + gi * xpr
                return gr, gi, ar, ai

            zz = jnp.zeros((8, w), F32)
            gcr, gci, ar, ai = lax.fori_loop(0, ntile, tile, (carry[:, re], carry[:, im], zz, zz))
            carry[:, re] = gcr
            carry[:, im] = gci
            dl_ref[:, re] += ar
            dl_ref[:, im] += ai
        g = gbuf[...].astype(MXU_DTYPE)
        du_ref[...] = jnp.dot(g, bt_ref[...], preferred_element_type=F32)
        db_ref[...] += lax.dot_general(u_ref[...].astype(MXU_DTYPE), g, (((0,), (0,)), ((), ())),
                                       preferred_element_type=F32)

    ucb = u_cb * (SSM_WIDTH // CG_CHANNELS)
    return pl.pallas_call(
        body, name=name, grid=(N_CG, nch),
        in_specs=[pl.BlockSpec((t_rows, CG_CHANNELS), lambda g, j: (jnp.minimum(chunk(j), nlc - 1), g)),
                  pl.BlockSpec((t_rows, CG_CHANNELS), lambda g, j: (chunk(j), ucb + g)),
                  pl.BlockSpec((t_rows, cw), lambda g, j: (chunk(j), g)),
                  pl.BlockSpec((8, cw), lambda g, j: (halo_blk(j), g)),
                  pl.BlockSpec((CG_CHANNELS, cw), lambda g, j: (g, 0)),
                  pl.BlockSpec((cw, CG_CHANNELS), lambda g, j: (g, 0)),
                  pl.BlockSpec((64, CG_STATES), lambda g, j: (0, g))],
        out_specs=[pl.BlockSpec((t_rows, CG_CHANNELS), lambda g, j: (chunk(j), g)),
                   pl.BlockSpec((CG_CHANNELS, cw), lambda g, j: (g, 0)),
                   pl.BlockSpec((CG_CHANNELS, cw), lambda g, j: (g, 0)),
                   pl.BlockSpec((8, cw), lambda g, j: (0, g))],
        out_shape=[_sds((n, SSM_WIDTH), F32), _sds((SSM_WIDTH, cw), F32), _sds((SSM_WIDTH, cw), F32),
                   _sds((8, 2 * N_STATE), F32)],
        scratch_shapes=[pltpu.VMEM((t_rows, cw), F32), pltpu.VMEM((8, cw), F32)],
        compiler_params=_cparams(("arbitrary", "arbitrary")),
    )(dys, z, xs, xs, cbd_t, bbd_t, lamc_adj)


_CG_GROUPS = SSM_GROUPS // N_CG


def _group_mask():
    idx = jnp.arange(_CG_GROUPS)
    return (idx[:, None] == idx[None, :])[None, :, None, None, :, None]


def _diag_blocks(p_re, p_im):
    t = jnp.stack([p_re, p_im], axis=2).reshape(N_CG, _CG_GROUPS, SSM_GROUP, 2, 1, SSM_STATE)
    return jnp.where(_group_mask(), t, 0.0).reshape(SSM_WIDTH, 2 * CG_STATES)


def _diag_extract(d):
    d6 = d.reshape(N_CG, _CG_GROUPS, SSM_GROUP, 2, _CG_GROUPS, SSM_STATE)
    blk = jnp.sum(jnp.where(_group_mask(), d6, 0.0), axis=4)
    blk = blk.reshape(SSM_GROUPS, SSM_GROUP, 2, SSM_STATE)
    return blk[:, :, 0], blk[:, :, 1]


def _block_transpose(d):
    return jnp.transpose(d.reshape(N_CG, CG_CHANNELS, 2 * CG_STATES), (0, 2, 1)).reshape(2 * N_STATE, CG_CHANNELS)


def _s5_disc(lam_re, lam_im, log_dt, b_re, b_im):
    lam = lax.complex(lam_re, lam_im)
    dt = jnp.exp(log_dt)[:, None]
    lam_bar = jnp.exp(lam * dt)
    b_bar = ((lam_bar - 1.0) / lam)[..., None] * lax.complex(b_re, b_im)
    return jnp.real(lam_bar), jnp.imag(lam_bar), jnp.real(b_bar), jnp.imag(b_bar)


def _lam_consts(lr, li, mirrored, conj):
    lam = lax.complex(lr.reshape(-1), -li.reshape(-1) if conj else li.reshape(-1))
    p2 = lam * lam
    p4 = p2 * p2
    pw = [lam, p2, p2 * lam, p4, p4 * lam, p4 * p2, p4 * p2 * lam, p4 * p4]
    rows = jnp.arange(8)[:, None]
    out = []
    for kk in (1, 2, 4):
        mask = (rows <= 7 - kk) if mirrored else (rows >= kk)
        pk = jnp.where(mask, pw[kk - 1][None, :], 0.0)
        out += [jnp.real(pk), jnp.imag(pk)]
    q = jnp.stack(pw[::-1] if mirrored else pw)
    return jnp.concatenate(out + [jnp.real(q), jnp.imag(q)], axis=0)


def _dev(t):
    return (t // 4, (t // 2) % 2, t % 2)


def _my_index():
    return 4 * lax.axis_index("x") + 2 * lax.axis_index("y") + lax.axis_index("c")


def _comm_call(body, name, arrs, lead, n_remote):
    nw = len(arrs)
    any_spec = pl.BlockSpec(memory_space=pl.ANY)
    return pl.pallas_call(
        body, name=name, out_shape=[_sds((lead,) + a.shape[-2:], a.dtype) for a in arrs],
        in_specs=[any_spec] * nw, out_specs=[any_spec] * nw,
        scratch_shapes=[pltpu.SemaphoreType.DMA((n_remote * nw,)), pltpu.SemaphoreType.DMA((n_remote * nw,)),
                        pltpu.SemaphoreType.DMA((2 * nw,))] + [pltpu.VMEM(a.shape[-2:], a.dtype) for a in arrs],
        compiler_params=pltpu.CompilerParams(vmem_limit_bytes=VMEM_LIMIT),
    )(*arrs)


class _LocalCopy:
    def __init__(self, src, dst, buf, sem_in, sem_out):
        self.fetch = pltpu.make_async_copy(src, buf, sem_in)
        self.store = pltpu.make_async_copy(buf, dst, sem_out)
        self.fetch.start()

    def forward(self):
        self.fetch.wait()
        self.store.start()

    def finish(self):
        self.store.wait()


def _exchange8(gs, name, same):
    nw = len(gs)

    def body(*refs):
        g_refs, o_refs, (ssem, rsem, lsem), bufs = refs[:nw], refs[nw:2 * nw], refs[2 * nw:2 * nw + 3], refs[2 * nw + 3:]
        me = _my_index()
        locs, sends = [], []
        for i, (g_ref, o_ref) in enumerate(zip(g_refs, o_refs)):
            src = (lambda t, g_ref=g_ref: g_ref) if same else (lambda t, g_ref=g_ref: g_ref.at[t])
            locs.append(_LocalCopy(src(me), o_ref.at[me], bufs[i], lsem.at[2 * i], lsem.at[2 * i + 1]))
            for d in range(1, 8):
                t = (me + d) % 8
                cp = pltpu.make_async_remote_copy(src_ref=src(t), dst_ref=o_ref.at[me], send_sem=ssem.at[7 * i + d - 1],
                                                  recv_sem=rsem.at[7 * i + d - 1], device_id=_dev(t), device_id_type=MESH)
                cp.start()
                sends.append(cp)
        for loc in locs:
            loc.forward()
        for i, (g_ref, o_ref) in enumerate(zip(g_refs, o_refs)):
            src = (lambda t, g_ref=g_ref: g_ref) if same else (lambda t, g_ref=g_ref: g_ref.at[t])
            for d in range(1, 8):
                s = (me + 8 - d) % 8
                pltpu.make_async_remote_copy(src_ref=src(s), dst_ref=o_ref.at[s], send_sem=ssem.at[7 * i + d - 1],
                                             recv_sem=rsem.at[7 * i + d - 1], device_id=_dev(s),
                                             device_id_type=MESH).wait_recv()
        for cp in sends:
            cp.wait_send()
        for loc in locs:
            loc.finish()

    return _comm_call(body, name, gs, 8, 7)


def _chip_copies(w_refs, o_refs, ssem, rsem, lsem, bufs, gather):
    x, y, cc = lax.axis_index("x"), lax.axis_index("y"), lax.axis_index("c")
    k = 2 * x + y
    peers = [(1 - x, y), (x, 1 - y), (1 - x, 1 - y)]
    fetch, store, sends, recvs = [], [], [], []
    for i, (w_ref, o_ref) in enumerate(zip(w_refs, o_refs)):
        if gather:
            fetch.append(pltpu.make_async_copy(w_ref.at[cc], bufs[i], lsem.at[2 * i]))
            store.append(pltpu.make_async_copy(bufs[i], o_ref.at[k], lsem.at[2 * i + 1]))
        for j, (px, py) in enumerate(peers):
            sems = dict(send_sem=ssem.at[3 * i + j], recv_sem=rsem.at[3 * i + j], device_id=(px, py, cc), device_id_type=MESH)
            src, dst = (w_ref.at[cc], o_ref.at[k]) if gather else (w_ref.at[2 * px + py], o_ref.at[j])
            sends.append(pltpu.make_async_remote_copy(src_ref=src, dst_ref=dst, **sems))
            src, dst = (w_ref.at[cc], o_ref.at[2 * px + py]) if gather else (w_ref.at[k], o_ref.at[j])
            recvs.append(pltpu.make_async_remote_copy(src_ref=src, dst_ref=dst, **sems))
    return fetch, store, sends, recvs


def _chips_start(*args):
    fetch, _, sends, _ = _chip_copies(*args)
    for cp in fetch + sends:
        cp.start()


def _chips_finish(*args):
    fetch, store, sends, recvs = _chip_copies(*args)
    for cp in fetch:
        cp.wait()
    for cp in store:
        cp.start()
    for cp in recvs:
        cp.wait_recv()
    for cp in sends:
        cp.wait_send()
    for cp in store:
        cp.wait()


def _chips_scratch(ws, gather):
    nw = len(ws)
    return ([pltpu.SemaphoreType.DMA((3 * nw,)), pltpu.SemaphoreType.DMA((3 * nw,)), pltpu.SemaphoreType.DMA((2 * nw,))]
            + ([pltpu.VMEM(a.shape[-2:], a.dtype) for a in ws] if gather else []))


def _ride_along(core, n_in, n_out, n_scr, grid, side):
    if side is None:
        return core, [], [], [], []
    arrs, gather = side
    ns = len(arrs)

    def body(*refs):
        a, b, c_ = n_in + ns, n_in + ns + n_out, n_in + 2 * ns + n_out
        s_scr = refs[c_ + n_scr:]
        sargs = (refs[n_in:a], refs[b:c_], *s_scr[:3], s_scr[3:], gather)
        ids = [pl.program_id(ax) for ax in range(len(grid))]
        first, last = ids[0] == 0, ids[0] == grid[0] - 1
        for i_, g_ in zip(ids[1:], grid[1:]):
            first, last = first & (i_ == 0), last & (i_ == g_ - 1)

        @pl.when(first)
        def _():
            _chips_start(*sargs)

        core(*refs[:n_in], *refs[a:b], *refs[c_:c_ + n_scr])

        @pl.when(last)
        def _():
            _chips_finish(*sargs)

    any_spec = pl.BlockSpec(memory_space=pl.ANY)
    shapes = [_sds((4 if gather else 3,) + a_.shape[-2:], a_.dtype) for a_ in arrs]
    return body, [any_spec] * ns, [any_spec] * ns, shapes, _chips_scratch(arrs, gather)


def _exchange_chips(ws, name, gather):
    nw = len(ws)

    def body(*refs):
        args = (refs[:nw], refs[nw:2 * nw], *refs[2 * nw:2 * nw + 3], refs[2 * nw + 3:], gather)
        _chips_start(*args)
        _chips_finish(*args)

    return _comm_call(body, name, ws, 4 if gather else 3, 3)


def _sibling_send(hs, name):
    nw = len(hs)

    def body(*refs):
        h_refs, o_refs, (ssem, rsem, lsem) = refs[:nw], refs[nw:2 * nw], refs[2 * nw:]
        x, y, cc = lax.axis_index("x"), lax.axis_index("y"), lax.axis_index("c")
        sends = []
        for i, (h_ref, o_ref) in enumerate(zip(h_refs, o_refs)):
            cp = pltpu.make_async_remote_copy(src_ref=h_ref.at[1 - cc], dst_ref=o_ref, send_sem=ssem.at[i],
                                              recv_sem=rsem.at[i], device_id=(x, y, 1 - cc), device_id_type=MESH)
            cp.start()
            sends.append(cp)
        for i, (h_ref, o_ref) in enumerate(zip(h_refs, o_refs)):
            pltpu.make_async_remote_copy(src_ref=h_ref.at[cc], dst_ref=o_ref, send_sem=ssem.at[i], recv_sem=rsem.at[i],
                                         device_id=(x, y, 1 - cc), device_id_type=MESH).wait_recv()
        for cp in sends:
            cp.wait_send()

    nw_spec = pl.BlockSpec(memory_space=pl.ANY)
    return pl.pallas_call(
        body, name=name, out_shape=[_sds(h.shape[1:], h.dtype) for h in hs],
        in_specs=[nw_spec] * nw, out_specs=[nw_spec] * nw,
        scratch_shapes=[pltpu.SemaphoreType.DMA((nw,)), pltpu.SemaphoreType.DMA((nw,)), pltpu.SemaphoreType.DMA((nw,))],
    )(*hs)


def _sibling_exchange(hs, name):
    nw = len(hs)

    def body(*refs):
        h_refs, o_refs, (ssem, rsem, lsem), bufs = refs[:nw], refs[nw:2 * nw], refs[2 * nw:2 * nw + 3], refs[2 * nw + 3:]
        x, y, cc = lax.axis_index("x"), lax.axis_index("y"), lax.axis_index("c")
        locs, sends = [], []
        for i, (h_ref, o_ref) in enumerate(zip(h_refs, o_refs)):
            locs.append(_LocalCopy(h_ref, o_ref.at[cc], bufs[i], lsem.at[2 * i], lsem.at[2 * i + 1]))
            cp = pltpu.make_async_remote_copy(src_ref=h_ref, dst_ref=o_ref.at[cc], send_sem=ssem.at[i], recv_sem=rsem.at[i],
                                              device_id=(x, y, 1 - cc), device_id_type=MESH)
            cp.start()
            sends.append(cp)
        for loc in locs:
            loc.forward()
        for i, (h_ref, o_ref) in enumerate(zip(h_refs, o_refs)):
            pltpu.make_async_remote_copy(src_ref=h_ref, dst_ref=o_ref.at[1 - cc], send_sem=ssem.at[i], recv_sem=rsem.at[i],
                                         device_id=(x, y, 1 - cc), device_id_type=MESH).wait_recv()
        for cp in sends:
            cp.wait_send()
        for loc in locs:
            loc.finish()

    return _comm_call(body, name, hs, 2, 1)


def _sum8(buf, name):
    _, r, c = buf.shape
    tr = _pick(r, (256, 128, 64, 32, 16, 8))
    flat = buf.reshape(8 * r, c)

    def body(*v):
        acc = v[0]
        for t in v[1:]:
            acc = acc + t
        return acc

    return _rowwise(body, name=name, nblk=r // tr, tr=tr, rows=[(flat, 0, c, s * r) for s in range(8)],
                    outs=[(c, F32)])[0]


def _pack(arrs, rows_mult=16):
    flat = jnp.concatenate([a.reshape(-1).astype(F32) for a in arrs])
    nel = flat.shape[0]
    r = -(-nel // PACK_W)
    r = -(-r // rows_mult) * rows_mult
    return jnp.pad(flat, (0, r * PACK_W - nel)).reshape(r, PACK_W)


def _unpack(buf, shapes):
    flat = buf.reshape(-1)
    out, o = [], 0
    for s in shapes:
        nel = int(np.prod(s))
        out.append(flat[o:o + nel].reshape(s))
        o += nel
    return out


def _adamw(g, w, m, v, name):
    r, wd = g.shape
    tr = _pick(r, tuple(t for t in (256, 128, 64, 32, 16, 8) if t * wd <= 262144) or (8,))
    c1 = 1.0 / (1.0 - ADAM_B1 ** ADAM_STEP)
    c2 = 1.0 / (1.0 - ADAM_B2 ** ADAM_STEP)

    def body(gv, wv, mv, vv):
        mn = ADAM_B1 * mv + (1.0 - ADAM_B1) * gv
        vn = ADAM_B2 * vv + (1.0 - ADAM_B2) * (gv * gv)
        delta = -ADAM_LR * ((mn * c1) / (jnp.sqrt(vn * c2) + ADAM_EPS) + ADAM_WD * wv)
        return delta, mn, vn

    return _rowwise(body, name=name, nblk=r // tr, tr=tr, rows=[(a, 0, wd, 0) for a in (g, w, m, v)],
                    outs=[(wd, F32)] * 3)


def kernel(x, c, ctx, c_ctx, w_mod, b_mod, norm1_g, norm2_g, w_in, q_a_g, w_uq, kv_a_g, w_ukv, q_norm_g, k_norm_g, w_o_attn, lam_re_f, lam_im_f, log_dt_f, c_re_f, c_im_f, lam_re_b, lam_im_b, log_dt_b, c_re_b, c_im_b, b_re, b_im, d_skip, w_glu, w_out, w_up, conv_w, conv_b, w_down, loss_target, m_c_ctx, m_w_mod, m_b_mod, m_norm1_g, m_norm2_g, m_w_in, m_q_a_g, m_w_uq, m_kv_a_g, m_w_ukv, m_q_norm_g, m_k_norm_g, m_w_o_attn, m_lam_re_f, m_lam_im_f, m_log_dt_f, m_c_re_f, m_c_im_f, m_lam_re_b, m_lam_im_b, m_log_dt_b, m_c_re_b, m_c_im_b, m_b_re, m_b_im, m_d_skip, m_w_glu, m_w_out, m_w_up, m_conv_w, m_conv_b, m_w_down, v_c_ctx, v_w_mod, v_b_mod, v_norm1_g, v_norm2_g, v_w_in, v_q_a_g, v_w_uq, v_kv_a_g, v_w_ukv, v_q_norm_g, v_k_norm_g, v_w_o_attn, v_lam_re_f, v_lam_im_f, v_log_dt_f, v_c_re_f, v_c_im_f, v_lam_re_b, v_lam_im_b, v_log_dt_b, v_c_re_b, v_c_im_b, v_b_re, v_b_im, v_d_skip, v_w_glu, v_w_out, v_w_up, v_conv_w, v_conv_b, v_w_down):
    weights = dict(c_ctx=c_ctx, w_mod=w_mod, b_mod=b_mod, norm1_g=norm1_g, norm2_g=norm2_g, w_in=w_in, q_a_g=q_a_g, w_uq=w_uq, kv_a_g=kv_a_g, w_ukv=w_ukv, q_norm_g=q_norm_g, k_norm_g=k_norm_g, w_o_attn=w_o_attn, lam_re_f=lam_re_f, lam_im_f=lam_im_f, log_dt_f=log_dt_f, c_re_f=c_re_f, c_im_f=c_im_f, lam_re_b=lam_re_b, lam_im_b=lam_im_b, log_dt_b=log_dt_b, c_re_b=c_re_b, c_im_b=c_im_b, b_re=b_re, b_im=b_im, d_skip=d_skip, w_glu=w_glu, w_out=w_out, w_up=w_up, conv_w=conv_w, conv_b=conv_b, w_down=w_down)
    mom_m = dict(c_ctx=m_c_ctx, w_mod=m_w_mod, b_mod=m_b_mod, norm1_g=m_norm1_g, norm2_g=m_norm2_g, w_in=m_w_in, q_a_g=m_q_a_g, w_uq=m_w_uq, kv_a_g=m_kv_a_g, w_ukv=m_w_ukv, q_norm_g=m_q_norm_g, k_norm_g=m_k_norm_g, w_o_attn=m_w_o_attn, lam_re_f=m_lam_re_f, lam_im_f=m_lam_im_f, log_dt_f=m_log_dt_f, c_re_f=m_c_re_f, c_im_f=m_c_im_f, lam_re_b=m_lam_re_b, lam_im_b=m_lam_im_b, log_dt_b=m_log_dt_b, c_re_b=m_c_re_b, c_im_b=m_c_im_b, b_re=m_b_re, b_im=m_b_im, d_skip=m_d_skip, w_glu=m_w_glu, w_out=m_w_out, w_up=m_w_up, conv_w=m_conv_w, conv_b=m_conv_b, w_down=m_w_down)
    mom_v = dict(c_ctx=v_c_ctx, w_mod=v_w_mod, b_mod=v_b_mod, norm1_g=v_norm1_g, norm2_g=v_norm2_g, w_in=v_w_in, q_a_g=v_q_a_g, w_uq=v_w_uq, kv_a_g=v_kv_a_g, w_ukv=v_w_ukv, q_norm_g=v_q_norm_g, k_norm_g=v_k_norm_g, w_o_attn=v_w_o_attn, lam_re_f=v_lam_re_f, lam_im_f=v_lam_im_f, log_dt_f=v_log_dt_f, c_re_f=v_c_re_f, c_im_f=v_c_im_f, lam_re_b=v_lam_re_b, lam_im_b=v_lam_im_b, log_dt_b=v_log_dt_b, c_re_b=v_c_re_b, c_im_b=v_c_im_b, b_re=v_b_re, b_im=v_b_im, d_skip=v_d_skip, w_glu=v_w_glu, w_out=v_w_out, w_up=v_w_up, conv_w=v_conv_w, conv_b=v_conv_b, w_down=v_w_down)
    names = list(weights)

    nl, d = x.shape[1], x.shape[2]
    nc = ctx.shape[1]
    n = nl + nc
    f2 = conv_b.shape[1]
    fh = f2 // 2
    d6 = b_mod.shape[1]
    mx, my, mc = lax.axis_index("x"), lax.axis_index("y"), lax.axis_index("c")
    chip = 2 * mx + my
    me = 4 * mx + 2 * my + mc
    tr = _pick(math.gcd(nl, nc), (256, 128, 64, 32, 16))
    nlb, nb = nl // tr, n // tr

    big_names = ["w_in", "w_uq", "w_ukv", "w_o_attn", "w_glu", "w_out", "w_up", "w_down"]
    row_sharded = ("w_out", "w_down")
    ffn_names = ["w_o_attn", "w_glu", "w_out", "w_up", "w_down"]
    early_names = [k for k in big_names if k not in ffn_names]
    full = {}

    def halves_in(names_):
        return [weights[k][0].astype(MXU_DTYPE).reshape(2, weights[k].shape[1] // 2, weights[k].shape[2]) for k in names_]

    def assemble(names_, my_halves, name):
        gathered = _sibling_exchange([t.reshape(-1, t.shape[2]) for t in my_halves], name)
        for k_, gth in zip(names_, gathered):
            r_, c_ = weights[k_].shape[1:]
            g4 = gth.reshape(2, 4, r_ // 2, c_)
            full[k_] = (jnp.transpose(g4, (1, 0, 2, 3)).reshape(4 * r_, c_) if k_ in row_sharded
                        else jnp.transpose(g4, (0, 2, 1, 3)).reshape(r_, 4 * c_))

    assemble(early_names, _exchange_chips(halves_in(early_names), "gather_weights", True), "gather_weight_halves")

    cwid = conv_w.shape[2]
    sw = -(-max(d, cwid) // 128) * 128
    small_in = jnp.concatenate([jnp.pad(c, ((0, 0), (0, sw - d))), jnp.pad(conv_w[0], ((0, 4), (0, sw - cwid)))], axis=0)
    (small_all,) = _exchange8([small_in], "gather_c", True)
    cs = small_all[:, 0, :d]
    conv_w_full = jnp.concatenate([small_all[2 * j, 1:4, :cwid] for j in range(4)], axis=1)
    cs16 = jnp.concatenate([cs, c_ctx[None, :], jnp.zeros((7, d), F32)], axis=0)

    csh = w_mod.shape[2]
    b_mod_sh = lax.dynamic_slice(b_mod, (0, chip * csh), (1, csh))

    def mod_fwd_body(c_ref, w_ref, b_ref, o_ref):
        a = _silu(c_ref[...]).astype(MXU_DTYPE)
        o_ref[...] = jnp.dot(a, w_ref[...].astype(MXU_DTYPE), preferred_element_type=F32) + b_ref[...]

    mod_sh = pl.pallas_call(mod_fwd_body, name="mod_fwd", out_shape=_sds((16, csh), F32),
                            compiler_params=pltpu.CompilerParams(vmem_limit_bytes=VMEM_LIMIT))(cs16, w_mod[0], b_mod_sh)
    (mod_all,) = _exchange8([mod_sh], "gather_mod", True)
    mod_full = jnp.concatenate([mod_all[2 * j] for j in range(4)], axis=1)
    modv = jnp.stack([lax.dynamic_slice(mod_full, (me, 0), (1, d6)), mod_full[8:9]])

    def mod_parts(m):
        return [m[:, j * d:(j + 1) * d] for j in range(6)]

    u_off, kv_off, kr_off = 2 * d, 2 * d + SSM_WIDTH, 2 * d + SSM_WIDTH + KV_LORA
    q_off = -(-(kr_off + SLOT) // Q_LORA) * Q_LORA
    zw = q_off + Q_LORA
    wi = full["w_in"]
    s0, s1, s2, s3 = Q_LORA, Q_LORA + KV_LORA, Q_LORA + KV_LORA + QK_ROPE, Q_LORA + KV_LORA + QK_ROPE + SSM_WIDTH
    zpad = lambda w_: jnp.zeros((d, w_), MXU_DTYPE)
    win_p = jnp.concatenate([wi[:, s3:], wi[:, s2:s3], wi[:, s0:s1], wi[:, s1:s2], zpad(SLOT - QK_ROPE),
                             zpad(q_off - kr_off - SLOT), wi[:, :s0]], axis=1)
    wuq_p = jnp.pad(full["w_uq"].reshape(Q_LORA, N_HEADS, QK_DIM), ((0, 0), (0, 0), (0, SLOT - QK_DIM))).reshape(Q_LORA, N_HEADS * SLOT)
    wukv3 = full["w_ukv"].reshape(KV_LORA, N_HEADS, QK_NOPE + V_DIM)
    padh = lambda t: jnp.pad(t, ((0, 0), (0, 0), (0, SLOT - t.shape[2]))).reshape(t.shape[0], N_HEADS * SLOT)
    wukv_p = jnp.concatenate([padh(wukv3[:, :, :QK_NOPE]), padh(wukv3[:, :, QK_NOPE:])], axis=1)
    hw = N_HEADS * SLOT
    gain_p = lambda g_: jnp.tile(jnp.pad(g_[0], (0, SLOT - QK_DIM)), N_HEADS)[None, :]
    qg_p, kg_p = gain_p(q_norm_g), gain_p(k_norm_g)

    tok = jnp.arange(nl)
    freqs = ROPE_THETA ** (-jnp.arange(QK_ROPE // 4, dtype=F32) / (QK_ROPE // 4))
    ang = jnp.concatenate([(tok // GRID_W)[:, None] * freqs, (tok % GRID_W)[:, None] * freqs], axis=-1)
    cos_t = jnp.concatenate([jnp.cos(ang), jnp.ones((nc, 16), F32)], axis=0)
    sin_t = jnp.concatenate([jnp.sin(ang), jnp.zeros((nc, 16), F32)], axis=0)
    zl = lambda w_: jnp.zeros((n, w_), F32)
    rope_c = jnp.concatenate([jnp.ones((n, QK_NOPE), F32), cos_t, cos_t, zl(SLOT - QK_DIM)], axis=1)
    rope_sa = jnp.concatenate([zl(QK_NOPE), -sin_t, zl(SLOT - QK_NOPE - 16)], axis=1)
    rope_sb = jnp.concatenate([zl(QK_NOPE + 16), sin_t, zl(SLOT - QK_DIM)], axis=1)

    dirs = (("f", lam_re_f, lam_im_f, log_dt_f, c_re_f, c_im_f, False), ("b", lam_re_b, lam_im_b, log_dt_b, c_re_b, c_im_b, True))
    bbd, cbd_t, cbd_n, bbd_t, lamc, lamc_adj, disc_vjps = [], [], [], [], [], [], []
    for _, l_re, l_im, l_dt, cr_, ci_, rev_ in dirs:
        (lbr, lbi, bbr, bbi), vjp = jax.vjp(_s5_disc, l_re[0], l_im[0], l_dt[0], b_re[0], b_im[0])
        disc_vjps.append(vjp)
        bb = _diag_blocks(jnp.transpose(bbr, (0, 2, 1)), jnp.transpose(bbi, (0, 2, 1))).astype(MXU_DTYPE)
        cc_ = _diag_blocks(cr_[0], -ci_[0]).astype(MXU_DTYPE)
        bbd.append(bb)
        bbd_t.append(_block_transpose(bb))
        cbd_t.append(cc_)
        cbd_n.append(_block_transpose(cc_))
        lamc.append(_lam_consts(lbr, lbi, rev_, False))
        lamc_adj.append(_lam_consts(lbr, lbi, not rev_, True))
    t_scan = tr

    xa = jnp.concatenate([x[0], ctx[0]], axis=0)
    n1g, n2g = norm1_g, norm2_g

    def norm1_body(xv, m, g):
        sh1, sc1 = m[:, :d], m[:, d:2 * d]
        return _rms_fwd(xv, g, d) * (1.0 + sc1) + sh1

    (h1,) = _rowwise(norm1_body, name="norm1_fwd", nblk=nb, tr=tr, rows=[(xa, 0, d, 0)], sels=[modv], fulls=[n1g],
                     outs=[(d, MXU_DTYPE)], seg=nlb)
    z = _mm(h1, win_p, "nn", "in_proj")
    gl_cb, u_cb, kv_cb, kr_cb, q_cb = 0, u_off // SSM_WIDTH, kv_off // KV_LORA, kr_off // SLOT, q_off // Q_LORA

    (cqn,) = _rowwise(lambda v, g: _rms_fwd(v, g, Q_LORA), name="qa_norm_fwd", nblk=nlb, tr=tr,
                      rows=[(z, q_cb, Q_LORA, 0)], fulls=[q_a_g], outs=[(Q_LORA, MXU_DTYPE)])
    qh = _mm(cqn, wuq_p, "nn", "q_up")

    def qhead_body(qv, cv, sav, sbv, g):
        return jnp.concatenate([_rope_fwd(_rms_fwd(t, g[:, :SLOT], QK_DIM), cv, sav, sbv) for t in _heads(qv)], axis=1)

    rope_rows = lambda: [(rope_c, 0, SLOT, 0), (rope_sa, 0, SLOT, 0), (rope_sb, 0, SLOT, 0)]
    (q_p,) = _rowwise(qhead_body, name="q_head_fwd", nblk=nlb, tr=tr, rows=[(qh, 0, hw, 0)] + rope_rows(),
                      fulls=[qg_p], outs=[(hw, MXU_DTYPE)])

    (ckvn,) = _rowwise(lambda v, g: _rms_fwd(v, g, KV_LORA), name="kva_norm_fwd", nblk=nb, tr=tr,
                       rows=[(z, kv_cb, KV_LORA, 0)], fulls=[kv_a_g], outs=[(KV_LORA, MXU_DTYPE)])
    kvpre = _mm(ckvn, wukv_p, "nn", "kv_up")

    def khead_body(kv_, vv_, krv, cv, sav, sbv, g):
        kpe = pltpu.roll(krv, QK_NOPE, 1)
        ks = [_rope_fwd(_rms_fwd(t + kpe, g[:, :SLOT], QK_DIM), cv, sav, sbv) for t in _heads(kv_)]
        return jnp.concatenate(ks, axis=1), vv_

    k_p, v_p = _rowwise(khead_body, name="k_head_fwd", nblk=nb, tr=tr,
                        rows=[(kvpre, 0, hw, 0), (kvpre, 1, hw, 0), (z, kr_cb, SLOT, 0)] + rope_rows(),
                        fulls=[kg_p], outs=[(hw, MXU_DTYPE), (hw, MXU_DTYPE)])

    scale = QK_DIM ** -0.5
    o_p, lse, ffn_halves = _attn_fwd(q_p, k_p, v_p, nl, scale, side=(halves_in(ffn_names), True))
    assemble(ffn_names, ffn_halves, "gather_ffn_weight_halves")
    wglu, wout, wup, wdown = full["w_glu"], full["w_out"], full["w_up"], full["w_down"]
    wo_p = jnp.pad(full["w_o_attn"].reshape(N_HEADS, V_DIM, d), ((0, 0), (0, SLOT - V_DIM), (0, 0))).reshape(N_HEADS * SLOT, d)
    a_l = _mm(o_p, wo_p, "nn", "attn_out")

    scans = [_s5_scan(z, u_cb, bbd[j], cbd_n[j], lamc[j], t_scan, nl, dirs[j][6], "s5_scan_" + dirs[j][0]) for j in range(2)]
    xs, ydir = [s_[0] for s_ in scans], [s_[1] for s_ in scans]

    def ssm_out_body(uv, a, b, dsk):
        ys = uv * dsk + a + b
        return ys, _gelu(ys)

    ys, ge = _rowwise(ssm_out_body, name="s5_out_fwd", nblk=nlb, tr=tr,
                      rows=[(z, u_cb, SSM_WIDTH, 0), (ydir[0], 0, SSM_WIDTH, 0), (ydir[1], 0, SSM_WIDTH, 0)],
                      fulls=[d_skip], outs=[(SSM_WIDTH, F32), (SSM_WIDTH, MXU_DTYPE)])
    glu_out = _mm(ge, wglu, "nn", "glu_proj")

    def merge_body(ga, gs, av, val, gate):
        return _sigmoid(ga) * av + _sigmoid(gs) * (val * _sigmoid(gate))

    merge_rows = lambda: [(z, 0, d, 0), (z, 1, d, 0), (a_l, 0, d, 0), (glu_out, 0, d, 0), (glu_out, 1, d, 0)]
    (merged,) = _rowwise(merge_body, name="merge_fwd", nblk=nlb, tr=tr, rows=merge_rows(), outs=[(d, MXU_DTYPE)])
    mo = _mm(merged, wout, "nn", "out_proj")
    mod_x = modv[0]

    def norm2_body(xv, mov, m, g):
        g1, sh2, sc2 = m[:, 2 * d:3 * d], m[:, 3 * d:4 * d], m[:, 4 * d:5 * d]
        x1v = xv + g1 * mov
        return x1v, _rms_fwd(x1v, g, d) * (1.0 + sc2) + sh2

    x1, h2 = _rowwise(norm2_body, name="norm2_fwd", nblk=nlb, tr=tr, rows=[(xa, 0, d, 0), (mo, 0, d, 0)],
                      fulls=[mod_x, n2g], outs=[(d, F32), (d, MXU_DTYPE)])
    up = _mm(h2, wup, "nn", "ffn_up")
    cw8 = jnp.zeros((8, f2), F32).at[:3].set(conv_w_full)

    def conv3(t3, w8, off):
        p_, c_, n_ = t3
        return p_ * w8[0:1, off:off + fh] + c_ * w8[1:2, off:off + fh] + n_ * w8[2:3, off:off + fh]

    def conv_fwd_body(val3, gate3, w8, bias):
        val2 = conv3(val3, w8, 0) + bias[:, :fh]
        gate2 = conv3(gate3, w8, fh) + bias[:, fh:]
        return _silu(gate2) * val2

    (act,) = _rowwise(conv_fwd_body, name="conv_fwd", nblk=nlb, tr=tr, rows=[(up, 0, fh, 0), (up, 1, fh, 0)],
                      halo=(0, 1), fulls=[cw8, conv_b], outs=[(fh, MXU_DTYPE)])
    dn = _mm(act, wdown, "nn", "ffn_down")
    tgt = loss_target[0]

    def loss_body(x1v, dnv, tv, m):
        g2 = m[:, 5 * d:6 * d]
        e = x1v + g2 * dnv - tv
        dx2v = e * (1.0 / d)
        return dx2v, dx2v * g2, e * e, dx2v * dnv

    dx2, ddn, loss_acc, dg2_acc = _rowwise(loss_body, name="loss", nblk=nlb, tr=tr,
                                           rows=[(x1, 0, d, 0), (dn, 0, d, 0), (tgt, 0, d, 0)], fulls=[mod_x],
                                           outs=[(d, F32), (d, MXU_DTYPE)], accs=[d, d])
    loss = lax.psum(0.5 / d * jnp.sum(loss_acc), ("x", "y", "c"))

    g_big = {}

    def chip_partials(names_, name):
        pcs = []
        for k_ in names_:
            r_, c_ = weights[k_].shape[1:]
            if k_ in row_sharded:
                p4 = jnp.transpose(g_big[k_].reshape(4, 2, r_ // 2, c_), (1, 0, 2, 3))
            else:
                p4 = jnp.transpose(g_big[k_].reshape(2, r_ // 2, 4, c_), (0, 2, 1, 3))
            pcs.append(p4.reshape(2, 2 * r_, c_))
        out = []
        for k_, p_, got in zip(names_, pcs, _sibling_send(pcs, name)):
            rows4, c_ = got.shape
            own = lax.dynamic_index_in_dim(p_, mc, 0, keepdims=False)
            tr_ = _pick(rows4, (256, 128, 64, 32, 16))
            s32, sb = _rowwise(lambda a, b: (a + b, a + b), name="sum_chip_" + k_, nblk=rows4 // tr_, tr=tr_,
                               rows=[(own, 0, c_, 0), (got, 0, c_, 0)], outs=[(c_, F32), (c_, MXU_DTYPE)])
            out.append((s32, sb, rows4 // 4, c_))
        return out

    def scatter_in(partials):
        return [sb.reshape(4, rh, c_) for _, sb, rh, c_ in partials]

    def chip_sums(names_, partials, recv3):
        out = []
        for k_, (s32, _, rh, c_), r3 in zip(names_, partials, recv3):
            mine = lax.dynamic_slice(s32, (chip * rh, 0), (rh, c_))
            tr_ = _pick(rh, (256, 128, 64, 32, 16))
            (red,) = _rowwise(lambda a, b0, b1, b2: a + b0 + b1 + b2, name="sum_grad_" + k_, nblk=rh // tr_, tr=tr_,
                              rows=[(mine, 0, c_, 0)] + [(r3.reshape(3 * rh, c_), 0, c_, j * rh) for j in range(3)],
                              outs=[(c_, F32)])
            out.append(red)
        return out

    dact = _mm(ddn, wdown, "nt", "ffn_down_dx")
    g_big["w_down"] = _mm(act, ddn, "tn", "ffn_down_dw")

    def conv_bwd_body(val3, gate3, da, w8, bias):
        val2 = conv3(val3, w8, 0) + bias[:, :fh]
        gate2 = conv3(gate3, w8, fh) + bias[:, fh:]
        sg = _sigmoid(gate2)
        dval2 = da * (gate2 * sg)
        dgate2 = da * val2 * (sg * (1.0 + gate2 * (1.0 - sg)))
        du2 = jnp.concatenate([dval2, dgate2], axis=1)
        taps = [jnp.concatenate([dval2 * val3[j], dgate2 * gate3[j]], axis=1) for j in range(3)]
        return du2, du2, taps[0], taps[1], taps[2]

    du2, dcb_acc, dcw0, dcw1, dcw2 = _rowwise(conv_bwd_body, name="conv_bwd", nblk=nlb, tr=tr,
                                              rows=[(up, 0, fh, 0), (up, 1, fh, 0), (dact, 0, fh, 0)], halo=(0, 1),
                                              fulls=[cw8, conv_b], outs=[(f2, F32)], accs=[f2, f2, f2, f2])

    def conv_t_body(dval3, dgate3, w8):
        rev = lambda t3: (t3[2], t3[1], t3[0])
        return jnp.concatenate([conv3(rev(dval3), w8, 0), conv3(rev(dgate3), w8, fh)], axis=1)

    (dup,) = _rowwise(conv_t_body, name="conv_bwd_dx", nblk=nlb, tr=tr, rows=[(du2, 0, fh, 0), (du2, 1, fh, 0)],
                      halo=(0, 1), fulls=[cw8], outs=[(f2, MXU_DTYPE)])
    dh2 = _mm(dup, wup, "nt", "ffn_up_dx")
    g_big["w_up"] = _mm(h2, dup, "tn", "ffn_up_dw")

    def norm2_bwd_body(x1v, dh, dx2v, mov, m, g):
        g1, sc2 = m[:, 2 * d:3 * d], m[:, 4 * d:5 * d]
        y = _rms_fwd(x1v, g, d)
        dxn, dgc = _rms_bwd(x1v, g, dh * (1.0 + sc2), d)
        dx1v = dx2v + dxn
        return dx1v, dx1v * g1, dgc, dh, dh * y, dx1v * mov

    dx1, dmo, dn2g_acc, dsh2_acc, dsc2_acc, dg1_acc = _rowwise(
        norm2_bwd_body, name="norm2_bwd", nblk=nlb, tr=tr,
        rows=[(x1, 0, d, 0), (dh2, 0, d, 0), (dx2, 0, d, 0), (mo, 0, d, 0)], fulls=[mod_x, n2g],
        outs=[(d, F32), (d, MXU_DTYPE)], accs=[d, d, d, d])
    dmerged = _mm(dmo, wout, "nt", "out_proj_dx")
    g_big["w_out"] = _mm(merged, dmo, "tn", "out_proj_dw")

    def merge_bwd_body(ga, gs, av, val, gate, dm):
        sa_, ss_, sg_ = _sigmoid(ga), _sigmoid(gs), _sigmoid(gate)
        s_l = val * sg_
        ds_l = dm * ss_
        dga = dm * av * sa_ * (1.0 - sa_)
        dgs = dm * s_l * ss_ * (1.0 - ss_)
        dval = ds_l * sg_
        dgate = ds_l * val * sg_ * (1.0 - sg_)
        return dm * sa_, jnp.concatenate([dval, dgate], axis=1), jnp.concatenate([dga, dgs], axis=1)

    da_l, dglu, dgl = _rowwise(merge_bwd_body, name="merge_bwd", nblk=nlb, tr=tr,
                               rows=merge_rows() + [(dmerged, 0, d, 0)],
                               outs=[(d, MXU_DTYPE), (2 * d, MXU_DTYPE), (2 * d, MXU_DTYPE)])
    dge = _mm(dglu, wglu, "nt", "glu_proj_dx")
    g_big["w_glu"] = _mm(ge, dglu, "tn", "glu_proj_dw")

    def ssm_out_bwd_body(ysv, dgev, uv, dsk):
        dys_ = dgev * _dgelu(ysv)
        return dys_, dys_ * dsk, dys_ * uv

    dys, du_skip, ddskip_acc = _rowwise(ssm_out_bwd_body, name="s5_out_bwd", nblk=nlb, tr=tr,
                                        rows=[(ys, 0, SSM_WIDTH, 0), (dge, 0, SSM_WIDTH, 0), (z, u_cb, SSM_WIDTH, 0)],
                                        fulls=[d_skip], outs=[(SSM_WIDTH, F32), (SSM_WIDTH, F32)], accs=[SSM_WIDTH])
    s5b = [_s5_bwd(dys, z, u_cb, xs[j], cbd_t[j], bbd_t[j], lamc_adj[j], t_scan, nl, dirs[j][6], "s5_bwd_" + dirs[j][0])
           for j in range(2)]
    du_nat = s5b[0][0] + s5b[1][0] + jnp.concatenate([du_skip, jnp.zeros((nc, SSM_WIDTH), F32)], axis=0)

    do_f = _mm(da_l, wo_p, "nt", "attn_out_dx")
    g_wo_p = _mm(o_p, da_l, "tn", "attn_out_dw")
    g_big["w_o_attn"] = g_wo_p.reshape(N_HEADS, SLOT, d)[:, :V_DIM].reshape(N_HEADS * V_DIM, d)
    ffn_partials = chip_partials(ffn_names, "ffn_grads_to_sibling")

    dq_p, dk_p, dv_p, ffn_recv = _attn_bwd(q_p, k_p, v_p, do_f, o_p, lse, nl, scale, side=(scatter_in(ffn_partials), False))

    def qhead_bwd_body(qv, dqv, cv, sav, sbv, g):
        dxs, dgs = [], []
        for t, dt_ in zip(_heads(qv), _heads(dqv)):
            dx_, dg_ = _rms_bwd(t, g[:, :SLOT], _rope_bwd(dt_, cv, sav, sbv), QK_DIM)
            dxs.append(dx_)
            dgs.append(dg_)
        return jnp.concatenate(dxs, axis=1), jnp.concatenate(dgs, axis=1)

    dqh, dqg_acc = _rowwise(qhead_bwd_body, name="q_head_bwd", nblk=nlb, tr=tr,
                            rows=[(qh, 0, hw, 0), (dq_p, 0, hw, 0)] + rope_rows(), fulls=[qg_p],
                            outs=[(hw, MXU_DTYPE)], accs=[hw])
    dcqn = _mm(dqh, wuq_p, "nt", "q_up_dx")
    g_wuq_p = _mm(cqn, dqh, "tn", "q_up_dw")
    dcq, dqag_acc = _rowwise(lambda v, dy, g: _rms_bwd(v, g, dy, Q_LORA), name="qa_norm_bwd", nblk=nlb, tr=tr,
                             rows=[(z, q_cb, Q_LORA, 0), (dcqn, 0, Q_LORA, 0)], fulls=[q_a_g],
                             outs=[(Q_LORA, MXU_DTYPE)], accs=[Q_LORA])

    def khead_bwd_body(kv_, krv, dkv_, dvv_, cv, sav, sbv, g):
        kpe = pltpu.roll(krv, QK_NOPE, 1)
        lane = lax.broadcasted_iota(jnp.int32, krv.shape, 1)
        dxs, dgs, dkr_ = [], [], jnp.zeros(krv.shape, F32)
        for t, dt_ in zip(_heads(kv_), _heads(dkv_)):
            dx_, dg_ = _rms_bwd(t + kpe, g[:, :SLOT], _rope_bwd(dt_, cv, sav, sbv), QK_DIM)
            dxs.append(jnp.where(lane < QK_NOPE, dx_, 0.0))
            dgs.append(dg_)
            dkr_ = dkr_ + dx_
        dkr_ = jnp.where(lane < QK_ROPE, pltpu.roll(dkr_, SLOT - QK_NOPE, 1), 0.0)
        return jnp.concatenate(dxs + [dvv_], axis=1), dkr_, jnp.concatenate(dgs, axis=1)

    dkvpre, dkr, dkg_acc = _rowwise(khead_bwd_body, name="k_head_bwd", nblk=nb, tr=tr,
                                    rows=[(kvpre, 0, hw, 0), (z, kr_cb, SLOT, 0), (dk_p, 0, hw, 0), (dv_p, 0, hw, 0)] + rope_rows(),
                                    fulls=[kg_p], outs=[(2 * hw, MXU_DTYPE), (SLOT, MXU_DTYPE)], accs=[hw])
    dckvn = _mm(dkvpre, wukv_p, "nt", "kv_up_dx")
    g_wukv_p = _mm(ckvn, dkvpre, "tn", "kv_up_dw")
    dckv, dkvag_acc = _rowwise(lambda v, dy, g: _rms_bwd(v, g, dy, KV_LORA), name="kva_norm_bwd", nblk=nb, tr=tr,
                               rows=[(z, kv_cb, KV_LORA, 0), (dckvn, 0, KV_LORA, 0)], fulls=[kv_a_g],
                               outs=[(KV_LORA, MXU_DTYPE)], accs=[KV_LORA])

    padc = lambda t: jnp.concatenate([t, jnp.zeros((nc, t.shape[1]), t.dtype)], axis=0)
    dz = jnp.concatenate([padc(dgl), du_nat.astype(MXU_DTYPE), dckv, dkr,
                          jnp.zeros((n, q_off - kr_off - SLOT), MXU_DTYPE), padc(dcq)], axis=1)
    dh1 = _mm(dz, win_p, "nt", "in_proj_dx")
    g_win_p = _mm(h1, dz, "tn", "in_proj_dw")

    def norm1_bwd_body(xv, dh, dx1v, m, g):
        sc1 = m[:, d:2 * d]
        y = _rms_fwd(xv, g, d)
        dxn, dgc = _rms_bwd(xv, g, dh * (1.0 + sc1), d)
        return dxn + dx1v, dgc, dh, dh * y

    dxa, dn1g_acc, dsh1_acc, dsc1_acc = _rowwise(norm1_bwd_body, name="norm1_bwd", nblk=nb, tr=tr,
                                                 rows=[(xa, 0, d, 0), (dh1, 0, d, 0), (dx1, 0, d, 0)], sels=[modv],
                                                 fulls=[n1g], outs=[(d, F32)], accs=[d, d, d], seg=nlb)
    grad_x = dxa[:nl][None]

    red8 = lambda a: jnp.sum(a, axis=-2)
    dmod_own = jnp.concatenate([red8(dsh1_acc[0]), red8(dsc1_acc[0]), red8(dg1_acc), red8(dsh2_acc), red8(dsc2_acc), red8(dg2_acc)])
    dmod_ctx = jnp.concatenate([red8(dsh1_acc[1]), red8(dsc1_acc[1]), jnp.zeros((4 * d,), F32)])
    dm_in = jnp.concatenate([dmod_own[None, :], dmod_ctx[None, :], jnp.zeros((6, d6), F32)], axis=0)
    (dm_all,) = _exchange8([dm_in], "gather_dmod", True)
    dm_own_sh = lax.dynamic_slice(dm_all[:, 0, :], (0, chip * csh), (8, csh))
    dm_ctx_sh = lax.dynamic_slice(dm_all[:, 1, :], (0, chip * csh), (8, csh))

    def mod_bwd_body(c_ref, own_ref, ctx_ref, w_ref, gw_ref, gb_ref, gc_ref):
        cv = c_ref[...]
        a = _silu(cv).astype(MXU_DTYPE)
        own = own_ref[...]
        ctx_tot = ctx_ref[0:1, :]
        for j in range(1, 8):
            ctx_tot = ctx_tot + ctx_ref[j:j + 1, :]
        g16 = jnp.concatenate([own, jnp.broadcast_to(ctx_tot, own.shape)], axis=0)
        rid = lax.broadcasted_iota(jnp.int32, g16.shape, 0)
        g16 = jnp.where(rid <= 8, g16, 0.0)
        gw_ref[...] = lax.dot_general(a, g16.astype(MXU_DTYPE), (((0,), (0,)), ((), ())), preferred_element_type=F32)
        gb_ref[...] = jnp.broadcast_to(jnp.sum(own, axis=0, keepdims=True) + ctx_tot, gb_ref.shape)
        gc = lax.dot_general(jnp.broadcast_to(ctx_tot, own.shape).astype(MXU_DTYPE), w_ref[...].astype(MXU_DTYPE),
                             (((1,), (1,)), ((), ())), preferred_element_type=F32)
        gc_ref[...] = gc * _dsilu(cv[8:9, :])

    g_wmod, g_bmod_sh, g_cctx_part = pl.pallas_call(
        mod_bwd_body, name="mod_bwd", out_shape=[_sds((d, csh), F32), _sds((8, csh), F32), _sds((8, d), F32)],
        compiler_params=pltpu.CompilerParams(vmem_limit_bytes=VMEM_LIMIT))(cs16, dm_own_sh, dm_ctx_sh, w_mod[0])
    north = (mc == 0).astype(F32)
    g_bmod_part = lax.dynamic_update_slice(jnp.zeros((1, d6), F32), g_bmod_sh[0:1] * north, (0, chip * csh))
    g_cctx_part = g_cctx_part[0] * north

    small_g = {}
    for j, dr in enumerate(dirs):
        sfx = dr[0]
        _, dbbd_j, dcbd_j, dlam_j = s5b[j]
        dl = red8(dlam_j).reshape(N_CG, 2, CG_STATES)
        db_re, db_im = _diag_extract(dbbd_j)
        cot = (dl[:, 0].reshape(SSM_GROUPS, SSM_STATE), dl[:, 1].reshape(SSM_GROUPS, SSM_STATE),
               jnp.transpose(db_re, (0, 2, 1)), jnp.transpose(db_im, (0, 2, 1)))
        g_lre, g_lim, g_ldt, g_bre, g_bim = disc_vjps[j](cot)
        small_g["lam_re_" + sfx], small_g["lam_im_" + sfx], small_g["log_dt_" + sfx] = g_lre, g_lim, g_ldt
        small_g["b_re"] = small_g.get("b_re", 0.0) + g_bre
        small_g["b_im"] = small_g.get("b_im", 0.0) + g_bim
        dc_re, dc_im = _diag_extract(dcbd_j)
        small_g["c_re_" + sfx], small_g["c_im_" + sfx] = dc_re, -dc_im
    head_fold = lambda acc: jnp.sum(red8(acc).reshape(N_HEADS, SLOT), axis=0)[:QK_DIM]
    small_g.update(c_ctx=g_cctx_part, b_mod=g_bmod_part[0], norm1_g=red8(dn1g_acc[0]) + red8(dn1g_acc[1]),
                   norm2_g=red8(dn2g_acc), q_a_g=red8(dqag_acc), kv_a_g=red8(dkvag_acc), q_norm_g=head_fold(dqg_acc),
                   k_norm_g=head_fold(dkg_acc), d_skip=red8(ddskip_acc), conv_b=red8(dcb_acc))
    g_convw_full = jnp.stack([red8(dcw0), red8(dcw1), red8(dcw2)])
    small_names = ["c_ctx", "b_mod", "norm1_g", "norm2_g", "q_a_g", "kv_a_g", "q_norm_g", "k_norm_g",
                   "lam_re_f", "lam_im_f", "log_dt_f", "c_re_f", "c_im_f", "lam_re_b", "lam_im_b", "log_dt_b",
                   "c_re_b", "c_im_b", "b_re", "b_im", "d_skip", "conv_b"]
    small_shapes = [weights[k].shape for k in small_names]
    spack = _pack([small_g[k] for k in small_names] + [g_convw_full], rows_mult=8)
    sred = _sum8(_exchange8([spack], "gather_small_grads", True)[0], "sum_small_grads")
    sg_list = _unpack(sred, small_shapes + [(3, f2)])
    g_small = dict(zip(small_names, sg_list[:-1]))
    g_small["conv_w"] = lax.dynamic_slice(sg_list[-1], (0, chip * cwid), (3, cwid))[None]

    gwi = g_win_p
    g_big["w_in"] = jnp.concatenate([gwi[:, q_off:q_off + Q_LORA], gwi[:, kv_off:kv_off + KV_LORA],
                                     gwi[:, kr_off:kr_off + QK_ROPE], gwi[:, u_off:u_off + SSM_WIDTH], gwi[:, :2 * d]], axis=1)
    g_big["w_uq"] = g_wuq_p.reshape(Q_LORA, N_HEADS, SLOT)[:, :, :QK_DIM].reshape(Q_LORA, N_HEADS * QK_DIM)
    gk3 = g_wukv_p[:, :hw].reshape(KV_LORA, N_HEADS, SLOT)[:, :, :QK_NOPE]
    gv3 = g_wukv_p[:, hw:].reshape(KV_LORA, N_HEADS, SLOT)[:, :, :V_DIM]
    g_big["w_ukv"] = jnp.concatenate([gk3, gv3], axis=2).reshape(KV_LORA, N_HEADS * (QK_NOPE + V_DIM))

    early_partials = chip_partials(early_names, "grads_to_sibling")
    early_recv = _exchange_chips(scatter_in(early_partials), "scatter_weight_grads", False)
    reduced = dict(zip(early_names + ffn_names, chip_sums(early_names, early_partials, early_recv)
                       + chip_sums(ffn_names, ffn_partials, ffn_recv)))
    both = _sibling_exchange([reduced[k_] for k_ in big_names], "exchange_halves")
    g_sh = {k_: b_.reshape((1,) + weights[k_].shape[1:]) for k_, b_ in zip(big_names, both)}
    g_sh["w_mod"] = g_wmod[None]

    grads = {**g_sh, **g_small}
    outs_d, outs_m, outs_v = {}, {}, {}
    for k_ in ["w_mod"] + big_names:
        shp = weights[k_].shape
        res = _adamw(*[t.reshape(shp[1:]) for t in (grads[k_], weights[k_], mom_m[k_], mom_v[k_])], "adamw_" + k_)
        for dst, buf in zip((outs_d, outs_m, outs_v), res):
            dst[k_] = buf.reshape(shp)
    adam_small = small_names + ["conv_w"]
    shapes = [weights[k_].shape for k_ in adam_small]
    res = _adamw(*[_pack([src[k_] for k_ in adam_small], rows_mult=8) for src in (grads, weights, mom_m, mom_v)], "adamw_small")
    for dst, buf in zip((outs_d, outs_m, outs_v), res):
        dst.update(zip(adam_small, _unpack(buf, shapes)))
    grads = {k_: grads[k_].reshape(weights[k_].shape) for k_ in names}
    return (loss, grad_x, *[grads[k_] for k_ in names], *[outs_d[k_] for k_ in names],
            *[outs_m[k_] for k_ in names], *[outs_v[k_] for k_ in names])
```

```python
import functools
import math

import numpy as np
import jax
import jax.numpy as jnp
from jax import lax
from jax.experimental import pallas as pl
from jax.experimental.pallas import tpu as pltpu

F32 = jnp.float32
MXU_DTYPE = jnp.bfloat16
MESH = pl.DeviceIdType.MESH

EPS = 1e-6
N_HEADS = 8
QK_NOPE = 64
QK_ROPE = 32
QK_DIM = QK_NOPE + QK_ROPE
V_DIM = 64
SLOT = 128
Q_LORA = 384
KV_LORA = 256
GRID_W = 64
ROPE_THETA = 10000.0
SSM_WIDTH = 512
SSM_GROUP = 16
SSM_GROUPS = 32
SSM_STATE = 64
N_STATE = SSM_GROUPS * SSM_STATE
CG_STATES = 512
N_CG = N_STATE // CG_STATES
CG_CHANNELS = SSM_WIDTH // N_CG
SCAN_LANES = 512
PACK_W = 1024

ADAM_LR = 0.001
ADAM_B1 = 0.9
ADAM_B2 = 0.999
ADAM_EPS = 1e-08
ADAM_WD = 0.01
ADAM_STEP = 10

VMEM_LIMIT = 56 * 1024 * 1024
LOG2E = 1.4426950408889634


def _pick(n, cands):
    for c in cands:
        if c <= n and n % c == 0:
            return c
    return n


def _cparams(sem):
    return pltpu.CompilerParams(dimension_semantics=sem, vmem_limit_bytes=VMEM_LIMIT)


def _sds(shape, dtype):
    return jax.ShapeDtypeStruct(tuple(shape), dtype)


_K_CANDS = (2816, 2048, 1536, 1408, 1280, 1152, 1024, 896, 768, 704, 640, 512, 384, 256, 128, 64, 32, 16)
_M_CANDS = (2048, 1408, 1024, 768, 512, 384, 256, 128, 64, 32, 16)
_N_CANDS = (1408, 1152, 1024, 768, 512, 384, 256, 128)
MM_VMEM_BUDGET = 40 * 1024 * 1024


def _mm_tiles(m, n, k_opts, a_bytes, b_bytes, o_bytes, m_cands):
    tn = n if n <= _N_CANDS[0] else _pick(n, _N_CANDS)
    for tk in k_opts:
        for tm in ((m,) if m <= m_cands[0] else ()) + tuple(t for t in m_cands if t < m and m % t == 0):
            if 2 * (tm * tk * a_bytes + tk * tn * b_bytes + tm * tn * o_bytes) + tm * tn * 4 <= MM_VMEM_BUDGET:
                return tm, tn, tk
    raise ValueError("no matmul tiling fits")


def _mm(a, b, mode, name, out_dtype=F32, rows=None, a_off=0, b_off=0):
    a_bytes, b_bytes, o_bytes = a.dtype.itemsize, b.dtype.itemsize, jnp.dtype(out_dtype).itemsize
    if mode == "tn":
        t_rows = rows or a.shape[0]
        m, n = a.shape[1], b.shape[1]
        k_opts = tuple(t for t in _K_CANDS if t <= t_rows and t_rows % t == 0) or (t_rows,)
        tm, tn, tk = _mm_tiles(m, n, k_opts, a_bytes, b_bytes, o_bytes, _M_CANDS[1:])
        nk = t_rows // tk
        ao, bo = a_off // tk, b_off // tk
        grid = (m // tm, n // tn, nk)
        in_specs = [pl.BlockSpec((tk, tm), lambda i, j, k: (k + ao, i)),
                    pl.BlockSpec((tk, tn), lambda i, j, k: (k + bo, j))]
        dn = (((0,), (0,)), ((), ()))
    else:
        m = rows or a.shape[0]
        kdim = a.shape[1]
        n = b.shape[1] if mode == "nn" else b.shape[0]
        k_opts = (kdim,) + tuple(t for t in _K_CANDS if t < kdim and kdim % t == 0)
        tm, tn, tk = _mm_tiles(m, n, k_opts, a_bytes, b_bytes, o_bytes, _M_CANDS)
        nk = kdim // tk
        ao = a_off // tm
        grid = (m // tm, n // tn, nk)
        if mode == "nn":
            in_specs = [pl.BlockSpec((tm, tk), lambda i, j, k: (i + ao, k)),
                        pl.BlockSpec((tk, tn), lambda i, j, k: (k, j))]
            dn = (((1,), (0,)), ((), ()))
        else:
            in_specs = [pl.BlockSpec((tm, tk), lambda i, j, k: (i + ao, k)),
                        pl.BlockSpec((tn, tk), lambda i, j, k: (j, k))]
            dn = (((1,), (1,)), ((), ()))
    use_scratch = nk > 1 and out_dtype != F32

    def body(a_ref, b_ref, o_ref, *scr):
        r = lax.dot_general(a_ref[...].astype(MXU_DTYPE), b_ref[...].astype(MXU_DTYPE), dn,
                            preferred_element_type=F32)
        if nk == 1:
            o_ref[...] = r.astype(o_ref.dtype)
        else:
            k = pl.program_id(2)
            acc = scr[0] if use_scratch else o_ref

            @pl.when(k == 0)
            def _():
                acc[...] = r

            @pl.when(k > 0)
            def _():
                acc[...] += r

            if use_scratch:
                @pl.when(k == nk - 1)
                def _():
                    o_ref[...] = acc[...].astype(o_ref.dtype)

    return pl.pallas_call(
        body, name=name, grid=grid, in_specs=in_specs,
        out_specs=pl.BlockSpec((tm, tn), lambda i, j, k: (i, j)),
        out_shape=_sds((m, n), out_dtype),
        scratch_shapes=[pltpu.VMEM((tm, tn), F32)] if use_scratch else [],
        compiler_params=_cparams(("parallel", "parallel", "arbitrary")),
    )(a, b)


def _rowwise(body, *, name, nblk, tr, rows=(), halo=(), sels=(), fulls=(), outs=(), accs=(), seg=None, side=None):
    n_rows, n_sel, n_full, n_out, n_acc = len(rows), len(sels), len(fulls), len(outs), len(accs)
    halo = tuple(halo)
    maxw = max([r[2] for r in rows] + [o[0] for o in outs] + list(accs))
    sr = _pick(tr, tuple(s for s in (256, 128, 64, 32, 16) if s * maxw <= 131072) or (16,))
    nsub = tr // sr
    total8 = nblk * tr // 8

    def seg_of(i):
        return jnp.where(i >= seg, 1, 0) if seg is not None else 0

    in_specs, operands = [], []
    for arr, cb, w, roff in rows:
        ob = roff // tr
        last = arr.shape[0] // tr - 1
        in_specs.append(pl.BlockSpec((tr, w), lambda i, cb=cb, ob=ob, last=last: (jnp.minimum(i + ob, last), cb)))
        operands.append(arr)
    for h in halo:
        arr, cb, w, roff = rows[h]
        o8, t8 = roff // 8, tr // 8
        in_specs.append(pl.BlockSpec((8, w), lambda i, cb=cb, o8=o8, t8=t8: (jnp.maximum(i * t8 - 1, 0) + o8, cb)))
        in_specs.append(pl.BlockSpec((8, w), lambda i, cb=cb, o8=o8, t8=t8: (jnp.minimum((i + 1) * t8, total8 - 1) + o8, cb)))
        operands += [arr, arr]
    for arr in sels:
        in_specs.append(pl.BlockSpec((None,) + arr.shape[1:], lambda i: (seg_of(i), 0, 0)))
        operands.append(arr)
    for arr in fulls:
        in_specs.append(pl.BlockSpec(arr.shape, lambda i: (0, 0)))
        operands.append(arr)
    out_specs, out_shape = [], []
    for w, dt in outs:
        out_specs.append(pl.BlockSpec((tr, w), lambda i: (i, 0)))
        out_shape.append(_sds((nblk * tr, w), dt))
    for w in accs:
        if seg is None:
            out_specs.append(pl.BlockSpec((8, w), lambda i: (0, 0)))
            out_shape.append(_sds((8, w), F32))
        else:
            out_specs.append(pl.BlockSpec((None, 8, w), lambda i: (seg_of(i), 0, 0)))
            out_shape.append(_sds((2, 8, w), F32))
    n_halo = 2 * len(halo)

    def kern(*refs):
        row_refs = refs[:n_rows]
        halo_refs = refs[n_rows:n_rows + n_halo]
        sel_refs = refs[n_rows + n_halo:n_rows + n_halo + n_sel]
        full_refs = refs[n_rows + n_halo + n_sel:n_rows + n_halo + n_sel + n_full]
        o0 = n_rows + n_halo + n_sel + n_full
        out_refs = refs[o0:o0 + n_out]
        acc_refs = refs[o0 + n_out:o0 + n_out + n_acc]
        i = pl.program_id(0)
        if n_acc:
            first = (i == 0) if seg is None else ((i == 0) | (i == seg))

            @pl.when(first)
            def _():
                for a_ref in acc_refs:
                    a_ref[...] = jnp.zeros(a_ref.shape, F32)

        def sub(s, carry):
            r0 = pl.multiple_of(s * sr, sr)
            vals = []
            for idx, r in enumerate(row_refs):
                cur = r[pl.ds(r0, sr), :]
                if idx in halo:
                    hp = halo_refs[2 * halo.index(idx)]
                    hn = halo_refs[2 * halo.index(idx) + 1]
                    cur = cur.astype(F32)
                    rid = lax.broadcasted_iota(jnp.int32, cur.shape, 0)
                    lo = r[pl.ds(pl.multiple_of(jnp.maximum(r0 - 8, 0), 8), 8), :].astype(F32)
                    lo = jnp.where(s == 0, hp[...].astype(F32), lo)
                    lo = jnp.where((s == 0) & (i == 0), 0.0, lo)
                    hi = r[pl.ds(pl.multiple_of(jnp.minimum(r0 + sr, tr - 8), 8), 8), :].astype(F32)
                    hi = jnp.where(s == nsub - 1, hn[...].astype(F32), hi)
                    hi = jnp.where((s == nsub - 1) & (i == nblk - 1), 0.0, hi)
                    prev = jnp.where(rid == 0, jnp.broadcast_to(lo[7:8, :], cur.shape), pltpu.roll(cur, 1, 0))
                    nxt = jnp.where(rid == sr - 1, jnp.broadcast_to(hi[0:1, :], cur.shape), pltpu.roll(cur, sr - 1, 0))
                    vals.append((prev, cur, nxt))
                else:
                    vals.append(cur)
            res = body(*vals, *[r[...] for r in sel_refs], *[r[...] for r in full_refs])
            if not isinstance(res, (tuple, list)):
                res = (res,)
            for o_ref, v in zip(out_refs, res[:n_out]):
                o_ref[pl.ds(r0, sr), :] = v.astype(o_ref.dtype)
            for a_ref, v in zip(acc_refs, res[n_out:]):
                a_ref[...] += jnp.sum(v.astype(F32).reshape(sr // 8, 8, v.shape[-1]), axis=0)
            return carry

        lax.fori_loop(0, nsub, sub, 0)

    n_res = n_out + n_acc
    kern, s_in, s_out, s_shape, s_scr = _ride_along(kern, len(operands), n_res, 0, (nblk,), side)
    res = pl.pallas_call(
        kern, name=name, grid=(nblk,), in_specs=in_specs + s_in, out_specs=out_specs + s_out,
        out_shape=out_shape + s_shape, scratch_shapes=s_scr, compiler_params=_cparams(("arbitrary",)),
    )(*operands, *(side[0] if side else ()))
    return res if side is None else (res[:n_res], res[n_res:])


def _sigmoid(x):
    return 1.0 / (1.0 + jnp.exp(-x))


def _silu(x):
    return x * _sigmoid(x)


def _dsilu(x):
    s = _sigmoid(x)
    return s * (1.0 + x * (1.0 - s))


_GELU_K = math.sqrt(2.0 / math.pi)


def _gelu(x):
    return 0.5 * x * (1.0 + jnp.tanh(_GELU_K * (x + 0.044715 * x * x * x)))


def _dgelu(x):
    t = jnp.tanh(_GELU_K * (x + 0.044715 * x * x * x))
    return 0.5 * (1.0 + t) + 0.5 * x * (1.0 - t * t) * _GELU_K * (1.0 + 3.0 * 0.044715 * x * x)


def _rms_fwd(x, g, width):
    r = lax.rsqrt(jnp.sum(x * x, axis=-1, keepdims=True) * (1.0 / width) + EPS)
    return x * r * g


def _rms_bwd(x, g, dy, width):
    r = lax.rsqrt(jnp.sum(x * x, axis=-1, keepdims=True) * (1.0 / width) + EPS)
    xn = x * r
    dyg = dy * g
    dx = r * (dyg - xn * (jnp.sum(dyg * xn, axis=-1, keepdims=True) * (1.0 / width)))
    return dx, dy * xn


def _rope_fwd(y, c, sa, sb):
    return y * c + pltpu.roll(y, SLOT - 16, 1) * sa + pltpu.roll(y, 16, 1) * sb


def _rope_bwd(d, c, sa, sb):
    return d * c + pltpu.roll(d * sa, 16, 1) + pltpu.roll(d * sb, SLOT - 16, 1)


def _heads(v):
    return [v[:, h * SLOT:(h + 1) * SLOT] for h in range(N_HEADS)]


def _attn_fwd(q, k, v, nl, scale, side=None):
    n = k.shape[0]
    tq = _pick(nl, (4096, 2048, 1024, 512, 256, 128))
    tk = _pick(n, (2816, 1408, 1152, 768, 384, 256, 128))
    sub = min(tq, 512)
    nk = n // tk
    rep = tk // SLOT
    c = scale * LOG2E

    def body(q_ref, k_ref, v_ref, o_ref, lse_ref, m_sc, l_sc, acc_sc):
        ki = pl.program_id(2)

        @pl.when(ki == 0)
        def _():
            m_sc[...] = jnp.full(m_sc.shape, -jnp.inf, F32)
            l_sc[...] = jnp.zeros(l_sc.shape, F32)
            acc_sc[...] = jnp.zeros(acc_sc.shape, F32)

        kb, vb = k_ref[...], v_ref[...]
        for sb in range(tq // sub):
            rows = slice(sb * sub, (sb + 1) * sub)
            s = lax.dot_general(q_ref[rows, :], kb, (((1,), (1,)), ((), ())), preferred_element_type=F32)
            m_prev = m_sc[rows, :]
            m_new = jnp.maximum(m_prev, jnp.max(s, axis=1, keepdims=True) * c)
            alpha = jnp.exp2(m_prev - m_new)
            p = jnp.exp2(s * c - jnp.tile(m_new, (1, rep)))
            l_sc[rows, :] = alpha * l_sc[rows, :] + jnp.sum(p, axis=1, keepdims=True)
            acc_sc[rows, :] = alpha * acc_sc[rows, :] + jnp.dot(p.astype(MXU_DTYPE), vb, preferred_element_type=F32)
            m_sc[rows, :] = m_new

        @pl.when(ki == nk - 1)
        def _():
            l = l_sc[...]
            o_ref[...] = (acc_sc[...] / l).astype(o_ref.dtype)
            lse_ref[...] = jnp.transpose(m_sc[...] + jnp.log2(l))[0:8, :]

    grid = (N_HEADS, nl // tq, nk)
    body, s_in, s_out, s_shape, s_scr = _ride_along(body, 3, 2, 3, grid, side)
    res = pl.pallas_call(
        body, name="attn_fwd", grid=grid,
        in_specs=[pl.BlockSpec((tq, SLOT), lambda h, i, j: (i, h)),
                  pl.BlockSpec((tk, SLOT), lambda h, i, j: (j, h)),
                  pl.BlockSpec((tk, SLOT), lambda h, i, j: (j, h))] + s_in,
        out_specs=[pl.BlockSpec((tq, SLOT), lambda h, i, j: (i, h)),
                   pl.BlockSpec((None, 8, tq), lambda h, i, j: (h, 0, i))] + s_out,
        out_shape=[_sds((nl, N_HEADS * SLOT), MXU_DTYPE), _sds((N_HEADS, 8, nl), F32)] + s_shape,
        scratch_shapes=[pltpu.VMEM((tq, SLOT), F32), pltpu.VMEM((tq, SLOT), F32), pltpu.VMEM((tq, SLOT), F32)] + s_scr,
        compiler_params=_cparams(("arbitrary", "arbitrary", "arbitrary")),
    )(q, k, v, *(side[0] if side else ()))
    return res[0], res[1], res[2:]


def _attn_bwd(q, k, v, do, o, lse_t, nl, scale, side=None):
    n = k.shape[0]
    tq = _pick(nl, (2048, 1024, 512, 256, 128))
    tk = _pick(n, (2816, 1408, 1152, 768, 384, 256, 128))
    sub = _pick(tk, (256, 128))
    nq, nk = nl // tq, n // tk
    c = scale * LOG2E

    def body(q_ref, k_ref, v_ref, do_ref, o_ref, lse_ref, dq_ref, dk_ref, dv_ref, dq_acc, dk_acc, dv_acc):
        ki, qi = pl.program_id(1), pl.program_id(2)

        @pl.when((ki == 0) & (qi == 0))
        def _():
            dq_acc[...] = jnp.zeros(dq_acc.shape, F32)

        @pl.when(qi == 0)
        def _():
            dk_acc[...] = jnp.zeros(dk_acc.shape, F32)
            dv_acc[...] = jnp.zeros(dv_acc.shape, F32)

        qb, dof = q_ref[...], do_ref[...]
        dob = dof.astype(MXU_DTYPE)
        lse_r = lse_ref[0:1, :]
        dl_r = jnp.sum(jnp.transpose(dof * o_ref[...].astype(F32)), axis=0, keepdims=True)
        dq_part = None
        for sb in range(tk // sub):
            rows = slice(sb * sub, (sb + 1) * sub)
            kb = k_ref[rows, :]
            s_t = lax.dot_general(kb, qb, (((1,), (1,)), ((), ())), preferred_element_type=F32)
            p_t = jnp.exp2(s_t * c - lse_r)
            dp_t = lax.dot_general(v_ref[rows, :], dob, (((1,), (1,)), ((), ())), preferred_element_type=F32)
            ds_t = (p_t * (dp_t - dl_r) * scale).astype(MXU_DTYPE)
            dv_acc[rows, :] += jnp.dot(p_t.astype(MXU_DTYPE), dob, preferred_element_type=F32)
            dk_acc[rows, :] += jnp.dot(ds_t, qb, preferred_element_type=F32)
            part = lax.dot_general(kb, ds_t, (((0,), (0,)), ((), ())), preferred_element_type=F32)
            dq_part = part if dq_part is None else dq_part + part
        c0 = pl.multiple_of(qi * tq, tq)
        dq_acc[:, pl.ds(c0, tq)] += dq_part

        @pl.when(ki == nk - 1)
        def _():
            dq_ref[...] = jnp.transpose(dq_acc[:, pl.ds(c0, tq)])

        @pl.when(qi == nq - 1)
        def _():
            dk_ref[...] = dk_acc[...]
            dv_ref[...] = dv_acc[...]

    grid = (N_HEADS, nk, nq)
    body, s_in, s_out, s_shape, s_scr = _ride_along(body, 6, 3, 3, grid, side)
    res = pl.pallas_call(
        body, name="attn_bwd", grid=grid,
        in_specs=[pl.BlockSpec((tq, SLOT), lambda h, j, i: (i, h)),
                  pl.BlockSpec((tk, SLOT), lambda h, j, i: (j, h)),
                  pl.BlockSpec((tk, SLOT), lambda h, j, i: (j, h)),
                  pl.BlockSpec((tq, SLOT), lambda h, j, i: (i, h)),
                  pl.BlockSpec((tq, SLOT), lambda h, j, i: (i, h)),
                  pl.BlockSpec((None, 8, tq), lambda h, j, i: (h, 0, i))] + s_in,
        out_specs=[pl.BlockSpec((tq, SLOT), lambda h, j, i: (jnp.where(j == nk - 1, i, 0), h)),
                   pl.BlockSpec((tk, SLOT), lambda h, j, i: (j, h)),
                   pl.BlockSpec((tk, SLOT), lambda h, j, i: (j, h))] + s_out,
        out_shape=[_sds((nl, N_HEADS * SLOT), F32), _sds((n, N_HEADS * SLOT), F32), _sds((n, N_HEADS * SLOT), F32)] + s_shape,
        scratch_shapes=[pltpu.VMEM((SLOT, nl), F32), pltpu.VMEM((tk, SLOT), F32), pltpu.VMEM((tk, SLOT), F32)] + s_scr,
        compiler_params=_cparams(("arbitrary", "arbitrary", "arbitrary")),
    )(q, k, v, do, o, lse_t, *(side[0] if side else ()))
    return res[0], res[1], res[2], res[3:]


def _scan_consts(c_ref, lg):
    cs = slice(lg * SCAN_LANES, (lg + 1) * SCAN_LANES)
    return [c_ref[8 * kk:8 * kk + 8, cs] for kk in range(8)]


def _tile_scan(br, bi, consts, reverse):
    p1r, p1i, p2r, p2i, p4r, p4i = consts[:6]
    for pr, pi, kk in ((p1r, p1i, 1), (p2r, p2i, 2), (p4r, p4i, 4)):
        sh = (8 - kk) if reverse else kk
        sr_, si_ = pltpu.roll(br, sh, 0), pltpu.roll(bi, sh, 0)
        br, bi = br + pr * sr_ - pi * si_, bi + pr * si_ + pi * sr_
    return br, bi


def _seq_chunk(j, nch, nlc, reverse):
    return (nch - 1 - j) if reverse else (j + nlc) % nch


def _s5_scan(z, u_cb, bbd, cbd_n, lamc, t_rows, nl, reverse, name):
    n = z.shape[0]
    nch, nlc = n // t_rows, nl // t_rows
    ntile = t_rows // 8
    w = SCAN_LANES
    edge = 0 if reverse else 7
    ucb = u_cb * (SSM_WIDTH // CG_CHANNELS)

    def chunk(j):
        return _seq_chunk(j, nch, nlc, reverse)

    def body(u_ref, b_ref, cn_ref, c_ref, xs_ref, y_ref, carry):
        j = pl.program_id(1)

        @pl.when(j == 0)
        def _():
            carry[...] = jnp.zeros(carry.shape, F32)

        xs_ref[...] = jnp.dot(u_ref[...].astype(MXU_DTYPE), b_ref[...], preferred_element_type=F32)
        for lg in range(CG_STATES // w):
            re = slice(lg * w, (lg + 1) * w)
            im = slice(CG_STATES + lg * w, CG_STATES + (lg + 1) * w)
            consts = _scan_consts(c_ref, lg)
            qr, qi = consts[6], consts[7]

            def tile(tt, st):
                cr, ci = st
                t = (ntile - 1 - tt) if reverse else tt
                r0 = pl.multiple_of(t * 8, 8)
                br, bi = _tile_scan(xs_ref[pl.ds(r0, 8), re], xs_ref[pl.ds(r0, 8), im], consts, reverse)
                lr = jnp.broadcast_to(cr[edge:edge + 1, :], br.shape)
                li = jnp.broadcast_to(ci[edge:edge + 1, :], bi.shape)
                xr = br + qr * lr - qi * li
                xi = bi + qr * li + qi * lr
                xs_ref[pl.ds(r0, 8), re] = xr
                xs_ref[pl.ds(r0, 8), im] = xi
                return xr, xi

            cr, ci = lax.fori_loop(0, ntile, tile, (carry[:, re], carry[:, im]))
            carry[:, re] = cr
            carry[:, im] = ci
        y_ref[...] = jnp.dot(xs_ref[...].astype(MXU_DTYPE), cn_ref[...], preferred_element_type=F32)

    cw = 2 * CG_STATES
    return pl.pallas_call(
        body, name=name, grid=(N_CG, nch),
        in_specs=[pl.BlockSpec((t_rows, CG_CHANNELS), lambda g, j: (chunk(j), ucb + g)),
                  pl.BlockSpec((CG_CHANNELS, cw), lambda g, j: (g, 0)),
                  pl.BlockSpec((cw, CG_CHANNELS), lambda g, j: (g, 0)),
                  pl.BlockSpec((64, CG_STATES), lambda g, j: (0, g))],
        out_specs=[pl.BlockSpec((t_rows, cw), lambda g, j: (chunk(j), g)),
                   pl.BlockSpec((t_rows, CG_CHANNELS), lambda g, j: (chunk(j), g))],
        out_shape=[_sds((n, 2 * N_STATE), F32), _sds((n, SSM_WIDTH), F32)],
        scratch_shapes=[pltpu.VMEM((8, cw), F32)],
        compiler_params=_cparams(("arbitrary", "arbitrary")),
    )(z, bbd, cbd_n, lamc)


def _s5_bwd(dys, z, u_cb, xs, cbd_t, bbd_t, lamc_adj, t_rows, nl, reverse, name):
    n = z.shape[0]
    nch, nlc = n // t_rows, nl // t_rows
    ntile = t_rows // 8
    t8 = t_rows // 8
    w = SCAN_LANES
    cw = 2 * CG_STATES
    adj_rev = not reverse
    edge = 0 if adj_rev else 7

    def chunk(j):
        return _seq_chunk(nch - 1 - j, nch, nlc, reverse)

    def halo_blk(j):
        if reverse:
            return jnp.minimum((chunk(j) + 1) * t8, n // 8 - 1)
        return (_seq_chunk(jnp.maximum(nch - 2 - j, 0), nch, nlc, False) + 1) * t8 - 1

    def body(dy_ref, u_ref, xs_ref, halo_ref, ct_ref, bt_ref, c_ref, du_ref, db_ref, dc_ref, dl_ref, gbuf, carry):
        j = pl.program_id(1)
        start = j == nch - 1

        @pl.when(j == 0)
        def _():
            carry[...] = jnp.zeros(carry.shape, F32)
            db_ref[...] = jnp.zeros(db_ref.shape, F32)
            dc_ref[...] = jnp.zeros(dc_ref.shape, F32)
            dl_ref[...] = jnp.zeros(dl_ref.shape, F32)

        dy = jnp.where(chunk(j) < nlc, dy_ref[...], 0.0).astype(MXU_DTYPE)
        gbuf[...] = jnp.dot(dy, ct_ref[...], preferred_element_type=F32)
        dc_ref[...] += lax.dot_general(dy, xs_ref[...].astype(MXU_DTYPE), (((0,), (0,)), ((), ())),
                                       preferred_element_type=F32)
        for lg in range(CG_STATES // w):
            re = slice(lg * w, (lg + 1) * w)
            im = slice(CG_STATES + lg * w, CG_STATES + (lg + 1) * w)
            consts = _scan_consts(c_ref, lg)
            qr, qi = consts[6], consts[7]
            hr, hi = halo_ref[:, re], halo_ref[:, im]

            def tile(tt, st):
                gcr, gci, ar, ai = st
                t = (ntile - 1 - tt) if adj_rev else tt
                r0 = pl.multiple_of(t * 8, 8)
                br, bi = _tile_scan(gbuf[pl.ds(r0, 8), re], gbuf[pl.ds(r0, 8), im], consts, adj_rev)
                lr = jnp.broadcast_to(gcr[edge:edge + 1, :], br.shape)
                li = jnp.broadcast_to(gci[edge:edge + 1, :], bi.shape)
                gr = br + qr * lr - qi * li
                gi = bi + qr * li + qi * lr
                gbuf[pl.ds(r0, 8), re] = gr
                gbuf[pl.ds(r0, 8), im] = gi
                xr, xi = xs_ref[pl.ds(r0, 8), re], xs_ref[pl.ds(r0, 8), im]
                rid = lax.broadcasted_iota(jnp.int32, xr.shape, 0)
                if reverse:
                    last = t == ntile - 1
                    rn = pl.multiple_of(jnp.minimum(r0 + 8, t_rows - 8), 8)
                    nbr = jnp.where(last, hr, xs_ref[pl.ds(rn, 8), re])
                    nbi = jnp.where(last, hi, xs_ref[pl.ds(rn, 8), im])
                    nbr = jnp.where(last & start, 0.0, nbr)
                    nbi = jnp.where(last & start, 0.0, nbi)
                    xpr = jnp.where(rid == 7, jnp.broadcast_to(nbr[0:1, :], xr.shape), pltpu.roll(xr, 7, 0))
                    xpi = jnp.where(rid == 7, jnp.broadcast_to(nbi[0:1, :], xi.shape), pltpu.roll(xi, 7, 0))
                else:
                    first = t == 0
                    rn = pl.multiple_of(jnp.maximum(r0 - 8, 0), 8)
                    nbr = jnp.where(first, hr, xs_ref[pl.ds(rn, 8), re])
                    nbi = jnp.where(first, hi, xs_ref[pl.ds(rn, 8), im])
                    nbr = jnp.where(first & start, 0.0, nbr)
                    nbi = jnp.where(first & start, 0.0, nbi)
                    xpr = jnp.where(rid == 0, jnp.broadcast_to(nbr[7:8, :], xr.shape), pltpu.roll(xr, 1, 0))
                    xpi = jnp.where(rid == 0, jnp.broadcast_to(nbi[7:8, :], xi.shape), pltpu.roll(xi, 1, 0))
                ar = ar + gr * xpr + gi * xpi
                ai = ai - gr * xpi + gi * xpr
                return gr, gi, ar, ai

            zz = jnp.zeros((8, w), F32)
            gcr, gci, ar, ai = lax.fori_loop(0, ntile, tile, (carry[:, re], carry[:, im], zz, zz))
            carry[:, re] = gcr
            carry[:, im] = gci
            dl_ref[:, re] += ar
            dl_ref[:, im] += ai
        g = gbuf[...].astype(MXU_DTYPE)
        du_ref[...] = jnp.dot(g, bt_ref[...], preferred_element_type=F32)
        db_ref[...] += lax.dot_general(u_ref[...].astype(MXU_DTYPE), g, (((0,), (0,)), ((), ())),
                                       preferred_element_type=F32)

    ucb = u_cb * (SSM_WIDTH // CG_CHANNELS)
    return pl.pallas_call(
        body, name=name, grid=(N_CG, nch),
        in_specs=[pl.BlockSpec((t_rows, CG_CHANNELS), lambda g, j: (jnp.minimum(chunk(j), nlc - 1), g)),
                  pl.BlockSpec((t_rows, CG_CHANNELS), lambda g, j: (chunk(j), ucb + g)),
                  pl.BlockSpec((t_rows, cw), lambda g, j: (chunk(j), g)),
                  pl.BlockSpec((8, cw), lambda g, j: (halo_blk(j), g)),
                  pl.BlockSpec((CG_CHANNELS, cw), lambda g, j: (g, 0)),
                  pl.BlockSpec((cw, CG_CHANNELS), lambda g, j: (g, 0)),
                  pl.BlockSpec((64, CG_STATES), lambda g, j: (0, g))],
        out_specs=[pl.BlockSpec((t_rows, CG_CHANNELS), lambda g, j: (chunk(j), g)),
                   pl.BlockSpec((CG_CHANNELS, cw), lambda g, j: (g, 0)),
                   pl.BlockSpec((CG_CHANNELS, cw), lambda g, j: (g, 0)),
                   pl.BlockSpec((8, cw), lambda g, j: (0, g))],
        out_shape=[_sds((n, SSM_WIDTH), F32), _sds((SSM_WIDTH, cw), F32), _sds((SSM_WIDTH, cw), F32),
                   _sds((8, 2 * N_STATE), F32)],
        scratch_shapes=[pltpu.VMEM((t_rows, cw), F32), pltpu.VMEM((8, cw), F32)],
        compiler_params=_cparams(("arbitrary", "arbitrary")),
    )(dys, z, xs, xs, cbd_t, bbd_t, lamc_adj)


_CG_GROUPS = SSM_GROUPS // N_CG


def _group_mask():
    idx = jnp.arange(_CG_GROUPS)
    return (idx[:, None] == idx[None, :])[None, :, None, None, :, None]


def _diag_blocks(p_re, p_im):
    t = jnp.stack([p_re, p_im], axis=2).reshape(N_CG, _CG_GROUPS, SSM_GROUP, 2, 1, SSM_STATE)
    return jnp.where(_group_mask(), t, 0.0).reshape(SSM_WIDTH, 2 * CG_STATES)


def _diag_extract(d):
    d6 = d.reshape(N_CG, _CG_GROUPS, SSM_GROUP, 2, _CG_GROUPS, SSM_STATE)
    blk = jnp.sum(jnp.where(_group_mask(), d6, 0.0), axis=4)
    blk = blk.reshape(SSM_GROUPS, SSM_GROUP, 2, SSM_STATE)
    return blk[:, :, 0], blk[:, :, 1]


def _block_transpose(d):
    return jnp.transpose(d.reshape(N_CG, CG_CHANNELS, 2 * CG_STATES), (0, 2, 1)).reshape(2 * N_STATE, CG_CHANNELS)


def _s5_disc(lam_re, lam_im, log_dt, b_re, b_im):
    lam = lax.complex(lam_re, lam_im)
    dt = jnp.exp(log_dt)[:, None]
    lam_bar = jnp.exp(lam * dt)
    b_bar = ((lam_bar - 1.0) / lam)[..., None] * lax.complex(b_re, b_im)
    return jnp.real(lam_bar), jnp.imag(lam_bar), jnp.real(b_bar), jnp.imag(b_bar)


def _lam_consts(lr, li, mirrored, conj):
    lam = lax.complex(lr.reshape(-1), -li.reshape(-1) if conj else li.reshape(-1))
    p2 = lam * lam
    p4 = p2 * p2
    pw = [lam, p2, p2 * lam, p4, p4 * lam, p4 * p2, p4 * p2 * lam, p4 * p4]
    rows = jnp.arange(8)[:, None]
    out = []
    for kk in (1, 2, 4):
        mask = (rows <= 7 - kk) if mirrored else (rows >= kk)
        pk = jnp.where(mask, pw[kk - 1][None, :], 0.0)
        out += [jnp.real(pk), jnp.imag(pk)]
    q = jnp.stack(pw[::-1] if mirrored else pw)
    return jnp.concatenate(out + [jnp.real(q), jnp.imag(q)], axis=0)


def _dev(t):
    return (t // 4, (t // 2) % 2, t % 2)


def _my_index():
    return 4 * lax.axis_index("x") + 2 * lax.axis_index("y") + lax.axis_index("c")


def _comm_call(body, name, arrs, lead, n_remote):
    nw = len(arrs)
    any_spec = pl.BlockSpec(memory_space=pl.ANY)
    return pl.pallas_call(
        body, name=name, out_shape=[_sds((lead,) + a.shape[-2:], a.dtype) for a in arrs],
        in_specs=[any_spec] * nw, out_specs=[any_spec] * nw,
        scratch_shapes=[pltpu.SemaphoreType.DMA((n_remote * nw,)), pltpu.SemaphoreType.DMA((n_remote * nw,)),
                        pltpu.SemaphoreType.DMA((2 * nw,))] + [pltpu.VMEM(a.shape[-2:], a.dtype) for a in arrs],
        compiler_params=pltpu.CompilerParams(vmem_limit_bytes=VMEM_LIMIT),
    )(*arrs)


class _LocalCopy:
    def __init__(self, src, dst, buf, sem_in, sem_out):
        self.fetch = pltpu.make_async_copy(src, buf, sem_in)
        self.store = pltpu.make_async_copy(buf, dst, sem_out)
        self.fetch.start()

    def forward(self):
        self.fetch.wait()
        self.store.start()

    def finish(self):
        self.store.wait()


def _exchange8(gs, name, same):
    nw = len(gs)

    def body(*refs):
        g_refs, o_refs, (ssem, rsem, lsem), bufs = refs[:nw], refs[nw:2 * nw], refs[2 * nw:2 * nw + 3], refs[2 * nw + 3:]
        me = _my_index()
        locs, sends = [], []
        for i, (g_ref, o_ref) in enumerate(zip(g_refs, o_refs)):
            src = (lambda t, g_ref=g_ref: g_ref) if same else (lambda t, g_ref=g_ref: g_ref.at[t])
            locs.append(_LocalCopy(src(me), o_ref.at[me], bufs[i], lsem.at[2 * i], lsem.at[2 * i + 1]))
            for d in range(1, 8):
                t = (me + d) % 8
                cp = pltpu.make_async_remote_copy(src_ref=src(t), dst_ref=o_ref.at[me], send_sem=ssem.at[7 * i + d - 1],
                                                  recv_sem=rsem.at[7 * i + d - 1], device_id=_dev(t), device_id_type=MESH)
                cp.start()
                sends.append(cp)
        for loc in locs:
            loc.forward()
        for i, (g_ref, o_ref) in enumerate(zip(g_refs, o_refs)):
            src = (lambda t, g_ref=g_ref: g_ref) if same else (lambda t, g_ref=g_ref: g_ref.at[t])
            for d in range(1, 8):
                s = (me + 8 - d) % 8
                pltpu.make_async_remote_copy(src_ref=src(s), dst_ref=o_ref.at[s], send_sem=ssem.at[7 * i + d - 1],
                                             recv_sem=rsem.at[7 * i + d - 1], device_id=_dev(s),
                                             device_id_type=MESH).wait_recv()
        for cp in sends:
            cp.wait_send()
        for loc in locs:
            loc.finish()

    return _comm_call(body, name, gs, 8, 7)


def _chip_copies(w_refs, o_refs, ssem, rsem, lsem, bufs, gather):
    x, y, cc = lax.axis_index("x"), lax.axis_index("y"), lax.axis_index("c")
    k = 2 * x + y
    peers = [(1 - x, y), (x, 1 - y), (1 - x, 1 - y)]
    fetch, store, sends, recvs = [], [], [], []
    for i, (w_ref, o_ref) in enumerate(zip(w_refs, o_refs)):
        if gather:
            fetch.append(pltpu.make_async_copy(w_ref.at[cc], bufs[i], lsem.at[2 * i]))
            store.append(pltpu.make_async_copy(bufs[i], o_ref.at[k], lsem.at[2 * i + 1]))
        for j, (px, py) in enumerate(peers):
            sems = dict(send_sem=ssem.at[3 * i + j], recv_sem=rsem.at[3 * i + j], device_id=(px, py, cc), device_id_type=MESH)
            src, dst = (w_ref.at[cc], o_ref.at[k]) if gather else (w_ref.at[2 * px + py], o_ref.at[j])
            sends.append(pltpu.make_async_remote_copy(src_ref=src, dst_ref=dst, **sems))
            src, dst = (w_ref.at[cc], o_ref.at[2 * px + py]) if gather else (w_ref.at[k], o_ref.at[j])
            recvs.append(pltpu.make_async_remote_copy(src_ref=src, dst_ref=dst, **sems))
    return fetch, store, sends, recvs


def _chips_start(*args):
    fetch, _, sends, _ = _chip_copies(*args)
    for cp in fetch + sends:
        cp.start()


def _chips_finish(*args):
    fetch, store, sends, recvs = _chip_copies(*args)
    for cp in fetch:
        cp.wait()
    for cp in store:
        cp.start()
    for cp in recvs:
        cp.wait_recv()
    for cp in sends:
        cp.wait_send()
    for cp in store:
        cp.wait()


def _chips_scratch(ws, gather):
    nw = len(ws)
    return ([pltpu.SemaphoreType.DMA((3 * nw,)), pltpu.SemaphoreType.DMA((3 * nw,)), pltpu.SemaphoreType.DMA((2 * nw,))]
            + ([pltpu.VMEM(a.shape[-2:], a.dtype) for a in ws] if gather else []))


def _ride_along(core, n_in, n_out, n_scr, grid, side):
    if side is None:
        return core, [], [], [], []
    arrs, gather = side
    ns = len(arrs)

    def body(*refs):
        a, b, c_ = n_in + ns, n_in + ns + n_out, n_in + 2 * ns + n_out
        s_scr = refs[c_ + n_scr:]
        sargs = (refs[n_in:a], refs[b:c_], *s_scr[:3], s_scr[3:], gather)
        ids = [pl.program_id(ax) for ax in range(len(grid))]
        first, last = ids[0] == 0, ids[0] == grid[0] - 1
        for i_, g_ in zip(ids[1:], grid[1:]):
            first, last = first & (i_ == 0), last & (i_ == g_ - 1)

        @pl.when(first)
        def _():
            _chips_start(*sargs)

        core(*refs[:n_in], *refs[a:b], *refs[c_:c_ + n_scr])

        @pl.when(last)
        def _():
            _chips_finish(*sargs)

    any_spec = pl.BlockSpec(memory_space=pl.ANY)
    shapes = [_sds((4 if gather else 3,) + a_.shape[-2:], a_.dtype) for a_ in arrs]
    return body, [any_spec] * ns, [any_spec] * ns, shapes, _chips_scratch(arrs, gather)


def _sibling_send(hs, name):
    nw = len(hs)

    def body(*refs):
        h_refs, o_refs, (ssem, rsem, lsem) = refs[:nw], refs[nw:2 * nw], refs[2 * nw:]
        x, y, cc = lax.axis_index("x"), lax.axis_index("y"), lax.axis_index("c")
        sends = []
        for i, (h_ref, o_ref) in enumerate(zip(h_refs, o_refs)):
            cp = pltpu.make_async_remote_copy(src_ref=h_ref.at[1 - cc], dst_ref=o_ref, send_sem=ssem.at[i],
                                              recv_sem=rsem.at[i], device_id=(x, y, 1 - cc), device_id_type=MESH)
            cp.start()
            sends.append(cp)
        for i, (h_ref, o_ref) in enumerate(zip(h_refs, o_refs)):
            pltpu.make_async_remote_copy(src_ref=h_ref.at[cc], dst_ref=o_ref, send_sem=ssem.at[i], recv_sem=rsem.at[i],
                                         device_id=(x, y, 1 - cc), device_id_type=MESH).wait_recv()
        for cp in sends:
            cp.wait_send()

    nw_spec = pl.BlockSpec(memory_space=pl.ANY)
    return pl.pallas_call(
        body, name=name, out_shape=[_sds(h.shape[1:], h.dtype) for h in hs],
        in_specs=[nw_spec] * nw, out_specs=[nw_spec] * nw,
        scratch_shapes=[pltpu.SemaphoreType.DMA((nw,)), pltpu.SemaphoreType.DMA((nw,)), pltpu.SemaphoreType.DMA((nw,))],
    )(*hs)


def _sibling_exchange(hs, name):
    nw = len(hs)

    def body(*refs):
        h_refs, o_refs, (ssem, rsem, lsem), bufs = refs[:nw], refs[nw:2 * nw], refs[2 * nw:2 * nw + 3], refs[2 * nw + 3:]
        x, y, cc = lax.axis_index("x"), lax.axis_index("y"), lax.axis_index("c")
        locs, sends = [], []
        for i, (h_ref, o_ref) in enumerate(zip(h_refs, o_refs)):
            locs.append(_LocalCopy(h_ref, o_ref.at[cc], bufs[i], lsem.at[2 * i], lsem.at[2 * i + 1]))
            cp = pltpu.make_async_remote_copy(src_ref=h_ref, dst_ref=o_ref.at[cc], send_sem=ssem.at[i], recv_sem=rsem.at[i],
                                              device_id=(x, y, 1 - cc), device_id_type=MESH)
            cp.start()
            sends.append(cp)
        for loc in locs:
            loc.forward()
        for i, (h_ref, o_ref) in enumerate(zip(h_refs, o_refs)):
            pltpu.make_async_remote_copy(src_ref=h_ref, dst_ref=o_ref.at[1 - cc], send_sem=ssem.at[i], recv_sem=rsem.at[i],
                                         device_id=(x, y, 1 - cc), device_id_type=MESH).wait_recv()
        for cp in sends:
            cp.wait_send()
        for loc in locs:
            loc.finish()

    return _comm_call(body, name, hs, 2, 1)


def _sum8(buf, name):
    _, r, c = buf.shape
    tr = _pick(r, (256, 128, 64, 32, 16, 8))
    flat = buf.reshape(8 * r, c)

    def body(*v):
        acc = v[0]
        for t in v[1:]:
            acc = acc + t
        return acc

    return _rowwise(body, name=name, nblk=r // tr, tr=tr, rows=[(flat, 0, c, s * r) for s in range(8)],
                    outs=[(c, F32)])[0]


def _pack(arrs, rows_mult=16):
    flat = jnp.concatenate([a.reshape(-1).astype(F32) for a in arrs])
    nel = flat.shape[0]
    r = -(-nel // PACK_W)
    r = -(-r // rows_mult) * rows_mult
    return jnp.pad(flat, (0, r * PACK_W - nel)).reshape(r, PACK_W)


def _unpack(buf, shapes):
    flat = buf.reshape(-1)
    out, o = [], 0
    for s in shapes:
        nel = int(np.prod(s))
        out.append(flat[o:o + nel].reshape(s))
        o += nel
    return out


def _adamw(g, w, m, v, name):
    r, wd = g.shape
    tr = _pick(r, tuple(t for t in (256, 128, 64, 32, 16, 8) if t * wd <= 262144) or (8,))
    c1 = 1.0 / (1.0 - ADAM_B1 ** ADAM_STEP)
    c2 = 1.0 / (1.0 - ADAM_B2 ** ADAM_STEP)

    def body(gv, wv, mv, vv):
        mn = ADAM_B1 * mv + (1.0 - ADAM_B1) * gv
        vn = ADAM_B2 * vv + (1.0 - ADAM_B2) * (gv * gv)
        delta = -ADAM_LR * ((mn * c1) / (jnp.sqrt(vn * c2) + ADAM_EPS) + ADAM_WD * wv)
        return delta, mn, vn

    return _rowwise(body, name=name, nblk=r // tr, tr=tr, rows=[(a, 0, wd, 0) for a in (g, w, m, v)],
                    outs=[(wd, F32)] * 3)


def kernel(x, c, ctx, c_ctx, w_mod, b_mod, norm1_g, norm2_g, w_in, q_a_g, w_uq, kv_a_g, w_ukv, q_norm_g, k_norm_g, w_o_attn, lam_re_f, lam_im_f, log_dt_f, c_re_f, c_im_f, lam_re_b, lam_im_b, log_dt_b, c_re_b, c_im_b, b_re, b_im, d_skip, w_glu, w_out, w_up, conv_w, conv_b, w_down, loss_target, m_c_ctx, m_w_mod, m_b_mod, m_norm1_g, m_norm2_g, m_w_in, m_q_a_g, m_w_uq, m_kv_a_g, m_w_ukv, m_q_norm_g, m_k_norm_g, m_w_o_attn, m_lam_re_f, m_lam_im_f, m_log_dt_f, m_c_re_f, m_c_im_f, m_lam_re_b, m_lam_im_b, m_log_dt_b, m_c_re_b, m_c_im_b, m_b_re, m_b_im, m_d_skip, m_w_glu, m_w_out, m_w_up, m_conv_w, m_conv_b, m_w_down, v_c_ctx, v_w_mod, v_b_mod, v_norm1_g, v_norm2_g, v_w_in, v_q_a_g, v_w_uq, v_kv_a_g, v_w_ukv, v_q_norm_g, v_k_norm_g, v_w_o_attn, v_lam_re_f, v_lam_im_f, v_log_dt_f, v_c_re_f, v_c_im_f, v_lam_re_b, v_lam_im_b, v_log_dt_b, v_c_re_b, v_c_im_b, v_b_re, v_b_im, v_d_skip, v_w_glu, v_w_out, v_w_up, v_conv_w, v_conv_b, v_w_down):
    weights = dict(c_ctx=c_ctx, w_mod=w_mod, b_mod=b_mod, norm1_g=norm1_g, norm2_g=norm2_g, w_in=w_in, q_a_g=q_a_g, w_uq=w_uq, kv_a_g=kv_a_g, w_ukv=w_ukv, q_norm_g=q_norm_g, k_norm_g=k_norm_g, w_o_attn=w_o_attn, lam_re_f=lam_re_f, lam_im_f=lam_im_f, log_dt_f=log_dt_f, c_re_f=c_re_f, c_im_f=c_im_f, lam_re_b=lam_re_b, lam_im_b=lam_im_b, log_dt_b=log_dt_b, c_re_b=c_re_b, c_im_b=c_im_b, b_re=b_re, b_im=b_im, d_skip=d_skip, w_glu=w_glu, w_out=w_out, w_up=w_up, conv_w=conv_w, conv_b=conv_b, w_down=w_down)
    mom_m = dict(c_ctx=m_c_ctx, w_mod=m_w_mod, b_mod=m_b_mod, norm1_g=m_norm1_g, norm2_g=m_norm2_g, w_in=m_w_in, q_a_g=m_q_a_g, w_uq=m_w_uq, kv_a_g=m_kv_a_g, w_ukv=m_w_ukv, q_norm_g=m_q_norm_g, k_norm_g=m_k_norm_g, w_o_attn=m_w_o_attn, lam_re_f=m_lam_re_f, lam_im_f=m_lam_im_f, log_dt_f=m_log_dt_f, c_re_f=m_c_re_f, c_im_f=m_c_im_f, lam_re_b=m_lam_re_b, lam_im_b=m_lam_im_b, log_dt_b=m_log_dt_b, c_re_b=m_c_re_b, c_im_b=m_c_im_b, b_re=m_b_re, b_im=m_b_im, d_skip=m_d_skip, w_glu=m_w_glu, w_out=m_w_out, w_up=m_w_up, conv_w=m_conv_w, conv_b=m_conv_b, w_down=m_w_down)
    mom_v = dict(c_ctx=v_c_ctx, w_mod=v_w_mod, b_mod=v_b_mod, norm1_g=v_norm1_g, norm2_g=v_norm2_g, w_in=v_w_in, q_a_g=v_q_a_g, w_uq=v_w_uq, kv_a_g=v_kv_a_g, w_ukv=v_w_ukv, q_norm_g=v_q_norm_g, k_norm_g=v_k_norm_g, w_o_attn=v_w_o_attn, lam_re_f=v_lam_re_f, lam_im_f=v_lam_im_f, log_dt_f=v_log_dt_f, c_re_f=v_c_re_f, c_im_f=v_c_im_f, lam_re_b=v_lam_re_b, lam_im_b=v_lam_im_b, log_dt_b=v_log_dt_b, c_re_b=v_c_re_b, c_im_b=v_c_im_b, b_re=v_b_re, b_im=v_b_im, d_skip=v_d_skip, w_glu=v_w_glu, w_out=v_w_out, w_up=v_w_up, conv_w=v_conv_w, conv_b=v_conv_b, w_down=v_w_down)
    names = list(weights)

    nl, d = x.shape[1], x.shape[2]
    nc = ctx.shape[1]
    n = nl + nc
    f2 = conv_b.shape[1]
    fh = f2 // 2
    d6 = b_mod.shape[1]
    mx, my, mc = lax.axis_index("x"), lax.axis_index("y"), lax.axis_index("c")
    chip = 2 * mx + my
    me = 4 * mx + 2 * my + mc
    tr = _pick(math.gcd(nl, nc), (256, 128, 64, 32, 16))
    nlb, nb = nl // tr, n // tr

    big_names = ["w_in", "w_uq", "w_ukv", "w_o_attn", "w_glu", "w_out", "w_up", "w_down"]
    row_sharded = ("w_out", "w_down")
    ffn_names = ["w_o_attn", "w_glu", "w_out", "w_up", "w_down"]
    early_names = [k for k in big_names if k not in ffn_names]
    full = {}

    def halves_in(names_):
        return [weights[k][0].astype(MXU_DTYPE).reshape(2, weights[k].shape[1] // 2, weights[k].shape[2]) for k in names_]

    def assemble(names_, my_halves, name):
        gathered = _sibling_exchange([t.reshape(-1, t.shape[2]) for t in my_halves], name)
        for k_, gth in zip(names_, gathered):
            r_, c_ = weights[k_].shape[1:]
            g4 = gth.reshape(2, 4, r_ // 2, c_)
            full[k_] = (jnp.transpose(g4, (1, 0, 2, 3)).reshape(4 * r_, c_) if k_ in row_sharded
                        else jnp.transpose(g4, (0, 2, 1, 3)).reshape(r_, 4 * c_))


    cwid = conv_w.shape[2]
    sw = -(-max(d, cwid) // 128) * 128
    small_in = jnp.concatenate([jnp.pad(c, ((0, 0), (0, sw - d))), jnp.pad(conv_w[0], ((0, 4), (0, sw - cwid)))], axis=0)
    (small_all,) = _exchange8([small_in], "gather_c", True)
    cs = small_all[:, 0, :d]
    conv_w_full = jnp.concatenate([small_all[2 * j, 1:4, :cwid] for j in range(4)], axis=1)
    cs16 = jnp.concatenate([cs, c_ctx[None, :], jnp.zeros((7, d), F32)], axis=0)

    csh = w_mod.shape[2]
    b_mod_sh = lax.dynamic_slice(b_mod, (0, chip * csh), (1, csh))

    def mod_fwd_body(c_ref, w_ref, b_ref, o_ref):
        a = _silu(c_ref[...]).astype(MXU_DTYPE)
        o_ref[...] = jnp.dot(a, w_ref[...].astype(MXU_DTYPE), preferred_element_type=F32) + b_ref[...]

    mod_sh = pl.pallas_call(mod_fwd_body, name="mod_fwd", out_shape=_sds((16, csh), F32),
                            compiler_params=pltpu.CompilerParams(vmem_limit_bytes=VMEM_LIMIT))(cs16, w_mod[0], b_mod_sh)
    (mod_all,) = _exchange8([mod_sh], "gather_mod", True)
    mod_full = jnp.concatenate([mod_all[2 * j] for j in range(4)], axis=1)
    modv = jnp.stack([lax.dynamic_slice(mod_full, (me, 0), (1, d6)), mod_full[8:9]])

    xa = jnp.concatenate([x[0], ctx[0]], axis=0)
    n1g, n2g = norm1_g, norm2_g

    def norm1_body(xv, m, g):
        sh1, sc1 = m[:, :d], m[:, d:2 * d]
        return _rms_fwd(xv, g, d) * (1.0 + sc1) + sh1

    (h1,), early_halves = _rowwise(norm1_body, name="norm1_fwd", nblk=nb, tr=tr, rows=[(xa, 0, d, 0)], sels=[modv],
                                   fulls=[n1g], outs=[(d, MXU_DTYPE)], seg=nlb, side=(halves_in(early_names), True))
    assemble(early_names, early_halves, "gather_weight_halves")

    u_off, kv_off, kr_off = 2 * d, 2 * d + SSM_WIDTH, 2 * d + SSM_WIDTH + KV_LORA
    q_off = -(-(kr_off + SLOT) // Q_LORA) * Q_LORA
    zw = q_off + Q_LORA
    wi = full["w_in"]
    s0, s1, s2, s3 = Q_LORA, Q_LORA + KV_LORA, Q_LORA + KV_LORA + QK_ROPE, Q_LORA + KV_LORA + QK_ROPE + SSM_WIDTH
    zpad = lambda w_: jnp.zeros((d, w_), MXU_DTYPE)
    win_p = jnp.concatenate([wi[:, s3:], wi[:, s2:s3], wi[:, s0:s1], wi[:, s1:s2], zpad(SLOT - QK_ROPE),
                             zpad(q_off - kr_off - SLOT), wi[:, :s0]], axis=1)
    wuq_p = jnp.pad(full["w_uq"].reshape(Q_LORA, N_HEADS, QK_DIM), ((0, 0), (0, 0), (0, SLOT - QK_DIM))).reshape(Q_LORA, N_HEADS * SLOT)
    wukv3 = full["w_ukv"].reshape(KV_LORA, N_HEADS, QK_NOPE + V_DIM)
    padh = lambda t: jnp.pad(t, ((0, 0), (0, 0), (0, SLOT - t.shape[2]))).reshape(t.shape[0], N_HEADS * SLOT)
    wukv_p = jnp.concatenate([padh(wukv3[:, :, :QK_NOPE]), padh(wukv3[:, :, QK_NOPE:])], axis=1)
    hw = N_HEADS * SLOT
    gain_p = lambda g_: jnp.tile(jnp.pad(g_[0], (0, SLOT - QK_DIM)), N_HEADS)[None, :]
    qg_p, kg_p = gain_p(q_norm_g), gain_p(k_norm_g)

    tok = jnp.arange(nl)
    freqs = ROPE_THETA ** (-jnp.arange(QK_ROPE // 4, dtype=F32) / (QK_ROPE // 4))
    ang = jnp.concatenate([(tok // GRID_W)[:, None] * freqs, (tok % GRID_W)[:, None] * freqs], axis=-1)
    cos_t = jnp.concatenate([jnp.cos(ang), jnp.ones((nc, 16), F32)], axis=0)
    sin_t = jnp.concatenate([jnp.sin(ang), jnp.zeros((nc, 16), F32)], axis=0)
    zl = lambda w_: jnp.zeros((n, w_), F32)
    rope_c = jnp.concatenate([jnp.ones((n, QK_NOPE), F32), cos_t, cos_t, zl(SLOT - QK_DIM)], axis=1)
    rope_sa = jnp.concatenate([zl(QK_NOPE), -sin_t, zl(SLOT - QK_NOPE - 16)], axis=1)
    rope_sb = jnp.concatenate([zl(QK_NOPE + 16), sin_t, zl(SLOT - QK_DIM)], axis=1)

    dirs = (("f", lam_re_f, lam_im_f, log_dt_f, c_re_f, c_im_f, False), ("b", lam_re_b, lam_im_b, log_dt_b, c_re_b, c_im_b, True))
    bbd, cbd_t, cbd_n, bbd_t, lamc, lamc_adj, disc_vjps = [], [], [], [], [], [], []
    for _, l_re, l_im, l_dt, cr_, ci_, rev_ in dirs:
        (lbr, lbi, bbr, bbi), vjp = jax.vjp(_s5_disc, l_re[0], l_im[0], l_dt[0], b_re[0], b_im[0])
        disc_vjps.append(vjp)
        bb = _diag_blocks(jnp.transpose(bbr, (0, 2, 1)), jnp.transpose(bbi, (0, 2, 1))).astype(MXU_DTYPE)
        cc_ = _diag_blocks(cr_[0], -ci_[0]).astype(MXU_DTYPE)
        bbd.append(bb)
        bbd_t.append(_block_transpose(bb))
        cbd_t.append(cc_)
        cbd_n.append(_block_transpose(cc_))
        lamc.append(_lam_consts(lbr, lbi, rev_, False))
        lamc_adj.append(_lam_consts(lbr, lbi, not rev_, True))
    t_scan = tr

    z = _mm(h1, win_p, "nn", "in_proj")
    gl_cb, u_cb, kv_cb, kr_cb, q_cb = 0, u_off // SSM_WIDTH, kv_off // KV_LORA, kr_off // SLOT, q_off // Q_LORA

    (cqn,) = _rowwise(lambda v, g: _rms_fwd(v, g, Q_LORA), name="qa_norm_fwd", nblk=nlb, tr=tr,
                      rows=[(z, q_cb, Q_LORA, 0)], fulls=[q_a_g], outs=[(Q_LORA, MXU_DTYPE)])
    qh = _mm(cqn, wuq_p, "nn", "q_up")

    def qhead_body(qv, cv, sav, sbv, g):
        return jnp.concatenate([_rope_fwd(_rms_fwd(t, g[:, :SLOT], QK_DIM), cv, sav, sbv) for t in _heads(qv)], axis=1)

    rope_rows = lambda: [(rope_c, 0, SLOT, 0), (rope_sa, 0, SLOT, 0), (rope_sb, 0, SLOT, 0)]
    (q_p,) = _rowwise(qhead_body, name="q_head_fwd", nblk=nlb, tr=tr, rows=[(qh, 0, hw, 0)] + rope_rows(),
                      fulls=[qg_p], outs=[(hw, MXU_DTYPE)])

    (ckvn,) = _rowwise(lambda v, g: _rms_fwd(v, g, KV_LORA), name="kva_norm_fwd", nblk=nb, tr=tr,
                       rows=[(z, kv_cb, KV_LORA, 0)], fulls=[kv_a_g], outs=[(KV_LORA, MXU_DTYPE)])
    kvpre = _mm(ckvn, wukv_p, "nn", "kv_up")

    def khead_body(kv_, vv_, krv, cv, sav, sbv, g):
        kpe = pltpu.roll(krv, QK_NOPE, 1)
        ks = [_rope_fwd(_rms_fwd(t + kpe, g[:, :SLOT], QK_DIM), cv, sav, sbv) for t in _heads(kv_)]
        return jnp.concatenate(ks, axis=1), vv_

    k_p, v_p = _rowwise(khead_body, name="k_head_fwd", nblk=nb, tr=tr,
                        rows=[(kvpre, 0, hw, 0), (kvpre, 1, hw, 0), (z, kr_cb, SLOT, 0)] + rope_rows(),
                        fulls=[kg_p], outs=[(hw, MXU_DTYPE), (hw, MXU_DTYPE)])

    scale = QK_DIM ** -0.5
    o_p, lse, ffn_halves = _attn_fwd(q_p, k_p, v_p, nl, scale, side=(halves_in(ffn_names), True))
    assemble(ffn_names, ffn_halves, "gather_ffn_weight_halves")
    wglu, wout, wup, wdown = full["w_glu"], full["w_out"], full["w_up"], full["w_down"]
    wo_p = jnp.pad(full["w_o_attn"].reshape(N_HEADS, V_DIM, d), ((0, 0), (0, SLOT - V_DIM), (0, 0))).reshape(N_HEADS * SLOT, d)
    a_l = _mm(o_p, wo_p, "nn", "attn_out")

    scans = [_s5_scan(z, u_cb, bbd[j], cbd_n[j], lamc[j], t_scan, nl, dirs[j][6], "s5_scan_" + dirs[j][0]) for j in range(2)]
    xs, ydir = [s_[0] for s_ in scans], [s_[1] for s_ in scans]

    def ssm_out_body(uv, a, b, dsk):
        ys = uv * dsk + a + b
        return ys, _gelu(ys)

    ys, ge = _rowwise(ssm_out_body, name="s5_out_fwd", nblk=nlb, tr=tr,
                      rows=[(z, u_cb, SSM_WIDTH, 0), (ydir[0], 0, SSM_WIDTH, 0), (ydir[1], 0, SSM_WIDTH, 0)],
                      fulls=[d_skip], outs=[(SSM_WIDTH, F32), (SSM_WIDTH, MXU_DTYPE)])
    glu_out = _mm(ge, wglu, "nn", "glu_proj")

    def merge_body(ga, gs, av, val, gate):
        return _sigmoid(ga) * av + _sigmoid(gs) * (val * _sigmoid(gate))

    merge_rows = lambda: [(z, 0, d, 0), (z, 1, d, 0), (a_l, 0, d, 0), (glu_out, 0, d, 0), (glu_out, 1, d, 0)]
    (merged,) = _rowwise(merge_body, name="merge_fwd", nblk=nlb, tr=tr, rows=merge_rows(), outs=[(d, MXU_DTYPE)])
    mo = _mm(merged, wout, "nn", "out_proj")
    mod_x = modv[0]

    def norm2_body(xv, mov, m, g):
        g1, sh2, sc2 = m[:, 2 * d:3 * d], m[:, 3 * d:4 * d], m[:, 4 * d:5 * d]
        x1v = xv + g1 * mov
        return x1v, _rms_fwd(x1v, g, d) * (1.0 + sc2) + sh2

    x1, h2 = _rowwise(norm2_body, name="norm2_fwd", nblk=nlb, tr=tr, rows=[(xa, 0, d, 0), (mo, 0, d, 0)],
                      fulls=[mod_x, n2g], outs=[(d, F32), (d, MXU_DTYPE)])
    up = _mm(h2, wup, "nn", "ffn_up")
    cw8 = jnp.zeros((8, f2), F32).at[:3].set(conv_w_full)

    def conv3(t3, w8, off):
        p_, c_, n_ = t3
        return p_ * w8[0:1, off:off + fh] + c_ * w8[1:2, off:off + fh] + n_ * w8[2:3, off:off + fh]

    def conv_fwd_body(val3, gate3, w8, bias):
        val2 = conv3(val3, w8, 0) + bias[:, :fh]
        gate2 = conv3(gate3, w8, fh) + bias[:, fh:]
        return _silu(gate2) * val2

    (act,) = _rowwise(conv_fwd_body, name="conv_fwd", nblk=nlb, tr=tr, rows=[(up, 0, fh, 0), (up, 1, fh, 0)],
                      halo=(0, 1), fulls=[cw8, conv_b], outs=[(fh, MXU_DTYPE)])
    dn = _mm(act, wdown, "nn", "ffn_down")
    tgt = loss_target[0]

    def loss_body(x1v, dnv, tv, m):
        g2 = m[:, 5 * d:6 * d]
        e = x1v + g2 * dnv - tv
        dx2v = e * (1.0 / d)
        return dx2v, dx2v * g2, e * e, dx2v * dnv

    dx2, ddn, loss_acc, dg2_acc = _rowwise(loss_body, name="loss", nblk=nlb, tr=tr,
                                           rows=[(x1, 0, d, 0), (dn, 0, d, 0), (tgt, 0, d, 0)], fulls=[mod_x],
                                           outs=[(d, F32), (d, MXU_DTYPE)], accs=[d, d])
    loss = lax.psum(0.5 / d * jnp.sum(loss_acc), ("x", "y", "c"))

    g_big = {}

    def chip_partials(names_, name):
        pcs = []
        for k_ in names_:
            r_, c_ = weights[k_].shape[1:]
            if k_ in row_sharded:
                p4 = jnp.transpose(g_big[k_].reshape(4, 2, r_ // 2, c_), (1, 0, 2, 3))
            else:
                p4 = jnp.transpose(g_big[k_].reshape(2, r_ // 2, 4, c_), (0, 2, 1, 3))
            pcs.append(p4.reshape(2, 2 * r_, c_))
        out = []
        for k_, p_, got in zip(names_, pcs, _sibling_send(pcs, name)):
            rows4, c_ = got.shape
            own = lax.dynamic_index_in_dim(p_, mc, 0, keepdims=False)
            tr_ = _pick(rows4, (256, 128, 64, 32, 16))
            s32, sb = _rowwise(lambda a, b: (a + b, a + b), name="sum_chip_" + k_, nblk=rows4 // tr_, tr=tr_,
                               rows=[(own, 0, c_, 0), (got, 0, c_, 0)], outs=[(c_, F32), (c_, MXU_DTYPE)])
            out.append((s32, sb, rows4 // 4, c_))
        return out

    def scatter_in(partials):
        return [sb.reshape(4, rh, c_) for _, sb, rh, c_ in partials]

    def chip_sums(names_, partials, recv3):
        out = []
        for k_, (s32, _, rh, c_), r3 in zip(names_, partials, recv3):
            mine = lax.dynamic_slice(s32, (chip * rh, 0), (rh, c_))
            tr_ = _pick(rh, (256, 128, 64, 32, 16))
            (red,) = _rowwise(lambda a, b0, b1, b2: a + b0 + b1 + b2, name="sum_grad_" + k_, nblk=rh // tr_, tr=tr_,
                              rows=[(mine, 0, c_, 0)] + [(r3.reshape(3 * rh, c_), 0, c_, j * rh) for j in range(3)],
                              outs=[(c_, F32)])
            out.append(red)
        return out

    dact = _mm(ddn, wdown, "nt", "ffn_down_dx")
    g_big["w_down"] = _mm(act, ddn, "tn", "ffn_down_dw")

    def conv_bwd_body(val3, gate3, da, w8, bias):
        val2 = conv3(val3, w8, 0) + bias[:, :fh]
        gate2 = conv3(gate3, w8, fh) + bias[:, fh:]
        sg = _sigmoid(gate2)
        dval2 = da * (gate2 * sg)
        dgate2 = da * val2 * (sg * (1.0 + gate2 * (1.0 - sg)))
        du2 = jnp.concatenate([dval2, dgate2], axis=1)
        taps = [jnp.concatenate([dval2 * val3[j], dgate2 * gate3[j]], axis=1) for j in range(3)]
        return du2, du2, taps[0], taps[1], taps[2]

    du2, dcb_acc, dcw0, dcw1, dcw2 = _rowwise(conv_bwd_body, name="conv_bwd", nblk=nlb, tr=tr,
                                              rows=[(up, 0, fh, 0), (up, 1, fh, 0), (dact, 0, fh, 0)], halo=(0, 1),
                                              fulls=[cw8, conv_b], outs=[(f2, F32)], accs=[f2, f2, f2, f2])

    def conv_t_body(dval3, dgate3, w8):
        rev = lambda t3: (t3[2], t3[1], t3[0])
        return jnp.concatenate([conv3(rev(dval3), w8, 0), conv3(rev(dgate3), w8, fh)], axis=1)

    (dup,) = _rowwise(conv_t_body, name="conv_bwd_dx", nblk=nlb, tr=tr, rows=[(du2, 0, fh, 0), (du2, 1, fh, 0)],
                      halo=(0, 1), fulls=[cw8], outs=[(f2, MXU_DTYPE)])
    dh2 = _mm(dup, wup, "nt", "ffn_up_dx")
    g_big["w_up"] = _mm(h2, dup, "tn", "ffn_up_dw")

    def norm2_bwd_body(x1v, dh, dx2v, mov, m, g):
        g1, sc2 = m[:, 2 * d:3 * d], m[:, 4 * d:5 * d]
        y = _rms_fwd(x1v, g, d)
        dxn, dgc = _rms_bwd(x1v, g, dh * (1.0 + sc2), d)
        dx1v = dx2v + dxn
        return dx1v, dx1v * g1, dgc, dh, dh * y, dx1v * mov

    dx1, dmo, dn2g_acc, dsh2_acc, dsc2_acc, dg1_acc = _rowwise(
        norm2_bwd_body, name="norm2_bwd", nblk=nlb, tr=tr,
        rows=[(x1, 0, d, 0), (dh2, 0, d, 0), (dx2, 0, d, 0), (mo, 0, d, 0)], fulls=[mod_x, n2g],
        outs=[(d, F32), (d, MXU_DTYPE)], accs=[d, d, d, d])
    dmerged = _mm(dmo, wout, "nt", "out_proj_dx")
    g_big["w_out"] = _mm(merged, dmo, "tn", "out_proj_dw")

    def merge_bwd_body(ga, gs, av, val, gate, dm):
        sa_, ss_, sg_ = _sigmoid(ga), _sigmoid(gs), _sigmoid(gate)
        s_l = val * sg_
        ds_l = dm * ss_
        dga = dm * av * sa_ * (1.0 - sa_)
        dgs = dm * s_l * ss_ * (1.0 - ss_)
        dval = ds_l * sg_
        dgate = ds_l * val * sg_ * (1.0 - sg_)
        return dm * sa_, jnp.concatenate([dval, dgate], axis=1), jnp.concatenate([dga, dgs], axis=1)

    da_l, dglu, dgl = _rowwise(merge_bwd_body, name="merge_bwd", nblk=nlb, tr=tr,
                               rows=merge_rows() + [(dmerged, 0, d, 0)],
                               outs=[(d, MXU_DTYPE), (2 * d, MXU_DTYPE), (2 * d, MXU_DTYPE)])
    dge = _mm(dglu, wglu, "nt", "glu_proj_dx")
    g_big["w_glu"] = _mm(ge, dglu, "tn", "glu_proj_dw")

    def ssm_out_bwd_body(ysv, dgev, uv, dsk):
        dys_ = dgev * _dgelu(ysv)
        return dys_, dys_ * dsk, dys_ * uv

    dys, du_skip, ddskip_acc = _rowwise(ssm_out_bwd_body, name="s5_out_bwd", nblk=nlb, tr=tr,
                                        rows=[(ys, 0, SSM_WIDTH, 0), (dge, 0, SSM_WIDTH, 0), (z, u_cb, SSM_WIDTH, 0)],
                                        fulls=[d_skip], outs=[(SSM_WIDTH, F32), (SSM_WIDTH, F32)], accs=[SSM_WIDTH])
    s5b = [_s5_bwd(dys, z, u_cb, xs[j], cbd_t[j], bbd_t[j], lamc_adj[j], t_scan, nl, dirs[j][6], "s5_bwd_" + dirs[j][0])
           for j in range(2)]
    du_nat = s5b[0][0] + s5b[1][0] + jnp.concatenate([du_skip, jnp.zeros((nc, SSM_WIDTH), F32)], axis=0)

    do_f = _mm(da_l, wo_p, "nt", "attn_out_dx")
    g_wo_p = _mm(o_p, da_l, "tn", "attn_out_dw")
    g_big["w_o_attn"] = g_wo_p.reshape(N_HEADS, SLOT, d)[:, :V_DIM].reshape(N_HEADS * V_DIM, d)
    ffn_partials = chip_partials(ffn_names, "ffn_grads_to_sibling")

    dq_p, dk_p, dv_p, ffn_recv = _attn_bwd(q_p, k_p, v_p, do_f, o_p, lse, nl, scale, side=(scatter_in(ffn_partials), False))

    def qhead_bwd_body(qv, dqv, cv, sav, sbv, g):
        dxs, dgs = [], []
        for t, dt_ in zip(_heads(qv), _heads(dqv)):
            dx_, dg_ = _rms_bwd(t, g[:, :SLOT], _rope_bwd(dt_, cv, sav, sbv), QK_DIM)
            dxs.append(dx_)
            dgs.append(dg_)
        return jnp.concatenate(dxs, axis=1), jnp.concatenate(dgs, axis=1)

    dqh, dqg_acc = _rowwise(qhead_bwd_body, name="q_head_bwd", nblk=nlb, tr=tr,
                            rows=[(qh, 0, hw, 0), (dq_p, 0, hw, 0)] + rope_rows(), fulls=[qg_p],
                            outs=[(hw, MXU_DTYPE)], accs=[hw])
    dcqn = _mm(dqh, wuq_p, "nt", "q_up_dx")
    g_wuq_p = _mm(cqn, dqh, "tn", "q_up_dw")
    dcq, dqag_acc = _rowwise(lambda v, dy, g: _rms_bwd(v, g, dy, Q_LORA), name="qa_norm_bwd", nblk=nlb, tr=tr,
                             rows=[(z, q_cb, Q_LORA, 0), (dcqn, 0, Q_LORA, 0)], fulls=[q_a_g],
                             outs=[(Q_LORA, MXU_DTYPE)], accs=[Q_LORA])

    def khead_bwd_body(kv_, krv, dkv_, dvv_, cv, sav, sbv, g):
        kpe = pltpu.roll(krv, QK_NOPE, 1)
        lane = lax.broadcasted_iota(jnp.int32, krv.shape, 1)
        dxs, dgs, dkr_ = [], [], jnp.zeros(krv.shape, F32)
        for t, dt_ in zip(_heads(kv_), _heads(dkv_)):
            dx_, dg_ = _rms_bwd(t + kpe, g[:, :SLOT], _rope_bwd(dt_, cv, sav, sbv), QK_DIM)
            dxs.append(jnp.where(lane < QK_NOPE, dx_, 0.0))
            dgs.append(dg_)
            dkr_ = dkr_ + dx_
        dkr_ = jnp.where(lane < QK_ROPE, pltpu.roll(dkr_, SLOT - QK_NOPE, 1), 0.0)
        return jnp.concatenate(dxs + [dvv_], axis=1), dkr_, jnp.concatenate(dgs, axis=1)

    dkvpre, dkr, dkg_acc = _rowwise(khead_bwd_body, name="k_head_bwd", nblk=nb, tr=tr,
                                    rows=[(kvpre, 0, hw, 0), (z, kr_cb, SLOT, 0), (dk_p, 0, hw, 0), (dv_p, 0, hw, 0)] + rope_rows(),
                                    fulls=[kg_p], outs=[(2 * hw, MXU_DTYPE), (SLOT, MXU_DTYPE)], accs=[hw])
    dckvn = _mm(dkvpre, wukv_p, "nt", "kv_up_dx")
    g_wukv_p = _mm(ckvn, dkvpre, "tn", "kv_up_dw")
    dckv, dkvag_acc = _rowwise(lambda v, dy, g: _rms_bwd(v, g, dy, KV_LORA), name="kva_norm_bwd", nblk=nb, tr=tr,
                               rows=[(z, kv_cb, KV_LORA, 0), (dckvn, 0, KV_LORA, 0)], fulls=[kv_a_g],
                               outs=[(KV_LORA, MXU_DTYPE)], accs=[KV_LORA])

    padc = lambda t: jnp.concatenate([t, jnp.zeros((nc, t.shape[1]), t.dtype)], axis=0)
    dz = jnp.concatenate([padc(dgl), du_nat.astype(MXU_DTYPE), dckv, dkr,
                          jnp.zeros((n, q_off - kr_off - SLOT), MXU_DTYPE), padc(dcq)], axis=1)
    gwi = _mm(h1, dz, "tn", "in_proj_dw")
    g_big["w_in"] = jnp.concatenate([gwi[:, q_off:q_off + Q_LORA], gwi[:, kv_off:kv_off + KV_LORA],
                                     gwi[:, kr_off:kr_off + QK_ROPE], gwi[:, u_off:u_off + SSM_WIDTH], gwi[:, :2 * d]], axis=1)
    g_big["w_uq"] = g_wuq_p.reshape(Q_LORA, N_HEADS, SLOT)[:, :, :QK_DIM].reshape(Q_LORA, N_HEADS * QK_DIM)
    gk3 = g_wukv_p[:, :hw].reshape(KV_LORA, N_HEADS, SLOT)[:, :, :QK_NOPE]
    gv3 = g_wukv_p[:, hw:].reshape(KV_LORA, N_HEADS, SLOT)[:, :, :V_DIM]
    g_big["w_ukv"] = jnp.concatenate([gk3, gv3], axis=2).reshape(KV_LORA, N_HEADS * (QK_NOPE + V_DIM))
    early_partials = chip_partials(early_names, "grads_to_sibling")
    dh1 = _mm(dz, win_p, "nt", "in_proj_dx")

    def norm1_bwd_body(xv, dh, dx1v, m, g):
        sc1 = m[:, d:2 * d]
        y = _rms_fwd(xv, g, d)
        dxn, dgc = _rms_bwd(xv, g, dh * (1.0 + sc1), d)
        return dxn + dx1v, dgc, dh, dh * y

    (dxa, dn1g_acc, dsh1_acc, dsc1_acc), early_recv = _rowwise(
        norm1_bwd_body, name="norm1_bwd", nblk=nb, tr=tr, rows=[(xa, 0, d, 0), (dh1, 0, d, 0), (dx1, 0, d, 0)],
        sels=[modv], fulls=[n1g], outs=[(d, F32)], accs=[d, d, d], seg=nlb, side=(scatter_in(early_partials), False))
    grad_x = dxa[:nl][None]

    red8 = lambda a: jnp.sum(a, axis=-2)
    dmod_own = jnp.concatenate([red8(dsh1_acc[0]), red8(dsc1_acc[0]), red8(dg1_acc), red8(dsh2_acc), red8(dsc2_acc), red8(dg2_acc)])
    dmod_ctx = jnp.concatenate([red8(dsh1_acc[1]), red8(dsc1_acc[1]), jnp.zeros((4 * d,), F32)])
    dm_in = jnp.concatenate([dmod_own[None, :], dmod_ctx[None, :], jnp.zeros((6, d6), F32)], axis=0)
    (dm_all,) = _exchange8([dm_in], "gather_dmod", True)
    dm_own_sh = lax.dynamic_slice(dm_all[:, 0, :], (0, chip * csh), (8, csh))
    dm_ctx_sh = lax.dynamic_slice(dm_all[:, 1, :], (0, chip * csh), (8, csh))

    def mod_bwd_body(c_ref, own_ref, ctx_ref, w_ref, gw_ref, gb_ref, gc_ref):
        cv = c_ref[...]
        a = _silu(cv).astype(MXU_DTYPE)
        own = own_ref[...]
        ctx_tot = ctx_ref[0:1, :]
        for j in range(1, 8):
            ctx_tot = ctx_tot + ctx_ref[j:j + 1, :]
        g16 = jnp.concatenate([own, jnp.broadcast_to(ctx_tot, own.shape)], axis=0)
        rid = lax.broadcasted_iota(jnp.int32, g16.shape, 0)
        g16 = jnp.where(rid <= 8, g16, 0.0)
        gw_ref[...] = lax.dot_general(a, g16.astype(MXU_DTYPE), (((0,), (0,)), ((), ())), preferred_element_type=F32)
        gb_ref[...] = jnp.broadcast_to(jnp.sum(own, axis=0, keepdims=True) + ctx_tot, gb_ref.shape)
        gc = lax.dot_general(jnp.broadcast_to(ctx_tot, own.shape).astype(MXU_DTYPE), w_ref[...].astype(MXU_DTYPE),
                             (((1,), (1,)), ((), ())), preferred_element_type=F32)
        gc_ref[...] = gc * _dsilu(cv[8:9, :])

    g_wmod, g_bmod_sh, g_cctx_part = pl.pallas_call(
        mod_bwd_body, name="mod_bwd", out_shape=[_sds((d, csh), F32), _sds((8, csh), F32), _sds((8, d), F32)],
        compiler_params=pltpu.CompilerParams(vmem_limit_bytes=VMEM_LIMIT))(cs16, dm_own_sh, dm_ctx_sh, w_mod[0])
    north = (mc == 0).astype(F32)
    g_bmod_part = lax.dynamic_update_slice(jnp.zeros((1, d6), F32), g_bmod_sh[0:1] * north, (0, chip * csh))
    g_cctx_part = g_cctx_part[0] * north

    small_g = {}
    for j, dr in enumerate(dirs):
        sfx = dr[0]
        _, dbbd_j, dcbd_j, dlam_j = s5b[j]
        dl = red8(dlam_j).reshape(N_CG, 2, CG_STATES)
        db_re, db_im = _diag_extract(dbbd_j)
        cot = (dl[:, 0].reshape(SSM_GROUPS, SSM_STATE), dl[:, 1].reshape(SSM_GROUPS, SSM_STATE),
               jnp.transpose(db_re, (0, 2, 1)), jnp.transpose(db_im, (0, 2, 1)))
        g_lre, g_lim, g_ldt, g_bre, g_bim = disc_vjps[j](cot)
        small_g["lam_re_" + sfx], small_g["lam_im_" + sfx], small_g["log_dt_" + sfx] = g_lre, g_lim, g_ldt
        small_g["b_re"] = small_g.get("b_re", 0.0) + g_bre
        small_g["b_im"] = small_g.get("b_im", 0.0) + g_bim
        dc_re, dc_im = _diag_extract(dcbd_j)
        small_g["c_re_" + sfx], small_g["c_im_" + sfx] = dc_re, -dc_im
    head_fold = lambda acc: jnp.sum(red8(acc).reshape(N_HEADS, SLOT), axis=0)[:QK_DIM]
    small_g.update(c_ctx=g_cctx_part, b_mod=g_bmod_part[0], norm1_g=red8(dn1g_acc[0]) + red8(dn1g_acc[1]),
                   norm2_g=red8(dn2g_acc), q_a_g=red8(dqag_acc), kv_a_g=red8(dkvag_acc), q_norm_g=head_fold(dqg_acc),
                   k_norm_g=head_fold(dkg_acc), d_skip=red8(ddskip_acc), conv_b=red8(dcb_acc))
    g_convw_full = jnp.stack([red8(dcw0), red8(dcw1), red8(dcw2)])
    small_names = ["c_ctx", "b_mod", "norm1_g", "norm2_g", "q_a_g", "kv_a_g", "q_norm_g", "k_norm_g",
                   "lam_re_f", "lam_im_f", "log_dt_f", "c_re_f", "c_im_f", "lam_re_b", "lam_im_b", "log_dt_b",
                   "c_re_b", "c_im_b", "b_re", "b_im", "d_skip", "conv_b"]
    small_shapes = [weights[k].shape for k in small_names]
    spack = _pack([small_g[k] for k in small_names] + [g_convw_full], rows_mult=8)
    sred = _sum8(_exchange8([spack], "gather_small_grads", True)[0], "sum_small_grads")
    sg_list = _unpack(sred, small_shapes + [(3, f2)])
    g_small = dict(zip(small_names, sg_list[:-1]))
    g_small["conv_w"] = lax.dynamic_slice(sg_list[-1], (0, chip * cwid), (3, cwid))[None]

    reduced = dict(zip(early_names + ffn_names, chip_sums(early_names, early_partials, early_recv)
                       + chip_sums(ffn_names, ffn_partials, ffn_recv)))
    both = _sibling_exchange([reduced[k_] for k_ in big_names], "exchange_halves")
    g_sh = {k_: b_.reshape((1,) + weights[k_].shape[1:]) for k_, b_ in zip(big_names, both)}
    g_sh["w_mod"] = g_wmod[None]

    grads = {**g_sh, **g_small}
    outs_d, outs_m, outs_v = {}, {}, {}
    for k_ in ["w_mod"] + big_names:
        shp = weights[k_].shape
        res = _adamw(*[t.reshape(shp[1:]) for t in (grads[k_], weights[k_], mom_m[k_], mom_v[k_])], "adamw_" + k_)
        for dst, buf in zip((outs_d, outs_m, outs_v), res):
            dst[k_] = buf.reshape(shp)
    adam_small = small_names + ["conv_w"]
    shapes = [weights[k_].shape for k_ in adam_small]
    res = _adamw(*[_pack([src[k_] for k_ in adam_small], rows_mult=8) for src in (grads, weights, mom_m, mom_v)], "adamw_small")
    for dst, buf in zip((outs_d, outs_m, outs_v), res):
        dst.update(zip(adam_small, _unpack(buf, shapes)))
    grads = {k_: grads[k_].reshape(weights[k_].shape) for k_ in names}
    return (loss, grad_x, *[grads[k_] for k_ in names], *[outs_d[k_] for k_ in names],
            *[outs_m[k_] for k_ in names], *[outs_v[k_] for k_ in names])
```

```python
import functools
import math

import numpy as np
import jax
import jax.numpy as jnp
from jax import lax
from jax.experimental import pallas as pl
from jax.experimental.pallas import tpu as pltpu

F32 = jnp.float32
MXU_DTYPE = jnp.bfloat16
MESH = pl.DeviceIdType.MESH

EPS = 1e-6
N_HEADS = 8
QK_NOPE = 64
QK_ROPE = 32
QK_DIM = QK_NOPE + QK_ROPE
V_DIM = 64
SLOT = 128
Q_LORA = 384
KV_LORA = 256
GRID_W = 64
ROPE_THETA = 10000.0
SSM_WIDTH = 512
SSM_GROUP = 16
SSM_GROUPS = 32
SSM_STATE = 64
N_STATE = SSM_GROUPS * SSM_STATE
CG_STATES = 512
N_CG = N_STATE // CG_STATES
CG_CHANNELS = SSM_WIDTH // N_CG
SCAN_LANES = 512
PACK_W = 1024

ADAM_LR = 0.001
ADAM_B1 = 0.9
ADAM_B2 = 0.999
ADAM_EPS = 1e-08
ADAM_WD = 0.01
ADAM_STEP = 10

VMEM_LIMIT = 56 * 1024 * 1024
LOG2E = 1.4426950408889634


def _pick(n, cands):
    for c in cands:
        if c <= n and n % c == 0:
            return c
    return n


def _cparams(sem):
    return pltpu.CompilerParams(dimension_semantics=sem, vmem_limit_bytes=VMEM_LIMIT)


def _sds(shape, dtype):
    return jax.ShapeDtypeStruct(tuple(shape), dtype)


_K_CANDS = (2816, 2048, 1536, 1408, 1280, 1152, 1024, 896, 768, 704, 640, 512, 384, 256, 128, 64, 32, 16)
_M_CANDS = (2048, 1408, 1024, 768, 512, 384, 256, 128, 64, 32, 16)
_N_CANDS = (1408, 1152, 1024, 768, 512, 384, 256, 128)
MM_VMEM_BUDGET = 40 * 1024 * 1024


def _mm_tiles(m, n, k_opts, a_bytes, b_bytes, o_bytes, m_cands):
    tn = n if n <= _N_CANDS[0] else _pick(n, _N_CANDS)
    for tk in k_opts:
        for tm in ((m,) if m <= m_cands[0] else ()) + tuple(t for t in m_cands if t < m and m % t == 0):
            if 2 * (tm * tk * a_bytes + tk * tn * b_bytes + tm * tn * o_bytes) + tm * tn * 4 <= MM_VMEM_BUDGET:
                return tm, tn, tk
    raise ValueError("no matmul tiling fits")


def _mm(a, b, mode, name, out_dtype=F32, rows=None, a_off=0, b_off=0):
    a_bytes, b_bytes, o_bytes = a.dtype.itemsize, b.dtype.itemsize, jnp.dtype(out_dtype).itemsize
    if mode == "tn":
        t_rows = rows or a.shape[0]
        m, n = a.shape[1], b.shape[1]
        k_opts = tuple(t for t in _K_CANDS if t <= t_rows and t_rows % t == 0) or (t_rows,)
        tm, tn, tk = _mm_tiles(m, n, k_opts, a_bytes, b_bytes, o_bytes, _M_CANDS[1:])
        nk = t_rows // tk
        ao, bo = a_off // tk, b_off // tk
        grid = (m // tm, n // tn, nk)
        in_specs = [pl.BlockSpec((tk, tm), lambda i, j, k: (k + ao, i)),
                    pl.BlockSpec((tk, tn), lambda i, j, k: (k + bo, j))]
        dn = (((0,), (0,)), ((), ()))
    else:
        m = rows or a.shape[0]
        kdim = a.shape[1]
        n = b.shape[1] if mode == "nn" else b.shape[0]
        k_opts = (kdim,) + tuple(t for t in _K_CANDS if t < kdim and kdim % t == 0)
        tm, tn, tk = _mm_tiles(m, n, k_opts, a_bytes, b_bytes, o_bytes, _M_CANDS)
        nk = kdim // tk
        ao = a_off // tm
        grid = (m // tm, n // tn, nk)
        if mode == "nn":
            in_specs = [pl.BlockSpec((tm, tk), lambda i, j, k: (i + ao, k)),
                        pl.BlockSpec((tk, tn), lambda i, j, k: (k, j))]
            dn = (((1,), (0,)), ((), ()))
        else:
            in_specs = [pl.BlockSpec((tm, tk), lambda i, j, k: (i + ao, k)),
                        pl.BlockSpec((tn, tk), lambda i, j, k: (j, k))]
            dn = (((1,), (1,)), ((), ()))
    use_scratch = nk > 1 and out_dtype != F32

    def body(a_ref, b_ref, o_ref, *scr):
        r = lax.dot_general(a_ref[...].astype(MXU_DTYPE), b_ref[...].astype(MXU_DTYPE), dn,
                            preferred_element_type=F32)
        if nk == 1:
            o_ref[...] = r.astype(o_ref.dtype)
        else:
            k = pl.program_id(2)
            acc = scr[0] if use_scratch else o_ref

            @pl.when(k == 0)
            def _():
                acc[...] = r

            @pl.when(k > 0)
            def _():
                acc[...] += r

            if use_scratch:
                @pl.when(k == nk - 1)
                def _():
                    o_ref[...] = acc[...].astype(o_ref.dtype)

    return pl.pallas_call(
        body, name=name, grid=grid, in_specs=in_specs,
        out_specs=pl.BlockSpec((tm, tn), lambda i, j, k: (i, j)),
        out_shape=_sds((m, n), out_dtype),
        scratch_shapes=[pltpu.VMEM((tm, tn), F32)] if use_scratch else [],
        compiler_params=_cparams(("parallel", "parallel", "arbitrary")),
    )(a, b)


def _rowwise(body, *, name, nblk, tr, rows=(), halo=(), sels=(), fulls=(), outs=(), accs=(), seg=None, side=None):
    n_rows, n_sel, n_full, n_out, n_acc = len(rows), len(sels), len(fulls), len(outs), len(accs)
    halo = tuple(halo)
    maxw = max([r[2] for r in rows] + [o[0] for o in outs] + list(accs))
    sr = _pick(tr, tuple(s for s in (256, 128, 64, 32, 16) if s * maxw <= 131072) or (16,))
    nsub = tr // sr
    total8 = nblk * tr // 8

    def seg_of(i):
        return jnp.where(i >= seg, 1, 0) if seg is not None else 0

    in_specs, operands = [], []
    for arr, cb, w, roff in rows:
        ob = roff // tr
        last = arr.shape[0] // tr - 1
        in_specs.append(pl.BlockSpec((tr, w), lambda i, cb=cb, ob=ob, last=last: (jnp.minimum(i + ob, last), cb)))
        operands.append(arr)
    for h in halo:
        arr, cb, w, roff = rows[h]
        o8, t8 = roff // 8, tr // 8
        in_specs.append(pl.BlockSpec((8, w), lambda i, cb=cb, o8=o8, t8=t8: (jnp.maximum(i * t8 - 1, 0) + o8, cb)))
        in_specs.append(pl.BlockSpec((8, w), lambda i, cb=cb, o8=o8, t8=t8: (jnp.minimum((i + 1) * t8, total8 - 1) + o8, cb)))
        operands += [arr, arr]
    for arr in sels:
        in_specs.append(pl.BlockSpec((None,) + arr.shape[1:], lambda i: (seg_of(i), 0, 0)))
        operands.append(arr)
    for arr in fulls:
        in_specs.append(pl.BlockSpec(arr.shape, lambda i: (0, 0)))
        operands.append(arr)
    out_specs, out_shape = [], []
    for w, dt in outs:
        out_specs.append(pl.BlockSpec((tr, w), lambda i: (i, 0)))
        out_shape.append(_sds((nblk * tr, w), dt))
    for w in accs:
        if seg is None:
            out_specs.append(pl.BlockSpec((8, w), lambda i: (0, 0)))
            out_shape.append(_sds((8, w), F32))
        else:
            out_specs.append(pl.BlockSpec((None, 8, w), lambda i: (seg_of(i), 0, 0)))
            out_shape.append(_sds((2, 8, w), F32))
    n_halo = 2 * len(halo)

    def kern(*refs):
        row_refs = refs[:n_rows]
        halo_refs = refs[n_rows:n_rows + n_halo]
        sel_refs = refs[n_rows + n_halo:n_rows + n_halo + n_sel]
        full_refs = refs[n_rows + n_halo + n_sel:n_rows + n_halo + n_sel + n_full]
        o0 = n_rows + n_halo + n_sel + n_full
        out_refs = refs[o0:o0 + n_out]
        acc_refs = refs[o0 + n_out:o0 + n_out + n_acc]
        i = pl.program_id(0)
        if n_acc:
            first = (i == 0) if seg is None else ((i == 0) | (i == seg))

            @pl.when(first)
            def _():
                for a_ref in acc_refs:
                    a_ref[...] = jnp.zeros(a_ref.shape, F32)

        def sub(s, carry):
            r0 = pl.multiple_of(s * sr, sr)
            vals = []
            for idx, r in enumerate(row_refs):
                cur = r[pl.ds(r0, sr), :]
                if idx in halo:
                    hp = halo_refs[2 * halo.index(idx)]
                    hn = halo_refs[2 * halo.index(idx) + 1]
                    cur = cur.astype(F32)
                    rid = lax.broadcasted_iota(jnp.int32, cur.shape, 0)
                    lo = r[pl.ds(pl.multiple_of(jnp.maximum(r0 - 8, 0), 8), 8), :].astype(F32)
                    lo = jnp.where(s == 0, hp[...].astype(F32), lo)
                    lo = jnp.where((s == 0) & (i == 0), 0.0, lo)
                    hi = r[pl.ds(pl.multiple_of(jnp.minimum(r0 + sr, tr - 8), 8), 8), :].astype(F32)
                    hi = jnp.where(s == nsub - 1, hn[...].astype(F32), hi)
                    hi = jnp.where((s == nsub - 1) & (i == nblk - 1), 0.0, hi)
                    prev = jnp.where(rid == 0, jnp.broadcast_to(lo[7:8, :], cur.shape), pltpu.roll(cur, 1, 0))
                    nxt = jnp.where(rid == sr - 1, jnp.broadcast_to(hi[0:1, :], cur.shape), pltpu.roll(cur, sr - 1, 0))
                    vals.append((prev, cur, nxt))
                else:
                    vals.append(cur)
            res = body(*vals, *[r[...] for r in sel_refs], *[r[...] for r in full_refs])
            if not isinstance(res, (tuple, list)):
                res = (res,)
            for o_ref, v in zip(out_refs, res[:n_out]):
                o_ref[pl.ds(r0, sr), :] = v.astype(o_ref.dtype)
            for a_ref, v in zip(acc_refs, res[n_out:]):
                a_ref[...] += jnp.sum(v.astype(F32).reshape(sr // 8, 8, v.shape[-1]), axis=0)
            return carry

        lax.fori_loop(0, nsub, sub, 0)

    n_res = n_out + n_acc
    kern, s_in, s_out, s_shape, s_scr = _ride_along(kern, len(operands), n_res, 0, (nblk,), side)
    res = pl.pallas_call(
        kern, name=name, grid=(nblk,), in_specs=in_specs + s_in, out_specs=out_specs + s_out,
        out_shape=out_shape + s_shape, scratch_shapes=s_scr, compiler_params=_cparams(("arbitrary",)),
    )(*operands, *(side[0] if side else ()))
    return res if side is None else (res[:n_res], res[n_res:])


def _sigmoid(x):
    return 1.0 / (1.0 + jnp.exp(-x))


def _silu(x):
    return x * _sigmoid(x)


def _dsilu(x):
    s = _sigmoid(x)
    return s * (1.0 + x * (1.0 - s))


_GELU_K = math.sqrt(2.0 / math.pi)


def _gelu(x):
    return 0.5 * x * (1.0 + jnp.tanh(_GELU_K * (x + 0.044715 * x * x * x)))


def _dgelu(x):
    t = jnp.tanh(_GELU_K * (x + 0.044715 * x * x * x))
    return 0.5 * (1.0 + t) + 0.5 * x * (1.0 - t * t) * _GELU_K * (1.0 + 3.0 * 0.044715 * x * x)


def _rms_fwd(x, g, width):
    r = lax.rsqrt(jnp.sum(x * x, axis=-1, keepdims=True) * (1.0 / width) + EPS)
    return x * r * g


def _rms_bwd(x, g, dy, width):
    r = lax.rsqrt(jnp.sum(x * x, axis=-1, keepdims=True) * (1.0 / width) + EPS)
    xn = x * r
    dyg = dy * g
    dx = r * (dyg - xn * (jnp.sum(dyg * xn, axis=-1, keepdims=True) * (1.0 / width)))
    return dx, dy * xn


def _rope_fwd(y, c, sa, sb):
    return y * c + pltpu.roll(y, SLOT - 16, 1) * sa + pltpu.roll(y, 16, 1) * sb


def _rope_bwd(d, c, sa, sb):
    return d * c + pltpu.roll(d * sa, 16, 1) + pltpu.roll(d * sb, SLOT - 16, 1)


def _heads(v):
    return [v[:, h * SLOT:(h + 1) * SLOT] for h in range(N_HEADS)]


def _attn_fwd(q, k, v, nl, scale, side=None):
    n = k.shape[0]
    tq = _pick(nl, (4096, 2048, 1024, 512, 256, 128))
    tk = _pick(n, (2816, 1408, 1152, 768, 384, 256, 128))
    sub = min(tq, 512)
    nk = n // tk
    rep = tk // SLOT
    c = scale * LOG2E

    def body(q_ref, k_ref, v_ref, o_ref, lse_ref, m_sc, l_sc, acc_sc):
        ki = pl.program_id(2)

        @pl.when(ki == 0)
        def _():
            m_sc[...] = jnp.full(m_sc.shape, -jnp.inf, F32)
            l_sc[...] = jnp.zeros(l_sc.shape, F32)
            acc_sc[...] = jnp.zeros(acc_sc.shape, F32)

        kb, vb = k_ref[...], v_ref[...]
        for sb in range(tq // sub):
            rows = slice(sb * sub, (sb + 1) * sub)
            s = lax.dot_general(q_ref[rows, :], kb, (((1,), (1,)), ((), ())), preferred_element_type=F32)
            m_prev = m_sc[rows, :]
            m_new = jnp.maximum(m_prev, jnp.max(s, axis=1, keepdims=True) * c)
            alpha = jnp.exp2(m_prev - m_new)
            p = jnp.exp2(s * c - jnp.tile(m_new, (1, rep)))
            l_sc[rows, :] = alpha * l_sc[rows, :] + jnp.sum(p, axis=1, keepdims=True)
            acc_sc[rows, :] = alpha * acc_sc[rows, :] + jnp.dot(p.astype(MXU_DTYPE), vb, preferred_element_type=F32)
            m_sc[rows, :] = m_new

        @pl.when(ki == nk - 1)
        def _():
            l = l_sc[...]
            o_ref[...] = (acc_sc[...] / l).astype(o_ref.dtype)
            lse_ref[...] = jnp.transpose(m_sc[...] + jnp.log2(l))[0:8, :]

    grid = (N_HEADS, nl // tq, nk)
    body, s_in, s_out, s_shape, s_scr = _ride_along(body, 3, 2, 3, grid, side)
    res = pl.pallas_call(
        body, name="attn_fwd", grid=grid,
        in_specs=[pl.BlockSpec((tq, SLOT), lambda h, i, j: (i, h)),
                  pl.BlockSpec((tk, SLOT), lambda h, i, j: (j, h)),
                  pl.BlockSpec((tk, SLOT), lambda h, i, j: (j, h))] + s_in,
        out_specs=[pl.BlockSpec((tq, SLOT), lambda h, i, j: (i, h)),
                   pl.BlockSpec((None, 8, tq), lambda h, i, j: (h, 0, i))] + s_out,
        out_shape=[_sds((nl, N_HEADS * SLOT), MXU_DTYPE), _sds((N_HEADS, 8, nl), F32)] + s_shape,
        scratch_shapes=[pltpu.VMEM((tq, SLOT), F32), pltpu.VMEM((tq, SLOT), F32), pltpu.VMEM((tq, SLOT), F32)] + s_scr,
        compiler_params=_cparams(("arbitrary", "arbitrary", "arbitrary")),
    )(q, k, v, *(side[0] if side else ()))
    return res[0], res[1], res[2:]


def _attn_bwd(q, k, v, do, o, lse_t, nl, scale, side=None):
    n = k.shape[0]
    tq = _pick(nl, (2048, 1024, 512, 256, 128))
    tk = _pick(n, (2816, 1408, 1152, 768, 384, 256, 128))
    sub = _pick(tk, (256, 128))
    nq, nk = nl // tq, n // tk
    c = scale * LOG2E

    def body(q_ref, k_ref, v_ref, do_ref, o_ref, lse_ref, dq_ref, dk_ref, dv_ref, dq_acc, dk_acc, dv_acc):
        ki, qi = pl.program_id(1), pl.program_id(2)

        @pl.when((ki == 0) & (qi == 0))
        def _():
            dq_acc[...] = jnp.zeros(dq_acc.shape, F32)

        @pl.when(qi == 0)
        def _():
            dk_acc[...] = jnp.zeros(dk_acc.shape, F32)
            dv_acc[...] = jnp.zeros(dv_acc.shape, F32)

        qb, dof = q_ref[...], do_ref[...]
        dob = dof.astype(MXU_DTYPE)
        lse_r = lse_ref[0:1, :]
        dl_r = jnp.sum(jnp.transpose(dof * o_ref[...].astype(F32)), axis=0, keepdims=True)
        dq_part = None
        for sb in range(tk // sub):
            rows = slice(sb * sub, (sb + 1) * sub)
            kb = k_ref[rows, :]
            s_t = lax.dot_general(kb, qb, (((1,), (1,)), ((), ())), preferred_element_type=F32)
            p_t = jnp.exp2(s_t * c - lse_r)
            dp_t = lax.dot_general(v_ref[rows, :], dob, (((1,), (1,)), ((), ())), preferred_element_type=F32)
            ds_t = (p_t * (dp_t - dl_r) * scale).astype(MXU_DTYPE)
            dv_acc[rows, :] += jnp.dot(p_t.astype(MXU_DTYPE), dob, preferred_element_type=F32)
            dk_acc[rows, :] += jnp.dot(ds_t, qb, preferred_element_type=F32)
            part = lax.dot_general(kb, ds_t, (((0,), (0,)), ((), ())), preferred_element_type=F32)
            dq_part = part if dq_part is None else dq_part + part
        c0 = pl.multiple_of(qi * tq, tq)
        dq_acc[:, pl.ds(c0, tq)] += dq_part

        @pl.when(ki == nk - 1)
        def _():
            dq_ref[...] = jnp.transpose(dq_acc[:, pl.ds(c0, tq)])

        @pl.when(qi == nq - 1)
        def _():
            dk_ref[...] = dk_acc[...]
            dv_ref[...] = dv_acc[...]

    grid = (N_HEADS, nk, nq)
    body, s_in, s_out, s_shape, s_scr = _ride_along(body, 6, 3, 3, grid, side)
    res = pl.pallas_call(
        body, name="attn_bwd", grid=grid,
        in_specs=[pl.BlockSpec((tq, SLOT), lambda h, j, i: (i, h)),
                  pl.BlockSpec((tk, SLOT), lambda h, j, i: (j, h)),
                  pl.BlockSpec((tk, SLOT), lambda h, j, i: (j, h)),
                  pl.BlockSpec((tq, SLOT), lambda h, j, i: (i, h)),
                  pl.BlockSpec((tq, SLOT), lambda h, j, i: (i, h)),
                  pl.BlockSpec((None, 8, tq), lambda h, j, i: (h, 0, i))] + s_in,
        out_specs=[pl.BlockSpec((tq, SLOT), lambda h, j, i: (jnp.where(j == nk - 1, i, 0), h)),
                   pl.BlockSpec((tk, SLOT), lambda h, j, i: (j, h)),
                   pl.BlockSpec((tk, SLOT), lambda h, j, i: (j, h))] + s_out,
        out_shape=[_sds((nl, N_HEADS * SLOT), F32), _sds((n, N_HEADS * SLOT), F32), _sds((n, N_HEADS * SLOT), F32)] + s_shape,
        scratch_shapes=[pltpu.VMEM((SLOT, nl), F32), pltpu.VMEM((tk, SLOT), F32), pltpu.VMEM((tk, SLOT), F32)] + s_scr,
        compiler_params=_cparams(("arbitrary", "arbitrary", "arbitrary")),
    )(q, k, v, do, o, lse_t, *(side[0] if side else ()))
    return res[0], res[1], res[2], res[3:]


def _scan_consts(c_ref, lg):
    cs = slice(lg * SCAN_LANES, (lg + 1) * SCAN_LANES)
    return [c_ref[8 * kk:8 * kk + 8, cs] for kk in range(8)]


def _tile_scan(br, bi, consts, reverse):
    p1r, p1i, p2r, p2i, p4r, p4i = consts[:6]
    for pr, pi, kk in ((p1r, p1i, 1), (p2r, p2i, 2), (p4r, p4i, 4)):
        sh = (8 - kk) if reverse else kk
        sr_, si_ = pltpu.roll(br, sh, 0), pltpu.roll(bi, sh, 0)
        br, bi = br + pr * sr_ - pi * si_, bi + pr * si_ + pi * sr_
    return br, bi


def _seq_chunk(j, nch, nlc, reverse):
    return (nch - 1 - j) if reverse else (j + nlc) % nch


def _s5_scan(z, u_cb, bbd, cbd_n, lamc, t_rows, nl, reverse, name):
    n = z.shape[0]
    nch, nlc = n // t_rows, nl // t_rows
    ntile = t_rows // 8
    w = SCAN_LANES
    edge = 0 if reverse else 7
    ucb = u_cb * (SSM_WIDTH // CG_CHANNELS)

    def chunk(j):
        return _seq_chunk(j, nch, nlc, reverse)

    def body(u_ref, b_ref, cn_ref, c_ref, xs_ref, y_ref, carry):
        j = pl.program_id(1)

        @pl.when(j == 0)
        def _():
            carry[...] = jnp.zeros(carry.shape, F32)

        xs_ref[...] = jnp.dot(u_ref[...].astype(MXU_DTYPE), b_ref[...], preferred_element_type=F32)
        for lg in range(CG_STATES // w):
            re = slice(lg * w, (lg + 1) * w)
            im = slice(CG_STATES + lg * w, CG_STATES + (lg + 1) * w)
            consts = _scan_consts(c_ref, lg)
            qr, qi = consts[6], consts[7]

            def tile(tt, st):
                cr, ci = st
                t = (ntile - 1 - tt) if reverse else tt
                r0 = pl.multiple_of(t * 8, 8)
                br, bi = _tile_scan(xs_ref[pl.ds(r0, 8), re], xs_ref[pl.ds(r0, 8), im], consts, reverse)
                lr = jnp.broadcast_to(cr[edge:edge + 1, :], br.shape)
                li = jnp.broadcast_to(ci[edge:edge + 1, :], bi.shape)
                xr = br + qr * lr - qi * li
                xi = bi + qr * li + qi * lr
                xs_ref[pl.ds(r0, 8), re] = xr
                xs_ref[pl.ds(r0, 8), im] = xi
                return xr, xi

            cr, ci = lax.fori_loop(0, ntile, tile, (carry[:, re], carry[:, im]))
            carry[:, re] = cr
            carry[:, im] = ci
        y_ref[...] = jnp.dot(xs_ref[...].astype(MXU_DTYPE), cn_ref[...], preferred_element_type=F32)

    cw = 2 * CG_STATES
    return pl.pallas_call(
        body, name=name, grid=(N_CG, nch),
        in_specs=[pl.BlockSpec((t_rows, CG_CHANNELS), lambda g, j: (chunk(j), ucb + g)),
                  pl.BlockSpec((CG_CHANNELS, cw), lambda g, j: (g, 0)),
                  pl.BlockSpec((cw, CG_CHANNELS), lambda g, j: (g, 0)),
                  pl.BlockSpec((64, CG_STATES), lambda g, j: (0, g))],
        out_specs=[pl.BlockSpec((t_rows, cw), lambda g, j: (chunk(j), g)),
                   pl.BlockSpec((t_rows, CG_CHANNELS), lambda g, j: (chunk(j), g))],
        out_shape=[_sds((n, 2 * N_STATE), F32), _sds((n, SSM_WIDTH), F32)],
        scratch_shapes=[pltpu.VMEM((8, cw), F32)],
        compiler_params=_cparams(("arbitrary", "arbitrary")),
    )(z, bbd, cbd_n, lamc)


def _s5_bwd(dys, z, u_cb, xs, cbd_t, bbd_t, lamc_adj, t_rows, nl, reverse, name):
    n = z.shape[0]
    nch, nlc = n // t_rows, nl // t_rows
    ntile = t_rows // 8
    t8 = t_rows // 8
    w = SCAN_LANES
    cw = 2 * CG_STATES
    adj_rev = not reverse
    edge = 0 if adj_rev else 7

    def chunk(j):
        return _seq_chunk(nch - 1 - j, nch, nlc, reverse)

    def halo_blk(j):
        if reverse:
            return jnp.minimum((chunk(j) + 1) * t8, n // 8 - 1)
        return (_seq_chunk(jnp.maximum(nch - 2 - j, 0), nch, nlc, False) + 1) * t8 - 1

    def body(dy_ref, u_ref, xs_ref, halo_ref, ct_ref, bt_ref, c_ref, du_ref, db_ref, dc_ref, dl_ref, gbuf, carry):
        j = pl.program_id(1)
        start = j == nch - 1

        @pl.when(j == 0)
        def _():
            carry[...] = jnp.zeros(carry.shape, F32)
            db_ref[...] = jnp.zeros(db_ref.shape, F32)
            dc_ref[...] = jnp.zeros(dc_ref.shape, F32)
            dl_ref[...] = jnp.zeros(dl_ref.shape, F32)

        dy = jnp.where(chunk(j) < nlc, dy_ref[...], 0.0).astype(MXU_DTYPE)
        gbuf[...] = jnp.dot(dy, ct_ref[...], preferred_element_type=F32)
        dc_ref[...] += lax.dot_general(dy, xs_ref[...].astype(MXU_DTYPE), (((0,), (0,)), ((), ())),
                                       preferred_element_type=F32)
        for lg in range(CG_STATES // w):
            re = slice(lg * w, (lg + 1) * w)
            im = slice(CG_STATES + lg * w, CG_STATES + (lg + 1) * w)
            consts = _scan_consts(c_ref, lg)
            qr, qi = consts[6], consts[7]
            hr, hi = halo_ref[:, re], halo_ref[:, im]

            def tile(tt, st):
                gcr, gci, ar, ai = st
                t = (ntile - 1 - tt) if adj_rev else tt
                r0 = pl.multiple_of(t * 8, 8)
                br, bi = _tile_scan(gbuf[pl.ds(r0, 8), re], gbuf[pl.ds(r0, 8), im], consts, adj_rev)
                lr = jnp.broadcast_to(gcr[edge:edge + 1, :], br.shape)
                li = jnp.broadcast_to(gci[edge:edge + 1, :], bi.shape)
                gr = br + qr * lr - qi * li
                gi = bi + qr * li + qi * lr
                gbuf[pl.ds(r0, 8), re] = gr
                gbuf[pl.ds(r0, 8), im] = gi
                xr, xi = xs_ref[pl.ds(r0, 8), re], xs_ref[pl.ds(r0, 8), im]
                rid = lax.broadcasted_iota(jnp.int32, xr.shape, 0)
                if reverse:
                    last = t == ntile - 1
                    rn = pl.multiple_of(jnp.minimum(r0 + 8, t_rows - 8), 8)
                    nbr = jnp.where(last, hr, xs_ref[pl.ds(rn, 8), re])
                    nbi = jnp.where(last, hi, xs_ref[pl.ds(rn, 8), im])
                    nbr = jnp.where(last & start, 0.0, nbr)
                    nbi = jnp.where(last & start, 0.0, nbi)
                    xpr = jnp.where(rid == 7, jnp.broadcast_to(nbr[0:1, :], xr.shape), pltpu.roll(xr, 7, 0))
                    xpi = jnp.where(rid == 7, jnp.broadcast_to(nbi[0:1, :], xi.shape), pltpu.roll(xi, 7, 0))
                else:
                    first = t == 0
                    rn = pl.multiple_of(jnp.maximum(r0 - 8, 0), 8)
                    nbr = jnp.where(first, hr, xs_ref[pl.ds(rn, 8), re])
                    nbi = jnp.where(first, hi, xs_ref[pl.ds(rn, 8), im])
                    nbr = jnp.where(first & start, 0.0, nbr)
                    nbi = jnp.where(first & start, 0.0, nbi)
                    xpr = jnp.where(rid == 0, jnp.broadcast_to(nbr[7:8, :], xr.shape), pltpu.roll(xr, 1, 0))
                    xpi = jnp.where(rid == 0, jnp.broadcast_to(nbi[7:8, :], xi.shape), pltpu.roll(xi, 1, 0))
                ar = ar + gr * xpr + gi * xpi
                ai = ai - gr * xpi + gi * xpr
                return gr, gi, ar, ai

            zz = jnp.zeros((8, w), F32)
            gcr, gci, ar, ai = lax.fori_loop(0, ntile, tile, (carry[:, re], carry[:, im], zz, zz))
            carry[:, re] = gcr
            carry[:, im] = gci
            dl_ref[:, re] += ar
            dl_ref[:, im] += ai
        g = gbuf[...].astype(MXU_DTYPE)
        du_ref[...] = jnp.dot(g, bt_ref[...], preferred_element_type=F32)
        db_ref[...] += lax.dot_general(u_ref[...].astype(MXU_DTYPE), g, (((0,), (0,)), ((), ())),
                                       preferred_element_type=F32)

    ucb = u_cb * (SSM_WIDTH // CG_CHANNELS)
    return pl.pallas_call(
        body, name=name, grid=(N_CG, nch),
        in_specs=[pl.BlockSpec((t_rows, CG_CHANNELS), lambda g, j: (jnp.minimum(chunk(j), nlc - 1), g)),
                  pl.BlockSpec((t_rows, CG_CHANNELS), lambda g, j: (chunk(j), ucb + g)),
                  pl.BlockSpec((t_rows, cw), lambda g, j: (chunk(j), g)),
                  pl.BlockSpec((8, cw), lambda g, j: (halo_blk(j), g)),
                  pl.BlockSpec((CG_CHANNELS, cw), lambda g, j: (g, 0)),
                  pl.BlockSpec((cw, CG_CHANNELS), lambda g, j: (g, 0)),
                  pl.BlockSpec((64, CG_STATES), lambda g, j: (0, g))],
        out_specs=[pl.BlockSpec((t_rows, CG_CHANNELS), lambda g, j: (chunk(j), g)),
                   pl.BlockSpec((CG_CHANNELS, cw), lambda g, j: (g, 0)),
                   pl.BlockSpec((CG_CHANNELS, cw), lambda g, j: (g, 0)),
                   pl.BlockSpec((8, cw), lambda g, j: (0, g))],
        out_shape=[_sds((n, SSM_WIDTH), F32), _sds((SSM_WIDTH, cw), F32), _sds((SSM_WIDTH, cw), F32),
                   _sds((8, 2 * N_STATE), F32)],
        scratch_shapes=[pltpu.VMEM((t_rows, cw), F32), pltpu.VMEM((8, cw), F32)],
        compiler_params=_cparams(("arbitrary", "arbitrary")),
    )(dys, z, xs, xs, cbd_t, bbd_t, lamc_adj)


_CG_GROUPS = SSM_GROUPS // N_CG


def _group_mask():
    idx = jnp.arange(_CG_GROUPS)
    return (idx[:, None] == idx[None, :])[None, :, None, None, :, None]


def _diag_blocks(p_re, p_im):
    t = jnp.stack([p_re, p_im], axis=2).reshape(N_CG, _CG_GROUPS, SSM_GROUP, 2, 1, SSM_STATE)
    return jnp.where(_group_mask(), t, 0.0).reshape(SSM_WIDTH, 2 * CG_STATES)


def _diag_extract(d):
    d6 = d.reshape(N_CG, _CG_GROUPS, SSM_GROUP, 2, _CG_GROUPS, SSM_STATE)
    blk = jnp.sum(jnp.where(_group_mask(), d6, 0.0), axis=4)
    blk = blk.reshape(SSM_GROUPS, SSM_GROUP, 2, SSM_STATE)
    return blk[:, :, 0], blk[:, :, 1]


def _block_transpose(d):
    return jnp.transpose(d.reshape(N_CG, CG_CHANNELS, 2 * CG_STATES), (0, 2, 1)).reshape(2 * N_STATE, CG_CHANNELS)


def _s5_disc(lam_re, lam_im, log_dt, b_re, b_im):
    lam = lax.complex(lam_re, lam_im)
    dt = jnp.exp(log_dt)[:, None]
    lam_bar = jnp.exp(lam * dt)
    b_bar = ((lam_bar - 1.0) / lam)[..., None] * lax.complex(b_re, b_im)
    return jnp.real(lam_bar), jnp.imag(lam_bar), jnp.real(b_bar), jnp.imag(b_bar)


def _lam_consts(lr, li, mirrored, conj):
    lam = lax.complex(lr.reshape(-1), -li.reshape(-1) if conj else li.reshape(-1))
    p2 = lam * lam
    p4 = p2 * p2
    pw = [lam, p2, p2 * lam, p4, p4 * lam, p4 * p2, p4 * p2 * lam, p4 * p4]
    rows = jnp.arange(8)[:, None]
    out = []
    for kk in (1, 2, 4):
        mask = (rows <= 7 - kk) if mirrored else (rows >= kk)
        pk = jnp.where(mask, pw[kk - 1][None, :], 0.0)
        out += [jnp.real(pk), jnp.imag(pk)]
    q = jnp.stack(pw[::-1] if mirrored else pw)
    return jnp.concatenate(out + [jnp.real(q), jnp.imag(q)], axis=0)


def _dev(t):
    return (t // 4, (t // 2) % 2, t % 2)


def _my_index():
    return 4 * lax.axis_index("x") + 2 * lax.axis_index("y") + lax.axis_index("c")


def _comm_call(body, name, arrs, lead, n_remote):
    nw = len(arrs)
    any_spec = pl.BlockSpec(memory_space=pl.ANY)
    return pl.pallas_call(
        body, name=name, out_shape=[_sds((lead,) + a.shape[-2:], a.dtype) for a in arrs],
        in_specs=[any_spec] * nw, out_specs=[any_spec] * nw,
        scratch_shapes=[pltpu.SemaphoreType.DMA((n_remote * nw,)), pltpu.SemaphoreType.DMA((n_remote * nw,)),
                        pltpu.SemaphoreType.DMA((2 * nw,))] + [pltpu.VMEM(a.shape[-2:], a.dtype) for a in arrs],
        compiler_params=pltpu.CompilerParams(vmem_limit_bytes=VMEM_LIMIT),
    )(*arrs)


class _LocalCopy:
    def __init__(self, src, dst, buf, sem_in, sem_out):
        self.fetch = pltpu.make_async_copy(src, buf, sem_in)
        self.store = pltpu.make_async_copy(buf, dst, sem_out)
        self.fetch.start()

    def forward(self):
        self.fetch.wait()
        self.store.start()

    def finish(self):
        self.store.wait()


ALL8 = "all8"


def _all8_copies(g_refs, o_refs, ssem, rsem, lsem, bufs):
    me = _my_index()
    fetch, store, sends, recvs = [], [], [], []
    for i, (g_ref, o_ref) in enumerate(zip(g_refs, o_refs)):
        fetch.append(pltpu.make_async_copy(g_ref, bufs[i], lsem.at[2 * i]))
        store.append(pltpu.make_async_copy(bufs[i], o_ref.at[me], lsem.at[2 * i + 1]))
        for dd in range(1, 8):
            t, s = (me + dd) % 8, (me + 8 - dd) % 8
            sems = dict(send_sem=ssem.at[7 * i + dd - 1], recv_sem=rsem.at[7 * i + dd - 1], device_id_type=MESH)
            sends.append(pltpu.make_async_remote_copy(src_ref=g_ref, dst_ref=o_ref.at[me], device_id=_dev(t), **sems))
            recvs.append(pltpu.make_async_remote_copy(src_ref=g_ref, dst_ref=o_ref.at[s], device_id=_dev(s), **sems))
    return fetch, store, sends, recvs


def _exchange8(gs, name):
    nw = len(gs)

    def body(*refs):
        args = (refs[:nw], refs[nw:2 * nw], *refs[2 * nw:2 * nw + 3], refs[2 * nw + 3:], ALL8)
        _chips_start(*args)
        _chips_finish(*args)

    return _comm_call(body, name, gs, 8, 7)


def _chip_copies(w_refs, o_refs, ssem, rsem, lsem, bufs, gather):
    if gather == ALL8:
        return _all8_copies(w_refs, o_refs, ssem, rsem, lsem, bufs)
    x, y, cc = lax.axis_index("x"), lax.axis_index("y"), lax.axis_index("c")
    k = 2 * x + y
    peers = [(1 - x, y), (x, 1 - y), (1 - x, 1 - y)]
    fetch, store, sends, recvs = [], [], [], []
    for i, (w_ref, o_ref) in enumerate(zip(w_refs, o_refs)):
        if gather:
            fetch.append(pltpu.make_async_copy(w_ref.at[cc], bufs[i], lsem.at[2 * i]))
            store.append(pltpu.make_async_copy(bufs[i], o_ref.at[k], lsem.at[2 * i + 1]))
        for j, (px, py) in enumerate(peers):
            sems = dict(send_sem=ssem.at[3 * i + j], recv_sem=rsem.at[3 * i + j], device_id=(px, py, cc), device_id_type=MESH)
            src, dst = (w_ref.at[cc], o_ref.at[k]) if gather else (w_ref.at[2 * px + py], o_ref.at[j])
            sends.append(pltpu.make_async_remote_copy(src_ref=src, dst_ref=dst, **sems))
            src, dst = (w_ref.at[cc], o_ref.at[2 * px + py]) if gather else (w_ref.at[k], o_ref.at[j])
            recvs.append(pltpu.make_async_remote_copy(src_ref=src, dst_ref=dst, **sems))
    return fetch, store, sends, recvs


def _chips_start(*args):
    fetch, _, sends, _ = _chip_copies(*args)
    for cp in fetch + sends:
        cp.start()


def _chips_finish(*args):
    fetch, store, sends, recvs = _chip_copies(*args)
    for cp in fetch:
        cp.wait()
    for cp in store:
        cp.start()
    for cp in recvs:
        cp.wait_recv()
    for cp in sends:
        cp.wait_send()
    for cp in store:
        cp.wait()


def _chips_scratch(ws, gather):
    nw = len(ws)
    n_remote = 7 if gather == ALL8 else 3
    return ([pltpu.SemaphoreType.DMA((n_remote * nw,)), pltpu.SemaphoreType.DMA((n_remote * nw,)),
             pltpu.SemaphoreType.DMA((2 * nw,))] + ([pltpu.VMEM(a.shape[-2:], a.dtype) for a in ws] if gather else []))


def _ride_along(core, n_in, n_out, n_scr, grid, side):
    if side is None:
        return core, [], [], [], []
    arrs, gather = side
    ns = len(arrs)

    def body(*refs):
        a, b, c_ = n_in + ns, n_in + ns + n_out, n_in + 2 * ns + n_out
        s_scr = refs[c_ + n_scr:]
        sargs = (refs[n_in:a], refs[b:c_], *s_scr[:3], s_scr[3:], gather)
        ids = [pl.program_id(ax) for ax in range(len(grid))]
        first, last = ids[0] == 0, ids[0] == grid[0] - 1
        for i_, g_ in zip(ids[1:], grid[1:]):
            first, last = first & (i_ == 0), last & (i_ == g_ - 1)

        @pl.when(first)
        def _():
            _chips_start(*sargs)

        core(*refs[:n_in], *refs[a:b], *refs[c_:c_ + n_scr])

        @pl.when(last)
        def _():
            _chips_finish(*sargs)

    any_spec = pl.BlockSpec(memory_space=pl.ANY)
    lead = 8 if gather == ALL8 else (4 if gather else 3)
    shapes = [_sds((lead,) + a_.shape[-2:], a_.dtype) for a_ in arrs]
    return body, [any_spec] * ns, [any_spec] * ns, shapes, _chips_scratch(arrs, gather)


def _sibling_send(hs, name):
    nw = len(hs)

    def body(*refs):
        h_refs, o_refs, (ssem, rsem, lsem) = refs[:nw], refs[nw:2 * nw], refs[2 * nw:]
        x, y, cc = lax.axis_index("x"), lax.axis_index("y"), lax.axis_index("c")
        sends = []
        for i, (h_ref, o_ref) in enumerate(zip(h_refs, o_refs)):
            cp = pltpu.make_async_remote_copy(src_ref=h_ref.at[1 - cc], dst_ref=o_ref, send_sem=ssem.at[i],
                                              recv_sem=rsem.at[i], device_id=(x, y, 1 - cc), device_id_type=MESH)
            cp.start()
            sends.append(cp)
        for i, (h_ref, o_ref) in enumerate(zip(h_refs, o_refs)):
            pltpu.make_async_remote_copy(src_ref=h_ref.at[cc], dst_ref=o_ref, send_sem=ssem.at[i], recv_sem=rsem.at[i],
                                         device_id=(x, y, 1 - cc), device_id_type=MESH).wait_recv()
        for cp in sends:
            cp.wait_send()

    nw_spec = pl.BlockSpec(memory_space=pl.ANY)
    return pl.pallas_call(
        body, name=name, out_shape=[_sds(h.shape[1:], h.dtype) for h in hs],
        in_specs=[nw_spec] * nw, out_specs=[nw_spec] * nw,
        scratch_shapes=[pltpu.SemaphoreType.DMA((nw,)), pltpu.SemaphoreType.DMA((nw,)), pltpu.SemaphoreType.DMA((nw,))],
    )(*hs)


def _sibling_exchange(hs, name):
    nw = len(hs)

    def body(*refs):
        h_refs, o_refs, (ssem, rsem, lsem), bufs = refs[:nw], refs[nw:2 * nw], refs[2 * nw:2 * nw + 3], refs[2 * nw + 3:]
        x, y, cc = lax.axis_index("x"), lax.axis_index("y"), lax.axis_index("c")
        locs, sends = [], []
        for i, (h_ref, o_ref) in enumerate(zip(h_refs, o_refs)):
            locs.append(_LocalCopy(h_ref, o_ref.at[cc], bufs[i], lsem.at[2 * i], lsem.at[2 * i + 1]))
            cp = pltpu.make_async_remote_copy(src_ref=h_ref, dst_ref=o_ref.at[cc], send_sem=ssem.at[i], recv_sem=rsem.at[i],
                                              device_id=(x, y, 1 - cc), device_id_type=MESH)
            cp.start()
            sends.append(cp)
        for loc in locs:
            loc.forward()
        for i, (h_ref, o_ref) in enumerate(zip(h_refs, o_refs)):
            pltpu.make_async_remote_copy(src_ref=h_ref, dst_ref=o_ref.at[1 - cc], send_sem=ssem.at[i], recv_sem=rsem.at[i],
                                         device_id=(x, y, 1 - cc), device_id_type=MESH).wait_recv()
        for cp in sends:
            cp.wait_send()
        for loc in locs:
            loc.finish()

    return _comm_call(body, name, hs, 2, 1)


def _sum8(buf, name):
    _, r, c = buf.shape
    tr = _pick(r, (256, 128, 64, 32, 16, 8))
    flat = buf.reshape(8 * r, c)

    def body(*v):
        acc = v[0]
        for t in v[1:]:
            acc = acc + t
        return acc

    return _rowwise(body, name=name, nblk=r // tr, tr=tr, rows=[(flat, 0, c, s * r) for s in range(8)],
                    outs=[(c, F32)])[0]


def _pack(arrs, rows_mult=16):
    flat = jnp.concatenate([a.reshape(-1).astype(F32) for a in arrs])
    nel = flat.shape[0]
    r = -(-nel // PACK_W)
    r = -(-r // rows_mult) * rows_mult
    return jnp.pad(flat, (0, r * PACK_W - nel)).reshape(r, PACK_W)


def _unpack(buf, shapes):
    flat = buf.reshape(-1)
    out, o = [], 0
    for s in shapes:
        nel = int(np.prod(s))
        out.append(flat[o:o + nel].reshape(s))
        o += nel
    return out


def _adamw(g, w, m, v, name):
    r, wd = g.shape
    tr = _pick(r, tuple(t for t in (256, 128, 64, 32, 16, 8) if t * wd <= 262144) or (8,))
    c1 = 1.0 / (1.0 - ADAM_B1 ** ADAM_STEP)
    c2 = 1.0 / (1.0 - ADAM_B2 ** ADAM_STEP)

    def body(gv, wv, mv, vv):
        mn = ADAM_B1 * mv + (1.0 - ADAM_B1) * gv
        vn = ADAM_B2 * vv + (1.0 - ADAM_B2) * (gv * gv)
        delta = -ADAM_LR * ((mn * c1) / (jnp.sqrt(vn * c2) + ADAM_EPS) + ADAM_WD * wv)
        return delta, mn, vn

    return _rowwise(body, name=name, nblk=r // tr, tr=tr, rows=[(a, 0, wd, 0) for a in (g, w, m, v)],
                    outs=[(wd, F32)] * 3)


def kernel(x, c, ctx, c_ctx, w_mod, b_mod, norm1_g, norm2_g, w_in, q_a_g, w_uq, kv_a_g, w_ukv, q_norm_g, k_norm_g, w_o_attn, lam_re_f, lam_im_f, log_dt_f, c_re_f, c_im_f, lam_re_b, lam_im_b, log_dt_b, c_re_b, c_im_b, b_re, b_im, d_skip, w_glu, w_out, w_up, conv_w, conv_b, w_down, loss_target, m_c_ctx, m_w_mod, m_b_mod, m_norm1_g, m_norm2_g, m_w_in, m_q_a_g, m_w_uq, m_kv_a_g, m_w_ukv, m_q_norm_g, m_k_norm_g, m_w_o_attn, m_lam_re_f, m_lam_im_f, m_log_dt_f, m_c_re_f, m_c_im_f, m_lam_re_b, m_lam_im_b, m_log_dt_b, m_c_re_b, m_c_im_b, m_b_re, m_b_im, m_d_skip, m_w_glu, m_w_out, m_w_up, m_conv_w, m_conv_b, m_w_down, v_c_ctx, v_w_mod, v_b_mod, v_norm1_g, v_norm2_g, v_w_in, v_q_a_g, v_w_uq, v_kv_a_g, v_w_ukv, v_q_norm_g, v_k_norm_g, v_w_o_attn, v_lam_re_f, v_lam_im_f, v_log_dt_f, v_c_re_f, v_c_im_f, v_lam_re_b, v_lam_im_b, v_log_dt_b, v_c_re_b, v_c_im_b, v_b_re, v_b_im, v_d_skip, v_w_glu, v_w_out, v_w_up, v_conv_w, v_conv_b, v_w_down):
    weights = dict(c_ctx=c_ctx, w_mod=w_mod, b_mod=b_mod, norm1_g=norm1_g, norm2_g=norm2_g, w_in=w_in, q_a_g=q_a_g, w_uq=w_uq, kv_a_g=kv_a_g, w_ukv=w_ukv, q_norm_g=q_norm_g, k_norm_g=k_norm_g, w_o_attn=w_o_attn, lam_re_f=lam_re_f, lam_im_f=lam_im_f, log_dt_f=log_dt_f, c_re_f=c_re_f, c_im_f=c_im_f, lam_re_b=lam_re_b, lam_im_b=lam_im_b, log_dt_b=log_dt_b, c_re_b=c_re_b, c_im_b=c_im_b, b_re=b_re, b_im=b_im, d_skip=d_skip, w_glu=w_glu, w_out=w_out, w_up=w_up, conv_w=conv_w, conv_b=conv_b, w_down=w_down)
    mom_m = dict(c_ctx=m_c_ctx, w_mod=m_w_mod, b_mod=m_b_mod, norm1_g=m_norm1_g, norm2_g=m_norm2_g, w_in=m_w_in, q_a_g=m_q_a_g, w_uq=m_w_uq, kv_a_g=m_kv_a_g, w_ukv=m_w_ukv, q_norm_g=m_q_norm_g, k_norm_g=m_k_norm_g, w_o_attn=m_w_o_attn, lam_re_f=m_lam_re_f, lam_im_f=m_lam_im_f, log_dt_f=m_log_dt_f, c_re_f=m_c_re_f, c_im_f=m_c_im_f, lam_re_b=m_lam_re_b, lam_im_b=m_lam_im_b, log_dt_b=m_log_dt_b, c_re_b=m_c_re_b, c_im_b=m_c_im_b, b_re=m_b_re, b_im=m_b_im, d_skip=m_d_skip, w_glu=m_w_glu, w_out=m_w_out, w_up=m_w_up, conv_w=m_conv_w, conv_b=m_conv_b, w_down=m_w_down)
    mom_v = dict(c_ctx=v_c_ctx, w_mod=v_w_mod, b_mod=v_b_mod, norm1_g=v_norm1_g, norm2_g=v_norm2_g, w_in=v_w_in, q_a_g=v_q_a_g, w_uq=v_w_uq, kv_a_g=v_kv_a_g, w_ukv=v_w_ukv, q_norm_g=v_q_norm_g, k_norm_g=v_k_norm_g, w_o_attn=v_w_o_attn, lam_re_f=v_lam_re_f, lam_im_f=v_lam_im_f, log_dt_f=v_log_dt_f, c_re_f=v_c_re_f, c_im_f=v_c_im_f, lam_re_b=v_lam_re_b, lam_im_b=v_lam_im_b, log_dt_b=v_log_dt_b, c_re_b=v_c_re_b, c_im_b=v_c_im_b, b_re=v_b_re, b_im=v_b_im, d_skip=v_d_skip, w_glu=v_w_glu, w_out=v_w_out, w_up=v_w_up, conv_w=v_conv_w, conv_b=v_conv_b, w_down=v_w_down)
    names = list(weights)

    nl, d = x.shape[1], x.shape[2]
    nc = ctx.shape[1]
    n = nl + nc
    f2 = conv_b.shape[1]
    fh = f2 // 2
    d6 = b_mod.shape[1]
    mx, my, mc = lax.axis_index("x"), lax.axis_index("y"), lax.axis_index("c")
    chip = 2 * mx + my
    me = 4 * mx + 2 * my + mc
    tr = _pick(math.gcd(nl, nc), (256, 128, 64, 32, 16))
    nlb, nb = nl // tr, n // tr

    big_names = ["w_in", "w_uq", "w_ukv", "w_o_attn", "w_glu", "w_out", "w_up", "w_down"]
    row_sharded = ("w_out", "w_down")
    ffn_names = ["w_o_attn", "w_glu", "w_out", "w_up", "w_down"]
    early_names = [k for k in big_names if k not in ffn_names]
    full = {}

    def halves_in(names_):
        return [weights[k][0].astype(MXU_DTYPE).reshape(2, weights[k].shape[1] // 2, weights[k].shape[2]) for k in names_]

    def assemble(names_, my_halves, name):
        gathered = _sibling_exchange([t.reshape(-1, t.shape[2]) for t in my_halves], name)
        for k_, gth in zip(names_, gathered):
            r_, c_ = weights[k_].shape[1:]
            g4 = gth.reshape(2, 4, r_ // 2, c_)
            full[k_] = (jnp.transpose(g4, (1, 0, 2, 3)).reshape(4 * r_, c_) if k_ in row_sharded
                        else jnp.transpose(g4, (0, 2, 1, 3)).reshape(r_, 4 * c_))


    cwid = conv_w.shape[2]
    sw = -(-max(d, cwid) // 128) * 128
    small_in = jnp.concatenate([jnp.pad(c, ((0, 0), (0, sw - d))), jnp.pad(conv_w[0], ((0, 4), (0, sw - cwid)))], axis=0)
    (small_all,) = _exchange8([small_in], "gather_c")
    cs = small_all[:, 0, :d]
    conv_w_full = jnp.concatenate([small_all[2 * j, 1:4, :cwid] for j in range(4)], axis=1)
    cs16 = jnp.concatenate([cs, c_ctx[None, :], jnp.zeros((7, d), F32)], axis=0)

    csh = w_mod.shape[2]
    b_mod_sh = lax.dynamic_slice(b_mod, (0, chip * csh), (1, csh))

    def mod_fwd_body(c_ref, w_ref, b_ref, o_ref):
        a = _silu(c_ref[...]).astype(MXU_DTYPE)
        o_ref[...] = jnp.dot(a, w_ref[...].astype(MXU_DTYPE), preferred_element_type=F32) + b_ref[...]

    mod_sh = pl.pallas_call(mod_fwd_body, name="mod_fwd", out_shape=_sds((16, csh), F32),
                            compiler_params=pltpu.CompilerParams(vmem_limit_bytes=VMEM_LIMIT))(cs16, w_mod[0], b_mod_sh)
    (mod_all,) = _exchange8([mod_sh], "gather_mod")
    mod_full = jnp.concatenate([mod_all[2 * j] for j in range(4)], axis=1)
    modv = jnp.stack([lax.dynamic_slice(mod_full, (me, 0), (1, d6)), mod_full[8:9]])

    xa = jnp.concatenate([x[0], ctx[0]], axis=0)
    n1g, n2g = norm1_g, norm2_g

    def norm1_body(xv, m, g):
        sh1, sc1 = m[:, :d], m[:, d:2 * d]
        return _rms_fwd(xv, g, d) * (1.0 + sc1) + sh1

    (h1,), early_halves = _rowwise(norm1_body, name="norm1_fwd", nblk=nb, tr=tr, rows=[(xa, 0, d, 0)], sels=[modv],
                                   fulls=[n1g], outs=[(d, MXU_DTYPE)], seg=nlb, side=(halves_in(early_names), True))
    assemble(early_names, early_halves, "gather_weight_halves")

    u_off, kv_off, kr_off = 2 * d, 2 * d + SSM_WIDTH, 2 * d + SSM_WIDTH + KV_LORA
    q_off = -(-(kr_off + SLOT) // Q_LORA) * Q_LORA
    zw = q_off + Q_LORA
    wi = full["w_in"]
    s0, s1, s2, s3 = Q_LORA, Q_LORA + KV_LORA, Q_LORA + KV_LORA + QK_ROPE, Q_LORA + KV_LORA + QK_ROPE + SSM_WIDTH
    zpad = lambda w_: jnp.zeros((d, w_), MXU_DTYPE)
    win_p = jnp.concatenate([wi[:, s3:], wi[:, s2:s3], wi[:, s0:s1], wi[:, s1:s2], zpad(SLOT - QK_ROPE),
                             zpad(q_off - kr_off - SLOT), wi[:, :s0]], axis=1)
    wuq_p = jnp.pad(full["w_uq"].reshape(Q_LORA, N_HEADS, QK_DIM), ((0, 0), (0, 0), (0, SLOT - QK_DIM))).reshape(Q_LORA, N_HEADS * SLOT)
    wukv3 = full["w_ukv"].reshape(KV_LORA, N_HEADS, QK_NOPE + V_DIM)
    padh = lambda t: jnp.pad(t, ((0, 0), (0, 0), (0, SLOT - t.shape[2]))).reshape(t.shape[0], N_HEADS * SLOT)
    wukv_p = jnp.concatenate([padh(wukv3[:, :, :QK_NOPE]), padh(wukv3[:, :, QK_NOPE:])], axis=1)
    hw = N_HEADS * SLOT
    gain_p = lambda g_: jnp.tile(jnp.pad(g_[0], (0, SLOT - QK_DIM)), N_HEADS)[None, :]
    qg_p, kg_p = gain_p(q_norm_g), gain_p(k_norm_g)

    tok = jnp.arange(nl)
    freqs = ROPE_THETA ** (-jnp.arange(QK_ROPE // 4, dtype=F32) / (QK_ROPE // 4))
    ang = jnp.concatenate([(tok // GRID_W)[:, None] * freqs, (tok % GRID_W)[:, None] * freqs], axis=-1)
    cos_t = jnp.concatenate([jnp.cos(ang), jnp.ones((nc, 16), F32)], axis=0)
    sin_t = jnp.concatenate([jnp.sin(ang), jnp.zeros((nc, 16), F32)], axis=0)
    zl = lambda w_: jnp.zeros((n, w_), F32)
    rope_c = jnp.concatenate([jnp.ones((n, QK_NOPE), F32), cos_t, cos_t, zl(SLOT - QK_DIM)], axis=1)
    rope_sa = jnp.concatenate([zl(QK_NOPE), -sin_t, zl(SLOT - QK_NOPE - 16)], axis=1)
    rope_sb = jnp.concatenate([zl(QK_NOPE + 16), sin_t, zl(SLOT - QK_DIM)], axis=1)

    dirs = (("f", lam_re_f, lam_im_f, log_dt_f, c_re_f, c_im_f, False), ("b", lam_re_b, lam_im_b, log_dt_b, c_re_b, c_im_b, True))
    bbd, cbd_t, cbd_n, bbd_t, lamc, lamc_adj, disc_vjps = [], [], [], [], [], [], []
    for _, l_re, l_im, l_dt, cr_, ci_, rev_ in dirs:
        (lbr, lbi, bbr, bbi), vjp = jax.vjp(_s5_disc, l_re[0], l_im[0], l_dt[0], b_re[0], b_im[0])
        disc_vjps.append(vjp)
        bb = _diag_blocks(jnp.transpose(bbr, (0, 2, 1)), jnp.transpose(bbi, (0, 2, 1))).astype(MXU_DTYPE)
        cc_ = _diag_blocks(cr_[0], -ci_[0]).astype(MXU_DTYPE)
        bbd.append(bb)
        bbd_t.append(_block_transpose(bb))
        cbd_t.append(cc_)
        cbd_n.append(_block_transpose(cc_))
        lamc.append(_lam_consts(lbr, lbi, rev_, False))
        lamc_adj.append(_lam_consts(lbr, lbi, not rev_, True))
    t_scan = tr

    z = _mm(h1, win_p, "nn", "in_proj")
    gl_cb, u_cb, kv_cb, kr_cb, q_cb = 0, u_off // SSM_WIDTH, kv_off // KV_LORA, kr_off // SLOT, q_off // Q_LORA

    (cqn,) = _rowwise(lambda v, g: _rms_fwd(v, g, Q_LORA), name="qa_norm_fwd", nblk=nlb, tr=tr,
                      rows=[(z, q_cb, Q_LORA, 0)], fulls=[q_a_g], outs=[(Q_LORA, MXU_DTYPE)])
    qh = _mm(cqn, wuq_p, "nn", "q_up")

    def qhead_body(qv, cv, sav, sbv, g):
        return jnp.concatenate([_rope_fwd(_rms_fwd(t, g[:, :SLOT], QK_DIM), cv, sav, sbv) for t in _heads(qv)], axis=1)

    rope_rows = lambda: [(rope_c, 0, SLOT, 0), (rope_sa, 0, SLOT, 0), (rope_sb, 0, SLOT, 0)]
    (q_p,) = _rowwise(qhead_body, name="q_head_fwd", nblk=nlb, tr=tr, rows=[(qh, 0, hw, 0)] + rope_rows(),
                      fulls=[qg_p], outs=[(hw, MXU_DTYPE)])

    (ckvn,) = _rowwise(lambda v, g: _rms_fwd(v, g, KV_LORA), name="kva_norm_fwd", nblk=nb, tr=tr,
                       rows=[(z, kv_cb, KV_LORA, 0)], fulls=[kv_a_g], outs=[(KV_LORA, MXU_DTYPE)])
    kvpre = _mm(ckvn, wukv_p, "nn", "kv_up")

    def khead_body(kv_, vv_, krv, cv, sav, sbv, g):
        kpe = pltpu.roll(krv, QK_NOPE, 1)
        ks = [_rope_fwd(_rms_fwd(t + kpe, g[:, :SLOT], QK_DIM), cv, sav, sbv) for t in _heads(kv_)]
        return jnp.concatenate(ks, axis=1), vv_

    k_p, v_p = _rowwise(khead_body, name="k_head_fwd", nblk=nb, tr=tr,
                        rows=[(kvpre, 0, hw, 0), (kvpre, 1, hw, 0), (z, kr_cb, SLOT, 0)] + rope_rows(),
                        fulls=[kg_p], outs=[(hw, MXU_DTYPE), (hw, MXU_DTYPE)])

    scale = QK_DIM ** -0.5
    o_p, lse, ffn_halves = _attn_fwd(q_p, k_p, v_p, nl, scale, side=(halves_in(ffn_names), True))
    assemble(ffn_names, ffn_halves, "gather_ffn_weight_halves")
    wglu, wout, wup, wdown = full["w_glu"], full["w_out"], full["w_up"], full["w_down"]
    wo_p = jnp.pad(full["w_o_attn"].reshape(N_HEADS, V_DIM, d), ((0, 0), (0, SLOT - V_DIM), (0, 0))).reshape(N_HEADS * SLOT, d)
    a_l = _mm(o_p, wo_p, "nn", "attn_out")

    scans = [_s5_scan(z, u_cb, bbd[j], cbd_n[j], lamc[j], t_scan, nl, dirs[j][6], "s5_scan_" + dirs[j][0]) for j in range(2)]
    xs, ydir = [s_[0] for s_ in scans], [s_[1] for s_ in scans]

    def ssm_out_body(uv, a, b, dsk):
        ys = uv * dsk + a + b
        return ys, _gelu(ys)

    ys, ge = _rowwise(ssm_out_body, name="s5_out_fwd", nblk=nlb, tr=tr,
                      rows=[(z, u_cb, SSM_WIDTH, 0), (ydir[0], 0, SSM_WIDTH, 0), (ydir[1], 0, SSM_WIDTH, 0)],
                      fulls=[d_skip], outs=[(SSM_WIDTH, F32), (SSM_WIDTH, MXU_DTYPE)])
    glu_out = _mm(ge, wglu, "nn", "glu_proj")

    def merge_body(ga, gs, av, val, gate):
        return _sigmoid(ga) * av + _sigmoid(gs) * (val * _sigmoid(gate))

    merge_rows = lambda: [(z, 0, d, 0), (z, 1, d, 0), (a_l, 0, d, 0), (glu_out, 0, d, 0), (glu_out, 1, d, 0)]
    (merged,) = _rowwise(merge_body, name="merge_fwd", nblk=nlb, tr=tr, rows=merge_rows(), outs=[(d, MXU_DTYPE)])
    mo = _mm(merged, wout, "nn", "out_proj")
    mod_x = modv[0]

    def norm2_body(xv, mov, m, g):
        g1, sh2, sc2 = m[:, 2 * d:3 * d], m[:, 3 * d:4 * d], m[:, 4 * d:5 * d]
        x1v = xv + g1 * mov
        return x1v, _rms_fwd(x1v, g, d) * (1.0 + sc2) + sh2

    x1, h2 = _rowwise(norm2_body, name="norm2_fwd", nblk=nlb, tr=tr, rows=[(xa, 0, d, 0), (mo, 0, d, 0)],
                      fulls=[mod_x, n2g], outs=[(d, F32), (d, MXU_DTYPE)])
    up = _mm(h2, wup, "nn", "ffn_up")
    cw8 = jnp.zeros((8, f2), F32).at[:3].set(conv_w_full)

    def conv3(t3, w8, off):
        p_, c_, n_ = t3
        return p_ * w8[0:1, off:off + fh] + c_ * w8[1:2, off:off + fh] + n_ * w8[2:3, off:off + fh]

    def conv_fwd_body(val3, gate3, w8, bias):
        val2 = conv3(val3, w8, 0) + bias[:, :fh]
        gate2 = conv3(gate3, w8, fh) + bias[:, fh:]
        return _silu(gate2) * val2

    (act,) = _rowwise(conv_fwd_body, name="conv_fwd", nblk=nlb, tr=tr, rows=[(up, 0, fh, 0), (up, 1, fh, 0)],
                      halo=(0, 1), fulls=[cw8, conv_b], outs=[(fh, MXU_DTYPE)])
    dn = _mm(act, wdown, "nn", "ffn_down")
    tgt = loss_target[0]

    def loss_body(x1v, dnv, tv, m):
        g2 = m[:, 5 * d:6 * d]
        e = x1v + g2 * dnv - tv
        dx2v = e * (1.0 / d)
        return dx2v, dx2v * g2, e * e, dx2v * dnv

    dx2, ddn, loss_acc, dg2_acc = _rowwise(loss_body, name="loss", nblk=nlb, tr=tr,
                                           rows=[(x1, 0, d, 0), (dn, 0, d, 0), (tgt, 0, d, 0)], fulls=[mod_x],
                                           outs=[(d, F32), (d, MXU_DTYPE)], accs=[d, d])
    loss = lax.psum(0.5 / d * jnp.sum(loss_acc), ("x", "y", "c"))

    g_big = {}

    def chip_partials(names_, name):
        pcs = []
        for k_ in names_:
            r_, c_ = weights[k_].shape[1:]
            if k_ in row_sharded:
                p4 = jnp.transpose(g_big[k_].reshape(4, 2, r_ // 2, c_), (1, 0, 2, 3))
            else:
                p4 = jnp.transpose(g_big[k_].reshape(2, r_ // 2, 4, c_), (0, 2, 1, 3))
            pcs.append(p4.reshape(2, 2 * r_, c_))
        out = []
        for k_, p_, got in zip(names_, pcs, _sibling_send(pcs, name)):
            rows4, c_ = got.shape
            own = lax.dynamic_index_in_dim(p_, mc, 0, keepdims=False)
            tr_ = _pick(rows4, (256, 128, 64, 32, 16))
            s32, sb = _rowwise(lambda a, b: (a + b, a + b), name="sum_chip_" + k_, nblk=rows4 // tr_, tr=tr_,
                               rows=[(own, 0, c_, 0), (got, 0, c_, 0)], outs=[(c_, F32), (c_, MXU_DTYPE)])
            out.append((s32, sb, rows4 // 4, c_))
        return out

    def scatter_in(partials):
        return [sb.reshape(4, rh, c_) for _, sb, rh, c_ in partials]

    def chip_sums(names_, partials, recv3):
        out = []
        for k_, (s32, _, rh, c_), r3 in zip(names_, partials, recv3):
            mine = lax.dynamic_slice(s32, (chip * rh, 0), (rh, c_))
            tr_ = _pick(rh, (256, 128, 64, 32, 16))
            (red,) = _rowwise(lambda a, b0, b1, b2: a + b0 + b1 + b2, name="sum_grad_" + k_, nblk=rh // tr_, tr=tr_,
                              rows=[(mine, 0, c_, 0)] + [(r3.reshape(3 * rh, c_), 0, c_, j * rh) for j in range(3)],
                              outs=[(c_, F32)])
            out.append(red)
        return out

    dact = _mm(ddn, wdown, "nt", "ffn_down_dx")
    g_big["w_down"] = _mm(act, ddn, "tn", "ffn_down_dw")

    def conv_bwd_body(val3, gate3, da, w8, bias):
        val2 = conv3(val3, w8, 0) + bias[:, :fh]
        gate2 = conv3(gate3, w8, fh) + bias[:, fh:]
        sg = _sigmoid(gate2)
        dval2 = da * (gate2 * sg)
        dgate2 = da * val2 * (sg * (1.0 + gate2 * (1.0 - sg)))
        du2 = jnp.concatenate([dval2, dgate2], axis=1)
        taps = [jnp.concatenate([dval2 * val3[j], dgate2 * gate3[j]], axis=1) for j in range(3)]
        return du2, du2, taps[0], taps[1], taps[2]

    du2, dcb_acc, dcw0, dcw1, dcw2 = _rowwise(conv_bwd_body, name="conv_bwd", nblk=nlb, tr=tr,
                                              rows=[(up, 0, fh, 0), (up, 1, fh, 0), (dact, 0, fh, 0)], halo=(0, 1),
                                              fulls=[cw8, conv_b], outs=[(f2, F32)], accs=[f2, f2, f2, f2])

    def conv_t_body(dval3, dgate3, w8):
        rev = lambda t3: (t3[2], t3[1], t3[0])
        return jnp.concatenate([conv3(rev(dval3), w8, 0), conv3(rev(dgate3), w8, fh)], axis=1)

    (dup,) = _rowwise(conv_t_body, name="conv_bwd_dx", nblk=nlb, tr=tr, rows=[(du2, 0, fh, 0), (du2, 1, fh, 0)],
                      halo=(0, 1), fulls=[cw8], outs=[(f2, MXU_DTYPE)])
    dh2 = _mm(dup, wup, "nt", "ffn_up_dx")
    g_big["w_up"] = _mm(h2, dup, "tn", "ffn_up_dw")

    def norm2_bwd_body(x1v, dh, dx2v, mov, m, g):
        g1, sc2 = m[:, 2 * d:3 * d], m[:, 4 * d:5 * d]
        y = _rms_fwd(x1v, g, d)
        dxn, dgc = _rms_bwd(x1v, g, dh * (1.0 + sc2), d)
        dx1v = dx2v + dxn
        return dx1v, dx1v * g1, dgc, dh, dh * y, dx1v * mov

    dx1, dmo, dn2g_acc, dsh2_acc, dsc2_acc, dg1_acc = _rowwise(
        norm2_bwd_body, name="norm2_bwd", nblk=nlb, tr=tr,
        rows=[(x1, 0, d, 0), (dh2, 0, d, 0), (dx2, 0, d, 0), (mo, 0, d, 0)], fulls=[mod_x, n2g],
        outs=[(d, F32), (d, MXU_DTYPE)], accs=[d, d, d, d])
    dmerged = _mm(dmo, wout, "nt", "out_proj_dx")
    g_big["w_out"] = _mm(merged, dmo, "tn", "out_proj_dw")

    def merge_bwd_body(ga, gs, av, val, gate, dm):
        sa_, ss_, sg_ = _sigmoid(ga), _sigmoid(gs), _sigmoid(gate)
        s_l = val * sg_
        ds_l = dm * ss_
        dga = dm * av * sa_ * (1.0 - sa_)
        dgs = dm * s_l * ss_ * (1.0 - ss_)
        dval = ds_l * sg_
        dgate = ds_l * val * sg_ * (1.0 - sg_)
        return dm * sa_, jnp.concatenate([dval, dgate], axis=1), jnp.concatenate([dga, dgs], axis=1)

    da_l, dglu, dgl = _rowwise(merge_bwd_body, name="merge_bwd", nblk=nlb, tr=tr,
                               rows=merge_rows() + [(dmerged, 0, d, 0)],
                               outs=[(d, MXU_DTYPE), (2 * d, MXU_DTYPE), (2 * d, MXU_DTYPE)])
    dge = _mm(dglu, wglu, "nt", "glu_proj_dx")
    g_big["w_glu"] = _mm(ge, dglu, "tn", "glu_proj_dw")

    def ssm_out_bwd_body(ysv, dgev, uv, dsk):
        dys_ = dgev * _dgelu(ysv)
        return dys_, dys_ * dsk, dys_ * uv

    dys, du_skip, ddskip_acc = _rowwise(ssm_out_bwd_body, name="s5_out_bwd", nblk=nlb, tr=tr,
                                        rows=[(ys, 0, SSM_WIDTH, 0), (dge, 0, SSM_WIDTH, 0), (z, u_cb, SSM_WIDTH, 0)],
                                        fulls=[d_skip], outs=[(SSM_WIDTH, F32), (SSM_WIDTH, F32)], accs=[SSM_WIDTH])
    s5b = [_s5_bwd(dys, z, u_cb, xs[j], cbd_t[j], bbd_t[j], lamc_adj[j], t_scan, nl, dirs[j][6], "s5_bwd_" + dirs[j][0])
           for j in range(2)]
    du_nat = s5b[0][0] + s5b[1][0] + jnp.concatenate([du_skip, jnp.zeros((nc, SSM_WIDTH), F32)], axis=0)

    do_f = _mm(da_l, wo_p, "nt", "attn_out_dx")
    g_wo_p = _mm(o_p, da_l, "tn", "attn_out_dw")
    g_big["w_o_attn"] = g_wo_p.reshape(N_HEADS, SLOT, d)[:, :V_DIM].reshape(N_HEADS * V_DIM, d)
    ffn_partials = chip_partials(ffn_names, "ffn_grads_to_sibling")

    dq_p, dk_p, dv_p, ffn_recv = _attn_bwd(q_p, k_p, v_p, do_f, o_p, lse, nl, scale, side=(scatter_in(ffn_partials), False))

    def qhead_bwd_body(qv, dqv, cv, sav, sbv, g):
        dxs, dgs = [], []
        for t, dt_ in zip(_heads(qv), _heads(dqv)):
            dx_, dg_ = _rms_bwd(t, g[:, :SLOT], _rope_bwd(dt_, cv, sav, sbv), QK_DIM)
            dxs.append(dx_)
            dgs.append(dg_)
        return jnp.concatenate(dxs, axis=1), jnp.concatenate(dgs, axis=1)

    dqh, dqg_acc = _rowwise(qhead_bwd_body, name="q_head_bwd", nblk=nlb, tr=tr,
                            rows=[(qh, 0, hw, 0), (dq_p, 0, hw, 0)] + rope_rows(), fulls=[qg_p],
                            outs=[(hw, MXU_DTYPE)], accs=[hw])
    dcqn = _mm(dqh, wuq_p, "nt", "q_up_dx")
    g_wuq_p = _mm(cqn, dqh, "tn", "q_up_dw")
    dcq, dqag_acc = _rowwise(lambda v, dy, g: _rms_bwd(v, g, dy, Q_LORA), name="qa_norm_bwd", nblk=nlb, tr=tr,
                             rows=[(z, q_cb, Q_LORA, 0), (dcqn, 0, Q_LORA, 0)], fulls=[q_a_g],
                             outs=[(Q_LORA, MXU_DTYPE)], accs=[Q_LORA])

    def khead_bwd_body(kv_, krv, dkv_, dvv_, cv, sav, sbv, g):
        kpe = pltpu.roll(krv, QK_NOPE, 1)
        lane = lax.broadcasted_iota(jnp.int32, krv.shape, 1)
        dxs, dgs, dkr_ = [], [], jnp.zeros(krv.shape, F32)
        for t, dt_ in zip(_heads(kv_), _heads(dkv_)):
            dx_, dg_ = _rms_bwd(t + kpe, g[:, :SLOT], _rope_bwd(dt_, cv, sav, sbv), QK_DIM)
            dxs.append(jnp.where(lane < QK_NOPE, dx_, 0.0))
            dgs.append(dg_)
            dkr_ = dkr_ + dx_
        dkr_ = jnp.where(lane < QK_ROPE, pltpu.roll(dkr_, SLOT - QK_NOPE, 1), 0.0)
        return jnp.concatenate(dxs + [dvv_], axis=1), dkr_, jnp.concatenate(dgs, axis=1)

    red8 = lambda a: jnp.sum(a, axis=-2)
    head_fold = lambda acc: jnp.sum(red8(acc).reshape(N_HEADS, SLOT), axis=0)[:QK_DIM]
    small_g = {}
    for j, dr in enumerate(dirs):
        sfx = dr[0]
        _, dbbd_j, dcbd_j, dlam_j = s5b[j]
        dl = red8(dlam_j).reshape(N_CG, 2, CG_STATES)
        db_re, db_im = _diag_extract(dbbd_j)
        cot = (dl[:, 0].reshape(SSM_GROUPS, SSM_STATE), dl[:, 1].reshape(SSM_GROUPS, SSM_STATE),
               jnp.transpose(db_re, (0, 2, 1)), jnp.transpose(db_im, (0, 2, 1)))
        g_lre, g_lim, g_ldt, g_bre, g_bim = disc_vjps[j](cot)
        small_g["lam_re_" + sfx], small_g["lam_im_" + sfx], small_g["log_dt_" + sfx] = g_lre, g_lim, g_ldt
        small_g["b_re"] = small_g.get("b_re", 0.0) + g_bre
        small_g["b_im"] = small_g.get("b_im", 0.0) + g_bim
        dc_re, dc_im = _diag_extract(dcbd_j)
        small_g["c_re_" + sfx], small_g["c_im_" + sfx] = dc_re, -dc_im
    small_g.update(norm2_g=red8(dn2g_acc), q_a_g=red8(dqag_acc), q_norm_g=head_fold(dqg_acc), d_skip=red8(ddskip_acc),
                   conv_b=red8(dcb_acc))
    g_convw_full = jnp.stack([red8(dcw0), red8(dcw1), red8(dcw2)])
    late_small = ["c_ctx", "b_mod", "norm1_g", "kv_a_g", "k_norm_g"]
    first_small = ["norm2_g", "q_a_g", "q_norm_g", "lam_re_f", "lam_im_f", "log_dt_f", "c_re_f", "c_im_f", "lam_re_b",
                   "lam_im_b", "log_dt_b", "c_re_b", "c_im_b", "b_re", "b_im", "d_skip", "conv_b"]
    first_pack = _pack([small_g[k] for k in first_small] + [g_convw_full], rows_mult=8)

    (dkvpre, dkr, dkg_acc), (first_all,) = _rowwise(
        khead_bwd_body, name="k_head_bwd", nblk=nb, tr=tr,
        rows=[(kvpre, 0, hw, 0), (z, kr_cb, SLOT, 0), (dk_p, 0, hw, 0), (dv_p, 0, hw, 0)] + rope_rows(),
        fulls=[kg_p], outs=[(2 * hw, MXU_DTYPE), (SLOT, MXU_DTYPE)], accs=[hw], side=([first_pack], ALL8))
    dckvn = _mm(dkvpre, wukv_p, "nt", "kv_up_dx")
    g_wukv_p = _mm(ckvn, dkvpre, "tn", "kv_up_dw")
    dckv, dkvag_acc = _rowwise(lambda v, dy, g: _rms_bwd(v, g, dy, KV_LORA), name="kva_norm_bwd", nblk=nb, tr=tr,
                               rows=[(z, kv_cb, KV_LORA, 0), (dckvn, 0, KV_LORA, 0)], fulls=[kv_a_g],
                               outs=[(KV_LORA, MXU_DTYPE)], accs=[KV_LORA])

    padc = lambda t: jnp.concatenate([t, jnp.zeros((nc, t.shape[1]), t.dtype)], axis=0)
    dz = jnp.concatenate([padc(dgl), du_nat.astype(MXU_DTYPE), dckv, dkr,
                          jnp.zeros((n, q_off - kr_off - SLOT), MXU_DTYPE), padc(dcq)], axis=1)
    gwi = _mm(h1, dz, "tn", "in_proj_dw")
    g_big["w_in"] = jnp.concatenate([gwi[:, q_off:q_off + Q_LORA], gwi[:, kv_off:kv_off + KV_LORA],
                                     gwi[:, kr_off:kr_off + QK_ROPE], gwi[:, u_off:u_off + SSM_WIDTH], gwi[:, :2 * d]], axis=1)
    g_big["w_uq"] = g_wuq_p.reshape(Q_LORA, N_HEADS, SLOT)[:, :, :QK_DIM].reshape(Q_LORA, N_HEADS * QK_DIM)
    gk3 = g_wukv_p[:, :hw].reshape(KV_LORA, N_HEADS, SLOT)[:, :, :QK_NOPE]
    gv3 = g_wukv_p[:, hw:].reshape(KV_LORA, N_HEADS, SLOT)[:, :, :V_DIM]
    g_big["w_ukv"] = jnp.concatenate([gk3, gv3], axis=2).reshape(KV_LORA, N_HEADS * (QK_NOPE + V_DIM))
    early_partials = chip_partials(early_names, "grads_to_sibling")
    dh1 = _mm(dz, win_p, "nt", "in_proj_dx")

    def norm1_bwd_body(xv, dh, dx1v, m, g):
        sc1 = m[:, d:2 * d]
        y = _rms_fwd(xv, g, d)
        dxn, dgc = _rms_bwd(xv, g, dh * (1.0 + sc1), d)
        return dxn + dx1v, dgc, dh, dh * y

    (dxa, dn1g_acc, dsh1_acc, dsc1_acc), early_recv = _rowwise(
        norm1_bwd_body, name="norm1_bwd", nblk=nb, tr=tr, rows=[(xa, 0, d, 0), (dh1, 0, d, 0), (dx1, 0, d, 0)],
        sels=[modv], fulls=[n1g], outs=[(d, F32)], accs=[d, d, d], seg=nlb, side=(scatter_in(early_partials), False))
    grad_x = dxa[:nl][None]

    dmod_own =jnp.concatenate([red8(dsh1_acc[0]), red8(dsc1_acc[0]), red8(dg1_acc), red8(dsh2_acc), red8(dsc2_acc), red8(dg2_acc)])
    dmod_ctx = jnp.concatenate([red8(dsh1_acc[1]), red8(dsc1_acc[1]), jnp.zeros((4 * d,), F32)])
    dm_in = jnp.concatenate([dmod_own[None, :], dmod_ctx[None, :], jnp.zeros((6, d6), F32)], axis=0)
    (dm_all,) = _exchange8([dm_in], "gather_dmod")
    dm_own_sh = lax.dynamic_slice(dm_all[:, 0, :], (0, chip * csh), (8, csh))
    dm_ctx_sh = lax.dynamic_slice(dm_all[:, 1, :], (0, chip * csh), (8, csh))

    def mod_bwd_body(c_ref, own_ref, ctx_ref, w_ref, gw_ref, gb_ref, gc_ref):
        cv = c_ref[...]
        a = _silu(cv).astype(MXU_DTYPE)
        own = own_ref[...]
        ctx_tot = ctx_ref[0:1, :]
        for j in range(1, 8):
            ctx_tot = ctx_tot + ctx_ref[j:j + 1, :]
        g16 = jnp.concatenate([own, jnp.broadcast_to(ctx_tot, own.shape)], axis=0)
        rid = lax.broadcasted_iota(jnp.int32, g16.shape, 0)
        g16 = jnp.where(rid <= 8, g16, 0.0)
        gw_ref[...] = lax.dot_general(a, g16.astype(MXU_DTYPE), (((0,), (0,)), ((), ())), preferred_element_type=F32)
        gb_ref[...] = jnp.broadcast_to(jnp.sum(own, axis=0, keepdims=True) + ctx_tot, gb_ref.shape)
        gc = lax.dot_general(jnp.broadcast_to(ctx_tot, own.shape).astype(MXU_DTYPE), w_ref[...].astype(MXU_DTYPE),
                             (((1,), (1,)), ((), ())), preferred_element_type=F32)
        gc_ref[...] = gc * _dsilu(cv[8:9, :])

    g_wmod, g_bmod_sh, g_cctx_part = pl.pallas_call(
        mod_bwd_body, name="mod_bwd", out_shape=[_sds((d, csh), F32), _sds((8, csh), F32), _sds((8, d), F32)],
        compiler_params=pltpu.CompilerParams(vmem_limit_bytes=VMEM_LIMIT))(cs16, dm_own_sh, dm_ctx_sh, w_mod[0])
    north = (mc == 0).astype(F32)
    g_bmod_part = lax.dynamic_update_slice(jnp.zeros((1, d6), F32), g_bmod_sh[0:1] * north, (0, chip * csh))
    g_cctx_part = g_cctx_part[0] * north

    small_g.update(c_ctx=g_cctx_part, b_mod=g_bmod_part[0], norm1_g=red8(dn1g_acc[0]) + red8(dn1g_acc[1]),
                   kv_a_g=red8(dkvag_acc), k_norm_g=head_fold(dkg_acc))
    (late_all,) = _exchange8([_pack([small_g[k] for k in late_small], rows_mult=8)], "gather_small_grads")
    sg_first = _unpack(_sum8(first_all, "sum_small_grads"), [weights[k].shape for k in first_small] + [(3, f2)])
    sg_late = _unpack(_sum8(late_all, "sum_last_small_grads"), [weights[k].shape for k in late_small])
    g_small = dict(zip(first_small + late_small, sg_first[:-1] + sg_late))
    g_small["conv_w"] = lax.dynamic_slice(sg_first[-1], (0, chip * cwid), (3, cwid))[None]
    small_names = first_small + late_small

    reduced = dict(zip(early_names + ffn_names, chip_sums(early_names, early_partials, early_recv)
                       + chip_sums(ffn_names, ffn_partials, ffn_recv)))
    both = _sibling_exchange([reduced[k_] for k_ in big_names], "exchange_halves")
    g_sh = {k_: b_.reshape((1,) + weights[k_].shape[1:]) for k_, b_ in zip(big_names, both)}
    g_sh["w_mod"] = g_wmod[None]

    grads = {**g_sh, **g_small}
    outs_d, outs_m, outs_v = {}, {}, {}
    for k_ in ["w_mod"] + big_names:
        shp = weights[k_].shape
        res = _adamw(*[t.reshape(shp[1:]) for t in (grads[k_], weights[k_], mom_m[k_], mom_v[k_])], "adamw_" + k_)
        for dst, buf in zip((outs_d, outs_m, outs_v), res):
            dst[k_] = buf.reshape(shp)
    adam_small = small_names + ["conv_w"]
    shapes = [weights[k_].shape for k_ in adam_small]
    res = _adamw(*[_pack([src[k_] for k_ in adam_small], rows_mult=8) for src in (grads, weights, mom_m, mom_v)], "adamw_small")
    for dst, buf in zip((outs_d, outs_m, outs_v), res):
        dst.update(zip(adam_small, _unpack(buf, shapes)))
    grads = {k_: grads[k_].reshape(weights[k_].shape) for k_ in names}
    return (loss, grad_x, *[grads[k_] for k_ in names], *[outs_d[k_] for k_ in names],
            *[outs_m[k_] for k_ in names], *[outs_v[k_] for k_ in names])
```

```python
import math

import numpy as np
import jax
import jax.numpy as jnp
from jax import lax
from jax.experimental import pallas as pl
from jax.experimental.pallas import tpu as pltpu

F32 = jnp.float32
MXU_DTYPE = jnp.bfloat16
MESH = pl.DeviceIdType.MESH

EPS = 1e-6
N_HEADS = 8
QK_NOPE = 64
QK_ROPE = 32
QK_DIM = QK_NOPE + QK_ROPE
V_DIM = 64
SLOT = 128
Q_LORA = 384
KV_LORA = 256
GRID_W = 64
ROPE_THETA = 10000.0
SSM_WIDTH = 512
SSM_GROUP = 16
SSM_GROUPS = 32
SSM_STATE = 64
N_STATE = SSM_GROUPS * SSM_STATE
CG_STATES = 512
N_CG = N_STATE // CG_STATES
CG_CHANNELS = SSM_WIDTH // N_CG
SCAN_LANES = 512
PACK_W = 1024

ADAM_LR = 0.001
ADAM_B1 = 0.9
ADAM_B2 = 0.999
ADAM_EPS = 1e-08
ADAM_WD = 0.01
ADAM_STEP = 10

VMEM_LIMIT = 56 * 1024 * 1024
LOG2E = 1.4426950408889634


def _pick(n, cands):
    for c in cands:
        if c <= n and n % c == 0:
            return c
    return n


def _cparams(sem):
    return pltpu.CompilerParams(dimension_semantics=sem, vmem_limit_bytes=VMEM_LIMIT)


def _sds(shape, dtype):
    return jax.ShapeDtypeStruct(tuple(shape), dtype)


_K_CANDS = (2816, 2048, 1536, 1408, 1280, 1152, 1024, 896, 768, 704, 640, 512, 384, 256, 128, 64, 32, 16)
_M_CANDS = (2048, 1408, 1024, 768, 512, 384, 256, 128, 64, 32, 16)
_N_CANDS = (1408, 1152, 1024, 768, 512, 384, 256, 128)
MM_VMEM_BUDGET = 40 * 1024 * 1024


def _mm_tiles(m, n, k_opts, a_bytes, b_bytes, o_bytes, m_cands):
    tn = n if n <= _N_CANDS[0] else _pick(n, _N_CANDS)
    for tk in k_opts:
        for tm in ((m,) if m <= m_cands[0] else ()) + tuple(t for t in m_cands if t < m and m % t == 0):
            if 2 * (tm * tk * a_bytes + tk * tn * b_bytes + tm * tn * o_bytes) + tm * tn * 4 <= MM_VMEM_BUDGET:
                return tm, tn, tk
    raise ValueError("no matmul tiling fits")


def _mm(a, b, mode, name, out_dtype=F32, rows=None, a_off=0, b_off=0):
    a_bytes, b_bytes, o_bytes = a.dtype.itemsize, b.dtype.itemsize, jnp.dtype(out_dtype).itemsize
    if mode == "tn":
        t_rows = rows or a.shape[0]
        m, n = a.shape[1], b.shape[1]
        k_opts = tuple(t for t in _K_CANDS if t <= t_rows and t_rows % t == 0) or (t_rows,)
        tm, tn, tk = _mm_tiles(m, n, k_opts, a_bytes, b_bytes, o_bytes, _M_CANDS[1:])
        nk = t_rows // tk
        ao, bo = a_off // tk, b_off // tk
        grid = (m // tm, n // tn, nk)
        in_specs = [pl.BlockSpec((tk, tm), lambda i, j, k: (k + ao, i)),
                    pl.BlockSpec((tk, tn), lambda i, j, k: (k + bo, j))]
        dn = (((0,), (0,)), ((), ()))
    else:
        m = rows or a.shape[0]
        kdim = a.shape[1]
        n = b.shape[1] if mode == "nn" else b.shape[0]
        k_opts = (kdim,) + tuple(t for t in _K_CANDS if t < kdim and kdim % t == 0)
        tm, tn, tk = _mm_tiles(m, n, k_opts, a_bytes, b_bytes, o_bytes, _M_CANDS)
        nk = kdim // tk
        ao = a_off // tm
        grid = (m // tm, n // tn, nk)
        if mode == "nn":
            in_specs = [pl.BlockSpec((tm, tk), lambda i, j, k: (i + ao, k)),
                        pl.BlockSpec((tk, tn), lambda i, j, k: (k, j))]
            dn = (((1,), (0,)), ((), ()))
        else:
            in_specs = [pl.BlockSpec((tm, tk), lambda i, j, k: (i + ao, k)),
                        pl.BlockSpec((tn, tk), lambda i, j, k: (j, k))]
            dn = (((1,), (1,)), ((), ()))
    use_scratch = nk > 1 and out_dtype != F32

    def body(a_ref, b_ref, o_ref, *scr):
        r = lax.dot_general(a_ref[...].astype(MXU_DTYPE), b_ref[...].astype(MXU_DTYPE), dn,
                            preferred_element_type=F32)
        if nk == 1:
            o_ref[...] = r.astype(o_ref.dtype)
        else:
            k = pl.program_id(2)
            acc = scr[0] if use_scratch else o_ref

            @pl.when(k == 0)
            def _():
                acc[...] = r

            @pl.when(k > 0)
            def _():
                acc[...] += r

            if use_scratch:
                @pl.when(k == nk - 1)
                def _():
                    o_ref[...] = acc[...].astype(o_ref.dtype)

    return pl.pallas_call(
        body, name=name, grid=grid, in_specs=in_specs,
        out_specs=pl.BlockSpec((tm, tn), lambda i, j, k: (i, j)),
        out_shape=_sds((m, n), out_dtype),
        scratch_shapes=[pltpu.VMEM((tm, tn), F32)] if use_scratch else [],
        compiler_params=_cparams(("parallel", "parallel", "arbitrary")),
    )(a, b)


def _rowwise(body, *, name, nblk, tr, rows=(), halo=(), sels=(), fulls=(), outs=(), accs=(), seg=None, side=None):
    n_rows, n_sel, n_full, n_out, n_acc = len(rows), len(sels), len(fulls), len(outs), len(accs)
    halo = tuple(halo)
    maxw = max([r[2] for r in rows] + [o[0] for o in outs] + list(accs))
    sr = _pick(tr, tuple(s for s in (256, 128, 64, 32, 16) if s * maxw <= 131072) or (16,))
    nsub = tr // sr
    total8 = nblk * tr // 8

    def seg_of(i):
        return jnp.where(i >= seg, 1, 0) if seg is not None else 0

    in_specs, operands = [], []
    for arr, cb, w, roff in rows:
        ob = roff // tr
        last = arr.shape[0] // tr - 1
        in_specs.append(pl.BlockSpec((tr, w), lambda i, cb=cb, ob=ob, last=last: (jnp.minimum(i + ob, last), cb)))
        operands.append(arr)
    for h in halo:
        arr, cb, w, roff = rows[h]
        o8, t8 = roff // 8, tr // 8
        in_specs.append(pl.BlockSpec((8, w), lambda i, cb=cb, o8=o8, t8=t8: (jnp.maximum(i * t8 - 1, 0) + o8, cb)))
        in_specs.append(pl.BlockSpec((8, w), lambda i, cb=cb, o8=o8, t8=t8: (jnp.minimum((i + 1) * t8, total8 - 1) + o8, cb)))
        operands += [arr, arr]
    for arr in sels:
        in_specs.append(pl.BlockSpec((None,) + arr.shape[1:], lambda i: (seg_of(i), 0, 0)))
        operands.append(arr)
    for arr in fulls:
        in_specs.append(pl.BlockSpec(arr.shape, lambda i: (0, 0)))
        operands.append(arr)
    out_specs, out_shape = [], []
    for w, dt in outs:
        out_specs.append(pl.BlockSpec((tr, w), lambda i: (i, 0)))
        out_shape.append(_sds((nblk * tr, w), dt))
    for w in accs:
        if seg is None:
            out_specs.append(pl.BlockSpec((8, w), lambda i: (0, 0)))
            out_shape.append(_sds((8, w), F32))
        else:
            out_specs.append(pl.BlockSpec((None, 8, w), lambda i: (seg_of(i), 0, 0)))
            out_shape.append(_sds((2, 8, w), F32))
    n_halo = 2 * len(halo)

    def kern(*refs):
        row_refs = refs[:n_rows]
        halo_refs = refs[n_rows:n_rows + n_halo]
        sel_refs = refs[n_rows + n_halo:n_rows + n_halo + n_sel]
        full_refs = refs[n_rows + n_halo + n_sel:n_rows + n_halo + n_sel + n_full]
        o0 = n_rows + n_halo + n_sel + n_full
        out_refs = refs[o0:o0 + n_out]
        acc_refs = refs[o0 + n_out:o0 + n_out + n_acc]
        i = pl.program_id(0)
        if n_acc:
            first = (i == 0) if seg is None else ((i == 0) | (i == seg))

            @pl.when(first)
            def _():
                for a_ref in acc_refs:
                    a_ref[...] = jnp.zeros(a_ref.shape, F32)

        def sub(s, carry):
            r0 = pl.multiple_of(s * sr, sr)
            vals = []
            for idx, r in enumerate(row_refs):
                cur = r[pl.ds(r0, sr), :]
                if idx in halo:
                    hp = halo_refs[2 * halo.index(idx)]
                    hn = halo_refs[2 * halo.index(idx) + 1]
                    cur = cur.astype(F32)
                    rid = lax.broadcasted_iota(jnp.int32, cur.shape, 0)
                    lo = r[pl.ds(pl.multiple_of(jnp.maximum(r0 - 8, 0), 8), 8), :].astype(F32)
                    lo = jnp.where(s == 0, hp[...].astype(F32), lo)
                    lo = jnp.where((s == 0) & (i == 0), 0.0, lo)
                    hi = r[pl.ds(pl.multiple_of(jnp.minimum(r0 + sr, tr - 8), 8), 8), :].astype(F32)
                    hi = jnp.where(s == nsub - 1, hn[...].astype(F32), hi)
                    hi = jnp.where((s == nsub - 1) & (i == nblk - 1), 0.0, hi)
                    prev = jnp.where(rid == 0, jnp.broadcast_to(lo[7:8, :], cur.shape), pltpu.roll(cur, 1, 0))
                    nxt = jnp.where(rid == sr - 1, jnp.broadcast_to(hi[0:1, :], cur.shape), pltpu.roll(cur, sr - 1, 0))
                    vals.append((prev, cur, nxt))
                else:
                    vals.append(cur)
            res = body(*vals, *[r[...] for r in sel_refs], *[r[...] for r in full_refs])
            if not isinstance(res, (tuple, list)):
                res = (res,)
            for o_ref, v in zip(out_refs, res[:n_out]):
                o_ref[pl.ds(r0, sr), :] = v.astype(o_ref.dtype)
            for a_ref, v in zip(acc_refs, res[n_out:]):
                a_ref[...] += jnp.sum(v.astype(F32).reshape(sr // 8, 8, v.shape[-1]), axis=0)
            return carry

        lax.fori_loop(0, nsub, sub, 0)

    n_res = n_out + n_acc
    kern, s_in, s_out, s_shape, s_scr = _ride_along(kern, len(operands), n_res, 0, (nblk,), side)
    res = pl.pallas_call(
        kern, name=name, grid=(nblk,), in_specs=in_specs + s_in, out_specs=out_specs + s_out,
        out_shape=out_shape + s_shape, scratch_shapes=s_scr, compiler_params=_cparams(("arbitrary",)),
    )(*operands, *(side[0] if side else ()))
    return res if side is None else (res[:n_res], res[n_res:])


def _sigmoid(x):
    return 1.0 / (1.0 + jnp.exp(-x))


def _silu(x):
    return x * _sigmoid(x)


def _dsilu(x):
    s = _sigmoid(x)
    return s * (1.0 + x * (1.0 - s))


_GELU_K = math.sqrt(2.0 / math.pi)


def _gelu(x):
    return 0.5 * x * (1.0 + jnp.tanh(_GELU_K * (x + 0.044715 * x * x * x)))


def _dgelu(x):
    t = jnp.tanh(_GELU_K * (x + 0.044715 * x * x * x))
    return 0.5 * (1.0 + t) + 0.5 * x * (1.0 - t * t) * _GELU_K * (1.0 + 3.0 * 0.044715 * x * x)


def _rms_fwd(x, g, width):
    r = lax.rsqrt(jnp.sum(x * x, axis=-1, keepdims=True) * (1.0 / width) + EPS)
    return x * r * g


def _rms_bwd(x, g, dy, width):
    r = lax.rsqrt(jnp.sum(x * x, axis=-1, keepdims=True) * (1.0 / width) + EPS)
    xn = x * r
    dyg = dy * g
    dx = r * (dyg - xn * (jnp.sum(dyg * xn, axis=-1, keepdims=True) * (1.0 / width)))
    return dx, dy * xn


def _rope_fwd(y, c, sa, sb):
    return y * c + pltpu.roll(y, SLOT - 16, 1) * sa + pltpu.roll(y, 16, 1) * sb


def _rope_bwd(d, c, sa, sb):
    return d * c + pltpu.roll(d * sa, 16, 1) + pltpu.roll(d * sb, SLOT - 16, 1)


def _heads(v):
    return [v[:, h * SLOT:(h + 1) * SLOT] for h in range(N_HEADS)]


def _attn_fwd(q, k, v, nl, scale, side=None):
    n = k.shape[0]
    tq = _pick(nl, (4096, 2048, 1024, 512, 256, 128))
    tk = _pick(n, (2816, 1408, 1152, 768, 384, 256, 128))
    sub = min(tq, 512)
    nk = n // tk
    rep = tk // SLOT
    c = scale * LOG2E

    def body(q_ref, k_ref, v_ref, o_ref, lse_ref, m_sc, l_sc, acc_sc):
        ki = pl.program_id(2)

        @pl.when(ki == 0)
        def _():
            m_sc[...] = jnp.full(m_sc.shape, -jnp.inf, F32)
            l_sc[...] = jnp.zeros(l_sc.shape, F32)
            acc_sc[...] = jnp.zeros(acc_sc.shape, F32)

        kb, vb = k_ref[...], v_ref[...]
        for sb in range(tq // sub):
            rows = slice(sb * sub, (sb + 1) * sub)
            s = lax.dot_general(q_ref[rows, :], kb, (((1,), (1,)), ((), ())), preferred_element_type=F32)
            m_prev = m_sc[rows, :]
            m_new = jnp.maximum(m_prev, jnp.max(s, axis=1, keepdims=True) * c)
            alpha = jnp.exp2(m_prev - m_new)
            p = jnp.exp2(s * c - jnp.tile(m_new, (1, rep)))
            l_sc[rows, :] = alpha * l_sc[rows, :] + jnp.sum(p, axis=1, keepdims=True)
            acc_sc[rows, :] = alpha * acc_sc[rows, :] + jnp.dot(p.astype(MXU_DTYPE), vb, preferred_element_type=F32)
            m_sc[rows, :] = m_new

        @pl.when(ki == nk - 1)
        def _():
            l = l_sc[...]
            o_ref[...] = (acc_sc[...] / l).astype(o_ref.dtype)
            lse_ref[...] = jnp.transpose(m_sc[...] + jnp.log2(l))[0:8, :]

    grid = (N_HEADS, nl // tq, nk)
    body, s_in, s_out, s_shape, s_scr = _ride_along(body, 3, 2, 3, grid, side)
    res = pl.pallas_call(
        body, name="attn_fwd", grid=grid,
        in_specs=[pl.BlockSpec((tq, SLOT), lambda h, i, j: (i, h)),
                  pl.BlockSpec((tk, SLOT), lambda h, i, j: (j, h)),
                  pl.BlockSpec((tk, SLOT), lambda h, i, j: (j, h))] + s_in,
        out_specs=[pl.BlockSpec((tq, SLOT), lambda h, i, j: (i, h)),
                   pl.BlockSpec((None, 8, tq), lambda h, i, j: (h, 0, i))] + s_out,
        out_shape=[_sds((nl, N_HEADS * SLOT), MXU_DTYPE), _sds((N_HEADS, 8, nl), F32)] + s_shape,
        scratch_shapes=[pltpu.VMEM((tq, SLOT), F32), pltpu.VMEM((tq, SLOT), F32), pltpu.VMEM((tq, SLOT), F32)] + s_scr,
        compiler_params=_cparams(("arbitrary", "arbitrary", "arbitrary")),
    )(q, k, v, *(side[0] if side else ()))
    return res[0], res[1], res[2:]


def _attn_bwd(q, k, v, do, o, lse_t, nl, scale, side=None):
    n = k.shape[0]
    tq = _pick(nl, (4096, 2048, 1024, 512, 256, 128))
    tk = _pick(n, (2816, 1408, 1152, 768, 384, 256, 128))
    sub = _pick(tk, (256, 128))
    nq, nk = nl // tq, n // tk
    c = scale * LOG2E

    def body(q_ref, k_ref, v_ref, do_ref, o_ref, lse_ref, dq_ref, dk_ref, dv_ref, dq_acc, dk_acc, dv_acc):
        ki, qi = pl.program_id(1), pl.program_id(2)

        @pl.when((ki == 0) & (qi == 0))
        def _():
            dq_acc[...] = jnp.zeros(dq_acc.shape, F32)

        @pl.when(qi == 0)
        def _():
            dk_acc[...] = jnp.zeros(dk_acc.shape, F32)
            dv_acc[...] = jnp.zeros(dv_acc.shape, F32)

        qb, dof = q_ref[...], do_ref[...]
        dob = dof.astype(MXU_DTYPE)
        lse_r = lse_ref[0:1, :]
        dl_r = jnp.sum(jnp.transpose(dof * o_ref[...].astype(F32)), axis=0, keepdims=True)
        dq_part = None
        for sb in range(tk // sub):
            rows = slice(sb * sub, (sb + 1) * sub)
            kb = k_ref[rows, :]
            s_t = lax.dot_general(kb, qb, (((1,), (1,)), ((), ())), preferred_element_type=F32)
            p_t = jnp.exp2(s_t * c - lse_r)
            dp_t = lax.dot_general(v_ref[rows, :], dob, (((1,), (1,)), ((), ())), preferred_element_type=F32)
            ds_t = (p_t * (dp_t - dl_r) * scale).astype(MXU_DTYPE)
            dv_acc[rows, :] += jnp.dot(p_t.astype(MXU_DTYPE), dob, preferred_element_type=F32)
            dk_acc[rows, :] += jnp.dot(ds_t, qb, preferred_element_type=F32)
            part = lax.dot_general(kb, ds_t, (((0,), (0,)), ((), ())), preferred_element_type=F32)
            dq_part = part if dq_part is None else dq_part + part
        c0 = pl.multiple_of(qi * tq, tq)
        dq_acc[:, pl.ds(c0, tq)] += dq_part

        @pl.when(ki == nk - 1)
        def _():
            dq_ref[...] = jnp.transpose(dq_acc[:, pl.ds(c0, tq)])

        @pl.when(qi == nq - 1)
        def _():
            dk_ref[...] = dk_acc[...]
            dv_ref[...] = dv_acc[...]

    grid = (N_HEADS, nk, nq)
    body, s_in, s_out, s_shape, s_scr = _ride_along(body, 6, 3, 3, grid, side)
    res = pl.pallas_call(
        body, name="attn_bwd", grid=grid,
        in_specs=[pl.BlockSpec((tq, SLOT), lambda h, j, i: (i, h)),
                  pl.BlockSpec((tk, SLOT), lambda h, j, i: (j, h)),
                  pl.BlockSpec((tk, SLOT), lambda h, j, i: (j, h)),
                  pl.BlockSpec((tq, SLOT), lambda h, j, i: (i, h)),
                  pl.BlockSpec((tq, SLOT), lambda h, j, i: (i, h)),
                  pl.BlockSpec((None, 8, tq), lambda h, j, i: (h, 0, i))] + s_in,
        out_specs=[pl.BlockSpec((tq, SLOT), lambda h, j, i: (jnp.where(j == nk - 1, i, 0), h)),
                   pl.BlockSpec((tk, SLOT), lambda h, j, i: (j, h)),
                   pl.BlockSpec((tk, SLOT), lambda h, j, i: (j, h))] + s_out,
        out_shape=[_sds((nl, N_HEADS * SLOT), F32), _sds((n, N_HEADS * SLOT), F32), _sds((n, N_HEADS * SLOT), F32)] + s_shape,
        scratch_shapes=[pltpu.VMEM((SLOT, nl), F32), pltpu.VMEM((tk, SLOT), F32), pltpu.VMEM((tk, SLOT), F32)] + s_scr,
        compiler_params=_cparams(("arbitrary", "arbitrary", "arbitrary")),
    )(q, k, v, do, o, lse_t, *(side[0] if side else ()))
    return res[0], res[1], res[2], res[3:]


def _scan_consts(c_ref, lg):
    cs = slice(lg * SCAN_LANES, (lg + 1) * SCAN_LANES)
    return [c_ref[8 * kk:8 * kk + 8, cs] for kk in range(8)]


def _tile_scan(br, bi, consts, reverse):
    p1r, p1i, p2r, p2i, p4r, p4i = consts[:6]
    for pr, pi, kk in ((p1r, p1i, 1), (p2r, p2i, 2), (p4r, p4i, 4)):
        sh = (8 - kk) if reverse else kk
        sr_, si_ = pltpu.roll(br, sh, 0), pltpu.roll(bi, sh, 0)
        br, bi = br + pr * sr_ - pi * si_, bi + pr * si_ + pi * sr_
    return br, bi


def _seq_chunk(j, nch, nlc, reverse):
    return (nch - 1 - j) if reverse else (j + nlc) % nch


def _s5_scan(z, u_cb, bbd, cbd_n, lamc, t_rows, nl, reverse, name):
    n = z.shape[0]
    nch, nlc = n // t_rows, nl // t_rows
    ntile = t_rows // 8
    w = SCAN_LANES
    edge = 0 if reverse else 7
    ucb = u_cb * (SSM_WIDTH // CG_CHANNELS)

    def chunk(j):
        return _seq_chunk(j, nch, nlc, reverse)

    def body(u_ref, b_ref, cn_ref, c_ref, xs_ref, y_ref, carry):
        j = pl.program_id(1)

        @pl.when(j == 0)
        def _():
            carry[...] = jnp.zeros(carry.shape, F32)

        xs_ref[...] = jnp.dot(u_ref[...].astype(MXU_DTYPE), b_ref[...], preferred_element_type=F32)
        for lg in range(CG_STATES // w):
            re = slice(lg * w, (lg + 1) * w)
            im = slice(CG_STATES + lg * w, CG_STATES + (lg + 1) * w)
            consts = _scan_consts(c_ref, lg)
            qr, qi = consts[6], consts[7]

            def tile(tt, st):
                cr, ci = st
                t = (ntile - 1 - tt) if reverse else tt
                r0 = pl.multiple_of(t * 8, 8)
                br, bi = _tile_scan(xs_ref[pl.ds(r0, 8), re], xs_ref[pl.ds(r0, 8), im], consts, reverse)
                lr = jnp.broadcast_to(cr[edge:edge + 1, :], br.shape)
                li = jnp.broadcast_to(ci[edge:edge + 1, :], bi.shape)
                xr = br + qr * lr - qi * li
                xi = bi + qr * li + qi * lr
                xs_ref[pl.ds(r0, 8), re] = xr
                xs_ref[pl.ds(r0, 8), im] = xi
                return xr, xi

            cr, ci = lax.fori_loop(0, ntile, tile, (carry[:, re], carry[:, im]))
            carry[:, re] = cr
            carry[:, im] = ci
        y_ref[...] = jnp.dot(xs_ref[...].astype(MXU_DTYPE), cn_ref[...], preferred_element_type=F32)

    cw = 2 * CG_STATES
    return pl.pallas_call(
        body, name=name, grid=(N_CG, nch),
        in_specs=[pl.BlockSpec((t_rows, CG_CHANNELS), lambda g, j: (chunk(j), ucb + g)),
                  pl.BlockSpec((CG_CHANNELS, cw), lambda g, j: (g, 0)),
                  pl.BlockSpec((cw, CG_CHANNELS), lambda g, j: (g, 0)),
                  pl.BlockSpec((64, CG_STATES), lambda g, j: (0, g))],
        out_specs=[pl.BlockSpec((t_rows, cw), lambda g, j: (chunk(j), g)),
                   pl.BlockSpec((t_rows, CG_CHANNELS), lambda g, j: (chunk(j), g))],
        out_shape=[_sds((n, 2 * N_STATE), F32), _sds((n, SSM_WIDTH), F32)],
        scratch_shapes=[pltpu.VMEM((8, cw), F32)],
        compiler_params=_cparams(("arbitrary", "arbitrary")),
    )(z, bbd, cbd_n, lamc)


def _s5_bwd(dys, z, u_cb, xs, cbd_t, bbd_t, lamc_adj, t_rows, nl, reverse, name):
    n = z.shape[0]
    nch, nlc = n // t_rows, nl // t_rows
    ntile = t_rows // 8
    t8 = t_rows // 8
    w = SCAN_LANES
    cw = 2 * CG_STATES
    adj_rev = not reverse
    edge = 0 if adj_rev else 7

    def chunk(j):
        return _seq_chunk(nch - 1 - j, nch, nlc, reverse)

    def halo_blk(j):
        if reverse:
            return jnp.minimum((chunk(j) + 1) * t8, n // 8 - 1)
        return (_seq_chunk(jnp.maximum(nch - 2 - j, 0), nch, nlc, False) + 1) * t8 - 1

    def body(dy_ref, u_ref, xs_ref, halo_ref, ct_ref, bt_ref, c_ref, du_ref, db_ref, dc_ref, dl_ref, gbuf, carry):
        j = pl.program_id(1)
        start = j == nch - 1

        @pl.when(j == 0)
        def _():
            carry[...] = jnp.zeros(carry.shape, F32)
            db_ref[...] = jnp.zeros(db_ref.shape, F32)
            dc_ref[...] = jnp.zeros(dc_ref.shape, F32)
            dl_ref[...] = jnp.zeros(dl_ref.shape, F32)

        dy = jnp.where(chunk(j) < nlc, dy_ref[...], 0.0).astype(MXU_DTYPE)
        gbuf[...] = jnp.dot(dy, ct_ref[...], preferred_element_type=F32)
        dc_ref[...] += lax.dot_general(dy, xs_ref[...].astype(MXU_DTYPE), (((0,), (0,)), ((), ())),
                                       preferred_element_type=F32)
        for lg in range(CG_STATES // w):
            re = slice(lg * w, (lg + 1) * w)
            im = slice(CG_STATES + lg * w, CG_STATES + (lg + 1) * w)
            consts = _scan_consts(c_ref, lg)
            qr, qi = consts[6], consts[7]
            hr, hi = halo_ref[:, re], halo_ref[:, im]

            def tile(tt, st):
                gcr, gci, ar, ai = st
                t = (ntile - 1 - tt) if adj_rev else tt
                r0 = pl.multiple_of(t * 8, 8)
                br, bi = _tile_scan(gbuf[pl.ds(r0, 8), re], gbuf[pl.ds(r0, 8), im], consts, adj_rev)
                lr = jnp.broadcast_to(gcr[edge:edge + 1, :], br.shape)
                li = jnp.broadcast_to(gci[edge:edge + 1, :], bi.shape)
                gr = br + qr * lr - qi * li
                gi = bi + qr * li + qi * lr
                gbuf[pl.ds(r0, 8), re] = gr
                gbuf[pl.ds(r0, 8), im] = gi
                xr, xi = xs_ref[pl.ds(r0, 8), re], xs_ref[pl.ds(r0, 8), im]
                rid = lax.broadcasted_iota(jnp.int32, xr.shape, 0)
                if reverse:
                    last = t == ntile - 1
                    rn = pl.multiple_of(jnp.minimum(r0 + 8, t_rows - 8), 8)
                    nbr = jnp.where(last, hr, xs_ref[pl.ds(rn, 8), re])
                    nbi = jnp.where(last, hi, xs_ref[pl.ds(rn, 8), im])
                    nbr = jnp.where(last & start, 0.0, nbr)
                    nbi = jnp.where(last & start, 0.0, nbi)
                    xpr = jnp.where(rid == 7, jnp.broadcast_to(nbr[0:1, :], xr.shape), pltpu.roll(xr, 7, 0))
                    xpi = jnp.where(rid == 7, jnp.broadcast_to(nbi[0:1, :], xi.shape), pltpu.roll(xi, 7, 0))
                else:
                    first = t == 0
                    rn = pl.multiple_of(jnp.maximum(r0 - 8, 0), 8)
                    nbr = jnp.where(first, hr, xs_ref[pl.ds(rn, 8), re])
                    nbi = jnp.where(first, hi, xs_ref[pl.ds(rn, 8), im])
                    nbr = jnp.where(first & start, 0.0, nbr)
                    nbi = jnp.where(first & start, 0.0, nbi)
                    xpr = jnp.where(rid == 0, jnp.broadcast_to(nbr[7:8, :], xr.shape), pltpu.roll(xr, 1, 0))
                    xpi = jnp.where(rid == 0, jnp.broadcast_to(nbi[7:8, :], xi.shape), pltpu.roll(xi, 1, 0))
                ar = ar + gr * xpr + gi * xpi
                ai = ai - gr * xpi + gi * xpr
                return gr, gi, ar, ai

            zz = jnp.zeros((8, w), F32)
            gcr, gci, ar, ai = lax.fori_loop(0, ntile, tile, (carry[:, re], carry[:, im], zz, zz))
            carry[:, re] = gcr
            carry[:, im] = gci
            dl_ref[:, re] += ar
            dl_ref[:, im] += ai
        g = gbuf[...].astype(MXU_DTYPE)
        du_ref[...] = jnp.dot(g, bt_ref[...], preferred_element_type=F32)
        db_ref[...] += lax.dot_general(u_ref[...].astype(MXU_DTYPE), g, (((0,), (0,)), ((), ())),
                                       preferred_element_type=F32)

    ucb = u_cb * (SSM_WIDTH // CG_CHANNELS)
    return pl.pallas_call(
        body, name=name, grid=(N_CG, nch),
        in_specs=[pl.BlockSpec((t_rows, CG_CHANNELS), lambda g, j: (jnp.minimum(chunk(j), nlc - 1), g)),
                  pl.BlockSpec((t_rows, CG_CHANNELS), lambda g, j: (chunk(j), ucb + g)),
                  pl.BlockSpec((t_rows, cw), lambda g, j: (chunk(j), g)),
                  pl.BlockSpec((8, cw), lambda g, j: (halo_blk(j), g)),
                  pl.BlockSpec((CG_CHANNELS, cw), lambda g, j: (g, 0)),
                  pl.BlockSpec((cw, CG_CHANNELS), lambda g, j: (g, 0)),
                  pl.BlockSpec((64, CG_STATES), lambda g, j: (0, g))],
        out_specs=[pl.BlockSpec((t_rows, CG_CHANNELS), lambda g, j: (chunk(j), g)),
                   pl.BlockSpec((CG_CHANNELS, cw), lambda g, j: (g, 0)),
                   pl.BlockSpec((CG_CHANNELS, cw), lambda g, j: (g, 0)),
                   pl.BlockSpec((8, cw), lambda g, j: (0, g))],
        out_shape=[_sds((n, SSM_WIDTH), F32), _sds((SSM_WIDTH, cw), F32), _sds((SSM_WIDTH, cw), F32),
                   _sds((8, 2 * N_STATE), F32)],
        scratch_shapes=[pltpu.VMEM((t_rows, cw), F32), pltpu.VMEM((8, cw), F32)],
        compiler_params=_cparams(("arbitrary", "arbitrary")),
    )(dys, z, xs, xs, cbd_t, bbd_t, lamc_adj)


_CG_GROUPS = SSM_GROUPS // N_CG


def _group_mask():
    idx = jnp.arange(_CG_GROUPS)
    return (idx[:, None] == idx[None, :])[None, :, None, None, :, None]


def _diag_blocks(p_re, p_im):
    t = jnp.stack([p_re, p_im], axis=2).reshape(N_CG, _CG_GROUPS, SSM_GROUP, 2, 1, SSM_STATE)
    return jnp.where(_group_mask(), t, 0.0).reshape(SSM_WIDTH, 2 * CG_STATES)


def _diag_extract(d):
    d6 = d.reshape(N_CG, _CG_GROUPS, SSM_GROUP, 2, _CG_GROUPS, SSM_STATE)
    blk = jnp.sum(jnp.where(_group_mask(), d6, 0.0), axis=4)
    blk = blk.reshape(SSM_GROUPS, SSM_GROUP, 2, SSM_STATE)
    return blk[:, :, 0], blk[:, :, 1]


def _block_transpose(d):
    return jnp.transpose(d.reshape(N_CG, CG_CHANNELS, 2 * CG_STATES), (0, 2, 1)).reshape(2 * N_STATE, CG_CHANNELS)


def _s5_disc(lam_re, lam_im, log_dt, b_re, b_im):
    lam = lax.complex(lam_re, lam_im)
    dt = jnp.exp(log_dt)[:, None]
    lam_bar = jnp.exp(lam * dt)
    b_bar = ((lam_bar - 1.0) / lam)[..., None] * lax.complex(b_re, b_im)
    return jnp.real(lam_bar), jnp.imag(lam_bar), jnp.real(b_bar), jnp.imag(b_bar)


def _lam_consts(lr, li, mirrored, conj):
    lam = lax.complex(lr.reshape(-1), -li.reshape(-1) if conj else li.reshape(-1))
    p2 = lam * lam
    p4 = p2 * p2
    pw = [lam, p2, p2 * lam, p4, p4 * lam, p4 * p2, p4 * p2 * lam, p4 * p4]
    rows = jnp.arange(8)[:, None]
    out = []
    for kk in (1, 2, 4):
        mask = (rows <= 7 - kk) if mirrored else (rows >= kk)
        pk = jnp.where(mask, pw[kk - 1][None, :], 0.0)
        out += [jnp.real(pk), jnp.imag(pk)]
    q = jnp.stack(pw[::-1] if mirrored else pw)
    return jnp.concatenate(out + [jnp.real(q), jnp.imag(q)], axis=0)


def _dev(t):
    return (t // 4, (t // 2) % 2, t % 2)


def _my_index():
    return 4 * lax.axis_index("x") + 2 * lax.axis_index("y") + lax.axis_index("c")


def _comm_call(body, name, arrs, lead, n_remote):
    nw = len(arrs)
    any_spec = pl.BlockSpec(memory_space=pl.ANY)
    return pl.pallas_call(
        body, name=name, out_shape=[_sds((lead,) + a.shape[-2:], a.dtype) for a in arrs],
        in_specs=[any_spec] * nw, out_specs=[any_spec] * nw,
        scratch_shapes=[pltpu.SemaphoreType.DMA((n_remote * nw,)), pltpu.SemaphoreType.DMA((n_remote * nw,)),
                        pltpu.SemaphoreType.DMA((2 * nw,))] + [pltpu.VMEM(a.shape[-2:], a.dtype) for a in arrs],
        compiler_params=pltpu.CompilerParams(vmem_limit_bytes=VMEM_LIMIT),
    )(*arrs)


class _LocalCopy:
    def __init__(self, src, dst, buf, sem_in, sem_out):
        self.fetch = pltpu.make_async_copy(src, buf, sem_in)
        self.store = pltpu.make_async_copy(buf, dst, sem_out)
        self.fetch.start()

    def forward(self):
        self.fetch.wait()
        self.store.start()

    def finish(self):
        self.store.wait()


ALL8 = "all8"


def _all8_copies(g_refs, o_refs, ssem, rsem, lsem, bufs):
    me = _my_index()
    fetch, store, sends, recvs = [], [], [], []
    for i, (g_ref, o_ref) in enumerate(zip(g_refs, o_refs)):
        fetch.append(pltpu.make_async_copy(g_ref, bufs[i], lsem.at[2 * i]))
        store.append(pltpu.make_async_copy(bufs[i], o_ref.at[me], lsem.at[2 * i + 1]))
        for dd in range(1, 8):
            t, s = (me + dd) % 8, (me + 8 - dd) % 8
            sems = dict(send_sem=ssem.at[7 * i + dd - 1], recv_sem=rsem.at[7 * i + dd - 1], device_id_type=MESH)
            sends.append(pltpu.make_async_remote_copy(src_ref=g_ref, dst_ref=o_ref.at[me], device_id=_dev(t), **sems))
            recvs.append(pltpu.make_async_remote_copy(src_ref=g_ref, dst_ref=o_ref.at[s], device_id=_dev(s), **sems))
    return fetch, store, sends, recvs


def _exchange8(gs, name):
    nw = len(gs)

    def body(*refs):
        args = (refs[:nw], refs[nw:2 * nw], *refs[2 * nw:2 * nw + 3], refs[2 * nw + 3:], ALL8)
        _chips_start(*args)
        _chips_finish(*args)

    return _comm_call(body, name, gs, 8, 7)


def _chip_copies(w_refs, o_refs, ssem, rsem, lsem, bufs, gather):
    if gather == ALL8:
        return _all8_copies(w_refs, o_refs, ssem, rsem, lsem, bufs)
    x, y, cc = lax.axis_index("x"), lax.axis_index("y"), lax.axis_index("c")
    k = 2 * x + y
    peers = [(1 - x, y), (x, 1 - y), (1 - x, 1 - y)]
    fetch, store, sends, recvs = [], [], [], []
    for i, (w_ref, o_ref) in enumerate(zip(w_refs, o_refs)):
        if gather:
            fetch.append(pltpu.make_async_copy(w_ref.at[cc], bufs[i], lsem.at[2 * i]))
            store.append(pltpu.make_async_copy(bufs[i], o_ref.at[k], lsem.at[2 * i + 1]))
        for j, (px, py) in enumerate(peers):
            sems = dict(send_sem=ssem.at[3 * i + j], recv_sem=rsem.at[3 * i + j], device_id=(px, py, cc), device_id_type=MESH)
            src, dst = (w_ref.at[cc], o_ref.at[k]) if gather else (w_ref.at[2 * px + py], o_ref.at[j])
            sends.append(pltpu.make_async_remote_copy(src_ref=src, dst_ref=dst, **sems))
            src, dst = (w_ref.at[cc], o_ref.at[2 * px + py]) if gather else (w_ref.at[k], o_ref.at[j])
            recvs.append(pltpu.make_async_remote_copy(src_ref=src, dst_ref=dst, **sems))
    return fetch, store, sends, recvs


def _chips_start(*args):
    fetch, _, sends, _ = _chip_copies(*args)
    for cp in fetch + sends:
        cp.start()


def _chips_finish(*args):
    fetch, store, sends, recvs = _chip_copies(*args)
    for cp in fetch:
        cp.wait()
    for cp in store:
        cp.start()
    for cp in recvs:
        cp.wait_recv()
    for cp in sends:
        cp.wait_send()
    for cp in store:
        cp.wait()


def _chips_scratch(ws, gather):
    nw = len(ws)
    n_remote = 7 if gather == ALL8 else 3
    return ([pltpu.SemaphoreType.DMA((n_remote * nw,)), pltpu.SemaphoreType.DMA((n_remote * nw,)),
             pltpu.SemaphoreType.DMA((2 * nw,))] + ([pltpu.VMEM(a.shape[-2:], a.dtype) for a in ws] if gather else []))


def _ride_along(core, n_in, n_out, n_scr, grid, side):
    if side is None:
        return core, [], [], [], []
    arrs, gather = side
    ns = len(arrs)

    def body(*refs):
        a, b, c_ = n_in + ns, n_in + ns + n_out, n_in + 2 * ns + n_out
        s_scr = refs[c_ + n_scr:]
        sargs = (refs[n_in:a], refs[b:c_], *s_scr[:3], s_scr[3:], gather)
        ids = [pl.program_id(ax) for ax in range(len(grid))]
        first, last = ids[0] == 0, ids[0] == grid[0] - 1
        for i_, g_ in zip(ids[1:], grid[1:]):
            first, last = first & (i_ == 0), last & (i_ == g_ - 1)

        @pl.when(first)
        def _():
            _chips_start(*sargs)

        core(*refs[:n_in], *refs[a:b], *refs[c_:c_ + n_scr])

        @pl.when(last)
        def _():
            _chips_finish(*sargs)

    any_spec = pl.BlockSpec(memory_space=pl.ANY)
    lead = 8 if gather == ALL8 else (4 if gather else 3)
    shapes = [_sds((lead,) + a_.shape[-2:], a_.dtype) for a_ in arrs]
    return body, [any_spec] * ns, [any_spec] * ns, shapes, _chips_scratch(arrs, gather)


def _sibling_send(hs, name):
    nw = len(hs)

    def body(*refs):
        h_refs, o_refs, (ssem, rsem, lsem) = refs[:nw], refs[nw:2 * nw], refs[2 * nw:]
        x, y, cc = lax.axis_index("x"), lax.axis_index("y"), lax.axis_index("c")
        sends = []
        for i, (h_ref, o_ref) in enumerate(zip(h_refs, o_refs)):
            cp = pltpu.make_async_remote_copy(src_ref=h_ref.at[1 - cc], dst_ref=o_ref, send_sem=ssem.at[i],
                                              recv_sem=rsem.at[i], device_id=(x, y, 1 - cc), device_id_type=MESH)
            cp.start()
            sends.append(cp)
        for i, (h_ref, o_ref) in enumerate(zip(h_refs, o_refs)):
            pltpu.make_async_remote_copy(src_ref=h_ref.at[cc], dst_ref=o_ref, send_sem=ssem.at[i], recv_sem=rsem.at[i],
                                         device_id=(x, y, 1 - cc), device_id_type=MESH).wait_recv()
        for cp in sends:
            cp.wait_send()

    nw_spec = pl.BlockSpec(memory_space=pl.ANY)
    return pl.pallas_call(
        body, name=name, out_shape=[_sds(h.shape[1:], h.dtype) for h in hs],
        in_specs=[nw_spec] * nw, out_specs=[nw_spec] * nw,
        scratch_shapes=[pltpu.SemaphoreType.DMA((nw,)), pltpu.SemaphoreType.DMA((nw,)), pltpu.SemaphoreType.DMA((nw,))],
    )(*hs)


def _sibling_exchange(hs, name):
    nw = len(hs)

    def body(*refs):
        h_refs, o_refs, (ssem, rsem, lsem), bufs = refs[:nw], refs[nw:2 * nw], refs[2 * nw:2 * nw + 3], refs[2 * nw + 3:]
        x, y, cc = lax.axis_index("x"), lax.axis_index("y"), lax.axis_index("c")
        locs, sends = [], []
        for i, (h_ref, o_ref) in enumerate(zip(h_refs, o_refs)):
            locs.append(_LocalCopy(h_ref, o_ref.at[cc], bufs[i], lsem.at[2 * i], lsem.at[2 * i + 1]))
            cp = pltpu.make_async_remote_copy(src_ref=h_ref, dst_ref=o_ref.at[cc], send_sem=ssem.at[i], recv_sem=rsem.at[i],
                                              device_id=(x, y, 1 - cc), device_id_type=MESH)
            cp.start()
            sends.append(cp)
        for loc in locs:
            loc.forward()
        for i, (h_ref, o_ref) in enumerate(zip(h_refs, o_refs)):
            pltpu.make_async_remote_copy(src_ref=h_ref, dst_ref=o_ref.at[1 - cc], send_sem=ssem.at[i], recv_sem=rsem.at[i],
                                         device_id=(x, y, 1 - cc), device_id_type=MESH).wait_recv()
        for cp in sends:
            cp.wait_send()
        for loc in locs:
            loc.finish()

    return _comm_call(body, name, hs, 2, 1)


def _sum8(buf, name):
    _, r, c = buf.shape
    tr = _pick(r, (256, 128, 64, 32, 16, 8))
    flat = buf.reshape(8 * r, c)

    def body(*v):
        acc = v[0]
        for t in v[1:]:
            acc = acc + t
        return acc

    return _rowwise(body, name=name, nblk=r // tr, tr=tr, rows=[(flat, 0, c, s * r) for s in range(8)],
                    outs=[(c, F32)])[0]


def _pack(arrs, rows_mult=16):
    flat = jnp.concatenate([a.reshape(-1).astype(F32) for a in arrs])
    nel = flat.shape[0]
    r = -(-nel // PACK_W)
    r = -(-r // rows_mult) * rows_mult
    return jnp.pad(flat, (0, r * PACK_W - nel)).reshape(r, PACK_W)


def _unpack(buf, shapes):
    flat = buf.reshape(-1)
    out, o = [], 0
    for s in shapes:
        nel = int(np.prod(s))
        out.append(flat[o:o + nel].reshape(s))
        o += nel
    return out


def _adamw(g, w, m, v, name):
    r, wd = g.shape
    tr = _pick(r, tuple(t for t in (256, 128, 64, 32, 16, 8) if t * wd <= 262144) or (8,))
    c1 = 1.0 / (1.0 - ADAM_B1 ** ADAM_STEP)
    c2 = 1.0 / (1.0 - ADAM_B2 ** ADAM_STEP)

    def body(gv, wv, mv, vv):
        mn = ADAM_B1 * mv + (1.0 - ADAM_B1) * gv
        vn = ADAM_B2 * vv + (1.0 - ADAM_B2) * (gv * gv)
        delta = -ADAM_LR * ((mn * c1) / (jnp.sqrt(vn * c2) + ADAM_EPS) + ADAM_WD * wv)
        return delta, mn, vn

    return _rowwise(body, name=name, nblk=r // tr, tr=tr, rows=[(a, 0, wd, 0) for a in (g, w, m, v)],
                    outs=[(wd, F32)] * 3)


def kernel(x, c, ctx, c_ctx, w_mod, b_mod, norm1_g, norm2_g, w_in, q_a_g, w_uq, kv_a_g, w_ukv, q_norm_g, k_norm_g, w_o_attn, lam_re_f, lam_im_f, log_dt_f, c_re_f, c_im_f, lam_re_b, lam_im_b, log_dt_b, c_re_b, c_im_b, b_re, b_im, d_skip, w_glu, w_out, w_up, conv_w, conv_b, w_down, loss_target, m_c_ctx, m_w_mod, m_b_mod, m_norm1_g, m_norm2_g, m_w_in, m_q_a_g, m_w_uq, m_kv_a_g, m_w_ukv, m_q_norm_g, m_k_norm_g, m_w_o_attn, m_lam_re_f, m_lam_im_f, m_log_dt_f, m_c_re_f, m_c_im_f, m_lam_re_b, m_lam_im_b, m_log_dt_b, m_c_re_b, m_c_im_b, m_b_re, m_b_im, m_d_skip, m_w_glu, m_w_out, m_w_up, m_conv_w, m_conv_b, m_w_down, v_c_ctx, v_w_mod, v_b_mod, v_norm1_g, v_norm2_g, v_w_in, v_q_a_g, v_w_uq, v_kv_a_g, v_w_ukv, v_q_norm_g, v_k_norm_g, v_w_o_attn, v_lam_re_f, v_lam_im_f, v_log_dt_f, v_c_re_f, v_c_im_f, v_lam_re_b, v_lam_im_b, v_log_dt_b, v_c_re_b, v_c_im_b, v_b_re, v_b_im, v_d_skip, v_w_glu, v_w_out, v_w_up, v_conv_w, v_conv_b, v_w_down):
    weights = dict(c_ctx=c_ctx, w_mod=w_mod, b_mod=b_mod, norm1_g=norm1_g, norm2_g=norm2_g, w_in=w_in, q_a_g=q_a_g, w_uq=w_uq, kv_a_g=kv_a_g, w_ukv=w_ukv, q_norm_g=q_norm_g, k_norm_g=k_norm_g, w_o_attn=w_o_attn, lam_re_f=lam_re_f, lam_im_f=lam_im_f, log_dt_f=log_dt_f, c_re_f=c_re_f, c_im_f=c_im_f, lam_re_b=lam_re_b, lam_im_b=lam_im_b, log_dt_b=log_dt_b, c_re_b=c_re_b, c_im_b=c_im_b, b_re=b_re, b_im=b_im, d_skip=d_skip, w_glu=w_glu, w_out=w_out, w_up=w_up, conv_w=conv_w, conv_b=conv_b, w_down=w_down)
    mom_m = dict(c_ctx=m_c_ctx, w_mod=m_w_mod, b_mod=m_b_mod, norm1_g=m_norm1_g, norm2_g=m_norm2_g, w_in=m_w_in, q_a_g=m_q_a_g, w_uq=m_w_uq, kv_a_g=m_kv_a_g, w_ukv=m_w_ukv, q_norm_g=m_q_norm_g, k_norm_g=m_k_norm_g, w_o_attn=m_w_o_attn, lam_re_f=m_lam_re_f, lam_im_f=m_lam_im_f, log_dt_f=m_log_dt_f, c_re_f=m_c_re_f, c_im_f=m_c_im_f, lam_re_b=m_lam_re_b, lam_im_b=m_lam_im_b, log_dt_b=m_log_dt_b, c_re_b=m_c_re_b, c_im_b=m_c_im_b, b_re=m_b_re, b_im=m_b_im, d_skip=m_d_skip, w_glu=m_w_glu, w_out=m_w_out, w_up=m_w_up, conv_w=m_conv_w, conv_b=m_conv_b, w_down=m_w_down)
    mom_v = dict(c_ctx=v_c_ctx, w_mod=v_w_mod, b_mod=v_b_mod, norm1_g=v_norm1_g, norm2_g=v_norm2_g, w_in=v_w_in, q_a_g=v_q_a_g, w_uq=v_w_uq, kv_a_g=v_kv_a_g, w_ukv=v_w_ukv, q_norm_g=v_q_norm_g, k_norm_g=v_k_norm_g, w_o_attn=v_w_o_attn, lam_re_f=v_lam_re_f, lam_im_f=v_lam_im_f, log_dt_f=v_log_dt_f, c_re_f=v_c_re_f, c_im_f=v_c_im_f, lam_re_b=v_lam_re_b, lam_im_b=v_lam_im_b, log_dt_b=v_log_dt_b, c_re_b=v_c_re_b, c_im_b=v_c_im_b, b_re=v_b_re, b_im=v_b_im, d_skip=v_d_skip, w_glu=v_w_glu, w_out=v_w_out, w_up=v_w_up, conv_w=v_conv_w, conv_b=v_conv_b, w_down=v_w_down)
    names = list(weights)

    nl, d = x.shape[1], x.shape[2]
    nc = ctx.shape[1]
    n = nl + nc
    f2 = conv_b.shape[1]
    fh = f2 // 2
    d6 = b_mod.shape[1]
    mx, my, mc = lax.axis_index("x"), lax.axis_index("y"), lax.axis_index("c")
    chip = 2 * mx + my
    me = 4 * mx + 2 * my + mc
    tr = _pick(math.gcd(nl, nc), (256, 128, 64, 32, 16))
    nlb, nb = nl // tr, n // tr

    big_names = ["w_in", "w_uq", "w_ukv", "w_o_attn", "w_glu", "w_out", "w_up", "w_down"]
    row_sharded = ("w_out", "w_down")
    ffn_names = ["w_o_attn", "w_glu", "w_out", "w_up", "w_down"]
    early_names = [k for k in big_names if k not in ffn_names]
    full = {}

    def halves_in(names_):
        return [weights[k][0].astype(MXU_DTYPE).reshape(2, weights[k].shape[1] // 2, weights[k].shape[2]) for k in names_]

    def assemble(names_, my_halves, name):
        gathered = _sibling_exchange([t.reshape(-1, t.shape[2]) for t in my_halves], name)
        for k_, gth in zip(names_, gathered):
            r_, c_ = weights[k_].shape[1:]
            g4 = gth.reshape(2, 4, r_ // 2, c_)
            full[k_] = (jnp.transpose(g4, (1, 0, 2, 3)).reshape(4 * r_, c_) if k_ in row_sharded
                        else jnp.transpose(g4, (0, 2, 1, 3)).reshape(r_, 4 * c_))


    cwid = conv_w.shape[2]
    sw = -(-max(d, cwid) // 128) * 128
    small_in = jnp.concatenate([jnp.pad(c, ((0, 0), (0, sw - d))), jnp.pad(conv_w[0], ((0, 4), (0, sw - cwid)))], axis=0)
    (small_all,) = _exchange8([small_in], "gather_c")
    cs = small_all[:, 0, :d]
    conv_w_full = jnp.concatenate([small_all[2 * j, 1:4, :cwid] for j in range(4)], axis=1)
    cs16 = jnp.concatenate([cs, c_ctx[None, :], jnp.zeros((7, d), F32)], axis=0)

    csh = w_mod.shape[2]
    b_mod_sh = lax.dynamic_slice(b_mod, (0, chip * csh), (1, csh))

    def mod_fwd_body(c_ref, w_ref, b_ref, o_ref):
        a = _silu(c_ref[...]).astype(MXU_DTYPE)
        o_ref[...] = jnp.dot(a, w_ref[...].astype(MXU_DTYPE), preferred_element_type=F32) + b_ref[...]

    mod_sh = pl.pallas_call(mod_fwd_body, name="mod_fwd", out_shape=_sds((16, csh), F32),
                            compiler_params=pltpu.CompilerParams(vmem_limit_bytes=VMEM_LIMIT))(cs16, w_mod[0], b_mod_sh)
    (mod_all,) = _exchange8([mod_sh], "gather_mod")
    mod_full = jnp.concatenate([mod_all[2 * j] for j in range(4)], axis=1)
    modv = jnp.stack([lax.dynamic_slice(mod_full, (me, 0), (1, d6)), mod_full[8:9]])

    xa = jnp.concatenate([x[0], ctx[0]], axis=0)
    n1g, n2g = norm1_g, norm2_g

    def norm1_body(xv, m, g):
        sh1, sc1 = m[:, :d], m[:, d:2 * d]
        return _rms_fwd(xv, g, d) * (1.0 + sc1) + sh1

    (h1,), early_halves = _rowwise(norm1_body, name="norm1_fwd", nblk=nb, tr=tr, rows=[(xa, 0, d, 0)], sels=[modv],
                                   fulls=[n1g], outs=[(d, MXU_DTYPE)], seg=nlb, side=(halves_in(early_names), True))
    assemble(early_names, early_halves, "gather_weight_halves")

    u_off, kv_off, kr_off = 2 * d, 2 * d + SSM_WIDTH, 2 * d + SSM_WIDTH + KV_LORA
    q_off = -(-(kr_off + SLOT) // Q_LORA) * Q_LORA
    zw = q_off + Q_LORA
    wi = full["w_in"]
    s0, s1, s2, s3 = Q_LORA, Q_LORA + KV_LORA, Q_LORA + KV_LORA + QK_ROPE, Q_LORA + KV_LORA + QK_ROPE + SSM_WIDTH
    zpad = lambda w_: jnp.zeros((d, w_), MXU_DTYPE)
    win_p = jnp.concatenate([wi[:, s3:], wi[:, s2:s3], wi[:, s0:s1], wi[:, s1:s2], zpad(SLOT - QK_ROPE),
                             zpad(q_off - kr_off - SLOT), wi[:, :s0]], axis=1)
    wuq_p = jnp.pad(full["w_uq"].reshape(Q_LORA, N_HEADS, QK_DIM), ((0, 0), (0, 0), (0, SLOT - QK_DIM))).reshape(Q_LORA, N_HEADS * SLOT)
    wukv3 = full["w_ukv"].reshape(KV_LORA, N_HEADS, QK_NOPE + V_DIM)
    padh = lambda t: jnp.pad(t, ((0, 0), (0, 0), (0, SLOT - t.shape[2]))).reshape(t.shape[0], N_HEADS * SLOT)
    wukv_p = jnp.concatenate([padh(wukv3[:, :, :QK_NOPE]), padh(wukv3[:, :, QK_NOPE:])], axis=1)
    hw = N_HEADS * SLOT
    gain_p = lambda g_: jnp.tile(jnp.pad(g_[0], (0, SLOT - QK_DIM)), N_HEADS)[None, :]
    qg_p, kg_p = gain_p(q_norm_g), gain_p(k_norm_g)

    tok = jnp.arange(nl)
    freqs = ROPE_THETA ** (-jnp.arange(QK_ROPE // 4, dtype=F32) / (QK_ROPE // 4))
    ang = jnp.concatenate([(tok // GRID_W)[:, None] * freqs, (tok % GRID_W)[:, None] * freqs], axis=-1)
    cos_t = jnp.concatenate([jnp.cos(ang), jnp.ones((nc, 16), F32)], axis=0)
    sin_t = jnp.concatenate([jnp.sin(ang), jnp.zeros((nc, 16), F32)], axis=0)
    zl = lambda w_: jnp.zeros((n, w_), F32)
    rope_c = jnp.concatenate([jnp.ones((n, QK_NOPE), F32), cos_t, cos_t, zl(SLOT - QK_DIM)], axis=1)
    rope_sa = jnp.concatenate([zl(QK_NOPE), -sin_t, zl(SLOT - QK_NOPE - 16)], axis=1)
    rope_sb = jnp.concatenate([zl(QK_NOPE + 16), sin_t, zl(SLOT - QK_DIM)], axis=1)

    dirs = (("f", lam_re_f, lam_im_f, log_dt_f, c_re_f, c_im_f, False), ("b", lam_re_b, lam_im_b, log_dt_b, c_re_b, c_im_b, True))
    bbd, cbd_t, cbd_n, bbd_t, lamc, lamc_adj, disc_vjps = [], [], [], [], [], [], []
    for _, l_re, l_im, l_dt, cr_, ci_, rev_ in dirs:
        (lbr, lbi, bbr, bbi), vjp = jax.vjp(_s5_disc, l_re[0], l_im[0], l_dt[0], b_re[0], b_im[0])
        disc_vjps.append(vjp)
        bb = _diag_blocks(jnp.transpose(bbr, (0, 2, 1)), jnp.transpose(bbi, (0, 2, 1))).astype(MXU_DTYPE)
        cc_ = _diag_blocks(cr_[0], -ci_[0]).astype(MXU_DTYPE)
        bbd.append(bb)
        bbd_t.append(_block_transpose(bb))
        cbd_t.append(cc_)
        cbd_n.append(_block_transpose(cc_))
        lamc.append(_lam_consts(lbr, lbi, rev_, False))
        lamc_adj.append(_lam_consts(lbr, lbi, not rev_, True))
    t_scan = tr

    z = _mm(h1, win_p, "nn", "in_proj")
    gl_cb, u_cb, kv_cb, kr_cb, q_cb = 0, u_off // SSM_WIDTH, kv_off // KV_LORA, kr_off // SLOT, q_off // Q_LORA

    (cqn,) = _rowwise(lambda v, g: _rms_fwd(v, g, Q_LORA), name="qa_norm_fwd", nblk=nlb, tr=tr,
                      rows=[(z, q_cb, Q_LORA, 0)], fulls=[q_a_g], outs=[(Q_LORA, MXU_DTYPE)])
    qh = _mm(cqn, wuq_p, "nn", "q_up")

    def qhead_body(qv, cv, sav, sbv, g):
        return jnp.concatenate([_rope_fwd(_rms_fwd(t, g[:, :SLOT], QK_DIM), cv, sav, sbv) for t in _heads(qv)], axis=1)

    rope_rows = lambda: [(rope_c, 0, SLOT, 0), (rope_sa, 0, SLOT, 0), (rope_sb, 0, SLOT, 0)]
    (q_p,) = _rowwise(qhead_body, name="q_head_fwd", nblk=nlb, tr=tr, rows=[(qh, 0, hw, 0)] + rope_rows(),
                      fulls=[qg_p], outs=[(hw, MXU_DTYPE)])

    (ckvn,) = _rowwise(lambda v, g: _rms_fwd(v, g, KV_LORA), name="kva_norm_fwd", nblk=nb, tr=tr,
                       rows=[(z, kv_cb, KV_LORA, 0)], fulls=[kv_a_g], outs=[(KV_LORA, MXU_DTYPE)])
    kvpre = _mm(ckvn, wukv_p, "nn", "kv_up")

    def khead_body(kv_, vv_, krv, cv, sav, sbv, g):
        kpe = pltpu.roll(krv, QK_NOPE, 1)
        ks = [_rope_fwd(_rms_fwd(t + kpe, g[:, :SLOT], QK_DIM), cv, sav, sbv) for t in _heads(kv_)]
        return jnp.concatenate(ks, axis=1), vv_

    k_p, v_p = _rowwise(khead_body, name="k_head_fwd", nblk=nb, tr=tr,
                        rows=[(kvpre, 0, hw, 0), (kvpre, 1, hw, 0), (z, kr_cb, SLOT, 0)] + rope_rows(),
                        fulls=[kg_p], outs=[(hw, MXU_DTYPE), (hw, MXU_DTYPE)])

    scale = QK_DIM ** -0.5
    o_p, lse, ffn_halves = _attn_fwd(q_p, k_p, v_p, nl, scale, side=(halves_in(ffn_names), True))
    assemble(ffn_names, ffn_halves, "gather_ffn_weight_halves")
    wglu, wout, wup, wdown = full["w_glu"], full["w_out"], full["w_up"], full["w_down"]
    wo_p = jnp.pad(full["w_o_attn"].reshape(N_HEADS, V_DIM, d), ((0, 0), (0, SLOT - V_DIM), (0, 0))).reshape(N_HEADS * SLOT, d)
    a_l = _mm(o_p, wo_p, "nn", "attn_out")

    scans = [_s5_scan(z, u_cb, bbd[j], cbd_n[j], lamc[j], t_scan, nl, dirs[j][6], "s5_scan_" + dirs[j][0]) for j in range(2)]
    xs, ydir = [s_[0] for s_ in scans], [s_[1] for s_ in scans]

    def ssm_out_body(uv, a, b, dsk):
        ys = uv * dsk + a + b
        return ys, _gelu(ys)

    ys, ge = _rowwise(ssm_out_body, name="s5_out_fwd", nblk=nlb, tr=tr,
                      rows=[(z, u_cb, SSM_WIDTH, 0), (ydir[0], 0, SSM_WIDTH, 0), (ydir[1], 0, SSM_WIDTH, 0)],
                      fulls=[d_skip], outs=[(SSM_WIDTH, F32), (SSM_WIDTH, MXU_DTYPE)])
    glu_out = _mm(ge, wglu, "nn", "glu_proj")

    def merge_body(ga, gs, av, val, gate):
        return _sigmoid(ga) * av + _sigmoid(gs) * (val * _sigmoid(gate))

    merge_rows = lambda: [(z, 0, d, 0), (z, 1, d, 0), (a_l, 0, d, 0), (glu_out, 0, d, 0), (glu_out, 1, d, 0)]
    (merged,) = _rowwise(merge_body, name="merge_fwd", nblk=nlb, tr=tr, rows=merge_rows(), outs=[(d, MXU_DTYPE)])
    mo = _mm(merged, wout, "nn", "out_proj")
    mod_x = modv[0]

    def norm2_body(xv, mov, m, g):
        g1, sh2, sc2 = m[:, 2 * d:3 * d], m[:, 3 * d:4 * d], m[:, 4 * d:5 * d]
        x1v = xv + g1 * mov
        return x1v, _rms_fwd(x1v, g, d) * (1.0 + sc2) + sh2

    x1, h2 = _rowwise(norm2_body, name="norm2_fwd", nblk=nlb, tr=tr, rows=[(xa, 0, d, 0), (mo, 0, d, 0)],
                      fulls=[mod_x, n2g], outs=[(d, F32), (d, MXU_DTYPE)])
    up = _mm(h2, wup, "nn", "ffn_up")
    cw8 = jnp.zeros((8, f2), F32).at[:3].set(conv_w_full)

    def conv3(t3, w8, off):
        p_, c_, n_ = t3
        return p_ * w8[0:1, off:off + fh] + c_ * w8[1:2, off:off + fh] + n_ * w8[2:3, off:off + fh]

    def conv_fwd_body(val3, gate3, w8, bias):
        val2 = conv3(val3, w8, 0) + bias[:, :fh]
        gate2 = conv3(gate3, w8, fh) + bias[:, fh:]
        return _silu(gate2) * val2

    (act,) = _rowwise(conv_fwd_body, name="conv_fwd", nblk=nlb, tr=tr, rows=[(up, 0, fh, 0), (up, 1, fh, 0)],
                      halo=(0, 1), fulls=[cw8, conv_b], outs=[(fh, MXU_DTYPE)])
    dn = _mm(act, wdown, "nn", "ffn_down")
    tgt = loss_target[0]

    def loss_body(x1v, dnv, tv, m):
        g2 = m[:, 5 * d:6 * d]
        e = x1v + g2 * dnv - tv
        dx2v = e * (1.0 / d)
        return dx2v, dx2v * g2, e * e, dx2v * dnv

    dx2, ddn, loss_acc, dg2_acc = _rowwise(loss_body, name="loss", nblk=nlb, tr=tr,
                                           rows=[(x1, 0, d, 0), (dn, 0, d, 0), (tgt, 0, d, 0)], fulls=[mod_x],
                                           outs=[(d, F32), (d, MXU_DTYPE)], accs=[d, d])
    loss = lax.psum(0.5 / d * jnp.sum(loss_acc), ("x", "y", "c"))

    g_big = {}

    def chip_partials(names_, name):
        pcs = []
        for k_ in names_:
            r_, c_ = weights[k_].shape[1:]
            if k_ in row_sharded:
                p4 = jnp.transpose(g_big[k_].reshape(4, 2, r_ // 2, c_), (1, 0, 2, 3))
            else:
                p4 = jnp.transpose(g_big[k_].reshape(2, r_ // 2, 4, c_), (0, 2, 1, 3))
            pcs.append(p4.reshape(2, 2 * r_, c_))
        out = []
        for k_, p_, got in zip(names_, pcs, _sibling_send(pcs, name)):
            rows4, c_ = got.shape
            own = lax.dynamic_index_in_dim(p_, mc, 0, keepdims=False)
            tr_ = _pick(rows4, (256, 128, 64, 32, 16))
            s32, sb = _rowwise(lambda a, b: (a + b, a + b), name="sum_chip_" + k_, nblk=rows4 // tr_, tr=tr_,
                               rows=[(own, 0, c_, 0), (got, 0, c_, 0)], outs=[(c_, F32), (c_, MXU_DTYPE)])
            out.append((s32, sb, rows4 // 4, c_))
        return out

    def scatter_in(partials):
        return [sb.reshape(4, rh, c_) for _, sb, rh, c_ in partials]

    def chip_sums(names_, partials, recv3):
        out = []
        for k_, (s32, _, rh, c_), r3 in zip(names_, partials, recv3):
            mine = lax.dynamic_slice(s32, (chip * rh, 0), (rh, c_))
            tr_ = _pick(rh, (256, 128, 64, 32, 16))
            (red,) = _rowwise(lambda a, b0, b1, b2: a + b0 + b1 + b2, name="sum_grad_" + k_, nblk=rh // tr_, tr=tr_,
                              rows=[(mine, 0, c_, 0)] + [(r3.reshape(3 * rh, c_), 0, c_, j * rh) for j in range(3)],
                              outs=[(c_, F32)])
            out.append(red)
        return out

    dact = _mm(ddn, wdown, "nt", "ffn_down_dx")
    g_big["w_down"] = _mm(act, ddn, "tn", "ffn_down_dw")

    def conv_bwd_body(val3, gate3, da, w8, bias):
        val2 = conv3(val3, w8, 0) + bias[:, :fh]
        gate2 = conv3(gate3, w8, fh) + bias[:, fh:]
        sg = _sigmoid(gate2)
        dval2 = da * (gate2 * sg)
        dgate2 = da * val2 * (sg * (1.0 + gate2 * (1.0 - sg)))
        du2 = jnp.concatenate([dval2, dgate2], axis=1)
        taps = [jnp.concatenate([dval2 * val3[j], dgate2 * gate3[j]], axis=1) for j in range(3)]
        return du2, du2, taps[0], taps[1], taps[2]

    du2, dcb_acc, dcw0, dcw1, dcw2 = _rowwise(conv_bwd_body, name="conv_bwd", nblk=nlb, tr=tr,
                                              rows=[(up, 0, fh, 0), (up, 1, fh, 0), (dact, 0, fh, 0)], halo=(0, 1),
                                              fulls=[cw8, conv_b], outs=[(f2, F32)], accs=[f2, f2, f2, f2])

    def conv_t_body(dval3, dgate3, w8):
        rev = lambda t3: (t3[2], t3[1], t3[0])
        return jnp.concatenate([conv3(rev(dval3), w8, 0), conv3(rev(dgate3), w8, fh)], axis=1)

    (dup,) = _rowwise(conv_t_body, name="conv_bwd_dx", nblk=nlb, tr=tr, rows=[(du2, 0, fh, 0), (du2, 1, fh, 0)],
                      halo=(0, 1), fulls=[cw8], outs=[(f2, MXU_DTYPE)])
    dh2 = _mm(dup, wup, "nt", "ffn_up_dx")
    g_big["w_up"] = _mm(h2, dup, "tn", "ffn_up_dw")

    def norm2_bwd_body(x1v, dh, dx2v, mov, m, g):
        g1, sc2 = m[:, 2 * d:3 * d], m[:, 4 * d:5 * d]
        y = _rms_fwd(x1v, g, d)
        dxn, dgc = _rms_bwd(x1v, g, dh * (1.0 + sc2), d)
        dx1v = dx2v + dxn
        return dx1v, dx1v * g1, dgc, dh, dh * y, dx1v * mov

    dx1, dmo, dn2g_acc, dsh2_acc, dsc2_acc, dg1_acc = _rowwise(
        norm2_bwd_body, name="norm2_bwd", nblk=nlb, tr=tr,
        rows=[(x1, 0, d, 0), (dh2, 0, d, 0), (dx2, 0, d, 0), (mo, 0, d, 0)], fulls=[mod_x, n2g],
        outs=[(d, F32), (d, MXU_DTYPE)], accs=[d, d, d, d])
    dmerged = _mm(dmo, wout, "nt", "out_proj_dx")
    g_big["w_out"] = _mm(merged, dmo, "tn", "out_proj_dw")

    def merge_bwd_body(ga, gs, av, val, gate, dm):
        sa_, ss_, sg_ = _sigmoid(ga), _sigmoid(gs), _sigmoid(gate)
        s_l = val * sg_
        ds_l = dm * ss_
        dga = dm * av * sa_ * (1.0 - sa_)
        dgs = dm * s_l * ss_ * (1.0 - ss_)
        dval = ds_l * sg_
        dgate = ds_l * val * sg_ * (1.0 - sg_)
        return dm * sa_, jnp.concatenate([dval, dgate], axis=1), jnp.concatenate([dga, dgs], axis=1)

    da_l, dglu, dgl = _rowwise(merge_bwd_body, name="merge_bwd", nblk=nlb, tr=tr,
                               rows=merge_rows() + [(dmerged, 0, d, 0)],
                               outs=[(d, MXU_DTYPE), (2 * d, MXU_DTYPE), (2 * d, MXU_DTYPE)])
    dge = _mm(dglu, wglu, "nt", "glu_proj_dx")
    g_big["w_glu"] = _mm(ge, dglu, "tn", "glu_proj_dw")

    def ssm_out_bwd_body(ysv, dgev, uv, dsk):
        dys_ = dgev * _dgelu(ysv)
        return dys_, dys_ * dsk, dys_ * uv

    dys, du_skip, ddskip_acc = _rowwise(ssm_out_bwd_body, name="s5_out_bwd", nblk=nlb, tr=tr,
                                        rows=[(ys, 0, SSM_WIDTH, 0), (dge, 0, SSM_WIDTH, 0), (z, u_cb, SSM_WIDTH, 0)],
                                        fulls=[d_skip], outs=[(SSM_WIDTH, F32), (SSM_WIDTH, F32)], accs=[SSM_WIDTH])
    s5b = [_s5_bwd(dys, z, u_cb, xs[j], cbd_t[j], bbd_t[j], lamc_adj[j], t_scan, nl, dirs[j][6], "s5_bwd_" + dirs[j][0])
           for j in range(2)]
    du_nat = s5b[0][0] + s5b[1][0] + jnp.concatenate([du_skip, jnp.zeros((nc, SSM_WIDTH), F32)], axis=0)

    do_f = _mm(da_l, wo_p, "nt", "attn_out_dx")
    g_wo_p = _mm(o_p, da_l, "tn", "attn_out_dw")
    g_big["w_o_attn"] = g_wo_p.reshape(N_HEADS, SLOT, d)[:, :V_DIM].reshape(N_HEADS * V_DIM, d)
    ffn_partials = chip_partials(ffn_names, "ffn_grads_to_sibling")

    dq_p, dk_p, dv_p, ffn_recv = _attn_bwd(q_p, k_p, v_p, do_f, o_p, lse, nl, scale, side=(scatter_in(ffn_partials), False))

    def qhead_bwd_body(qv, dqv, cv, sav, sbv, g):
        dxs, dgs = [], []
        for t, dt_ in zip(_heads(qv), _heads(dqv)):
            dx_, dg_ = _rms_bwd(t, g[:, :SLOT], _rope_bwd(dt_, cv, sav, sbv), QK_DIM)
            dxs.append(dx_)
            dgs.append(dg_)
        return jnp.concatenate(dxs, axis=1), jnp.concatenate(dgs, axis=1)

    dqh, dqg_acc = _rowwise(qhead_bwd_body, name="q_head_bwd", nblk=nlb, tr=tr,
                            rows=[(qh, 0, hw, 0), (dq_p, 0, hw, 0)] + rope_rows(), fulls=[qg_p],
                            outs=[(hw, MXU_DTYPE)], accs=[hw])
    dcqn = _mm(dqh, wuq_p, "nt", "q_up_dx")
    g_wuq_p = _mm(cqn, dqh, "tn", "q_up_dw")
    dcq, dqag_acc = _rowwise(lambda v, dy, g: _rms_bwd(v, g, dy, Q_LORA), name="qa_norm_bwd", nblk=nlb, tr=tr,
                             rows=[(z, q_cb, Q_LORA, 0), (dcqn, 0, Q_LORA, 0)], fulls=[q_a_g],
                             outs=[(Q_LORA, MXU_DTYPE)], accs=[Q_LORA])

    def khead_bwd_body(kv_, krv, dkv_, dvv_, cv, sav, sbv, g):
        kpe = pltpu.roll(krv, QK_NOPE, 1)
        lane = lax.broadcasted_iota(jnp.int32, krv.shape, 1)
        dxs, dgs, dkr_ = [], [], jnp.zeros(krv.shape, F32)
        for t, dt_ in zip(_heads(kv_), _heads(dkv_)):
            dx_, dg_ = _rms_bwd(t + kpe, g[:, :SLOT], _rope_bwd(dt_, cv, sav, sbv), QK_DIM)
            dxs.append(jnp.where(lane < QK_NOPE, dx_, 0.0))
            dgs.append(dg_)
            dkr_ = dkr_ + dx_
        dkr_ = jnp.where(lane < QK_ROPE, pltpu.roll(dkr_, SLOT - QK_NOPE, 1), 0.0)
        return jnp.concatenate(dxs + [dvv_], axis=1), dkr_, jnp.concatenate(dgs, axis=1)

    red8 = lambda a: jnp.sum(a, axis=-2)
    head_fold = lambda acc: jnp.sum(red8(acc).reshape(N_HEADS, SLOT), axis=0)[:QK_DIM]
    small_g = {}
    for j, dr in enumerate(dirs):
        sfx = dr[0]
        _, dbbd_j, dcbd_j, dlam_j = s5b[j]
        dl = red8(dlam_j).reshape(N_CG, 2, CG_STATES)
        db_re, db_im = _diag_extract(dbbd_j)
        cot = (dl[:, 0].reshape(SSM_GROUPS, SSM_STATE), dl[:, 1].reshape(SSM_GROUPS, SSM_STATE),
               jnp.transpose(db_re, (0, 2, 1)), jnp.transpose(db_im, (0, 2, 1)))
        g_lre, g_lim, g_ldt, g_bre, g_bim = disc_vjps[j](cot)
        small_g["lam_re_" + sfx], small_g["lam_im_" + sfx], small_g["log_dt_" + sfx] = g_lre, g_lim, g_ldt
        small_g["b_re"] = small_g.get("b_re", 0.0) + g_bre
        small_g["b_im"] = small_g.get("b_im", 0.0) + g_bim
        dc_re, dc_im = _diag_extract(dcbd_j)
        small_g["c_re_" + sfx], small_g["c_im_" + sfx] = dc_re, -dc_im
    small_g.update(norm2_g=red8(dn2g_acc), q_a_g=red8(dqag_acc), q_norm_g=head_fold(dqg_acc), d_skip=red8(ddskip_acc),
                   conv_b=red8(dcb_acc))
    g_convw_full = jnp.stack([red8(dcw0), red8(dcw1), red8(dcw2)])
    late_small = ["c_ctx", "b_mod", "norm1_g", "kv_a_g", "k_norm_g"]
    first_small = ["norm2_g", "q_a_g", "q_norm_g", "lam_re_f", "lam_im_f", "log_dt_f", "c_re_f", "c_im_f", "lam_re_b",
                   "lam_im_b", "log_dt_b", "c_re_b", "c_im_b", "b_re", "b_im", "d_skip", "conv_b"]
    first_pack = _pack([small_g[k] for k in first_small] + [g_convw_full], rows_mult=8)

    (dkvpre, dkr, dkg_acc), (first_all,) = _rowwise(
        khead_bwd_body, name="k_head_bwd", nblk=nb, tr=tr,
        rows=[(kvpre, 0, hw, 0), (z, kr_cb, SLOT, 0), (dk_p, 0, hw, 0), (dv_p, 0, hw, 0)] + rope_rows(),
        fulls=[kg_p], outs=[(2 * hw, MXU_DTYPE), (SLOT, MXU_DTYPE)], accs=[hw], side=([first_pack], ALL8))
    dckvn = _mm(dkvpre, wukv_p, "nt", "kv_up_dx")
    g_wukv_p = _mm(ckvn, dkvpre, "tn", "kv_up_dw")
    dckv, dkvag_acc = _rowwise(lambda v, dy, g: _rms_bwd(v, g, dy, KV_LORA), name="kva_norm_bwd", nblk=nb, tr=tr,
                               rows=[(z, kv_cb, KV_LORA, 0), (dckvn, 0, KV_LORA, 0)], fulls=[kv_a_g],
                               outs=[(KV_LORA, MXU_DTYPE)], accs=[KV_LORA])

    padc = lambda t: jnp.concatenate([t, jnp.zeros((nc, t.shape[1]), t.dtype)], axis=0)
    dz = jnp.concatenate([padc(dgl), du_nat.astype(MXU_DTYPE), dckv, dkr,
                          jnp.zeros((n, q_off - kr_off - SLOT), MXU_DTYPE), padc(dcq)], axis=1)
    gwi = _mm(h1, dz, "tn", "in_proj_dw")
    g_big["w_in"] = jnp.concatenate([gwi[:, q_off:q_off + Q_LORA], gwi[:, kv_off:kv_off + KV_LORA],
                                     gwi[:, kr_off:kr_off + QK_ROPE], gwi[:, u_off:u_off + SSM_WIDTH], gwi[:, :2 * d]], axis=1)
    g_big["w_uq"] = g_wuq_p.reshape(Q_LORA, N_HEADS, SLOT)[:, :, :QK_DIM].reshape(Q_LORA, N_HEADS * QK_DIM)
    gk3 = g_wukv_p[:, :hw].reshape(KV_LORA, N_HEADS, SLOT)[:, :, :QK_NOPE]
    gv3 = g_wukv_p[:, hw:].reshape(KV_LORA, N_HEADS, SLOT)[:, :, :V_DIM]
    g_big["w_ukv"] = jnp.concatenate([gk3, gv3], axis=2).reshape(KV_LORA, N_HEADS * (QK_NOPE + V_DIM))
    early_partials = chip_partials(early_names, "grads_to_sibling")
    dh1 = _mm(dz, win_p, "nt", "in_proj_dx")

    def norm1_bwd_body(xv, dh, dx1v, m, g):
        sc1 = m[:, d:2 * d]
        y = _rms_fwd(xv, g, d)
        dxn, dgc = _rms_bwd(xv, g, dh * (1.0 + sc1), d)
        return dxn + dx1v, dgc, dh, dh * y

    (dxa, dn1g_acc, dsh1_acc, dsc1_acc), early_recv = _rowwise(
        norm1_bwd_body, name="norm1_bwd", nblk=nb, tr=tr, rows=[(xa, 0, d, 0), (dh1, 0, d, 0), (dx1, 0, d, 0)],
        sels=[modv], fulls=[n1g], outs=[(d, F32)], accs=[d, d, d], seg=nlb, side=(scatter_in(early_partials), False))
    grad_x = dxa[:nl][None]

    dmod_own =jnp.concatenate([red8(dsh1_acc[0]), red8(dsc1_acc[0]), red8(dg1_acc), red8(dsh2_acc), red8(dsc2_acc), red8(dg2_acc)])
    dmod_ctx = jnp.concatenate([red8(dsh1_acc[1]), red8(dsc1_acc[1]), jnp.zeros((4 * d,), F32)])
    dm_in = jnp.concatenate([dmod_own[None, :], dmod_ctx[None, :], jnp.zeros((6, d6), F32)], axis=0)
    (dm_all,) = _exchange8([dm_in], "gather_dmod")
    dm_own_sh = lax.dynamic_slice(dm_all[:, 0, :], (0, chip * csh), (8, csh))
    dm_ctx_sh = lax.dynamic_slice(dm_all[:, 1, :], (0, chip * csh), (8, csh))

    def mod_bwd_body(c_ref, own_ref, ctx_ref, w_ref, gw_ref, gb_ref, gc_ref):
        cv = c_ref[...]
        a = _silu(cv).astype(MXU_DTYPE)
        own = own_ref[...]
        ctx_tot = ctx_ref[0:1, :]
        for j in range(1, 8):
            ctx_tot = ctx_tot + ctx_ref[j:j + 1, :]
        g16 = jnp.concatenate([own, jnp.broadcast_to(ctx_tot, own.shape)], axis=0)
        rid = lax.broadcasted_iota(jnp.int32, g16.shape, 0)
        g16 = jnp.where(rid <= 8, g16, 0.0)
        gw_ref[...] = lax.dot_general(a, g16.astype(MXU_DTYPE), (((0,), (0,)), ((), ())), preferred_element_type=F32)
        gb_ref[...] = jnp.broadcast_to(jnp.sum(own, axis=0, keepdims=True) + ctx_tot, gb_ref.shape)
        gc = lax.dot_general(jnp.broadcast_to(ctx_tot, own.shape).astype(MXU_DTYPE), w_ref[...].astype(MXU_DTYPE),
                             (((1,), (1,)), ((), ())), preferred_element_type=F32)
        gc_ref[...] = gc * _dsilu(cv[8:9, :])

    g_wmod, g_bmod_sh, g_cctx_part = pl.pallas_call(
        mod_bwd_body, name="mod_bwd", out_shape=[_sds((d, csh), F32), _sds((8, csh), F32), _sds((8, d), F32)],
        compiler_params=pltpu.CompilerParams(vmem_limit_bytes=VMEM_LIMIT))(cs16, dm_own_sh, dm_ctx_sh, w_mod[0])
    north = (mc == 0).astype(F32)
    g_bmod_part = lax.dynamic_update_slice(jnp.zeros((1, d6), F32), g_bmod_sh[0:1] * north, (0, chip * csh))
    g_cctx_part = g_cctx_part[0] * north

    small_g.update(c_ctx=g_cctx_part, b_mod=g_bmod_part[0], norm1_g=red8(dn1g_acc[0]) + red8(dn1g_acc[1]),
                   kv_a_g=red8(dkvag_acc), k_norm_g=head_fold(dkg_acc))
    (late_all,) = _exchange8([_pack([small_g[k] for k in late_small], rows_mult=8)], "gather_small_grads")
    sg_first = _unpack(_sum8(first_all, "sum_small_grads"), [weights[k].shape for k in first_small] + [(3, f2)])
    sg_late = _unpack(_sum8(late_all, "sum_last_small_grads"), [weights[k].shape for k in late_small])
    g_small = dict(zip(first_small + late_small, sg_first[:-1] + sg_late))
    g_small["conv_w"] = lax.dynamic_slice(sg_first[-1], (0, chip * cwid), (3, cwid))[None]
    small_names = first_small + late_small

    reduced = dict(zip(early_names + ffn_names, chip_sums(early_names, early_partials, early_recv)
                       + chip_sums(ffn_names, ffn_partials, ffn_recv)))
    both = _sibling_exchange([reduced[k_] for k_ in big_names], "exchange_halves")
    g_sh = {k_: b_.reshape((1,) + weights[k_].shape[1:]) for k_, b_ in zip(big_names, both)}
    g_sh["w_mod"] = g_wmod[None]

    grads = {**g_sh, **g_small}
    outs_d, outs_m, outs_v = {}, {}, {}
    for k_ in ["w_mod"] + big_names:
        shp = weights[k_].shape
        res = _adamw(*[t.reshape(shp[1:]) for t in (grads[k_], weights[k_], mom_m[k_], mom_v[k_])], "adamw_" + k_)
        for dst, buf in zip((outs_d, outs_m, outs_v), res):
            dst[k_] = buf.reshape(shp)
    adam_small = small_names + ["conv_w"]
    shapes = [weights[k_].shape for k_ in adam_small]
    res = _adamw(*[_pack([src[k_] for k_ in adam_small], rows_mult=8) for src in (grads, weights, mom_m, mom_v)], "adamw_small")
    for dst, buf in zip((outs_d, outs_m, outs_v), res):
        dst.update(zip(adam_small, _unpack(buf, shapes)))
    grads = {k_: grads[k_].reshape(weights[k_].shape) for k_ in names}
    return (loss, grad_x, *[grads[k_] for k_ in names], *[outs_d[k_] for k_ in names],
            *[outs_m[k_] for k_ in names], *[outs_v[k_] for k_ in names])
```

```python
import math

import numpy as np
import jax
import jax.numpy as jnp
from jax import lax
from jax.experimental import pallas as pl
from jax.experimental.pallas import tpu as pltpu

F32 = jnp.float32
MXU_DTYPE = jnp.bfloat16
MESH = pl.DeviceIdType.MESH

EPS = 1e-6
N_HEADS = 8
QK_NOPE = 64
QK_ROPE = 32
QK_DIM = QK_NOPE + QK_ROPE
V_DIM = 64
SLOT = 128
Q_LORA = 384
KV_LORA = 256
GRID_W = 64
ROPE_THETA = 10000.0
SSM_WIDTH = 512
SSM_GROUP = 16
SSM_GROUPS = 32
SSM_STATE = 64
N_STATE = SSM_GROUPS * SSM_STATE
CG_STATES = 512
N_CG = N_STATE // CG_STATES
CG_CHANNELS = SSM_WIDTH // N_CG
SCAN_LANES = 512
PACK_W = 1024

ADAM_LR = 0.001
ADAM_B1 = 0.9
ADAM_B2 = 0.999
ADAM_EPS = 1e-08
ADAM_WD = 0.01
ADAM_STEP = 10

VMEM_LIMIT = 56 * 1024 * 1024
LOG2E = 1.4426950408889634


def _pick(n, cands):
    for c in cands:
        if c <= n and n % c == 0:
            return c
    return n


def _cparams(sem):
    return pltpu.CompilerParams(dimension_semantics=sem, vmem_limit_bytes=VMEM_LIMIT)


def _sds(shape, dtype):
    return jax.ShapeDtypeStruct(tuple(shape), dtype)


_K_CANDS = (2816, 2048, 1536, 1408, 1280, 1152, 1024, 896, 768, 704, 640, 512, 384, 256, 128, 64, 32, 16)
_M_CANDS = (2048, 1408, 1024, 768, 512, 384, 256, 128, 64, 32, 16)
_N_CANDS = (1408, 1152, 1024, 768, 512, 384, 256, 128)
MM_VMEM_BUDGET = 40 * 1024 * 1024


def _mm_tiles(m, n, k_opts, a_bytes, b_bytes, o_bytes, m_cands):
    tn = n if n <= _N_CANDS[0] else _pick(n, _N_CANDS)
    for tk in k_opts:
        for tm in ((m,) if m <= m_cands[0] else ()) + tuple(t for t in m_cands if t < m and m % t == 0):
            if 2 * (tm * tk * a_bytes + tk * tn * b_bytes + tm * tn * o_bytes) + tm * tn * 4 <= MM_VMEM_BUDGET:
                return tm, tn, tk
    raise ValueError("no matmul tiling fits")


def _mm(a, b, mode, name, out_dtype=F32, rows=None, a_off=0, b_off=0):
    a_bytes, b_bytes, o_bytes = a.dtype.itemsize, b.dtype.itemsize, jnp.dtype(out_dtype).itemsize
    if mode == "tn":
        t_rows = rows or a.shape[0]
        m, n = a.shape[1], b.shape[1]
        k_opts = tuple(t for t in _K_CANDS if t <= t_rows and t_rows % t == 0) or (t_rows,)
        tm, tn, tk = _mm_tiles(m, n, k_opts, a_bytes, b_bytes, o_bytes, _M_CANDS[1:])
        nk = t_rows // tk
        ao, bo = a_off // tk, b_off // tk
        grid = (m // tm, n // tn, nk)
        in_specs = [pl.BlockSpec((tk, tm), lambda i, j, k: (k + ao, i)),
                    pl.BlockSpec((tk, tn), lambda i, j, k: (k + bo, j))]
        dn = (((0,), (0,)), ((), ()))
    else:
        m = rows or a.shape[0]
        kdim = a.shape[1]
        n = b.shape[1] if mode == "nn" else b.shape[0]
        k_opts = (kdim,) + tuple(t for t in _K_CANDS if t < kdim and kdim % t == 0)
        tm, tn, tk = _mm_tiles(m, n, k_opts, a_bytes, b_bytes, o_bytes, _M_CANDS)
        nk = kdim // tk
        ao = a_off // tm
        grid = (m // tm, n // tn, nk)
        if mode == "nn":
            in_specs = [pl.BlockSpec((tm, tk), lambda i, j, k: (i + ao, k)),
                        pl.BlockSpec((tk, tn), lambda i, j, k: (k, j))]
            dn = (((1,), (0,)), ((), ()))
        else:
            in_specs = [pl.BlockSpec((tm, tk), lambda i, j, k: (i + ao, k)),
                        pl.BlockSpec((tn, tk), lambda i, j, k: (j, k))]
            dn = (((1,), (1,)), ((), ()))
    use_scratch = nk > 1 and out_dtype != F32

    def body(a_ref, b_ref, o_ref, *scr):
        r = lax.dot_general(a_ref[...].astype(MXU_DTYPE), b_ref[...].astype(MXU_DTYPE), dn,
                            preferred_element_type=F32)
        if nk == 1:
            o_ref[...] = r.astype(o_ref.dtype)
        else:
            k = pl.program_id(2)
            acc = scr[0] if use_scratch else o_ref

            @pl.when(k == 0)
            def _():
                acc[...] = r

            @pl.when(k > 0)
            def _():
                acc[...] += r

            if use_scratch:
                @pl.when(k == nk - 1)
                def _():
                    o_ref[...] = acc[...].astype(o_ref.dtype)

    return pl.pallas_call(
        body, name=name, grid=grid, in_specs=in_specs,
        out_specs=pl.BlockSpec((tm, tn), lambda i, j, k: (i, j)),
        out_shape=_sds((m, n), out_dtype),
        scratch_shapes=[pltpu.VMEM((tm, tn), F32)] if use_scratch else [],
        compiler_params=_cparams(("parallel", "parallel", "arbitrary")),
    )(a, b)


def _rowwise(body, *, name, nblk, tr, rows=(), halo=(), sels=(), fulls=(), outs=(), accs=(), seg=None, side=None):
    n_rows, n_sel, n_full, n_out, n_acc = len(rows), len(sels), len(fulls), len(outs), len(accs)
    halo = tuple(halo)
    maxw = max([r[2] for r in rows] + [o[0] for o in outs] + list(accs))
    sr = _pick(tr, tuple(s for s in (256, 128, 64, 32, 16) if s * maxw <= 131072) or (16,))
    nsub = tr // sr
    total8 = nblk * tr // 8

    def seg_of(i):
        return jnp.where(i >= seg, 1, 0) if seg is not None else 0

    in_specs, operands = [], []
    for arr, cb, w, roff in rows:
        ob = roff // tr
        last = arr.shape[0] // tr - 1
        in_specs.append(pl.BlockSpec((tr, w), lambda i, cb=cb, ob=ob, last=last: (jnp.minimum(i + ob, last), cb)))
        operands.append(arr)
    for h in halo:
        arr, cb, w, roff = rows[h]
        o8, t8 = roff // 8, tr // 8
        in_specs.append(pl.BlockSpec((8, w), lambda i, cb=cb, o8=o8, t8=t8: (jnp.maximum(i * t8 - 1, 0) + o8, cb)))
        in_specs.append(pl.BlockSpec((8, w), lambda i, cb=cb, o8=o8, t8=t8: (jnp.minimum((i + 1) * t8, total8 - 1) + o8, cb)))
        operands += [arr, arr]
    for arr in sels:
        in_specs.append(pl.BlockSpec((None,) + arr.shape[1:], lambda i: (seg_of(i), 0, 0)))
        operands.append(arr)
    for arr in fulls:
        in_specs.append(pl.BlockSpec(arr.shape, lambda i: (0, 0)))
        operands.append(arr)
    out_specs, out_shape = [], []
    for w, dt in outs:
        out_specs.append(pl.BlockSpec((tr, w), lambda i: (i, 0)))
        out_shape.append(_sds((nblk * tr, w), dt))
    for w in accs:
        if seg is None:
            out_specs.append(pl.BlockSpec((8, w), lambda i: (0, 0)))
            out_shape.append(_sds((8, w), F32))
        else:
            out_specs.append(pl.BlockSpec((None, 8, w), lambda i: (seg_of(i), 0, 0)))
            out_shape.append(_sds((2, 8, w), F32))
    n_halo = 2 * len(halo)

    def kern(*refs):
        row_refs = refs[:n_rows]
        halo_refs = refs[n_rows:n_rows + n_halo]
        sel_refs = refs[n_rows + n_halo:n_rows + n_halo + n_sel]
        full_refs = refs[n_rows + n_halo + n_sel:n_rows + n_halo + n_sel + n_full]
        o0 = n_rows + n_halo + n_sel + n_full
        out_refs = refs[o0:o0 + n_out]
        acc_refs = refs[o0 + n_out:o0 + n_out + n_acc]
        i = pl.program_id(0)
        if n_acc:
            first = (i == 0) if seg is None else ((i == 0) | (i == seg))

            @pl.when(first)
            def _():
                for a_ref in acc_refs:
                    a_ref[...] = jnp.zeros(a_ref.shape, F32)

        def sub(s, carry):
            r0 = pl.multiple_of(s * sr, sr)
            vals = []
            for idx, r in enumerate(row_refs):
                cur = r[pl.ds(r0, sr), :]
                if idx in halo:
                    hp = halo_refs[2 * halo.index(idx)]
                    hn = halo_refs[2 * halo.index(idx) + 1]
                    cur = cur.astype(F32)
                    rid = lax.broadcasted_iota(jnp.int32, cur.shape, 0)
                    lo = r[pl.ds(pl.multiple_of(jnp.maximum(r0 - 8, 0), 8), 8), :].astype(F32)
                    lo = jnp.where(s == 0, hp[...].astype(F32), lo)
                    lo = jnp.where((s == 0) & (i == 0), 0.0, lo)
                    hi = r[pl.ds(pl.multiple_of(jnp.minimum(r0 + sr, tr - 8), 8), 8), :].astype(F32)
                    hi = jnp.where(s == nsub - 1, hn[...].astype(F32), hi)
                    hi = jnp.where((s == nsub - 1) & (i == nblk - 1), 0.0, hi)
                    prev = jnp.where(rid == 0, jnp.broadcast_to(lo[7:8, :], cur.shape), pltpu.roll(cur, 1, 0))
                    nxt = jnp.where(rid == sr - 1, jnp.broadcast_to(hi[0:1, :], cur.shape), pltpu.roll(cur, sr - 1, 0))
                    vals.append((prev, cur, nxt))
                else:
                    vals.append(cur)
            res = body(*vals, *[r[...] for r in sel_refs], *[r[...] for r in full_refs])
            if not isinstance(res, (tuple, list)):
                res = (res,)
            for o_ref, v in zip(out_refs, res[:n_out]):
                o_ref[pl.ds(r0, sr), :] = v.astype(o_ref.dtype)
            for a_ref, v in zip(acc_refs, res[n_out:]):
                a_ref[...] += jnp.sum(v.astype(F32).reshape(sr // 8, 8, v.shape[-1]), axis=0)
            return carry

        lax.fori_loop(0, nsub, sub, 0)

    n_res = n_out + n_acc
    kern, s_in, s_out, s_shape, s_scr = _ride_along(kern, len(operands), n_res, 0, (nblk,), side)
    res = pl.pallas_call(
        kern, name=name, grid=(nblk,), in_specs=in_specs + s_in, out_specs=out_specs + s_out,
        out_shape=out_shape + s_shape, scratch_shapes=s_scr, compiler_params=_cparams(("arbitrary",)),
    )(*operands, *(side[0] if side else ()))
    return res if side is None else (res[:n_res], res[n_res:])


def _sigmoid(x):
    return 1.0 / (1.0 + jnp.exp(-x))


def _silu(x):
    return x * _sigmoid(x)


def _dsilu(x):
    s = _sigmoid(x)
    return s * (1.0 + x * (1.0 - s))


_GELU_K = math.sqrt(2.0 / math.pi)


def _gelu(x):
    return 0.5 * x * (1.0 + jnp.tanh(_GELU_K * (x + 0.044715 * x * x * x)))


def _dgelu(x):
    t = jnp.tanh(_GELU_K * (x + 0.044715 * x * x * x))
    return 0.5 * (1.0 + t) + 0.5 * x * (1.0 - t * t) * _GELU_K * (1.0 + 3.0 * 0.044715 * x * x)


def _rms_fwd(x, g, width):
    r = lax.rsqrt(jnp.sum(x * x, axis=-1, keepdims=True) * (1.0 / width) + EPS)
    return x * r * g


def _rms_bwd(x, g, dy, width):
    r = lax.rsqrt(jnp.sum(x * x, axis=-1, keepdims=True) * (1.0 / width) + EPS)
    xn = x * r
    dyg = dy * g
    dx = r * (dyg - xn * (jnp.sum(dyg * xn, axis=-1, keepdims=True) * (1.0 / width)))
    return dx, dy * xn


def _rope_fwd(y, c, sa, sb):
    return y * c + pltpu.roll(y, SLOT - 16, 1) * sa + pltpu.roll(y, 16, 1) * sb


def _rope_bwd(d, c, sa, sb):
    return d * c + pltpu.roll(d * sa, 16, 1) + pltpu.roll(d * sb, SLOT - 16, 1)


def _heads(v):
    return [v[:, h * SLOT:(h + 1) * SLOT] for h in range(N_HEADS)]


def _attn_fwd(q, k, v, nl, scale, side=None):
    n = k.shape[0]
    tq = _pick(nl, (4096, 2048, 1024, 512, 256, 128))
    tk = _pick(n, (2816, 1408, 1152, 768, 384, 256, 128))
    sub = min(tq, 512)
    nk = n // tk
    rep = tk // SLOT
    c = scale * LOG2E

    def body(q_ref, k_ref, v_ref, o_ref, lse_ref, m_sc, acc_sc):
        ki = pl.program_id(2)

        @pl.when(ki == 0)
        def _():
            m_sc[...] = jnp.full(m_sc.shape, -jnp.inf, F32)
            acc_sc[...] = jnp.zeros(acc_sc.shape, F32)

        kb, vb = k_ref[...], v_ref[...]
        for sb in range(tq // sub):
            rows = slice(sb * sub, (sb + 1) * sub)
            s = lax.dot_general(q_ref[rows, :], kb, (((1,), (1,)), ((), ())), preferred_element_type=F32)
            m_prev = m_sc[rows, :]
            m_new = jnp.maximum(m_prev, jnp.max(s, axis=1, keepdims=True) * c)
            alpha = jnp.exp2(m_prev - m_new)
            p = jnp.exp2(s * c - jnp.tile(m_new, (1, rep)))
            acc_sc[rows, :] = alpha * acc_sc[rows, :] + jnp.dot(p.astype(MXU_DTYPE), vb, preferred_element_type=F32)
            m_sc[rows, :] = m_new

        @pl.when(ki == nk - 1)
        def _():
            acc = acc_sc[...]
            l = acc[:, V_DIM:V_DIM + 1]
            o_ref[...] = (acc / l).astype(o_ref.dtype)
            lse_ref[...] = jnp.transpose(m_sc[...] + jnp.log2(jnp.broadcast_to(l, acc.shape)))[0:8, :]

    grid = (N_HEADS, nl // tq, nk)
    body, s_in, s_out, s_shape, s_scr = _ride_along(body, 3, 2, 2, grid, side)
    res = pl.pallas_call(
        body, name="attn_fwd", grid=grid,
        in_specs=[pl.BlockSpec((tq, SLOT), lambda h, i, j: (i, h)),
                  pl.BlockSpec((tk, SLOT), lambda h, i, j: (j, h)),
                  pl.BlockSpec((tk, SLOT), lambda h, i, j: (j, h))] + s_in,
        out_specs=[pl.BlockSpec((tq, SLOT), lambda h, i, j: (i, h)),
                   pl.BlockSpec((None, 8, tq), lambda h, i, j: (h, 0, i))] + s_out,
        out_shape=[_sds((nl, N_HEADS * SLOT), MXU_DTYPE), _sds((N_HEADS, 8, nl), F32)] + s_shape,
        scratch_shapes=[pltpu.VMEM((tq, SLOT), F32), pltpu.VMEM((tq, SLOT), F32)] + s_scr,
        compiler_params=_cparams(("arbitrary", "arbitrary", "arbitrary")),
    )(q, k, v, *(side[0] if side else ()))
    return res[0], res[1], res[2:]


def _attn_bwd(q, k, v, do, o, lse_t, nl, scale, side=None):
    n = k.shape[0]
    tq = _pick(nl, (2048, 1024, 512, 256, 128))
    tk = _pick(n, (2816, 1408, 1152, 768, 384, 256, 128))
    sub = _pick(tk, (256, 128))
    nq, nk = nl // tq, n // tk
    c = scale * LOG2E

    def body(q_ref, k_ref, v_ref, do_ref, o_ref, lse_ref, dq_ref, dk_ref, dv_ref, dq_acc, dk_acc, dv_acc):
        ki, qi = pl.program_id(1), pl.program_id(2)

        @pl.when((ki == 0) & (qi == 0))
        def _():
            dq_acc[...] = jnp.zeros(dq_acc.shape, F32)

        @pl.when(qi == 0)
        def _():
            dk_acc[...] = jnp.zeros(dk_acc.shape, F32)
            dv_acc[...] = jnp.zeros(dv_acc.shape, F32)

        qb, dof = q_ref[...], do_ref[...]
        dob = dof.astype(MXU_DTYPE)
        lse_r = lse_ref[0:1, :]
        dl_r = jnp.sum(jnp.transpose(dof * o_ref[...].astype(F32)), axis=0, keepdims=True)
        dq_part = None
        for sb in range(tk // sub):
            rows = slice(sb * sub, (sb + 1) * sub)
            kb = k_ref[rows, :]
            s_t = lax.dot_general(kb, qb, (((1,), (1,)), ((), ())), preferred_element_type=F32)
            p_t = jnp.exp2(s_t * c - lse_r)
            dp_t = lax.dot_general(v_ref[rows, :], dob, (((1,), (1,)), ((), ())), preferred_element_type=F32)
            ds_t = (p_t * (dp_t - dl_r) * scale).astype(MXU_DTYPE)
            dv_acc[rows, :] += jnp.dot(p_t.astype(MXU_DTYPE), dob, preferred_element_type=F32)
            dk_acc[rows, :] += jnp.dot(ds_t, qb, preferred_element_type=F32)
            part = lax.dot_general(kb, ds_t, (((0,), (0,)), ((), ())), preferred_element_type=F32)
            dq_part = part if dq_part is None else dq_part + part
        c0 = pl.multiple_of(qi * tq, tq)
        dq_acc[:, pl.ds(c0, tq)] += dq_part

        @pl.when(ki == nk - 1)
        def _():
            dq_ref[...] = jnp.transpose(dq_acc[:, pl.ds(c0, tq)])

        @pl.when(qi == nq - 1)
        def _():
            dk_ref[...] = dk_acc[...]
            dv_ref[...] = dv_acc[...]

    grid = (N_HEADS, nk, nq)
    body, s_in, s_out, s_shape, s_scr = _ride_along(body, 6, 3, 3, grid, side)
    res = pl.pallas_call(
        body, name="attn_bwd", grid=grid,
        in_specs=[pl.BlockSpec((tq, SLOT), lambda h, j, i: (i, h)),
                  pl.BlockSpec((tk, SLOT), lambda h, j, i: (j, h)),
                  pl.BlockSpec((tk, SLOT), lambda h, j, i: (j, h)),
                  pl.BlockSpec((tq, SLOT), lambda h, j, i: (i, h)),
                  pl.BlockSpec((tq, SLOT), lambda h, j, i: (i, h)),
                  pl.BlockSpec((None, 8, tq), lambda h, j, i: (h, 0, i))] + s_in,
        out_specs=[pl.BlockSpec((tq, SLOT), lambda h, j, i: (jnp.where(j == nk - 1, i, 0), h)),
                   pl.BlockSpec((tk, SLOT), lambda h, j, i: (j, h)),
                   pl.BlockSpec((tk, SLOT), lambda h, j, i: (j, h))] + s_out,
        out_shape=[_sds((nl, N_HEADS * SLOT), F32), _sds((n, N_HEADS * SLOT), F32), _sds((n, N_HEADS * SLOT), F32)] + s_shape,
        scratch_shapes=[pltpu.VMEM((SLOT, nl), F32), pltpu.VMEM((tk, SLOT), F32), pltpu.VMEM((tk, SLOT), F32)] + s_scr,
        compiler_params=_cparams(("arbitrary", "arbitrary", "arbitrary")),
    )(q, k, v, do, o, lse_t, *(side[0] if side else ()))
    return res[0], res[1], res[2], res[3:]


def _scan_consts(c_ref, lg):
    cs = slice(lg * SCAN_LANES, (lg + 1) * SCAN_LANES)
    return [c_ref[8 * kk:8 * kk + 8, cs] for kk in range(8)]


def _tile_scan(br, bi, consts, reverse):
    p1r, p1i, p2r, p2i, p4r, p4i = consts[:6]
    for pr, pi, kk in ((p1r, p1i, 1), (p2r, p2i, 2), (p4r, p4i, 4)):
        sh = (8 - kk) if reverse else kk
        sr_, si_ = pltpu.roll(br, sh, 0), pltpu.roll(bi, sh, 0)
        br, bi = br + pr * sr_ - pi * si_, bi + pr * si_ + pi * sr_
    return br, bi


def _seq_chunk(j, nch, nlc, reverse):
    return (nch - 1 - j) if reverse else (j + nlc) % nch


def _s5_scan(z, u_cb, bbd, cbd_n, lamc, t_rows, nl, reverse, name):
    n = z.shape[0]
    nch, nlc = n // t_rows, nl // t_rows
    ntile = t_rows // 8
    w = SCAN_LANES
    edge = 0 if reverse else 7
    ucb = u_cb * (SSM_WIDTH // CG_CHANNELS)

    def chunk(j):
        return _seq_chunk(j, nch, nlc, reverse)

    def body(u_ref, b_ref, cn_ref, c_ref, xs_ref, y_ref, carry):
        j = pl.program_id(1)

        @pl.when(j == 0)
        def _():
            carry[...] = jnp.zeros(carry.shape, F32)

        xs_ref[...] = jnp.dot(u_ref[...].astype(MXU_DTYPE), b_ref[...], preferred_element_type=F32)
        for lg in range(CG_STATES // w):
            re = slice(lg * w, (lg + 1) * w)
            im = slice(CG_STATES + lg * w, CG_STATES + (lg + 1) * w)
            consts = _scan_consts(c_ref, lg)
            qr, qi = consts[6], consts[7]

            def tile(tt, st):
                cr, ci = st
                t = (ntile - 1 - tt) if reverse else tt
                r0 = pl.multiple_of(t * 8, 8)
                br, bi = _tile_scan(xs_ref[pl.ds(r0, 8), re], xs_ref[pl.ds(r0, 8), im], consts, reverse)
                lr = jnp.broadcast_to(cr[edge:edge + 1, :], br.shape)
                li = jnp.broadcast_to(ci[edge:edge + 1, :], bi.shape)
                xr = br + qr * lr - qi * li
                xi = bi + qr * li + qi * lr
                xs_ref[pl.ds(r0, 8), re] = xr
                xs_ref[pl.ds(r0, 8), im] = xi
                return xr, xi

            cr, ci = lax.fori_loop(0, ntile, tile, (carry[:, re], carry[:, im]))
            carry[:, re] = cr
            carry[:, im] = ci
        y_ref[...] = jnp.dot(xs_ref[...].astype(MXU_DTYPE), cn_ref[...], preferred_element_type=F32)

    cw = 2 * CG_STATES
    return pl.pallas_call(
        body, name=name, grid=(N_CG, nch),
        in_specs=[pl.BlockSpec((t_rows, CG_CHANNELS), lambda g, j: (chunk(j), ucb + g)),
                  pl.BlockSpec((CG_CHANNELS, cw), lambda g, j: (g, 0)),
                  pl.BlockSpec((cw, CG_CHANNELS), lambda g, j: (g, 0)),
                  pl.BlockSpec((64, CG_STATES), lambda g, j: (0, g))],
        out_specs=[pl.BlockSpec((t_rows, cw), lambda g, j: (chunk(j), g)),
                   pl.BlockSpec((t_rows, CG_CHANNELS), lambda g, j: (chunk(j), g))],
        out_shape=[_sds((n, 2 * N_STATE), F32), _sds((n, SSM_WIDTH), F32)],
        scratch_shapes=[pltpu.VMEM((8, cw), F32)],
        compiler_params=_cparams(("arbitrary", "arbitrary")),
    )(z, bbd, cbd_n, lamc)


def _s5_bwd(dys, z, u_cb, xs, cbd_t, bbd_t, lamc_adj, t_rows, nl, reverse, name):
    n = z.shape[0]
    nch, nlc = n // t_rows, nl // t_rows
    ntile = t_rows // 8
    t8 = t_rows // 8
    w = SCAN_LANES
    cw = 2 * CG_STATES
    adj_rev = not reverse
    edge = 0 if adj_rev else 7

    def chunk(j):
        return _seq_chunk(nch - 1 - j, nch, nlc, reverse)

    def halo_blk(j):
        if reverse:
            return jnp.minimum((chunk(j) + 1) * t8, n // 8 - 1)
        return (_seq_chunk(jnp.maximum(nch - 2 - j, 0), nch, nlc, False) + 1) * t8 - 1

    def body(dy_ref, u_ref, xs_ref, halo_ref, ct_ref, bt_ref, c_ref, du_ref, db_ref, dc_ref, dl_ref, gbuf, carry):
        j = pl.program_id(1)
        start = j == nch - 1

        @pl.when(j == 0)
        def _():
            carry[...] = jnp.zeros(carry.shape, F32)
            db_ref[...] = jnp.zeros(db_ref.shape, F32)
            dc_ref[...] = jnp.zeros(dc_ref.shape, F32)
            dl_ref[...] = jnp.zeros(dl_ref.shape, F32)

        dy = jnp.where(chunk(j) < nlc, dy_ref[...], 0.0).astype(MXU_DTYPE)
        gbuf[...] = jnp.dot(dy, ct_ref[...], preferred_element_type=F32)
        dc_ref[...] += lax.dot_general(dy, xs_ref[...].astype(MXU_DTYPE), (((0,), (0,)), ((), ())),
                                       preferred_element_type=F32)
        for lg in range(CG_STATES // w):
            re = slice(lg * w, (lg + 1) * w)
            im = slice(CG_STATES + lg * w, CG_STATES + (lg + 1) * w)
            consts = _scan_consts(c_ref, lg)
            qr, qi = consts[6], consts[7]
            hr, hi = halo_ref[:, re], halo_ref[:, im]

            def tile(tt, st):
                gcr, gci, ar, ai = st
                t = (ntile - 1 - tt) if adj_rev else tt
                r0 = pl.multiple_of(t * 8, 8)
                br, bi = _tile_scan(gbuf[pl.ds(r0, 8), re], gbuf[pl.ds(r0, 8), im], consts, adj_rev)
                lr = jnp.broadcast_to(gcr[edge:edge + 1, :], br.shape)
                li = jnp.broadcast_to(gci[edge:edge + 1, :], bi.shape)
                gr = br + qr * lr - qi * li
                gi = bi + qr * li + qi * lr
                gbuf[pl.ds(r0, 8), re] = gr
                gbuf[pl.ds(r0, 8), im] = gi
                xr, xi = xs_ref[pl.ds(r0, 8), re], xs_ref[pl.ds(r0, 8), im]
                rid = lax.broadcasted_iota(jnp.int32, xr.shape, 0)
                if reverse:
                    last = t == ntile - 1
                    rn = pl.multiple_of(jnp.minimum(r0 + 8, t_rows - 8), 8)
                    nbr = jnp.where(last, hr, xs_ref[pl.ds(rn, 8), re])
                    nbi = jnp.where(last, hi, xs_ref[pl.ds(rn, 8), im])
                    nbr = jnp.where(last & start, 0.0, nbr)
                    nbi = jnp.where(last & start, 0.0, nbi)
                    xpr = jnp.where(rid == 7, jnp.broadcast_to(nbr[0:1, :], xr.shape), pltpu.roll(xr, 7, 0))
                    xpi = jnp.where(rid == 7, jnp.broadcast_to(nbi[0:1, :], xi.shape), pltpu.roll(xi, 7, 0))
                else:
                    first = t == 0
                    rn = pl.multiple_of(jnp.maximum(r0 - 8, 0), 8)
                    nbr = jnp.where(first, hr, xs_ref[pl.ds(rn, 8), re])
                    nbi = jnp.where(first, hi, xs_ref[pl.ds(rn, 8), im])
                    nbr = jnp.where(first & start, 0.0, nbr)
                    nbi = jnp.where(first & start, 0.0, nbi)
                    xpr = jnp.where(rid == 0, jnp.broadcast_to(nbr[7:8, :], xr.shape), pltpu.roll(xr, 1, 0))
                    xpi = jnp.where(rid == 0, jnp.broadcast_to(nbi[7:8, :], xi.shape), pltpu.roll(xi, 1, 0))
                ar = ar + gr * xpr + gi * xpi
                ai = ai - gr * xpi + gi * xpr
                return gr, gi, ar, ai

            zz = jnp.zeros((8, w), F32)
            gcr, gci, ar, ai = lax.fori_loop(0, ntile, tile, (carry[:, re], carry[:, im], zz, zz))
            carry[:, re] = gcr
            carry[:, im] = gci
            dl_ref[:, re] += ar
            dl_ref[:, im] += ai
        g = gbuf[...].astype(MXU_DTYPE)
        du_ref[...] = jnp.dot(g, bt_ref[...], preferred_element_type=F32)
        db_ref[...] += lax.dot_general(u_ref[...].astype(MXU_DTYPE), g, (((0,), (0,)), ((), ())),
                                       preferred_element_type=F32)

    ucb = u_cb * (SSM_WIDTH // CG_CHANNELS)
    return pl.pallas_call(
        body, name=name, grid=(N_CG, nch),
        in_specs=[pl.BlockSpec((t_rows, CG_CHANNELS), lambda g, j: (jnp.minimum(chunk(j), nlc - 1), g)),
                  pl.BlockSpec((t_rows, CG_CHANNELS), lambda g, j: (chunk(j), ucb + g)),
                  pl.BlockSpec((t_rows, cw), lambda g, j: (chunk(j), g)),
                  pl.BlockSpec((8, cw), lambda g, j: (halo_blk(j), g)),
                  pl.BlockSpec((CG_CHANNELS, cw), lambda g, j: (g, 0)),
                  pl.BlockSpec((cw, CG_CHANNELS), lambda g, j: (g, 0)),
                  pl.BlockSpec((64, CG_STATES), lambda g, j: (0, g))],
        out_specs=[pl.BlockSpec((t_rows, CG_CHANNELS), lambda g, j: (chunk(j), g)),
                   pl.BlockSpec((CG_CHANNELS, cw), lambda g, j: (g, 0)),
                   pl.BlockSpec((CG_CHANNELS, cw), lambda g, j: (g, 0)),
                   pl.BlockSpec((8, cw), lambda g, j: (0, g))],
        out_shape=[_sds((n, SSM_WIDTH), F32), _sds((SSM_WIDTH, cw), F32), _sds((SSM_WIDTH, cw), F32),
                   _sds((8, 2 * N_STATE), F32)],
        scratch_shapes=[pltpu.VMEM((t_rows, cw), F32), pltpu.VMEM((8, cw), F32)],
        compiler_params=_cparams(("arbitrary", "arbitrary")),
    )(dys, z, xs, xs, cbd_t, bbd_t, lamc_adj)


_CG_GROUPS = SSM_GROUPS // N_CG


def _group_mask():
    idx = jnp.arange(_CG_GROUPS)
    return (idx[:, None] == idx[None, :])[None, :, None, None, :, None]


def _diag_blocks(p_re, p_im):
    t = jnp.stack([p_re, p_im], axis=2).reshape(N_CG, _CG_GROUPS, SSM_GROUP, 2, 1, SSM_STATE)
    return jnp.where(_group_mask(), t, 0.0).reshape(SSM_WIDTH, 2 * CG_STATES)


def _diag_extract(d):
    d6 = d.reshape(N_CG, _CG_GROUPS, SSM_GROUP, 2, _CG_GROUPS, SSM_STATE)
    blk = jnp.sum(jnp.where(_group_mask(), d6, 0.0), axis=4)
    blk = blk.reshape(SSM_GROUPS, SSM_GROUP, 2, SSM_STATE)
    return blk[:, :, 0], blk[:, :, 1]


def _block_transpose(d):
    return jnp.transpose(d.reshape(N_CG, CG_CHANNELS, 2 * CG_STATES), (0, 2, 1)).reshape(2 * N_STATE, CG_CHANNELS)


def _s5_disc(lam_re, lam_im, log_dt, b_re, b_im):
    lam = lax.complex(lam_re, lam_im)
    dt = jnp.exp(log_dt)[:, None]
    lam_bar = jnp.exp(lam * dt)
    b_bar = ((lam_bar - 1.0) / lam)[..., None] * lax.complex(b_re, b_im)
    return jnp.real(lam_bar), jnp.imag(lam_bar), jnp.real(b_bar), jnp.imag(b_bar)


def _lam_consts(lr, li, mirrored, conj):
    lam = lax.complex(lr.reshape(-1), -li.reshape(-1) if conj else li.reshape(-1))
    p2 = lam * lam
    p4 = p2 * p2
    pw = [lam, p2, p2 * lam, p4, p4 * lam, p4 * p2, p4 * p2 * lam, p4 * p4]
    rows = jnp.arange(8)[:, None]
    out = []
    for kk in (1, 2, 4):
        mask = (rows <= 7 - kk) if mirrored else (rows >= kk)
        pk = jnp.where(mask, pw[kk - 1][None, :], 0.0)
        out += [jnp.real(pk), jnp.imag(pk)]
    q = jnp.stack(pw[::-1] if mirrored else pw)
    return jnp.concatenate(out + [jnp.real(q), jnp.imag(q)], axis=0)


def _dev(t):
    return (t // 4, (t // 2) % 2, t % 2)


def _my_index():
    return 4 * lax.axis_index("x") + 2 * lax.axis_index("y") + lax.axis_index("c")


def _comm_call(body, name, arrs, lead, n_remote):
    nw = len(arrs)
    any_spec = pl.BlockSpec(memory_space=pl.ANY)
    return pl.pallas_call(
        body, name=name, out_shape=[_sds((lead,) + a.shape[-2:], a.dtype) for a in arrs],
        in_specs=[any_spec] * nw, out_specs=[any_spec] * nw,
        scratch_shapes=[pltpu.SemaphoreType.DMA((n_remote * nw,)), pltpu.SemaphoreType.DMA((n_remote * nw,)),
                        pltpu.SemaphoreType.DMA((2 * nw,))] + [pltpu.VMEM(a.shape[-2:], a.dtype) for a in arrs],
        compiler_params=pltpu.CompilerParams(vmem_limit_bytes=VMEM_LIMIT),
    )(*arrs)


class _LocalCopy:
    def __init__(self, src, dst, buf, sem_in, sem_out):
        self.fetch = pltpu.make_async_copy(src, buf, sem_in)
        self.store = pltpu.make_async_copy(buf, dst, sem_out)
        self.fetch.start()

    def forward(self):
        self.fetch.wait()
        self.store.start()

    def finish(self):
        self.store.wait()


ALL8 = "all8"


def _all8_copies(g_refs, o_refs, ssem, rsem, lsem, bufs):
    me = _my_index()
    fetch, store, sends, recvs = [], [], [], []
    for i, (g_ref, o_ref) in enumerate(zip(g_refs, o_refs)):
        fetch.append(pltpu.make_async_copy(g_ref, bufs[i], lsem.at[2 * i]))
        store.append(pltpu.make_async_copy(bufs[i], o_ref.at[me], lsem.at[2 * i + 1]))
        for dd in range(1, 8):
            t, s = (me + dd) % 8, (me + 8 - dd) % 8
            sems = dict(send_sem=ssem.at[7 * i + dd - 1], recv_sem=rsem.at[7 * i + dd - 1], device_id_type=MESH)
            sends.append(pltpu.make_async_remote_copy(src_ref=g_ref, dst_ref=o_ref.at[me], device_id=_dev(t), **sems))
            recvs.append(pltpu.make_async_remote_copy(src_ref=g_ref, dst_ref=o_ref.at[s], device_id=_dev(s), **sems))
    return fetch, store, sends, recvs


def _exchange8(gs, name):
    nw = len(gs)

    def body(*refs):
        args = (refs[:nw], refs[nw:2 * nw], *refs[2 * nw:2 * nw + 3], refs[2 * nw + 3:], ALL8)
        _chips_start(*args)
        _chips_finish(*args)

    return _comm_call(body, name, gs, 8, 7)


def _chip_copies(w_refs, o_refs, ssem, rsem, lsem, bufs, gather):
    if gather == ALL8:
        return _all8_copies(w_refs, o_refs, ssem, rsem, lsem, bufs)
    x, y, cc = lax.axis_index("x"), lax.axis_index("y"), lax.axis_index("c")
    k = 2 * x + y
    peers = [(1 - x, y), (x, 1 - y), (1 - x, 1 - y)]
    fetch, store, sends, recvs = [], [], [], []
    for i, (w_ref, o_ref) in enumerate(zip(w_refs, o_refs)):
        if gather:
            fetch.append(pltpu.make_async_copy(w_ref.at[cc], bufs[i], lsem.at[2 * i]))
            store.append(pltpu.make_async_copy(bufs[i], o_ref.at[k], lsem.at[2 * i + 1]))
        for j, (px, py) in enumerate(peers):
            sems = dict(send_sem=ssem.at[3 * i + j], recv_sem=rsem.at[3 * i + j], device_id=(px, py, cc), device_id_type=MESH)
            src, dst = (w_ref.at[cc], o_ref.at[k]) if gather else (w_ref.at[2 * px + py], o_ref.at[j])
            sends.append(pltpu.make_async_remote_copy(src_ref=src, dst_ref=dst, **sems))
            src, dst = (w_ref.at[cc], o_ref.at[2 * px + py]) if gather else (w_ref.at[k], o_ref.at[j])
            recvs.append(pltpu.make_async_remote_copy(src_ref=src, dst_ref=dst, **sems))
    return fetch, store, sends, recvs


def _chips_start(*args):
    fetch, _, sends, _ = _chip_copies(*args)
    for cp in fetch + sends:
        cp.start()


def _chips_finish(*args):
    fetch, store, sends, recvs = _chip_copies(*args)
    for cp in fetch:
        cp.wait()
    for cp in store:
        cp.start()
    for cp in recvs:
        cp.wait_recv()
    for cp in sends:
        cp.wait_send()
    for cp in store:
        cp.wait()


def _chips_scratch(ws, gather):
    nw = len(ws)
    n_remote = 7 if gather == ALL8 else 3
    return ([pltpu.SemaphoreType.DMA((n_remote * nw,)), pltpu.SemaphoreType.DMA((n_remote * nw,)),
             pltpu.SemaphoreType.DMA((2 * nw,))] + ([pltpu.VMEM(a.shape[-2:], a.dtype) for a in ws] if gather else []))


def _ride_along(core, n_in, n_out, n_scr, grid, side):
    if side is None:
        return core, [], [], [], []
    arrs, gather = side
    ns = len(arrs)

    def body(*refs):
        a, b, c_ = n_in + ns, n_in + ns + n_out, n_in + 2 * ns + n_out
        s_scr = refs[c_ + n_scr:]
        sargs = (refs[n_in:a], refs[b:c_], *s_scr[:3], s_scr[3:], gather)
        ids = [pl.program_id(ax) for ax in range(len(grid))]
        first, last = ids[0] == 0, ids[0] == grid[0] - 1
        for i_, g_ in zip(ids[1:], grid[1:]):
            first, last = first & (i_ == 0), last & (i_ == g_ - 1)

        @pl.when(first)
        def _():
            _chips_start(*sargs)

        core(*refs[:n_in], *refs[a:b], *refs[c_:c_ + n_scr])

        @pl.when(last)
        def _():
            _chips_finish(*sargs)

    any_spec = pl.BlockSpec(memory_space=pl.ANY)
    lead = 8 if gather == ALL8 else (4 if gather else 3)
    shapes = [_sds((lead,) + a_.shape[-2:], a_.dtype) for a_ in arrs]
    return body, [any_spec] * ns, [any_spec] * ns, shapes, _chips_scratch(arrs, gather)


def _sibling_send(hs, name):
    nw = len(hs)

    def body(*refs):
        h_refs, o_refs, (ssem, rsem, lsem) = refs[:nw], refs[nw:2 * nw], refs[2 * nw:]
        x, y, cc = lax.axis_index("x"), lax.axis_index("y"), lax.axis_index("c")
        sends = []
        for i, (h_ref, o_ref) in enumerate(zip(h_refs, o_refs)):
            cp = pltpu.make_async_remote_copy(src_ref=h_ref.at[1 - cc], dst_ref=o_ref, send_sem=ssem.at[i],
                                              recv_sem=rsem.at[i], device_id=(x, y, 1 - cc), device_id_type=MESH)
            cp.start()
            sends.append(cp)
        for i, (h_ref, o_ref) in enumerate(zip(h_refs, o_refs)):
            pltpu.make_async_remote_copy(src_ref=h_ref.at[cc], dst_ref=o_ref, send_sem=ssem.at[i], recv_sem=rsem.at[i],
                                         device_id=(x, y, 1 - cc), device_id_type=MESH).wait_recv()
        for cp in sends:
            cp.wait_send()

    nw_spec = pl.BlockSpec(memory_space=pl.ANY)
    return pl.pallas_call(
        body, name=name, out_shape=[_sds(h.shape[1:], h.dtype) for h in hs],
        in_specs=[nw_spec] * nw, out_specs=[nw_spec] * nw,
        scratch_shapes=[pltpu.SemaphoreType.DMA((nw,)), pltpu.SemaphoreType.DMA((nw,)), pltpu.SemaphoreType.DMA((nw,))],
    )(*hs)


def _sibling_exchange(hs, name):
    nw = len(hs)

    def body(*refs):
        h_refs, o_refs, (ssem, rsem, lsem), bufs = refs[:nw], refs[nw:2 * nw], refs[2 * nw:2 * nw + 3], refs[2 * nw + 3:]
        x, y, cc = lax.axis_index("x"), lax.axis_index("y"), lax.axis_index("c")
        locs, sends = [], []
        for i, (h_ref, o_ref) in enumerate(zip(h_refs, o_refs)):
            locs.append(_LocalCopy(h_ref, o_ref.at[cc], bufs[i], lsem.at[2 * i], lsem.at[2 * i + 1]))
            cp = pltpu.make_async_remote_copy(src_ref=h_ref, dst_ref=o_ref.at[cc], send_sem=ssem.at[i], recv_sem=rsem.at[i],
                                              device_id=(x, y, 1 - cc), device_id_type=MESH)
            cp.start()
            sends.append(cp)
        for loc in locs:
            loc.forward()
        for i, (h_ref, o_ref) in enumerate(zip(h_refs, o_refs)):
            pltpu.make_async_remote_copy(src_ref=h_ref, dst_ref=o_ref.at[1 - cc], send_sem=ssem.at[i], recv_sem=rsem.at[i],
                                         device_id=(x, y, 1 - cc), device_id_type=MESH).wait_recv()
        for cp in sends:
            cp.wait_send()
        for loc in locs:
            loc.finish()

    return _comm_call(body, name, hs, 2, 1)


def _sum8(buf, name):
    _, r, c = buf.shape
    tr = _pick(r, (256, 128, 64, 32, 16, 8))
    flat = buf.reshape(8 * r, c)

    def body(*v):
        acc = v[0]
        for t in v[1:]:
            acc = acc + t
        return acc

    return _rowwise(body, name=name, nblk=r // tr, tr=tr, rows=[(flat, 0, c, s * r) for s in range(8)],
                    outs=[(c, F32)])[0]


def _pack(arrs, rows_mult=16):
    flat = jnp.concatenate([a.reshape(-1).astype(F32) for a in arrs])
    nel = flat.shape[0]
    r = -(-nel // PACK_W)
    r = -(-r // rows_mult) * rows_mult
    return jnp.pad(flat, (0, r * PACK_W - nel)).reshape(r, PACK_W)


def _unpack(buf, shapes):
    flat = buf.reshape(-1)
    out, o = [], 0
    for s in shapes:
        nel = int(np.prod(s))
        out.append(flat[o:o + nel].reshape(s))
        o += nel
    return out


def _adamw(g, w, m, v, name):
    r, wd = g.shape
    tr = _pick(r, tuple(t for t in (256, 128, 64, 32, 16, 8) if t * wd <= 262144) or (8,))
    c1 = 1.0 / (1.0 - ADAM_B1 ** ADAM_STEP)
    c2 = 1.0 / (1.0 - ADAM_B2 ** ADAM_STEP)

    def body(gv, wv, mv, vv):
        mn = ADAM_B1 * mv + (1.0 - ADAM_B1) * gv
        vn = ADAM_B2 * vv + (1.0 - ADAM_B2) * (gv * gv)
        delta = -ADAM_LR * ((mn * c1) / (jnp.sqrt(vn * c2) + ADAM_EPS) + ADAM_WD * wv)
        return delta, mn, vn

    return _rowwise(body, name=name, nblk=r // tr, tr=tr, rows=[(a, 0, wd, 0) for a in (g, w, m, v)],
                    outs=[(wd, F32)] * 3)


def kernel(x, c, ctx, c_ctx, w_mod, b_mod, norm1_g, norm2_g, w_in, q_a_g, w_uq, kv_a_g, w_ukv, q_norm_g, k_norm_g, w_o_attn, lam_re_f, lam_im_f, log_dt_f, c_re_f, c_im_f, lam_re_b, lam_im_b, log_dt_b, c_re_b, c_im_b, b_re, b_im, d_skip, w_glu, w_out, w_up, conv_w, conv_b, w_down, loss_target, m_c_ctx, m_w_mod, m_b_mod, m_norm1_g, m_norm2_g, m_w_in, m_q_a_g, m_w_uq, m_kv_a_g, m_w_ukv, m_q_norm_g, m_k_norm_g, m_w_o_attn, m_lam_re_f, m_lam_im_f, m_log_dt_f, m_c_re_f, m_c_im_f, m_lam_re_b, m_lam_im_b, m_log_dt_b, m_c_re_b, m_c_im_b, m_b_re, m_b_im, m_d_skip, m_w_glu, m_w_out, m_w_up, m_conv_w, m_conv_b, m_w_down, v_c_ctx, v_w_mod, v_b_mod, v_norm1_g, v_norm2_g, v_w_in, v_q_a_g, v_w_uq, v_kv_a_g, v_w_ukv, v_q_norm_g, v_k_norm_g, v_w_o_attn, v_lam_re_f, v_lam_im_f, v_log_dt_f, v_c_re_f, v_c_im_f, v_lam_re_b, v_lam_im_b, v_log_dt_b, v_c_re_b, v_c_im_b, v_b_re, v_b_im, v_d_skip, v_w_glu, v_w_out, v_w_up, v_conv_w, v_conv_b, v_w_down):
    weights = dict(c_ctx=c_ctx, w_mod=w_mod, b_mod=b_mod, norm1_g=norm1_g, norm2_g=norm2_g, w_in=w_in, q_a_g=q_a_g, w_uq=w_uq, kv_a_g=kv_a_g, w_ukv=w_ukv, q_norm_g=q_norm_g, k_norm_g=k_norm_g, w_o_attn=w_o_attn, lam_re_f=lam_re_f, lam_im_f=lam_im_f, log_dt_f=log_dt_f, c_re_f=c_re_f, c_im_f=c_im_f, lam_re_b=lam_re_b, lam_im_b=lam_im_b, log_dt_b=log_dt_b, c_re_b=c_re_b, c_im_b=c_im_b, b_re=b_re, b_im=b_im, d_skip=d_skip, w_glu=w_glu, w_out=w_out, w_up=w_up, conv_w=conv_w, conv_b=conv_b, w_down=w_down)
    mom_m = dict(c_ctx=m_c_ctx, w_mod=m_w_mod, b_mod=m_b_mod, norm1_g=m_norm1_g, norm2_g=m_norm2_g, w_in=m_w_in, q_a_g=m_q_a_g, w_uq=m_w_uq, kv_a_g=m_kv_a_g, w_ukv=m_w_ukv, q_norm_g=m_q_norm_g, k_norm_g=m_k_norm_g, w_o_attn=m_w_o_attn, lam_re_f=m_lam_re_f, lam_im_f=m_lam_im_f, log_dt_f=m_log_dt_f, c_re_f=m_c_re_f, c_im_f=m_c_im_f, lam_re_b=m_lam_re_b, lam_im_b=m_lam_im_b, log_dt_b=m_log_dt_b, c_re_b=m_c_re_b, c_im_b=m_c_im_b, b_re=m_b_re, b_im=m_b_im, d_skip=m_d_skip, w_glu=m_w_glu, w_out=m_w_out, w_up=m_w_up, conv_w=m_conv_w, conv_b=m_conv_b, w_down=m_w_down)
    mom_v = dict(c_ctx=v_c_ctx, w_mod=v_w_mod, b_mod=v_b_mod, norm1_g=v_norm1_g, norm2_g=v_norm2_g, w_in=v_w_in, q_a_g=v_q_a_g, w_uq=v_w_uq, kv_a_g=v_kv_a_g, w_ukv=v_w_ukv, q_norm_g=v_q_norm_g, k_norm_g=v_k_norm_g, w_o_attn=v_w_o_attn, lam_re_f=v_lam_re_f, lam_im_f=v_lam_im_f, log_dt_f=v_log_dt_f, c_re_f=v_c_re_f, c_im_f=v_c_im_f, lam_re_b=v_lam_re_b, lam_im_b=v_lam_im_b, log_dt_b=v_log_dt_b, c_re_b=v_c_re_b, c_im_b=v_c_im_b, b_re=v_b_re, b_im=v_b_im, d_skip=v_d_skip, w_glu=v_w_glu, w_out=v_w_out, w_up=v_w_up, conv_w=v_conv_w, conv_b=v_conv_b, w_down=v_w_down)
    names = list(weights)

    nl, d = x.shape[1], x.shape[2]
    nc = ctx.shape[1]
    n = nl + nc
    f2 = conv_b.shape[1]
    fh = f2 // 2
    d6 = b_mod.shape[1]
    mx, my, mc = lax.axis_index("x"), lax.axis_index("y"), lax.axis_index("c")
    chip = 2 * mx + my
    me = 4 * mx + 2 * my + mc
    tr = _pick(math.gcd(nl, nc), (256, 128, 64, 32, 16))
    nlb, nb = nl // tr, n // tr

    big_names = ["w_in", "w_uq", "w_ukv", "w_o_attn", "w_glu", "w_out", "w_up", "w_down"]
    row_sharded = ("w_out", "w_down")
    ffn_names = ["w_o_attn", "w_glu", "w_out", "w_up", "w_down"]
    early_names = [k for k in big_names if k not in ffn_names]
    full = {}

    def halves_in(names_):
        return [weights[k][0].astype(MXU_DTYPE).reshape(2, weights[k].shape[1] // 2, weights[k].shape[2]) for k in names_]

    def assemble(names_, my_halves, name):
        gathered = _sibling_exchange([t.reshape(-1, t.shape[2]) for t in my_halves], name)
        for k_, gth in zip(names_, gathered):
            r_, c_ = weights[k_].shape[1:]
            g4 = gth.reshape(2, 4, r_ // 2, c_)
            full[k_] = (jnp.transpose(g4, (1, 0, 2, 3)).reshape(4 * r_, c_) if k_ in row_sharded
                        else jnp.transpose(g4, (0, 2, 1, 3)).reshape(r_, 4 * c_))


    cwid = conv_w.shape[2]
    sw = -(-max(d, cwid) // 128) * 128
    small_in = jnp.concatenate([jnp.pad(c, ((0, 0), (0, sw - d))), jnp.pad(conv_w[0], ((0, 4), (0, sw - cwid)))], axis=0)
    (small_all,) = _exchange8([small_in], "gather_c")
    cs = small_all[:, 0, :d]
    conv_w_full = jnp.concatenate([small_all[2 * j, 1:4, :cwid] for j in range(4)], axis=1)
    cs16 = jnp.concatenate([cs, c_ctx[None, :], jnp.zeros((7, d), F32)], axis=0)

    csh = w_mod.shape[2]
    b_mod_sh = lax.dynamic_slice(b_mod, (0, chip * csh), (1, csh))

    def mod_fwd_body(c_ref, w_ref, b_ref, o_ref):
        a = _silu(c_ref[...]).astype(MXU_DTYPE)
        o_ref[...] = jnp.dot(a, w_ref[...].astype(MXU_DTYPE), preferred_element_type=F32) + b_ref[...]

    mod_sh = pl.pallas_call(mod_fwd_body, name="mod_fwd", out_shape=_sds((16, csh), F32),
                            compiler_params=pltpu.CompilerParams(vmem_limit_bytes=VMEM_LIMIT))(cs16, w_mod[0], b_mod_sh)
    (mod_all,) = _exchange8([mod_sh], "gather_mod")
    mod_full = jnp.concatenate([mod_all[2 * j] for j in range(4)], axis=1)
    modv = jnp.stack([lax.dynamic_slice(mod_full, (me, 0), (1, d6)), mod_full[8:9]])

    xa = jnp.concatenate([x[0], ctx[0]], axis=0)
    n1g, n2g = norm1_g, norm2_g

    def norm1_body(xv, m, g):
        sh1, sc1 = m[:, :d], m[:, d:2 * d]
        return _rms_fwd(xv, g, d) * (1.0 + sc1) + sh1

    (h1,), early_halves = _rowwise(norm1_body, name="norm1_fwd", nblk=nb, tr=tr, rows=[(xa, 0, d, 0)], sels=[modv],
                                   fulls=[n1g], outs=[(d, MXU_DTYPE)], seg=nlb, side=(halves_in(early_names), True))
    assemble(early_names, early_halves, "gather_weight_halves")

    u_off, kv_off, kr_off = 2 * d, 2 * d + SSM_WIDTH, 2 * d + SSM_WIDTH + KV_LORA
    q_off = -(-(kr_off + SLOT) // Q_LORA) * Q_LORA
    zw = q_off + Q_LORA
    wi = full["w_in"]
    s0, s1, s2, s3 = Q_LORA, Q_LORA + KV_LORA, Q_LORA + KV_LORA + QK_ROPE, Q_LORA + KV_LORA + QK_ROPE + SSM_WIDTH
    zpad = lambda w_: jnp.zeros((d, w_), MXU_DTYPE)
    win_p = jnp.concatenate([wi[:, s3:], wi[:, s2:s3], wi[:, s0:s1], wi[:, s1:s2], zpad(SLOT - QK_ROPE),
                             zpad(q_off - kr_off - SLOT), wi[:, :s0]], axis=1)
    wuq_p = jnp.pad(full["w_uq"].reshape(Q_LORA, N_HEADS, QK_DIM), ((0, 0), (0, 0), (0, SLOT - QK_DIM))).reshape(Q_LORA, N_HEADS * SLOT)
    wukv3 = full["w_ukv"].reshape(KV_LORA, N_HEADS, QK_NOPE + V_DIM)
    padh = lambda t: jnp.pad(t, ((0, 0), (0, 0), (0, SLOT - t.shape[2]))).reshape(t.shape[0], N_HEADS * SLOT)
    wukv_p = jnp.concatenate([padh(wukv3[:, :, :QK_NOPE]), padh(wukv3[:, :, QK_NOPE:])], axis=1)
    hw = N_HEADS * SLOT
    gain_p = lambda g_: jnp.tile(jnp.pad(g_[0], (0, SLOT - QK_DIM)), N_HEADS)[None, :]
    qg_p, kg_p = gain_p(q_norm_g), gain_p(k_norm_g)

    tok = jnp.arange(nl)
    freqs = ROPE_THETA ** (-jnp.arange(QK_ROPE // 4, dtype=F32) / (QK_ROPE // 4))
    ang = jnp.concatenate([(tok // GRID_W)[:, None] * freqs, (tok % GRID_W)[:, None] * freqs], axis=-1)
    cos_t = jnp.concatenate([jnp.cos(ang), jnp.ones((nc, 16), F32)], axis=0)
    sin_t = jnp.concatenate([jnp.sin(ang), jnp.zeros((nc, 16), F32)], axis=0)
    zl = lambda w_: jnp.zeros((n, w_), F32)
    rope_c = jnp.concatenate([jnp.ones((n, QK_NOPE), F32), cos_t, cos_t, zl(SLOT - QK_DIM)], axis=1)
    rope_sa = jnp.concatenate([zl(QK_NOPE), -sin_t, zl(SLOT - QK_NOPE - 16)], axis=1)
    rope_sb = jnp.concatenate([zl(QK_NOPE + 16), sin_t, zl(SLOT - QK_DIM)], axis=1)

    dirs = (("f", lam_re_f, lam_im_f, log_dt_f, c_re_f, c_im_f, False), ("b", lam_re_b, lam_im_b, log_dt_b, c_re_b, c_im_b, True))
    bbd, cbd_t, cbd_n, bbd_t, lamc, lamc_adj, disc_vjps = [], [], [], [], [], [], []
    for _, l_re, l_im, l_dt, cr_, ci_, rev_ in dirs:
        (lbr, lbi, bbr, bbi), vjp = jax.vjp(_s5_disc, l_re[0], l_im[0], l_dt[0], b_re[0], b_im[0])
        disc_vjps.append(vjp)
        bb = _diag_blocks(jnp.transpose(bbr, (0, 2, 1)), jnp.transpose(bbi, (0, 2, 1))).astype(MXU_DTYPE)
        cc_ = _diag_blocks(cr_[0], -ci_[0]).astype(MXU_DTYPE)
        bbd.append(bb)
        bbd_t.append(_block_transpose(bb))
        cbd_t.append(cc_)
        cbd_n.append(_block_transpose(cc_))
        lamc.append(_lam_consts(lbr, lbi, rev_, False))
        lamc_adj.append(_lam_consts(lbr, lbi, not rev_, True))
    t_scan = tr

    z = _mm(h1, win_p, "nn", "in_proj")
    gl_cb, u_cb, kv_cb, kr_cb, q_cb = 0, u_off // SSM_WIDTH, kv_off // KV_LORA, kr_off // SLOT, q_off // Q_LORA

    (cqn,) = _rowwise(lambda v, g: _rms_fwd(v, g, Q_LORA), name="qa_norm_fwd", nblk=nlb, tr=tr,
                      rows=[(z, q_cb, Q_LORA, 0)], fulls=[q_a_g], outs=[(Q_LORA, MXU_DTYPE)])
    qh = _mm(cqn, wuq_p, "nn", "q_up")

    def qhead_body(qv, cv, sav, sbv, g):
        return jnp.concatenate([_rope_fwd(_rms_fwd(t, g[:, :SLOT], QK_DIM), cv, sav, sbv) for t in _heads(qv)], axis=1)

    rope_rows = lambda: [(rope_c, 0, SLOT, 0), (rope_sa, 0, SLOT, 0), (rope_sb, 0, SLOT, 0)]
    (q_p,) = _rowwise(qhead_body, name="q_head_fwd", nblk=nlb, tr=tr, rows=[(qh, 0, hw, 0)] + rope_rows(),
                      fulls=[qg_p], outs=[(hw, MXU_DTYPE)])

    (ckvn,) = _rowwise(lambda v, g: _rms_fwd(v, g, KV_LORA), name="kva_norm_fwd", nblk=nb, tr=tr,
                       rows=[(z, kv_cb, KV_LORA, 0)], fulls=[kv_a_g], outs=[(KV_LORA, MXU_DTYPE)])
    kvpre = _mm(ckvn, wukv_p, "nn", "kv_up")

    def khead_body(kv_, vv_, krv, cv, sav, sbv, g):
        kpe = pltpu.roll(krv, QK_NOPE, 1)
        ks = [_rope_fwd(_rms_fwd(t + kpe, g[:, :SLOT], QK_DIM), cv, sav, sbv) for t in _heads(kv_)]
        ones_lane = lax.broadcasted_iota(jnp.int32, vv_.shape, 1) % SLOT == V_DIM
        return jnp.concatenate(ks, axis=1), jnp.where(ones_lane, 1.0, vv_)

    k_p, v_p = _rowwise(khead_body, name="k_head_fwd", nblk=nb, tr=tr,
                        rows=[(kvpre, 0, hw, 0), (kvpre, 1, hw, 0), (z, kr_cb, SLOT, 0)] + rope_rows(),
                        fulls=[kg_p], outs=[(hw, MXU_DTYPE), (hw, MXU_DTYPE)])

    scale = QK_DIM ** -0.5
    o_p, lse, ffn_halves = _attn_fwd(q_p, k_p, v_p, nl, scale, side=(halves_in(ffn_names), True))
    assemble(ffn_names, ffn_halves, "gather_ffn_weight_halves")
    wglu, wout, wup, wdown = full["w_glu"], full["w_out"], full["w_up"], full["w_down"]
    wo_p = jnp.pad(full["w_o_attn"].reshape(N_HEADS, V_DIM, d), ((0, 0), (0, SLOT - V_DIM), (0, 0))).reshape(N_HEADS * SLOT, d)
    a_l = _mm(o_p, wo_p, "nn", "attn_out")

    scans = [_s5_scan(z, u_cb, bbd[j], cbd_n[j], lamc[j], t_scan, nl, dirs[j][6], "s5_scan_" + dirs[j][0]) for j in range(2)]
    xs, ydir = [s_[0] for s_ in scans], [s_[1] for s_ in scans]

    def ssm_out_body(uv, a, b, dsk):
        ys = uv * dsk + a + b
        return ys, _gelu(ys)

    ys, ge = _rowwise(ssm_out_body, name="s5_out_fwd", nblk=nlb, tr=tr,
                      rows=[(z, u_cb, SSM_WIDTH, 0), (ydir[0], 0, SSM_WIDTH, 0), (ydir[1], 0, SSM_WIDTH, 0)],
                      fulls=[d_skip], outs=[(SSM_WIDTH, F32), (SSM_WIDTH, MXU_DTYPE)])
    glu_out = _mm(ge, wglu, "nn", "glu_proj")

    def merge_body(ga, gs, av, val, gate):
        return _sigmoid(ga) * av + _sigmoid(gs) * (val * _sigmoid(gate))

    merge_rows = lambda: [(z, 0, d, 0), (z, 1, d, 0), (a_l, 0, d, 0), (glu_out, 0, d, 0), (glu_out, 1, d, 0)]
    (merged,) = _rowwise(merge_body, name="merge_fwd", nblk=nlb, tr=tr, rows=merge_rows(), outs=[(d, MXU_DTYPE)])
    mo = _mm(merged, wout, "nn", "out_proj")
    mod_x = modv[0]

    def norm2_body(xv, mov, m, g):
        g1, sh2, sc2 = m[:, 2 * d:3 * d], m[:, 3 * d:4 * d], m[:, 4 * d:5 * d]
        x1v = xv + g1 * mov
        return x1v, _rms_fwd(x1v, g, d) * (1.0 + sc2) + sh2

    x1, h2 = _rowwise(norm2_body, name="norm2_fwd", nblk=nlb, tr=tr, rows=[(xa, 0, d, 0), (mo, 0, d, 0)],
                      fulls=[mod_x, n2g], outs=[(d, F32), (d, MXU_DTYPE)])
    up = _mm(h2, wup, "nn", "ffn_up")
    cw8 = jnp.zeros((8, f2), F32).at[:3].set(conv_w_full)

    def conv3(t3, w8, off):
        p_, c_, n_ = t3
        return p_ * w8[0:1, off:off + fh] + c_ * w8[1:2, off:off + fh] + n_ * w8[2:3, off:off + fh]

    def conv_fwd_body(val3, gate3, w8, bias):
        val2 = conv3(val3, w8, 0) + bias[:, :fh]
        gate2 = conv3(gate3, w8, fh) + bias[:, fh:]
        return _silu(gate2) * val2

    (act,) = _rowwise(conv_fwd_body, name="conv_fwd", nblk=nlb, tr=tr, rows=[(up, 0, fh, 0), (up, 1, fh, 0)],
                      halo=(0, 1), fulls=[cw8, conv_b], outs=[(fh, MXU_DTYPE)])
    dn = _mm(act, wdown, "nn", "ffn_down")
    tgt = loss_target[0]

    def loss_body(x1v, dnv, tv, m):
        g2 = m[:, 5 * d:6 * d]
        e = x1v + g2 * dnv - tv
        dx2v = e * (1.0 / d)
        return dx2v, dx2v * g2, e * e, dx2v * dnv

    dx2, ddn, loss_acc, dg2_acc = _rowwise(loss_body, name="loss", nblk=nlb, tr=tr,
                                           rows=[(x1, 0, d, 0), (dn, 0, d, 0), (tgt, 0, d, 0)], fulls=[mod_x],
                                           outs=[(d, F32), (d, MXU_DTYPE)], accs=[d, d])
    loss = lax.psum(0.5 / d * jnp.sum(loss_acc), ("x", "y", "c"))

    g_big = {}

    def chip_partials(names_, name):
        pcs = []
        for k_ in names_:
            r_, c_ = weights[k_].shape[1:]
            if k_ in row_sharded:
                p4 = jnp.transpose(g_big[k_].reshape(4, 2, r_ // 2, c_), (1, 0, 2, 3))
            else:
                p4 = jnp.transpose(g_big[k_].reshape(2, r_ // 2, 4, c_), (0, 2, 1, 3))
            pcs.append(p4.reshape(2, 2 * r_, c_))
        out = []
        for k_, p_, got in zip(names_, pcs, _sibling_send(pcs, name)):
            rows4, c_ = got.shape
            own = lax.dynamic_index_in_dim(p_, mc, 0, keepdims=False)
            tr_ = _pick(rows4, (256, 128, 64, 32, 16))
            s32, sb = _rowwise(lambda a, b: (a + b, a + b), name="sum_chip_" + k_, nblk=rows4 // tr_, tr=tr_,
                               rows=[(own, 0, c_, 0), (got, 0, c_, 0)], outs=[(c_, F32), (c_, MXU_DTYPE)])
            out.append((s32, sb, rows4 // 4, c_))
        return out

    def scatter_in(partials):
        return [sb.reshape(4, rh, c_) for _, sb, rh, c_ in partials]

    def chip_sums(names_, partials, recv3):
        out = []
        for k_, (s32, _, rh, c_), r3 in zip(names_, partials, recv3):
            mine = lax.dynamic_slice(s32, (chip * rh, 0), (rh, c_))
            tr_ = _pick(rh, (256, 128, 64, 32, 16))
            (red,) = _rowwise(lambda a, b0, b1, b2: a + b0 + b1 + b2, name="sum_grad_" + k_, nblk=rh // tr_, tr=tr_,
                              rows=[(mine, 0, c_, 0)] + [(r3.reshape(3 * rh, c_), 0, c_, j * rh) for j in range(3)],
                              outs=[(c_, F32)])
            out.append(red)
        return out

    dact = _mm(ddn, wdown, "nt", "ffn_down_dx")
    g_big["w_down"] = _mm(act, ddn, "tn", "ffn_down_dw")

    def conv_bwd_body(val3, gate3, da, w8, bias):
        val2 = conv3(val3, w8, 0) + bias[:, :fh]
        gate2 = conv3(gate3, w8, fh) + bias[:, fh:]
        sg = _sigmoid(gate2)
        dval2 = da * (gate2 * sg)
        dgate2 = da * val2 * (sg * (1.0 + gate2 * (1.0 - sg)))
        du2 = jnp.concatenate([dval2, dgate2], axis=1)
        taps = [jnp.concatenate([dval2 * val3[j], dgate2 * gate3[j]], axis=1) for j in range(3)]
        return du2, du2, taps[0], taps[1], taps[2]

    du2, dcb_acc, dcw0, dcw1, dcw2 = _rowwise(conv_bwd_body, name="conv_bwd", nblk=nlb, tr=tr,
                                              rows=[(up, 0, fh, 0), (up, 1, fh, 0), (dact, 0, fh, 0)], halo=(0, 1),
                                              fulls=[cw8, conv_b], outs=[(f2, F32)], accs=[f2, f2, f2, f2])

    def conv_t_body(dval3, dgate3, w8):
        rev = lambda t3: (t3[2], t3[1], t3[0])
        return jnp.concatenate([conv3(rev(dval3), w8, 0), conv3(rev(dgate3), w8, fh)], axis=1)

    (dup,) = _rowwise(conv_t_body, name="conv_bwd_dx", nblk=nlb, tr=tr, rows=[(du2, 0, fh, 0), (du2, 1, fh, 0)],
                      halo=(0, 1), fulls=[cw8], outs=[(f2, MXU_DTYPE)])
    dh2 = _mm(dup, wup, "nt", "ffn_up_dx")
    g_big["w_up"] = _mm(h2, dup, "tn", "ffn_up_dw")

    def norm2_bwd_body(x1v, dh, dx2v, mov, m, g):
        g1, sc2 = m[:, 2 * d:3 * d], m[:, 4 * d:5 * d]
        y = _rms_fwd(x1v, g, d)
        dxn, dgc = _rms_bwd(x1v, g, dh * (1.0 + sc2), d)
        dx1v = dx2v + dxn
        return dx1v, dx1v * g1, dgc, dh, dh * y, dx1v * mov

    dx1, dmo, dn2g_acc, dsh2_acc, dsc2_acc, dg1_acc = _rowwise(
        norm2_bwd_body, name="norm2_bwd", nblk=nlb, tr=tr,
        rows=[(x1, 0, d, 0), (dh2, 0, d, 0), (dx2, 0, d, 0), (mo, 0, d, 0)], fulls=[mod_x, n2g],
        outs=[(d, F32), (d, MXU_DTYPE)], accs=[d, d, d, d])
    dmerged = _mm(dmo, wout, "nt", "out_proj_dx")
    g_big["w_out"] = _mm(merged, dmo, "tn", "out_proj_dw")

    def merge_bwd_body(ga, gs, av, val, gate, dm):
        sa_, ss_, sg_ = _sigmoid(ga), _sigmoid(gs), _sigmoid(gate)
        s_l = val * sg_
        ds_l = dm * ss_
        dga = dm * av * sa_ * (1.0 - sa_)
        dgs = dm * s_l * ss_ * (1.0 - ss_)
        dval = ds_l * sg_
        dgate = ds_l * val * sg_ * (1.0 - sg_)
        return dm * sa_, jnp.concatenate([dval, dgate], axis=1), jnp.concatenate([dga, dgs], axis=1)

    da_l, dglu, dgl = _rowwise(merge_bwd_body, name="merge_bwd", nblk=nlb, tr=tr,
                               rows=merge_rows() + [(dmerged, 0, d, 0)],
                               outs=[(d, MXU_DTYPE), (2 * d, MXU_DTYPE), (2 * d, MXU_DTYPE)])
    dge = _mm(dglu, wglu, "nt", "glu_proj_dx")
    g_big["w_glu"] = _mm(ge, dglu, "tn", "glu_proj_dw")

    def ssm_out_bwd_body(ysv, dgev, uv, dsk):
        dys_ = dgev * _dgelu(ysv)
        return dys_, dys_ * dsk, dys_ * uv

    dys, du_skip, ddskip_acc = _rowwise(ssm_out_bwd_body, name="s5_out_bwd", nblk=nlb, tr=tr,
                                        rows=[(ys, 0, SSM_WIDTH, 0), (dge, 0, SSM_WIDTH, 0), (z, u_cb, SSM_WIDTH, 0)],
                                        fulls=[d_skip], outs=[(SSM_WIDTH, F32), (SSM_WIDTH, F32)], accs=[SSM_WIDTH])
    s5b = [_s5_bwd(dys, z, u_cb, xs[j], cbd_t[j], bbd_t[j], lamc_adj[j], t_scan, nl, dirs[j][6], "s5_bwd_" + dirs[j][0])
           for j in range(2)]
    du_nat = s5b[0][0] + s5b[1][0] + jnp.concatenate([du_skip, jnp.zeros((nc, SSM_WIDTH), F32)], axis=0)

    do_f = _mm(da_l, wo_p, "nt", "attn_out_dx")
    g_wo_p = _mm(o_p, da_l, "tn", "attn_out_dw")
    g_big["w_o_attn"] = g_wo_p.reshape(N_HEADS, SLOT, d)[:, :V_DIM].reshape(N_HEADS * V_DIM, d)
    ffn_partials = chip_partials(ffn_names, "ffn_grads_to_sibling")

    dq_p, dk_p, dv_p, ffn_recv = _attn_bwd(q_p, k_p, v_p, do_f, o_p, lse, nl, scale, side=(scatter_in(ffn_partials), False))

    def qhead_bwd_body(qv, dqv, cv, sav, sbv, g):
        dxs, dgs = [], []
        for t, dt_ in zip(_heads(qv), _heads(dqv)):
            dx_, dg_ = _rms_bwd(t, g[:, :SLOT], _rope_bwd(dt_, cv, sav, sbv), QK_DIM)
            dxs.append(dx_)
            dgs.append(dg_)
        return jnp.concatenate(dxs, axis=1), jnp.concatenate(dgs, axis=1)

    dqh, dqg_acc = _rowwise(qhead_bwd_body, name="q_head_bwd", nblk=nlb, tr=tr,
                            rows=[(qh, 0, hw, 0), (dq_p, 0, hw, 0)] + rope_rows(), fulls=[qg_p],
                            outs=[(hw, MXU_DTYPE)], accs=[hw])
    dcqn = _mm(dqh, wuq_p, "nt", "q_up_dx")
    g_wuq_p = _mm(cqn, dqh, "tn", "q_up_dw")
    dcq, dqag_acc = _rowwise(lambda v, dy, g: _rms_bwd(v, g, dy, Q_LORA), name="qa_norm_bwd", nblk=nlb, tr=tr,
                             rows=[(z, q_cb, Q_LORA, 0), (dcqn, 0, Q_LORA, 0)], fulls=[q_a_g],
                             outs=[(Q_LORA, MXU_DTYPE)], accs=[Q_LORA])

    def khead_bwd_body(kv_, krv, dkv_, dvv_, cv, sav, sbv, g):
        kpe = pltpu.roll(krv, QK_NOPE, 1)
        lane = lax.broadcasted_iota(jnp.int32, krv.shape, 1)
        dxs, dgs, dkr_ = [], [], jnp.zeros(krv.shape, F32)
        for t, dt_ in zip(_heads(kv_), _heads(dkv_)):
            dx_, dg_ = _rms_bwd(t + kpe, g[:, :SLOT], _rope_bwd(dt_, cv, sav, sbv), QK_DIM)
            dxs.append(jnp.where(lane < QK_NOPE, dx_, 0.0))
            dgs.append(dg_)
            dkr_ = dkr_ + dx_
        dkr_ = jnp.where(lane < QK_ROPE, pltpu.roll(dkr_, SLOT - QK_NOPE, 1), 0.0)
        return jnp.concatenate(dxs + [dvv_], axis=1), dkr_, jnp.concatenate(dgs, axis=1)

    red8 = lambda a: jnp.sum(a, axis=-2)
    head_fold = lambda acc: jnp.sum(red8(acc).reshape(N_HEADS, SLOT), axis=0)[:QK_DIM]
    small_g = {}
    for j, dr in enumerate(dirs):
        sfx = dr[0]
        _, dbbd_j, dcbd_j, dlam_j = s5b[j]
        dl = red8(dlam_j).reshape(N_CG, 2, CG_STATES)
        db_re, db_im = _diag_extract(dbbd_j)
        cot = (dl[:, 0].reshape(SSM_GROUPS, SSM_STATE), dl[:, 1].reshape(SSM_GROUPS, SSM_STATE),
               jnp.transpose(db_re, (0, 2, 1)), jnp.transpose(db_im, (0, 2, 1)))
        g_lre, g_lim, g_ldt, g_bre, g_bim = disc_vjps[j](cot)
        small_g["lam_re_" + sfx], small_g["lam_im_" + sfx], small_g["log_dt_" + sfx] = g_lre, g_lim, g_ldt
        small_g["b_re"] = small_g.get("b_re", 0.0) + g_bre
        small_g["b_im"] = small_g.get("b_im", 0.0) + g_bim
        dc_re, dc_im = _diag_extract(dcbd_j)
        small_g["c_re_" + sfx], small_g["c_im_" + sfx] = dc_re, -dc_im
    small_g.update(norm2_g=red8(dn2g_acc), q_a_g=red8(dqag_acc), q_norm_g=head_fold(dqg_acc), d_skip=red8(ddskip_acc),
                   conv_b=red8(dcb_acc))
    g_convw_full = jnp.stack([red8(dcw0), red8(dcw1), red8(dcw2)])
    late_small = ["c_ctx", "b_mod", "norm1_g", "kv_a_g", "k_norm_g"]
    first_small = ["norm2_g", "q_a_g", "q_norm_g", "lam_re_f", "lam_im_f", "log_dt_f", "c_re_f", "c_im_f", "lam_re_b",
                   "lam_im_b", "log_dt_b", "c_re_b", "c_im_b", "b_re", "b_im", "d_skip", "conv_b"]
    first_pack = _pack([small_g[k] for k in first_small] + [g_convw_full], rows_mult=8)

    (dkvpre, dkr, dkg_acc), (first_all,) = _rowwise(
        khead_bwd_body, name="k_head_bwd", nblk=nb, tr=tr,
        rows=[(kvpre, 0, hw, 0), (z, kr_cb, SLOT, 0), (dk_p, 0, hw, 0), (dv_p, 0, hw, 0)] + rope_rows(),
        fulls=[kg_p], outs=[(2 * hw, MXU_DTYPE), (SLOT, MXU_DTYPE)], accs=[hw], side=([first_pack], ALL8))
    dckvn = _mm(dkvpre, wukv_p, "nt", "kv_up_dx")
    g_wukv_p = _mm(ckvn, dkvpre, "tn", "kv_up_dw")
    dckv, dkvag_acc = _rowwise(lambda v, dy, g: _rms_bwd(v, g, dy, KV_LORA), name="kva_norm_bwd", nblk=nb, tr=tr,
                               rows=[(z, kv_cb, KV_LORA, 0), (dckvn, 0, KV_LORA, 0)], fulls=[kv_a_g],
                               outs=[(KV_LORA, MXU_DTYPE)], accs=[KV_LORA])

    padc = lambda t: jnp.concatenate([t, jnp.zeros((nc, t.shape[1]), t.dtype)], axis=0)
    dz = jnp.concatenate([padc(dgl), du_nat.astype(MXU_DTYPE), dckv, dkr,
                          jnp.zeros((n, q_off - kr_off - SLOT), MXU_DTYPE), padc(dcq)], axis=1)
    gwi = _mm(h1, dz, "tn", "in_proj_dw")
    g_big["w_in"] = jnp.concatenate([gwi[:, q_off:q_off + Q_LORA], gwi[:, kv_off:kv_off + KV_LORA],
                                     gwi[:, kr_off:kr_off + QK_ROPE], gwi[:, u_off:u_off + SSM_WIDTH], gwi[:, :2 * d]], axis=1)
    g_big["w_uq"] = g_wuq_p.reshape(Q_LORA, N_HEADS, SLOT)[:, :, :QK_DIM].reshape(Q_LORA, N_HEADS * QK_DIM)
    gk3 = g_wukv_p[:, :hw].reshape(KV_LORA, N_HEADS, SLOT)[:, :, :QK_NOPE]
    gv3 = g_wukv_p[:, hw:].reshape(KV_LORA, N_HEADS, SLOT)[:, :, :V_DIM]
    g_big["w_ukv"] = jnp.concatenate([gk3, gv3], axis=2).reshape(KV_LORA, N_HEADS * (QK_NOPE + V_DIM))
    early_partials = chip_partials(early_names, "grads_to_sibling")
    dh1 = _mm(dz, win_p, "nt", "in_proj_dx")

    def norm1_bwd_body(xv, dh, dx1v, m, g):
        sc1 = m[:, d:2 * d]
        y = _rms_fwd(xv, g, d)
        dxn, dgc = _rms_bwd(xv, g, dh * (1.0 + sc1), d)
        return dxn + dx1v, dgc, dh, dh * y

    (dxa, dn1g_acc, dsh1_acc, dsc1_acc), early_recv = _rowwise(
        norm1_bwd_body, name="norm1_bwd", nblk=nb, tr=tr, rows=[(xa, 0, d, 0), (dh1, 0, d, 0), (dx1, 0, d, 0)],
        sels=[modv], fulls=[n1g], outs=[(d, F32)], accs=[d, d, d], seg=nlb, side=(scatter_in(early_partials), False))
    grad_x = dxa[:nl][None]

    dmod_own =jnp.concatenate([red8(dsh1_acc[0]), red8(dsc1_acc[0]), red8(dg1_acc), red8(dsh2_acc), red8(dsc2_acc), red8(dg2_acc)])
    dmod_ctx = jnp.concatenate([red8(dsh1_acc[1]), red8(dsc1_acc[1]), jnp.zeros((4 * d,), F32)])
    dm_in = jnp.concatenate([dmod_own[None, :], dmod_ctx[None, :], jnp.zeros((6, d6), F32)], axis=0)
    (dm_all,) = _exchange8([dm_in], "gather_dmod")
    dm_own_sh = lax.dynamic_slice(dm_all[:, 0, :], (0, chip * csh), (8, csh))
    dm_ctx_sh = lax.dynamic_slice(dm_all[:, 1, :], (0, chip * csh), (8, csh))

    def mod_bwd_body(c_ref, own_ref, ctx_ref, w_ref, gw_ref, gb_ref, gc_ref):
        cv = c_ref[...]
        a = _silu(cv).astype(MXU_DTYPE)
        own = own_ref[...]
        ctx_tot = ctx_ref[0:1, :]
        for j in range(1, 8):
            ctx_tot = ctx_tot + ctx_ref[j:j + 1, :]
        g16 = jnp.concatenate([own, jnp.broadcast_to(ctx_tot, own.shape)], axis=0)
        rid = lax.broadcasted_iota(jnp.int32, g16.shape, 0)
        g16 = jnp.where(rid <= 8, g16, 0.0)
        gw_ref[...] = lax.dot_general(a, g16.astype(MXU_DTYPE), (((0,), (0,)), ((), ())), preferred_element_type=F32)
        gb_ref[...] = jnp.broadcast_to(jnp.sum(own, axis=0, keepdims=True) + ctx_tot, gb_ref.shape)
        gc = lax.dot_general(jnp.broadcast_to(ctx_tot, own.shape).astype(MXU_DTYPE), w_ref[...].astype(MXU_DTYPE),
                             (((1,), (1,)), ((), ())), preferred_element_type=F32)
        gc_ref[...] = gc * _dsilu(cv[8:9, :])

    g_wmod, g_bmod_sh, g_cctx_part = pl.pallas_call(
        mod_bwd_body, name="mod_bwd", out_shape=[_sds((d, csh), F32), _sds((8, csh), F32), _sds((8, d), F32)],
        compiler_params=pltpu.CompilerParams(vmem_limit_bytes=VMEM_LIMIT))(cs16, dm_own_sh, dm_ctx_sh, w_mod[0])
    north = (mc == 0).astype(F32)
    g_bmod_part = lax.dynamic_update_slice(jnp.zeros((1, d6), F32), g_bmod_sh[0:1] * north, (0, chip * csh))
    g_cctx_part = g_cctx_part[0] * north

    small_g.update(c_ctx=g_cctx_part, b_mod=g_bmod_part[0], norm1_g=red8(dn1g_acc[0]) + red8(dn1g_acc[1]),
                   kv_a_g=red8(dkvag_acc), k_norm_g=head_fold(dkg_acc))
    (late_all,) = _exchange8([_pack([small_g[k] for k in late_small], rows_mult=8)], "gather_small_grads")
    sg_first = _unpack(_sum8(first_all, "sum_small_grads"), [weights[k].shape for k in first_small] + [(3, f2)])
    sg_late = _unpack(_sum8(late_all, "sum_last_small_grads"), [weights[k].shape for k in late_small])
    g_small = dict(zip(first_small + late_small, sg_first[:-1] + sg_late))
    g_small["conv_w"] = lax.dynamic_slice(sg_first[-1], (0, chip * cwid), (3, cwid))[None]
    small_names = first_small + late_small

    reduced = dict(zip(early_names + ffn_names, chip_sums(early_names, early_partials, early_recv)
                       + chip_sums(ffn_names, ffn_partials, ffn_recv)))
    both = _sibling_exchange([reduced[k_] for k_ in big_names], "exchange_halves")
    g_sh = {k_: b_.reshape((1,) + weights[k_].shape[1:]) for k_, b_ in zip(big_names, both)}
    g_sh["w_mod"] = g_wmod[None]

    grads = {**g_sh, **g_small}
    outs_d, outs_m, outs_v = {}, {}, {}
    for k_ in ["w_mod"] + big_names:
        shp = weights[k_].shape
        res = _adamw(*[t.reshape(shp[1:]) for t in (grads[k_], weights[k_], mom_m[k_], mom_v[k_])], "adamw_" + k_)
        for dst, buf in zip((outs_d, outs_m, outs_v), res):
            dst[k_] = buf.reshape(shp)
    adam_small = small_names + ["conv_w"]
    shapes = [weights[k_].shape for k_ in adam_small]
    res = _adamw(*[_pack([src[k_] for k_ in adam_small], rows_mult=8) for src in (grads, weights, mom_m, mom_v)], "adamw_small")
    for dst, buf in zip((outs_d, outs_m, outs_v), res):
        dst.update(zip(adam_small, _unpack(buf, shapes)))
    grads = {k_: grads[k_].reshape(weights[k_].shape) for k_ in names}
    return (loss, grad_x, *[grads[k_] for k_ in names], *[outs_d[k_] for k_ in names],
            *[outs_m[k_] for k_ in names], *[outs_v[k_] for k_ in names])
```
